```python
import math
import jax, jax.numpy as jnp
from jax import lax
import numpy as np

D_MODEL = 2048
BATCH = 8
SEQ = 4096
DEPTH = 2

MEM_LEN = 256
CONV_WIDTH = 1024
CONV_K = 31
SSM_WIDTH = 1024
SSM_GROUP = 16
SSM_GROUPS = SSM_WIDTH // SSM_GROUP
SSM_STATE = 64
XA_HEADS = 4
XA_HEAD_DIM = D_MODEL // XA_HEADS
D_FF = 5632
FFN_K = 3
EPS = 1e-6
DT_MIN = 1e-3
DT_MAX = 1e-1
IN_COLS = 2 * CONV_WIDTH + SSM_WIDTH + 2 * D_MODEL

kernel_name = "griffin_gated_conformer_s5_hybrid"


def rmsnorm(x, g):
    xf = x.astype(jnp.float32)
    y = xf * lax.rsqrt(jnp.mean(xf * xf, axis=-1, keepdims=True) + EPS)
    return (y * g.astype(jnp.float32)).astype(x.dtype)


def layernorm(x, g, b):
    xf = x.astype(jnp.float32)
    mu = jnp.mean(xf, axis=-1, keepdims=True)
    xc = xf - mu
    y = xc * lax.rsqrt(jnp.mean(xc * xc, axis=-1, keepdims=True) + EPS)
    return (y * g.astype(jnp.float32) + b.astype(jnp.float32)).astype(x.dtype)


def causal_dwconv(x, w):
    k = w.shape[0]
    return lax.conv_general_dilated(
        x, w[:, None, :].astype(x.dtype), window_strides=(1,), padding=[(k - 1, 0)],
        dimension_numbers=("NWC", "WIO", "NWC"), feature_group_count=x.shape[-1])


def conformer_conv_branch(u2, dw_w, dw_b, ln_g, ln_b, w_pw):
    a, b = jnp.split(u2, 2, axis=-1)
    h = a * jax.nn.sigmoid(b)
    h = causal_dwconv(h, dw_w) + dw_b.astype(h.dtype)
    h = layernorm(h, ln_g, ln_b)
    h = jax.nn.silu(h)
    return h @ w_pw


def _cmul_scan(e1, e2):
    a1r, a1i, b1r, b1i = e1
    a2r, a2i, b2r, b2i = e2
    ar = a2r * a1r - a2i * a1i
    ai = a2r * a1i + a2i * a1r
    br = a2r * b1r - a2i * b1i + b2r
    bi = a2r * b1i + a2i * b1r + b2i
    return ar, ai, br, bi


def s5_branch(u, a_re, a_im, log_dt, b_re, b_im, c_re, c_im, d_skip, w_glu):
    bsz, seq, _ = u.shape
    f32 = jnp.float32
    uf = u.astype(f32).reshape(bsz, seq, SSM_GROUPS, SSM_GROUP)
    ar = jnp.minimum(a_re.astype(f32), -1e-4)
    ai = a_im.astype(f32)
    dt = jnp.exp(log_dt.astype(f32))[:, None]
    mag = jnp.exp(dt * ar)
    abar_re = mag * jnp.cos(dt * ai)
    abar_im = mag * jnp.sin(dt * ai)
    den = ar * ar + ai * ai
    nr = abar_re - 1.0
    ni = abar_im
    z_re = (nr * ar + ni * ai) / den
    z_im = (ni * ar - nr * ai) / den
    br = b_re.astype(f32)
    bi = b_im.astype(f32)
    bbar_re = z_re[..., None] * br - z_im[..., None] * bi
    bbar_im = z_re[..., None] * bi + z_im[..., None] * br
    bu_re = jnp.einsum("blgh,gph->blgp", uf, bbar_re)
    bu_im = jnp.einsum("blgh,gph->blgp", uf, bbar_im)
    shape_a = (1, seq, SSM_GROUPS, SSM_STATE)
    a_t_re = jnp.broadcast_to(abar_re[None, None], shape_a)
    a_t_im = jnp.broadcast_to(abar_im[None, None], shape_a)
    _, _, xr, xi = lax.associative_scan(_cmul_scan, (a_t_re, a_t_im, bu_re, bu_im), axis=1)
    y = (jnp.einsum("blgp,ghp->blgh", xr, c_re.astype(f32))
         - jnp.einsum("blgp,ghp->blgh", xi, c_im.astype(f32)))
    y = y.reshape(bsz, seq, SSM_WIDTH) + d_skip.astype(f32) * uf.reshape(bsz, seq, SSM_WIDTH)
    y = jax.nn.gelu(y).astype(u.dtype)
    g = y @ w_glu
    ga, gb = jnp.split(g, 2, axis=-1)
    return ga * jax.nn.sigmoid(gb)


def cross_attention(h, m, w_q, w_kv, w_o):
    bsz, seq, _ = h.shape
    q = (h @ w_q).reshape(bsz, seq, XA_HEADS, XA_HEAD_DIM)
    k, v = jnp.split(m @ w_kv, 2, axis=-1)
    k = k.reshape(bsz, MEM_LEN, XA_HEADS, XA_HEAD_DIM)
    v = v.reshape(bsz, MEM_LEN, XA_HEADS, XA_HEAD_DIM)
    s = jnp.einsum("blhd,bmhd->bhlm", q, k).astype(jnp.float32) * (XA_HEAD_DIM ** -0.5)
    p = jax.nn.softmax(s, axis=-1).astype(v.dtype)
    o = jnp.einsum("bhlm,bmhd->blhd", p, v).reshape(bsz, seq, D_MODEL)
    return o @ w_o


def conv_ffn(h, w_up, dw_w, w_down):
    up = causal_dwconv(h @ w_up, dw_w)
    gate, val = jnp.split(up, 2, axis=-1)
    return (jax.nn.silu(gate) * val) @ w_down


def _fwd_setup_inputs(seed: int = 0) -> dict:
    key = jax.random.key(seed)
    ks = iter(jax.random.split(key, 40))

    def nrm(shape, scale):
        return jax.random.normal(next(ks), shape, jnp.float32) * scale

    def gain(shape):
        return 1.0 + nrm(shape, 0.02)

    L = DEPTH
    n = jnp.arange(SSM_STATE, dtype=jnp.float32)
    a_re = -0.5 + nrm((L, SSM_GROUPS, SSM_STATE), 0.01)
    a_im = math.pi * n[None, None, :] + nrm((L, SSM_GROUPS, SSM_STATE), 0.01)
    log_dt = jax.random.uniform(next(ks), (L, SSM_GROUPS), jnp.float32,
                                math.log(DT_MIN), math.log(DT_MAX))
    return {
        "x": nrm((BATCH, SEQ, D_MODEL), 1.0),
        "mem": nrm((BATCH, MEM_LEN, D_MODEL), 1.0),
        "mix_norm_g": gain((L, D_MODEL)),
        "w_in": nrm((L, D_MODEL, IN_COLS), D_MODEL ** -0.5),
        "conv_dw_w": nrm((L, CONV_K, CONV_WIDTH), CONV_K ** -0.5),
        "conv_dw_b": nrm((L, CONV_WIDTH), 0.02),
        "conv_ln_g": gain((L, CONV_WIDTH)),
        "conv_ln_b": nrm((L, CONV_WIDTH), 0.02),
        "conv_w_pw": nrm((L, CONV_WIDTH, D_MODEL), CONV_WIDTH ** -0.5),
        "ssm_a_re": a_re,
        "ssm_a_im": a_im,
        "ssm_log_dt": log_dt,
        "ssm_b_re": nrm((L, SSM_GROUPS, SSM_STATE, SSM_GROUP), (2 * SSM_GROUP) ** -0.5),
        "ssm_b_im": nrm((L, SSM_GROUPS, SSM_STATE, SSM_GROUP), (2 * SSM_GROUP) ** -0.5),
        "ssm_c_re": nrm((L, SSM_GROUPS, SSM_GROUP, SSM_STATE), SSM_STATE ** -0.5),
        "ssm_c_im": nrm((L, SSM_GROUPS, SSM_GROUP, SSM_STATE), SSM_STATE ** -0.5),
        "ssm_d": nrm((L, SSM_WIDTH), 1.0),
        "ssm_w_glu": nrm((L, SSM_WIDTH, 2 * D_MODEL), SSM_WIDTH ** -0.5),
        "w_out": nrm((L, D_MODEL, D_MODEL), D_MODEL ** -0.5),
        "xa_norm_g": gain((L, D_MODEL)),
        "mem_norm_g": gain((L, D_MODEL)),
        "xa_w_q": nrm((L, D_MODEL, D_MODEL), D_MODEL ** -0.5),
        "xa_w_kv": nrm((L, D_MODEL, 2 * D_MODEL), D_MODEL ** -0.5),
        "xa_w_o": nrm((L, D_MODEL, D_MODEL), D_MODEL ** -0.5),
        "ffn_norm_g": gain((L, D_MODEL)),
        "ffn_w_up": nrm((L, D_MODEL, 2 * D_FF), D_MODEL ** -0.5),
        "ffn_dw_w": nrm((L, FFN_K, 2 * D_FF), FFN_K ** -0.5),
        "ffn_w_down": nrm((L, D_FF, D_MODEL), D_FF ** -0.5),
        "final_norm_g": gain((D_MODEL,)),
    }


def _fwd_reference(x, mem, mix_norm_g, w_in, conv_dw_w, conv_dw_b, conv_ln_g, conv_ln_b, conv_w_pw,
              ssm_a_re, ssm_a_im, ssm_log_dt, ssm_b_re, ssm_b_im, ssm_c_re, ssm_c_im, ssm_d,
              ssm_w_glu, w_out, xa_norm_g, mem_norm_g, xa_w_q, xa_w_kv, xa_w_o,
              ffn_norm_g, ffn_w_up, ffn_dw_w, ffn_w_down, final_norm_g):
    split_pts = [2 * CONV_WIDTH, 2 * CONV_WIDTH + SSM_WIDTH]
    for i in range(DEPTH):
        h = rmsnorm(x, mix_norm_g[i])
        proj = h @ w_in[i]
        u_conv, u_ssm, gate_logits = jnp.split(proj, split_pts, axis=-1)
        y_a = conformer_conv_branch(u_conv, conv_dw_w[i], conv_dw_b[i], conv_ln_g[i],
                                    conv_ln_b[i], conv_w_pw[i])
        y_b = s5_branch(u_ssm, ssm_a_re[i], ssm_a_im[i], ssm_log_dt[i], ssm_b_re[i],
                        ssm_b_im[i], ssm_c_re[i], ssm_c_im[i], ssm_d[i], ssm_w_glu[i])
        g_a, g_b = jnp.split(jax.nn.sigmoid(gate_logits), 2, axis=-1)
        x = x + (g_a * y_a + g_b * y_b) @ w_out[i]
        h = rmsnorm(x, xa_norm_g[i])
        m = rmsnorm(mem, mem_norm_g[i])
        x = x + cross_attention(h, m, xa_w_q[i], xa_w_kv[i], xa_w_o[i])
        h = rmsnorm(x, ffn_norm_g[i])
        x = x + conv_ffn(h, ffn_w_up[i], ffn_dw_w[i], ffn_w_down[i])
    return rmsnorm(x, final_norm_g)


import jax as _jax
import jax.numpy as _jnp

TWIN_FORMAT = 'train_step'
FWD_PARAMS = ['x', 'mem', 'mix_norm_g', 'w_in', 'conv_dw_w', 'conv_dw_b', 'conv_ln_g', 'conv_ln_b', 'conv_w_pw', 'ssm_a_re', 'ssm_a_im', 'ssm_log_dt', 'ssm_b_re', 'ssm_b_im', 'ssm_c_re', 'ssm_c_im', 'ssm_d', 'ssm_w_glu', 'w_out', 'xa_norm_g', 'mem_norm_g', 'xa_w_q', 'xa_w_kv', 'xa_w_o', 'ffn_norm_g', 'ffn_w_up', 'ffn_dw_w', 'ffn_w_down', 'final_norm_g']
TWIN_WEIGHTS = ['mix_norm_g', 'w_in', 'conv_dw_w', 'conv_dw_b', 'conv_ln_g', 'conv_ln_b', 'conv_w_pw', 'ssm_a_re', 'ssm_a_im', 'ssm_log_dt', 'ssm_b_re', 'ssm_b_im', 'ssm_c_re', 'ssm_c_im', 'ssm_d', 'ssm_w_glu', 'w_out', 'xa_norm_g', 'mem_norm_g', 'xa_w_q', 'xa_w_kv', 'xa_w_o', 'ffn_norm_g', 'ffn_w_up', 'ffn_dw_w', 'ffn_w_down', 'final_norm_g']
TWIN_DIFF_INPUT = 'x'
TWIN_INPUTS = ['x', 'mem', 'mix_norm_g', 'w_in', 'conv_dw_w', 'conv_dw_b', 'conv_ln_g', 'conv_ln_b', 'conv_w_pw', 'ssm_a_re', 'ssm_a_im', 'ssm_log_dt', 'ssm_b_re', 'ssm_b_im', 'ssm_c_re', 'ssm_c_im', 'ssm_d', 'ssm_w_glu', 'w_out', 'xa_norm_g', 'mem_norm_g', 'xa_w_q', 'xa_w_kv', 'xa_w_o', 'ffn_norm_g', 'ffn_w_up', 'ffn_dw_w', 'ffn_w_down', 'final_norm_g', 'loss_target', 'm_mix_norm_g', 'm_w_in', 'm_conv_dw_w', 'm_conv_dw_b', 'm_conv_ln_g', 'm_conv_ln_b', 'm_conv_w_pw', 'm_ssm_a_re', 'm_ssm_a_im', 'm_ssm_log_dt', 'm_ssm_b_re', 'm_ssm_b_im', 'm_ssm_c_re', 'm_ssm_c_im', 'm_ssm_d', 'm_ssm_w_glu', 'm_w_out', 'm_xa_norm_g', 'm_mem_norm_g', 'm_xa_w_q', 'm_xa_w_kv', 'm_xa_w_o', 'm_ffn_norm_g', 'm_ffn_w_up', 'm_ffn_dw_w', 'm_ffn_w_down', 'm_final_norm_g', 'v_mix_norm_g', 'v_w_in', 'v_conv_dw_w', 'v_conv_dw_b', 'v_conv_ln_g', 'v_conv_ln_b', 'v_conv_w_pw', 'v_ssm_a_re', 'v_ssm_a_im', 'v_ssm_log_dt', 'v_ssm_b_re', 'v_ssm_b_im', 'v_ssm_c_re', 'v_ssm_c_im', 'v_ssm_d', 'v_ssm_w_glu', 'v_w_out', 'v_xa_norm_g', 'v_mem_norm_g', 'v_xa_w_q', 'v_xa_w_kv', 'v_xa_w_o', 'v_ffn_norm_g', 'v_ffn_w_up', 'v_ffn_dw_w', 'v_ffn_w_down', 'v_final_norm_g']
TWIN_OUTPUTS = ['loss', 'grad_x', 'grad_mix_norm_g', 'grad_w_in', 'grad_conv_dw_w', 'grad_conv_dw_b', 'grad_conv_ln_g', 'grad_conv_ln_b', 'grad_conv_w_pw', 'grad_ssm_a_re', 'grad_ssm_a_im', 'grad_ssm_log_dt', 'grad_ssm_b_re', 'grad_ssm_b_im', 'grad_ssm_c_re', 'grad_ssm_c_im', 'grad_ssm_d', 'grad_ssm_w_glu', 'grad_w_out', 'grad_xa_norm_g', 'grad_mem_norm_g', 'grad_xa_w_q', 'grad_xa_w_kv', 'grad_xa_w_o', 'grad_ffn_norm_g', 'grad_ffn_w_up', 'grad_ffn_dw_w', 'grad_ffn_w_down', 'grad_final_norm_g', 'delta_mix_norm_g', 'delta_w_in', 'delta_conv_dw_w', 'delta_conv_dw_b', 'delta_conv_ln_g', 'delta_conv_ln_b', 'delta_conv_w_pw', 'delta_ssm_a_re', 'delta_ssm_a_im', 'delta_ssm_log_dt', 'delta_ssm_b_re', 'delta_ssm_b_im', 'delta_ssm_c_re', 'delta_ssm_c_im', 'delta_ssm_d', 'delta_ssm_w_glu', 'delta_w_out', 'delta_xa_norm_g', 'delta_mem_norm_g', 'delta_xa_w_q', 'delta_xa_w_kv', 'delta_xa_w_o', 'delta_ffn_norm_g', 'delta_ffn_w_up', 'delta_ffn_dw_w', 'delta_ffn_w_down', 'delta_final_norm_g', 'new_m_mix_norm_g', 'new_m_w_in', 'new_m_conv_dw_w', 'new_m_conv_dw_b', 'new_m_conv_ln_g', 'new_m_conv_ln_b', 'new_m_conv_w_pw', 'new_m_ssm_a_re', 'new_m_ssm_a_im', 'new_m_ssm_log_dt', 'new_m_ssm_b_re', 'new_m_ssm_b_im', 'new_m_ssm_c_re', 'new_m_ssm_c_im', 'new_m_ssm_d', 'new_m_ssm_w_glu', 'new_m_w_out', 'new_m_xa_norm_g', 'new_m_mem_norm_g', 'new_m_xa_w_q', 'new_m_xa_w_kv', 'new_m_xa_w_o', 'new_m_ffn_norm_g', 'new_m_ffn_w_up', 'new_m_ffn_dw_w', 'new_m_ffn_w_down', 'new_m_final_norm_g', 'new_v_mix_norm_g', 'new_v_w_in', 'new_v_conv_dw_w', 'new_v_conv_dw_b', 'new_v_conv_ln_g', 'new_v_conv_ln_b', 'new_v_conv_w_pw', 'new_v_ssm_a_re', 'new_v_ssm_a_im', 'new_v_ssm_log_dt', 'new_v_ssm_b_re', 'new_v_ssm_b_im', 'new_v_ssm_c_re', 'new_v_ssm_c_im', 'new_v_ssm_d', 'new_v_ssm_w_glu', 'new_v_w_out', 'new_v_xa_norm_g', 'new_v_mem_norm_g', 'new_v_xa_w_q', 'new_v_xa_w_kv', 'new_v_xa_w_o', 'new_v_ffn_norm_g', 'new_v_ffn_w_up', 'new_v_ffn_dw_w', 'new_v_ffn_w_down', 'new_v_final_norm_g']
TWIN_LEAF_KINDS = {'loss': 'loss', 'grad_x': 'grad_x', 'grad_mix_norm_g': 'grad_w', 'grad_w_in': 'grad_w', 'grad_conv_dw_w': 'grad_w', 'grad_conv_dw_b': 'grad_w', 'grad_conv_ln_g': 'grad_w', 'grad_conv_ln_b': 'grad_w', 'grad_conv_w_pw': 'grad_w', 'grad_ssm_a_re': 'grad_w', 'grad_ssm_a_im': 'grad_w', 'grad_ssm_log_dt': 'grad_w', 'grad_ssm_b_re': 'grad_w', 'grad_ssm_b_im': 'grad_w', 'grad_ssm_c_re': 'grad_w', 'grad_ssm_c_im': 'grad_w', 'grad_ssm_d': 'grad_w', 'grad_ssm_w_glu': 'grad_w', 'grad_w_out': 'grad_w', 'grad_xa_norm_g': 'grad_w', 'grad_mem_norm_g': 'grad_w', 'grad_xa_w_q': 'grad_w', 'grad_xa_w_kv': 'grad_w', 'grad_xa_w_o': 'grad_w', 'grad_ffn_norm_g': 'grad_w', 'grad_ffn_w_up': 'grad_w', 'grad_ffn_dw_w': 'grad_w', 'grad_ffn_w_down': 'grad_w', 'grad_final_norm_g': 'grad_w', 'delta_mix_norm_g': 'delta_w', 'delta_w_in': 'delta_w', 'delta_conv_dw_w': 'delta_w', 'delta_conv_dw_b': 'delta_w', 'delta_conv_ln_g': 'delta_w', 'delta_conv_ln_b': 'delta_w', 'delta_conv_w_pw': 'delta_w', 'delta_ssm_a_re': 'delta_w', 'delta_ssm_a_im': 'delta_w', 'delta_ssm_log_dt': 'delta_w', 'delta_ssm_b_re': 'delta_w', 'delta_ssm_b_im': 'delta_w', 'delta_ssm_c_re': 'delta_w', 'delta_ssm_c_im': 'delta_w', 'delta_ssm_d': 'delta_w', 'delta_ssm_w_glu': 'delta_w', 'delta_w_out': 'delta_w', 'delta_xa_norm_g': 'delta_w', 'delta_mem_norm_g': 'delta_w', 'delta_xa_w_q': 'delta_w', 'delta_xa_w_kv': 'delta_w', 'delta_xa_w_o': 'delta_w', 'delta_ffn_norm_g': 'delta_w', 'delta_ffn_w_up': 'delta_w', 'delta_ffn_dw_w': 'delta_w', 'delta_ffn_w_down': 'delta_w', 'delta_final_norm_g': 'delta_w', 'new_m_mix_norm_g': 'new_m', 'new_m_w_in': 'new_m', 'new_m_conv_dw_w': 'new_m', 'new_m_conv_dw_b': 'new_m', 'new_m_conv_ln_g': 'new_m', 'new_m_conv_ln_b': 'new_m', 'new_m_conv_w_pw': 'new_m', 'new_m_ssm_a_re': 'new_m', 'new_m_ssm_a_im': 'new_m', 'new_m_ssm_log_dt': 'new_m', 'new_m_ssm_b_re': 'new_m', 'new_m_ssm_b_im': 'new_m', 'new_m_ssm_c_re': 'new_m', 'new_m_ssm_c_im': 'new_m', 'new_m_ssm_d': 'new_m', 'new_m_ssm_w_glu': 'new_m', 'new_m_w_out': 'new_m', 'new_m_xa_norm_g': 'new_m', 'new_m_mem_norm_g': 'new_m', 'new_m_xa_w_q': 'new_m', 'new_m_xa_w_kv': 'new_m', 'new_m_xa_w_o': 'new_m', 'new_m_ffn_norm_g': 'new_m', 'new_m_ffn_w_up': 'new_m', 'new_m_ffn_dw_w': 'new_m', 'new_m_ffn_w_down': 'new_m', 'new_m_final_norm_g': 'new_m', 'new_v_mix_norm_g': 'new_v', 'new_v_w_in': 'new_v', 'new_v_conv_dw_w': 'new_v', 'new_v_conv_dw_b': 'new_v', 'new_v_conv_ln_g': 'new_v', 'new_v_conv_ln_b': 'new_v', 'new_v_conv_w_pw': 'new_v', 'new_v_ssm_a_re': 'new_v', 'new_v_ssm_a_im': 'new_v', 'new_v_ssm_log_dt': 'new_v', 'new_v_ssm_b_re': 'new_v', 'new_v_ssm_b_im': 'new_v', 'new_v_ssm_c_re': 'new_v', 'new_v_ssm_c_im': 'new_v', 'new_v_ssm_d': 'new_v', 'new_v_ssm_w_glu': 'new_v', 'new_v_w_out': 'new_v', 'new_v_xa_norm_g': 'new_v', 'new_v_mem_norm_g': 'new_v', 'new_v_xa_w_q': 'new_v', 'new_v_xa_w_kv': 'new_v', 'new_v_xa_w_o': 'new_v', 'new_v_ffn_norm_g': 'new_v', 'new_v_ffn_w_up': 'new_v', 'new_v_ffn_dw_w': 'new_v', 'new_v_ffn_w_down': 'new_v', 'new_v_final_norm_g': 'new_v'}


def _forward(args):
    return _fwd_reference(*[args[k] for k in FWD_PARAMS])


def _output_shape():
    def fwd():
        inp = _fwd_setup_inputs(0)
        return _fwd_reference(*[inp[k] for k in FWD_PARAMS])
    out = _jax.eval_shape(fwd)
    return out.shape, out.dtype

N_MICROBATCH = 1
ADAM_LR = 0.001
ADAM_B1 = 0.9
ADAM_B2 = 0.999
ADAM_EPS = 1e-08
ADAM_WD = 0.01
ADAM_STEP = 10
PER_EXAMPLE_BATCH_AXIS = {'x': 0, 'mem': 0, 'loss_target': 0}
SHARED_INPUTS = []
_WEIGHT_DTYPES = {'mix_norm_g': _jnp.float32, 'w_in': _jnp.float32, 'conv_dw_w': _jnp.float32, 'conv_dw_b': _jnp.float32, 'conv_ln_g': _jnp.float32, 'conv_ln_b': _jnp.float32, 'conv_w_pw': _jnp.float32, 'ssm_a_re': _jnp.float32, 'ssm_a_im': _jnp.float32, 'ssm_log_dt': _jnp.float32, 'ssm_b_re': _jnp.float32, 'ssm_b_im': _jnp.float32, 'ssm_c_re': _jnp.float32, 'ssm_c_im': _jnp.float32, 'ssm_d': _jnp.float32, 'ssm_w_glu': _jnp.float32, 'w_out': _jnp.float32, 'xa_norm_g': _jnp.float32, 'mem_norm_g': _jnp.float32, 'xa_w_q': _jnp.float32, 'xa_w_kv': _jnp.float32, 'xa_w_o': _jnp.float32, 'ffn_norm_g': _jnp.float32, 'ffn_w_up': _jnp.float32, 'ffn_dw_w': _jnp.float32, 'ffn_w_down': _jnp.float32, 'final_norm_g': _jnp.float32}
MOMENT_SCALE = {'mix_norm_g': 4.114344e-02, 'w_in': 2.122492e-02, 'conv_dw_w': 4.307175e-02, 'conv_dw_b': 9.598670e-02, 'conv_ln_g': 5.151655e-02, 'conv_ln_b': 4.550424e-02, 'conv_w_pw': 2.980432e-02, 'ssm_a_re': 1.770877e-03, 'ssm_a_im': 1.813113e-03, 'ssm_log_dt': 1.637182e+00, 'ssm_b_re': 1.195722e-03, 'ssm_b_im': 1.210619e-03, 'ssm_c_re': 1.710376e-03, 'ssm_c_im': 1.705299e-03, 'ssm_d': 2.846519e-02, 'ssm_w_glu': 1.289647e-02, 'w_out': 3.469320e-02, 'xa_norm_g': 9.641226e-03, 'mem_norm_g': 1.354688e-02, 'xa_w_q': 9.232233e-03, 'xa_w_kv': 9.378052e-03, 'xa_w_o': 9.494619e-03, 'ffn_norm_g': 6.184442e-02, 'ffn_w_up': 2.682961e-02, 'ffn_dw_w': 2.651327e-02, 'ffn_w_down': 4.376245e-02, 'final_norm_g': 1.598256e+01}


def _to_microbatches(a, axis):
    t = _jnp.moveaxis(a, axis, 0)
    t = t.reshape((N_MICROBATCH, t.shape[0] // N_MICROBATCH) + t.shape[1:])
    return _jnp.moveaxis(t, 1, axis + 1)


def setup_inputs(seed: int = 0) -> dict:
    inp = _fwd_setup_inputs(seed)
    key = _jax.random.fold_in(_jax.random.key(seed), 7919)
    shape, _ = _output_shape()
    out = dict(inp)
    out["loss_target"] = _jax.random.normal(_jax.random.fold_in(key, 0), shape, _jnp.float32)
    for i, name in enumerate(TWIN_WEIGHTS):
        w = inp[name].astype(_jnp.float32)
        if MOMENT_SCALE is None:
            s = _jnp.sqrt(_jnp.mean(_jnp.square(w)) + 1e-30)
        else:
            s = MOMENT_SCALE[name]
        km, kv = _jax.random.split(_jax.random.fold_in(key, i + 1))
        out[name] = w
        out["m_" + name] = s * _jax.random.normal(km, w.shape, _jnp.float32)
        out["v_" + name] = (s * s) * _jax.random.uniform(kv, w.shape, _jnp.float32, 0.5, 1.5)
    if N_MICROBATCH > 1:
        for name, axis in PER_EXAMPLE_BATCH_AXIS.items():
            out[name] = _to_microbatches(out[name], axis)
    return {'x': out['x'], 'mem': out['mem'], 'mix_norm_g': out['mix_norm_g'], 'w_in': out['w_in'], 'conv_dw_w': out['conv_dw_w'], 'conv_dw_b': out['conv_dw_b'], 'conv_ln_g': out['conv_ln_g'], 'conv_ln_b': out['conv_ln_b'], 'conv_w_pw': out['conv_w_pw'], 'ssm_a_re': out['ssm_a_re'], 'ssm_a_im': out['ssm_a_im'], 'ssm_log_dt': out['ssm_log_dt'], 'ssm_b_re': out['ssm_b_re'], 'ssm_b_im': out['ssm_b_im'], 'ssm_c_re': out['ssm_c_re'], 'ssm_c_im': out['ssm_c_im'], 'ssm_d': out['ssm_d'], 'ssm_w_glu': out['ssm_w_glu'], 'w_out': out['w_out'], 'xa_norm_g': out['xa_norm_g'], 'mem_norm_g': out['mem_norm_g'], 'xa_w_q': out['xa_w_q'], 'xa_w_kv': out['xa_w_kv'], 'xa_w_o': out['xa_w_o'], 'ffn_norm_g': out['ffn_norm_g'], 'ffn_w_up': out['ffn_w_up'], 'ffn_dw_w': out['ffn_dw_w'], 'ffn_w_down': out['ffn_w_down'], 'final_norm_g': out['final_norm_g'], 'loss_target': out['loss_target'], 'm_mix_norm_g': out['m_mix_norm_g'], 'm_w_in': out['m_w_in'], 'm_conv_dw_w': out['m_conv_dw_w'], 'm_conv_dw_b': out['m_conv_dw_b'], 'm_conv_ln_g': out['m_conv_ln_g'], 'm_conv_ln_b': out['m_conv_ln_b'], 'm_conv_w_pw': out['m_conv_w_pw'], 'm_ssm_a_re': out['m_ssm_a_re'], 'm_ssm_a_im': out['m_ssm_a_im'], 'm_ssm_log_dt': out['m_ssm_log_dt'], 'm_ssm_b_re': out['m_ssm_b_re'], 'm_ssm_b_im': out['m_ssm_b_im'], 'm_ssm_c_re': out['m_ssm_c_re'], 'm_ssm_c_im': out['m_ssm_c_im'], 'm_ssm_d': out['m_ssm_d'], 'm_ssm_w_glu': out['m_ssm_w_glu'], 'm_w_out': out['m_w_out'], 'm_xa_norm_g': out['m_xa_norm_g'], 'm_mem_norm_g': out['m_mem_norm_g'], 'm_xa_w_q': out['m_xa_w_q'], 'm_xa_w_kv': out['m_xa_w_kv'], 'm_xa_w_o': out['m_xa_w_o'], 'm_ffn_norm_g': out['m_ffn_norm_g'], 'm_ffn_w_up': out['m_ffn_w_up'], 'm_ffn_dw_w': out['m_ffn_dw_w'], 'm_ffn_w_down': out['m_ffn_w_down'], 'm_final_norm_g': out['m_final_norm_g'], 'v_mix_norm_g': out['v_mix_norm_g'], 'v_w_in': out['v_w_in'], 'v_conv_dw_w': out['v_conv_dw_w'], 'v_conv_dw_b': out['v_conv_dw_b'], 'v_conv_ln_g': out['v_conv_ln_g'], 'v_conv_ln_b': out['v_conv_ln_b'], 'v_conv_w_pw': out['v_conv_w_pw'], 'v_ssm_a_re': out['v_ssm_a_re'], 'v_ssm_a_im': out['v_ssm_a_im'], 'v_ssm_log_dt': out['v_ssm_log_dt'], 'v_ssm_b_re': out['v_ssm_b_re'], 'v_ssm_b_im': out['v_ssm_b_im'], 'v_ssm_c_re': out['v_ssm_c_re'], 'v_ssm_c_im': out['v_ssm_c_im'], 'v_ssm_d': out['v_ssm_d'], 'v_ssm_w_glu': out['v_ssm_w_glu'], 'v_w_out': out['v_w_out'], 'v_xa_norm_g': out['v_xa_norm_g'], 'v_mem_norm_g': out['v_mem_norm_g'], 'v_xa_w_q': out['v_xa_w_q'], 'v_xa_w_kv': out['v_xa_w_kv'], 'v_xa_w_o': out['v_xa_w_o'], 'v_ffn_norm_g': out['v_ffn_norm_g'], 'v_ffn_w_up': out['v_ffn_w_up'], 'v_ffn_dw_w': out['v_ffn_dw_w'], 'v_ffn_w_down': out['v_ffn_w_down'], 'v_final_norm_g': out['v_final_norm_g']}


def _loss(weights, diff, rest, loss_target):
    with _jax.named_scope("forward"):
        args = {**rest, TWIN_DIFF_INPUT: diff, **{k: w.astype(_WEIGHT_DTYPES[k]) for k, w in weights.items()}}
        y = _forward(args)
    with _jax.named_scope("loss_head"):
        err = _jnp.square(y.astype(_jnp.float32) - loss_target)
        return 0.5 * _jnp.sum(_jnp.mean(err, axis=-1)) if err.ndim else 0.5 * err


def _adamw(w, g, m, v):
    m = ADAM_B1 * m + (1.0 - ADAM_B1) * g
    v = ADAM_B2 * v + (1.0 - ADAM_B2) * _jnp.square(g)
    m_hat = m / (1.0 - ADAM_B1 ** ADAM_STEP)
    v_hat = v / (1.0 - ADAM_B2 ** ADAM_STEP)
    delta = -ADAM_LR * (m_hat / (_jnp.sqrt(v_hat) + ADAM_EPS) + ADAM_WD * w)
    return delta, m, v


def reference(x, mem, mix_norm_g, w_in, conv_dw_w, conv_dw_b, conv_ln_g, conv_ln_b, conv_w_pw, ssm_a_re, ssm_a_im, ssm_log_dt, ssm_b_re, ssm_b_im, ssm_c_re, ssm_c_im, ssm_d, ssm_w_glu, w_out, xa_norm_g, mem_norm_g, xa_w_q, xa_w_kv, xa_w_o, ffn_norm_g, ffn_w_up, ffn_dw_w, ffn_w_down, final_norm_g, loss_target, m_mix_norm_g, m_w_in, m_conv_dw_w, m_conv_dw_b, m_conv_ln_g, m_conv_ln_b, m_conv_w_pw, m_ssm_a_re, m_ssm_a_im, m_ssm_log_dt, m_ssm_b_re, m_ssm_b_im, m_ssm_c_re, m_ssm_c_im, m_ssm_d, m_ssm_w_glu, m_w_out, m_xa_norm_g, m_mem_norm_g, m_xa_w_q, m_xa_w_kv, m_xa_w_o, m_ffn_norm_g, m_ffn_w_up, m_ffn_dw_w, m_ffn_w_down, m_final_norm_g, v_mix_norm_g, v_w_in, v_conv_dw_w, v_conv_dw_b, v_conv_ln_g, v_conv_ln_b, v_conv_w_pw, v_ssm_a_re, v_ssm_a_im, v_ssm_log_dt, v_ssm_b_re, v_ssm_b_im, v_ssm_c_re, v_ssm_c_im, v_ssm_d, v_ssm_w_glu, v_w_out, v_xa_norm_g, v_mem_norm_g, v_xa_w_q, v_xa_w_kv, v_xa_w_o, v_ffn_norm_g, v_ffn_w_up, v_ffn_dw_w, v_ffn_w_down, v_final_norm_g):
    given = dict(x=x, mem=mem, mix_norm_g=mix_norm_g, w_in=w_in, conv_dw_w=conv_dw_w, conv_dw_b=conv_dw_b, conv_ln_g=conv_ln_g, conv_ln_b=conv_ln_b, conv_w_pw=conv_w_pw, ssm_a_re=ssm_a_re, ssm_a_im=ssm_a_im, ssm_log_dt=ssm_log_dt, ssm_b_re=ssm_b_re, ssm_b_im=ssm_b_im, ssm_c_re=ssm_c_re, ssm_c_im=ssm_c_im, ssm_d=ssm_d, ssm_w_glu=ssm_w_glu, w_out=w_out, xa_norm_g=xa_norm_g, mem_norm_g=mem_norm_g, xa_w_q=xa_w_q, xa_w_kv=xa_w_kv, xa_w_o=xa_w_o, ffn_norm_g=ffn_norm_g, ffn_w_up=ffn_w_up, ffn_dw_w=ffn_dw_w, ffn_w_down=ffn_w_down, final_norm_g=final_norm_g, loss_target=loss_target, m_mix_norm_g=m_mix_norm_g, m_w_in=m_w_in, m_conv_dw_w=m_conv_dw_w, m_conv_dw_b=m_conv_dw_b, m_conv_ln_g=m_conv_ln_g, m_conv_ln_b=m_conv_ln_b, m_conv_w_pw=m_conv_w_pw, m_ssm_a_re=m_ssm_a_re, m_ssm_a_im=m_ssm_a_im, m_ssm_log_dt=m_ssm_log_dt, m_ssm_b_re=m_ssm_b_re, m_ssm_b_im=m_ssm_b_im, m_ssm_c_re=m_ssm_c_re, m_ssm_c_im=m_ssm_c_im, m_ssm_d=m_ssm_d, m_ssm_w_glu=m_ssm_w_glu, m_w_out=m_w_out, m_xa_norm_g=m_xa_norm_g, m_mem_norm_g=m_mem_norm_g, m_xa_w_q=m_xa_w_q, m_xa_w_kv=m_xa_w_kv, m_xa_w_o=m_xa_w_o, m_ffn_norm_g=m_ffn_norm_g, m_ffn_w_up=m_ffn_w_up, m_ffn_dw_w=m_ffn_dw_w, m_ffn_w_down=m_ffn_w_down, m_final_norm_g=m_final_norm_g, v_mix_norm_g=v_mix_norm_g, v_w_in=v_w_in, v_conv_dw_w=v_conv_dw_w, v_conv_dw_b=v_conv_dw_b, v_conv_ln_g=v_conv_ln_g, v_conv_ln_b=v_conv_ln_b, v_conv_w_pw=v_conv_w_pw, v_ssm_a_re=v_ssm_a_re, v_ssm_a_im=v_ssm_a_im, v_ssm_log_dt=v_ssm_log_dt, v_ssm_b_re=v_ssm_b_re, v_ssm_b_im=v_ssm_b_im, v_ssm_c_re=v_ssm_c_re, v_ssm_c_im=v_ssm_c_im, v_ssm_d=v_ssm_d, v_ssm_w_glu=v_ssm_w_glu, v_w_out=v_w_out, v_xa_norm_g=v_xa_norm_g, v_mem_norm_g=v_mem_norm_g, v_xa_w_q=v_xa_w_q, v_xa_w_kv=v_xa_w_kv, v_xa_w_o=v_xa_w_o, v_ffn_norm_g=v_ffn_norm_g, v_ffn_w_up=v_ffn_w_up, v_ffn_dw_w=v_ffn_dw_w, v_ffn_w_down=v_ffn_w_down, v_final_norm_g=v_final_norm_g)
    weights = {n: given[n] for n in TWIN_WEIGHTS}
    shared = {n: given[n] for n in SHARED_INPUTS}
    per_example = {n: given[n] for n in ['x', 'mem']}
    grad_fn = _jax.value_and_grad(_loss, argnums=(0, 1))

    def one_microbatch(ex, loss_target):
        ex = dict(ex)
        diff = ex.pop(TWIN_DIFF_INPUT)
        return grad_fn(weights, diff, {**shared, **ex}, loss_target)

    if N_MICROBATCH == 1:
        loss, (grad_w, grad_x) = one_microbatch(per_example, given["loss_target"])
    else:
        def body(carry, xs):
            loss_sum, grad_sum = carry
            l_k, (gw_k, gx_k) = one_microbatch(xs[0], xs[1])
            with _jax.named_scope("update"):
                return (loss_sum + l_k, _jax.tree.map(_jnp.add, grad_sum, gw_k)), gx_k

        init = (_jnp.zeros((), _jnp.float32), _jax.tree.map(_jnp.zeros_like, weights))
        (loss, grad_w), grad_x = _jax.lax.scan(body, init, (per_example, given["loss_target"]))
    with _jax.named_scope("update"):
        delta_w, new_m, new_v = {}, {}, {}
        for n in TWIN_WEIGHTS:
            delta_w[n], new_m[n], new_v[n] = _adamw(weights[n], grad_w[n], given["m_" + n], given["v_" + n])
    return (loss, grad_x, *[grad_w[n] for n in TWIN_WEIGHTS], *[delta_w[n] for n in TWIN_WEIGHTS],
            *[new_m[n] for n in TWIN_WEIGHTS], *[new_v[n] for n in TWIN_WEIGHTS])
```

```python
import functools

import jax
import jax.numpy as jnp
from jax import lax
from jax.experimental import pallas as pl
from jax.experimental.pallas import tpu as pltpu

F32 = jnp.float32
BF16 = jnp.bfloat16
MESH_ID = pl.DeviceIdType.MESH
N_DEV = 8
EPS = 1e-6
VMEM_LIMIT = 48 * 1024 * 1024
ANY = pl.BlockSpec(memory_space=pl.ANY)

ADAM_LR = 0.001
ADAM_B1 = 0.9
ADAM_B2 = 0.999
ADAM_EPS = 1e-08
ADAM_WD = 0.01
ADAM_STEP = 10

CONV_K = 31
FFN_K = 3
XA_HEADS = 4
SSM_GROUP = 16
SSM_STATE = 64
HALO = 32
LANE = 128
SSM_LANES = 512


def _pick(n, prefs):
    for p in prefs:
        if p <= n and n % p == 0:
            return p
    return n


def _params(sem, vmem=VMEM_LIMIT):
    return pltpu.CompilerParams(dimension_semantics=sem, vmem_limit_bytes=vmem)


def _sigmoid(x):
    return 1.0 / (1.0 + jnp.exp(-x))


def _silu(x):
    return x * _sigmoid(x)


def _gelu(x):
    return 0.5 * x * (1.0 + jnp.tanh(0.7978845608028654 * (x + 0.044715 * (x * x * x))))


def _rms(x, g):
    return x * lax.rsqrt(jnp.mean(x * x, axis=-1, keepdims=True) + EPS) * g


def _convpost(hc, bias, ln_g, ln_b):
    h = hc + bias
    mu = jnp.mean(h, axis=-1, keepdims=True)
    xc = h - mu
    y = xc * lax.rsqrt(jnp.mean(xc * xc, axis=-1, keepdims=True) + EPS)
    return _silu(y * ln_g + ln_b)


def _mixf(gla, glb, ya, ga, gb):
    return _sigmoid(gla) * ya + _sigmoid(glb) * (ga * _sigmoid(gb))


class _W:
    def __init__(self, arr, layer, blocked):
        self.arr, self.layer, self.blocked = arr, layer, blocked
        if blocked:
            _, _, self.K, self.nb = arr.shape
            self.N = N_DEV * self.nb
        else:
            _, self.K, self.N = arr.shape
            self.nb = self.N

    def spec(self, tk, tn, ki, ni):
        l = self.layer
        if self.blocked:
            per = self.nb // tn
            return pl.BlockSpec((None, None, tk, tn), lambda *g: (ni(*g) // per, l, ki(*g), ni(*g) % per))
        return pl.BlockSpec((None, tk, tn), lambda *g: (l, ki(*g), ni(*g)))


_M_TILES = (1024, 512, 256, 128, 64, 32, 16, 8)
_N_TILES = (1408, 1024, 896, 512, 256, 128)
_K_TILES = (512, 1408, 896, 256, 128)


def _mm_nn(name, a, w, out_dtype, add=None):
    M, K = a.shape
    assert K == w.K
    tm, tk, tn = _pick(M, _M_TILES), _pick(K, _K_TILES), _pick(w.nb, _N_TILES)
    nk = K // tk

    def body(*refs):
        if add is None:
            a_ref, w_ref, o_ref, acc = refs
        else:
            a_ref, w_ref, r_ref, o_ref, acc = refs
        k = pl.program_id(2)

        @pl.when(k == 0)
        def _():
            acc[...] = jnp.zeros_like(acc)

        acc[...] += jnp.dot(a_ref[...].astype(BF16), w_ref[...], preferred_element_type=F32)

        @pl.when(k == nk - 1)
        def _():
            res = acc[...]
            if add is not None:
                res = res + r_ref[...]
            o_ref[...] = res.astype(o_ref.dtype)

    in_specs = [pl.BlockSpec((tm, tk), lambda i, j, k: (i, k)),
                w.spec(tk, tn, lambda i, j, k: k, lambda i, j, k: j)]
    args = [a, w.arr]
    if add is not None:
        in_specs.append(pl.BlockSpec((tm, tn), lambda i, j, k: (i, j)))
        args.append(add)
    return pl.pallas_call(
        body, name=name, grid=(M // tm, w.N // tn, nk), in_specs=in_specs,
        out_specs=pl.BlockSpec((tm, tn), lambda i, j, k: (i, j)),
        out_shape=jax.ShapeDtypeStruct((M, w.N), out_dtype),
        scratch_shapes=[pltpu.VMEM((tm, tn), F32)],
        compiler_params=_params(("parallel", "parallel", "arbitrary")),
    )(*args)


def _mm_nt(name, a, w, out_dtype):
    M, N = a.shape
    assert N == w.N
    tm, tkk, tnn = _pick(M, _M_TILES), _pick(w.K, _N_TILES), _pick(w.nb, _K_TILES)
    nn = N // tnn

    def body(a_ref, w_ref, o_ref, acc):
        n = pl.program_id(2)

        @pl.when(n == 0)
        def _():
            acc[...] = jnp.zeros_like(acc)

        acc[...] += lax.dot_general(a_ref[...].astype(BF16), w_ref[...], (((1,), (1,)), ((), ())),
                                    preferred_element_type=F32)

        @pl.when(n == nn - 1)
        def _():
            o_ref[...] = acc[...].astype(o_ref.dtype)

    return pl.pallas_call(
        body, name=name, grid=(M // tm, w.K // tkk, nn),
        in_specs=[pl.BlockSpec((tm, tnn), lambda i, j, n: (i, n)),
                  w.spec(tkk, tnn, lambda i, j, n: j, lambda i, j, n: n)],
        out_specs=pl.BlockSpec((tm, tkk), lambda i, j, n: (i, j)),
        out_shape=jax.ShapeDtypeStruct((M, w.K), out_dtype),
        scratch_shapes=[pltpu.VMEM((tm, tkk), F32)],
        compiler_params=_params(("parallel", "parallel", "arbitrary")),
    )(a, w.arr)


def _mm_tn(name, a, b, nb=None):
    T, K = a.shape
    _, N = b.shape
    width = N if nb is None else nb
    tkk, tn, tt = _pick(K, (1024, 512, 256, 128)), _pick(width, _N_TILES), _pick(T, (512, 256, 128, 64, 32, 16))
    nt = T // tt
    per = width // tn

    def body(a_ref, b_ref, o_ref, acc):
        t = pl.program_id(2)

        @pl.when(t == 0)
        def _():
            acc[...] = jnp.zeros_like(acc)

        acc[...] += lax.dot_general(a_ref[...].astype(BF16), b_ref[...].astype(BF16), (((0,), (0,)), ((), ())),
                                    preferred_element_type=F32)

        @pl.when(t == nt - 1)
        def _():
            o_ref[...] = acc[...].astype(o_ref.dtype)

    if nb is None:
        out_spec = pl.BlockSpec((tkk, tn), lambda i, j, t: (i, j))
        out_shape = jax.ShapeDtypeStruct((K, N), BF16)
    else:
        out_spec = pl.BlockSpec((None, tkk, tn), lambda i, j, t: (j // per, i, j % per))
        out_shape = jax.ShapeDtypeStruct((N_DEV, K, nb), BF16)
    return pl.pallas_call(
        body, name=name, grid=(K // tkk, N // tn, nt),
        in_specs=[pl.BlockSpec((tt, tkk), lambda i, j, t: (t, i)),
                  pl.BlockSpec((tt, tn), lambda i, j, t: (t, j))],
        out_specs=out_spec, out_shape=out_shape,
        scratch_shapes=[pltpu.VMEM((tkk, tn), F32)],
        compiler_params=_params(("parallel", "parallel", "arbitrary")),
    )(a, b)


def _rowwise(name, fn, rows, consts, outs, accs=(), tile=256, ncol=1, nrows=None):
    n_r, n_c, n_o = len(rows), len(consts), len(outs)
    T = rows[0][0].shape[0] if nrows is None else nrows
    tile = _pick(T, tuple(t for t in (512, 256, 128, 64, 32, 16, 8) if t <= tile))
    nt = T // tile

    def body(*refs):
        vals = [r[...] for r in refs[:n_r + n_c]]
        res = fn(*vals)
        if not isinstance(res, (tuple, list)):
            res = (res,)
        o_refs = refs[n_r + n_c:n_r + n_c + n_o]
        a_refs = refs[n_r + n_c + n_o:]
        for r, v in zip(o_refs, res[:n_o]):
            r[...] = v.astype(r.dtype)
        first = pl.program_id(1) == 0
        for r, v in zip(a_refs, res[n_o:]):
            @pl.when(first)
            def _(r=r, v=v):
                r[...] = v.astype(F32)

            @pl.when(jnp.logical_not(first))
            def _(r=r, v=v):
                r[...] += v.astype(F32)

    in_specs, args = [], []
    for arr, w, off, roff in rows:
        rb = roff // tile
        assert roff % tile == 0
        in_specs.append(pl.BlockSpec((tile, w), lambda j, i, off=off, rb=rb: (i + rb, off + j)))
        args.append(arr)
    for cst in consts:
        in_specs.append(pl.BlockSpec(cst.shape, lambda j, i: (0, 0)))
        args.append(cst)
    out_specs, out_shape = [], []
    for tw, dt in outs:
        out_specs.append(pl.BlockSpec((tile, tw // ncol), lambda j, i: (i, j)))
        out_shape.append(jax.ShapeDtypeStruct((T, tw), dt))
    for nr, tw in accs:
        out_specs.append(pl.BlockSpec((nr, tw // ncol), lambda j, i: (0, j)))
        out_shape.append(jax.ShapeDtypeStruct((nr, tw), F32))
    res = pl.pallas_call(
        body, name=name, grid=(ncol, nt), in_specs=in_specs, out_specs=out_specs, out_shape=out_shape,
        compiler_params=_params(("parallel", "arbitrary")),
    )(*args)
    return res


def _rms_fwd(name, x, g):
    D = x.shape[1]
    return _rowwise(name, lambda xv, gv: _rms(xv, gv), [(x, D, 0, 0)], [g], [(D, BF16)])[0]


def _rms_bwd(name, x, g, dh, dx_in):
    D = x.shape[1]

    def fn(xv, dhv, dxv, gv):
        _, vjp = jax.vjp(_rms, xv, gv)
        dx, dg = vjp(dhv.astype(F32))
        tot = dx + dxv
        return tot, tot, jnp.sum(dg, axis=0, keepdims=True)

    return _rowwise(name, fn, [(x, D, 0, 0), (dh, D, 0, 0), (dx_in, D, 0, 0)], [g], [(D, F32), (D, BF16)], [(1, D)],
                    tile=128)


def _lag_views(win, K, R, forward):
    n = win.shape[0]
    for r in range(8):
        if r >= K:
            break
        if r == 0:
            rolled = win
        else:
            rolled = pltpu.roll(win, (n - r) if forward else r, axis=0)
        for q in range((K - 1 - r) // 8 + 1):
            s = 8 * q + r
            if forward:
                yield s, rolled[8 * q:8 * q + R]
            else:
                yield s, rolled[HALO - 8 * q:HALO - 8 * q + R]


def _conv_chunk(win, w_ref, K, R):
    acc = None
    for s, view in _lag_views(win, K, R, forward=False):
        term = w_ref[K - 1 - s:K - s, :] * view
        acc = term if acc is None else acc + term
    return acc


def _conv_chunk_t(win, w_ref, K, R):
    acc = None
    for s, view in _lag_views(win, K, R, forward=True):
        term = w_ref[K - 1 - s:K - s, :] * view
        acc = term if acc is None else acc + term
    return acc


def _conv_dw(xwin, dy, K, R):
    taps = [None] * K
    for s, view in _lag_views(xwin, K, R, forward=False):
        taps[K - 1 - s] = jnp.sum(dy * view, axis=0, keepdims=True)
    return taps


def _chunks(T):
    R = _pick(T, (128, 64, 32))
    return R, T // R


def _glu_conv_fwd(name, proj, w, cw_total):
    T = proj.shape[0]
    C = cw_total
    cw = LANE
    nb = C // cw
    R, nch = _chunks(T)

    def body(a_ref, b_ref, w_ref, o_ref, s_ref):
        s_ref[0:HALO, :] = jnp.zeros((HALO, cw), F32)

        def fill(i, _):
            r0 = pl.multiple_of(i * R, R)
            s_ref[pl.ds(HALO + r0, R), :] = a_ref[pl.ds(r0, R), :] * _sigmoid(b_ref[pl.ds(r0, R), :])
            return 0

        lax.fori_loop(0, nch, fill, 0)

        def conv(i, _):
            r0 = pl.multiple_of(i * R, R)
            o_ref[pl.ds(r0, R), :] = _conv_chunk(s_ref[pl.ds(r0, R + HALO), :], w_ref, CONV_K, R)
            return 0

        lax.fori_loop(0, nch, conv, 0)

    return pl.pallas_call(
        body, name=name, grid=(nb,),
        in_specs=[pl.BlockSpec((T, cw), lambda j: (0, j)), pl.BlockSpec((T, cw), lambda j: (0, nb + j)),
                  pl.BlockSpec((HALO, cw), lambda j: (0, j))],
        out_specs=pl.BlockSpec((T, cw), lambda j: (0, j)),
        out_shape=jax.ShapeDtypeStruct((T, C), F32),
        scratch_shapes=[pltpu.VMEM((T + HALO, cw), F32)],
        compiler_params=_params(("parallel",)),
    )(proj, proj, w)


def _glu_conv_bwd(name, proj, w, dhc, cw_total):
    T = proj.shape[0]
    C = cw_total
    cw = LANE
    nb = C // cw
    R, nch = _chunks(T)

    def body(a_ref, b_ref, w_ref, dy_ref, da_ref, db_ref, dw_ref, s_ref, g_ref, acc_ref):
        s_ref[0:HALO, :] = jnp.zeros((HALO, cw), F32)
        g_ref[T:T + HALO, :] = jnp.zeros((HALO, cw), F32)
        acc_ref[...] = jnp.zeros_like(acc_ref)

        def fill(i, _):
            r0 = pl.multiple_of(i * R, R)
            s_ref[pl.ds(HALO + r0, R), :] = a_ref[pl.ds(r0, R), :] * _sigmoid(b_ref[pl.ds(r0, R), :])
            g_ref[pl.ds(r0, R), :] = dy_ref[pl.ds(r0, R), :]
            return 0

        lax.fori_loop(0, nch, fill, 0)

        def back(i, _):
            r0 = pl.multiple_of(i * R, R)
            dhg = _conv_chunk_t(g_ref[pl.ds(r0, R + HALO), :], w_ref, CONV_K, R)
            av = a_ref[pl.ds(r0, R), :]
            sg = _sigmoid(b_ref[pl.ds(r0, R), :])
            da_ref[pl.ds(r0, R), :] = (dhg * sg).astype(da_ref.dtype)
            db_ref[pl.ds(r0, R), :] = (dhg * av * sg * (1.0 - sg)).astype(db_ref.dtype)
            taps = _conv_dw(s_ref[pl.ds(r0, R + HALO), :], dy_ref[pl.ds(r0, R), :], CONV_K, R)
            for k, tap in enumerate(taps):
                acc_ref[k:k + 1, :] += tap
            return 0

        lax.fori_loop(0, nch, back, 0)
        dw_ref[...] = acc_ref[...]

    return pl.pallas_call(
        body, name=name, grid=(nb,),
        in_specs=[pl.BlockSpec((T, cw), lambda j: (0, j)), pl.BlockSpec((T, cw), lambda j: (0, nb + j)),
                  pl.BlockSpec((HALO, cw), lambda j: (0, j)), pl.BlockSpec((T, cw), lambda j: (0, j))],
        out_specs=[pl.BlockSpec((T, cw), lambda j: (0, j)), pl.BlockSpec((T, cw), lambda j: (0, j)),
                   pl.BlockSpec((HALO, cw), lambda j: (0, j))],
        out_shape=[jax.ShapeDtypeStruct((T, C), BF16), jax.ShapeDtypeStruct((T, C), BF16),
                   jax.ShapeDtypeStruct((HALO, C), F32)],
        scratch_shapes=[pltpu.VMEM((T + HALO, cw), F32), pltpu.VMEM((T + HALO, cw), F32),
                        pltpu.VMEM((HALO, cw), F32)],
        compiler_params=_params(("parallel",)),
    )(proj, proj, w, dhc)


def _ffn_conv_fwd(name, up, w, dff):
    T = up.shape[0]
    cw = LANE
    nb = dff // cw
    R, nch = _chunks(T)

    def body(g_ref, v_ref, wg_ref, wv_ref, o_ref, sg_ref, sv_ref):
        sg_ref[0:HALO, :] = jnp.zeros((HALO, cw), F32)
        sv_ref[0:HALO, :] = jnp.zeros((HALO, cw), F32)

        def fill(i, _):
            r0 = pl.multiple_of(i * R, R)
            sg_ref[pl.ds(HALO + r0, R), :] = g_ref[pl.ds(r0, R), :]
            sv_ref[pl.ds(HALO + r0, R), :] = v_ref[pl.ds(r0, R), :]
            return 0

        lax.fori_loop(0, nch, fill, 0)

        def conv(i, _):
            r0 = pl.multiple_of(i * R, R)
            gc = _conv_chunk(sg_ref[pl.ds(r0, R + HALO), :], wg_ref, FFN_K, R)
            vc = _conv_chunk(sv_ref[pl.ds(r0, R + HALO), :], wv_ref, FFN_K, R)
            o_ref[pl.ds(r0, R), :] = (_silu(gc) * vc).astype(o_ref.dtype)
            return 0

        lax.fori_loop(0, nch, conv, 0)

    return pl.pallas_call(
        body, name=name, grid=(nb,),
        in_specs=[pl.BlockSpec((T, cw), lambda j: (0, j)), pl.BlockSpec((T, cw), lambda j: (0, nb + j)),
                  pl.BlockSpec((8, cw), lambda j: (0, j)), pl.BlockSpec((8, cw), lambda j: (0, nb + j))],
        out_specs=pl.BlockSpec((T, cw), lambda j: (0, j)),
        out_shape=jax.ShapeDtypeStruct((T, dff), BF16),
        scratch_shapes=[pltpu.VMEM((T + HALO, cw), F32), pltpu.VMEM((T + HALO, cw), F32)],
        compiler_params=_params(("parallel",)),
    )(up, up, w, w)


def _ffn_conv_bwd(name, up, w, dact, dff):
    T = up.shape[0]
    cw = LANE
    nb = dff // cw
    R, nch = _chunks(T)

    def body(g_ref, v_ref, wg_ref, wv_ref, da_ref, dg_ref, dv_ref, dwg_ref, dwv_ref,
             sg_ref, sv_ref, tg_ref, tv_ref, ag_ref, av_ref):
        zero = jnp.zeros((HALO, cw), F32)
        sg_ref[0:HALO, :] = zero
        sv_ref[0:HALO, :] = zero
        tg_ref[T:T + HALO, :] = zero
        tv_ref[T:T + HALO, :] = zero
        ag_ref[...] = jnp.zeros_like(ag_ref)
        av_ref[...] = jnp.zeros_like(av_ref)

        def fill(i, _):
            r0 = pl.multiple_of(i * R, R)
            sg_ref[pl.ds(HALO + r0, R), :] = g_ref[pl.ds(r0, R), :]
            sv_ref[pl.ds(HALO + r0, R), :] = v_ref[pl.ds(r0, R), :]
            return 0

        lax.fori_loop(0, nch, fill, 0)

        def grads(i, _):
            r0 = pl.multiple_of(i * R, R)
            gwin = sg_ref[pl.ds(r0, R + HALO), :]
            vwin = sv_ref[pl.ds(r0, R + HALO), :]
            gc = _conv_chunk(gwin, wg_ref, FFN_K, R)
            vc = _conv_chunk(vwin, wv_ref, FFN_K, R)
            da = da_ref[pl.ds(r0, R), :].astype(F32)
            sg = _sigmoid(gc)
            dgc = da * vc * (sg * (1.0 + gc * (1.0 - sg)))
            dvc = da * (gc * sg)
            tg_ref[pl.ds(r0, R), :] = dgc
            tv_ref[pl.ds(r0, R), :] = dvc
            for k, tap in enumerate(_conv_dw(gwin, dgc, FFN_K, R)):
                ag_ref[k:k + 1, :] += tap
            for k, tap in enumerate(_conv_dw(vwin, dvc, FFN_K, R)):
                av_ref[k:k + 1, :] += tap
            return 0

        lax.fori_loop(0, nch, grads, 0)

        def back(i, _):
            r0 = pl.multiple_of(i * R, R)
            dg_ref[pl.ds(r0, R), :] = _conv_chunk_t(tg_ref[pl.ds(r0, R + HALO), :], wg_ref, FFN_K, R).astype(dg_ref.dtype)
            dv_ref[pl.ds(r0, R), :] = _conv_chunk_t(tv_ref[pl.ds(r0, R + HALO), :], wv_ref, FFN_K, R).astype(dv_ref.dtype)
            return 0

        lax.fori_loop(0, nch, back, 0)
        dwg_ref[...] = ag_ref[...]
        dwv_ref[...] = av_ref[...]

    col = lambda j: (0, j)
    dg, dv, dwg, dwv = pl.pallas_call(
        body, name=name, grid=(nb,),
        in_specs=[pl.BlockSpec((T, cw), col), pl.BlockSpec((T, cw), lambda j: (0, nb + j)),
                  pl.BlockSpec((8, cw), col), pl.BlockSpec((8, cw), lambda j: (0, nb + j)),
                  pl.BlockSpec((T, cw), col)],
        out_specs=[pl.BlockSpec((T, cw), col), pl.BlockSpec((T, cw), col),
                   pl.BlockSpec((8, cw), col), pl.BlockSpec((8, cw), col)],
        out_shape=[jax.ShapeDtypeStruct((T, dff), BF16), jax.ShapeDtypeStruct((T, dff), BF16),
                   jax.ShapeDtypeStruct((8, dff), F32), jax.ShapeDtypeStruct((8, dff), F32)],
        scratch_shapes=[pltpu.VMEM((T + HALO, cw), F32), pltpu.VMEM((T + HALO, cw), F32),
                        pltpu.VMEM((T + HALO, cw), F32), pltpu.VMEM((T + HALO, cw), F32),
                        pltpu.VMEM((8, cw), F32), pltpu.VMEM((8, cw), F32)],
        compiler_params=_params(("parallel",)),
    )(up, up, w, w, dact)
    return jnp.concatenate([dg, dv], axis=1), jnp.concatenate([dwg, dwv], axis=1)


def _zoh(a_re, a_im, log_dt):
    ar = jnp.minimum(a_re, -1e-4)
    ai = a_im
    dt = jnp.exp(log_dt)
    mag = jnp.exp(dt * ar)
    abar_re = mag * jnp.cos(dt * ai)
    abar_im = mag * jnp.sin(dt * ai)
    den = ar * ar + ai * ai
    nr = abar_re - 1.0
    ni = abar_im
    return abar_re, abar_im, (nr * ar + ni * ai) / den, (ni * ar - nr * ai) / den


def _discretize(a_re, a_im, log_dt, a_re_h, a_im_h, log_dt_h, b_re, b_im):
    abar_re, abar_im, _, _ = _zoh(a_re, a_im, log_dt)
    _, _, z_re, z_im = _zoh(a_re_h, a_im_h, log_dt_h)
    return abar_re, abar_im, z_re * b_re - z_im * b_im, z_re * b_im + z_im * b_re


def _full_specs(arrs):
    return [pl.BlockSpec(a.shape, lambda *_, n=len(a.shape): (0,) * n) for a in arrs]


def _ssm_prep(name, raw):
    def body(*refs):
        res = _discretize(*[r[...] for r in refs[:8]])
        for r, v in zip(refs[8:], res):
            r[...] = v

    outs = [jax.ShapeDtypeStruct(raw[0].shape, F32)] * 2 + [jax.ShapeDtypeStruct(raw[6].shape, F32)] * 2
    return pl.pallas_call(body, name=name, in_specs=_full_specs(raw), out_specs=_full_specs(outs), out_shape=outs)(*raw)


def _ssm_prep_bwd(name, raw, cots):
    G = raw[0].shape[0]
    H = raw[3].shape[0] // G

    def body(*refs):
        _, vjp = jax.vjp(_discretize, *[r[...] for r in refs[:8]])
        g = vjp(tuple(r[...] for r in refs[8:12]))
        outs = refs[12:]
        for k in range(3):
            rep = g[3 + k]
            outs[k][...] = g[k] + jnp.sum(rep.reshape(G, H, rep.shape[1]), axis=1)
        outs[3][...] = g[6]
        outs[4][...] = g[7]

    outs = [jax.ShapeDtypeStruct(a.shape, F32) for a in (raw[0], raw[1], raw[2], raw[6], raw[7])]
    return pl.pallas_call(body, name=name, in_specs=_full_specs(list(raw) + list(cots)), out_specs=_full_specs(outs),
                          out_shape=outs)(*raw, *cots)


def _cmul(ar, ai, br, bi):
    return ar * br - ai * bi, ar * bi + ai * br


def _scan_coefs(ar, ai, reverse):
    W = ar.shape[1]
    row = lax.broadcasted_iota(jnp.int32, (8, W), 0)
    p = [None] * 9
    p[1] = (ar, ai)
    for n in range(2, 9):
        p[n] = _cmul(*p[n // 2], *p[n - n // 2])
    steps = []
    for s in (1, 2, 4):
        valid = (row <= 7 - s) if reverse else (row >= s)
        steps.append((jnp.where(valid, p[s][0], 0.0), jnp.where(valid, p[s][1], 0.0)))
    pr = jnp.zeros((8, W), F32)
    pi = jnp.zeros((8, W), F32)
    for i in range(8):
        n = (8 - i) if reverse else (i + 1)
        pr = jnp.where(row == i, p[n][0], pr)
        pi = jnp.where(row == i, p[n][1], pi)
    return steps, (pr, pi)


def _scan_tile(xr, xi, cr, ci, coefs, reverse):
    steps, (pr, pi) = coefs
    for s, (sr, si) in zip((1, 2, 4), steps):
        shift = (8 - s) if reverse else s
        rr = pltpu.roll(xr, shift, axis=0)
        ri = pltpu.roll(xi, shift, axis=0)
        xr, xi = xr + sr * rr - si * ri, xi + sr * ri + si * rr
    xr, xi = xr + pr * cr - pi * ci, xi + pr * ci + pi * cr
    return xr, xi


def _edge_rows(x, reverse):
    W = x.shape[1]
    return jnp.broadcast_to(x[0:1, :] if reverse else x[7:8, :], (8, W))


def _ssm_chunk(T):
    return _pick(T, (256, 128, 64))


def _ssm_fwd(name, proj, u_off, p, width):
    T = proj.shape[0]
    NB = width // LANE
    Q = _ssm_chunk(T)
    nch = T // Q
    W = SSM_LANES

    def body(u_ref, bre, bim, cre, cim, ar_ref, ai_ref, d_ref, y_ref, ckr_ref, cki_ref, br_s, bi_s, car_r, car_i):
        c = pl.program_id(1)

        @pl.when(c == 0)
        def _():
            car_r[...] = jnp.zeros_like(car_r)
            car_i[...] = jnp.zeros_like(car_i)

        ckr_ref[...] = car_r[...]
        cki_ref[...] = car_i[...]
        u = u_ref[...]
        u16 = u.astype(BF16)
        br_s[...] = jnp.dot(u16, bre[...], preferred_element_type=F32)
        bi_s[...] = jnp.dot(u16, bim[...], preferred_element_type=F32)
        coefs = _scan_coefs(ar_ref[...], ai_ref[...], False)

        def tile(j, carry):
            r0 = pl.multiple_of(j * 8, 8)
            xr, xi = _scan_tile(br_s[pl.ds(r0, 8), :], bi_s[pl.ds(r0, 8), :], carry[0], carry[1], coefs, False)
            br_s[pl.ds(r0, 8), :] = xr
            bi_s[pl.ds(r0, 8), :] = xi
            return _edge_rows(xr, False), _edge_rows(xi, False)

        cr, ci = lax.fori_loop(0, Q // 8, tile, (car_r[...], car_i[...]))
        car_r[...] = cr
        car_i[...] = ci
        nt = (((1,), (1,)), ((), ()))
        y = (lax.dot_general(br_s[...].astype(BF16), cre[...], nt, preferred_element_type=F32)
             - lax.dot_general(bi_s[...].astype(BF16), cim[...], nt, preferred_element_type=F32)
             + d_ref[...] * u)
        y_ref[...] = _gelu(y).astype(y_ref.dtype)

    blk = lambda b, c: (b, 0, 0)
    mat = pl.BlockSpec((None, LANE, W), blk)
    vec = pl.BlockSpec((None, 1, W), blk)
    ck = pl.BlockSpec((None, None, 8, W), lambda b, c: (b, c, 0, 0))
    return pl.pallas_call(
        body, name=name, grid=(NB, nch),
        in_specs=[pl.BlockSpec((Q, LANE), lambda b, c: (c, u_off + b)), mat, mat, mat, mat, vec, vec,
                  pl.BlockSpec((1, LANE), lambda b, c: (0, b))],
        out_specs=[pl.BlockSpec((Q, LANE), lambda b, c: (c, b)), ck, ck],
        out_shape=[jax.ShapeDtypeStruct((T, width), BF16), jax.ShapeDtypeStruct((NB, nch, 8, W), F32),
                   jax.ShapeDtypeStruct((NB, nch, 8, W), F32)],
        scratch_shapes=[pltpu.VMEM((Q, W), F32), pltpu.VMEM((Q, W), F32), pltpu.VMEM((8, W), F32),
                        pltpu.VMEM((8, W), F32)],
        compiler_params=_params(("parallel", "arbitrary")),
    )(proj, p["bre"], p["bim"], p["cre"], p["cim"], p["ar"], p["ai"], p["d"])


def _ssm_bwd(name, proj, u_off, p, ck_r, ck_i, dyg, width):
    T = proj.shape[0]
    NB = width // LANE
    Q = _ssm_chunk(T)
    nch = T // Q
    W = SSM_LANES
    nt_dims = (((1,), (1,)), ((), ()))
    tn_dims = (((0,), (0,)), ((), ()))

    def body(u_ref, dy_ref, ckr_ref, cki_ref, bre, bim, cre, cim, ar_ref, ai_ref, d_ref,
             du_ref, dbr_ref, dbi_ref, dcr_ref, dci_ref, dar_ref, dai_ref, dd_ref,
             xr_s, xi_s, lr_s, li_s, lam_r, lam_i):
        c = pl.program_id(1)

        @pl.when(c == 0)
        def _():
            lam_r[...] = jnp.zeros_like(lam_r)
            lam_i[...] = jnp.zeros_like(lam_i)
            for r in (dbr_ref, dbi_ref, dcr_ref, dci_ref, dar_ref, dai_ref, dd_ref):
                r[...] = jnp.zeros_like(r)

        u = u_ref[...]
        u16 = u.astype(BF16)
        ar, ai = ar_ref[...], ai_ref[...]
        xr_s[0:8, :] = ckr_ref[...]
        xi_s[0:8, :] = cki_ref[...]
        xr_s[8:Q + 8, :] = jnp.dot(u16, bre[...], preferred_element_type=F32)
        xi_s[8:Q + 8, :] = jnp.dot(u16, bim[...], preferred_element_type=F32)
        fcoefs = _scan_coefs(ar, ai, False)

        def ftile(j, carry):
            r0 = pl.multiple_of(j * 8 + 8, 8)
            xr, xi = _scan_tile(xr_s[pl.ds(r0, 8), :], xi_s[pl.ds(r0, 8), :], carry[0], carry[1], fcoefs, False)
            xr_s[pl.ds(r0, 8), :] = xr
            xi_s[pl.ds(r0, 8), :] = xi
            return _edge_rows(xr, False), _edge_rows(xi, False)

        lax.fori_loop(0, Q // 8, ftile, (ckr_ref[...], cki_ref[...]))
        xr16 = xr_s[8:Q + 8, :].astype(BF16)
        xi16 = xi_s[8:Q + 8, :].astype(BF16)
        y = (lax.dot_general(xr16, cre[...], nt_dims, preferred_element_type=F32)
             - lax.dot_general(xi16, cim[...], nt_dims, preferred_element_type=F32) + d_ref[...] * u)
        _, gelu_vjp = jax.vjp(_gelu, y)
        dy = gelu_vjp(dy_ref[...].astype(F32))[0]
        dy16 = dy.astype(BF16)
        dd_ref[...] += jnp.broadcast_to(jnp.sum(dy * u, axis=0, keepdims=True), (8, LANE))
        dcr_ref[...] += lax.dot_general(dy16, xr16, tn_dims, preferred_element_type=F32)
        dci_ref[...] -= lax.dot_general(dy16, xi16, tn_dims, preferred_element_type=F32)
        lr_s[...] = jnp.dot(dy16, cre[...], preferred_element_type=F32)
        li_s[...] = -jnp.dot(dy16, cim[...], preferred_element_type=F32)
        rcoefs = _scan_coefs(ar, -ai, True)
        row = lax.broadcasted_iota(jnp.int32, (8, W), 0)

        def rtile(jj, carry):
            j = Q // 8 - 1 - jj
            r0 = pl.multiple_of(j * 8, 8)
            lr, li = _scan_tile(lr_s[pl.ds(r0, 8), :], li_s[pl.ds(r0, 8), :], carry[0], carry[1], rcoefs, True)
            lr_s[pl.ds(r0, 8), :] = lr
            li_s[pl.ds(r0, 8), :] = li
            cur_r, cur_i = xr_s[pl.ds(r0 + 8, 8), :], xi_s[pl.ds(r0 + 8, 8), :]
            prv_r, prv_i = xr_s[pl.ds(r0, 8), :], xi_s[pl.ds(r0, 8), :]
            xpr = jnp.where(row == 0, _edge_rows(prv_r, False), pltpu.roll(cur_r, 1, axis=0))
            xpi = jnp.where(row == 0, _edge_rows(prv_i, False), pltpu.roll(cur_i, 1, axis=0))
            return (_edge_rows(lr, True), _edge_rows(li, True),
                    carry[2] + lr * xpr + li * xpi, carry[3] + li * xpr - lr * xpi)

        zero = jnp.zeros((8, W), F32)
        cr, ci, sar, sai = lax.fori_loop(0, Q // 8, rtile, (lam_r[...], lam_i[...], zero, zero))
        lam_r[...] = cr
        lam_i[...] = ci
        dar_ref[...] += jnp.broadcast_to(jnp.sum(sar, axis=0, keepdims=True), (8, W))
        dai_ref[...] += jnp.broadcast_to(jnp.sum(sai, axis=0, keepdims=True), (8, W))
        lr16 = lr_s[...].astype(BF16)
        li16 = li_s[...].astype(BF16)
        dbr_ref[...] += lax.dot_general(u16, lr16, tn_dims, preferred_element_type=F32)
        dbi_ref[...] += lax.dot_general(u16, li16, tn_dims, preferred_element_type=F32)
        du = (lax.dot_general(lr16, bre[...], nt_dims, preferred_element_type=F32)
              + lax.dot_general(li16, bim[...], nt_dims, preferred_element_type=F32) + d_ref[...] * dy)
        du_ref[...] = du.astype(du_ref.dtype)

    blk = lambda b, c: (b, 0, 0)
    mat = pl.BlockSpec((None, LANE, W), blk)
    vec = pl.BlockSpec((None, 1, W), blk)
    acc8 = pl.BlockSpec((None, 8, W), blk)
    ck = pl.BlockSpec((None, None, 8, W), lambda b, c: (b, nch - 1 - c, 0, 0))
    return pl.pallas_call(
        body, name=name, grid=(NB, nch),
        in_specs=[pl.BlockSpec((Q, LANE), lambda b, c: (nch - 1 - c, u_off + b)),
                  pl.BlockSpec((Q, LANE), lambda b, c: (nch - 1 - c, b)), ck, ck, mat, mat, mat, mat, vec, vec,
                  pl.BlockSpec((1, LANE), lambda b, c: (0, b))],
        out_specs=[pl.BlockSpec((Q, LANE), lambda b, c: (nch - 1 - c, b)), mat, mat, mat, mat, acc8, acc8,
                   pl.BlockSpec((None, 8, LANE), blk)],
        out_shape=[jax.ShapeDtypeStruct((T, width), BF16)] + [jax.ShapeDtypeStruct((NB, LANE, W), F32)] * 4
                  + [jax.ShapeDtypeStruct((NB, 8, W), F32)] * 2 + [jax.ShapeDtypeStruct((NB, 8, LANE), F32)],
        scratch_shapes=[pltpu.VMEM((Q + 8, W), F32), pltpu.VMEM((Q + 8, W), F32), pltpu.VMEM((Q, W), F32),
                        pltpu.VMEM((Q, W), F32), pltpu.VMEM((8, W), F32), pltpu.VMEM((8, W), F32)],
        compiler_params=_params(("parallel", "arbitrary")),
    )(proj, dyg, ck_r, ck_i, p["bre"], p["bim"], p["cre"], p["cim"], p["ar"], p["ai"], p["d"])


def _block_diag(w):
    G, H, P = w.shape
    eye = jnp.eye(8, dtype=w.dtype)
    return (w.reshape(G // 8, 8, H, 1, P) * eye[None, :, None, :, None]).reshape(G // 8, 8 * H, 8 * P)


def _block_diag_t(d, H, P):
    NB = d.shape[0]
    d = d.reshape(NB, 8, H, 8, P)
    eye = jnp.eye(8, dtype=d.dtype)
    return jnp.sum(d * eye[None, :, None, :, None], axis=3).reshape(NB * 8, H, P)


def _attn_fwd(name, q, kv, heads):
    T, D = q.shape
    Mm = kv.shape[0]
    hd = D // heads
    tq = _pick(T, (512, 256, 128))
    scale = hd ** -0.5

    def body(q_ref, k_ref, v_ref, o_ref):
        s = lax.dot_general(q_ref[...], k_ref[...], (((1,), (1,)), ((), ())), preferred_element_type=F32) * scale
        s = s - jnp.max(s, axis=-1, keepdims=True)
        e = jnp.exp(s)
        p = e / jnp.sum(e, axis=-1, keepdims=True)
        o_ref[...] = jnp.dot(p.astype(BF16), v_ref[...], preferred_element_type=F32).astype(o_ref.dtype)

    return pl.pallas_call(
        body, name=name, grid=(heads, T // tq),
        in_specs=[pl.BlockSpec((tq, hd), lambda h, i: (i, h)), pl.BlockSpec((Mm, hd), lambda h, i: (0, h)),
                  pl.BlockSpec((Mm, hd), lambda h, i: (0, heads + h))],
        out_specs=pl.BlockSpec((tq, hd), lambda h, i: (i, h)),
        out_shape=jax.ShapeDtypeStruct((T, D), BF16),
        compiler_params=_params(("parallel", "parallel")),
    )(q, kv, kv)


def _attn_bwd(name, q, kv, do, heads):
    T, D = q.shape
    Mm = kv.shape[0]
    hd = D // heads
    tq = _pick(T, (512, 256, 128))
    scale = hd ** -0.5
    nt_dims = (((1,), (1,)), ((), ()))
    tn_dims = (((0,), (0,)), ((), ()))

    def body(q_ref, k_ref, v_ref, do_ref, dq_ref, dk_ref, dv_ref):
        i = pl.program_id(1)

        @pl.when(i == 0)
        def _():
            dk_ref[...] = jnp.zeros_like(dk_ref)
            dv_ref[...] = jnp.zeros_like(dv_ref)

        qv, kvl, vv, dov = q_ref[...], k_ref[...], v_ref[...], do_ref[...]
        s = lax.dot_general(qv, kvl, nt_dims, preferred_element_type=F32) * scale
        s = s - jnp.max(s, axis=-1, keepdims=True)
        e = jnp.exp(s)
        p = e / jnp.sum(e, axis=-1, keepdims=True)
        p16 = p.astype(BF16)
        dv_ref[...] += lax.dot_general(p16, dov, tn_dims, preferred_element_type=F32)
        dp = lax.dot_general(dov, vv, nt_dims, preferred_element_type=F32)
        ds = (p * (dp - jnp.sum(dp * p, axis=-1, keepdims=True)) * scale).astype(BF16)
        dq_ref[...] = jnp.dot(ds, kvl, preferred_element_type=F32).astype(dq_ref.dtype)
        dk_ref[...] += lax.dot_general(ds, qv, tn_dims, preferred_element_type=F32)

    return pl.pallas_call(
        body, name=name, grid=(heads, T // tq),
        in_specs=[pl.BlockSpec((tq, hd), lambda h, i: (i, h)), pl.BlockSpec((Mm, hd), lambda h, i: (0, h)),
                  pl.BlockSpec((Mm, hd), lambda h, i: (0, heads + h)), pl.BlockSpec((tq, hd), lambda h, i: (i, h))],
        out_specs=[pl.BlockSpec((tq, hd), lambda h, i: (i, h)), pl.BlockSpec((Mm, hd), lambda h, i: (0, h)),
                   pl.BlockSpec((Mm, hd), lambda h, i: (0, h))],
        out_shape=[jax.ShapeDtypeStruct((T, D), BF16), jax.ShapeDtypeStruct((Mm, D), F32),
                   jax.ShapeDtypeStruct((Mm, D), F32)],
        compiler_params=_params(("parallel", "arbitrary")),
    )(q, kv, kv, do)


def _position():
    return lax.axis_index("x"), lax.axis_index("y"), lax.axis_index("c")


def _allgather(name, blk, row_mode):
    L = blk.shape[0]
    out_shape = (L, N_DEV) + blk.shape[1:] if row_mode else (N_DEV,) + blk.shape

    def body(x_ref, out_ref, send_sems, recv_sems, local_sem):
        x, y, c = _position()
        me, sibling = (x, y, c), (x, y, 1 - c)
        chips = [(1 - x, y), (x, 1 - y), (1 - x, 1 - y)]

        def slot(px, py, pc):
            b = 4 * px + 2 * py + pc
            return out_ref.at[:, b] if row_mode else out_ref.at[b]

        def copy(k, block, to, src=None):
            return pltpu.make_async_remote_copy(
                src_ref=slot(*block) if src is None else src, dst_ref=slot(*block),
                send_sem=send_sems.at[k], recv_sem=recv_sems.at[k], device_id=to, device_id_type=MESH_ID)

        mine = pltpu.make_async_copy(x_ref, slot(*me), local_sem)
        mine.start()
        first = [copy(0, me, sibling, src=x_ref)]
        first += [copy(1 + j, me, (*chip, c), src=x_ref) for j, chip in enumerate(chips)]
        for cp in first:
            cp.start()
        passed = [copy(4 + j, (*chip, c), sibling) for j, chip in enumerate(chips)]
        for j, chip in enumerate(chips):
            copy(1 + j, (*chip, c), me).wait_recv()
            passed[j].start()
        copy(0, sibling, me).wait_recv()
        for j, chip in enumerate(chips):
            copy(4 + j, (*chip, 1 - c), me).wait_recv()
        for cp in first + passed:
            cp.wait_send()
        mine.wait()

    return pl.pallas_call(
        body, name=name, in_specs=[ANY], out_specs=ANY,
        out_shape=jax.ShapeDtypeStruct(out_shape, blk.dtype),
        scratch_shapes=[pltpu.SemaphoreType.DMA((7,)), pltpu.SemaphoreType.DMA((7,)), pltpu.SemaphoreType.DMA],
    )(blk)


def _rs_sibling(name, grads):
    L = len(grads)
    blk = grads[0].shape[1:]

    def body(*refs):
        g_refs, out_ref, send_sems, recv_sems, local_sems = refs[:L], refs[L], refs[L + 1], refs[L + 2], refs[L + 3]
        x, y, c = _position()
        sibling = (x, y, 1 - c)
        remote, local = [], []
        for l in range(L):
            for q in range(4):
                k = 4 * l + q
                remote.append(pltpu.make_async_remote_copy(
                    src_ref=g_refs[l].at[2 * q + (1 - c)], dst_ref=out_ref.at[q, l],
                    send_sem=send_sems.at[k], recv_sem=recv_sems.at[k], device_id=sibling, device_id_type=MESH_ID))
                local.append(pltpu.make_async_copy(g_refs[l].at[2 * q + c], out_ref.at[4 + q, l], local_sems.at[k]))
        for cp in remote + local:
            cp.start()
        for cp in remote:
            cp.wait_recv()
        for cp in remote:
            cp.wait_send()
        for cp in local:
            cp.wait()

    return pl.pallas_call(
        body, name=name, in_specs=[ANY] * L, out_specs=ANY,
        out_shape=jax.ShapeDtypeStruct((N_DEV, L) + blk, grads[0].dtype),
        scratch_shapes=[pltpu.SemaphoreType.DMA((4 * L,)), pltpu.SemaphoreType.DMA((4 * L,)),
                        pltpu.SemaphoreType.DMA((4 * L,))],
    )(*grads)


def _rs_chips(name, part):
    def body(p_ref, out_ref, send_sems, recv_sems, local_sem):
        x, y, c = _position()
        flips = [(1 - x, y), (x, 1 - y), (1 - x, 1 - y)]
        mine = pltpu.make_async_copy(p_ref.at[2 * x + y], out_ref.at[3], local_sem)
        mine.start()
        copies = [pltpu.make_async_remote_copy(
            src_ref=p_ref.at[2 * fx + fy], dst_ref=out_ref.at[k], send_sem=send_sems.at[k], recv_sem=recv_sems.at[k],
            device_id=(fx, fy, c), device_id_type=MESH_ID) for k, (fx, fy) in enumerate(flips)]
        for cp in copies:
            cp.start()
        for cp in copies:
            cp.wait_recv()
        for cp in copies:
            cp.wait_send()
        mine.wait()

    return pl.pallas_call(
        body, name=name, in_specs=[ANY], out_specs=ANY,
        out_shape=jax.ShapeDtypeStruct(part.shape, part.dtype),
        scratch_shapes=[pltpu.SemaphoreType.DMA((3,)), pltpu.SemaphoreType.DMA((3,)), pltpu.SemaphoreType.DMA],
    )(part)


def _adamw_math(g, w, m, v):
    m = ADAM_B1 * m + (1.0 - ADAM_B1) * g
    v = ADAM_B2 * v + (1.0 - ADAM_B2) * (g * g)
    m_hat = m / (1.0 - ADAM_B1 ** ADAM_STEP)
    v_hat = v / (1.0 - ADAM_B2 ** ADAM_STEP)
    delta = -ADAM_LR * (m_hat / (jnp.sqrt(v_hat) + ADAM_EPS) + ADAM_WD * w)
    return delta, m, v


def _sum_adamw(name, parts, nparts, w, m, v):
    R, C = w.shape
    cwid = _pick(C, (512, 256, 128))

    def fn(*vals):
        g = vals[0].astype(F32)
        for pv in vals[1:nparts]:
            g = g + pv.astype(F32)
        delta, nm, nv = _adamw_math(g, *vals[nparts:])
        return g, delta, nm, nv

    rows = [(parts, cwid, 0, k * R) for k in range(nparts)] + [(a, cwid, 0, 0) for a in (w, m, v)]
    return _rowwise(name, fn, rows, [], [(C, F32)] * 4, tile=256, ncol=C // cwid, nrows=R)


def _pair_sum(name, both, R):
    C = both.shape[1]
    cwid = _pick(C, (512, 256, 128))
    fn = lambda a, b: a.astype(F32) + b.astype(F32)
    return _rowwise(name, fn, [(both, cwid, 0, 0), (both, cwid, 0, R)], [], [(C, BF16)], tile=256,
                    ncol=C // cwid, nrows=R)[0]


def _reduce_update(name, grads, w, m, v):
    L, r, c = w.shape
    landed = _rs_sibling(name + "_rs_sibling", grads)
    part = _pair_sum(name + "_pair_sum", landed.reshape(N_DEV * L * r, c), 4 * L * r)
    got = _rs_chips(name + "_rs_chips", part.reshape(4, L, r, c))
    outs = _sum_adamw(name + "_adamw", got.reshape(4 * L * r, c), 4, w.reshape(L * r, c), m.reshape(L * r, c),
                      v.reshape(L * r, c))
    return [o.reshape(L, r, c) for o in outs]


def _loss_head(name, x, g, target):
    T, D = x.shape

    def fn(xv, tv, gv):
        def f(xx, gg):
            err = _rms(xx, gg) - tv
            return 0.5 * jnp.sum(jnp.mean(err * err, axis=-1, keepdims=True))

        loss, (dx, dg) = jax.value_and_grad(f, argnums=(0, 1))(xv, gv)
        return dx, dx, jnp.full((8, LANE), loss, F32), jnp.sum(dg, axis=0, keepdims=True)

    dx, dx16, loss, dg = _rowwise(name, fn, [(x, D, 0, 0), (target, D, 0, 0)], [g], [(D, F32), (D, BF16)],
                                  [(8, LANE), (1, D)], tile=128)
    return loss[0, 0], dx, dx16, dg


_SHARDED_COL = ("w_in", "conv_dw_w", "conv_w_pw", "ssm_w_glu", "xa_w_kv", "ffn_w_up", "ffn_dw_w")
_SHARDED_ROW = ("w_out", "xa_w_q", "xa_w_o", "ffn_w_down")
_WEIGHTS = ['mix_norm_g', 'w_in', 'conv_dw_w', 'conv_dw_b', 'conv_ln_g', 'conv_ln_b', 'conv_w_pw', 'ssm_a_re',
            'ssm_a_im', 'ssm_log_dt', 'ssm_b_re', 'ssm_b_im', 'ssm_c_re', 'ssm_c_im', 'ssm_d', 'ssm_w_glu', 'w_out',
            'xa_norm_g', 'mem_norm_g', 'xa_w_q', 'xa_w_kv', 'xa_w_o', 'ffn_norm_g', 'ffn_w_up', 'ffn_dw_w',
            'ffn_w_down', 'final_norm_g']
_FWD = ['x', 'mem'] + _WEIGHTS


def _pad_rows(a, rows):
    return jnp.pad(a, ((0, 0), (0, rows - a.shape[1]), (0, 0)))


def _step(inp, target, mom_m, mom_v):
    x0 = inp["x"][0]
    mem = inp["mem"][0]
    T, D = x0.shape
    L = inp["w_in"].shape[0]
    CW = inp["conv_dw_b"].shape[1]
    SW = inp["ssm_d"].shape[1]
    DFF = inp["ffn_w_down"].shape[1] * N_DEV
    G = SW // SSM_GROUP
    NB = SW // LANE
    u_off = (2 * CW) // LANE
    gate_off = (2 * CW + SW) // 1024

    gathered = {}
    for n in _SHARDED_COL:
        small = n in ("conv_dw_w", "ffn_dw_w")
        blk = inp[n] if small else inp[n].astype(BF16)
        if n == "conv_dw_w":
            blk = _pad_rows(blk, HALO)
        if n == "ffn_dw_w":
            blk = _pad_rows(blk, 8)
        gathered[n] = _allgather("ag_" + n, blk, False)
    for n in _SHARDED_ROW:
        full = _allgather("ag_" + n, inp[n].astype(BF16), True)
        gathered[n] = full.reshape(L, N_DEV * full.shape[2], full.shape[3])

    def W(n, l):
        return _W(gathered[n], l, n in _SHARDED_COL)

    def dw_filter(n, l):
        g = gathered[n]
        return jnp.transpose(g[:, l], (1, 0, 2)).reshape(g.shape[2], N_DEV * g.shape[3])

    def row(n, l):
        return inp[n][l][None, :]

    ssm_raw, ssm_p = [], []
    for l in range(L):
        a_re, a_im, ldt = inp["ssm_a_re"][l], inp["ssm_a_im"][l], inp["ssm_log_dt"][l][:, None]
        rep = lambda a: jnp.repeat(a, SSM_GROUP, axis=0)
        flat = lambda b: jnp.transpose(b, (0, 2, 1)).reshape(G * SSM_GROUP, SSM_STATE)
        raw = (a_re, a_im, ldt, rep(a_re), rep(a_im), rep(ldt), flat(inp["ssm_b_re"][l]), flat(inp["ssm_b_im"][l]))
        abar_re, abar_im, bbar_re, bbar_im = _ssm_prep("ssm_prep", raw)
        bbar_re = bbar_re.reshape(G, SSM_GROUP, SSM_STATE)
        bbar_im = bbar_im.reshape(G, SSM_GROUP, SSM_STATE)
        ssm_raw.append(raw)
        ssm_p.append(dict(
            bre=_block_diag(bbar_re).astype(BF16), bim=_block_diag(bbar_im).astype(BF16),
            cre=_block_diag(inp["ssm_c_re"][l]).astype(BF16), cim=_block_diag(inp["ssm_c_im"][l]).astype(BF16),
            ar=abar_re.reshape(NB, 1, SSM_LANES), ai=abar_im.reshape(NB, 1, SSM_LANES), d=row("ssm_d", l)))

    saved = []
    x = x0
    for l in range(L):
        s = {"x_in": x}
        s["h1"] = _rms_fwd("rms_mix", x, row("mix_norm_g", l))
        s["proj"] = _mm_nn("mm_w_in", s["h1"], W("w_in", l), F32)
        s["hc"] = _glu_conv_fwd("conv_fwd", s["proj"], dw_filter("conv_dw_w", l), CW)
        s["hs"] = _rowwise("conv_post", _convpost, [(s["hc"], CW, 0, 0)],
                           [row("conv_dw_b", l), row("conv_ln_g", l), row("conv_ln_b", l)], [(CW, BF16)])[0]
        s["ya"] = _mm_nn("mm_w_pw", s["hs"], W("conv_w_pw", l), F32)
        s["yg"], s["ck_r"], s["ck_i"] = _ssm_fwd("ssm_fwd", s["proj"], u_off, ssm_p[l], SW)
        s["gg"] = _mm_nn("mm_w_glu", s["yg"], W("ssm_w_glu", l), F32)
        nmix = D // 1024
        mix_rows = [(s["proj"], 1024, gate_off, 0), (s["proj"], 1024, gate_off + nmix, 0), (s["ya"], 1024, 0, 0),
                    (s["gg"], 1024, 0, 0), (s["gg"], 1024, nmix, 0)]
        s["mix_rows"] = mix_rows
        s["mix"] = _rowwise("mix_fwd", _mixf, mix_rows, [], [(D, BF16)], ncol=nmix)[0]
        x = _mm_nn("mm_w_out", s["mix"], W("w_out", l), F32, add=x)
        s["x1"] = x
        s["h2"] = _rms_fwd("rms_xa", x, row("xa_norm_g", l))
        s["q"] = _mm_nn("mm_w_q", s["h2"], W("xa_w_q", l), BF16)
        s["mn"] = _rms_fwd("rms_mem", mem, row("mem_norm_g", l))
        s["kv"] = _mm_nn("mm_w_kv", s["mn"], W("xa_w_kv", l), BF16)
        s["o"] = _attn_fwd("attn_fwd", s["q"], s["kv"], XA_HEADS)
        x = _mm_nn("mm_w_o", s["o"], W("xa_w_o", l), F32, add=x)
        s["x2"] = x
        s["h3"] = _rms_fwd("rms_ffn", x, row("ffn_norm_g", l))
        s["up"] = _mm_nn("mm_w_up", s["h3"], W("ffn_w_up", l), F32)
        s["act"] = _ffn_conv_fwd("ffn_conv_fwd", s["up"], dw_filter("ffn_dw_w", l), DFF)
        x = _mm_nn("mm_w_down", s["act"], W("ffn_w_down", l), F32, add=x)
        saved.append(s)

    loss_part, dx, dx16, d_final_g = _loss_head("loss_head", x, inp["final_norm_g"][None, :], target[0])

    big = {n: [None] * L for n in _SHARDED_COL + _SHARDED_ROW}
    small = {n: [None] * L for n in _WEIGHTS if n not in big and n != "final_norm_g"}
    for l in reversed(range(L)):
        s = saved[l]
        dact = _mm_nt("mm_w_down_t", dx16, W("ffn_w_down", l), BF16)
        big["ffn_w_down"][l] = _mm_tn("mm_dw_down", s["act"], dx16).reshape(N_DEV, DFF // N_DEV, D)
        d_up, d_ffn_dw = _ffn_conv_bwd("ffn_conv_bwd", s["up"], dw_filter("ffn_dw_w", l), dact, DFF)
        big["ffn_dw_w"][l] = jnp.transpose(d_ffn_dw.reshape(8, N_DEV, 2 * DFF // N_DEV), (1, 0, 2))
        dh3 = _mm_nt("mm_w_up_t", d_up, W("ffn_w_up", l), BF16)
        big["ffn_w_up"][l] = _mm_tn("mm_dw_up", s["h3"], d_up, nb=2 * DFF // N_DEV)
        dx, dx16, small["ffn_norm_g"][l] = _rms_bwd("rms_ffn_bwd", s["x2"], row("ffn_norm_g", l), dh3, dx)
        do = _mm_nt("mm_w_o_t", dx16, W("xa_w_o", l), BF16)
        big["xa_w_o"][l] = _mm_tn("mm_dw_o", s["o"], dx16).reshape(N_DEV, D // N_DEV, D)
        dq, dk, dv = _attn_bwd("attn_bwd", s["q"], s["kv"], do, XA_HEADS)
        dkv = jnp.concatenate([dk, dv], axis=1).astype(BF16)
        dh2 = _mm_nt("mm_w_q_t", dq, W("xa_w_q", l), BF16)
        big["xa_w_q"][l] = _mm_tn("mm_dw_q", s["h2"], dq).reshape(N_DEV, D // N_DEV, D)
        dmn = _mm_nt("mm_w_kv_t", dkv, W("xa_w_kv", l), BF16)
        big["xa_w_kv"][l] = _mm_tn("mm_dw_kv", s["mn"], dkv, nb=2 * D // N_DEV)

        def mem_bwd(mv, dv_, gv):
            _, vjp = jax.vjp(_rms, mv, gv)
            return jnp.sum(vjp(dv_.astype(F32))[1], axis=0, keepdims=True)

        small["mem_norm_g"][l] = _rowwise("rms_mem_bwd", mem_bwd, [(mem, D, 0, 0), (dmn, D, 0, 0)],
                                          [row("mem_norm_g", l)], [], [(1, D)], tile=128)[0]
        dx, dx16, small["xa_norm_g"][l] = _rms_bwd("rms_xa_bwd", s["x1"], row("xa_norm_g", l), dh2, dx)
        dmix = _mm_nt("mm_w_out_t", dx16, W("w_out", l), BF16)
        big["w_out"][l] = _mm_tn("mm_dw_out", s["mix"], dx16).reshape(N_DEV, D // N_DEV, D)

        def mix_bwd(gla, glb, ya, ga, gb, dm):
            _, vjp = jax.vjp(_mixf, gla, glb, ya, ga, gb)
            return vjp(dm.astype(F32))

        nmix = D // 1024
        dgla, dglb, dya, dga, dgb = _rowwise("mix_bwd", mix_bwd, s["mix_rows"] + [(dmix, 1024, 0, 0)], [],
                                             [(D, BF16)] * 5, ncol=nmix)
        dgg = jnp.concatenate([dga, dgb], axis=1)
        dyg = _mm_nt("mm_w_glu_t", dgg, W("ssm_w_glu", l), BF16)
        big["ssm_w_glu"][l] = _mm_tn("mm_dw_glu", s["yg"], dgg, nb=2 * D // N_DEV)
        du, dbr, dbi, dcr, dci, dar, dai, dd = _ssm_bwd("ssm_bwd", s["proj"], u_off, ssm_p[l], s["ck_r"], s["ck_i"],
                                                        dyg, SW)
        cots = (dar[:, 0, :].reshape(G, SSM_STATE), dai[:, 0, :].reshape(G, SSM_STATE),
                _block_diag_t(dbr, SSM_GROUP, SSM_STATE).reshape(G * SSM_GROUP, SSM_STATE),
                _block_diag_t(dbi, SSM_GROUP, SSM_STATE).reshape(G * SSM_GROUP, SSM_STATE))
        g_are, g_aim, g_ldt, g_bre, g_bim = _ssm_prep_bwd("ssm_prep_bwd", ssm_raw[l], cots)
        small["ssm_a_re"][l], small["ssm_a_im"][l], small["ssm_log_dt"][l] = g_are, g_aim, g_ldt[:, 0]
        small["ssm_b_re"][l] = jnp.transpose(g_bre.reshape(G, SSM_GROUP, SSM_STATE), (0, 2, 1))
        small["ssm_b_im"][l] = jnp.transpose(g_bim.reshape(G, SSM_GROUP, SSM_STATE), (0, 2, 1))
        small["ssm_c_re"][l] = _block_diag_t(dcr, SSM_GROUP, SSM_STATE)
        small["ssm_c_im"][l] = _block_diag_t(dci, SSM_GROUP, SSM_STATE)
        small["ssm_d"][l] = dd[:, 0, :].reshape(SW)
        dhs = _mm_nt("mm_w_pw_t", dya, W("conv_w_pw", l), BF16)
        big["conv_w_pw"][l] = _mm_tn("mm_dw_pw", s["hs"], dya, nb=D // N_DEV)

        def post_bwd(hc, dh, b, lg, lb):
            _, vjp = jax.vjp(_convpost, hc, b, lg, lb)
            dhc, db, dlg, dlb = vjp(dh.astype(F32))
            return dhc, jnp.sum(db, axis=0, keepdims=True), jnp.sum(dlg, axis=0, keepdims=True), \
                jnp.sum(dlb, axis=0, keepdims=True)

        dhc, small["conv_dw_b"][l], small["conv_ln_g"][l], small["conv_ln_b"][l] = _rowwise(
            "conv_post_bwd", post_bwd, [(s["hc"], CW, 0, 0), (dhs, CW, 0, 0)],
            [row("conv_dw_b", l), row("conv_ln_g", l), row("conv_ln_b", l)], [(CW, F32)], [(1, CW)] * 3)
        da, db, d_conv_dw = _glu_conv_bwd("conv_bwd", s["proj"], dw_filter("conv_dw_w", l), dhc, CW)
        big["conv_dw_w"][l] = jnp.transpose(d_conv_dw.reshape(HALO, N_DEV, CW // N_DEV), (1, 0, 2))
        dproj = jnp.concatenate([da, db, du, dgla, dglb], axis=1)
        dh1 = _mm_nt("mm_w_in_t", dproj, W("w_in", l), BF16)
        big["w_in"][l] = _mm_tn("mm_dw_in", s["h1"], dproj, nb=dproj.shape[1] // N_DEV)
        dx, dx16, small["mix_norm_g"][l] = _rms_bwd("rms_mix_bwd", s["x_in"], row("mix_norm_g", l), dh1, dx)

    results = {}
    for n in _SHARDED_COL + _SHARDED_ROW:
        w, m, v = inp[n], mom_m[n], mom_v[n]
        pad = {"conv_dw_w": HALO, "ffn_dw_w": 8}.get(n)
        if pad:
            w, m, v = _pad_rows(w, pad), _pad_rows(m, pad), _pad_rows(v, pad)
        outs = _reduce_update(n, [g.astype(BF16) for g in big[n]], w, m, v)
        if pad:
            outs = [o[:, :inp[n].shape[1], :] for o in outs]
        results[n] = outs

    names = [n for n in _WEIGHTS if n not in big]
    flat_g = jnp.concatenate([(jnp.stack(small[n]) if n != "final_norm_g" else d_final_g).reshape(-1)
                              for n in names])
    sizes = [inp[n].size for n in names]
    total = sum(sizes)
    rows_p = -(-total // (8 * LANE)) * 8
    padn = rows_p * LANE - total

    def pack(parts, fill):
        return jnp.concatenate([p.reshape(-1) for p in parts] + [jnp.full((padn,), fill, F32)]).reshape(rows_p, LANE)

    g_all = _allgather("ag_small_grads", pack([flat_g], 0.0)[None], False)
    outs = _sum_adamw("small_adamw", g_all.reshape(N_DEV * rows_p, LANE), N_DEV,
                      pack([inp[n] for n in names], 0.0), pack([mom_m[n] for n in names], 0.0),
                      pack([mom_v[n] for n in names], 1.0))
    offs = 0
    for n, sz in zip(names, sizes):
        results[n] = [o.reshape(-1)[offs:offs + sz].reshape(inp[n].shape) for o in outs]
        offs += sz

    loss = lax.psum(loss_part, ("x", "y", "c"))
    grad_x = dx[None]
    return (loss, grad_x, *[results[n][0] for n in _WEIGHTS], *[results[n][1] for n in _WEIGHTS],
            *[results[n][2] for n in _WEIGHTS], *[results[n][3] for n in _WEIGHTS])


def kernel(x, mem, mix_norm_g, w_in, conv_dw_w, conv_dw_b, conv_ln_g, conv_ln_b, conv_w_pw, ssm_a_re, ssm_a_im, ssm_log_dt, ssm_b_re, ssm_b_im, ssm_c_re, ssm_c_im, ssm_d, ssm_w_glu, w_out, xa_norm_g, mem_norm_g, xa_w_q, xa_w_kv, xa_w_o, ffn_norm_g, ffn_w_up, ffn_dw_w, ffn_w_down, final_norm_g, loss_target, m_mix_norm_g, m_w_in, m_conv_dw_w, m_conv_dw_b, m_conv_ln_g, m_conv_ln_b, m_conv_w_pw, m_ssm_a_re, m_ssm_a_im, m_ssm_log_dt, m_ssm_b_re, m_ssm_b_im, m_ssm_c_re, m_ssm_c_im, m_ssm_d, m_ssm_w_glu, m_w_out, m_xa_norm_g, m_mem_norm_g, m_xa_w_q, m_xa_w_kv, m_xa_w_o, m_ffn_norm_g, m_ffn_w_up, m_ffn_dw_w, m_ffn_w_down, m_final_norm_g, v_mix_norm_g, v_w_in, v_conv_dw_w, v_conv_dw_b, v_conv_ln_g, v_conv_ln_b, v_conv_w_pw, v_ssm_a_re, v_ssm_a_im, v_ssm_log_dt, v_ssm_b_re, v_ssm_b_im, v_ssm_c_re, v_ssm_c_im, v_ssm_d, v_ssm_w_glu, v_w_out, v_xa_norm_g, v_mem_norm_g, v_xa_w_q, v_xa_w_kv, v_xa_w_o, v_ffn_norm_g, v_ffn_w_up, v_ffn_dw_w, v_ffn_w_down, v_final_norm_g):
    args = (x, mem, mix_norm_g, w_in, conv_dw_w, conv_dw_b, conv_ln_g, conv_ln_b, conv_w_pw, ssm_a_re, ssm_a_im, ssm_log_dt, ssm_b_re, ssm_b_im, ssm_c_re, ssm_c_im, ssm_d, ssm_w_glu, w_out, xa_norm_g, mem_norm_g, xa_w_q, xa_w_kv, xa_w_o, ffn_norm_g, ffn_w_up, ffn_dw_w, ffn_w_down, final_norm_g)
    ms = (m_mix_norm_g, m_w_in, m_conv_dw_w, m_conv_dw_b, m_conv_ln_g, m_conv_ln_b, m_conv_w_pw, m_ssm_a_re, m_ssm_a_im, m_ssm_log_dt, m_ssm_b_re, m_ssm_b_im, m_ssm_c_re, m_ssm_c_im, m_ssm_d, m_ssm_w_glu, m_w_out, m_xa_norm_g, m_mem_norm_g, m_xa_w_q, m_xa_w_kv, m_xa_w_o, m_ffn_norm_g, m_ffn_w_up, m_ffn_dw_w, m_ffn_w_down, m_final_norm_g)
    vs = (v_mix_norm_g, v_w_in, v_conv_dw_w, v_conv_dw_b, v_conv_ln_g, v_conv_ln_b, v_conv_w_pw, v_ssm_a_re, v_ssm_a_im, v_ssm_log_dt, v_ssm_b_re, v_ssm_b_im, v_ssm_c_re, v_ssm_c_im, v_ssm_d, v_ssm_w_glu, v_w_out, v_xa_norm_g, v_mem_norm_g, v_xa_w_q, v_xa_w_kv, v_xa_w_o, v_ffn_norm_g, v_ffn_w_up, v_ffn_dw_w, v_ffn_w_down, v_final_norm_g)
    return _step(dict(zip(_FWD, args)), loss_target, dict(zip(_WEIGHTS, ms)), dict(zip(_WEIGHTS, vs)))
```

```python
import functools

import jax
import jax.numpy as jnp
from jax import lax
from jax.experimental import pallas as pl
from jax.experimental.pallas import tpu as pltpu

F32 = jnp.float32
BF16 = jnp.bfloat16
MESH_ID = pl.DeviceIdType.MESH
N_DEV = 8
EPS = 1e-6
VMEM_LIMIT = 48 * 1024 * 1024
ANY = pl.BlockSpec(memory_space=pl.ANY)

ADAM_LR = 0.001
ADAM_B1 = 0.9
ADAM_B2 = 0.999
ADAM_EPS = 1e-08
ADAM_WD = 0.01
ADAM_STEP = 10

CONV_K = 31
FFN_K = 3
XA_HEADS = 4
SSM_GROUP = 16
SSM_STATE = 64
HALO = 32
LANE = 128
SSM_LANES = 512


def _pick(n, prefs):
    for p in prefs:
        if p <= n and n % p == 0:
            return p
    return n


def _params(sem, vmem=VMEM_LIMIT):
    return pltpu.CompilerParams(dimension_semantics=sem, vmem_limit_bytes=vmem)


def _sigmoid(x):
    return 1.0 / (1.0 + jnp.exp(-x))


def _silu(x):
    return x * _sigmoid(x)


def _gelu(x):
    return 0.5 * x * (1.0 + jnp.tanh(0.7978845608028654 * (x + 0.044715 * (x * x * x))))


def _rms(x, g):
    return x * lax.rsqrt(jnp.mean(x * x, axis=-1, keepdims=True) + EPS) * g


def _convpost(hc, bias, ln_g, ln_b):
    h = hc + bias
    mu = jnp.mean(h, axis=-1, keepdims=True)
    xc = h - mu
    y = xc * lax.rsqrt(jnp.mean(xc * xc, axis=-1, keepdims=True) + EPS)
    return _silu(y * ln_g + ln_b)


def _mixf(gla, glb, ya, ga, gb):
    return _sigmoid(gla) * ya + _sigmoid(glb) * (ga * _sigmoid(gb))


class _W:
    def __init__(self, arr, layer, blocked):
        self.arr, self.layer, self.blocked = arr, layer, blocked
        if blocked:
            _, _, self.K, self.nb = arr.shape
            self.N = N_DEV * self.nb
        else:
            _, self.K, self.N = arr.shape
            self.nb = self.N

    def spec(self, tk, tn, ki, ni):
        l = self.layer
        if self.blocked:
            per = self.nb // tn
            return pl.BlockSpec((None, None, tk, tn), lambda *g: (ni(*g) // per, l, ki(*g), ni(*g) % per))
        return pl.BlockSpec((None, tk, tn), lambda *g: (l, ki(*g), ni(*g)))


_M_TILES = (1024, 512, 256, 128, 64, 32, 16, 8)
_N_TILES = (1408, 1024, 896, 512, 256, 128)
_K_TILES = (512, 1408, 896, 256, 128)


def _mm_nn(name, a, w, out_dtype, add=None):
    M, K = a.shape
    assert K == w.K
    tm, tk, tn = _pick(M, _M_TILES), _pick(K, _K_TILES), _pick(w.nb, _N_TILES)
    nk = K // tk

    def body(*refs):
        if add is None:
            a_ref, w_ref, o_ref, acc = refs
        else:
            a_ref, w_ref, r_ref, o_ref, acc = refs
        k = pl.program_id(2)

        @pl.when(k == 0)
        def _():
            acc[...] = jnp.zeros_like(acc)

        acc[...] += jnp.dot(a_ref[...].astype(BF16), w_ref[...], preferred_element_type=F32)

        @pl.when(k == nk - 1)
        def _():
            res = acc[...]
            if add is not None:
                res = res + r_ref[...]
            o_ref[...] = res.astype(o_ref.dtype)

    in_specs = [pl.BlockSpec((tm, tk), lambda i, j, k: (i, k)),
                w.spec(tk, tn, lambda i, j, k: k, lambda i, j, k: j)]
    args = [a, w.arr]
    if add is not None:
        in_specs.append(pl.BlockSpec((tm, tn), lambda i, j, k: (i, j)))
        args.append(add)
    return pl.pallas_call(
        body, name=name, grid=(M // tm, w.N // tn, nk), in_specs=in_specs,
        out_specs=pl.BlockSpec((tm, tn), lambda i, j, k: (i, j)),
        out_shape=jax.ShapeDtypeStruct((M, w.N), out_dtype),
        scratch_shapes=[pltpu.VMEM((tm, tn), F32)],
        compiler_params=_params(("parallel", "parallel", "arbitrary")),
    )(*args)


def _mm_nt(name, a, w, out_dtype):
    M, N = a.shape
    assert N == w.N
    tm, tkk, tnn = _pick(M, _M_TILES), _pick(w.K, _N_TILES), _pick(w.nb, _K_TILES)
    nn = N // tnn

    def body(a_ref, w_ref, o_ref, acc):
        n = pl.program_id(2)

        @pl.when(n == 0)
        def _():
            acc[...] = jnp.zeros_like(acc)

        acc[...] += lax.dot_general(a_ref[...].astype(BF16), w_ref[...], (((1,), (1,)), ((), ())),
                                    preferred_element_type=F32)

        @pl.when(n == nn - 1)
        def _():
            o_ref[...] = acc[...].astype(o_ref.dtype)

    return pl.pallas_call(
        body, name=name, grid=(M // tm, w.K // tkk, nn),
        in_specs=[pl.BlockSpec((tm, tnn), lambda i, j, n: (i, n)),
                  w.spec(tkk, tnn, lambda i, j, n: j, lambda i, j, n: n)],
        out_specs=pl.BlockSpec((tm, tkk), lambda i, j, n: (i, j)),
        out_shape=jax.ShapeDtypeStruct((M, w.K), out_dtype),
        scratch_shapes=[pltpu.VMEM((tm, tkk), F32)],
        compiler_params=_params(("parallel", "parallel", "arbitrary")),
    )(a, w.arr)


def _mm_tn(name, a, b, nb=None):
    T, K = a.shape
    _, N = b.shape
    width = N if nb is None else nb
    tkk, tn, tt = _pick(K, (1024, 512, 256, 128)), _pick(width, _N_TILES), _pick(T, (512, 256, 128, 64, 32, 16))
    nt = T // tt
    per = width // tn

    def body(a_ref, b_ref, o_ref, acc):
        t = pl.program_id(2)

        @pl.when(t == 0)
        def _():
            acc[...] = jnp.zeros_like(acc)

        acc[...] += lax.dot_general(a_ref[...].astype(BF16), b_ref[...].astype(BF16), (((0,), (0,)), ((), ())),
                                    preferred_element_type=F32)

        @pl.when(t == nt - 1)
        def _():
            o_ref[...] = acc[...].astype(o_ref.dtype)

    if nb is None:
        out_spec = pl.BlockSpec((tkk, tn), lambda i, j, t: (i, j))
        out_shape = jax.ShapeDtypeStruct((K, N), BF16)
    else:
        out_spec = pl.BlockSpec((None, tkk, tn), lambda i, j, t: (j // per, i, j % per))
        out_shape = jax.ShapeDtypeStruct((N_DEV, K, nb), BF16)
    return pl.pallas_call(
        body, name=name, grid=(K // tkk, N // tn, nt),
        in_specs=[pl.BlockSpec((tt, tkk), lambda i, j, t: (t, i)),
                  pl.BlockSpec((tt, tn), lambda i, j, t: (t, j))],
        out_specs=out_spec, out_shape=out_shape,
        scratch_shapes=[pltpu.VMEM((tkk, tn), F32)],
        compiler_params=_params(("parallel", "parallel", "arbitrary")),
    )(a, b)


def _rowwise(name, fn, rows, consts, outs, accs=(), tile=256, ncol=1, nrows=None):
    n_r, n_c, n_o = len(rows), len(consts), len(outs)
    T = rows[0][0].shape[0] if nrows is None else nrows
    tile = _pick(T, tuple(t for t in (512, 256, 128, 64, 32, 16, 8) if t <= tile))
    nt = T // tile

    def body(*refs):
        vals = [r[...] for r in refs[:n_r + n_c]]
        res = fn(*vals)
        if not isinstance(res, (tuple, list)):
            res = (res,)
        o_refs = refs[n_r + n_c:n_r + n_c + n_o]
        a_refs = refs[n_r + n_c + n_o:]
        for r, v in zip(o_refs, res[:n_o]):
            r[...] = v.astype(r.dtype)
        first = pl.program_id(1) == 0
        for r, v in zip(a_refs, res[n_o:]):
            @pl.when(first)
            def _(r=r, v=v):
                r[...] = v.astype(F32)

            @pl.when(jnp.logical_not(first))
            def _(r=r, v=v):
                r[...] += v.astype(F32)

    in_specs, args = [], []
    for arr, w, off, roff in rows:
        rb = roff // tile
        assert roff % tile == 0
        in_specs.append(pl.BlockSpec((tile, w), lambda j, i, off=off, rb=rb: (i + rb, off + j)))
        args.append(arr)
    for cst in consts:
        in_specs.append(pl.BlockSpec(cst.shape, lambda j, i: (0, 0)))
        args.append(cst)
    out_specs, out_shape = [], []
    for tw, dt in outs:
        out_specs.append(pl.BlockSpec((tile, tw // ncol), lambda j, i: (i, j)))
        out_shape.append(jax.ShapeDtypeStruct((T, tw), dt))
    for nr, tw in accs:
        out_specs.append(pl.BlockSpec((nr, tw // ncol), lambda j, i: (0, j)))
        out_shape.append(jax.ShapeDtypeStruct((nr, tw), F32))
    res = pl.pallas_call(
        body, name=name, grid=(ncol, nt), in_specs=in_specs, out_specs=out_specs, out_shape=out_shape,
        compiler_params=_params(("parallel", "arbitrary")),
    )(*args)
    return res


def _rms_fwd(name, x, g):
    D = x.shape[1]
    return _rowwise(name, lambda xv, gv: _rms(xv, gv), [(x, D, 0, 0)], [g], [(D, BF16)])[0]


def _rms_bwd(name, x, g, dh, dx_in):
    D = x.shape[1]

    def fn(xv, dhv, dxv, gv):
        _, vjp = jax.vjp(_rms, xv, gv)
        dx, dg = vjp(dhv.astype(F32))
        tot = dx + dxv
        return tot, tot, jnp.sum(dg, axis=0, keepdims=True)

    return _rowwise(name, fn, [(x, D, 0, 0), (dh, D, 0, 0), (dx_in, D, 0, 0)], [g], [(D, F32), (D, BF16)], [(1, D)],
                    tile=256)


def _lag_views(win, K, R, forward):
    n = win.shape[0]
    for r in range(8):
        if r >= K:
            break
        if r == 0:
            rolled = win
        else:
            rolled = pltpu.roll(win, (n - r) if forward else r, axis=0)
        for q in range((K - 1 - r) // 8 + 1):
            s = 8 * q + r
            if forward:
                yield s, rolled[8 * q:8 * q + R]
            else:
                yield s, rolled[HALO - 8 * q:HALO - 8 * q + R]


def _conv_chunk(win, w_ref, K, R):
    acc = None
    for s, view in _lag_views(win, K, R, forward=False):
        term = w_ref[K - 1 - s:K - s, :] * view
        acc = term if acc is None else acc + term
    return acc


def _conv_chunk_t(win, w_ref, K, R):
    acc = None
    for s, view in _lag_views(win, K, R, forward=True):
        term = w_ref[K - 1 - s:K - s, :] * view
        acc = term if acc is None else acc + term
    return acc


def _conv_dw(xwin, dy, K, R):
    taps = [None] * K
    for s, view in _lag_views(xwin, K, R, forward=False):
        taps[K - 1 - s] = jnp.sum(dy * view, axis=0, keepdims=True)
    return taps


def _chunks(T):
    R = _pick(T, (128, 64, 32))
    return R, T // R


def _glu_conv_fwd(name, proj, w, cw_total):
    T = proj.shape[0]
    C = cw_total
    cw = LANE
    nb = C // cw
    R, nch = _chunks(T)

    def body(a_ref, b_ref, w_ref, o_ref, s_ref):
        s_ref[0:HALO, :] = jnp.zeros((HALO, cw), F32)

        def fill(i, _):
            r0 = pl.multiple_of(i * R, R)
            s_ref[pl.ds(HALO + r0, R), :] = a_ref[pl.ds(r0, R), :] * _sigmoid(b_ref[pl.ds(r0, R), :])
            return 0

        lax.fori_loop(0, nch, fill, 0)

        def conv(i, _):
            r0 = pl.multiple_of(i * R, R)
            o_ref[pl.ds(r0, R), :] = _conv_chunk(s_ref[pl.ds(r0, R + HALO), :], w_ref, CONV_K, R)
            return 0

        lax.fori_loop(0, nch, conv, 0)

    return pl.pallas_call(
        body, name=name, grid=(nb,),
        in_specs=[pl.BlockSpec((T, cw), lambda j: (0, j)), pl.BlockSpec((T, cw), lambda j: (0, nb + j)),
                  pl.BlockSpec((HALO, cw), lambda j: (0, j))],
        out_specs=pl.BlockSpec((T, cw), lambda j: (0, j)),
        out_shape=jax.ShapeDtypeStruct((T, C), F32),
        scratch_shapes=[pltpu.VMEM((T + HALO, cw), F32)],
        compiler_params=_params(("parallel",)),
    )(proj, proj, w)


def _glu_conv_bwd(name, proj, w, dhc, cw_total):
    T = proj.shape[0]
    C = cw_total
    cw = LANE
    nb = C // cw
    R, nch = _chunks(T)

    def body(a_ref, b_ref, w_ref, dy_ref, da_ref, db_ref, dw_ref, s_ref, g_ref, acc_ref):
        s_ref[0:HALO, :] = jnp.zeros((HALO, cw), F32)
        g_ref[T:T + HALO, :] = jnp.zeros((HALO, cw), F32)
        acc_ref[...] = jnp.zeros_like(acc_ref)

        def fill(i, _):
            r0 = pl.multiple_of(i * R, R)
            s_ref[pl.ds(HALO + r0, R), :] = a_ref[pl.ds(r0, R), :] * _sigmoid(b_ref[pl.ds(r0, R), :])
            g_ref[pl.ds(r0, R), :] = dy_ref[pl.ds(r0, R), :]
            return 0

        lax.fori_loop(0, nch, fill, 0)

        def back(i, _):
            r0 = pl.multiple_of(i * R, R)
            dhg = _conv_chunk_t(g_ref[pl.ds(r0, R + HALO), :], w_ref, CONV_K, R)
            av = a_ref[pl.ds(r0, R), :]
            sg = _sigmoid(b_ref[pl.ds(r0, R), :])
            da_ref[pl.ds(r0, R), :] = (dhg * sg).astype(da_ref.dtype)
            db_ref[pl.ds(r0, R), :] = (dhg * av * sg * (1.0 - sg)).astype(db_ref.dtype)
            taps = _conv_dw(s_ref[pl.ds(r0, R + HALO), :], dy_ref[pl.ds(r0, R), :], CONV_K, R)
            for k, tap in enumerate(taps):
                acc_ref[k:k + 1, :] += tap
            return 0

        lax.fori_loop(0, nch, back, 0)
        dw_ref[...] = acc_ref[...]

    return pl.pallas_call(
        body, name=name, grid=(nb,),
        in_specs=[pl.BlockSpec((T, cw), lambda j: (0, j)), pl.BlockSpec((T, cw), lambda j: (0, nb + j)),
                  pl.BlockSpec((HALO, cw), lambda j: (0, j)), pl.BlockSpec((T, cw), lambda j: (0, j))],
        out_specs=[pl.BlockSpec((T, cw), lambda j: (0, j)), pl.BlockSpec((T, cw), lambda j: (0, j)),
                   pl.BlockSpec((HALO, cw), lambda j: (0, j))],
        out_shape=[jax.ShapeDtypeStruct((T, C), BF16), jax.ShapeDtypeStruct((T, C), BF16),
                   jax.ShapeDtypeStruct((HALO, C), F32)],
        scratch_shapes=[pltpu.VMEM((T + HALO, cw), F32), pltpu.VMEM((T + HALO, cw), F32),
                        pltpu.VMEM((HALO, cw), F32)],
        compiler_params=_params(("parallel",)),
    )(proj, proj, w, dhc)


def _ffn_conv_fwd(name, up, w, dff):
    T = up.shape[0]
    cw = LANE
    nb = dff // cw
    R, nch = _chunks(T)

    def body(g_ref, v_ref, wg_ref, wv_ref, o_ref, sg_ref, sv_ref):
        sg_ref[0:HALO, :] = jnp.zeros((HALO, cw), F32)
        sv_ref[0:HALO, :] = jnp.zeros((HALO, cw), F32)

        def fill(i, _):
            r0 = pl.multiple_of(i * R, R)
            sg_ref[pl.ds(HALO + r0, R), :] = g_ref[pl.ds(r0, R), :]
            sv_ref[pl.ds(HALO + r0, R), :] = v_ref[pl.ds(r0, R), :]
            return 0

        lax.fori_loop(0, nch, fill, 0)

        def conv(i, _):
            r0 = pl.multiple_of(i * R, R)
            gc = _conv_chunk(sg_ref[pl.ds(r0, R + HALO), :], wg_ref, FFN_K, R)
            vc = _conv_chunk(sv_ref[pl.ds(r0, R + HALO), :], wv_ref, FFN_K, R)
            o_ref[pl.ds(r0, R), :] = (_silu(gc) * vc).astype(o_ref.dtype)
            return 0

        lax.fori_loop(0, nch, conv, 0)

    return pl.pallas_call(
        body, name=name, grid=(nb,),
        in_specs=[pl.BlockSpec((T, cw), lambda j: (0, j)), pl.BlockSpec((T, cw), lambda j: (0, nb + j)),
                  pl.BlockSpec((8, cw), lambda j: (0, j)), pl.BlockSpec((8, cw), lambda j: (0, nb + j))],
        out_specs=pl.BlockSpec((T, cw), lambda j: (0, j)),
        out_shape=jax.ShapeDtypeStruct((T, dff), BF16),
        scratch_shapes=[pltpu.VMEM((T + HALO, cw), F32), pltpu.VMEM((T + HALO, cw), F32)],
        compiler_params=_params(("parallel",)),
    )(up, up, w, w)


def _ffn_conv_bwd(name, up, w, dact, dff):
    T = up.shape[0]
    cw = LANE
    nb = dff // cw
    R, nch = _chunks(T)

    def body(g_ref, v_ref, wg_ref, wv_ref, da_ref, dg_ref, dv_ref, dwg_ref, dwv_ref,
             sg_ref, sv_ref, tg_ref, tv_ref, ag_ref, av_ref):
        zero = jnp.zeros((HALO, cw), F32)
        sg_ref[0:HALO, :] = zero
        sv_ref[0:HALO, :] = zero
        tg_ref[T:T + HALO, :] = zero
        tv_ref[T:T + HALO, :] = zero
        ag_ref[...] = jnp.zeros_like(ag_ref)
        av_ref[...] = jnp.zeros_like(av_ref)

        def fill(i, _):
            r0 = pl.multiple_of(i * R, R)
            sg_ref[pl.ds(HALO + r0, R), :] = g_ref[pl.ds(r0, R), :]
            sv_ref[pl.ds(HALO + r0, R), :] = v_ref[pl.ds(r0, R), :]
            return 0

        lax.fori_loop(0, nch, fill, 0)

        def grads(i, _):
            r0 = pl.multiple_of(i * R, R)
            gwin = sg_ref[pl.ds(r0, R + HALO), :]
            vwin = sv_ref[pl.ds(r0, R + HALO), :]
            gc = _conv_chunk(gwin, wg_ref, FFN_K, R)
            vc = _conv_chunk(vwin, wv_ref, FFN_K, R)
            da = da_ref[pl.ds(r0, R), :].astype(F32)
            sg = _sigmoid(gc)
            dgc = da * vc * (sg * (1.0 + gc * (1.0 - sg)))
            dvc = da * (gc * sg)
            tg_ref[pl.ds(r0, R), :] = dgc
            tv_ref[pl.ds(r0, R), :] = dvc
            for k, tap in enumerate(_conv_dw(gwin, dgc, FFN_K, R)):
                ag_ref[k:k + 1, :] += tap
            for k, tap in enumerate(_conv_dw(vwin, dvc, FFN_K, R)):
                av_ref[k:k + 1, :] += tap
            return 0

        lax.fori_loop(0, nch, grads, 0)

        def back(i, _):
            r0 = pl.multiple_of(i * R, R)
            dg_ref[pl.ds(r0, R), :] = _conv_chunk_t(tg_ref[pl.ds(r0, R + HALO), :], wg_ref, FFN_K, R).astype(dg_ref.dtype)
            dv_ref[pl.ds(r0, R), :] = _conv_chunk_t(tv_ref[pl.ds(r0, R + HALO), :], wv_ref, FFN_K, R).astype(dv_ref.dtype)
            return 0

        lax.fori_loop(0, nch, back, 0)
        dwg_ref[...] = ag_ref[...]
        dwv_ref[...] = av_ref[...]

    col = lambda j: (0, j)
    dg, dv, dwg, dwv = pl.pallas_call(
        body, name=name, grid=(nb,),
        in_specs=[pl.BlockSpec((T, cw), col), pl.BlockSpec((T, cw), lambda j: (0, nb + j)),
                  pl.BlockSpec((8, cw), col), pl.BlockSpec((8, cw), lambda j: (0, nb + j)),
                  pl.BlockSpec((T, cw), col)],
        out_specs=[pl.BlockSpec((T, cw), col), pl.BlockSpec((T, cw), col),
                   pl.BlockSpec((8, cw), col), pl.BlockSpec((8, cw), col)],
        out_shape=[jax.ShapeDtypeStruct((T, dff), BF16), jax.ShapeDtypeStruct((T, dff), BF16),
                   jax.ShapeDtypeStruct((8, dff), F32), jax.ShapeDtypeStruct((8, dff), F32)],
        scratch_shapes=[pltpu.VMEM((T + HALO, cw), F32), pltpu.VMEM((T + HALO, cw), F32),
                        pltpu.VMEM((T + HALO, cw), F32), pltpu.VMEM((T + HALO, cw), F32),
                        pltpu.VMEM((8, cw), F32), pltpu.VMEM((8, cw), F32)],
        compiler_params=_params(("parallel",)),
    )(up, up, w, w, dact)
    return jnp.concatenate([dg, dv], axis=1), jnp.concatenate([dwg, dwv], axis=1)


def _zoh(a_re, a_im, log_dt):
    ar = jnp.minimum(a_re, -1e-4)
    ai = a_im
    dt = jnp.exp(log_dt)
    mag = jnp.exp(dt * ar)
    abar_re = mag * jnp.cos(dt * ai)
    abar_im = mag * jnp.sin(dt * ai)
    den = ar * ar + ai * ai
    nr = abar_re - 1.0
    ni = abar_im
    return abar_re, abar_im, (nr * ar + ni * ai) / den, (ni * ar - nr * ai) / den


def _discretize(a_re, a_im, log_dt, a_re_h, a_im_h, log_dt_h, b_re, b_im):
    abar_re, abar_im, _, _ = _zoh(a_re, a_im, log_dt)
    _, _, z_re, z_im = _zoh(a_re_h, a_im_h, log_dt_h)
    return abar_re, abar_im, z_re * b_re - z_im * b_im, z_re * b_im + z_im * b_re


def _full_specs(arrs):
    return [pl.BlockSpec(a.shape, lambda *_, n=len(a.shape): (0,) * n) for a in arrs]


def _ssm_prep(name, raw):
    def body(*refs):
        res = _discretize(*[r[...] for r in refs[:8]])
        for r, v in zip(refs[8:], res):
            r[...] = v

    outs = [jax.ShapeDtypeStruct(raw[0].shape, F32)] * 2 + [jax.ShapeDtypeStruct(raw[6].shape, F32)] * 2
    return pl.pallas_call(body, name=name, in_specs=_full_specs(raw), out_specs=_full_specs(outs), out_shape=outs)(*raw)


def _ssm_prep_bwd(name, raw, cots):
    G = raw[0].shape[0]
    H = raw[3].shape[0] // G

    def body(*refs):
        _, vjp = jax.vjp(_discretize, *[r[...] for r in refs[:8]])
        g = vjp(tuple(r[...] for r in refs[8:12]))
        outs = refs[12:]
        for k in range(3):
            rep = g[3 + k]
            outs[k][...] = g[k] + jnp.sum(rep.reshape(G, H, rep.shape[1]), axis=1)
        outs[3][...] = g[6]
        outs[4][...] = g[7]

    outs = [jax.ShapeDtypeStruct(a.shape, F32) for a in (raw[0], raw[1], raw[2], raw[6], raw[7])]
    return pl.pallas_call(body, name=name, in_specs=_full_specs(list(raw) + list(cots)), out_specs=_full_specs(outs),
                          out_shape=outs)(*raw, *cots)


def _cmul(ar, ai, br, bi):
    return ar * br - ai * bi, ar * bi + ai * br


def _scan_coefs(ar, ai, reverse):
    W = ar.shape[1]
    row = lax.broadcasted_iota(jnp.int32, (8, W), 0)
    p = [None] * 9
    p[1] = (ar, ai)
    for n in range(2, 9):
        p[n] = _cmul(*p[n // 2], *p[n - n // 2])
    steps = []
    for s in (1, 2, 4):
        valid = (row <= 7 - s) if reverse else (row >= s)
        steps.append((jnp.where(valid, p[s][0], 0.0), jnp.where(valid, p[s][1], 0.0)))
    pr = jnp.zeros((8, W), F32)
    pi = jnp.zeros((8, W), F32)
    for i in range(8):
        n = (8 - i) if reverse else (i + 1)
        pr = jnp.where(row == i, p[n][0], pr)
        pi = jnp.where(row == i, p[n][1], pi)
    return steps, (pr, pi)


def _scan_tile(xr, xi, cr, ci, coefs, reverse):
    steps, (pr, pi) = coefs
    for s, (sr, si) in zip((1, 2, 4), steps):
        shift = (8 - s) if reverse else s
        rr = pltpu.roll(xr, shift, axis=0)
        ri = pltpu.roll(xi, shift, axis=0)
        xr, xi = xr + sr * rr - si * ri, xi + sr * ri + si * rr
    xr, xi = xr + pr * cr - pi * ci, xi + pr * ci + pi * cr
    return xr, xi


def _edge_rows(x, reverse):
    W = x.shape[1]
    return jnp.broadcast_to(x[0:1, :] if reverse else x[7:8, :], (8, W))


def _ssm_chunk(T):
    return _pick(T, (256, 128, 64))


def _ssm_fwd(name, proj, u_off, p, width):
    T = proj.shape[0]
    NB = width // LANE
    Q = _ssm_chunk(T)
    nch = T // Q
    W = SSM_LANES

    def body(u_ref, bre, bim, cre, cim, ar_ref, ai_ref, d_ref, y_ref, ckr_ref, cki_ref, br_s, bi_s, car_r, car_i):
        c = pl.program_id(1)

        @pl.when(c == 0)
        def _():
            car_r[...] = jnp.zeros_like(car_r)
            car_i[...] = jnp.zeros_like(car_i)

        ckr_ref[...] = car_r[...]
        cki_ref[...] = car_i[...]
        u = u_ref[...]
        u16 = u.astype(BF16)
        br_s[...] = jnp.dot(u16, bre[...], preferred_element_type=F32)
        bi_s[...] = jnp.dot(u16, bim[...], preferred_element_type=F32)
        coefs = _scan_coefs(ar_ref[...], ai_ref[...], False)

        def tile(j, carry):
            r0 = pl.multiple_of(j * 8, 8)
            xr, xi = _scan_tile(br_s[pl.ds(r0, 8), :], bi_s[pl.ds(r0, 8), :], carry[0], carry[1], coefs, False)
            br_s[pl.ds(r0, 8), :] = xr
            bi_s[pl.ds(r0, 8), :] = xi
            return _edge_rows(xr, False), _edge_rows(xi, False)

        cr, ci = lax.fori_loop(0, Q // 8, tile, (car_r[...], car_i[...]))
        car_r[...] = cr
        car_i[...] = ci
        nt = (((1,), (1,)), ((), ()))
        y = (lax.dot_general(br_s[...].astype(BF16), cre[...], nt, preferred_element_type=F32)
             - lax.dot_general(bi_s[...].astype(BF16), cim[...], nt, preferred_element_type=F32)
             + d_ref[...] * u)
        y_ref[...] = _gelu(y).astype(y_ref.dtype)

    blk = lambda b, c: (b, 0, 0)
    mat = pl.BlockSpec((None, LANE, W), blk)
    vec = pl.BlockSpec((None, 1, W), blk)
    ck = pl.BlockSpec((None, None, 8, W), lambda b, c: (b, c, 0, 0))
    return pl.pallas_call(
        body, name=name, grid=(NB, nch),
        in_specs=[pl.BlockSpec((Q, LANE), lambda b, c: (c, u_off + b)), mat, mat, mat, mat, vec, vec,
                  pl.BlockSpec((1, LANE), lambda b, c: (0, b))],
        out_specs=[pl.BlockSpec((Q, LANE), lambda b, c: (c, b)), ck, ck],
        out_shape=[jax.ShapeDtypeStruct((T, width), BF16), jax.ShapeDtypeStruct((NB, nch, 8, W), F32),
                   jax.ShapeDtypeStruct((NB, nch, 8, W), F32)],
        scratch_shapes=[pltpu.VMEM((Q, W), F32), pltpu.VMEM((Q, W), F32), pltpu.VMEM((8, W), F32),
                        pltpu.VMEM((8, W), F32)],
        compiler_params=_params(("parallel", "arbitrary")),
    )(proj, p["bre"], p["bim"], p["cre"], p["cim"], p["ar"], p["ai"], p["d"])


def _ssm_bwd(name, proj, u_off, p, ck_r, ck_i, dyg, width):
    T = proj.shape[0]
    NB = width // LANE
    Q = _ssm_chunk(T)
    nch = T // Q
    W = SSM_LANES
    nt_dims = (((1,), (1,)), ((), ()))
    tn_dims = (((0,), (0,)), ((), ()))

    def body(u_ref, dy_ref, ckr_ref, cki_ref, bre, bim, cre, cim, ar_ref, ai_ref, d_ref,
             du_ref, dbr_ref, dbi_ref, dcr_ref, dci_ref, dar_ref, dai_ref, dd_ref,
             xr_s, xi_s, lr_s, li_s, lam_r, lam_i):
        c = pl.program_id(1)

        @pl.when(c == 0)
        def _():
            lam_r[...] = jnp.zeros_like(lam_r)
            lam_i[...] = jnp.zeros_like(lam_i)
            for r in (dbr_ref, dbi_ref, dcr_ref, dci_ref, dar_ref, dai_ref, dd_ref):
                r[...] = jnp.zeros_like(r)

        u = u_ref[...]
        u16 = u.astype(BF16)
        ar, ai = ar_ref[...], ai_ref[...]
        xr_s[0:8, :] = ckr_ref[...]
        xi_s[0:8, :] = cki_ref[...]
        xr_s[8:Q + 8, :] = jnp.dot(u16, bre[...], preferred_element_type=F32)
        xi_s[8:Q + 8, :] = jnp.dot(u16, bim[...], preferred_element_type=F32)
        fcoefs = _scan_coefs(ar, ai, False)

        def ftile(j, carry):
            r0 = pl.multiple_of(j * 8 + 8, 8)
            xr, xi = _scan_tile(xr_s[pl.ds(r0, 8), :], xi_s[pl.ds(r0, 8), :], carry[0], carry[1], fcoefs, False)
            xr_s[pl.ds(r0, 8), :] = xr
            xi_s[pl.ds(r0, 8), :] = xi
            return _edge_rows(xr, False), _edge_rows(xi, False)

        lax.fori_loop(0, Q // 8, ftile, (ckr_ref[...], cki_ref[...]))
        xr16 = xr_s[8:Q + 8, :].astype(BF16)
        xi16 = xi_s[8:Q + 8, :].astype(BF16)
        y = (lax.dot_general(xr16, cre[...], nt_dims, preferred_element_type=F32)
             - lax.dot_general(xi16, cim[...], nt_dims, preferred_element_type=F32) + d_ref[...] * u)
        _, gelu_vjp = jax.vjp(_gelu, y)
        dy = gelu_vjp(dy_ref[...].astype(F32))[0]
        dy16 = dy.astype(BF16)
        dd_ref[...] += jnp.broadcast_to(jnp.sum(dy * u, axis=0, keepdims=True), (8, LANE))
        dcr_ref[...] += lax.dot_general(dy16, xr16, tn_dims, preferred_element_type=F32)
        dci_ref[...] -= lax.dot_general(dy16, xi16, tn_dims, preferred_element_type=F32)
        lr_s[...] = jnp.dot(dy16, cre[...], preferred_element_type=F32)
        li_s[...] = -jnp.dot(dy16, cim[...], preferred_element_type=F32)
        rcoefs = _scan_coefs(ar, -ai, True)
        row = lax.broadcasted_iota(jnp.int32, (8, W), 0)

        def rtile(jj, carry):
            j = Q // 8 - 1 - jj
            r0 = pl.multiple_of(j * 8, 8)
            lr, li = _scan_tile(lr_s[pl.ds(r0, 8), :], li_s[pl.ds(r0, 8), :], carry[0], carry[1], rcoefs, True)
            lr_s[pl.ds(r0, 8), :] = lr
            li_s[pl.ds(r0, 8), :] = li
            cur_r, cur_i = xr_s[pl.ds(r0 + 8, 8), :], xi_s[pl.ds(r0 + 8, 8), :]
            prv_r, prv_i = xr_s[pl.ds(r0, 8), :], xi_s[pl.ds(r0, 8), :]
            xpr = jnp.where(row == 0, _edge_rows(prv_r, False), pltpu.roll(cur_r, 1, axis=0))
            xpi = jnp.where(row == 0, _edge_rows(prv_i, False), pltpu.roll(cur_i, 1, axis=0))
            return (_edge_rows(lr, True), _edge_rows(li, True),
                    carry[2] + lr * xpr + li * xpi, carry[3] + li * xpr - lr * xpi)

        zero = jnp.zeros((8, W), F32)
        cr, ci, sar, sai = lax.fori_loop(0, Q // 8, rtile, (lam_r[...], lam_i[...], zero, zero))
        lam_r[...] = cr
        lam_i[...] = ci
        dar_ref[...] += jnp.broadcast_to(jnp.sum(sar, axis=0, keepdims=True), (8, W))
        dai_ref[...] += jnp.broadcast_to(jnp.sum(sai, axis=0, keepdims=True), (8, W))
        lr16 = lr_s[...].astype(BF16)
        li16 = li_s[...].astype(BF16)
        dbr_ref[...] += lax.dot_general(u16, lr16, tn_dims, preferred_element_type=F32)
        dbi_ref[...] += lax.dot_general(u16, li16, tn_dims, preferred_element_type=F32)
        du = (lax.dot_general(lr16, bre[...], nt_dims, preferred_element_type=F32)
              + lax.dot_general(li16, bim[...], nt_dims, preferred_element_type=F32) + d_ref[...] * dy)
        du_ref[...] = du.astype(du_ref.dtype)

    blk = lambda b, c: (b, 0, 0)
    mat = pl.BlockSpec((None, LANE, W), blk)
    vec = pl.BlockSpec((None, 1, W), blk)
    acc8 = pl.BlockSpec((None, 8, W), blk)
    ck = pl.BlockSpec((None, None, 8, W), lambda b, c: (b, nch - 1 - c, 0, 0))
    return pl.pallas_call(
        body, name=name, grid=(NB, nch),
        in_specs=[pl.BlockSpec((Q, LANE), lambda b, c: (nch - 1 - c, u_off + b)),
                  pl.BlockSpec((Q, LANE), lambda b, c: (nch - 1 - c, b)), ck, ck, mat, mat, mat, mat, vec, vec,
                  pl.BlockSpec((1, LANE), lambda b, c: (0, b))],
        out_specs=[pl.BlockSpec((Q, LANE), lambda b, c: (nch - 1 - c, b)), mat, mat, mat, mat, acc8, acc8,
                   pl.BlockSpec((None, 8, LANE), blk)],
        out_shape=[jax.ShapeDtypeStruct((T, width), BF16)] + [jax.ShapeDtypeStruct((NB, LANE, W), F32)] * 4
                  + [jax.ShapeDtypeStruct((NB, 8, W), F32)] * 2 + [jax.ShapeDtypeStruct((NB, 8, LANE), F32)],
        scratch_shapes=[pltpu.VMEM((Q + 8, W), F32), pltpu.VMEM((Q + 8, W), F32), pltpu.VMEM((Q, W), F32),
                        pltpu.VMEM((Q, W), F32), pltpu.VMEM((8, W), F32), pltpu.VMEM((8, W), F32)],
        compiler_params=_params(("parallel", "arbitrary")),
    )(proj, dyg, ck_r, ck_i, p["bre"], p["bim"], p["cre"], p["cim"], p["ar"], p["ai"], p["d"])


def _block_diag(w):
    G, H, P = w.shape
    eye = jnp.eye(8, dtype=w.dtype)
    return (w.reshape(G // 8, 8, H, 1, P) * eye[None, :, None, :, None]).reshape(G // 8, 8 * H, 8 * P)


def _block_diag_t(d, H, P):
    NB = d.shape[0]
    d = d.reshape(NB, 8, H, 8, P)
    eye = jnp.eye(8, dtype=d.dtype)
    return jnp.sum(d * eye[None, :, None, :, None], axis=3).reshape(NB * 8, H, P)


def _attn_fwd(name, q, kv, heads):
    T, D = q.shape
    Mm = kv.shape[0]
    hd = D // heads
    tq = _pick(T, (512, 256, 128))
    scale = hd ** -0.5

    def body(q_ref, k_ref, v_ref, o_ref):
        s = lax.dot_general(q_ref[...], k_ref[...], (((1,), (1,)), ((), ())), preferred_element_type=F32) * scale
        s = s - jnp.max(s, axis=-1, keepdims=True)
        e = jnp.exp(s)
        p = e / jnp.sum(e, axis=-1, keepdims=True)
        o_ref[...] = jnp.dot(p.astype(BF16), v_ref[...], preferred_element_type=F32).astype(o_ref.dtype)

    return pl.pallas_call(
        body, name=name, grid=(heads, T // tq),
        in_specs=[pl.BlockSpec((tq, hd), lambda h, i: (i, h)), pl.BlockSpec((Mm, hd), lambda h, i: (0, h)),
                  pl.BlockSpec((Mm, hd), lambda h, i: (0, heads + h))],
        out_specs=pl.BlockSpec((tq, hd), lambda h, i: (i, h)),
        out_shape=jax.ShapeDtypeStruct((T, D), BF16),
        compiler_params=_params(("parallel", "parallel")),
    )(q, kv, kv)


def _attn_bwd(name, q, kv, do, heads):
    T, D = q.shape
    Mm = kv.shape[0]
    hd = D // heads
    tq = _pick(T, (512, 256, 128))
    scale = hd ** -0.5
    nt_dims = (((1,), (1,)), ((), ()))
    tn_dims = (((0,), (0,)), ((), ()))

    def body(q_ref, k_ref, v_ref, do_ref, dq_ref, dk_ref, dv_ref):
        i = pl.program_id(1)

        @pl.when(i == 0)
        def _():
            dk_ref[...] = jnp.zeros_like(dk_ref)
            dv_ref[...] = jnp.zeros_like(dv_ref)

        qv, kvl, vv, dov = q_ref[...], k_ref[...], v_ref[...], do_ref[...]
        s = lax.dot_general(qv, kvl, nt_dims, preferred_element_type=F32) * scale
        s = s - jnp.max(s, axis=-1, keepdims=True)
        e = jnp.exp(s)
        p = e / jnp.sum(e, axis=-1, keepdims=True)
        p16 = p.astype(BF16)
        dv_ref[...] += lax.dot_general(p16, dov, tn_dims, preferred_element_type=F32)
        dp = lax.dot_general(dov, vv, nt_dims, preferred_element_type=F32)
        ds = (p * (dp - jnp.sum(dp * p, axis=-1, keepdims=True)) * scale).astype(BF16)
        dq_ref[...] = jnp.dot(ds, kvl, preferred_element_type=F32).astype(dq_ref.dtype)
        dk_ref[...] += lax.dot_general(ds, qv, tn_dims, preferred_element_type=F32)

    return pl.pallas_call(
        body, name=name, grid=(heads, T // tq),
        in_specs=[pl.BlockSpec((tq, hd), lambda h, i: (i, h)), pl.BlockSpec((Mm, hd), lambda h, i: (0, h)),
                  pl.BlockSpec((Mm, hd), lambda h, i: (0, heads + h)), pl.BlockSpec((tq, hd), lambda h, i: (i, h))],
        out_specs=[pl.BlockSpec((tq, hd), lambda h, i: (i, h)), pl.BlockSpec((Mm, hd), lambda h, i: (0, h)),
                   pl.BlockSpec((Mm, hd), lambda h, i: (0, h))],
        out_shape=[jax.ShapeDtypeStruct((T, D), BF16), jax.ShapeDtypeStruct((Mm, D), F32),
                   jax.ShapeDtypeStruct((Mm, D), F32)],
        compiler_params=_params(("parallel", "arbitrary")),
    )(q, kv, kv, do)


def _position():
    return lax.axis_index("x"), lax.axis_index("y"), lax.axis_index("c")


def _allgather(name, blk, row_mode):
    L = blk.shape[0]
    out_shape = (L, N_DEV) + blk.shape[1:] if row_mode else (N_DEV,) + blk.shape

    def body(x_ref, out_ref, send_sems, recv_sems, local_sem):
        x, y, c = _position()
        me, sibling = (x, y, c), (x, y, 1 - c)
        chips = [(1 - x, y), (x, 1 - y), (1 - x, 1 - y)]

        def slot(px, py, pc):
            b = 4 * px + 2 * py + pc
            return out_ref.at[:, b] if row_mode else out_ref.at[b]

        def copy(k, block, to, src=None):
            return pltpu.make_async_remote_copy(
                src_ref=slot(*block) if src is None else src, dst_ref=slot(*block),
                send_sem=send_sems.at[k], recv_sem=recv_sems.at[k], device_id=to, device_id_type=MESH_ID)

        mine = pltpu.make_async_copy(x_ref, slot(*me), local_sem)
        mine.start()
        first = [copy(0, me, sibling, src=x_ref)]
        first += [copy(1 + j, me, (*chip, c), src=x_ref) for j, chip in enumerate(chips)]
        for cp in first:
            cp.start()
        passed = [copy(4 + j, (*chip, c), sibling) for j, chip in enumerate(chips)]
        for j, chip in enumerate(chips):
            copy(1 + j, (*chip, c), me).wait_recv()
            passed[j].start()
        copy(0, sibling, me).wait_recv()
        for j, chip in enumerate(chips):
            copy(4 + j, (*chip, 1 - c), me).wait_recv()
        for cp in first + passed:
            cp.wait_send()
        mine.wait()

    return pl.pallas_call(
        body, name=name, in_specs=[ANY], out_specs=ANY,
        out_shape=jax.ShapeDtypeStruct(out_shape, blk.dtype),
        scratch_shapes=[pltpu.SemaphoreType.DMA((7,)), pltpu.SemaphoreType.DMA((7,)), pltpu.SemaphoreType.DMA],
    )(blk)


def _rs_sibling(name, grads):
    L = len(grads)
    blk = grads[0].shape[1:]

    def body(*refs):
        g_refs, out_ref, send_sems, recv_sems = refs[:L], refs[L], refs[L + 1], refs[L + 2]
        x, y, c = _position()
        sibling = (x, y, 1 - c)
        copies = []
        for l in range(L):
            for q in range(4):
                k = 4 * l + q
                copies.append(pltpu.make_async_remote_copy(
                    src_ref=g_refs[l].at[2 * q + (1 - c)], dst_ref=out_ref.at[q, l],
                    send_sem=send_sems.at[k], recv_sem=recv_sems.at[k], device_id=sibling, device_id_type=MESH_ID))
        for cp in copies:
            cp.start()
        for cp in copies:
            cp.wait_recv()
        for cp in copies:
            cp.wait_send()

    return pl.pallas_call(
        body, name=name, in_specs=[ANY] * L, out_specs=ANY,
        out_shape=jax.ShapeDtypeStruct((4, L) + blk, grads[0].dtype),
        scratch_shapes=[pltpu.SemaphoreType.DMA((4 * L,)), pltpu.SemaphoreType.DMA((4 * L,))],
    )(*grads)


def _rs_chips(name, parts):
    L = len(parts)
    blk = parts[0].shape[1:]

    def body(*refs):
        p_refs, out_ref, send_sems, recv_sems, local_sems = refs[:L], refs[L], refs[L + 1], refs[L + 2], refs[L + 3]
        x, y, c = _position()
        flips = [(1 - x, y), (x, 1 - y), (1 - x, 1 - y)]
        local = [pltpu.make_async_copy(p_refs[l].at[2 * x + y], out_ref.at[3, l], local_sems.at[l]) for l in range(L)]
        copies = [pltpu.make_async_remote_copy(
            src_ref=p_refs[l].at[2 * fx + fy], dst_ref=out_ref.at[k, l], send_sem=send_sems.at[3 * l + k],
            recv_sem=recv_sems.at[3 * l + k], device_id=(fx, fy, c), device_id_type=MESH_ID)
            for l in range(L) for k, (fx, fy) in enumerate(flips)]
        for cp in copies + local:
            cp.start()
        for cp in copies:
            cp.wait_recv()
        for cp in copies:
            cp.wait_send()
        for cp in local:
            cp.wait()

    return pl.pallas_call(
        body, name=name, in_specs=[ANY] * L, out_specs=ANY,
        out_shape=jax.ShapeDtypeStruct((4, L) + blk, parts[0].dtype),
        scratch_shapes=[pltpu.SemaphoreType.DMA((3 * L,)), pltpu.SemaphoreType.DMA((3 * L,)),
                        pltpu.SemaphoreType.DMA((L,))],
    )(*parts)


def _adamw_math(g, w, m, v):
    m = ADAM_B1 * m + (1.0 - ADAM_B1) * g
    v = ADAM_B2 * v + (1.0 - ADAM_B2) * (g * g)
    m_hat = m / (1.0 - ADAM_B1 ** ADAM_STEP)
    v_hat = v / (1.0 - ADAM_B2 ** ADAM_STEP)
    delta = -ADAM_LR * (m_hat / (jnp.sqrt(v_hat) + ADAM_EPS) + ADAM_WD * w)
    return delta, m, v


def _sum_adamw(name, parts, nparts, w, m, v):
    R, C = w.shape
    tile = _ew_tile(R, C, nparts + 7)

    def fn(*vals):
        g = vals[0].astype(F32)
        for pv in vals[1:nparts]:
            g = g + pv.astype(F32)
        delta, nm, nv = _adamw_math(g, *vals[nparts:])
        return g, delta, nm, nv

    rows = [(parts, C, 0, k * R) for k in range(nparts)] + [(a, C, 0, 0) for a in (w, m, v)]
    return _rowwise(name, fn, rows, [], [(C, F32)] * 4, tile=tile, nrows=R)


def _ew_tile(R, C, nblocks):
    budget = (VMEM_LIMIT * 3) // 4
    return _pick(R, tuple(t for t in (1024, 512, 256, 128, 64, 32, 16, 8) if 8 * t * C * nblocks <= budget))


def _pair_sum(name, grads, landed, layer, c_idx):
    _, r, c = grads.shape
    tile = _ew_tile(r, c, 3)

    def body(c_ref, g_ref, s_ref, o_ref):
        o_ref[...] = (g_ref[...].astype(F32) + s_ref[...].astype(F32)).astype(o_ref.dtype)

    return pl.pallas_call(
        body, name=name,
        grid_spec=pltpu.PrefetchScalarGridSpec(
            num_scalar_prefetch=1, grid=(4, r // tile),
            in_specs=[pl.BlockSpec((None, tile, c), lambda q, i, c_ref: (2 * q + c_ref[0], i, 0)),
                      pl.BlockSpec((None, None, tile, c), lambda q, i, c_ref: (q, layer, i, 0))],
            out_specs=pl.BlockSpec((None, tile, c), lambda q, i, c_ref: (q, i, 0))),
        out_shape=jax.ShapeDtypeStruct((4, r, c), BF16),
        compiler_params=_params(("parallel", "parallel")),
    )(c_idx, grads, landed)


def _reduce_update(name, grads, w, m, v):
    L, r, c = w.shape
    c_idx = lax.axis_index("c").astype(jnp.int32).reshape(1)
    landed = _rs_sibling(name + "_rs_sibling", grads)
    parts = [_pair_sum(name + "_pair_sum", grads[l], landed, l, c_idx) for l in range(L)]
    got = _rs_chips(name + "_rs_chips", parts)
    outs = _sum_adamw(name + "_adamw", got.reshape(4 * L * r, c), 4, w.reshape(L * r, c), m.reshape(L * r, c),
                      v.reshape(L * r, c))
    return [o.reshape(L, r, c) for o in outs]


def _loss_head(name, x, g, target):
    T, D = x.shape

    def fn(xv, tv, gv):
        def f(xx, gg):
            err = _rms(xx, gg) - tv
            return 0.5 * jnp.sum(jnp.mean(err * err, axis=-1, keepdims=True))

        loss, (dx, dg) = jax.value_and_grad(f, argnums=(0, 1))(xv, gv)
        return dx, dx, jnp.full((8, LANE), loss, F32), jnp.sum(dg, axis=0, keepdims=True)

    dx, dx16, loss, dg = _rowwise(name, fn, [(x, D, 0, 0), (target, D, 0, 0)], [g], [(D, F32), (D, BF16)],
                                  [(8, LANE), (1, D)], tile=128)
    return loss[0, 0], dx, dx16, dg


_SHARDED_COL = ("w_in", "conv_dw_w", "conv_w_pw", "ssm_w_glu", "xa_w_kv", "ffn_w_up", "ffn_dw_w")
_SHARDED_ROW = ("w_out", "xa_w_q", "xa_w_o", "ffn_w_down")
_WEIGHTS = ['mix_norm_g', 'w_in', 'conv_dw_w', 'conv_dw_b', 'conv_ln_g', 'conv_ln_b', 'conv_w_pw', 'ssm_a_re',
            'ssm_a_im', 'ssm_log_dt', 'ssm_b_re', 'ssm_b_im', 'ssm_c_re', 'ssm_c_im', 'ssm_d', 'ssm_w_glu', 'w_out',
            'xa_norm_g', 'mem_norm_g', 'xa_w_q', 'xa_w_kv', 'xa_w_o', 'ffn_norm_g', 'ffn_w_up', 'ffn_dw_w',
            'ffn_w_down', 'final_norm_g']
_FWD = ['x', 'mem'] + _WEIGHTS


def _pad_rows(a, rows):
    return jnp.pad(a, ((0, 0), (0, rows - a.shape[1]), (0, 0)))


def _step(inp, target, mom_m, mom_v):
    x0 = inp["x"][0]
    mem = inp["mem"][0]
    T, D = x0.shape
    L = inp["w_in"].shape[0]
    CW = inp["conv_dw_b"].shape[1]
    SW = inp["ssm_d"].shape[1]
    DFF = inp["ffn_w_down"].shape[1] * N_DEV
    G = SW // SSM_GROUP
    NB = SW // LANE
    u_off = (2 * CW) // LANE
    gate_off = (2 * CW + SW) // 1024

    gathered = {}
    for n in _SHARDED_COL:
        small = n in ("conv_dw_w", "ffn_dw_w")
        blk = inp[n] if small else inp[n].astype(BF16)
        if n == "conv_dw_w":
            blk = _pad_rows(blk, HALO)
        if n == "ffn_dw_w":
            blk = _pad_rows(blk, 8)
        gathered[n] = _allgather("ag_" + n, blk, False)
    for n in _SHARDED_ROW:
        full = _allgather("ag_" + n, inp[n].astype(BF16), True)
        gathered[n] = full.reshape(L, N_DEV * full.shape[2], full.shape[3])

    def W(n, l):
        return _W(gathered[n], l, n in _SHARDED_COL)

    def dw_filter(n, l):
        g = gathered[n]
        return jnp.transpose(g[:, l], (1, 0, 2)).reshape(g.shape[2], N_DEV * g.shape[3])

    def row(n, l):
        return inp[n][l][None, :]

    ssm_raw, ssm_p = [], []
    for l in range(L):
        a_re, a_im, ldt = inp["ssm_a_re"][l], inp["ssm_a_im"][l], inp["ssm_log_dt"][l][:, None]
        rep = lambda a: jnp.repeat(a, SSM_GROUP, axis=0)
        flat = lambda b: jnp.transpose(b, (0, 2, 1)).reshape(G * SSM_GROUP, SSM_STATE)
        raw = (a_re, a_im, ldt, rep(a_re), rep(a_im), rep(ldt), flat(inp["ssm_b_re"][l]), flat(inp["ssm_b_im"][l]))
        abar_re, abar_im, bbar_re, bbar_im = _ssm_prep("ssm_prep", raw)
        bbar_re = bbar_re.reshape(G, SSM_GROUP, SSM_STATE)
        bbar_im = bbar_im.reshape(G, SSM_GROUP, SSM_STATE)
        ssm_raw.append(raw)
        ssm_p.append(dict(
            bre=_block_diag(bbar_re).astype(BF16), bim=_block_diag(bbar_im).astype(BF16),
            cre=_block_diag(inp["ssm_c_re"][l]).astype(BF16), cim=_block_diag(inp["ssm_c_im"][l]).astype(BF16),
            ar=abar_re.reshape(NB, 1, SSM_LANES), ai=abar_im.reshape(NB, 1, SSM_LANES), d=row("ssm_d", l)))

    saved = []
    x = x0
    for l in range(L):
        s = {"x_in": x}
        s["h1"] = _rms_fwd("rms_mix", x, row("mix_norm_g", l))
        s["proj"] = _mm_nn("mm_w_in", s["h1"], W("w_in", l), F32)
        s["hc"] = _glu_conv_fwd("conv_fwd", s["proj"], dw_filter("conv_dw_w", l), CW)
        s["hs"] = _rowwise("conv_post", _convpost, [(s["hc"], CW, 0, 0)],
                           [row("conv_dw_b", l), row("conv_ln_g", l), row("conv_ln_b", l)], [(CW, BF16)])[0]
        s["ya"] = _mm_nn("mm_w_pw", s["hs"], W("conv_w_pw", l), F32)
        s["yg"], s["ck_r"], s["ck_i"] = _ssm_fwd("ssm_fwd", s["proj"], u_off, ssm_p[l], SW)
        s["gg"] = _mm_nn("mm_w_glu", s["yg"], W("ssm_w_glu", l), F32)
        nmix = D // 1024
        mix_rows = [(s["proj"], 1024, gate_off, 0), (s["proj"], 1024, gate_off + nmix, 0), (s["ya"], 1024, 0, 0),
                    (s["gg"], 1024, 0, 0), (s["gg"], 1024, nmix, 0)]
        s["mix_rows"] = mix_rows
        s["mix"] = _rowwise("mix_fwd", _mixf, mix_rows, [], [(D, BF16)], ncol=nmix)[0]
        x = _mm_nn("mm_w_out", s["mix"], W("w_out", l), F32, add=x)
        s["x1"] = x
        s["h2"] = _rms_fwd("rms_xa", x, row("xa_norm_g", l))
        s["q"] = _mm_nn("mm_w_q", s["h2"], W("xa_w_q", l), BF16)
        s["mn"] = _rms_fwd("rms_mem", mem, row("mem_norm_g", l))
        s["kv"] = _mm_nn("mm_w_kv", s["mn"], W("xa_w_kv", l), BF16)
        s["o"] = _attn_fwd("attn_fwd", s["q"], s["kv"], XA_HEADS)
        x = _mm_nn("mm_w_o", s["o"], W("xa_w_o", l), F32, add=x)
        s["x2"] = x
        s["h3"] = _rms_fwd("rms_ffn", x, row("ffn_norm_g", l))
        s["up"] = _mm_nn("mm_w_up", s["h3"], W("ffn_w_up", l), F32)
        s["act"] = _ffn_conv_fwd("ffn_conv_fwd", s["up"], dw_filter("ffn_dw_w", l), DFF)
        x = _mm_nn("mm_w_down", s["act"], W("ffn_w_down", l), F32, add=x)
        saved.append(s)

    loss_part, dx, dx16, d_final_g = _loss_head("loss_head", x, inp["final_norm_g"][None, :], target[0])

    big = {n: [None] * L for n in _SHARDED_COL + _SHARDED_ROW}
    small = {n: [None] * L for n in _WEIGHTS if n not in big and n != "final_norm_g"}
    for l in reversed(range(L)):
        s = saved[l]
        dact = _mm_nt("mm_w_down_t", dx16, W("ffn_w_down", l), BF16)
        big["ffn_w_down"][l] = _mm_tn("mm_dw_down", s["act"], dx16).reshape(N_DEV, DFF // N_DEV, D)
        d_up, d_ffn_dw = _ffn_conv_bwd("ffn_conv_bwd", s["up"], dw_filter("ffn_dw_w", l), dact, DFF)
        big["ffn_dw_w"][l] = jnp.transpose(d_ffn_dw.reshape(8, N_DEV, 2 * DFF // N_DEV), (1, 0, 2))
        dh3 = _mm_nt("mm_w_up_t", d_up, W("ffn_w_up", l), BF16)
        big["ffn_w_up"][l] = _mm_tn("mm_dw_up", s["h3"], d_up, nb=2 * DFF // N_DEV)
        dx, dx16, small["ffn_norm_g"][l] = _rms_bwd("rms_ffn_bwd", s["x2"], row("ffn_norm_g", l), dh3, dx)
        do = _mm_nt("mm_w_o_t", dx16, W("xa_w_o", l), BF16)
        big["xa_w_o"][l] = _mm_tn("mm_dw_o", s["o"], dx16).reshape(N_DEV, D // N_DEV, D)
        dq, dk, dv = _attn_bwd("attn_bwd", s["q"], s["kv"], do, XA_HEADS)
        dkv = jnp.concatenate([dk, dv], axis=1).astype(BF16)
        dh2 = _mm_nt("mm_w_q_t", dq, W("xa_w_q", l), BF16)
        big["xa_w_q"][l] = _mm_tn("mm_dw_q", s["h2"], dq).reshape(N_DEV, D // N_DEV, D)
        dmn = _mm_nt("mm_w_kv_t", dkv, W("xa_w_kv", l), BF16)
        big["xa_w_kv"][l] = _mm_tn("mm_dw_kv", s["mn"], dkv, nb=2 * D // N_DEV)

        def mem_bwd(mv, dv_, gv):
            _, vjp = jax.vjp(_rms, mv, gv)
            return jnp.sum(vjp(dv_.astype(F32))[1], axis=0, keepdims=True)

        small["mem_norm_g"][l] = _rowwise("rms_mem_bwd", mem_bwd, [(mem, D, 0, 0), (dmn, D, 0, 0)],
                                          [row("mem_norm_g", l)], [], [(1, D)], tile=128)[0]
        dx, dx16, small["xa_norm_g"][l] = _rms_bwd("rms_xa_bwd", s["x1"], row("xa_norm_g", l), dh2, dx)
        dmix = _mm_nt("mm_w_out_t", dx16, W("w_out", l), BF16)
        big["w_out"][l] = _mm_tn("mm_dw_out", s["mix"], dx16).reshape(N_DEV, D // N_DEV, D)

        def mix_bwd(gla, glb, ya, ga, gb, dm):
            _, vjp = jax.vjp(_mixf, gla, glb, ya, ga, gb)
            return vjp(dm.astype(F32))

        nmix = D // 1024
        dgla, dglb, dya, dga, dgb = _rowwise("mix_bwd", mix_bwd, s["mix_rows"] + [(dmix, 1024, 0, 0)], [],
                                             [(D, BF16)] * 5, ncol=nmix)
        dgg = jnp.concatenate([dga, dgb], axis=1)
        dyg = _mm_nt("mm_w_glu_t", dgg, W("ssm_w_glu", l), BF16)
        big["ssm_w_glu"][l] = _mm_tn("mm_dw_glu", s["yg"], dgg, nb=2 * D // N_DEV)
        du, dbr, dbi, dcr, dci, dar, dai, dd = _ssm_bwd("ssm_bwd", s["proj"], u_off, ssm_p[l], s["ck_r"], s["ck_i"],
                                                        dyg, SW)
        cots = (dar[:, 0, :].reshape(G, SSM_STATE), dai[:, 0, :].reshape(G, SSM_STATE),
                _block_diag_t(dbr, SSM_GROUP, SSM_STATE).reshape(G * SSM_GROUP, SSM_STATE),
                _block_diag_t(dbi, SSM_GROUP, SSM_STATE).reshape(G * SSM_GROUP, SSM_STATE))
        g_are, g_aim, g_ldt, g_bre, g_bim = _ssm_prep_bwd("ssm_prep_bwd", ssm_raw[l], cots)
        small["ssm_a_re"][l], small["ssm_a_im"][l], small["ssm_log_dt"][l] = g_are, g_aim, g_ldt[:, 0]
        small["ssm_b_re"][l] = jnp.transpose(g_bre.reshape(G, SSM_GROUP, SSM_STATE), (0, 2, 1))
        small["ssm_b_im"][l] = jnp.transpose(g_bim.reshape(G, SSM_GROUP, SSM_STATE), (0, 2, 1))
        small["ssm_c_re"][l] = _block_diag_t(dcr, SSM_GROUP, SSM_STATE)
        small["ssm_c_im"][l] = _block_diag_t(dci, SSM_GROUP, SSM_STATE)
        small["ssm_d"][l] = dd[:, 0, :].reshape(SW)
        dhs = _mm_nt("mm_w_pw_t", dya, W("conv_w_pw", l), BF16)
        big["conv_w_pw"][l] = _mm_tn("mm_dw_pw", s["hs"], dya, nb=D // N_DEV)

        def post_bwd(hc, dh, b, lg, lb):
            _, vjp = jax.vjp(_convpost, hc, b, lg, lb)
            dhc, db, dlg, dlb = vjp(dh.astype(F32))
            return dhc, jnp.sum(db, axis=0, keepdims=True), jnp.sum(dlg, axis=0, keepdims=True), \
                jnp.sum(dlb, axis=0, keepdims=True)

        dhc, small["conv_dw_b"][l], small["conv_ln_g"][l], small["conv_ln_b"][l] = _rowwise(
            "conv_post_bwd", post_bwd, [(s["hc"], CW, 0, 0), (dhs, CW, 0, 0)],
            [row("conv_dw_b", l), row("conv_ln_g", l), row("conv_ln_b", l)], [(CW, F32)], [(1, CW)] * 3)
        da, db, d_conv_dw = _glu_conv_bwd("conv_bwd", s["proj"], dw_filter("conv_dw_w", l), dhc, CW)
        big["conv_dw_w"][l] = jnp.transpose(d_conv_dw.reshape(HALO, N_DEV, CW // N_DEV), (1, 0, 2))
        dproj = jnp.concatenate([da, db, du, dgla, dglb], axis=1)
        dh1 = _mm_nt("mm_w_in_t", dproj, W("w_in", l), BF16)
        big["w_in"][l] = _mm_tn("mm_dw_in", s["h1"], dproj, nb=dproj.shape[1] // N_DEV)
        dx, dx16, small["mix_norm_g"][l] = _rms_bwd("rms_mix_bwd", s["x_in"], row("mix_norm_g", l), dh1, dx)

    results = {}
    for n in _SHARDED_COL + _SHARDED_ROW:
        w, m, v = inp[n], mom_m[n], mom_v[n]
        pad = {"conv_dw_w": HALO, "ffn_dw_w": 8}.get(n)
        if pad:
            w, m, v = _pad_rows(w, pad), _pad_rows(m, pad), _pad_rows(v, pad)
        outs = _reduce_update(n, [g.astype(BF16) for g in big[n]], w, m, v)
        if pad:
            outs = [o[:, :inp[n].shape[1], :] for o in outs]
        results[n] = outs

    names = [n for n in _WEIGHTS if n not in big]
    flat_g = jnp.concatenate([(jnp.stack(small[n]) if n != "final_norm_g" else d_final_g).reshape(-1)
                              for n in names])
    sizes = [inp[n].size for n in names]
    total = sum(sizes)
    pack_w = 8 * LANE
    rows_p = -(-total // (LANE * pack_w)) * LANE
    padn = rows_p * pack_w - total

    def pack(parts, fill):
        return jnp.concatenate([p.reshape(-1) for p in parts] + [jnp.full((padn,), fill, F32)]).reshape(rows_p, pack_w)

    g_all = _allgather("ag_small_grads", pack([flat_g], 0.0)[None], False)
    outs = _sum_adamw("small_adamw", g_all.reshape(N_DEV * rows_p, pack_w), N_DEV,
                      pack([inp[n] for n in names], 0.0), pack([mom_m[n] for n in names], 0.0),
                      pack([mom_v[n] for n in names], 1.0))
    offs = 0
    for n, sz in zip(names, sizes):
        results[n] = [o.reshape(-1)[offs:offs + sz].reshape(inp[n].shape) for o in outs]
        offs += sz

    loss = lax.psum(loss_part, ("x", "y", "c"))
    grad_x = dx[None]
    return (loss, grad_x, *[results[n][0] for n in _WEIGHTS], *[results[n][1] for n in _WEIGHTS],
            *[results[n][2] for n in _WEIGHTS], *[results[n][3] for n in _WEIGHTS])


def kernel(x, mem, mix_norm_g, w_in, conv_dw_w, conv_dw_b, conv_ln_g, conv_ln_b, conv_w_pw, ssm_a_re, ssm_a_im, ssm_log_dt, ssm_b_re, ssm_b_im, ssm_c_re, ssm_c_im, ssm_d, ssm_w_glu, w_out, xa_norm_g, mem_norm_g, xa_w_q, xa_w_kv, xa_w_o, ffn_norm_g, ffn_w_up, ffn_dw_w, ffn_w_down, final_norm_g, loss_target, m_mix_norm_g, m_w_in, m_conv_dw_w, m_conv_dw_b, m_conv_ln_g, m_conv_ln_b, m_conv_w_pw, m_ssm_a_re, m_ssm_a_im, m_ssm_log_dt, m_ssm_b_re, m_ssm_b_im, m_ssm_c_re, m_ssm_c_im, m_ssm_d, m_ssm_w_glu, m_w_out, m_xa_norm_g, m_mem_norm_g, m_xa_w_q, m_xa_w_kv, m_xa_w_o, m_ffn_norm_g, m_ffn_w_up, m_ffn_dw_w, m_ffn_w_down, m_final_norm_g, v_mix_norm_g, v_w_in, v_conv_dw_w, v_conv_dw_b, v_conv_ln_g, v_conv_ln_b, v_conv_w_pw, v_ssm_a_re, v_ssm_a_im, v_ssm_log_dt, v_ssm_b_re, v_ssm_b_im, v_ssm_c_re, v_ssm_c_im, v_ssm_d, v_ssm_w_glu, v_w_out, v_xa_norm_g, v_mem_norm_g, v_xa_w_q, v_xa_w_kv, v_xa_w_o, v_ffn_norm_g, v_ffn_w_up, v_ffn_dw_w, v_ffn_w_down, v_final_norm_g):
    args = (x, mem, mix_norm_g, w_in, conv_dw_w, conv_dw_b, conv_ln_g, conv_ln_b, conv_w_pw, ssm_a_re, ssm_a_im, ssm_log_dt, ssm_b_re, ssm_b_im, ssm_c_re, ssm_c_im, ssm_d, ssm_w_glu, w_out, xa_norm_g, mem_norm_g, xa_w_q, xa_w_kv, xa_w_o, ffn_norm_g, ffn_w_up, ffn_dw_w, ffn_w_down, final_norm_g)
    ms = (m_mix_norm_g, m_w_in, m_conv_dw_w, m_conv_dw_b, m_conv_ln_g, m_conv_ln_b, m_conv_w_pw, m_ssm_a_re, m_ssm_a_im, m_ssm_log_dt, m_ssm_b_re, m_ssm_b_im, m_ssm_c_re, m_ssm_c_im, m_ssm_d, m_ssm_w_glu, m_w_out, m_xa_norm_g, m_mem_norm_g, m_xa_w_q, m_xa_w_kv, m_xa_w_o, m_ffn_norm_g, m_ffn_w_up, m_ffn_dw_w, m_ffn_w_down, m_final_norm_g)
    vs = (v_mix_norm_g, v_w_in, v_conv_dw_w, v_conv_dw_b, v_conv_ln_g, v_conv_ln_b, v_conv_w_pw, v_ssm_a_re, v_ssm_a_im, v_ssm_log_dt, v_ssm_b_re, v_ssm_b_im, v_ssm_c_re, v_ssm_c_im, v_ssm_d, v_ssm_w_glu, v_w_out, v_xa_norm_g, v_mem_norm_g, v_xa_w_q, v_xa_w_kv, v_xa_w_o, v_ffn_norm_g, v_ffn_w_up, v_ffn_dw_w, v_ffn_w_down, v_final_norm_g)
    return _step(dict(zip(_FWD, args)), loss_target, dict(zip(_WEIGHTS, ms)), dict(zip(_WEIGHTS, vs)))
```

```python
import functools

import jax
import jax.numpy as jnp
from jax import lax
from jax.experimental import pallas as pl
from jax.experimental.pallas import tpu as pltpu
from jax.experimental.pallas import tpu_sc as plsc

F32 = jnp.float32
BF16 = jnp.bfloat16
MESH_ID = pl.DeviceIdType.MESH
N_DEV = 8
EPS = 1e-6
VMEM_LIMIT = 48 * 1024 * 1024
ANY = pl.BlockSpec(memory_space=pl.ANY)

ADAM_LR = 0.001
ADAM_B1 = 0.9
ADAM_B2 = 0.999
ADAM_EPS = 1e-08
ADAM_WD = 0.01
ADAM_STEP = 10

CONV_K = 31
FFN_K = 3
XA_HEADS = 4
SSM_GROUP = 16
SSM_STATE = 64
HALO = 32
LANE = 128
SSM_LANES = 512
AG_COLLECTIVE_ID = 1


def _pick(n, prefs):
    for p in prefs:
        if p <= n and n % p == 0:
            return p
    return n


def _params(sem, vmem=VMEM_LIMIT):
    return pltpu.CompilerParams(dimension_semantics=sem, vmem_limit_bytes=vmem)


def _sigmoid(x):
    return 1.0 / (1.0 + jnp.exp(-x))


def _silu(x):
    return x * _sigmoid(x)


def _gelu(x):
    return 0.5 * x * (1.0 + jnp.tanh(0.7978845608028654 * (x + 0.044715 * (x * x * x))))


def _rms(x, g):
    return x * lax.rsqrt(jnp.mean(x * x, axis=-1, keepdims=True) + EPS) * g


def _convpost(hc, bias, ln_g, ln_b):
    h = hc + bias
    mu = jnp.mean(h, axis=-1, keepdims=True)
    xc = h - mu
    y = xc * lax.rsqrt(jnp.mean(xc * xc, axis=-1, keepdims=True) + EPS)
    return _silu(y * ln_g + ln_b)


def _mixf(gla, glb, ya, ga, gb):
    return _sigmoid(gla) * ya + _sigmoid(glb) * (ga * _sigmoid(gb))


class _W:
    def __init__(self, arr, layer, blocked):
        self.arr, self.layer, self.blocked = arr, layer, blocked
        if blocked:
            _, _, self.K, self.nb = arr.shape
            self.N = N_DEV * self.nb
        else:
            _, self.K, self.N = arr.shape
            self.nb = self.N

    def spec(self, tk, tn, ki, ni):
        l = self.layer
        if self.blocked:
            per = self.nb // tn
            return pl.BlockSpec((None, None, tk, tn), lambda *g: (ni(*g) // per, l, ki(*g), ni(*g) % per))
        return pl.BlockSpec((None, tk, tn), lambda *g: (l, ki(*g), ni(*g)))


_M_TILES = (1024, 512, 256, 128, 64, 32, 16, 8)
_N_TILES = (1408, 1024, 896, 512, 256, 128)
_K_TILES = (512, 1408, 896, 256, 128)


def _mm_nn(name, a, w, out_dtype, add=None):
    M, K = a.shape
    assert K == w.K
    tm, tk, tn = _pick(M, _M_TILES), _pick(K, _K_TILES), _pick(w.nb, _N_TILES)
    nk = K // tk

    def body(*refs):
        if add is None:
            a_ref, w_ref, o_ref, acc = refs
        else:
            a_ref, w_ref, r_ref, o_ref, acc = refs
        k = pl.program_id(2)

        @pl.when(k == 0)
        def _():
            acc[...] = jnp.zeros_like(acc)

        acc[...] += jnp.dot(a_ref[...].astype(BF16), w_ref[...], preferred_element_type=F32)

        @pl.when(k == nk - 1)
        def _():
            res = acc[...]
            if add is not None:
                res = res + r_ref[...]
            o_ref[...] = res.astype(o_ref.dtype)

    in_specs = [pl.BlockSpec((tm, tk), lambda i, j, k: (i, k)),
                w.spec(tk, tn, lambda i, j, k: k, lambda i, j, k: j)]
    args = [a, w.arr]
    if add is not None:
        in_specs.append(pl.BlockSpec((tm, tn), lambda i, j, k: (i, j)))
        args.append(add)
    return pl.pallas_call(
        body, name=name, grid=(M // tm, w.N // tn, nk), in_specs=in_specs,
        out_specs=pl.BlockSpec((tm, tn), lambda i, j, k: (i, j)),
        out_shape=jax.ShapeDtypeStruct((M, w.N), out_dtype),
        scratch_shapes=[pltpu.VMEM((tm, tn), F32)],
        compiler_params=_params(("parallel", "parallel", "arbitrary")),
    )(*args)


def _mm_nt(name, a, w, out_dtype):
    M, N = a.shape
    assert N == w.N
    tm, tkk, tnn = _pick(M, _M_TILES), _pick(w.K, _N_TILES), _pick(w.nb, _K_TILES)
    nn = N // tnn

    def body(a_ref, w_ref, o_ref, acc):
        n = pl.program_id(2)

        @pl.when(n == 0)
        def _():
            acc[...] = jnp.zeros_like(acc)

        acc[...] += lax.dot_general(a_ref[...].astype(BF16), w_ref[...], (((1,), (1,)), ((), ())),
                                    preferred_element_type=F32)

        @pl.when(n == nn - 1)
        def _():
            o_ref[...] = acc[...].astype(o_ref.dtype)

    return pl.pallas_call(
        body, name=name, grid=(M // tm, w.K // tkk, nn),
        in_specs=[pl.BlockSpec((tm, tnn), lambda i, j, n: (i, n)),
                  w.spec(tkk, tnn, lambda i, j, n: j, lambda i, j, n: n)],
        out_specs=pl.BlockSpec((tm, tkk), lambda i, j, n: (i, j)),
        out_shape=jax.ShapeDtypeStruct((M, w.K), out_dtype),
        scratch_shapes=[pltpu.VMEM((tm, tkk), F32)],
        compiler_params=_params(("parallel", "parallel", "arbitrary")),
    )(a, w.arr)


def _mm_tn(name, a, b, nb=None):
    T, K = a.shape
    _, N = b.shape
    width = N if nb is None else nb
    tkk, tn, tt = _pick(K, (1024, 512, 256, 128)), _pick(width, _N_TILES), _pick(T, (512, 256, 128, 64, 32, 16))
    nt = T // tt
    per = width // tn

    def body(a_ref, b_ref, o_ref, acc):
        t = pl.program_id(2)

        @pl.when(t == 0)
        def _():
            acc[...] = jnp.zeros_like(acc)

        acc[...] += lax.dot_general(a_ref[...].astype(BF16), b_ref[...].astype(BF16), (((0,), (0,)), ((), ())),
                                    preferred_element_type=F32)

        @pl.when(t == nt - 1)
        def _():
            o_ref[...] = acc[...].astype(o_ref.dtype)

    if nb is None:
        out_spec = pl.BlockSpec((tkk, tn), lambda i, j, t: (i, j))
        out_shape = jax.ShapeDtypeStruct((K, N), BF16)
    else:
        out_spec = pl.BlockSpec((None, tkk, tn), lambda i, j, t: (j // per, i, j % per))
        out_shape = jax.ShapeDtypeStruct((N_DEV, K, nb), BF16)
    return pl.pallas_call(
        body, name=name, grid=(K // tkk, N // tn, nt),
        in_specs=[pl.BlockSpec((tt, tkk), lambda i, j, t: (t, i)),
                  pl.BlockSpec((tt, tn), lambda i, j, t: (t, j))],
        out_specs=out_spec, out_shape=out_shape,
        scratch_shapes=[pltpu.VMEM((tkk, tn), F32)],
        compiler_params=_params(("parallel", "parallel", "arbitrary")),
    )(a, b)


def _rowwise(name, fn, rows, consts, outs, accs=(), tile=256, ncol=1, nrows=None):
    n_r, n_c, n_o = len(rows), len(consts), len(outs)
    T = rows[0][0].shape[0] if nrows is None else nrows
    tile = _pick(T, tuple(t for t in (512, 256, 128, 64, 32, 16, 8) if t <= tile))
    nt = T // tile

    def body(*refs):
        vals = [r[...] for r in refs[:n_r + n_c]]
        res = fn(*vals)
        if not isinstance(res, (tuple, list)):
            res = (res,)
        o_refs = refs[n_r + n_c:n_r + n_c + n_o]
        a_refs = refs[n_r + n_c + n_o:]
        for r, v in zip(o_refs, res[:n_o]):
            r[...] = v.astype(r.dtype)
        first = pl.program_id(1) == 0
        for r, v in zip(a_refs, res[n_o:]):
            @pl.when(first)
            def _(r=r, v=v):
                r[...] = v.astype(F32)

            @pl.when(jnp.logical_not(first))
            def _(r=r, v=v):
                r[...] += v.astype(F32)

    in_specs, args = [], []
    for arr, w, off, roff in rows:
        rb = roff // tile
        assert roff % tile == 0
        in_specs.append(pl.BlockSpec((tile, w), lambda j, i, off=off, rb=rb: (i + rb, off + j)))
        args.append(arr)
    for cst in consts:
        in_specs.append(pl.BlockSpec(cst.shape, lambda j, i: (0, 0)))
        args.append(cst)
    out_specs, out_shape = [], []
    for tw, dt in outs:
        out_specs.append(pl.BlockSpec((tile, tw // ncol), lambda j, i: (i, j)))
        out_shape.append(jax.ShapeDtypeStruct((T, tw), dt))
    for nr, tw in accs:
        out_specs.append(pl.BlockSpec((nr, tw // ncol), lambda j, i: (0, j)))
        out_shape.append(jax.ShapeDtypeStruct((nr, tw), F32))
    res = pl.pallas_call(
        body, name=name, grid=(ncol, nt), in_specs=in_specs, out_specs=out_specs, out_shape=out_shape,
        compiler_params=_params(("parallel", "arbitrary")),
    )(*args)
    return res


def _rms_fwd(name, x, g):
    D = x.shape[1]
    return _rowwise(name, lambda xv, gv: _rms(xv, gv), [(x, D, 0, 0)], [g], [(D, BF16)])[0]


def _rms_bwd(name, x, g, dh, dx_in):
    D = x.shape[1]

    def fn(xv, dhv, dxv, gv):
        _, vjp = jax.vjp(_rms, xv, gv)
        dx, dg = vjp(dhv.astype(F32))
        tot = dx + dxv
        return tot, tot, jnp.sum(dg, axis=0, keepdims=True)

    return _rowwise(name, fn, [(x, D, 0, 0), (dh, D, 0, 0), (dx_in, D, 0, 0)], [g], [(D, F32), (D, BF16)], [(1, D)],
                    tile=256)


def _lag_views(win, K, R, forward):
    n = win.shape[0]
    for r in range(8):
        if r >= K:
            break
        if r == 0:
            rolled = win
        else:
            rolled = pltpu.roll(win, (n - r) if forward else r, axis=0)
        for q in range((K - 1 - r) // 8 + 1):
            s = 8 * q + r
            if forward:
                yield s, rolled[8 * q:8 * q + R]
            else:
                yield s, rolled[HALO - 8 * q:HALO - 8 * q + R]


def _conv_chunk(win, w_ref, K, R):
    acc = None
    for s, view in _lag_views(win, K, R, forward=False):
        term = w_ref[K - 1 - s:K - s, :] * view
        acc = term if acc is None else acc + term
    return acc


def _conv_chunk_t(win, w_ref, K, R):
    acc = None
    for s, view in _lag_views(win, K, R, forward=True):
        term = w_ref[K - 1 - s:K - s, :] * view
        acc = term if acc is None else acc + term
    return acc


def _conv_dw(xwin, dy, K, R):
    taps = [None] * K
    for s, view in _lag_views(xwin, K, R, forward=False):
        taps[K - 1 - s] = jnp.sum(dy * view, axis=0, keepdims=True)
    return taps


def _chunks(T):
    R = _pick(T, (128, 64, 32))
    return R, T // R


def _glu_conv_fwd(name, proj, w, cw_total):
    T = proj.shape[0]
    C = cw_total
    cw = LANE
    nb = C // cw
    R, nch = _chunks(T)

    def body(a_ref, b_ref, w_ref, o_ref, s_ref):
        s_ref[0:HALO, :] = jnp.zeros((HALO, cw), F32)

        def fill(i, _):
            r0 = pl.multiple_of(i * R, R)
            s_ref[pl.ds(HALO + r0, R), :] = a_ref[pl.ds(r0, R), :] * _sigmoid(b_ref[pl.ds(r0, R), :])
            return 0

        lax.fori_loop(0, nch, fill, 0)

        def conv(i, _):
            r0 = pl.multiple_of(i * R, R)
            o_ref[pl.ds(r0, R), :] = _conv_chunk(s_ref[pl.ds(r0, R + HALO), :], w_ref, CONV_K, R)
            return 0

        lax.fori_loop(0, nch, conv, 0)

    return pl.pallas_call(
        body, name=name, grid=(nb,),
        in_specs=[pl.BlockSpec((T, cw), lambda j: (0, j)), pl.BlockSpec((T, cw), lambda j: (0, nb + j)),
                  pl.BlockSpec((HALO, cw), lambda j: (0, j))],
        out_specs=pl.BlockSpec((T, cw), lambda j: (0, j)),
        out_shape=jax.ShapeDtypeStruct((T, C), F32),
        scratch_shapes=[pltpu.VMEM((T + HALO, cw), F32)],
        compiler_params=_params(("parallel",)),
    )(proj, proj, w)


def _glu_conv_bwd(name, proj, w, dhc, cw_total):
    T = proj.shape[0]
    C = cw_total
    cw = LANE
    nb = C // cw
    R, nch = _chunks(T)

    def body(a_ref, b_ref, w_ref, dy_ref, da_ref, db_ref, dw_ref, s_ref, g_ref, acc_ref):
        s_ref[0:HALO, :] = jnp.zeros((HALO, cw), F32)
        g_ref[T:T + HALO, :] = jnp.zeros((HALO, cw), F32)
        acc_ref[...] = jnp.zeros_like(acc_ref)

        def fill(i, _):
            r0 = pl.multiple_of(i * R, R)
            s_ref[pl.ds(HALO + r0, R), :] = a_ref[pl.ds(r0, R), :] * _sigmoid(b_ref[pl.ds(r0, R), :])
            g_ref[pl.ds(r0, R), :] = dy_ref[pl.ds(r0, R), :]
            return 0

        lax.fori_loop(0, nch, fill, 0)

        def back(i, _):
            r0 = pl.multiple_of(i * R, R)
            dhg = _conv_chunk_t(g_ref[pl.ds(r0, R + HALO), :], w_ref, CONV_K, R)
            av = a_ref[pl.ds(r0, R), :]
            sg = _sigmoid(b_ref[pl.ds(r0, R), :])
            da_ref[pl.ds(r0, R), :] = (dhg * sg).astype(da_ref.dtype)
            db_ref[pl.ds(r0, R), :] = (dhg * av * sg * (1.0 - sg)).astype(db_ref.dtype)
            taps = _conv_dw(s_ref[pl.ds(r0, R + HALO), :], dy_ref[pl.ds(r0, R), :], CONV_K, R)
            for k, tap in enumerate(taps):
                acc_ref[k:k + 1, :] += tap
            return 0

        lax.fori_loop(0, nch, back, 0)
        dw_ref[...] = acc_ref[...]

    return pl.pallas_call(
        body, name=name, grid=(nb,),
        in_specs=[pl.BlockSpec((T, cw), lambda j: (0, j)), pl.BlockSpec((T, cw), lambda j: (0, nb + j)),
                  pl.BlockSpec((HALO, cw), lambda j: (0, j)), pl.BlockSpec((T, cw), lambda j: (0, j))],
        out_specs=[pl.BlockSpec((T, cw), lambda j: (0, j)), pl.BlockSpec((T, cw), lambda j: (0, j)),
                   pl.BlockSpec((HALO, cw), lambda j: (0, j))],
        out_shape=[jax.ShapeDtypeStruct((T, C), BF16), jax.ShapeDtypeStruct((T, C), BF16),
                   jax.ShapeDtypeStruct((HALO, C), F32)],
        scratch_shapes=[pltpu.VMEM((T + HALO, cw), F32), pltpu.VMEM((T + HALO, cw), F32),
                        pltpu.VMEM((HALO, cw), F32)],
        compiler_params=_params(("parallel",)),
    )(proj, proj, w, dhc)


def _ffn_conv_fwd(name, up, w, dff):
    T = up.shape[0]
    cw = LANE
    nb = dff // cw
    R, nch = _chunks(T)

    def body(g_ref, v_ref, wg_ref, wv_ref, o_ref, sg_ref, sv_ref):
        sg_ref[0:HALO, :] = jnp.zeros((HALO, cw), F32)
        sv_ref[0:HALO, :] = jnp.zeros((HALO, cw), F32)

        def fill(i, _):
            r0 = pl.multiple_of(i * R, R)
            sg_ref[pl.ds(HALO + r0, R), :] = g_ref[pl.ds(r0, R), :]
            sv_ref[pl.ds(HALO + r0, R), :] = v_ref[pl.ds(r0, R), :]
            return 0

        lax.fori_loop(0, nch, fill, 0)

        def conv(i, _):
            r0 = pl.multiple_of(i * R, R)
            gc = _conv_chunk(sg_ref[pl.ds(r0, R + HALO), :], wg_ref, FFN_K, R)
            vc = _conv_chunk(sv_ref[pl.ds(r0, R + HALO), :], wv_ref, FFN_K, R)
            o_ref[pl.ds(r0, R), :] = (_silu(gc) * vc).astype(o_ref.dtype)
            return 0

        lax.fori_loop(0, nch, conv, 0)

    return pl.pallas_call(
        body, name=name, grid=(nb,),
        in_specs=[pl.BlockSpec((T, cw), lambda j: (0, j)), pl.BlockSpec((T, cw), lambda j: (0, nb + j)),
                  pl.BlockSpec((8, cw), lambda j: (0, j)), pl.BlockSpec((8, cw), lambda j: (0, nb + j))],
        out_specs=pl.BlockSpec((T, cw), lambda j: (0, j)),
        out_shape=jax.ShapeDtypeStruct((T, dff), BF16),
        scratch_shapes=[pltpu.VMEM((T + HALO, cw), F32), pltpu.VMEM((T + HALO, cw), F32)],
        compiler_params=_params(("parallel",)),
    )(up, up, w, w)


def _ffn_conv_bwd(name, up, w, dact, dff):
    T = up.shape[0]
    cw = LANE
    nb = dff // cw
    R, nch = _chunks(T)

    def body(g_ref, v_ref, wg_ref, wv_ref, da_ref, dg_ref, dv_ref, dwg_ref, dwv_ref,
             sg_ref, sv_ref, tg_ref, tv_ref, ag_ref, av_ref):
        zero = jnp.zeros((HALO, cw), F32)
        sg_ref[0:HALO, :] = zero
        sv_ref[0:HALO, :] = zero
        tg_ref[T:T + HALO, :] = zero
        tv_ref[T:T + HALO, :] = zero
        ag_ref[...] = jnp.zeros_like(ag_ref)
        av_ref[...] = jnp.zeros_like(av_ref)

        def fill(i, _):
            r0 = pl.multiple_of(i * R, R)
            sg_ref[pl.ds(HALO + r0, R), :] = g_ref[pl.ds(r0, R), :]
            sv_ref[pl.ds(HALO + r0, R), :] = v_ref[pl.ds(r0, R), :]
            return 0

        lax.fori_loop(0, nch, fill, 0)

        def grads(i, _):
            r0 = pl.multiple_of(i * R, R)
            gwin = sg_ref[pl.ds(r0, R + HALO), :]
            vwin = sv_ref[pl.ds(r0, R + HALO), :]
            gc = _conv_chunk(gwin, wg_ref, FFN_K, R)
            vc = _conv_chunk(vwin, wv_ref, FFN_K, R)
            da = da_ref[pl.ds(r0, R), :].astype(F32)
            sg = _sigmoid(gc)
            dgc = da * vc * (sg * (1.0 + gc * (1.0 - sg)))
            dvc = da * (gc * sg)
            tg_ref[pl.ds(r0, R), :] = dgc
            tv_ref[pl.ds(r0, R), :] = dvc
            for k, tap in enumerate(_conv_dw(gwin, dgc, FFN_K, R)):
                ag_ref[k:k + 1, :] += tap
            for k, tap in enumerate(_conv_dw(vwin, dvc, FFN_K, R)):
                av_ref[k:k + 1, :] += tap
            return 0

        lax.fori_loop(0, nch, grads, 0)

        def back(i, _):
            r0 = pl.multiple_of(i * R, R)
            dg_ref[pl.ds(r0, R), :] = _conv_chunk_t(tg_ref[pl.ds(r0, R + HALO), :], wg_ref, FFN_K, R).astype(dg_ref.dtype)
            dv_ref[pl.ds(r0, R), :] = _conv_chunk_t(tv_ref[pl.ds(r0, R + HALO), :], wv_ref, FFN_K, R).astype(dv_ref.dtype)
            return 0

        lax.fori_loop(0, nch, back, 0)
        dwg_ref[...] = ag_ref[...]
        dwv_ref[...] = av_ref[...]

    col = lambda j: (0, j)
    dg, dv, dwg, dwv = pl.pallas_call(
        body, name=name, grid=(nb,),
        in_specs=[pl.BlockSpec((T, cw), col), pl.BlockSpec((T, cw), lambda j: (0, nb + j)),
                  pl.BlockSpec((8, cw), col), pl.BlockSpec((8, cw), lambda j: (0, nb + j)),
                  pl.BlockSpec((T, cw), col)],
        out_specs=[pl.BlockSpec((T, cw), col), pl.BlockSpec((T, cw), col),
                   pl.BlockSpec((8, cw), col), pl.BlockSpec((8, cw), col)],
        out_shape=[jax.ShapeDtypeStruct((T, dff), BF16), jax.ShapeDtypeStruct((T, dff), BF16),
                   jax.ShapeDtypeStruct((8, dff), F32), jax.ShapeDtypeStruct((8, dff), F32)],
        scratch_shapes=[pltpu.VMEM((T + HALO, cw), F32), pltpu.VMEM((T + HALO, cw), F32),
                        pltpu.VMEM((T + HALO, cw), F32), pltpu.VMEM((T + HALO, cw), F32),
                        pltpu.VMEM((8, cw), F32), pltpu.VMEM((8, cw), F32)],
        compiler_params=_params(("parallel",)),
    )(up, up, w, w, dact)
    return jnp.concatenate([dg, dv], axis=1), jnp.concatenate([dwg, dwv], axis=1)


def _zoh(a_re, a_im, log_dt):
    ar = jnp.minimum(a_re, -1e-4)
    ai = a_im
    dt = jnp.exp(log_dt)
    mag = jnp.exp(dt * ar)
    abar_re = mag * jnp.cos(dt * ai)
    abar_im = mag * jnp.sin(dt * ai)
    den = ar * ar + ai * ai
    nr = abar_re - 1.0
    ni = abar_im
    return abar_re, abar_im, (nr * ar + ni * ai) / den, (ni * ar - nr * ai) / den


def _discretize(a_re, a_im, log_dt, a_re_h, a_im_h, log_dt_h, b_re, b_im):
    abar_re, abar_im, _, _ = _zoh(a_re, a_im, log_dt)
    _, _, z_re, z_im = _zoh(a_re_h, a_im_h, log_dt_h)
    return abar_re, abar_im, z_re * b_re - z_im * b_im, z_re * b_im + z_im * b_re


def _full_specs(arrs):
    return [pl.BlockSpec(a.shape, lambda *_, n=len(a.shape): (0,) * n) for a in arrs]


def _ssm_prep(name, raw):
    def body(*refs):
        res = _discretize(*[r[...] for r in refs[:8]])
        for r, v in zip(refs[8:], res):
            r[...] = v

    outs = [jax.ShapeDtypeStruct(raw[0].shape, F32)] * 2 + [jax.ShapeDtypeStruct(raw[6].shape, F32)] * 2
    return pl.pallas_call(body, name=name, in_specs=_full_specs(raw), out_specs=_full_specs(outs), out_shape=outs)(*raw)


def _ssm_prep_bwd(name, raw, cots):
    G = raw[0].shape[0]
    H = raw[3].shape[0] // G

    def body(*refs):
        _, vjp = jax.vjp(_discretize, *[r[...] for r in refs[:8]])
        g = vjp(tuple(r[...] for r in refs[8:12]))
        outs = refs[12:]
        for k in range(3):
            rep = g[3 + k]
            outs[k][...] = g[k] + jnp.sum(rep.reshape(G, H, rep.shape[1]), axis=1)
        outs[3][...] = g[6]
        outs[4][...] = g[7]

    outs = [jax.ShapeDtypeStruct(a.shape, F32) for a in (raw[0], raw[1], raw[2], raw[6], raw[7])]
    return pl.pallas_call(body, name=name, in_specs=_full_specs(list(raw) + list(cots)), out_specs=_full_specs(outs),
                          out_shape=outs)(*raw, *cots)


def _cmul(ar, ai, br, bi):
    return ar * br - ai * bi, ar * bi + ai * br


def _scan_coefs(ar, ai, reverse):
    W = ar.shape[1]
    row = lax.broadcasted_iota(jnp.int32, (8, W), 0)
    p = [None] * 9
    p[1] = (ar, ai)
    for n in range(2, 9):
        p[n] = _cmul(*p[n // 2], *p[n - n // 2])
    steps = []
    for s in (1, 2, 4):
        valid = (row <= 7 - s) if reverse else (row >= s)
        steps.append((jnp.where(valid, p[s][0], 0.0), jnp.where(valid, p[s][1], 0.0)))
    pr = jnp.zeros((8, W), F32)
    pi = jnp.zeros((8, W), F32)
    for i in range(8):
        n = (8 - i) if reverse else (i + 1)
        pr = jnp.where(row == i, p[n][0], pr)
        pi = jnp.where(row == i, p[n][1], pi)
    return steps, (pr, pi)


def _scan_tile(xr, xi, cr, ci, coefs, reverse):
    steps, (pr, pi) = coefs
    for s, (sr, si) in zip((1, 2, 4), steps):
        shift = (8 - s) if reverse else s
        rr = pltpu.roll(xr, shift, axis=0)
        ri = pltpu.roll(xi, shift, axis=0)
        xr, xi = xr + sr * rr - si * ri, xi + sr * ri + si * rr
    xr, xi = xr + pr * cr - pi * ci, xi + pr * ci + pi * cr
    return xr, xi


def _edge_rows(x, reverse):
    W = x.shape[1]
    return jnp.broadcast_to(x[0:1, :] if reverse else x[7:8, :], (8, W))


def _ssm_chunk(T):
    return _pick(T, (256, 128, 64))


def _ssm_fwd(name, proj, u_off, p, width):
    T = proj.shape[0]
    NB = width // LANE
    Q = _ssm_chunk(T)
    nch = T // Q
    W = SSM_LANES

    def body(u_ref, bre, bim, cre, cim, ar_ref, ai_ref, d_ref, y_ref, ckr_ref, cki_ref, br_s, bi_s, car_r, car_i):
        c = pl.program_id(1)

        @pl.when(c == 0)
        def _():
            car_r[...] = jnp.zeros_like(car_r)
            car_i[...] = jnp.zeros_like(car_i)

        ckr_ref[...] = car_r[...]
        cki_ref[...] = car_i[...]
        u = u_ref[...]
        u16 = u.astype(BF16)
        br_s[...] = jnp.dot(u16, bre[...], preferred_element_type=F32)
        bi_s[...] = jnp.dot(u16, bim[...], preferred_element_type=F32)
        coefs = _scan_coefs(ar_ref[...], ai_ref[...], False)

        def tile(j, carry):
            r0 = pl.multiple_of(j * 8, 8)
            xr, xi = _scan_tile(br_s[pl.ds(r0, 8), :], bi_s[pl.ds(r0, 8), :], carry[0], carry[1], coefs, False)
            br_s[pl.ds(r0, 8), :] = xr
            bi_s[pl.ds(r0, 8), :] = xi
            return _edge_rows(xr, False), _edge_rows(xi, False)

        cr, ci = lax.fori_loop(0, Q // 8, tile, (car_r[...], car_i[...]))
        car_r[...] = cr
        car_i[...] = ci
        nt = (((1,), (1,)), ((), ()))
        y = (lax.dot_general(br_s[...].astype(BF16), cre[...], nt, preferred_element_type=F32)
             - lax.dot_general(bi_s[...].astype(BF16), cim[...], nt, preferred_element_type=F32)
             + d_ref[...] * u)
        y_ref[...] = _gelu(y).astype(y_ref.dtype)

    blk = lambda b, c: (b, 0, 0)
    mat = pl.BlockSpec((None, LANE, W), blk)
    vec = pl.BlockSpec((None, 1, W), blk)
    ck = pl.BlockSpec((None, None, 8, W), lambda b, c: (b, c, 0, 0))
    return pl.pallas_call(
        body, name=name, grid=(NB, nch),
        in_specs=[pl.BlockSpec((Q, LANE), lambda b, c: (c, u_off + b)), mat, mat, mat, mat, vec, vec,
                  pl.BlockSpec((1, LANE), lambda b, c: (0, b))],
        out_specs=[pl.BlockSpec((Q, LANE), lambda b, c: (c, b)), ck, ck],
        out_shape=[jax.ShapeDtypeStruct((T, width), BF16), jax.ShapeDtypeStruct((NB, nch, 8, W), F32),
                   jax.ShapeDtypeStruct((NB, nch, 8, W), F32)],
        scratch_shapes=[pltpu.VMEM((Q, W), F32), pltpu.VMEM((Q, W), F32), pltpu.VMEM((8, W), F32),
                        pltpu.VMEM((8, W), F32)],
        compiler_params=_params(("parallel", "arbitrary")),
    )(proj, p["bre"], p["bim"], p["cre"], p["cim"], p["ar"], p["ai"], p["d"])


def _ssm_bwd(name, proj, u_off, p, ck_r, ck_i, dyg, width):
    T = proj.shape[0]
    NB = width // LANE
    Q = _ssm_chunk(T)
    nch = T // Q
    W = SSM_LANES
    nt_dims = (((1,), (1,)), ((), ()))
    tn_dims = (((0,), (0,)), ((), ()))

    def body(u_ref, dy_ref, ckr_ref, cki_ref, bre, bim, cre, cim, ar_ref, ai_ref, d_ref,
             du_ref, dbr_ref, dbi_ref, dcr_ref, dci_ref, dar_ref, dai_ref, dd_ref,
             xr_s, xi_s, lr_s, li_s, lam_r, lam_i):
        c = pl.program_id(1)

        @pl.when(c == 0)
        def _():
            lam_r[...] = jnp.zeros_like(lam_r)
            lam_i[...] = jnp.zeros_like(lam_i)
            for r in (dbr_ref, dbi_ref, dcr_ref, dci_ref, dar_ref, dai_ref, dd_ref):
                r[...] = jnp.zeros_like(r)

        u = u_ref[...]
        u16 = u.astype(BF16)
        ar, ai = ar_ref[...], ai_ref[...]
        xr_s[0:8, :] = ckr_ref[...]
        xi_s[0:8, :] = cki_ref[...]
        xr_s[8:Q + 8, :] = jnp.dot(u16, bre[...], preferred_element_type=F32)
        xi_s[8:Q + 8, :] = jnp.dot(u16, bim[...], preferred_element_type=F32)
        fcoefs = _scan_coefs(ar, ai, False)

        def ftile(j, carry):
            r0 = pl.multiple_of(j * 8 + 8, 8)
            xr, xi = _scan_tile(xr_s[pl.ds(r0, 8), :], xi_s[pl.ds(r0, 8), :], carry[0], carry[1], fcoefs, False)
            xr_s[pl.ds(r0, 8), :] = xr
            xi_s[pl.ds(r0, 8), :] = xi
            return _edge_rows(xr, False), _edge_rows(xi, False)

        lax.fori_loop(0, Q // 8, ftile, (ckr_ref[...], cki_ref[...]))
        xr16 = xr_s[8:Q + 8, :].astype(BF16)
        xi16 = xi_s[8:Q + 8, :].astype(BF16)
        y = (lax.dot_general(xr16, cre[...], nt_dims, preferred_element_type=F32)
             - lax.dot_general(xi16, cim[...], nt_dims, preferred_element_type=F32) + d_ref[...] * u)
        _, gelu_vjp = jax.vjp(_gelu, y)
        dy = gelu_vjp(dy_ref[...].astype(F32))[0]
        dy16 = dy.astype(BF16)
        dd_ref[...] += jnp.broadcast_to(jnp.sum(dy * u, axis=0, keepdims=True), (8, LANE))
        dcr_ref[...] += lax.dot_general(dy16, xr16, tn_dims, preferred_element_type=F32)
        dci_ref[...] -= lax.dot_general(dy16, xi16, tn_dims, preferred_element_type=F32)
        lr_s[...] = jnp.dot(dy16, cre[...], preferred_element_type=F32)
        li_s[...] = -jnp.dot(dy16, cim[...], preferred_element_type=F32)
        rcoefs = _scan_coefs(ar, -ai, True)
        row = lax.broadcasted_iota(jnp.int32, (8, W), 0)

        def rtile(jj, carry):
            j = Q // 8 - 1 - jj
            r0 = pl.multiple_of(j * 8, 8)
            lr, li = _scan_tile(lr_s[pl.ds(r0, 8), :], li_s[pl.ds(r0, 8), :], carry[0], carry[1], rcoefs, True)
            lr_s[pl.ds(r0, 8), :] = lr
            li_s[pl.ds(r0, 8), :] = li
            cur_r, cur_i = xr_s[pl.ds(r0 + 8, 8), :], xi_s[pl.ds(r0 + 8, 8), :]
            prv_r, prv_i = xr_s[pl.ds(r0, 8), :], xi_s[pl.ds(r0, 8), :]
            xpr = jnp.where(row == 0, _edge_rows(prv_r, False), pltpu.roll(cur_r, 1, axis=0))
            xpi = jnp.where(row == 0, _edge_rows(prv_i, False), pltpu.roll(cur_i, 1, axis=0))
            return (_edge_rows(lr, True), _edge_rows(li, True),
                    carry[2] + lr * xpr + li * xpi, carry[3] + li * xpr - lr * xpi)

        zero = jnp.zeros((8, W), F32)
        cr, ci, sar, sai = lax.fori_loop(0, Q // 8, rtile, (lam_r[...], lam_i[...], zero, zero))
        lam_r[...] = cr
        lam_i[...] = ci
        dar_ref[...] += jnp.broadcast_to(jnp.sum(sar, axis=0, keepdims=True), (8, W))
        dai_ref[...] += jnp.broadcast_to(jnp.sum(sai, axis=0, keepdims=True), (8, W))
        lr16 = lr_s[...].astype(BF16)
        li16 = li_s[...].astype(BF16)
        dbr_ref[...] += lax.dot_general(u16, lr16, tn_dims, preferred_element_type=F32)
        dbi_ref[...] += lax.dot_general(u16, li16, tn_dims, preferred_element_type=F32)
        du = (lax.dot_general(lr16, bre[...], nt_dims, preferred_element_type=F32)
              + lax.dot_general(li16, bim[...], nt_dims, preferred_element_type=F32) + d_ref[...] * dy)
        du_ref[...] = du.astype(du_ref.dtype)

    blk = lambda b, c: (b, 0, 0)
    mat = pl.BlockSpec((None, LANE, W), blk)
    vec = pl.BlockSpec((None, 1, W), blk)
    acc8 = pl.BlockSpec((None, 8, W), blk)
    ck = pl.BlockSpec((None, None, 8, W), lambda b, c: (b, nch - 1 - c, 0, 0))
    return pl.pallas_call(
        body, name=name, grid=(NB, nch),
        in_specs=[pl.BlockSpec((Q, LANE), lambda b, c: (nch - 1 - c, u_off + b)),
                  pl.BlockSpec((Q, LANE), lambda b, c: (nch - 1 - c, b)), ck, ck, mat, mat, mat, mat, vec, vec,
                  pl.BlockSpec((1, LANE), lambda b, c: (0, b))],
        out_specs=[pl.BlockSpec((Q, LANE), lambda b, c: (nch - 1 - c, b)), mat, mat, mat, mat, acc8, acc8,
                   pl.BlockSpec((None, 8, LANE), blk)],
        out_shape=[jax.ShapeDtypeStruct((T, width), BF16)] + [jax.ShapeDtypeStruct((NB, LANE, W), F32)] * 4
                  + [jax.ShapeDtypeStruct((NB, 8, W), F32)] * 2 + [jax.ShapeDtypeStruct((NB, 8, LANE), F32)],
        scratch_shapes=[pltpu.VMEM((Q + 8, W), F32), pltpu.VMEM((Q + 8, W), F32), pltpu.VMEM((Q, W), F32),
                        pltpu.VMEM((Q, W), F32), pltpu.VMEM((8, W), F32), pltpu.VMEM((8, W), F32)],
        compiler_params=_params(("parallel", "arbitrary")),
    )(proj, dyg, ck_r, ck_i, p["bre"], p["bim"], p["cre"], p["cim"], p["ar"], p["ai"], p["d"])


def _block_diag(w):
    G, H, P = w.shape
    eye = jnp.eye(8, dtype=w.dtype)
    return (w.reshape(G // 8, 8, H, 1, P) * eye[None, :, None, :, None]).reshape(G // 8, 8 * H, 8 * P)


def _block_diag_t(d, H, P):
    NB = d.shape[0]
    d = d.reshape(NB, 8, H, 8, P)
    eye = jnp.eye(8, dtype=d.dtype)
    return jnp.sum(d * eye[None, :, None, :, None], axis=3).reshape(NB * 8, H, P)


def _attn_fwd(name, q, kv, heads):
    T, D = q.shape
    Mm = kv.shape[0]
    hd = D // heads
    tq = _pick(T, (512, 256, 128))
    scale = hd ** -0.5

    def body(q_ref, k_ref, v_ref, o_ref):
        s = lax.dot_general(q_ref[...], k_ref[...], (((1,), (1,)), ((), ())), preferred_element_type=F32) * scale
        s = s - jnp.max(s, axis=-1, keepdims=True)
        e = jnp.exp(s)
        p = e / jnp.sum(e, axis=-1, keepdims=True)
        o_ref[...] = jnp.dot(p.astype(BF16), v_ref[...], preferred_element_type=F32).astype(o_ref.dtype)

    return pl.pallas_call(
        body, name=name, grid=(heads, T // tq),
        in_specs=[pl.BlockSpec((tq, hd), lambda h, i: (i, h)), pl.BlockSpec((Mm, hd), lambda h, i: (0, h)),
                  pl.BlockSpec((Mm, hd), lambda h, i: (0, heads + h))],
        out_specs=pl.BlockSpec((tq, hd), lambda h, i: (i, h)),
        out_shape=jax.ShapeDtypeStruct((T, D), BF16),
        compiler_params=_params(("parallel", "parallel")),
    )(q, kv, kv)


def _attn_bwd(name, q, kv, do, heads):
    T, D = q.shape
    Mm = kv.shape[0]
    hd = D // heads
    tq = _pick(T, (512, 256, 128))
    scale = hd ** -0.5
    nt_dims = (((1,), (1,)), ((), ()))
    tn_dims = (((0,), (0,)), ((), ()))

    def body(q_ref, k_ref, v_ref, do_ref, dq_ref, dk_ref, dv_ref):
        i = pl.program_id(1)

        @pl.when(i == 0)
        def _():
            dk_ref[...] = jnp.zeros_like(dk_ref)
            dv_ref[...] = jnp.zeros_like(dv_ref)

        qv, kvl, vv, dov = q_ref[...], k_ref[...], v_ref[...], do_ref[...]
        s = lax.dot_general(qv, kvl, nt_dims, preferred_element_type=F32) * scale
        s = s - jnp.max(s, axis=-1, keepdims=True)
        e = jnp.exp(s)
        p = e / jnp.sum(e, axis=-1, keepdims=True)
        p16 = p.astype(BF16)
        dv_ref[...] += lax.dot_general(p16, dov, tn_dims, preferred_element_type=F32)
        dp = lax.dot_general(dov, vv, nt_dims, preferred_element_type=F32)
        ds = (p * (dp - jnp.sum(dp * p, axis=-1, keepdims=True)) * scale).astype(BF16)
        dq_ref[...] = jnp.dot(ds, kvl, preferred_element_type=F32).astype(dq_ref.dtype)
        dk_ref[...] += lax.dot_general(ds, qv, tn_dims, preferred_element_type=F32)

    return pl.pallas_call(
        body, name=name, grid=(heads, T // tq),
        in_specs=[pl.BlockSpec((tq, hd), lambda h, i: (i, h)), pl.BlockSpec((Mm, hd), lambda h, i: (0, h)),
                  pl.BlockSpec((Mm, hd), lambda h, i: (0, heads + h)), pl.BlockSpec((tq, hd), lambda h, i: (i, h))],
        out_specs=[pl.BlockSpec((tq, hd), lambda h, i: (i, h)), pl.BlockSpec((Mm, hd), lambda h, i: (0, h)),
                   pl.BlockSpec((Mm, hd), lambda h, i: (0, h))],
        out_shape=[jax.ShapeDtypeStruct((T, D), BF16), jax.ShapeDtypeStruct((Mm, D), F32),
                   jax.ShapeDtypeStruct((Mm, D), F32)],
        compiler_params=_params(("parallel", "arbitrary")),
    )(q, kv, kv, do)


def _position():
    return lax.axis_index("x"), lax.axis_index("y"), lax.axis_index("c")


def _allgather(name, blk, row_mode):
    L = blk.shape[0]
    out_shape = (L, N_DEV) + blk.shape[1:] if row_mode else (N_DEV,) + blk.shape
    x_ref = jax.new_ref(blk, memory_space=pltpu.MemorySpace.HBM)
    out_ref = jax.empty_ref(jax.ShapeDtypeStruct(out_shape, blk.dtype), memory_space=pltpu.MemorySpace.HBM)

    def body(send_sems, recv_sems, local_sem):
        x, y, c = _position()
        me, sibling = (x, y, c), (x, y, 1 - c)
        chips = [(1 - x, y), (x, 1 - y), (1 - x, 1 - y)]
        barrier = pltpu.get_barrier_semaphore()
        for peer in [sibling] + [(*chip, c) for chip in chips]:
            pl.semaphore_signal(barrier, inc=1, device_id=peer, device_id_type=MESH_ID)
        pl.semaphore_wait(barrier, 4)

        def slot(px, py, pc):
            b = 4 * px + 2 * py + pc
            return out_ref.at[:, b] if row_mode else out_ref.at[b]

        def copy(k, block, to, src=None):
            return pltpu.make_async_remote_copy(
                src_ref=slot(*block) if src is None else src, dst_ref=slot(*block),
                send_sem=send_sems.at[k], recv_sem=recv_sems.at[k], device_id=to, device_id_type=MESH_ID)

        mine = pltpu.make_async_copy(x_ref, slot(*me), local_sem)
        mine.start()
        first = [copy(0, me, sibling, src=x_ref)]
        first += [copy(1 + j, me, (*chip, c), src=x_ref) for j, chip in enumerate(chips)]
        for cp in first:
            cp.start()
        passed = [copy(4 + j, (*chip, c), sibling) for j, chip in enumerate(chips)]
        for j, chip in enumerate(chips):
            copy(1 + j, (*chip, c), me).wait_recv()
            passed[j].start()
        copy(0, sibling, me).wait_recv()
        for j, chip in enumerate(chips):
            copy(4 + j, (*chip, 1 - c), me).wait_recv()
        for cp in first + passed:
            cp.wait_send()
        mine.wait()

    pl.kernel(
        body, mesh=plsc.ScalarSubcoreMesh(axis_name="sequencer", num_cores=1), name=name,
        scratch_types=(pltpu.SemaphoreType.DMA((7,)), pltpu.SemaphoreType.DMA((7,)), pltpu.SemaphoreType.DMA),
        compiler_params=pltpu.CompilerParams(collective_id=AG_COLLECTIVE_ID),
    )()
    return out_ref[...]


def _rs_sibling(name, grads):
    L = len(grads)
    blk = grads[0].shape[1:]

    def body(*refs):
        g_refs, out_ref, send_sems, recv_sems = refs[:L], refs[L], refs[L + 1], refs[L + 2]
        x, y, c = _position()
        sibling = (x, y, 1 - c)
        copies = []
        for l in range(L):
            for q in range(4):
                k = 4 * l + q
                copies.append(pltpu.make_async_remote_copy(
                    src_ref=g_refs[l].at[2 * q + (1 - c)], dst_ref=out_ref.at[q, l],
                    send_sem=send_sems.at[k], recv_sem=recv_sems.at[k], device_id=sibling, device_id_type=MESH_ID))
        for cp in copies:
            cp.start()
        for cp in copies:
            cp.wait_recv()
        for cp in copies:
            cp.wait_send()

    return pl.pallas_call(
        body, name=name, in_specs=[ANY] * L, out_specs=ANY,
        out_shape=jax.ShapeDtypeStruct((4, L) + blk, grads[0].dtype),
        scratch_shapes=[pltpu.SemaphoreType.DMA((4 * L,)), pltpu.SemaphoreType.DMA((4 * L,))],
    )(*grads)


def _rs_chips(name, parts):
    L = len(parts)
    blk = parts[0].shape[1:]

    def body(*refs):
        p_refs, out_ref, send_sems, recv_sems, local_sems = refs[:L], refs[L], refs[L + 1], refs[L + 2], refs[L + 3]
        x, y, c = _position()
        flips = [(1 - x, y), (x, 1 - y), (1 - x, 1 - y)]
        local = [pltpu.make_async_copy(p_refs[l].at[2 * x + y], out_ref.at[3, l], local_sems.at[l]) for l in range(L)]
        copies = [pltpu.make_async_remote_copy(
            src_ref=p_refs[l].at[2 * fx + fy], dst_ref=out_ref.at[k, l], send_sem=send_sems.at[3 * l + k],
            recv_sem=recv_sems.at[3 * l + k], device_id=(fx, fy, c), device_id_type=MESH_ID)
            for l in range(L) for k, (fx, fy) in enumerate(flips)]
        for cp in copies + local:
            cp.start()
        for cp in copies:
            cp.wait_recv()
        for cp in copies:
            cp.wait_send()
        for cp in local:
            cp.wait()

    return pl.pallas_call(
        body, name=name, in_specs=[ANY] * L, out_specs=ANY,
        out_shape=jax.ShapeDtypeStruct((4, L) + blk, parts[0].dtype),
        scratch_shapes=[pltpu.SemaphoreType.DMA((3 * L,)), pltpu.SemaphoreType.DMA((3 * L,)),
                        pltpu.SemaphoreType.DMA((L,))],
    )(*parts)


def _adamw_math(g, w, m, v):
    m = ADAM_B1 * m + (1.0 - ADAM_B1) * g
    v = ADAM_B2 * v + (1.0 - ADAM_B2) * (g * g)
    m_hat = m / (1.0 - ADAM_B1 ** ADAM_STEP)
    v_hat = v / (1.0 - ADAM_B2 ** ADAM_STEP)
    delta = -ADAM_LR * (m_hat / (jnp.sqrt(v_hat) + ADAM_EPS) + ADAM_WD * w)
    return delta, m, v


def _sum_adamw(name, parts, nparts, w, m, v):
    R, C = w.shape
    tile = _ew_tile(R, C, nparts + 7)

    def fn(*vals):
        g = vals[0].astype(F32)
        for pv in vals[1:nparts]:
            g = g + pv.astype(F32)
        delta, nm, nv = _adamw_math(g, *vals[nparts:])
        return g, delta, nm, nv

    rows = [(parts, C, 0, k * R) for k in range(nparts)] + [(a, C, 0, 0) for a in (w, m, v)]
    return _rowwise(name, fn, rows, [], [(C, F32)] * 4, tile=tile, nrows=R)


def _ew_tile(R, C, nblocks):
    budget = (VMEM_LIMIT * 3) // 4
    return _pick(R, tuple(t for t in (1024, 512, 256, 128, 64, 32, 16, 8) if 8 * t * C * nblocks <= budget))


def _pair_sum(name, grads, landed, layer, c_idx):
    _, r, c = grads.shape
    tile = _ew_tile(r, c, 3)

    def body(c_ref, g_ref, s_ref, o_ref):
        o_ref[...] = (g_ref[...].astype(F32) + s_ref[...].astype(F32)).astype(o_ref.dtype)

    return pl.pallas_call(
        body, name=name,
        grid_spec=pltpu.PrefetchScalarGridSpec(
            num_scalar_prefetch=1, grid=(4, r // tile),
            in_specs=[pl.BlockSpec((None, tile, c), lambda q, i, c_ref: (2 * q + c_ref[0], i, 0)),
                      pl.BlockSpec((None, None, tile, c), lambda q, i, c_ref: (q, layer, i, 0))],
            out_specs=pl.BlockSpec((None, tile, c), lambda q, i, c_ref: (q, i, 0))),
        out_shape=jax.ShapeDtypeStruct((4, r, c), BF16),
        compiler_params=_params(("parallel", "parallel")),
    )(c_idx, grads, landed)


def _reduce_update(name, grads, w, m, v):
    L, r, c = w.shape
    c_idx = lax.axis_index("c").astype(jnp.int32).reshape(1)
    landed = _rs_sibling(name + "_rs_sibling", grads)
    parts = [_pair_sum(name + "_pair_sum", grads[l], landed, l, c_idx) for l in range(L)]
    got = _rs_chips(name + "_rs_chips", parts)
    outs = _sum_adamw(name + "_adamw", got.reshape(4 * L * r, c), 4, w.reshape(L * r, c), m.reshape(L * r, c),
                      v.reshape(L * r, c))
    return [o.reshape(L, r, c) for o in outs]


def _loss_head(name, x, g, target):
    T, D = x.shape

    def fn(xv, tv, gv):
        def f(xx, gg):
            err = _rms(xx, gg) - tv
            return 0.5 * jnp.sum(jnp.mean(err * err, axis=-1, keepdims=True))

        loss, (dx, dg) = jax.value_and_grad(f, argnums=(0, 1))(xv, gv)
        return dx, dx, jnp.full((8, LANE), loss, F32), jnp.sum(dg, axis=0, keepdims=True)

    dx, dx16, loss, dg = _rowwise(name, fn, [(x, D, 0, 0), (target, D, 0, 0)], [g], [(D, F32), (D, BF16)],
                                  [(8, LANE), (1, D)], tile=128)
    return loss[0, 0], dx, dx16, dg


_SHARDED_COL = ("w_in", "conv_dw_w", "conv_w_pw", "ssm_w_glu", "xa_w_kv", "ffn_w_up", "ffn_dw_w")
_SHARDED_ROW = ("w_out", "xa_w_q", "xa_w_o", "ffn_w_down")
_WEIGHTS = ['mix_norm_g', 'w_in', 'conv_dw_w', 'conv_dw_b', 'conv_ln_g', 'conv_ln_b', 'conv_w_pw', 'ssm_a_re',
            'ssm_a_im', 'ssm_log_dt', 'ssm_b_re', 'ssm_b_im', 'ssm_c_re', 'ssm_c_im', 'ssm_d', 'ssm_w_glu', 'w_out',
            'xa_norm_g', 'mem_norm_g', 'xa_w_q', 'xa_w_kv', 'xa_w_o', 'ffn_norm_g', 'ffn_w_up', 'ffn_dw_w',
            'ffn_w_down', 'final_norm_g']
_FWD = ['x', 'mem'] + _WEIGHTS


def _pad_rows(a, rows):
    return jnp.pad(a, ((0, 0), (0, rows - a.shape[1]), (0, 0)))


def _step(inp, target, mom_m, mom_v):
    x0 = inp["x"][0]
    mem = inp["mem"][0]
    T, D = x0.shape
    L = inp["w_in"].shape[0]
    CW = inp["conv_dw_b"].shape[1]
    SW = inp["ssm_d"].shape[1]
    DFF = inp["ffn_w_down"].shape[1] * N_DEV
    G = SW // SSM_GROUP
    NB = SW // LANE
    u_off = (2 * CW) // LANE
    gate_off = (2 * CW + SW) // 1024

    gathered = {}
    for n in _SHARDED_COL:
        small = n in ("conv_dw_w", "ffn_dw_w")
        blk = inp[n] if small else inp[n].astype(BF16)
        if n == "conv_dw_w":
            blk = _pad_rows(blk, HALO)
        if n == "ffn_dw_w":
            blk = _pad_rows(blk, 8)
        gathered[n] = _allgather("ag_" + n, blk, False)
    for n in _SHARDED_ROW:
        full = _allgather("ag_" + n, inp[n].astype(BF16), True)
        gathered[n] = full.reshape(L, N_DEV * full.shape[2], full.shape[3])

    def W(n, l):
        return _W(gathered[n], l, n in _SHARDED_COL)

    def dw_filter(n, l):
        g = gathered[n]
        return jnp.transpose(g[:, l], (1, 0, 2)).reshape(g.shape[2], N_DEV * g.shape[3])

    def row(n, l):
        return inp[n][l][None, :]

    ssm_raw, ssm_p = [], []
    for l in range(L):
        a_re, a_im, ldt = inp["ssm_a_re"][l], inp["ssm_a_im"][l], inp["ssm_log_dt"][l][:, None]
        rep = lambda a: jnp.repeat(a, SSM_GROUP, axis=0)
        flat = lambda b: jnp.transpose(b, (0, 2, 1)).reshape(G * SSM_GROUP, SSM_STATE)
        raw = (a_re, a_im, ldt, rep(a_re), rep(a_im), rep(ldt), flat(inp["ssm_b_re"][l]), flat(inp["ssm_b_im"][l]))
        abar_re, abar_im, bbar_re, bbar_im = _ssm_prep("ssm_prep", raw)
        bbar_re = bbar_re.reshape(G, SSM_GROUP, SSM_STATE)
        bbar_im = bbar_im.reshape(G, SSM_GROUP, SSM_STATE)
        ssm_raw.append(raw)
        ssm_p.append(dict(
            bre=_block_diag(bbar_re).astype(BF16), bim=_block_diag(bbar_im).astype(BF16),
            cre=_block_diag(inp["ssm_c_re"][l]).astype(BF16), cim=_block_diag(inp["ssm_c_im"][l]).astype(BF16),
            ar=abar_re.reshape(NB, 1, SSM_LANES), ai=abar_im.reshape(NB, 1, SSM_LANES), d=row("ssm_d", l)))

    saved = []
    x = x0
    for l in range(L):
        s = {"x_in": x}
        s["h1"] = _rms_fwd("rms_mix", x, row("mix_norm_g", l))
        s["proj"] = _mm_nn("mm_w_in", s["h1"], W("w_in", l), F32)
        s["hc"] = _glu_conv_fwd("conv_fwd", s["proj"], dw_filter("conv_dw_w", l), CW)
        s["hs"] = _rowwise("conv_post", _convpost, [(s["hc"], CW, 0, 0)],
                           [row("conv_dw_b", l), row("conv_ln_g", l), row("conv_ln_b", l)], [(CW, BF16)])[0]
        s["ya"] = _mm_nn("mm_w_pw", s["hs"], W("conv_w_pw", l), F32)
        s["yg"], s["ck_r"], s["ck_i"] = _ssm_fwd("ssm_fwd", s["proj"], u_off, ssm_p[l], SW)
        s["gg"] = _mm_nn("mm_w_glu", s["yg"], W("ssm_w_glu", l), F32)
        nmix = D // 1024
        mix_rows = [(s["proj"], 1024, gate_off, 0), (s["proj"], 1024, gate_off + nmix, 0), (s["ya"], 1024, 0, 0),
                    (s["gg"], 1024, 0, 0), (s["gg"], 1024, nmix, 0)]
        s["mix_rows"] = mix_rows
        s["mix"] = _rowwise("mix_fwd", _mixf, mix_rows, [], [(D, BF16)], ncol=nmix)[0]
        x = _mm_nn("mm_w_out", s["mix"], W("w_out", l), F32, add=x)
        s["x1"] = x
        s["h2"] = _rms_fwd("rms_xa", x, row("xa_norm_g", l))
        s["q"] = _mm_nn("mm_w_q", s["h2"], W("xa_w_q", l), BF16)
        s["mn"] = _rms_fwd("rms_mem", mem, row("mem_norm_g", l))
        s["kv"] = _mm_nn("mm_w_kv", s["mn"], W("xa_w_kv", l), BF16)
        s["o"] = _attn_fwd("attn_fwd", s["q"], s["kv"], XA_HEADS)
        x = _mm_nn("mm_w_o", s["o"], W("xa_w_o", l), F32, add=x)
        s["x2"] = x
        s["h3"] = _rms_fwd("rms_ffn", x, row("ffn_norm_g", l))
        s["up"] = _mm_nn("mm_w_up", s["h3"], W("ffn_w_up", l), F32)
        s["act"] = _ffn_conv_fwd("ffn_conv_fwd", s["up"], dw_filter("ffn_dw_w", l), DFF)
        x = _mm_nn("mm_w_down", s["act"], W("ffn_w_down", l), F32, add=x)
        saved.append(s)

    loss_part, dx, dx16, d_final_g = _loss_head("loss_head", x, inp["final_norm_g"][None, :], target[0])

    big = {n: [None] * L for n in _SHARDED_COL + _SHARDED_ROW}
    small = {n: [None] * L for n in _WEIGHTS if n not in big and n != "final_norm_g"}
    for l in reversed(range(L)):
        s = saved[l]
        dact = _mm_nt("mm_w_down_t", dx16, W("ffn_w_down", l), BF16)
        big["ffn_w_down"][l] = _mm_tn("mm_dw_down", s["act"], dx16).reshape(N_DEV, DFF // N_DEV, D)
        d_up, d_ffn_dw = _ffn_conv_bwd("ffn_conv_bwd", s["up"], dw_filter("ffn_dw_w", l), dact, DFF)
        big["ffn_dw_w"][l] = jnp.transpose(d_ffn_dw.reshape(8, N_DEV, 2 * DFF // N_DEV), (1, 0, 2))
        dh3 = _mm_nt("mm_w_up_t", d_up, W("ffn_w_up", l), BF16)
        big["ffn_w_up"][l] = _mm_tn("mm_dw_up", s["h3"], d_up, nb=2 * DFF // N_DEV)
        dx, dx16, small["ffn_norm_g"][l] = _rms_bwd("rms_ffn_bwd", s["x2"], row("ffn_norm_g", l), dh3, dx)
        do = _mm_nt("mm_w_o_t", dx16, W("xa_w_o", l), BF16)
        big["xa_w_o"][l] = _mm_tn("mm_dw_o", s["o"], dx16).reshape(N_DEV, D // N_DEV, D)
        dq, dk, dv = _attn_bwd("attn_bwd", s["q"], s["kv"], do, XA_HEADS)
        dkv = jnp.concatenate([dk, dv], axis=1).astype(BF16)
        dh2 = _mm_nt("mm_w_q_t", dq, W("xa_w_q", l), BF16)
        big["xa_w_q"][l] = _mm_tn("mm_dw_q", s["h2"], dq).reshape(N_DEV, D // N_DEV, D)
        dmn = _mm_nt("mm_w_kv_t", dkv, W("xa_w_kv", l), BF16)
        big["xa_w_kv"][l] = _mm_tn("mm_dw_kv", s["mn"], dkv, nb=2 * D // N_DEV)

        def mem_bwd(mv, dv_, gv):
            _, vjp = jax.vjp(_rms, mv, gv)
            return jnp.sum(vjp(dv_.astype(F32))[1], axis=0, keepdims=True)

        small["mem_norm_g"][l] = _rowwise("rms_mem_bwd", mem_bwd, [(mem, D, 0, 0), (dmn, D, 0, 0)],
                                          [row("mem_norm_g", l)], [], [(1, D)], tile=128)[0]
        dx, dx16, small["xa_norm_g"][l] = _rms_bwd("rms_xa_bwd", s["x1"], row("xa_norm_g", l), dh2, dx)
        dmix = _mm_nt("mm_w_out_t", dx16, W("w_out", l), BF16)
        big["w_out"][l] = _mm_tn("mm_dw_out", s["mix"], dx16).reshape(N_DEV, D // N_DEV, D)

        def mix_bwd(gla, glb, ya, ga, gb, dm):
            _, vjp = jax.vjp(_mixf, gla, glb, ya, ga, gb)
            return vjp(dm.astype(F32))

        nmix = D // 1024
        dgla, dglb, dya, dga, dgb = _rowwise("mix_bwd", mix_bwd, s["mix_rows"] + [(dmix, 1024, 0, 0)], [],
                                             [(D, BF16)] * 5, ncol=nmix)
        dgg = jnp.concatenate([dga, dgb], axis=1)
        dyg = _mm_nt("mm_w_glu_t", dgg, W("ssm_w_glu", l), BF16)
        big["ssm_w_glu"][l] = _mm_tn("mm_dw_glu", s["yg"], dgg, nb=2 * D // N_DEV)
        du, dbr, dbi, dcr, dci, dar, dai, dd = _ssm_bwd("ssm_bwd", s["proj"], u_off, ssm_p[l], s["ck_r"], s["ck_i"],
                                                        dyg, SW)
        cots = (dar[:, 0, :].reshape(G, SSM_STATE), dai[:, 0, :].reshape(G, SSM_STATE),
                _block_diag_t(dbr, SSM_GROUP, SSM_STATE).reshape(G * SSM_GROUP, SSM_STATE),
                _block_diag_t(dbi, SSM_GROUP, SSM_STATE).reshape(G * SSM_GROUP, SSM_STATE))
        g_are, g_aim, g_ldt, g_bre, g_bim = _ssm_prep_bwd("ssm_prep_bwd", ssm_raw[l], cots)
        small["ssm_a_re"][l], small["ssm_a_im"][l], small["ssm_log_dt"][l] = g_are, g_aim, g_ldt[:, 0]
        small["ssm_b_re"][l] = jnp.transpose(g_bre.reshape(G, SSM_GROUP, SSM_STATE), (0, 2, 1))
        small["ssm_b_im"][l] = jnp.transpose(g_bim.reshape(G, SSM_GROUP, SSM_STATE), (0, 2, 1))
        small["ssm_c_re"][l] = _block_diag_t(dcr, SSM_GROUP, SSM_STATE)
        small["ssm_c_im"][l] = _block_diag_t(dci, SSM_GROUP, SSM_STATE)
        small["ssm_d"][l] = dd[:, 0, :].reshape(SW)
        dhs = _mm_nt("mm_w_pw_t", dya, W("conv_w_pw", l), BF16)
        big["conv_w_pw"][l] = _mm_tn("mm_dw_pw", s["hs"], dya, nb=D // N_DEV)

        def post_bwd(hc, dh, b, lg, lb):
            _, vjp = jax.vjp(_convpost, hc, b, lg, lb)
            dhc, db, dlg, dlb = vjp(dh.astype(F32))
            return dhc, jnp.sum(db, axis=0, keepdims=True), jnp.sum(dlg, axis=0, keepdims=True), \
                jnp.sum(dlb, axis=0, keepdims=True)

        dhc, small["conv_dw_b"][l], small["conv_ln_g"][l], small["conv_ln_b"][l] = _rowwise(
            "conv_post_bwd", post_bwd, [(s["hc"], CW, 0, 0), (dhs, CW, 0, 0)],
            [row("conv_dw_b", l), row("conv_ln_g", l), row("conv_ln_b", l)], [(CW, F32)], [(1, CW)] * 3)
        da, db, d_conv_dw = _glu_conv_bwd("conv_bwd", s["proj"], dw_filter("conv_dw_w", l), dhc, CW)
        big["conv_dw_w"][l] = jnp.transpose(d_conv_dw.reshape(HALO, N_DEV, CW // N_DEV), (1, 0, 2))
        dproj = jnp.concatenate([da, db, du, dgla, dglb], axis=1)
        dh1 = _mm_nt("mm_w_in_t", dproj, W("w_in", l), BF16)
        big["w_in"][l] = _mm_tn("mm_dw_in", s["h1"], dproj, nb=dproj.shape[1] // N_DEV)
        dx, dx16, small["mix_norm_g"][l] = _rms_bwd("rms_mix_bwd", s["x_in"], row("mix_norm_g", l), dh1, dx)

    results = {}
    for n in _SHARDED_COL + _SHARDED_ROW:
        w, m, v = inp[n], mom_m[n], mom_v[n]
        pad = {"conv_dw_w": HALO, "ffn_dw_w": 8}.get(n)
        if pad:
            w, m, v = _pad_rows(w, pad), _pad_rows(m, pad), _pad_rows(v, pad)
        outs = _reduce_update(n, [g.astype(BF16) for g in big[n]], w, m, v)
        if pad:
            outs = [o[:, :inp[n].shape[1], :] for o in outs]
        results[n] = outs

    names = [n for n in _WEIGHTS if n not in big]
    flat_g = jnp.concatenate([(jnp.stack(small[n]) if n != "final_norm_g" else d_final_g).reshape(-1)
                              for n in names])
    sizes = [inp[n].size for n in names]
    total = sum(sizes)
    pack_w = 8 * LANE
    rows_p = -(-total // (LANE * pack_w)) * LANE
    padn = rows_p * pack_w - total

    def pack(parts, fill):
        return jnp.concatenate([p.reshape(-1) for p in parts] + [jnp.full((padn,), fill, F32)]).reshape(rows_p, pack_w)

    g_all = _allgather("ag_small_grads", pack([flat_g], 0.0)[None], False)
    outs = _sum_adamw("small_adamw", g_all.reshape(N_DEV * rows_p, pack_w), N_DEV,
                      pack([inp[n] for n in names], 0.0), pack([mom_m[n] for n in names], 0.0),
                      pack([mom_v[n] for n in names], 1.0))
    offs = 0
    for n, sz in zip(names, sizes):
        results[n] = [o.reshape(-1)[offs:offs + sz].reshape(inp[n].shape) for o in outs]
        offs += sz

    loss = lax.psum(loss_part, ("x", "y", "c"))
    grad_x = dx[None]
    return (loss, grad_x, *[results[n][0] for n in _WEIGHTS], *[results[n][1] for n in _WEIGHTS],
            *[results[n][2] for n in _WEIGHTS], *[results[n][3] for n in _WEIGHTS])


def kernel(x, mem, mix_norm_g, w_in, conv_dw_w, conv_dw_b, conv_ln_g, conv_ln_b, conv_w_pw, ssm_a_re, ssm_a_im, ssm_log_dt, ssm_b_re, ssm_b_im, ssm_c_re, ssm_c_im, ssm_d, ssm_w_glu, w_out, xa_norm_g, mem_norm_g, xa_w_q, xa_w_kv, xa_w_o, ffn_norm_g, ffn_w_up, ffn_dw_w, ffn_w_down, final_norm_g, loss_target, m_mix_norm_g, m_w_in, m_conv_dw_w, m_conv_dw_b, m_conv_ln_g, m_conv_ln_b, m_conv_w_pw, m_ssm_a_re, m_ssm_a_im, m_ssm_log_dt, m_ssm_b_re, m_ssm_b_im, m_ssm_c_re, m_ssm_c_im, m_ssm_d, m_ssm_w_glu, m_w_out, m_xa_norm_g, m_mem_norm_g, m_xa_w_q, m_xa_w_kv, m_xa_w_o, m_ffn_norm_g, m_ffn_w_up, m_ffn_dw_w, m_ffn_w_down, m_final_norm_g, v_mix_norm_g, v_w_in, v_conv_dw_w, v_conv_dw_b, v_conv_ln_g, v_conv_ln_b, v_conv_w_pw, v_ssm_a_re, v_ssm_a_im, v_ssm_log_dt, v_ssm_b_re, v_ssm_b_im, v_ssm_c_re, v_ssm_c_im, v_ssm_d, v_ssm_w_glu, v_w_out, v_xa_norm_g, v_mem_norm_g, v_xa_w_q, v_xa_w_kv, v_xa_w_o, v_ffn_norm_g, v_ffn_w_up, v_ffn_dw_w, v_ffn_w_down, v_final_norm_g):
    args = (x, mem, mix_norm_g, w_in, conv_dw_w, conv_dw_b, conv_ln_g, conv_ln_b, conv_w_pw, ssm_a_re, ssm_a_im, ssm_log_dt, ssm_b_re, ssm_b_im, ssm_c_re, ssm_c_im, ssm_d, ssm_w_glu, w_out, xa_norm_g, mem_norm_g, xa_w_q, xa_w_kv, xa_w_o, ffn_norm_g, ffn_w_up, ffn_dw_w, ffn_w_down, final_norm_g)
    ms = (m_mix_norm_g, m_w_in, m_conv_dw_w, m_conv_dw_b, m_conv_ln_g, m_conv_ln_b, m_conv_w_pw, m_ssm_a_re, m_ssm_a_im, m_ssm_log_dt, m_ssm_b_re, m_ssm_b_im, m_ssm_c_re, m_ssm_c_im, m_ssm_d, m_ssm_w_glu, m_w_out, m_xa_norm_g, m_mem_norm_g, m_xa_w_q, m_xa_w_kv, m_xa_w_o, m_ffn_norm_g, m_ffn_w_up, m_ffn_dw_w, m_ffn_w_down, m_final_norm_g)
    vs = (v_mix_norm_g, v_w_in, v_conv_dw_w, v_conv_dw_b, v_conv_ln_g, v_conv_ln_b, v_conv_w_pw, v_ssm_a_re, v_ssm_a_im, v_ssm_log_dt, v_ssm_b_re, v_ssm_b_im, v_ssm_c_re, v_ssm_c_im, v_ssm_d, v_ssm_w_glu, v_w_out, v_xa_norm_g, v_mem_norm_g, v_xa_w_q, v_xa_w_kv, v_xa_w_o, v_ffn_norm_g, v_ffn_w_up, v_ffn_dw_w, v_ffn_w_down, v_final_norm_g)
    return _step(dict(zip(_FWD, args)), loss_target, dict(zip(_WEIGHTS, ms)), dict(zip(_WEIGHTS, vs)))
```

```python
import functools

import jax
import jax.numpy as jnp
from jax import lax
from jax.experimental import pallas as pl
from jax.experimental.pallas import tpu as pltpu
from jax.experimental.pallas import tpu_sc as plsc

F32 = jnp.float32
BF16 = jnp.bfloat16
MESH_ID = pl.DeviceIdType.MESH
N_DEV = 8
EPS = 1e-6
VMEM_LIMIT = 48 * 1024 * 1024
ANY = pl.BlockSpec(memory_space=pl.ANY)

ADAM_LR = 0.001
ADAM_B1 = 0.9
ADAM_B2 = 0.999
ADAM_EPS = 1e-08
ADAM_WD = 0.01
ADAM_STEP = 10

CONV_K = 31
FFN_K = 3
XA_HEADS = 4
SSM_GROUP = 16
SSM_STATE = 64
HALO = 32
LANE = 128
SSM_LANES = 512
AG_COLLECTIVE_ID = 1
RS_SIBLING_COLLECTIVE_ID = 2
RS_CHIPS_COLLECTIVE_ID = 3


def _pick(n, prefs):
    for p in prefs:
        if p <= n and n % p == 0:
            return p
    return n


def _params(sem, vmem=VMEM_LIMIT):
    return pltpu.CompilerParams(dimension_semantics=sem, vmem_limit_bytes=vmem)


_LAST_CALL = []


def _tc_call(*call_args, **call_kwargs):
    call = pl.pallas_call(*call_args, **call_kwargs)

    def run(*args):
        args = list(args)
        if _LAST_CALL:
            i = next(k for k, a in enumerate(args) if a.ndim >= 2)
            args[i] = lax.optimization_barrier((args[i], _LAST_CALL[0]))[0]
        out = call(*args)
        _LAST_CALL[:] = [out[0] if isinstance(out, (tuple, list)) else out]
        return out

    return run


def _sigmoid(x):
    return 1.0 / (1.0 + jnp.exp(-x))


def _silu(x):
    return x * _sigmoid(x)


def _gelu(x):
    return 0.5 * x * (1.0 + jnp.tanh(0.7978845608028654 * (x + 0.044715 * (x * x * x))))


def _rms(x, g):
    return x * lax.rsqrt(jnp.mean(x * x, axis=-1, keepdims=True) + EPS) * g


def _convpost(hc, bias, ln_g, ln_b):
    h = hc + bias
    mu = jnp.mean(h, axis=-1, keepdims=True)
    xc = h - mu
    y = xc * lax.rsqrt(jnp.mean(xc * xc, axis=-1, keepdims=True) + EPS)
    return _silu(y * ln_g + ln_b)


def _mixf(gla, glb, ya, ga, gb):
    return _sigmoid(gla) * ya + _sigmoid(glb) * (ga * _sigmoid(gb))


class _W:
    def __init__(self, arr, layer, blocked):
        self.arr, self.layer, self.blocked = arr, layer, blocked
        if blocked:
            _, _, self.K, self.nb = arr.shape
            self.N = N_DEV * self.nb
        else:
            _, self.K, self.N = arr.shape
            self.nb = self.N

    def spec(self, tk, tn, ki, ni):
        l = self.layer
        if self.blocked:
            per = self.nb // tn
            return pl.BlockSpec((None, None, tk, tn), lambda *g: (ni(*g) // per, l, ki(*g), ni(*g) % per))
        return pl.BlockSpec((None, tk, tn), lambda *g: (l, ki(*g), ni(*g)))


_M_TILES = (1024, 512, 256, 128, 64, 32, 16, 8)
_N_TILES = (1408, 1024, 896, 512, 256, 128)
_K_TILES = (512, 1408, 896, 256, 128)
MAX_FULL_K = 2048
_K_FULL_TILES = (2048, 1408, 1024, 896, 512, 256, 128)


def _mm_nn(name, a, w, out_dtype, add=None):
    M, K = a.shape
    assert K == w.K
    tm, tn = _pick(M, _M_TILES), _pick(w.nb, _N_TILES)
    tk = K if K <= MAX_FULL_K else _pick(K, _K_FULL_TILES)
    nk = K // tk

    def body(*refs):
        a_ref, w_ref = refs[:2]
        r_ref = refs[2] if add is not None else None
        o_ref = refs[3] if add is not None else refs[2]
        part = jnp.dot(a_ref[...].astype(BF16), w_ref[...], preferred_element_type=F32)
        if nk == 1:
            o_ref[...] = (part if add is None else part + r_ref[...]).astype(o_ref.dtype)
            return
        acc = refs[-1]
        k = pl.program_id(2)

        @pl.when(k == 0)
        def _():
            acc[...] = part

        @pl.when(k > 0)
        def _():
            acc[...] += part

        @pl.when(k == nk - 1)
        def _():
            res = acc[...]
            if add is not None:
                res = res + r_ref[...]
            o_ref[...] = res.astype(o_ref.dtype)

    in_specs = [pl.BlockSpec((tm, tk), lambda i, j, k: (i, k)),
                w.spec(tk, tn, lambda i, j, k: k, lambda i, j, k: j)]
    args = [a, w.arr]
    if add is not None:
        in_specs.append(pl.BlockSpec((tm, tn), lambda i, j, k: (i, j)))
        args.append(add)
    return _tc_call(
        body, name=name, grid=(M // tm, w.N // tn, nk), in_specs=in_specs,
        out_specs=pl.BlockSpec((tm, tn), lambda i, j, k: (i, j)),
        out_shape=jax.ShapeDtypeStruct((M, w.N), out_dtype),
        scratch_shapes=[] if nk == 1 else [pltpu.VMEM((tm, tn), F32)],
        compiler_params=_params(("parallel", "parallel", "arbitrary")),
    )(*args)


def _mm_nt(name, a, w, out_dtype):
    M, N = a.shape
    assert N == w.N
    tm, tkk = _pick(M, _M_TILES), _pick(w.K, _N_TILES)
    tnn = w.nb if w.nb <= MAX_FULL_K else _pick(w.nb, _K_FULL_TILES)
    nn = N // tnn

    def body(a_ref, w_ref, o_ref, *scratch):
        part = lax.dot_general(a_ref[...].astype(BF16), w_ref[...], (((1,), (1,)), ((), ())),
                               preferred_element_type=F32)
        if nn == 1:
            o_ref[...] = part.astype(o_ref.dtype)
            return
        acc = scratch[0]
        n = pl.program_id(2)

        @pl.when(n == 0)
        def _():
            acc[...] = part

        @pl.when(n > 0)
        def _():
            acc[...] += part

        @pl.when(n == nn - 1)
        def _():
            o_ref[...] = acc[...].astype(o_ref.dtype)

    return _tc_call(
        body, name=name, grid=(M // tm, w.K // tkk, nn),
        in_specs=[pl.BlockSpec((tm, tnn), lambda i, j, n: (i, n)),
                  w.spec(tkk, tnn, lambda i, j, n: j, lambda i, j, n: n)],
        out_specs=pl.BlockSpec((tm, tkk), lambda i, j, n: (i, j)),
        out_shape=jax.ShapeDtypeStruct((M, w.K), out_dtype),
        scratch_shapes=[] if nn == 1 else [pltpu.VMEM((tm, tkk), F32)],
        compiler_params=_params(("parallel", "parallel", "arbitrary")),
    )(a, w.arr)


def _mm_tn(name, a, b, nb=None):
    T, K = a.shape
    _, N = b.shape
    width = N if nb is None else nb
    tkk, tn, tt = _pick(K, (1024, 512, 256, 128)), _pick(width, _N_TILES), _pick(T, (1024, 512, 256, 128, 64, 32, 16))
    nt = T // tt
    per = width // tn

    def body(a_ref, b_ref, o_ref, acc):
        t = pl.program_id(2)
        part = lax.dot_general(a_ref[...].astype(BF16), b_ref[...].astype(BF16), (((0,), (0,)), ((), ())),
                               preferred_element_type=F32)

        @pl.when(t == 0)
        def _():
            acc[...] = part

        @pl.when(t > 0)
        def _():
            acc[...] += part

        @pl.when(t == nt - 1)
        def _():
            o_ref[...] = acc[...].astype(o_ref.dtype)

    if nb is None:
        out_spec = pl.BlockSpec((tkk, tn), lambda i, j, t: (i, j))
        out_shape = jax.ShapeDtypeStruct((K, N), BF16)
    else:
        out_spec = pl.BlockSpec((None, tkk, tn), lambda i, j, t: (j // per, i, j % per))
        out_shape = jax.ShapeDtypeStruct((N_DEV, K, nb), BF16)
    return _tc_call(
        body, name=name, grid=(K // tkk, N // tn, nt),
        in_specs=[pl.BlockSpec((tt, tkk), lambda i, j, t: (t, i)),
                  pl.BlockSpec((tt, tn), lambda i, j, t: (t, j))],
        out_specs=out_spec, out_shape=out_shape,
        scratch_shapes=[pltpu.VMEM((tkk, tn), F32)],
        compiler_params=_params(("parallel", "parallel", "arbitrary")),
    )(a, b)


def _rowwise(name, fn, rows, consts, outs, accs=(), tile=256, ncol=1, nrows=None):
    n_r, n_c, n_o = len(rows), len(consts), len(outs)
    T = rows[0][0].shape[0] if nrows is None else nrows
    tile = _pick(T, tuple(t for t in (512, 256, 128, 64, 32, 16, 8) if t <= tile))
    nt = T // tile

    def body(*refs):
        vals = [r[...] for r in refs[:n_r + n_c]]
        res = fn(*vals)
        if not isinstance(res, (tuple, list)):
            res = (res,)
        o_refs = refs[n_r + n_c:n_r + n_c + n_o]
        a_refs = refs[n_r + n_c + n_o:]
        for r, v in zip(o_refs, res[:n_o]):
            r[...] = v.astype(r.dtype)
        first = pl.program_id(1) == 0
        for r, v in zip(a_refs, res[n_o:]):
            @pl.when(first)
            def _(r=r, v=v):
                r[...] = v.astype(F32)

            @pl.when(jnp.logical_not(first))
            def _(r=r, v=v):
                r[...] += v.astype(F32)

    in_specs, args = [], []
    for arr, w, off, roff in rows:
        rb = roff // tile
        assert roff % tile == 0
        in_specs.append(pl.BlockSpec((tile, w), lambda j, i, off=off, rb=rb: (i + rb, off + j)))
        args.append(arr)
    for cst in consts:
        in_specs.append(pl.BlockSpec(cst.shape, lambda j, i: (0, 0)))
        args.append(cst)
    out_specs, out_shape = [], []
    for tw, dt in outs:
        out_specs.append(pl.BlockSpec((tile, tw // ncol), lambda j, i: (i, j)))
        out_shape.append(jax.ShapeDtypeStruct((T, tw), dt))
    for nr, tw in accs:
        out_specs.append(pl.BlockSpec((nr, tw // ncol), lambda j, i: (0, j)))
        out_shape.append(jax.ShapeDtypeStruct((nr, tw), F32))
    res = _tc_call(
        body, name=name, grid=(ncol, nt), in_specs=in_specs, out_specs=out_specs, out_shape=out_shape,
        compiler_params=_params(("parallel", "arbitrary")),
    )(*args)
    return res


def _rms_fwd(name, x, g):
    D = x.shape[1]
    return _rowwise(name, lambda xv, gv: _rms(xv, gv), [(x, D, 0, 0)], [g], [(D, BF16)])[0]


def _rms_bwd(name, x, g, dh, dx_in):
    D = x.shape[1]

    def fn(xv, dhv, dxv, gv):
        _, vjp = jax.vjp(_rms, xv, gv)
        dx, dg = vjp(dhv.astype(F32))
        tot = dx + dxv
        return tot, tot, jnp.sum(dg, axis=0, keepdims=True)

    return _rowwise(name, fn, [(x, D, 0, 0), (dh, D, 0, 0), (dx_in, D, 0, 0)], [g], [(D, F32), (D, BF16)], [(1, D)],
                    tile=256)


def _lag_views(win, K, R, forward):
    n = win.shape[0]
    for r in range(8):
        if r >= K:
            break
        if r == 0:
            rolled = win
        else:
            rolled = pltpu.roll(win, (n - r) if forward else r, axis=0)
        for q in range((K - 1 - r) // 8 + 1):
            s = 8 * q + r
            if forward:
                yield s, rolled[8 * q:8 * q + R]
            else:
                yield s, rolled[HALO - 8 * q:HALO - 8 * q + R]


def _conv_chunk(win, w_ref, K, R):
    acc = None
    for s, view in _lag_views(win, K, R, forward=False):
        term = w_ref[K - 1 - s:K - s, :] * view
        acc = term if acc is None else acc + term
    return acc


def _conv_chunk_t(win, w_ref, K, R):
    acc = None
    for s, view in _lag_views(win, K, R, forward=True):
        term = w_ref[K - 1 - s:K - s, :] * view
        acc = term if acc is None else acc + term
    return acc


def _conv_dw(xwin, dy, K, R):
    taps = [None] * K
    for s, view in _lag_views(xwin, K, R, forward=False):
        taps[K - 1 - s] = jnp.sum(dy * view, axis=0, keepdims=True)
    return taps


def _chunks(T):
    R = _pick(T, (128, 64, 32))
    return R, T // R


def _glu_conv_fwd(name, proj, w, cw_total):
    T = proj.shape[0]
    C = cw_total
    cw = LANE
    nb = C // cw
    R, nch = _chunks(T)

    def body(a_ref, b_ref, w_ref, o_ref, s_ref):
        s_ref[0:HALO, :] = jnp.zeros((HALO, cw), F32)

        def fill(i, _):
            r0 = pl.multiple_of(i * R, R)
            s_ref[pl.ds(HALO + r0, R), :] = a_ref[pl.ds(r0, R), :] * _sigmoid(b_ref[pl.ds(r0, R), :])
            return 0

        lax.fori_loop(0, nch, fill, 0)

        def conv(i, _):
            r0 = pl.multiple_of(i * R, R)
            o_ref[pl.ds(r0, R), :] = _conv_chunk(s_ref[pl.ds(r0, R + HALO), :], w_ref, CONV_K, R)
            return 0

        lax.fori_loop(0, nch, conv, 0)

    return _tc_call(
        body, name=name, grid=(nb,),
        in_specs=[pl.BlockSpec((T, cw), lambda j: (0, j)), pl.BlockSpec((T, cw), lambda j: (0, nb + j)),
                  pl.BlockSpec((HALO, cw), lambda j: (0, j))],
        out_specs=pl.BlockSpec((T, cw), lambda j: (0, j)),
        out_shape=jax.ShapeDtypeStruct((T, C), F32),
        scratch_shapes=[pltpu.VMEM((T + HALO, cw), F32)],
        compiler_params=_params(("parallel",)),
    )(proj, proj, w)


def _glu_conv_bwd(name, proj, w, dhc, cw_total):
    T = proj.shape[0]
    C = cw_total
    cw = LANE
    nb = C // cw
    R, nch = _chunks(T)

    def body(a_ref, b_ref, w_ref, dy_ref, da_ref, db_ref, dw_ref, s_ref, g_ref, acc_ref):
        s_ref[0:HALO, :] = jnp.zeros((HALO, cw), F32)
        g_ref[T:T + HALO, :] = jnp.zeros((HALO, cw), F32)
        acc_ref[...] = jnp.zeros_like(acc_ref)

        def fill(i, _):
            r0 = pl.multiple_of(i * R, R)
            s_ref[pl.ds(HALO + r0, R), :] = a_ref[pl.ds(r0, R), :] * _sigmoid(b_ref[pl.ds(r0, R), :])
            g_ref[pl.ds(r0, R), :] = dy_ref[pl.ds(r0, R), :]
            return 0

        lax.fori_loop(0, nch, fill, 0)

        def back(i, _):
            r0 = pl.multiple_of(i * R, R)
            dhg = _conv_chunk_t(g_ref[pl.ds(r0, R + HALO), :], w_ref, CONV_K, R)
            av = a_ref[pl.ds(r0, R), :]
            sg = _sigmoid(b_ref[pl.ds(r0, R), :])
            da_ref[pl.ds(r0, R), :] = (dhg * sg).astype(da_ref.dtype)
            db_ref[pl.ds(r0, R), :] = (dhg * av * sg * (1.0 - sg)).astype(db_ref.dtype)
            taps = _conv_dw(s_ref[pl.ds(r0, R + HALO), :], dy_ref[pl.ds(r0, R), :], CONV_K, R)
            for k, tap in enumerate(taps):
                acc_ref[k:k + 1, :] += tap
            return 0

        lax.fori_loop(0, nch, back, 0)
        dw_ref[...] = acc_ref[...]

    return _tc_call(
        body, name=name, grid=(nb,),
        in_specs=[pl.BlockSpec((T, cw), lambda j: (0, j)), pl.BlockSpec((T, cw), lambda j: (0, nb + j)),
                  pl.BlockSpec((HALO, cw), lambda j: (0, j)), pl.BlockSpec((T, cw), lambda j: (0, j))],
        out_specs=[pl.BlockSpec((T, cw), lambda j: (0, j)), pl.BlockSpec((T, cw), lambda j: (0, j)),
                   pl.BlockSpec((HALO, cw), lambda j: (0, j))],
        out_shape=[jax.ShapeDtypeStruct((T, C), BF16), jax.ShapeDtypeStruct((T, C), BF16),
                   jax.ShapeDtypeStruct((HALO, C), F32)],
        scratch_shapes=[pltpu.VMEM((T + HALO, cw), F32), pltpu.VMEM((T + HALO, cw), F32),
                        pltpu.VMEM((HALO, cw), F32)],
        compiler_params=_params(("parallel",)),
    )(proj, proj, w, dhc)


def _ffn_conv_fwd(name, up, w, dff):
    T = up.shape[0]
    cw = LANE
    nb = dff // cw
    R, nch = _chunks(T)

    def body(g_ref, v_ref, wg_ref, wv_ref, o_ref, sg_ref, sv_ref):
        sg_ref[0:HALO, :] = jnp.zeros((HALO, cw), F32)
        sv_ref[0:HALO, :] = jnp.zeros((HALO, cw), F32)

        def fill(i, _):
            r0 = pl.multiple_of(i * R, R)
            sg_ref[pl.ds(HALO + r0, R), :] = g_ref[pl.ds(r0, R), :]
            sv_ref[pl.ds(HALO + r0, R), :] = v_ref[pl.ds(r0, R), :]
            return 0

        lax.fori_loop(0, nch, fill, 0)

        def conv(i, _):
            r0 = pl.multiple_of(i * R, R)
            gc = _conv_chunk(sg_ref[pl.ds(r0, R + HALO), :], wg_ref, FFN_K, R)
            vc = _conv_chunk(sv_ref[pl.ds(r0, R + HALO), :], wv_ref, FFN_K, R)
            o_ref[pl.ds(r0, R), :] = (_silu(gc) * vc).astype(o_ref.dtype)
            return 0

        lax.fori_loop(0, nch, conv, 0)

    return _tc_call(
        body, name=name, grid=(nb,),
        in_specs=[pl.BlockSpec((T, cw), lambda j: (0, j)), pl.BlockSpec((T, cw), lambda j: (0, nb + j)),
                  pl.BlockSpec((8, cw), lambda j: (0, j)), pl.BlockSpec((8, cw), lambda j: (0, nb + j))],
        out_specs=pl.BlockSpec((T, cw), lambda j: (0, j)),
        out_shape=jax.ShapeDtypeStruct((T, dff), BF16),
        scratch_shapes=[pltpu.VMEM((T + HALO, cw), F32), pltpu.VMEM((T + HALO, cw), F32)],
        compiler_params=_params(("parallel",)),
    )(up, up, w, w)


def _ffn_conv_bwd(name, up, w, dact, dff):
    T = up.shape[0]
    cw = LANE
    nb = dff // cw
    R, nch = _chunks(T)

    def body(g_ref, v_ref, wg_ref, wv_ref, da_ref, dg_ref, dv_ref, dwg_ref, dwv_ref,
             sg_ref, sv_ref, tg_ref, tv_ref, ag_ref, av_ref):
        zero = jnp.zeros((HALO, cw), F32)
        sg_ref[0:HALO, :] = zero
        sv_ref[0:HALO, :] = zero
        tg_ref[T:T + HALO, :] = zero
        tv_ref[T:T + HALO, :] = zero
        ag_ref[...] = jnp.zeros_like(ag_ref)
        av_ref[...] = jnp.zeros_like(av_ref)

        def fill(i, _):
            r0 = pl.multiple_of(i * R, R)
            sg_ref[pl.ds(HALO + r0, R), :] = g_ref[pl.ds(r0, R), :]
            sv_ref[pl.ds(HALO + r0, R), :] = v_ref[pl.ds(r0, R), :]
            return 0

        lax.fori_loop(0, nch, fill, 0)

        def grads(i, _):
            r0 = pl.multiple_of(i * R, R)
            gwin = sg_ref[pl.ds(r0, R + HALO), :]
            vwin = sv_ref[pl.ds(r0, R + HALO), :]
            gc = _conv_chunk(gwin, wg_ref, FFN_K, R)
            vc = _conv_chunk(vwin, wv_ref, FFN_K, R)
            da = da_ref[pl.ds(r0, R), :].astype(F32)
            sg = _sigmoid(gc)
            dgc = da * vc * (sg * (1.0 + gc * (1.0 - sg)))
            dvc = da * (gc * sg)
            tg_ref[pl.ds(r0, R), :] = dgc
            tv_ref[pl.ds(r0, R), :] = dvc
            for k, tap in enumerate(_conv_dw(gwin, dgc, FFN_K, R)):
                ag_ref[k:k + 1, :] += tap
            for k, tap in enumerate(_conv_dw(vwin, dvc, FFN_K, R)):
                av_ref[k:k + 1, :] += tap
            return 0

        lax.fori_loop(0, nch, grads, 0)

        def back(i, _):
            r0 = pl.multiple_of(i * R, R)
            dg_ref[pl.ds(r0, R), :] = _conv_chunk_t(tg_ref[pl.ds(r0, R + HALO), :], wg_ref, FFN_K, R).astype(dg_ref.dtype)
            dv_ref[pl.ds(r0, R), :] = _conv_chunk_t(tv_ref[pl.ds(r0, R + HALO), :], wv_ref, FFN_K, R).astype(dv_ref.dtype)
            return 0

        lax.fori_loop(0, nch, back, 0)
        dwg_ref[...] = ag_ref[...]
        dwv_ref[...] = av_ref[...]

    col = lambda j: (0, j)
    dg, dv, dwg, dwv = _tc_call(
        body, name=name, grid=(nb,),
        in_specs=[pl.BlockSpec((T, cw), col), pl.BlockSpec((T, cw), lambda j: (0, nb + j)),
                  pl.BlockSpec((8, cw), col), pl.BlockSpec((8, cw), lambda j: (0, nb + j)),
                  pl.BlockSpec((T, cw), col)],
        out_specs=[pl.BlockSpec((T, cw), col), pl.BlockSpec((T, cw), col),
                   pl.BlockSpec((8, cw), col), pl.BlockSpec((8, cw), col)],
        out_shape=[jax.ShapeDtypeStruct((T, dff), BF16), jax.ShapeDtypeStruct((T, dff), BF16),
                   jax.ShapeDtypeStruct((8, dff), F32), jax.ShapeDtypeStruct((8, dff), F32)],
        scratch_shapes=[pltpu.VMEM((T + HALO, cw), F32), pltpu.VMEM((T + HALO, cw), F32),
                        pltpu.VMEM((T + HALO, cw), F32), pltpu.VMEM((T + HALO, cw), F32),
                        pltpu.VMEM((8, cw), F32), pltpu.VMEM((8, cw), F32)],
        compiler_params=_params(("parallel",)),
    )(up, up, w, w, dact)
    return jnp.concatenate([dg, dv], axis=1), jnp.concatenate([dwg, dwv], axis=1)


def _zoh(a_re, a_im, log_dt):
    ar = jnp.minimum(a_re, -1e-4)
    ai = a_im
    dt = jnp.exp(log_dt)
    mag = jnp.exp(dt * ar)
    abar_re = mag * jnp.cos(dt * ai)
    abar_im = mag * jnp.sin(dt * ai)
    den = ar * ar + ai * ai
    nr = abar_re - 1.0
    ni = abar_im
    return abar_re, abar_im, (nr * ar + ni * ai) / den, (ni * ar - nr * ai) / den


def _discretize(a_re, a_im, log_dt, a_re_h, a_im_h, log_dt_h, b_re, b_im):
    abar_re, abar_im, _, _ = _zoh(a_re, a_im, log_dt)
    _, _, z_re, z_im = _zoh(a_re_h, a_im_h, log_dt_h)
    return abar_re, abar_im, z_re * b_re - z_im * b_im, z_re * b_im + z_im * b_re


def _full_specs(arrs):
    return [pl.BlockSpec(a.shape, lambda *_, n=len(a.shape): (0,) * n) for a in arrs]


def _ssm_prep(name, raw):
    def body(*refs):
        res = _discretize(*[r[...] for r in refs[:8]])
        for r, v in zip(refs[8:], res):
            r[...] = v

    outs = [jax.ShapeDtypeStruct(raw[0].shape, F32)] * 2 + [jax.ShapeDtypeStruct(raw[6].shape, F32)] * 2
    return _tc_call(body, name=name, in_specs=_full_specs(raw), out_specs=_full_specs(outs), out_shape=outs)(*raw)


def _ssm_prep_bwd(name, raw, cots):
    G = raw[0].shape[0]
    H = raw[3].shape[0] // G

    def body(*refs):
        _, vjp = jax.vjp(_discretize, *[r[...] for r in refs[:8]])
        g = vjp(tuple(r[...] for r in refs[8:12]))
        outs = refs[12:]
        for k in range(3):
            rep = g[3 + k]
            outs[k][...] = g[k] + jnp.sum(rep.reshape(G, H, rep.shape[1]), axis=1)
        outs[3][...] = g[6]
        outs[4][...] = g[7]

    outs = [jax.ShapeDtypeStruct(a.shape, F32) for a in (raw[0], raw[1], raw[2], raw[6], raw[7])]
    return _tc_call(body, name=name, in_specs=_full_specs(list(raw) + list(cots)), out_specs=_full_specs(outs),
                          out_shape=outs)(*raw, *cots)


def _cmul(ar, ai, br, bi):
    return ar * br - ai * bi, ar * bi + ai * br


def _scan_coefs(ar, ai, reverse):
    W = ar.shape[1]
    row = lax.broadcasted_iota(jnp.int32, (8, W), 0)
    p = [None] * 9
    p[1] = (ar, ai)
    for n in range(2, 9):
        p[n] = _cmul(*p[n // 2], *p[n - n // 2])
    steps = []
    for s in (1, 2, 4):
        valid = (row <= 7 - s) if reverse else (row >= s)
        steps.append((jnp.where(valid, p[s][0], 0.0), jnp.where(valid, p[s][1], 0.0)))
    pr = jnp.zeros((8, W), F32)
    pi = jnp.zeros((8, W), F32)
    for i in range(8):
        n = (8 - i) if reverse else (i + 1)
        pr = jnp.where(row == i, p[n][0], pr)
        pi = jnp.where(row == i, p[n][1], pi)
    return steps, (pr, pi)


def _scan_tile(xr, xi, cr, ci, coefs, reverse):
    steps, (pr, pi) = coefs
    for s, (sr, si) in zip((1, 2, 4), steps):
        shift = (8 - s) if reverse else s
        rr = pltpu.roll(xr, shift, axis=0)
        ri = pltpu.roll(xi, shift, axis=0)
        xr, xi = xr + sr * rr - si * ri, xi + sr * ri + si * rr
    xr, xi = xr + pr * cr - pi * ci, xi + pr * ci + pi * cr
    return xr, xi


def _edge_rows(x, reverse):
    W = x.shape[1]
    return jnp.broadcast_to(x[0:1, :] if reverse else x[7:8, :], (8, W))


def _ssm_chunk(T):
    return _pick(T, (256, 128, 64))


def _ssm_fwd(name, proj, u_off, p, width):
    T = proj.shape[0]
    NB = width // LANE
    Q = _ssm_chunk(T)
    nch = T // Q
    W = SSM_LANES

    def body(u_ref, bre, bim, cre, cim, ar_ref, ai_ref, d_ref, y_ref, ckr_ref, cki_ref, br_s, bi_s, car_r, car_i):
        c = pl.program_id(1)

        @pl.when(c == 0)
        def _():
            car_r[...] = jnp.zeros_like(car_r)
            car_i[...] = jnp.zeros_like(car_i)

        ckr_ref[...] = car_r[...]
        cki_ref[...] = car_i[...]
        u = u_ref[...]
        u16 = u.astype(BF16)
        br_s[...] = jnp.dot(u16, bre[...], preferred_element_type=F32)
        bi_s[...] = jnp.dot(u16, bim[...], preferred_element_type=F32)
        coefs = _scan_coefs(ar_ref[...], ai_ref[...], False)

        def tile(j, carry):
            r0 = pl.multiple_of(j * 8, 8)
            xr, xi = _scan_tile(br_s[pl.ds(r0, 8), :], bi_s[pl.ds(r0, 8), :], carry[0], carry[1], coefs, False)
            br_s[pl.ds(r0, 8), :] = xr
            bi_s[pl.ds(r0, 8), :] = xi
            return _edge_rows(xr, False), _edge_rows(xi, False)

        cr, ci = lax.fori_loop(0, Q // 8, tile, (car_r[...], car_i[...]))
        car_r[...] = cr
        car_i[...] = ci
        nt = (((1,), (1,)), ((), ()))
        y = (lax.dot_general(br_s[...].astype(BF16), cre[...], nt, preferred_element_type=F32)
             - lax.dot_general(bi_s[...].astype(BF16), cim[...], nt, preferred_element_type=F32)
             + d_ref[...] * u)
        y_ref[...] = _gelu(y).astype(y_ref.dtype)

    blk = lambda b, c: (b, 0, 0)
    mat = pl.BlockSpec((None, LANE, W), blk)
    vec = pl.BlockSpec((None, 1, W), blk)
    ck = pl.BlockSpec((None, None, 8, W), lambda b, c: (b, c, 0, 0))
    return _tc_call(
        body, name=name, grid=(NB, nch),
        in_specs=[pl.BlockSpec((Q, LANE), lambda b, c: (c, u_off + b)), mat, mat, mat, mat, vec, vec,
                  pl.BlockSpec((1, LANE), lambda b, c: (0, b))],
        out_specs=[pl.BlockSpec((Q, LANE), lambda b, c: (c, b)), ck, ck],
        out_shape=[jax.ShapeDtypeStruct((T, width), BF16), jax.ShapeDtypeStruct((NB, nch, 8, W), F32),
                   jax.ShapeDtypeStruct((NB, nch, 8, W), F32)],
        scratch_shapes=[pltpu.VMEM((Q, W), F32), pltpu.VMEM((Q, W), F32), pltpu.VMEM((8, W), F32),
                        pltpu.VMEM((8, W), F32)],
        compiler_params=_params(("parallel", "arbitrary")),
    )(proj, p["bre"], p["bim"], p["cre"], p["cim"], p["ar"], p["ai"], p["d"])


def _ssm_bwd(name, proj, u_off, p, ck_r, ck_i, dyg, width):
    T = proj.shape[0]
    NB = width // LANE
    Q = _ssm_chunk(T)
    nch = T // Q
    W = SSM_LANES
    nt_dims = (((1,), (1,)), ((), ()))
    tn_dims = (((0,), (0,)), ((), ()))

    def body(u_ref, dy_ref, ckr_ref, cki_ref, bre, bim, cre, cim, ar_ref, ai_ref, d_ref,
             du_ref, dbr_ref, dbi_ref, dcr_ref, dci_ref, dar_ref, dai_ref, dd_ref,
             xr_s, xi_s, lr_s, li_s, lam_r, lam_i):
        c = pl.program_id(1)

        @pl.when(c == 0)
        def _():
            lam_r[...] = jnp.zeros_like(lam_r)
            lam_i[...] = jnp.zeros_like(lam_i)
            for r in (dbr_ref, dbi_ref, dcr_ref, dci_ref, dar_ref, dai_ref, dd_ref):
                r[...] = jnp.zeros_like(r)

        u = u_ref[...]
        u16 = u.astype(BF16)
        ar, ai = ar_ref[...], ai_ref[...]
        xr_s[0:8, :] = ckr_ref[...]
        xi_s[0:8, :] = cki_ref[...]
        xr_s[8:Q + 8, :] = jnp.dot(u16, bre[...], preferred_element_type=F32)
        xi_s[8:Q + 8, :] = jnp.dot(u16, bim[...], preferred_element_type=F32)
        fcoefs = _scan_coefs(ar, ai, False)

        def ftile(j, carry):
            r0 = pl.multiple_of(j * 8 + 8, 8)
            xr, xi = _scan_tile(xr_s[pl.ds(r0, 8), :], xi_s[pl.ds(r0, 8), :], carry[0], carry[1], fcoefs, False)
            xr_s[pl.ds(r0, 8), :] = xr
            xi_s[pl.ds(r0, 8), :] = xi
            return _edge_rows(xr, False), _edge_rows(xi, False)

        lax.fori_loop(0, Q // 8, ftile, (ckr_ref[...], cki_ref[...]))
        xr16 = xr_s[8:Q + 8, :].astype(BF16)
        xi16 = xi_s[8:Q + 8, :].astype(BF16)
        y = (lax.dot_general(xr16, cre[...], nt_dims, preferred_element_type=F32)
             - lax.dot_general(xi16, cim[...], nt_dims, preferred_element_type=F32) + d_ref[...] * u)
        _, gelu_vjp = jax.vjp(_gelu, y)
        dy = gelu_vjp(dy_ref[...].astype(F32))[0]
        dy16 = dy.astype(BF16)
        dd_ref[...] += jnp.broadcast_to(jnp.sum(dy * u, axis=0, keepdims=True), (8, LANE))
        dcr_ref[...] += lax.dot_general(dy16, xr16, tn_dims, preferred_element_type=F32)
        dci_ref[...] -= lax.dot_general(dy16, xi16, tn_dims, preferred_element_type=F32)
        lr_s[...] = jnp.dot(dy16, cre[...], preferred_element_type=F32)
        li_s[...] = -jnp.dot(dy16, cim[...], preferred_element_type=F32)
        rcoefs = _scan_coefs(ar, -ai, True)
        row = lax.broadcasted_iota(jnp.int32, (8, W), 0)

        def rtile(jj, carry):
            j = Q // 8 - 1 - jj
            r0 = pl.multiple_of(j * 8, 8)
            lr, li = _scan_tile(lr_s[pl.ds(r0, 8), :], li_s[pl.ds(r0, 8), :], carry[0], carry[1], rcoefs, True)
            lr_s[pl.ds(r0, 8), :] = lr
            li_s[pl.ds(r0, 8), :] = li
            cur_r, cur_i = xr_s[pl.ds(r0 + 8, 8), :], xi_s[pl.ds(r0 + 8, 8), :]
            prv_r, prv_i = xr_s[pl.ds(r0, 8), :], xi_s[pl.ds(r0, 8), :]
            xpr = jnp.where(row == 0, _edge_rows(prv_r, False), pltpu.roll(cur_r, 1, axis=0))
            xpi = jnp.where(row == 0, _edge_rows(prv_i, False), pltpu.roll(cur_i, 1, axis=0))
            return (_edge_rows(lr, True), _edge_rows(li, True),
                    carry[2] + lr * xpr + li * xpi, carry[3] + li * xpr - lr * xpi)

        zero = jnp.zeros((8, W), F32)
        cr, ci, sar, sai = lax.fori_loop(0, Q // 8, rtile, (lam_r[...], lam_i[...], zero, zero))
        lam_r[...] = cr
        lam_i[...] = ci
        dar_ref[...] += jnp.broadcast_to(jnp.sum(sar, axis=0, keepdims=True), (8, W))
        dai_ref[...] += jnp.broadcast_to(jnp.sum(sai, axis=0, keepdims=True), (8, W))
        lr16 = lr_s[...].astype(BF16)
        li16 = li_s[...].astype(BF16)
        dbr_ref[...] += lax.dot_general(u16, lr16, tn_dims, preferred_element_type=F32)
        dbi_ref[...] += lax.dot_general(u16, li16, tn_dims, preferred_element_type=F32)
        du = (lax.dot_general(lr16, bre[...], nt_dims, preferred_element_type=F32)
              + lax.dot_general(li16, bim[...], nt_dims, preferred_element_type=F32) + d_ref[...] * dy)
        du_ref[...] = du.astype(du_ref.dtype)

    blk = lambda b, c: (b, 0, 0)
    mat = pl.BlockSpec((None, LANE, W), blk)
    vec = pl.BlockSpec((None, 1, W), blk)
    acc8 = pl.BlockSpec((None, 8, W), blk)
    ck = pl.BlockSpec((None, None, 8, W), lambda b, c: (b, nch - 1 - c, 0, 0))
    return _tc_call(
        body, name=name, grid=(NB, nch),
        in_specs=[pl.BlockSpec((Q, LANE), lambda b, c: (nch - 1 - c, u_off + b)),
                  pl.BlockSpec((Q, LANE), lambda b, c: (nch - 1 - c, b)), ck, ck, mat, mat, mat, mat, vec, vec,
                  pl.BlockSpec((1, LANE), lambda b, c: (0, b))],
        out_specs=[pl.BlockSpec((Q, LANE), lambda b, c: (nch - 1 - c, b)), mat, mat, mat, mat, acc8, acc8,
                   pl.BlockSpec((None, 8, LANE), blk)],
        out_shape=[jax.ShapeDtypeStruct((T, width), BF16)] + [jax.ShapeDtypeStruct((NB, LANE, W), F32)] * 4
                  + [jax.ShapeDtypeStruct((NB, 8, W), F32)] * 2 + [jax.ShapeDtypeStruct((NB, 8, LANE), F32)],
        scratch_shapes=[pltpu.VMEM((Q + 8, W), F32), pltpu.VMEM((Q + 8, W), F32), pltpu.VMEM((Q, W), F32),
                        pltpu.VMEM((Q, W), F32), pltpu.VMEM((8, W), F32), pltpu.VMEM((8, W), F32)],
        compiler_params=_params(("parallel", "arbitrary")),
    )(proj, dyg, ck_r, ck_i, p["bre"], p["bim"], p["cre"], p["cim"], p["ar"], p["ai"], p["d"])


def _block_diag(w):
    G, H, P = w.shape
    eye = jnp.eye(8, dtype=w.dtype)
    return (w.reshape(G // 8, 8, H, 1, P) * eye[None, :, None, :, None]).reshape(G // 8, 8 * H, 8 * P)


def _block_diag_t(d, H, P):
    NB = d.shape[0]
    d = d.reshape(NB, 8, H, 8, P)
    eye = jnp.eye(8, dtype=d.dtype)
    return jnp.sum(d * eye[None, :, None, :, None], axis=3).reshape(NB * 8, H, P)


def _attn_fwd(name, q, kv, heads):
    T, D = q.shape
    Mm = kv.shape[0]
    hd = D // heads
    tq = _pick(T, (512, 256, 128))
    scale = hd ** -0.5

    def body(q_ref, k_ref, v_ref, o_ref):
        s = lax.dot_general(q_ref[...], k_ref[...], (((1,), (1,)), ((), ())), preferred_element_type=F32) * scale
        s = s - jnp.max(s, axis=-1, keepdims=True)
        e = jnp.exp(s)
        p = e / jnp.sum(e, axis=-1, keepdims=True)
        o_ref[...] = jnp.dot(p.astype(BF16), v_ref[...], preferred_element_type=F32).astype(o_ref.dtype)

    return _tc_call(
        body, name=name, grid=(heads, T // tq),
        in_specs=[pl.BlockSpec((tq, hd), lambda h, i: (i, h)), pl.BlockSpec((Mm, hd), lambda h, i: (0, h)),
                  pl.BlockSpec((Mm, hd), lambda h, i: (0, heads + h))],
        out_specs=pl.BlockSpec((tq, hd), lambda h, i: (i, h)),
        out_shape=jax.ShapeDtypeStruct((T, D), BF16),
        compiler_params=_params(("parallel", "parallel")),
    )(q, kv, kv)


def _attn_bwd(name, q, kv, do, heads):
    T, D = q.shape
    Mm = kv.shape[0]
    hd = D // heads
    tq = _pick(T, (512, 256, 128))
    scale = hd ** -0.5
    nt_dims = (((1,), (1,)), ((), ()))
    tn_dims = (((0,), (0,)), ((), ()))

    def body(q_ref, k_ref, v_ref, do_ref, dq_ref, dk_ref, dv_ref):
        i = pl.program_id(1)

        @pl.when(i == 0)
        def _():
            dk_ref[...] = jnp.zeros_like(dk_ref)
            dv_ref[...] = jnp.zeros_like(dv_ref)

        qv, kvl, vv, dov = q_ref[...], k_ref[...], v_ref[...], do_ref[...]
        s = lax.dot_general(qv, kvl, nt_dims, preferred_element_type=F32) * scale
        s = s - jnp.max(s, axis=-1, keepdims=True)
        e = jnp.exp(s)
        p = e / jnp.sum(e, axis=-1, keepdims=True)
        p16 = p.astype(BF16)
        dv_ref[...] += lax.dot_general(p16, dov, tn_dims, preferred_element_type=F32)
        dp = lax.dot_general(dov, vv, nt_dims, preferred_element_type=F32)
        ds = (p * (dp - jnp.sum(dp * p, axis=-1, keepdims=True)) * scale).astype(BF16)
        dq_ref[...] = jnp.dot(ds, kvl, preferred_element_type=F32).astype(dq_ref.dtype)
        dk_ref[...] += lax.dot_general(ds, qv, tn_dims, preferred_element_type=F32)

    return _tc_call(
        body, name=name, grid=(heads, T // tq),
        in_specs=[pl.BlockSpec((tq, hd), lambda h, i: (i, h)), pl.BlockSpec((Mm, hd), lambda h, i: (0, h)),
                  pl.BlockSpec((Mm, hd), lambda h, i: (0, heads + h)), pl.BlockSpec((tq, hd), lambda h, i: (i, h))],
        out_specs=[pl.BlockSpec((tq, hd), lambda h, i: (i, h)), pl.BlockSpec((Mm, hd), lambda h, i: (0, h)),
                   pl.BlockSpec((Mm, hd), lambda h, i: (0, h))],
        out_shape=[jax.ShapeDtypeStruct((T, D), BF16), jax.ShapeDtypeStruct((Mm, D), F32),
                   jax.ShapeDtypeStruct((Mm, D), F32)],
        compiler_params=_params(("parallel", "arbitrary")),
    )(q, kv, kv, do)


def _position():
    return lax.axis_index("x"), lax.axis_index("y"), lax.axis_index("c")


def _allgather(name, blk, row_mode):
    L = blk.shape[0]
    out_shape = (L, N_DEV) + blk.shape[1:] if row_mode else (N_DEV,) + blk.shape
    x_ref = jax.new_ref(blk, memory_space=pltpu.MemorySpace.HBM)
    out_ref = jax.empty_ref(jax.ShapeDtypeStruct(out_shape, blk.dtype), memory_space=pltpu.MemorySpace.HBM)

    def body(send_sems, recv_sems, local_sem):
        x, y, c = _position()
        me, sibling = (x, y, c), (x, y, 1 - c)
        chips = [(1 - x, y), (x, 1 - y), (1 - x, 1 - y)]
        barrier = pltpu.get_barrier_semaphore()
        for peer in [sibling] + [(*chip, c) for chip in chips]:
            pl.semaphore_signal(barrier, inc=1, device_id=peer, device_id_type=MESH_ID)
        pl.semaphore_wait(barrier, 4)

        def slot(px, py, pc):
            b = 4 * px + 2 * py + pc
            return out_ref.at[:, b] if row_mode else out_ref.at[b]

        def copy(k, block, to, src=None):
            return pltpu.make_async_remote_copy(
                src_ref=slot(*block) if src is None else src, dst_ref=slot(*block),
                send_sem=send_sems.at[k], recv_sem=recv_sems.at[k], device_id=to, device_id_type=MESH_ID)

        mine = pltpu.make_async_copy(x_ref, slot(*me), local_sem)
        mine.start()
        first = [copy(0, me, sibling, src=x_ref)]
        first += [copy(1 + j, me, (*chip, c), src=x_ref) for j, chip in enumerate(chips)]
        for cp in first:
            cp.start()
        passed = [copy(4 + j, (*chip, c), sibling) for j, chip in enumerate(chips)]
        for j, chip in enumerate(chips):
            copy(1 + j, (*chip, c), me).wait_recv()
            passed[j].start()
        copy(0, sibling, me).wait_recv()
        for j, chip in enumerate(chips):
            copy(4 + j, (*chip, 1 - c), me).wait_recv()
        for cp in first + passed:
            cp.wait_send()
        mine.wait()

    pl.kernel(
        body, mesh=plsc.ScalarSubcoreMesh(axis_name="sequencer", num_cores=1), name=name,
        scratch_types=(pltpu.SemaphoreType.DMA((7,)), pltpu.SemaphoreType.DMA((7,)), pltpu.SemaphoreType.DMA),
        compiler_params=pltpu.CompilerParams(collective_id=AG_COLLECTIVE_ID),
    )()
    return out_ref[...]


def _sequencer_kernel(name, body, scratch_types, collective_id):
    pl.kernel(
        body, mesh=plsc.ScalarSubcoreMesh(axis_name="sequencer", num_cores=1), name=name,
        scratch_types=scratch_types, compiler_params=pltpu.CompilerParams(collective_id=collective_id),
    )()


def _handshake(peers):
    barrier = pltpu.get_barrier_semaphore()
    for peer in peers:
        pl.semaphore_signal(barrier, inc=1, device_id=peer, device_id_type=MESH_ID)
    pl.semaphore_wait(barrier, len(peers))


def _rs_sibling(name, grads):
    hbm = pltpu.MemorySpace.HBM
    g_ref = jax.new_ref(grads, memory_space=hbm)
    out_ref = jax.empty_ref(jax.ShapeDtypeStruct((4,) + grads.shape[1:], grads.dtype), memory_space=hbm)

    def body(send_sems, recv_sems):
        x, y, c = _position()
        sibling = (x, y, 1 - c)
        _handshake([sibling])
        copies = [pltpu.make_async_remote_copy(
            src_ref=g_ref.at[2 * q + (1 - c)], dst_ref=out_ref.at[q], send_sem=send_sems.at[q],
            recv_sem=recv_sems.at[q], device_id=sibling, device_id_type=MESH_ID) for q in range(4)]
        for cp in copies:
            cp.start()
        for cp in copies:
            cp.wait_recv()
        for cp in copies:
            cp.wait_send()

    _sequencer_kernel(name, body, (pltpu.SemaphoreType.DMA((4,)), pltpu.SemaphoreType.DMA((4,))),
                      RS_SIBLING_COLLECTIVE_ID)
    return out_ref[...]


def _rs_chips(name, part):
    hbm = pltpu.MemorySpace.HBM
    p_ref = jax.new_ref(part, memory_space=hbm)
    out_ref = jax.empty_ref(jax.ShapeDtypeStruct(part.shape, part.dtype), memory_space=hbm)

    def body(send_sems, recv_sems, local_sem):
        x, y, c = _position()
        flips = [(1 - x, y), (x, 1 - y), (1 - x, 1 - y)]
        _handshake([(fx, fy, c) for fx, fy in flips])
        mine = pltpu.make_async_copy(p_ref.at[2 * x + y], out_ref.at[3], local_sem)
        copies = [pltpu.make_async_remote_copy(
            src_ref=p_ref.at[2 * fx + fy], dst_ref=out_ref.at[k], send_sem=send_sems.at[k], recv_sem=recv_sems.at[k],
            device_id=(fx, fy, c), device_id_type=MESH_ID) for k, (fx, fy) in enumerate(flips)]
        for cp in copies:
            cp.start()
        mine.start()
        for cp in copies:
            cp.wait_recv()
        for cp in copies:
            cp.wait_send()
        mine.wait()

    _sequencer_kernel(name, body, (pltpu.SemaphoreType.DMA((3,)), pltpu.SemaphoreType.DMA((3,)),
                                   pltpu.SemaphoreType.DMA), RS_CHIPS_COLLECTIVE_ID)
    return out_ref[...]


def _adamw_math(g, w, m, v):
    m = ADAM_B1 * m + (1.0 - ADAM_B1) * g
    v = ADAM_B2 * v + (1.0 - ADAM_B2) * (g * g)
    m_hat = m / (1.0 - ADAM_B1 ** ADAM_STEP)
    v_hat = v / (1.0 - ADAM_B2 ** ADAM_STEP)
    delta = -ADAM_LR * (m_hat / (jnp.sqrt(v_hat) + ADAM_EPS) + ADAM_WD * w)
    return delta, m, v


def _sum_adamw(name, parts, nparts, w, m, v):
    R, C = w.shape
    tile = _ew_tile(R, C, nparts + 7)

    def fn(*vals):
        g = vals[0].astype(F32)
        for pv in vals[1:nparts]:
            g = g + pv.astype(F32)
        delta, nm, nv = _adamw_math(g, *vals[nparts:])
        return g, delta, nm, nv

    rows = [(parts, C, 0, k * R) for k in range(nparts)] + [(a, C, 0, 0) for a in (w, m, v)]
    return _rowwise(name, fn, rows, [], [(C, F32)] * 4, tile=tile, nrows=R)


def _ew_tile(R, C, nblocks):
    budget = (VMEM_LIMIT * 3) // 4
    return _pick(R, tuple(t for t in (1024, 512, 256, 128, 64, 32, 16, 8) if 8 * t * C * nblocks <= budget))


def _pair_sum(name, grads, landed, c_idx):
    _, r, c = grads.shape
    tile = _ew_tile(r, c, 3)

    def body(c_ref, g_ref, s_ref, o_ref):
        o_ref[...] = (g_ref[...].astype(F32) + s_ref[...].astype(F32)).astype(o_ref.dtype)

    return _tc_call(
        body, name=name,
        grid_spec=pltpu.PrefetchScalarGridSpec(
            num_scalar_prefetch=1, grid=(4, r // tile),
            in_specs=[pl.BlockSpec((None, tile, c), lambda q, i, c_ref: (2 * q + c_ref[0], i, 0)),
                      pl.BlockSpec((None, tile, c), lambda q, i, c_ref: (q, i, 0))],
            out_specs=pl.BlockSpec((None, tile, c), lambda q, i, c_ref: (q, i, 0))),
        out_shape=jax.ShapeDtypeStruct((4, r, c), BF16),
        compiler_params=_params(("parallel", "parallel")),
    )(c_idx, grads, landed)


def _adamw_layer(name, got, w, m, v, layer, prev):
    L, r, c = w.shape
    tile = _ew_tile(r, c, 11)

    def body(g0, g1, g2, g3, w_ref, m_ref, v_ref, *rest):
        outs = rest[-4:]
        g = (g0[...].astype(F32) + g1[...].astype(F32)) + (g2[...].astype(F32) + g3[...].astype(F32))
        delta, nm, nv = _adamw_math(g, w_ref[...], m_ref[...], v_ref[...])
        for ref, val in zip(outs, (g, delta, nm, nv)):
            ref[...] = val

    slab = pl.BlockSpec((None, tile, c), lambda i: (layer, i, 0))
    in_specs = [pl.BlockSpec((None, tile, c), lambda i, k=k: (k, i, 0)) for k in range(4)] + [slab] * 3
    args = [got, got, got, got, w, m, v]
    aliases = {}
    if prev is not None:
        in_specs += [ANY] * 4
        args += list(prev)
        aliases = {7 + k: k for k in range(4)}
    return _tc_call(
        body, name=name, grid=(r // tile,), in_specs=in_specs, out_specs=[slab] * 4,
        out_shape=[jax.ShapeDtypeStruct((L, r, c), F32)] * 4, input_output_aliases=aliases,
        compiler_params=_params(("parallel",)),
    )(*args)


def _reduce_pipeline(name, layer, grads, w, m, v, state):
    c_idx = lax.axis_index("c").astype(jnp.int32).reshape(1)
    landed = _rs_sibling(name + "_rs_sibling", grads)
    yield
    part = _pair_sum(name + "_pair_sum", grads, landed, c_idx)
    got = _rs_chips(name + "_rs_chips", part)
    yield
    yield
    state[name] = _adamw_layer(name + "_adamw", got, w, m, v, layer, state.get(name))


def _loss_head(name, x, g, target):
    T, D = x.shape

    def fn(xv, tv, gv):
        def f(xx, gg):
            err = _rms(xx, gg) - tv
            return 0.5 * jnp.sum(jnp.mean(err * err, axis=-1, keepdims=True))

        loss, (dx, dg) = jax.value_and_grad(f, argnums=(0, 1))(xv, gv)
        return dx, dx, jnp.full((8, LANE), loss, F32), jnp.sum(dg, axis=0, keepdims=True)

    dx, dx16, loss, dg = _rowwise(name, fn, [(x, D, 0, 0), (target, D, 0, 0)], [g], [(D, F32), (D, BF16)],
                                  [(8, LANE), (1, D)], tile=128)
    return loss[0, 0], dx, dx16, dg


_SHARDED_COL = ("w_in", "conv_dw_w", "conv_w_pw", "ssm_w_glu", "xa_w_kv", "ffn_w_up", "ffn_dw_w")
_SHARDED_ROW = ("w_out", "xa_w_q", "xa_w_o", "ffn_w_down")
_WEIGHTS = ['mix_norm_g', 'w_in', 'conv_dw_w', 'conv_dw_b', 'conv_ln_g', 'conv_ln_b', 'conv_w_pw', 'ssm_a_re',
            'ssm_a_im', 'ssm_log_dt', 'ssm_b_re', 'ssm_b_im', 'ssm_c_re', 'ssm_c_im', 'ssm_d', 'ssm_w_glu', 'w_out',
            'xa_norm_g', 'mem_norm_g', 'xa_w_q', 'xa_w_kv', 'xa_w_o', 'ffn_norm_g', 'ffn_w_up', 'ffn_dw_w',
            'ffn_w_down', 'final_norm_g']
_FWD = ['x', 'mem'] + _WEIGHTS


def _pad_rows(a, rows):
    return jnp.pad(a, ((0, 0), (0, rows - a.shape[1]), (0, 0)))


def _step(inp, target, mom_m, mom_v):
    _LAST_CALL.clear()
    x0 = inp["x"][0]
    mem = inp["mem"][0]
    T, D = x0.shape
    L = inp["w_in"].shape[0]
    CW = inp["conv_dw_b"].shape[1]
    SW = inp["ssm_d"].shape[1]
    DFF = inp["ffn_w_down"].shape[1] * N_DEV
    G = SW // SSM_GROUP
    NB = SW // LANE
    u_off = (2 * CW) // LANE
    gate_off = (2 * CW + SW) // 1024

    gathered = {}
    for n in _SHARDED_COL:
        small = n in ("conv_dw_w", "ffn_dw_w")
        blk = inp[n] if small else inp[n].astype(BF16)
        if n == "conv_dw_w":
            blk = _pad_rows(blk, HALO)
        if n == "ffn_dw_w":
            blk = _pad_rows(blk, 8)
        gathered[n] = _allgather("ag_" + n, blk, False)
    for n in _SHARDED_ROW:
        full = _allgather("ag_" + n, inp[n].astype(BF16), True)
        gathered[n] = full.reshape(L, N_DEV * full.shape[2], full.shape[3])

    def W(n, l):
        return _W(gathered[n], l, n in _SHARDED_COL)

    def dw_filter(n, l):
        g = gathered[n]
        return jnp.transpose(g[:, l], (1, 0, 2)).reshape(g.shape[2], N_DEV * g.shape[3])

    def row(n, l):
        return inp[n][l][None, :]

    ssm_raw, ssm_p = [], []
    for l in range(L):
        a_re, a_im, ldt = inp["ssm_a_re"][l], inp["ssm_a_im"][l], inp["ssm_log_dt"][l][:, None]
        rep = lambda a: jnp.repeat(a, SSM_GROUP, axis=0)
        flat = lambda b: jnp.transpose(b, (0, 2, 1)).reshape(G * SSM_GROUP, SSM_STATE)
        raw = (a_re, a_im, ldt, rep(a_re), rep(a_im), rep(ldt), flat(inp["ssm_b_re"][l]), flat(inp["ssm_b_im"][l]))
        abar_re, abar_im, bbar_re, bbar_im = _ssm_prep("ssm_prep", raw)
        bbar_re = bbar_re.reshape(G, SSM_GROUP, SSM_STATE)
        bbar_im = bbar_im.reshape(G, SSM_GROUP, SSM_STATE)
        ssm_raw.append(raw)
        ssm_p.append(dict(
            bre=_block_diag(bbar_re).astype(BF16), bim=_block_diag(bbar_im).astype(BF16),
            cre=_block_diag(inp["ssm_c_re"][l]).astype(BF16), cim=_block_diag(inp["ssm_c_im"][l]).astype(BF16),
            ar=abar_re.reshape(NB, 1, SSM_LANES), ai=abar_im.reshape(NB, 1, SSM_LANES), d=row("ssm_d", l)))

    saved = []
    x = x0
    for l in range(L):
        s = {"x_in": x}
        s["h1"] = _rms_fwd("rms_mix", x, row("mix_norm_g", l))
        s["proj"] = _mm_nn("mm_w_in", s["h1"], W("w_in", l), F32)
        s["hc"] = _glu_conv_fwd("conv_fwd", s["proj"], dw_filter("conv_dw_w", l), CW)
        s["hs"] = _rowwise("conv_post", _convpost, [(s["hc"], CW, 0, 0)],
                           [row("conv_dw_b", l), row("conv_ln_g", l), row("conv_ln_b", l)], [(CW, BF16)])[0]
        s["ya"] = _mm_nn("mm_w_pw", s["hs"], W("conv_w_pw", l), F32)
        s["yg"], s["ck_r"], s["ck_i"] = _ssm_fwd("ssm_fwd", s["proj"], u_off, ssm_p[l], SW)
        s["gg"] = _mm_nn("mm_w_glu", s["yg"], W("ssm_w_glu", l), F32)
        nmix = D // 1024
        mix_rows = [(s["proj"], 1024, gate_off, 0), (s["proj"], 1024, gate_off + nmix, 0), (s["ya"], 1024, 0, 0),
                    (s["gg"], 1024, 0, 0), (s["gg"], 1024, nmix, 0)]
        s["mix_rows"] = mix_rows
        s["mix"] = _rowwise("mix_fwd", _mixf, mix_rows, [], [(D, BF16)], ncol=nmix)[0]
        x = _mm_nn("mm_w_out", s["mix"], W("w_out", l), F32, add=x)
        s["x1"] = x
        s["h2"] = _rms_fwd("rms_xa", x, row("xa_norm_g", l))
        s["q"] = _mm_nn("mm_w_q", s["h2"], W("xa_w_q", l), BF16)
        s["mn"] = _rms_fwd("rms_mem", mem, row("mem_norm_g", l))
        s["kv"] = _mm_nn("mm_w_kv", s["mn"], W("xa_w_kv", l), BF16)
        s["o"] = _attn_fwd("attn_fwd", s["q"], s["kv"], XA_HEADS)
        x = _mm_nn("mm_w_o", s["o"], W("xa_w_o", l), F32, add=x)
        s["x2"] = x
        s["h3"] = _rms_fwd("rms_ffn", x, row("ffn_norm_g", l))
        s["up"] = _mm_nn("mm_w_up", s["h3"], W("ffn_w_up", l), F32)
        s["act"] = _ffn_conv_fwd("ffn_conv_fwd", s["up"], dw_filter("ffn_dw_w", l), DFF)
        x = _mm_nn("mm_w_down", s["act"], W("ffn_w_down", l), F32, add=x)
        saved.append(s)

    loss_part, dx, dx16, d_final_g = _loss_head("loss_head", x, inp["final_norm_g"][None, :], target[0])

    big = _SHARDED_COL + _SHARDED_ROW
    small = {n: [None] * L for n in _WEIGHTS if n not in big and n != "final_norm_g"}
    pads = {"conv_dw_w": HALO, "ffn_dw_w": 8}
    shards = {n: tuple(_pad_rows(a, pads[n]) if n in pads else a for a in (inp[n], mom_m[n], mom_v[n])) for n in big}
    state, queue = {}, []

    def tick():
        for gen in list(queue):
            if next(gen, "done") == "done":
                queue.remove(gen)

    def emit(n, l, g):
        queue.append(_reduce_pipeline(n, l, g.astype(BF16), *shards[n], state))
        tick()

    for l in reversed(range(L)):
        s = saved[l]
        dact = _mm_nt("mm_w_down_t", dx16, W("ffn_w_down", l), BF16)
        emit("ffn_w_down", l, _mm_tn("mm_dw_down", s["act"], dx16).reshape(N_DEV, DFF // N_DEV, D))
        d_up, d_ffn_dw = _ffn_conv_bwd("ffn_conv_bwd", s["up"], dw_filter("ffn_dw_w", l), dact, DFF)
        emit("ffn_dw_w", l, jnp.transpose(d_ffn_dw.reshape(8, N_DEV, 2 * DFF // N_DEV), (1, 0, 2)))
        dh3 = _mm_nt("mm_w_up_t", d_up, W("ffn_w_up", l), BF16)
        emit("ffn_w_up", l, _mm_tn("mm_dw_up", s["h3"], d_up, nb=2 * DFF // N_DEV))
        dx, dx16, small["ffn_norm_g"][l] = _rms_bwd("rms_ffn_bwd", s["x2"], row("ffn_norm_g", l), dh3, dx)
        do = _mm_nt("mm_w_o_t", dx16, W("xa_w_o", l), BF16)
        emit("xa_w_o", l, _mm_tn("mm_dw_o", s["o"], dx16).reshape(N_DEV, D // N_DEV, D))
        dq, dk, dv = _attn_bwd("attn_bwd", s["q"], s["kv"], do, XA_HEADS)
        dkv = jnp.concatenate([dk, dv], axis=1).astype(BF16)
        dh2 = _mm_nt("mm_w_q_t", dq, W("xa_w_q", l), BF16)
        emit("xa_w_q", l, _mm_tn("mm_dw_q", s["h2"], dq).reshape(N_DEV, D // N_DEV, D))
        dmn = _mm_nt("mm_w_kv_t", dkv, W("xa_w_kv", l), BF16)
        emit("xa_w_kv", l, _mm_tn("mm_dw_kv", s["mn"], dkv, nb=2 * D // N_DEV))

        def mem_bwd(mv, dv_, gv):
            _, vjp = jax.vjp(_rms, mv, gv)
            return jnp.sum(vjp(dv_.astype(F32))[1], axis=0, keepdims=True)

        small["mem_norm_g"][l] = _rowwise("rms_mem_bwd", mem_bwd, [(mem, D, 0, 0), (dmn, D, 0, 0)],
                                          [row("mem_norm_g", l)], [], [(1, D)], tile=128)[0]
        dx, dx16, small["xa_norm_g"][l] = _rms_bwd("rms_xa_bwd", s["x1"], row("xa_norm_g", l), dh2, dx)
        dmix = _mm_nt("mm_w_out_t", dx16, W("w_out", l), BF16)
        emit("w_out", l, _mm_tn("mm_dw_out", s["mix"], dx16).reshape(N_DEV, D // N_DEV, D))

        def mix_bwd(gla, glb, ya, ga, gb, dm):
            _, vjp = jax.vjp(_mixf, gla, glb, ya, ga, gb)
            return vjp(dm.astype(F32))

        nmix = D // 1024
        dgla, dglb, dya, dga, dgb = _rowwise("mix_bwd", mix_bwd, s["mix_rows"] + [(dmix, 1024, 0, 0)], [],
                                             [(D, BF16)] * 5, ncol=nmix)
        dgg = jnp.concatenate([dga, dgb], axis=1)
        dyg = _mm_nt("mm_w_glu_t", dgg, W("ssm_w_glu", l), BF16)
        emit("ssm_w_glu", l, _mm_tn("mm_dw_glu", s["yg"], dgg, nb=2 * D // N_DEV))
        du, dbr, dbi, dcr, dci, dar, dai, dd = _ssm_bwd("ssm_bwd", s["proj"], u_off, ssm_p[l], s["ck_r"], s["ck_i"],
                                                        dyg, SW)
        cots = (dar[:, 0, :].reshape(G, SSM_STATE), dai[:, 0, :].reshape(G, SSM_STATE),
                _block_diag_t(dbr, SSM_GROUP, SSM_STATE).reshape(G * SSM_GROUP, SSM_STATE),
                _block_diag_t(dbi, SSM_GROUP, SSM_STATE).reshape(G * SSM_GROUP, SSM_STATE))
        g_are, g_aim, g_ldt, g_bre, g_bim = _ssm_prep_bwd("ssm_prep_bwd", ssm_raw[l], cots)
        small["ssm_a_re"][l], small["ssm_a_im"][l], small["ssm_log_dt"][l] = g_are, g_aim, g_ldt[:, 0]
        small["ssm_b_re"][l] = jnp.transpose(g_bre.reshape(G, SSM_GROUP, SSM_STATE), (0, 2, 1))
        small["ssm_b_im"][l] = jnp.transpose(g_bim.reshape(G, SSM_GROUP, SSM_STATE), (0, 2, 1))
        small["ssm_c_re"][l] = _block_diag_t(dcr, SSM_GROUP, SSM_STATE)
        small["ssm_c_im"][l] = _block_diag_t(dci, SSM_GROUP, SSM_STATE)
        small["ssm_d"][l] = dd[:, 0, :].reshape(SW)
        dhs = _mm_nt("mm_w_pw_t", dya, W("conv_w_pw", l), BF16)
        emit("conv_w_pw", l, _mm_tn("mm_dw_pw", s["hs"], dya, nb=D // N_DEV))

        def post_bwd(hc, dh, b, lg, lb):
            _, vjp = jax.vjp(_convpost, hc, b, lg, lb)
            dhc, db, dlg, dlb = vjp(dh.astype(F32))
            return dhc, jnp.sum(db, axis=0, keepdims=True), jnp.sum(dlg, axis=0, keepdims=True), \
                jnp.sum(dlb, axis=0, keepdims=True)

        dhc, small["conv_dw_b"][l], small["conv_ln_g"][l], small["conv_ln_b"][l] = _rowwise(
            "conv_post_bwd", post_bwd, [(s["hc"], CW, 0, 0), (dhs, CW, 0, 0)],
            [row("conv_dw_b", l), row("conv_ln_g", l), row("conv_ln_b", l)], [(CW, F32)], [(1, CW)] * 3)
        da, db, d_conv_dw = _glu_conv_bwd("conv_bwd", s["proj"], dw_filter("conv_dw_w", l), dhc, CW)
        emit("conv_dw_w", l, jnp.transpose(d_conv_dw.reshape(HALO, N_DEV, CW // N_DEV), (1, 0, 2)))
        dproj = jnp.concatenate([da, db, du, dgla, dglb], axis=1)
        dh1 = _mm_nt("mm_w_in_t", dproj, W("w_in", l), BF16)
        emit("w_in", l, _mm_tn("mm_dw_in", s["h1"], dproj, nb=dproj.shape[1] // N_DEV))
        dx, dx16, small["mix_norm_g"][l] = _rms_bwd("rms_mix_bwd", s["x_in"], row("mix_norm_g", l), dh1, dx)

    names = [n for n in _WEIGHTS if n not in big]
    flat_g = jnp.concatenate([(jnp.stack(small[n]) if n != "final_norm_g" else d_final_g).reshape(-1)
                              for n in names])
    sizes = [inp[n].size for n in names]
    total = sum(sizes)
    pack_w = 8 * LANE
    rows_p = -(-total // (LANE * pack_w)) * LANE
    padn = rows_p * pack_w - total

    def pack(parts, fill):
        return jnp.concatenate([p.reshape(-1) for p in parts] + [jnp.full((padn,), fill, F32)]).reshape(rows_p, pack_w)

    g_all = _allgather("ag_small_grads", pack([flat_g], 0.0)[None], False)
    while queue:
        tick()
    results = {n: [o[:, :inp[n].shape[1], :] for o in state[n]] if n in pads else state[n] for n in big}
    outs = _sum_adamw("small_adamw", g_all.reshape(N_DEV * rows_p, pack_w), N_DEV,
                      pack([inp[n] for n in names], 0.0), pack([mom_m[n] for n in names], 0.0),
                      pack([mom_v[n] for n in names], 1.0))
    offs = 0
    for n, sz in zip(names, sizes):
        results[n] = [o.reshape(-1)[offs:offs + sz].reshape(inp[n].shape) for o in outs]
        offs += sz

    _LAST_CALL.clear()
    loss = lax.psum(loss_part, ("x", "y", "c"))
    grad_x = dx[None]
    return (loss, grad_x, *[results[n][0] for n in _WEIGHTS], *[results[n][1] for n in _WEIGHTS],
            *[results[n][2] for n in _WEIGHTS], *[results[n][3] for n in _WEIGHTS])


def kernel(x, mem, mix_norm_g, w_in, conv_dw_w, conv_dw_b, conv_ln_g, conv_ln_b, conv_w_pw, ssm_a_re, ssm_a_im, ssm_log_dt, ssm_b_re, ssm_b_im, ssm_c_re, ssm_c_im, ssm_d, ssm_w_glu, w_out, xa_norm_g, mem_norm_g, xa_w_q, xa_w_kv, xa_w_o, ffn_norm_g, ffn_w_up, ffn_dw_w, ffn_w_down, final_norm_g, loss_target, m_mix_norm_g, m_w_in, m_conv_dw_w, m_conv_dw_b, m_conv_ln_g, m_conv_ln_b, m_conv_w_pw, m_ssm_a_re, m_ssm_a_im, m_ssm_log_dt, m_ssm_b_re, m_ssm_b_im, m_ssm_c_re, m_ssm_c_im, m_ssm_d, m_ssm_w_glu, m_w_out, m_xa_norm_g, m_mem_norm_g, m_xa_w_q, m_xa_w_kv, m_xa_w_o, m_ffn_norm_g, m_ffn_w_up, m_ffn_dw_w, m_ffn_w_down, m_final_norm_g, v_mix_norm_g, v_w_in, v_conv_dw_w, v_conv_dw_b, v_conv_ln_g, v_conv_ln_b, v_conv_w_pw, v_ssm_a_re, v_ssm_a_im, v_ssm_log_dt, v_ssm_b_re, v_ssm_b_im, v_ssm_c_re, v_ssm_c_im, v_ssm_d, v_ssm_w_glu, v_w_out, v_xa_norm_g, v_mem_norm_g, v_xa_w_q, v_xa_w_kv, v_xa_w_o, v_ffn_norm_g, v_ffn_w_up, v_ffn_dw_w, v_ffn_w_down, v_final_norm_g):
    args = (x, mem, mix_norm_g, w_in, conv_dw_w, conv_dw_b, conv_ln_g, conv_ln_b, conv_w_pw, ssm_a_re, ssm_a_im, ssm_log_dt, ssm_b_re, ssm_b_im, ssm_c_re, ssm_c_im, ssm_d, ssm_w_glu, w_out, xa_norm_g, mem_norm_g, xa_w_q, xa_w_kv, xa_w_o, ffn_norm_g, ffn_w_up, ffn_dw_w, ffn_w_down, final_norm_g)
    ms = (m_mix_norm_g, m_w_in, m_conv_dw_w, m_conv_dw_b, m_conv_ln_g, m_conv_ln_b, m_conv_w_pw, m_ssm_a_re, m_ssm_a_im, m_ssm_log_dt, m_ssm_b_re, m_ssm_b_im, m_ssm_c_re, m_ssm_c_im, m_ssm_d, m_ssm_w_glu, m_w_out, m_xa_norm_g, m_mem_norm_g, m_xa_w_q, m_xa_w_kv, m_xa_w_o, m_ffn_norm_g, m_ffn_w_up, m_ffn_dw_w, m_ffn_w_down, m_final_norm_g)
    vs = (v_mix_norm_g, v_w_in, v_conv_dw_w, v_conv_dw_b, v_conv_ln_g, v_conv_ln_b, v_conv_w_pw, v_ssm_a_re, v_ssm_a_im, v_ssm_log_dt, v_ssm_b_re, v_ssm_b_im, v_ssm_c_re, v_ssm_c_im, v_ssm_d, v_ssm_w_glu, v_w_out, v_xa_norm_g, v_mem_norm_g, v_xa_w_q, v_xa_w_kv, v_xa_w_o, v_ffn_norm_g, v_ffn_w_up, v_ffn_dw_w, v_ffn_w_down, v_final_norm_g)
    return _step(dict(zip(_FWD, args)), loss_target, dict(zip(_WEIGHTS, ms)), dict(zip(_WEIGHTS, vs)))
```

```python
import functools

import jax
import jax.numpy as jnp
from jax import lax
from jax.experimental import pallas as pl
from jax.experimental.pallas import tpu as pltpu
from jax.experimental.pallas import tpu_sc as plsc

F32 = jnp.float32
BF16 = jnp.bfloat16
MESH_ID = pl.DeviceIdType.MESH
N_DEV = 8
EPS = 1e-6
VMEM_LIMIT = 48 * 1024 * 1024
ANY = pl.BlockSpec(memory_space=pl.ANY)

ADAM_LR = 0.001
ADAM_B1 = 0.9
ADAM_B2 = 0.999
ADAM_EPS = 1e-08
ADAM_WD = 0.01
ADAM_STEP = 10

CONV_K = 31
FFN_K = 3
XA_HEADS = 4
SSM_GROUP = 16
SSM_STATE = 64
HALO = 32
LANE = 128
SSM_LANES = 512
AG_COLLECTIVE_ID = 1
RS_SIBLING_COLLECTIVE_ID = 2
RS_CHIPS_COLLECTIVE_ID = 3


def _pick(n, prefs):
    for p in prefs:
        if p <= n and n % p == 0:
            return p
    return n


def _params(sem, vmem=VMEM_LIMIT):
    return pltpu.CompilerParams(dimension_semantics=sem, vmem_limit_bytes=vmem)


_LAST_CALL = []


def _tc_call(*call_args, **call_kwargs):
    call = pl.pallas_call(*call_args, **call_kwargs)

    def run(*args):
        args = list(args)
        if _LAST_CALL:
            i = next(k for k, a in enumerate(args) if a.ndim >= 2)
            args[i] = lax.optimization_barrier((args[i], _LAST_CALL[0]))[0]
        out = call(*args)
        _LAST_CALL[:] = [out[0] if isinstance(out, (tuple, list)) else out]
        return out

    return run


def _sigmoid(x):
    return 1.0 / (1.0 + jnp.exp(-x))


def _silu(x):
    return x * _sigmoid(x)


def _gelu(x):
    return 0.5 * x * (1.0 + jnp.tanh(0.7978845608028654 * (x + 0.044715 * (x * x * x))))


def _rms(x, g):
    return x * lax.rsqrt(jnp.mean(x * x, axis=-1, keepdims=True) + EPS) * g


def _convpost(hc, bias, ln_g, ln_b):
    h = hc + bias
    mu = jnp.mean(h, axis=-1, keepdims=True)
    xc = h - mu
    y = xc * lax.rsqrt(jnp.mean(xc * xc, axis=-1, keepdims=True) + EPS)
    return _silu(y * ln_g + ln_b)


def _mixf(gla, glb, ya, ga, gb):
    return _sigmoid(gla) * ya + _sigmoid(glb) * (ga * _sigmoid(gb))


class _W:
    def __init__(self, arr, layer, blocked):
        self.arr, self.layer, self.blocked = arr, layer, blocked
        if blocked:
            _, _, self.K, self.nb = arr.shape
            self.N = N_DEV * self.nb
        else:
            _, self.K, self.N = arr.shape
            self.nb = self.N

    def spec(self, tk, tn, ki, ni):
        l = self.layer
        if self.blocked:
            per = self.nb // tn
            return pl.BlockSpec((None, None, tk, tn), lambda *g: (ni(*g) // per, l, ki(*g), ni(*g) % per))
        return pl.BlockSpec((None, tk, tn), lambda *g: (l, ki(*g), ni(*g)))


_M_TILES = (1024, 512, 256, 128, 64, 32, 16, 8)
_N_TILES = (1408, 1024, 896, 512, 256, 128)
_K_TILES = (512, 1408, 896, 256, 128)
MAX_FULL_K = 2048
_K_FULL_TILES = (2048, 1408, 1024, 896, 512, 256, 128)


def _mm_nn(name, a, w, out_dtype, add=None):
    M, K = a.shape
    assert K == w.K
    tm, tn = _pick(M, _M_TILES), _pick(w.nb, _N_TILES)
    tk = K if K <= MAX_FULL_K else _pick(K, _K_FULL_TILES)
    nk = K // tk

    def body(*refs):
        a_ref, w_ref = refs[:2]
        r_ref = refs[2] if add is not None else None
        o_ref = refs[3] if add is not None else refs[2]
        part = jnp.dot(a_ref[...].astype(BF16), w_ref[...], preferred_element_type=F32)
        if nk == 1:
            o_ref[...] = (part if add is None else part + r_ref[...]).astype(o_ref.dtype)
            return
        acc = refs[-1]
        k = pl.program_id(2)

        @pl.when(k == 0)
        def _():
            acc[...] = part

        @pl.when(k > 0)
        def _():
            acc[...] += part

        @pl.when(k == nk - 1)
        def _():
            res = acc[...]
            if add is not None:
                res = res + r_ref[...]
            o_ref[...] = res.astype(o_ref.dtype)

    in_specs = [pl.BlockSpec((tm, tk), lambda i, j, k: (i, k)),
                w.spec(tk, tn, lambda i, j, k: k, lambda i, j, k: j)]
    args = [a, w.arr]
    if add is not None:
        in_specs.append(pl.BlockSpec((tm, tn), lambda i, j, k: (i, j)))
        args.append(add)
    return _tc_call(
        body, name=name, grid=(M // tm, w.N // tn, nk), in_specs=in_specs,
        out_specs=pl.BlockSpec((tm, tn), lambda i, j, k: (i, j)),
        out_shape=jax.ShapeDtypeStruct((M, w.N), out_dtype),
        scratch_shapes=[] if nk == 1 else [pltpu.VMEM((tm, tn), F32)],
        compiler_params=_params(("parallel", "parallel", "arbitrary")),
    )(*args)


def _mm_nt(name, a, w, out_dtype):
    M, N = a.shape
    assert N == w.N
    tm, tkk = _pick(M, _M_TILES), _pick(w.K, _N_TILES)
    tnn = w.nb if w.nb <= MAX_FULL_K else _pick(w.nb, _K_FULL_TILES)
    nn = N // tnn

    def body(a_ref, w_ref, o_ref, *scratch):
        part = lax.dot_general(a_ref[...].astype(BF16), w_ref[...], (((1,), (1,)), ((), ())),
                               preferred_element_type=F32)
        if nn == 1:
            o_ref[...] = part.astype(o_ref.dtype)
            return
        acc = scratch[0]
        n = pl.program_id(2)

        @pl.when(n == 0)
        def _():
            acc[...] = part

        @pl.when(n > 0)
        def _():
            acc[...] += part

        @pl.when(n == nn - 1)
        def _():
            o_ref[...] = acc[...].astype(o_ref.dtype)

    return _tc_call(
        body, name=name, grid=(M // tm, w.K // tkk, nn),
        in_specs=[pl.BlockSpec((tm, tnn), lambda i, j, n: (i, n)),
                  w.spec(tkk, tnn, lambda i, j, n: j, lambda i, j, n: n)],
        out_specs=pl.BlockSpec((tm, tkk), lambda i, j, n: (i, j)),
        out_shape=jax.ShapeDtypeStruct((M, w.K), out_dtype),
        scratch_shapes=[] if nn == 1 else [pltpu.VMEM((tm, tkk), F32)],
        compiler_params=_params(("parallel", "parallel", "arbitrary")),
    )(a, w.arr)


def _mm_tn(name, a, b, nb=None):
    T, K = a.shape
    _, N = b.shape
    width = N if nb is None else nb
    tkk, tn, tt = _pick(K, (1024, 512, 256, 128)), _pick(width, _N_TILES), _pick(T, (2048, 1024, 512, 256, 128, 64, 32, 16))
    nt = T // tt
    per = width // tn

    def body(a_ref, b_ref, o_ref, acc):
        t = pl.program_id(2)
        part = lax.dot_general(a_ref[...].astype(BF16), b_ref[...].astype(BF16), (((0,), (0,)), ((), ())),
                               preferred_element_type=F32)

        @pl.when(t == 0)
        def _():
            acc[...] = part

        @pl.when(t > 0)
        def _():
            acc[...] += part

        @pl.when(t == nt - 1)
        def _():
            o_ref[...] = acc[...].astype(o_ref.dtype)

    if nb is None:
        out_spec = pl.BlockSpec((tkk, tn), lambda i, j, t: (i, j))
        out_shape = jax.ShapeDtypeStruct((K, N), BF16)
    else:
        out_spec = pl.BlockSpec((None, tkk, tn), lambda i, j, t: (j // per, i, j % per))
        out_shape = jax.ShapeDtypeStruct((N_DEV, K, nb), BF16)
    return _tc_call(
        body, name=name, grid=(K // tkk, N // tn, nt),
        in_specs=[pl.BlockSpec((tt, tkk), lambda i, j, t: (t, i)),
                  pl.BlockSpec((tt, tn), lambda i, j, t: (t, j))],
        out_specs=out_spec, out_shape=out_shape,
        scratch_shapes=[pltpu.VMEM((tkk, tn), F32)],
        compiler_params=_params(("parallel", "parallel", "arbitrary")),
    )(a, b)


def _rowwise(name, fn, rows, consts, outs, accs=(), tile=256, ncol=1, nrows=None):
    n_r, n_c, n_o = len(rows), len(consts), len(outs)
    T = rows[0][0].shape[0] if nrows is None else nrows
    tile = _pick(T, tuple(t for t in (512, 256, 128, 64, 32, 16, 8) if t <= tile))
    nt = T // tile

    def body(*refs):
        vals = [r[...] for r in refs[:n_r + n_c]]
        res = fn(*vals)
        if not isinstance(res, (tuple, list)):
            res = (res,)
        o_refs = refs[n_r + n_c:n_r + n_c + n_o]
        a_refs = refs[n_r + n_c + n_o:]
        for r, v in zip(o_refs, res[:n_o]):
            r[...] = v.astype(r.dtype)
        first = pl.program_id(1) == 0
        for r, v in zip(a_refs, res[n_o:]):
            @pl.when(first)
            def _(r=r, v=v):
                r[...] = v.astype(F32)

            @pl.when(jnp.logical_not(first))
            def _(r=r, v=v):
                r[...] += v.astype(F32)

    in_specs, args = [], []
    for arr, w, off, roff in rows:
        rb = roff // tile
        assert roff % tile == 0
        in_specs.append(pl.BlockSpec((tile, w), lambda j, i, off=off, rb=rb: (i + rb, off + j)))
        args.append(arr)
    for cst in consts:
        in_specs.append(pl.BlockSpec(cst.shape, lambda j, i: (0, 0)))
        args.append(cst)
    out_specs, out_shape = [], []
    for tw, dt in outs:
        out_specs.append(pl.BlockSpec((tile, tw // ncol), lambda j, i: (i, j)))
        out_shape.append(jax.ShapeDtypeStruct((T, tw), dt))
    for nr, tw in accs:
        out_specs.append(pl.BlockSpec((nr, tw // ncol), lambda j, i: (0, j)))
        out_shape.append(jax.ShapeDtypeStruct((nr, tw), F32))
    res = _tc_call(
        body, name=name, grid=(ncol, nt), in_specs=in_specs, out_specs=out_specs, out_shape=out_shape,
        compiler_params=_params(("parallel", "arbitrary")),
    )(*args)
    return res


def _rms_fwd(name, x, g):
    D = x.shape[1]
    return _rowwise(name, lambda xv, gv: _rms(xv, gv), [(x, D, 0, 0)], [g], [(D, BF16)])[0]


def _rms_bwd(name, x, g, dh, dx_in):
    D = x.shape[1]

    def fn(xv, dhv, dxv, gv):
        _, vjp = jax.vjp(_rms, xv, gv)
        dx, dg = vjp(dhv.astype(F32))
        tot = dx + dxv
        return tot, tot, jnp.sum(dg, axis=0, keepdims=True)

    return _rowwise(name, fn, [(x, D, 0, 0), (dh, D, 0, 0), (dx_in, D, 0, 0)], [g], [(D, F32), (D, BF16)], [(1, D)],
                    tile=256)


def _lag_views(win, K, R, forward):
    n = win.shape[0]
    for r in range(8):
        if r >= K:
            break
        if r == 0:
            rolled = win
        else:
            rolled = pltpu.roll(win, (n - r) if forward else r, axis=0)
        for q in range((K - 1 - r) // 8 + 1):
            s = 8 * q + r
            if forward:
                yield s, rolled[8 * q:8 * q + R]
            else:
                yield s, rolled[HALO - 8 * q:HALO - 8 * q + R]


def _conv_chunk(win, w_ref, K, R):
    acc = None
    for s, view in _lag_views(win, K, R, forward=False):
        term = w_ref[K - 1 - s:K - s, :] * view
        acc = term if acc is None else acc + term
    return acc


def _conv_chunk_t(win, w_ref, K, R):
    acc = None
    for s, view in _lag_views(win, K, R, forward=True):
        term = w_ref[K - 1 - s:K - s, :] * view
        acc = term if acc is None else acc + term
    return acc


def _conv_dw(xwin, dy, K, R):
    taps = [None] * K
    for s, view in _lag_views(xwin, K, R, forward=False):
        taps[K - 1 - s] = jnp.sum(dy * view, axis=0, keepdims=True)
    return taps


def _chunks(T):
    R = _pick(T, (128, 64, 32))
    return R, T // R


def _glu_conv_fwd(name, proj, w, cw_total):
    T = proj.shape[0]
    C = cw_total
    cw = LANE
    nb = C // cw
    R, nch = _chunks(T)

    def body(a_ref, b_ref, w_ref, o_ref, s_ref):
        s_ref[0:HALO, :] = jnp.zeros((HALO, cw), F32)

        def fill(i, _):
            r0 = pl.multiple_of(i * R, R)
            s_ref[pl.ds(HALO + r0, R), :] = a_ref[pl.ds(r0, R), :] * _sigmoid(b_ref[pl.ds(r0, R), :])
            return 0

        lax.fori_loop(0, nch, fill, 0)

        def conv(i, _):
            r0 = pl.multiple_of(i * R, R)
            o_ref[pl.ds(r0, R), :] = _conv_chunk(s_ref[pl.ds(r0, R + HALO), :], w_ref, CONV_K, R)
            return 0

        lax.fori_loop(0, nch, conv, 0)

    return _tc_call(
        body, name=name, grid=(nb,),
        in_specs=[pl.BlockSpec((T, cw), lambda j: (0, j)), pl.BlockSpec((T, cw), lambda j: (0, nb + j)),
                  pl.BlockSpec((HALO, cw), lambda j: (0, j))],
        out_specs=pl.BlockSpec((T, cw), lambda j: (0, j)),
        out_shape=jax.ShapeDtypeStruct((T, C), F32),
        scratch_shapes=[pltpu.VMEM((T + HALO, cw), F32)],
        compiler_params=_params(("parallel",)),
    )(proj, proj, w)


def _glu_conv_bwd(name, proj, w, dhc, cw_total):
    T = proj.shape[0]
    C = cw_total
    cw = LANE
    nb = C // cw
    R, nch = _chunks(T)

    def body(a_ref, b_ref, w_ref, dy_ref, da_ref, db_ref, dw_ref, s_ref, g_ref, acc_ref):
        s_ref[0:HALO, :] = jnp.zeros((HALO, cw), F32)
        g_ref[T:T + HALO, :] = jnp.zeros((HALO, cw), F32)
        acc_ref[...] = jnp.zeros_like(acc_ref)

        def fill(i, _):
            r0 = pl.multiple_of(i * R, R)
            s_ref[pl.ds(HALO + r0, R), :] = a_ref[pl.ds(r0, R), :] * _sigmoid(b_ref[pl.ds(r0, R), :])
            g_ref[pl.ds(r0, R), :] = dy_ref[pl.ds(r0, R), :]
            return 0

        lax.fori_loop(0, nch, fill, 0)

        def back(i, _):
            r0 = pl.multiple_of(i * R, R)
            dhg = _conv_chunk_t(g_ref[pl.ds(r0, R + HALO), :], w_ref, CONV_K, R)
            av = a_ref[pl.ds(r0, R), :]
            sg = _sigmoid(b_ref[pl.ds(r0, R), :])
            da_ref[pl.ds(r0, R), :] = (dhg * sg).astype(da_ref.dtype)
            db_ref[pl.ds(r0, R), :] = (dhg * av * sg * (1.0 - sg)).astype(db_ref.dtype)
            taps = _conv_dw(s_ref[pl.ds(r0, R + HALO), :], dy_ref[pl.ds(r0, R), :], CONV_K, R)
            for k, tap in enumerate(taps):
                acc_ref[k:k + 1, :] += tap
            return 0

        lax.fori_loop(0, nch, back, 0)
        dw_ref[...] = acc_ref[...]

    return _tc_call(
        body, name=name, grid=(nb,),
        in_specs=[pl.BlockSpec((T, cw), lambda j: (0, j)), pl.BlockSpec((T, cw), lambda j: (0, nb + j)),
                  pl.BlockSpec((HALO, cw), lambda j: (0, j)), pl.BlockSpec((T, cw), lambda j: (0, j))],
        out_specs=[pl.BlockSpec((T, cw), lambda j: (0, j)), pl.BlockSpec((T, cw), lambda j: (0, j)),
                   pl.BlockSpec((HALO, cw), lambda j: (0, j))],
        out_shape=[jax.ShapeDtypeStruct((T, C), BF16), jax.ShapeDtypeStruct((T, C), BF16),
                   jax.ShapeDtypeStruct((HALO, C), F32)],
        scratch_shapes=[pltpu.VMEM((T + HALO, cw), F32), pltpu.VMEM((T + HALO, cw), F32),
                        pltpu.VMEM((HALO, cw), F32)],
        compiler_params=_params(("parallel",)),
    )(proj, proj, w, dhc)


def _ffn_conv_fwd(name, up, w, dff):
    T = up.shape[0]
    cw = LANE
    nb = dff // cw
    R, nch = _chunks(T)

    def body(g_ref, v_ref, wg_ref, wv_ref, o_ref, sg_ref, sv_ref):
        sg_ref[0:HALO, :] = jnp.zeros((HALO, cw), F32)
        sv_ref[0:HALO, :] = jnp.zeros((HALO, cw), F32)

        def fill(i, _):
            r0 = pl.multiple_of(i * R, R)
            sg_ref[pl.ds(HALO + r0, R), :] = g_ref[pl.ds(r0, R), :]
            sv_ref[pl.ds(HALO + r0, R), :] = v_ref[pl.ds(r0, R), :]
            return 0

        lax.fori_loop(0, nch, fill, 0)

        def conv(i, _):
            r0 = pl.multiple_of(i * R, R)
            gc = _conv_chunk(sg_ref[pl.ds(r0, R + HALO), :], wg_ref, FFN_K, R)
            vc = _conv_chunk(sv_ref[pl.ds(r0, R + HALO), :], wv_ref, FFN_K, R)
            o_ref[pl.ds(r0, R), :] = (_silu(gc) * vc).astype(o_ref.dtype)
            return 0

        lax.fori_loop(0, nch, conv, 0)

    return _tc_call(
        body, name=name, grid=(nb,),
        in_specs=[pl.BlockSpec((T, cw), lambda j: (0, j)), pl.BlockSpec((T, cw), lambda j: (0, nb + j)),
                  pl.BlockSpec((8, cw), lambda j: (0, j)), pl.BlockSpec((8, cw), lambda j: (0, nb + j))],
        out_specs=pl.BlockSpec((T, cw), lambda j: (0, j)),
        out_shape=jax.ShapeDtypeStruct((T, dff), BF16),
        scratch_shapes=[pltpu.VMEM((T + HALO, cw), F32), pltpu.VMEM((T + HALO, cw), F32)],
        compiler_params=_params(("parallel",)),
    )(up, up, w, w)


def _ffn_conv_bwd(name, up, w, dact, dff):
    T = up.shape[0]
    cw = LANE
    nb = dff // cw
    R, nch = _chunks(T)

    def body(g_ref, v_ref, wg_ref, wv_ref, da_ref, dg_ref, dv_ref, dwg_ref, dwv_ref,
             sg_ref, sv_ref, tg_ref, tv_ref, ag_ref, av_ref):
        zero = jnp.zeros((HALO, cw), F32)
        sg_ref[0:HALO, :] = zero
        sv_ref[0:HALO, :] = zero
        tg_ref[T:T + HALO, :] = zero
        tv_ref[T:T + HALO, :] = zero
        ag_ref[...] = jnp.zeros_like(ag_ref)
        av_ref[...] = jnp.zeros_like(av_ref)

        def fill(i, _):
            r0 = pl.multiple_of(i * R, R)
            sg_ref[pl.ds(HALO + r0, R), :] = g_ref[pl.ds(r0, R), :]
            sv_ref[pl.ds(HALO + r0, R), :] = v_ref[pl.ds(r0, R), :]
            return 0

        lax.fori_loop(0, nch, fill, 0)

        def grads(i, _):
            r0 = pl.multiple_of(i * R, R)
            gwin = sg_ref[pl.ds(r0, R + HALO), :]
            vwin = sv_ref[pl.ds(r0, R + HALO), :]
            gc = _conv_chunk(gwin, wg_ref, FFN_K, R)
            vc = _conv_chunk(vwin, wv_ref, FFN_K, R)
            da = da_ref[pl.ds(r0, R), :].astype(F32)
            sg = _sigmoid(gc)
            dgc = da * vc * (sg * (1.0 + gc * (1.0 - sg)))
            dvc = da * (gc * sg)
            tg_ref[pl.ds(r0, R), :] = dgc
            tv_ref[pl.ds(r0, R), :] = dvc
            for k, tap in enumerate(_conv_dw(gwin, dgc, FFN_K, R)):
                ag_ref[k:k + 1, :] += tap
            for k, tap in enumerate(_conv_dw(vwin, dvc, FFN_K, R)):
                av_ref[k:k + 1, :] += tap
            return 0

        lax.fori_loop(0, nch, grads, 0)

        def back(i, _):
            r0 = pl.multiple_of(i * R, R)
            dg_ref[pl.ds(r0, R), :] = _conv_chunk_t(tg_ref[pl.ds(r0, R + HALO), :], wg_ref, FFN_K, R).astype(dg_ref.dtype)
            dv_ref[pl.ds(r0, R), :] = _conv_chunk_t(tv_ref[pl.ds(r0, R + HALO), :], wv_ref, FFN_K, R).astype(dv_ref.dtype)
            return 0

        lax.fori_loop(0, nch, back, 0)
        dwg_ref[...] = ag_ref[...]
        dwv_ref[...] = av_ref[...]

    col = lambda j: (0, j)
    dg, dv, dwg, dwv = _tc_call(
        body, name=name, grid=(nb,),
        in_specs=[pl.BlockSpec((T, cw), col), pl.BlockSpec((T, cw), lambda j: (0, nb + j)),
                  pl.BlockSpec((8, cw), col), pl.BlockSpec((8, cw), lambda j: (0, nb + j)),
                  pl.BlockSpec((T, cw), col)],
        out_specs=[pl.BlockSpec((T, cw), col), pl.BlockSpec((T, cw), col),
                   pl.BlockSpec((8, cw), col), pl.BlockSpec((8, cw), col)],
        out_shape=[jax.ShapeDtypeStruct((T, dff), BF16), jax.ShapeDtypeStruct((T, dff), BF16),
                   jax.ShapeDtypeStruct((8, dff), F32), jax.ShapeDtypeStruct((8, dff), F32)],
        scratch_shapes=[pltpu.VMEM((T + HALO, cw), F32), pltpu.VMEM((T + HALO, cw), F32),
                        pltpu.VMEM((T + HALO, cw), F32), pltpu.VMEM((T + HALO, cw), F32),
                        pltpu.VMEM((8, cw), F32), pltpu.VMEM((8, cw), F32)],
        compiler_params=_params(("parallel",)),
    )(up, up, w, w, dact)
    return jnp.concatenate([dg, dv], axis=1), jnp.concatenate([dwg, dwv], axis=1)


def _zoh(a_re, a_im, log_dt):
    ar = jnp.minimum(a_re, -1e-4)
    ai = a_im
    dt = jnp.exp(log_dt)
    mag = jnp.exp(dt * ar)
    abar_re = mag * jnp.cos(dt * ai)
    abar_im = mag * jnp.sin(dt * ai)
    den = ar * ar + ai * ai
    nr = abar_re - 1.0
    ni = abar_im
    return abar_re, abar_im, (nr * ar + ni * ai) / den, (ni * ar - nr * ai) / den


def _discretize(a_re, a_im, log_dt, a_re_h, a_im_h, log_dt_h, b_re, b_im):
    abar_re, abar_im, _, _ = _zoh(a_re, a_im, log_dt)
    _, _, z_re, z_im = _zoh(a_re_h, a_im_h, log_dt_h)
    return abar_re, abar_im, z_re * b_re - z_im * b_im, z_re * b_im + z_im * b_re


def _full_specs(arrs):
    return [pl.BlockSpec(a.shape, lambda *_, n=len(a.shape): (0,) * n) for a in arrs]


def _ssm_prep(name, raw):
    def body(*refs):
        res = _discretize(*[r[...] for r in refs[:8]])
        for r, v in zip(refs[8:], res):
            r[...] = v

    outs = [jax.ShapeDtypeStruct(raw[0].shape, F32)] * 2 + [jax.ShapeDtypeStruct(raw[6].shape, F32)] * 2
    return _tc_call(body, name=name, in_specs=_full_specs(raw), out_specs=_full_specs(outs), out_shape=outs)(*raw)


def _ssm_prep_bwd(name, raw, cots):
    G = raw[0].shape[0]
    H = raw[3].shape[0] // G

    def body(*refs):
        _, vjp = jax.vjp(_discretize, *[r[...] for r in refs[:8]])
        g = vjp(tuple(r[...] for r in refs[8:12]))
        outs = refs[12:]
        for k in range(3):
            rep = g[3 + k]
            outs[k][...] = g[k] + jnp.sum(rep.reshape(G, H, rep.shape[1]), axis=1)
        outs[3][...] = g[6]
        outs[4][...] = g[7]

    outs = [jax.ShapeDtypeStruct(a.shape, F32) for a in (raw[0], raw[1], raw[2], raw[6], raw[7])]
    return _tc_call(body, name=name, in_specs=_full_specs(list(raw) + list(cots)), out_specs=_full_specs(outs),
                          out_shape=outs)(*raw, *cots)


def _cmul(ar, ai, br, bi):
    return ar * br - ai * bi, ar * bi + ai * br


def _scan_coefs(ar, ai, reverse):
    W = ar.shape[1]
    row = lax.broadcasted_iota(jnp.int32, (8, W), 0)
    p = [None] * 9
    p[1] = (ar, ai)
    for n in range(2, 9):
        p[n] = _cmul(*p[n // 2], *p[n - n // 2])
    steps = []
    for s in (1, 2, 4):
        valid = (row <= 7 - s) if reverse else (row >= s)
        steps.append((jnp.where(valid, p[s][0], 0.0), jnp.where(valid, p[s][1], 0.0)))
    pr = jnp.zeros((8, W), F32)
    pi = jnp.zeros((8, W), F32)
    for i in range(8):
        n = (8 - i) if reverse else (i + 1)
        pr = jnp.where(row == i, p[n][0], pr)
        pi = jnp.where(row == i, p[n][1], pi)
    return steps, (pr, pi)


def _scan_tile(xr, xi, cr, ci, coefs, reverse):
    steps, (pr, pi) = coefs
    for s, (sr, si) in zip((1, 2, 4), steps):
        shift = (8 - s) if reverse else s
        rr = pltpu.roll(xr, shift, axis=0)
        ri = pltpu.roll(xi, shift, axis=0)
        xr, xi = xr + sr * rr - si * ri, xi + sr * ri + si * rr
    xr, xi = xr + pr * cr - pi * ci, xi + pr * ci + pi * cr
    return xr, xi


def _edge_rows(x, reverse):
    W = x.shape[1]
    return jnp.broadcast_to(x[0:1, :] if reverse else x[7:8, :], (8, W))


def _ssm_chunk(T):
    return _pick(T, (256, 128, 64))


def _ssm_fwd(name, proj, u_off, p, width):
    T = proj.shape[0]
    NB = width // LANE
    Q = _ssm_chunk(T)
    nch = T // Q
    W = SSM_LANES

    def body(u_ref, bre, bim, cre, cim, ar_ref, ai_ref, d_ref, y_ref, ckr_ref, cki_ref, br_s, bi_s, car_r, car_i):
        c = pl.program_id(1)

        @pl.when(c == 0)
        def _():
            car_r[...] = jnp.zeros_like(car_r)
            car_i[...] = jnp.zeros_like(car_i)

        ckr_ref[...] = car_r[...]
        cki_ref[...] = car_i[...]
        u = u_ref[...]
        u16 = u.astype(BF16)
        br_s[...] = jnp.dot(u16, bre[...], preferred_element_type=F32)
        bi_s[...] = jnp.dot(u16, bim[...], preferred_element_type=F32)
        coefs = _scan_coefs(ar_ref[...], ai_ref[...], False)

        def tile(j, carry):
            r0 = pl.multiple_of(j * 8, 8)
            xr, xi = _scan_tile(br_s[pl.ds(r0, 8), :], bi_s[pl.ds(r0, 8), :], carry[0], carry[1], coefs, False)
            br_s[pl.ds(r0, 8), :] = xr
            bi_s[pl.ds(r0, 8), :] = xi
            return _edge_rows(xr, False), _edge_rows(xi, False)

        cr, ci = lax.fori_loop(0, Q // 8, tile, (car_r[...], car_i[...]))
        car_r[...] = cr
        car_i[...] = ci
        nt = (((1,), (1,)), ((), ()))
        y = (lax.dot_general(br_s[...].astype(BF16), cre[...], nt, preferred_element_type=F32)
             - lax.dot_general(bi_s[...].astype(BF16), cim[...], nt, preferred_element_type=F32)
             + d_ref[...] * u)
        y_ref[...] = _gelu(y).astype(y_ref.dtype)

    blk = lambda b, c: (b, 0, 0)
    mat = pl.BlockSpec((None, LANE, W), blk)
    vec = pl.BlockSpec((None, 1, W), blk)
    ck = pl.BlockSpec((None, None, 8, W), lambda b, c: (b, c, 0, 0))
    return _tc_call(
        body, name=name, grid=(NB, nch),
        in_specs=[pl.BlockSpec((Q, LANE), lambda b, c: (c, u_off + b)), mat, mat, mat, mat, vec, vec,
                  pl.BlockSpec((1, LANE), lambda b, c: (0, b))],
        out_specs=[pl.BlockSpec((Q, LANE), lambda b, c: (c, b)), ck, ck],
        out_shape=[jax.ShapeDtypeStruct((T, width), BF16), jax.ShapeDtypeStruct((NB, nch, 8, W), F32),
                   jax.ShapeDtypeStruct((NB, nch, 8, W), F32)],
        scratch_shapes=[pltpu.VMEM((Q, W), F32), pltpu.VMEM((Q, W), F32), pltpu.VMEM((8, W), F32),
                        pltpu.VMEM((8, W), F32)],
        compiler_params=_params(("parallel", "arbitrary")),
    )(proj, p["bre"], p["bim"], p["cre"], p["cim"], p["ar"], p["ai"], p["d"])


def _ssm_bwd(name, proj, u_off, p, ck_r, ck_i, dyg, width):
    T = proj.shape[0]
    NB = width // LANE
    Q = _ssm_chunk(T)
    nch = T // Q
    W = SSM_LANES
    nt_dims = (((1,), (1,)), ((), ()))
    tn_dims = (((0,), (0,)), ((), ()))

    def body(u_ref, dy_ref, ckr_ref, cki_ref, bre, bim, cre, cim, ar_ref, ai_ref, d_ref,
             du_ref, dbr_ref, dbi_ref, dcr_ref, dci_ref, dar_ref, dai_ref, dd_ref,
             xr_s, xi_s, lr_s, li_s, lam_r, lam_i):
        c = pl.program_id(1)

        @pl.when(c == 0)
        def _():
            lam_r[...] = jnp.zeros_like(lam_r)
            lam_i[...] = jnp.zeros_like(lam_i)
            for r in (dbr_ref, dbi_ref, dcr_ref, dci_ref, dar_ref, dai_ref, dd_ref):
                r[...] = jnp.zeros_like(r)

        u = u_ref[...]
        u16 = u.astype(BF16)
        ar, ai = ar_ref[...], ai_ref[...]
        xr_s[0:8, :] = ckr_ref[...]
        xi_s[0:8, :] = cki_ref[...]
        xr_s[8:Q + 8, :] = jnp.dot(u16, bre[...], preferred_element_type=F32)
        xi_s[8:Q + 8, :] = jnp.dot(u16, bim[...], preferred_element_type=F32)
        fcoefs = _scan_coefs(ar, ai, False)

        def ftile(j, carry):
            r0 = pl.multiple_of(j * 8 + 8, 8)
            xr, xi = _scan_tile(xr_s[pl.ds(r0, 8), :], xi_s[pl.ds(r0, 8), :], carry[0], carry[1], fcoefs, False)
            xr_s[pl.ds(r0, 8), :] = xr
            xi_s[pl.ds(r0, 8), :] = xi
            return _edge_rows(xr, False), _edge_rows(xi, False)

        lax.fori_loop(0, Q // 8, ftile, (ckr_ref[...], cki_ref[...]))
        xr16 = xr_s[8:Q + 8, :].astype(BF16)
        xi16 = xi_s[8:Q + 8, :].astype(BF16)
        y = (lax.dot_general(xr16, cre[...], nt_dims, preferred_element_type=F32)
             - lax.dot_general(xi16, cim[...], nt_dims, preferred_element_type=F32) + d_ref[...] * u)
        _, gelu_vjp = jax.vjp(_gelu, y)
        dy = gelu_vjp(dy_ref[...].astype(F32))[0]
        dy16 = dy.astype(BF16)
        dd_ref[...] += jnp.broadcast_to(jnp.sum(dy * u, axis=0, keepdims=True), (8, LANE))
        dcr_ref[...] += lax.dot_general(dy16, xr16, tn_dims, preferred_element_type=F32)
        dci_ref[...] -= lax.dot_general(dy16, xi16, tn_dims, preferred_element_type=F32)
        lr_s[...] = jnp.dot(dy16, cre[...], preferred_element_type=F32)
        li_s[...] = -jnp.dot(dy16, cim[...], preferred_element_type=F32)
        rcoefs = _scan_coefs(ar, -ai, True)
        row = lax.broadcasted_iota(jnp.int32, (8, W), 0)

        def rtile(jj, carry):
            j = Q // 8 - 1 - jj
            r0 = pl.multiple_of(j * 8, 8)
            lr, li = _scan_tile(lr_s[pl.ds(r0, 8), :], li_s[pl.ds(r0, 8), :], carry[0], carry[1], rcoefs, True)
            lr_s[pl.ds(r0, 8), :] = lr
            li_s[pl.ds(r0, 8), :] = li
            cur_r, cur_i = xr_s[pl.ds(r0 + 8, 8), :], xi_s[pl.ds(r0 + 8, 8), :]
            prv_r, prv_i = xr_s[pl.ds(r0, 8), :], xi_s[pl.ds(r0, 8), :]
            xpr = jnp.where(row == 0, _edge_rows(prv_r, False), pltpu.roll(cur_r, 1, axis=0))
            xpi = jnp.where(row == 0, _edge_rows(prv_i, False), pltpu.roll(cur_i, 1, axis=0))
            return (_edge_rows(lr, True), _edge_rows(li, True),
                    carry[2] + lr * xpr + li * xpi, carry[3] + li * xpr - lr * xpi)

        zero = jnp.zeros((8, W), F32)
        cr, ci, sar, sai = lax.fori_loop(0, Q // 8, rtile, (lam_r[...], lam_i[...], zero, zero))
        lam_r[...] = cr
        lam_i[...] = ci
        dar_ref[...] += jnp.broadcast_to(jnp.sum(sar, axis=0, keepdims=True), (8, W))
        dai_ref[...] += jnp.broadcast_to(jnp.sum(sai, axis=0, keepdims=True), (8, W))
        lr16 = lr_s[...].astype(BF16)
        li16 = li_s[...].astype(BF16)
        dbr_ref[...] += lax.dot_general(u16, lr16, tn_dims, preferred_element_type=F32)
        dbi_ref[...] += lax.dot_general(u16, li16, tn_dims, preferred_element_type=F32)
        du = (lax.dot_general(lr16, bre[...], nt_dims, preferred_element_type=F32)
              + lax.dot_general(li16, bim[...], nt_dims, preferred_element_type=F32) + d_ref[...] * dy)
        du_ref[...] = du.astype(du_ref.dtype)

    blk = lambda b, c: (b, 0, 0)
    mat = pl.BlockSpec((None, LANE, W), blk)
    vec = pl.BlockSpec((None, 1, W), blk)
    acc8 = pl.BlockSpec((None, 8, W), blk)
    ck = pl.BlockSpec((None, None, 8, W), lambda b, c: (b, nch - 1 - c, 0, 0))
    return _tc_call(
        body, name=name, grid=(NB, nch),
        in_specs=[pl.BlockSpec((Q, LANE), lambda b, c: (nch - 1 - c, u_off + b)),
                  pl.BlockSpec((Q, LANE), lambda b, c: (nch - 1 - c, b)), ck, ck, mat, mat, mat, mat, vec, vec,
                  pl.BlockSpec((1, LANE), lambda b, c: (0, b))],
        out_specs=[pl.BlockSpec((Q, LANE), lambda b, c: (nch - 1 - c, b)), mat, mat, mat, mat, acc8, acc8,
                   pl.BlockSpec((None, 8, LANE), blk)],
        out_shape=[jax.ShapeDtypeStruct((T, width), BF16)] + [jax.ShapeDtypeStruct((NB, LANE, W), F32)] * 4
                  + [jax.ShapeDtypeStruct((NB, 8, W), F32)] * 2 + [jax.ShapeDtypeStruct((NB, 8, LANE), F32)],
        scratch_shapes=[pltpu.VMEM((Q + 8, W), F32), pltpu.VMEM((Q + 8, W), F32), pltpu.VMEM((Q, W), F32),
                        pltpu.VMEM((Q, W), F32), pltpu.VMEM((8, W), F32), pltpu.VMEM((8, W), F32)],
        compiler_params=_params(("parallel", "arbitrary")),
    )(proj, dyg, ck_r, ck_i, p["bre"], p["bim"], p["cre"], p["cim"], p["ar"], p["ai"], p["d"])


def _block_diag(w):
    G, H, P = w.shape
    eye = jnp.eye(8, dtype=w.dtype)
    return (w.reshape(G // 8, 8, H, 1, P) * eye[None, :, None, :, None]).reshape(G // 8, 8 * H, 8 * P)


def _block_diag_t(d, H, P):
    NB = d.shape[0]
    d = d.reshape(NB, 8, H, 8, P)
    eye = jnp.eye(8, dtype=d.dtype)
    return jnp.sum(d * eye[None, :, None, :, None], axis=3).reshape(NB * 8, H, P)


def _attn_fwd(name, q, kv, heads):
    T, D = q.shape
    Mm = kv.shape[0]
    hd = D // heads
    tq = _pick(T, (512, 256, 128))
    scale = hd ** -0.5

    def body(q_ref, k_ref, v_ref, o_ref):
        s = lax.dot_general(q_ref[...], k_ref[...], (((1,), (1,)), ((), ())), preferred_element_type=F32) * scale
        s = s - jnp.max(s, axis=-1, keepdims=True)
        e = jnp.exp(s)
        p = e / jnp.sum(e, axis=-1, keepdims=True)
        o_ref[...] = jnp.dot(p.astype(BF16), v_ref[...], preferred_element_type=F32).astype(o_ref.dtype)

    return _tc_call(
        body, name=name, grid=(heads, T // tq),
        in_specs=[pl.BlockSpec((tq, hd), lambda h, i: (i, h)), pl.BlockSpec((Mm, hd), lambda h, i: (0, h)),
                  pl.BlockSpec((Mm, hd), lambda h, i: (0, heads + h))],
        out_specs=pl.BlockSpec((tq, hd), lambda h, i: (i, h)),
        out_shape=jax.ShapeDtypeStruct((T, D), BF16),
        compiler_params=_params(("parallel", "parallel")),
    )(q, kv, kv)


def _attn_bwd(name, q, kv, do, heads):
    T, D = q.shape
    Mm = kv.shape[0]
    hd = D // heads
    tq = _pick(T, (512, 256, 128))
    scale = hd ** -0.5
    nt_dims = (((1,), (1,)), ((), ()))
    tn_dims = (((0,), (0,)), ((), ()))

    def body(q_ref, k_ref, v_ref, do_ref, dq_ref, dk_ref, dv_ref):
        i = pl.program_id(1)

        @pl.when(i == 0)
        def _():
            dk_ref[...] = jnp.zeros_like(dk_ref)
            dv_ref[...] = jnp.zeros_like(dv_ref)

        qv, kvl, vv, dov = q_ref[...], k_ref[...], v_ref[...], do_ref[...]
        s = lax.dot_general(qv, kvl, nt_dims, preferred_element_type=F32) * scale
        s = s - jnp.max(s, axis=-1, keepdims=True)
        e = jnp.exp(s)
        p = e / jnp.sum(e, axis=-1, keepdims=True)
        p16 = p.astype(BF16)
        dv_ref[...] += lax.dot_general(p16, dov, tn_dims, preferred_element_type=F32)
        dp = lax.dot_general(dov, vv, nt_dims, preferred_element_type=F32)
        ds = (p * (dp - jnp.sum(dp * p, axis=-1, keepdims=True)) * scale).astype(BF16)
        dq_ref[...] = jnp.dot(ds, kvl, preferred_element_type=F32).astype(dq_ref.dtype)
        dk_ref[...] += lax.dot_general(ds, qv, tn_dims, preferred_element_type=F32)

    return _tc_call(
        body, name=name, grid=(heads, T // tq),
        in_specs=[pl.BlockSpec((tq, hd), lambda h, i: (i, h)), pl.BlockSpec((Mm, hd), lambda h, i: (0, h)),
                  pl.BlockSpec((Mm, hd), lambda h, i: (0, heads + h)), pl.BlockSpec((tq, hd), lambda h, i: (i, h))],
        out_specs=[pl.BlockSpec((tq, hd), lambda h, i: (i, h)), pl.BlockSpec((Mm, hd), lambda h, i: (0, h)),
                   pl.BlockSpec((Mm, hd), lambda h, i: (0, h))],
        out_shape=[jax.ShapeDtypeStruct((T, D), BF16), jax.ShapeDtypeStruct((Mm, D), F32),
                   jax.ShapeDtypeStruct((Mm, D), F32)],
        compiler_params=_params(("parallel", "arbitrary")),
    )(q, kv, kv, do)


def _position():
    return lax.axis_index("x"), lax.axis_index("y"), lax.axis_index("c")


def _allgather(name, blk, row_mode):
    L = blk.shape[0]
    out_shape = (L, N_DEV) + blk.shape[1:] if row_mode else (N_DEV,) + blk.shape
    x_ref = jax.new_ref(blk, memory_space=pltpu.MemorySpace.HBM)
    out_ref = jax.empty_ref(jax.ShapeDtypeStruct(out_shape, blk.dtype), memory_space=pltpu.MemorySpace.HBM)

    def body(send_sems, recv_sems, local_sem):
        x, y, c = _position()
        me, sibling = (x, y, c), (x, y, 1 - c)
        chips = [(1 - x, y), (x, 1 - y), (1 - x, 1 - y)]
        barrier = pltpu.get_barrier_semaphore()
        for peer in [sibling] + [(*chip, c) for chip in chips]:
            pl.semaphore_signal(barrier, inc=1, device_id=peer, device_id_type=MESH_ID)
        pl.semaphore_wait(barrier, 4)

        def slot(px, py, pc):
            b = 4 * px + 2 * py + pc
            return out_ref.at[:, b] if row_mode else out_ref.at[b]

        def copy(k, block, to, src=None):
            return pltpu.make_async_remote_copy(
                src_ref=slot(*block) if src is None else src, dst_ref=slot(*block),
                send_sem=send_sems.at[k], recv_sem=recv_sems.at[k], device_id=to, device_id_type=MESH_ID)

        mine = pltpu.make_async_copy(x_ref, slot(*me), local_sem)
        mine.start()
        first = [copy(0, me, sibling, src=x_ref)]
        first += [copy(1 + j, me, (*chip, c), src=x_ref) for j, chip in enumerate(chips)]
        for cp in first:
            cp.start()
        passed = [copy(4 + j, (*chip, c), sibling) for j, chip in enumerate(chips)]
        for j, chip in enumerate(chips):
            copy(1 + j, (*chip, c), me).wait_recv()
            passed[j].start()
        copy(0, sibling, me).wait_recv()
        for j, chip in enumerate(chips):
            copy(4 + j, (*chip, 1 - c), me).wait_recv()
        for cp in first + passed:
            cp.wait_send()
        mine.wait()

    pl.kernel(
        body, mesh=plsc.ScalarSubcoreMesh(axis_name="sequencer", num_cores=1), name=name,
        scratch_types=(pltpu.SemaphoreType.DMA((7,)), pltpu.SemaphoreType.DMA((7,)), pltpu.SemaphoreType.DMA),
        compiler_params=pltpu.CompilerParams(collective_id=AG_COLLECTIVE_ID),
    )()
    return out_ref[...]


def _sequencer_kernel(name, body, scratch_types, collective_id):
    pl.kernel(
        body, mesh=plsc.ScalarSubcoreMesh(axis_name="sequencer", num_cores=1), name=name,
        scratch_types=scratch_types, compiler_params=pltpu.CompilerParams(collective_id=collective_id),
    )()


def _handshake(peers):
    barrier = pltpu.get_barrier_semaphore()
    for peer in peers:
        pl.semaphore_signal(barrier, inc=1, device_id=peer, device_id_type=MESH_ID)
    pl.semaphore_wait(barrier, len(peers))


def _rs_sibling(name, grads):
    hbm = pltpu.MemorySpace.HBM
    g_ref = jax.new_ref(grads, memory_space=hbm)
    out_ref = jax.empty_ref(jax.ShapeDtypeStruct((4,) + grads.shape[1:], grads.dtype), memory_space=hbm)

    def body(send_sems, recv_sems):
        x, y, c = _position()
        sibling = (x, y, 1 - c)
        _handshake([sibling])
        copies = [pltpu.make_async_remote_copy(
            src_ref=g_ref.at[2 * q + (1 - c)], dst_ref=out_ref.at[q], send_sem=send_sems.at[q],
            recv_sem=recv_sems.at[q], device_id=sibling, device_id_type=MESH_ID) for q in range(4)]
        for cp in copies:
            cp.start()
        for cp in copies:
            cp.wait_recv()
        for cp in copies:
            cp.wait_send()

    _sequencer_kernel(name, body, (pltpu.SemaphoreType.DMA((4,)), pltpu.SemaphoreType.DMA((4,))),
                      RS_SIBLING_COLLECTIVE_ID)
    return out_ref[...]


def _rs_chips(name, part):
    hbm = pltpu.MemorySpace.HBM
    p_ref = jax.new_ref(part, memory_space=hbm)
    out_ref = jax.empty_ref(jax.ShapeDtypeStruct(part.shape, part.dtype), memory_space=hbm)

    def body(send_sems, recv_sems, local_sem):
        x, y, c = _position()
        flips = [(1 - x, y), (x, 1 - y), (1 - x, 1 - y)]
        _handshake([(fx, fy, c) for fx, fy in flips])
        mine = pltpu.make_async_copy(p_ref.at[2 * x + y], out_ref.at[3], local_sem)
        copies = [pltpu.make_async_remote_copy(
            src_ref=p_ref.at[2 * fx + fy], dst_ref=out_ref.at[k], send_sem=send_sems.at[k], recv_sem=recv_sems.at[k],
            device_id=(fx, fy, c), device_id_type=MESH_ID) for k, (fx, fy) in enumerate(flips)]
        for cp in copies:
            cp.start()
        mine.start()
        for cp in copies:
            cp.wait_recv()
        for cp in copies:
            cp.wait_send()
        mine.wait()

    _sequencer_kernel(name, body, (pltpu.SemaphoreType.DMA((3,)), pltpu.SemaphoreType.DMA((3,)),
                                   pltpu.SemaphoreType.DMA), RS_CHIPS_COLLECTIVE_ID)
    return out_ref[...]


def _adamw_math(g, w, m, v):
    m = ADAM_B1 * m + (1.0 - ADAM_B1) * g
    v = ADAM_B2 * v + (1.0 - ADAM_B2) * (g * g)
    m_hat = m / (1.0 - ADAM_B1 ** ADAM_STEP)
    v_hat = v / (1.0 - ADAM_B2 ** ADAM_STEP)
    delta = -ADAM_LR * (m_hat / (jnp.sqrt(v_hat) + ADAM_EPS) + ADAM_WD * w)
    return delta, m, v


def _sum_adamw(name, parts, nparts, w, m, v):
    R, C = w.shape
    tile = _ew_tile(R, C, nparts + 7)

    def fn(*vals):
        g = vals[0].astype(F32)
        for pv in vals[1:nparts]:
            g = g + pv.astype(F32)
        delta, nm, nv = _adamw_math(g, *vals[nparts:])
        return g, delta, nm, nv

    rows = [(parts, C, 0, k * R) for k in range(nparts)] + [(a, C, 0, 0) for a in (w, m, v)]
    return _rowwise(name, fn, rows, [], [(C, F32)] * 4, tile=tile, nrows=R)


def _ew_tile(R, C, nblocks):
    budget = (VMEM_LIMIT * 3) // 4
    return _pick(R, tuple(t for t in (1024, 512, 256, 128, 64, 32, 16, 8) if 8 * t * C * nblocks <= budget))


def _pair_sum(name, grads, landed, c_idx):
    _, r, c = grads.shape
    tile = _ew_tile(r, c, 3)

    def body(c_ref, g_ref, s_ref, o_ref):
        o_ref[...] = (g_ref[...].astype(F32) + s_ref[...].astype(F32)).astype(o_ref.dtype)

    return _tc_call(
        body, name=name,
        grid_spec=pltpu.PrefetchScalarGridSpec(
            num_scalar_prefetch=1, grid=(4, r // tile),
            in_specs=[pl.BlockSpec((None, tile, c), lambda q, i, c_ref: (2 * q + c_ref[0], i, 0)),
                      pl.BlockSpec((None, tile, c), lambda q, i, c_ref: (q, i, 0))],
            out_specs=pl.BlockSpec((None, tile, c), lambda q, i, c_ref: (q, i, 0))),
        out_shape=jax.ShapeDtypeStruct((4, r, c), BF16),
        compiler_params=_params(("parallel", "parallel")),
    )(c_idx, grads, landed)


def _adamw_layer(name, got, w, m, v, layer, prev):
    L, r, c = w.shape
    tile = _ew_tile(r, c, 11)

    def body(g0, g1, g2, g3, w_ref, m_ref, v_ref, *rest):
        outs = rest[-4:]
        g = (g0[...].astype(F32) + g1[...].astype(F32)) + (g2[...].astype(F32) + g3[...].astype(F32))
        delta, nm, nv = _adamw_math(g, w_ref[...], m_ref[...], v_ref[...])
        for ref, val in zip(outs, (g, delta, nm, nv)):
            ref[...] = val

    slab = pl.BlockSpec((None, tile, c), lambda i: (layer, i, 0))
    in_specs = [pl.BlockSpec((None, tile, c), lambda i, k=k: (k, i, 0)) for k in range(4)] + [slab] * 3
    args = [got, got, got, got, w, m, v]
    aliases = {}
    if prev is not None:
        in_specs += [ANY] * 4
        args += list(prev)
        aliases = {7 + k: k for k in range(4)}
    return _tc_call(
        body, name=name, grid=(r // tile,), in_specs=in_specs, out_specs=[slab] * 4,
        out_shape=[jax.ShapeDtypeStruct((L, r, c), F32)] * 4, input_output_aliases=aliases,
        compiler_params=_params(("parallel",)),
    )(*args)


def _reduce_pipeline(name, layer, grads, w, m, v, state):
    c_idx = lax.axis_index("c").astype(jnp.int32).reshape(1)
    landed = _rs_sibling(name + "_rs_sibling", grads)
    yield
    part = _pair_sum(name + "_pair_sum", grads, landed, c_idx)
    got = _rs_chips(name + "_rs_chips", part)
    yield
    yield
    yield
    state[name] = _adamw_layer(name + "_adamw", got, w, m, v, layer, state.get(name))


def _loss_head(name, x, g, target):
    T, D = x.shape

    def fn(xv, tv, gv):
        def f(xx, gg):
            err = _rms(xx, gg) - tv
            return 0.5 * jnp.sum(jnp.mean(err * err, axis=-1, keepdims=True))

        loss, (dx, dg) = jax.value_and_grad(f, argnums=(0, 1))(xv, gv)
        return dx, dx, jnp.full((8, LANE), loss, F32), jnp.sum(dg, axis=0, keepdims=True)

    dx, dx16, loss, dg = _rowwise(name, fn, [(x, D, 0, 0), (target, D, 0, 0)], [g], [(D, F32), (D, BF16)],
                                  [(8, LANE), (1, D)], tile=128)
    return loss[0, 0], dx, dx16, dg


_SHARDED_COL = ("w_in", "conv_dw_w", "conv_w_pw", "ssm_w_glu", "xa_w_kv", "ffn_w_up", "ffn_dw_w")
_SHARDED_ROW = ("w_out", "xa_w_q", "xa_w_o", "ffn_w_down")
_WEIGHTS = ['mix_norm_g', 'w_in', 'conv_dw_w', 'conv_dw_b', 'conv_ln_g', 'conv_ln_b', 'conv_w_pw', 'ssm_a_re',
            'ssm_a_im', 'ssm_log_dt', 'ssm_b_re', 'ssm_b_im', 'ssm_c_re', 'ssm_c_im', 'ssm_d', 'ssm_w_glu', 'w_out',
            'xa_norm_g', 'mem_norm_g', 'xa_w_q', 'xa_w_kv', 'xa_w_o', 'ffn_norm_g', 'ffn_w_up', 'ffn_dw_w',
            'ffn_w_down', 'final_norm_g']
_FWD = ['x', 'mem'] + _WEIGHTS
_AG_ORDER = ("w_in", "conv_dw_w", "conv_w_pw", "ssm_w_glu", "w_out", "xa_w_q", "xa_w_kv", "xa_w_o", "ffn_w_up",
             "ffn_dw_w", "ffn_w_down")


def _pad_rows(a, rows):
    return jnp.pad(a, ((0, 0), (0, rows - a.shape[1]), (0, 0)))


def _step(inp, target, mom_m, mom_v):
    _LAST_CALL.clear()
    x0 = inp["x"][0]
    mem = inp["mem"][0]
    T, D = x0.shape
    L = inp["w_in"].shape[0]
    CW = inp["conv_dw_b"].shape[1]
    SW = inp["ssm_d"].shape[1]
    DFF = inp["ffn_w_down"].shape[1] * N_DEV
    G = SW // SSM_GROUP
    NB = SW // LANE
    u_off = (2 * CW) // LANE
    gate_off = (2 * CW + SW) // 1024

    gathered = {n: [None] * L for n in _AG_ORDER}
    filter_rows = {"conv_dw_w": HALO, "ffn_dw_w": 8}
    for l in range(L):
        for n in _AG_ORDER:
            blk = inp[n][l:l + 1]
            blk = _pad_rows(blk, filter_rows[n]) if n in filter_rows else blk.astype(BF16)
            if n in _SHARDED_COL:
                gathered[n][l] = _allgather("ag_" + n, blk, False)
            else:
                full = _allgather("ag_" + n, blk, True)
                gathered[n][l] = full.reshape(1, N_DEV * full.shape[2], full.shape[3])

    def W(n, l):
        return _W(gathered[n][l], 0, n in _SHARDED_COL)

    def dw_filter(n, l):
        g = gathered[n][l]
        return jnp.transpose(g[:, 0], (1, 0, 2)).reshape(g.shape[2], N_DEV * g.shape[3])

    def row(n, l):
        return inp[n][l][None, :]

    ssm_raw, ssm_p = [], []
    for l in range(L):
        a_re, a_im, ldt = inp["ssm_a_re"][l], inp["ssm_a_im"][l], inp["ssm_log_dt"][l][:, None]
        rep = lambda a: jnp.repeat(a, SSM_GROUP, axis=0)
        flat = lambda b: jnp.transpose(b, (0, 2, 1)).reshape(G * SSM_GROUP, SSM_STATE)
        raw = (a_re, a_im, ldt, rep(a_re), rep(a_im), rep(ldt), flat(inp["ssm_b_re"][l]), flat(inp["ssm_b_im"][l]))
        abar_re, abar_im, bbar_re, bbar_im = _ssm_prep("ssm_prep", raw)
        bbar_re = bbar_re.reshape(G, SSM_GROUP, SSM_STATE)
        bbar_im = bbar_im.reshape(G, SSM_GROUP, SSM_STATE)
        ssm_raw.append(raw)
        ssm_p.append(dict(
            bre=_block_diag(bbar_re).astype(BF16), bim=_block_diag(bbar_im).astype(BF16),
            cre=_block_diag(inp["ssm_c_re"][l]).astype(BF16), cim=_block_diag(inp["ssm_c_im"][l]).astype(BF16),
            ar=abar_re.reshape(NB, 1, SSM_LANES), ai=abar_im.reshape(NB, 1, SSM_LANES), d=row("ssm_d", l)))

    saved = []
    x = x0
    for l in range(L):
        s = {"x_in": x}
        s["h1"] = _rms_fwd("rms_mix", x, row("mix_norm_g", l))
        s["proj"] = _mm_nn("mm_w_in", s["h1"], W("w_in", l), F32)
        s["hc"] = _glu_conv_fwd("conv_fwd", s["proj"], dw_filter("conv_dw_w", l), CW)
        s["hs"] = _rowwise("conv_post", _convpost, [(s["hc"], CW, 0, 0)],
                           [row("conv_dw_b", l), row("conv_ln_g", l), row("conv_ln_b", l)], [(CW, BF16)])[0]
        s["ya"] = _mm_nn("mm_w_pw", s["hs"], W("conv_w_pw", l), F32)
        s["yg"], s["ck_r"], s["ck_i"] = _ssm_fwd("ssm_fwd", s["proj"], u_off, ssm_p[l], SW)
        s["gg"] = _mm_nn("mm_w_glu", s["yg"], W("ssm_w_glu", l), F32)
        nmix = D // 1024
        mix_rows = [(s["proj"], 1024, gate_off, 0), (s["proj"], 1024, gate_off + nmix, 0), (s["ya"], 1024, 0, 0),
                    (s["gg"], 1024, 0, 0), (s["gg"], 1024, nmix, 0)]
        s["mix_rows"] = mix_rows
        s["mix"] = _rowwise("mix_fwd", _mixf, mix_rows, [], [(D, BF16)], ncol=nmix)[0]
        x = _mm_nn("mm_w_out", s["mix"], W("w_out", l), F32, add=x)
        s["x1"] = x
        s["h2"] = _rms_fwd("rms_xa", x, row("xa_norm_g", l))
        s["q"] = _mm_nn("mm_w_q", s["h2"], W("xa_w_q", l), BF16)
        s["mn"] = _rms_fwd("rms_mem", mem, row("mem_norm_g", l))
        s["kv"] = _mm_nn("mm_w_kv", s["mn"], W("xa_w_kv", l), BF16)
        s["o"] = _attn_fwd("attn_fwd", s["q"], s["kv"], XA_HEADS)
        x = _mm_nn("mm_w_o", s["o"], W("xa_w_o", l), F32, add=x)
        s["x2"] = x
        s["h3"] = _rms_fwd("rms_ffn", x, row("ffn_norm_g", l))
        s["up"] = _mm_nn("mm_w_up", s["h3"], W("ffn_w_up", l), F32)
        s["act"] = _ffn_conv_fwd("ffn_conv_fwd", s["up"], dw_filter("ffn_dw_w", l), DFF)
        x = _mm_nn("mm_w_down", s["act"], W("ffn_w_down", l), F32, add=x)
        saved.append(s)

    loss_part, dx, dx16, d_final_g = _loss_head("loss_head", x, inp["final_norm_g"][None, :], target[0])

    big = _SHARDED_COL + _SHARDED_ROW
    small = {n: [None] * L for n in _WEIGHTS if n not in big and n != "final_norm_g"}
    pads = {"conv_dw_w": HALO, "ffn_dw_w": 8}
    shards = {n: tuple(_pad_rows(a, pads[n]) if n in pads else a for a in (inp[n], mom_m[n], mom_v[n])) for n in big}
    state, queue = {}, []
    names = [n for n in _WEIGHTS if n not in big]
    sizes = [inp[n].size for n in names]
    total = sum(sizes)
    pack_w = 8 * LANE
    rows_p = -(-total // (LANE * pack_w)) * LANE
    padn = rows_p * pack_w - total

    def pack(parts, fill):
        return jnp.concatenate([p.reshape(-1) for p in parts] + [jnp.full((padn,), fill, F32)]).reshape(rows_p, pack_w)

    def tick():
        for gen in list(queue):
            if next(gen, "done") == "done":
                queue.remove(gen)

    def emit(n, l, g):
        queue.append(_reduce_pipeline(n, l, g.astype(BF16), *shards[n], state))
        tick()

    for l in reversed(range(L)):
        s = saved[l]
        dact = _mm_nt("mm_w_down_t", dx16, W("ffn_w_down", l), BF16)
        emit("ffn_w_down", l, _mm_tn("mm_dw_down", s["act"], dx16).reshape(N_DEV, DFF // N_DEV, D))
        d_up, d_ffn_dw = _ffn_conv_bwd("ffn_conv_bwd", s["up"], dw_filter("ffn_dw_w", l), dact, DFF)
        emit("ffn_dw_w", l, jnp.transpose(d_ffn_dw.reshape(8, N_DEV, 2 * DFF // N_DEV), (1, 0, 2)))
        dh3 = _mm_nt("mm_w_up_t", d_up, W("ffn_w_up", l), BF16)
        emit("ffn_w_up", l, _mm_tn("mm_dw_up", s["h3"], d_up, nb=2 * DFF // N_DEV))
        dx, dx16, small["ffn_norm_g"][l] = _rms_bwd("rms_ffn_bwd", s["x2"], row("ffn_norm_g", l), dh3, dx)
        do = _mm_nt("mm_w_o_t", dx16, W("xa_w_o", l), BF16)
        emit("xa_w_o", l, _mm_tn("mm_dw_o", s["o"], dx16).reshape(N_DEV, D // N_DEV, D))
        dq, dk, dv = _attn_bwd("attn_bwd", s["q"], s["kv"], do, XA_HEADS)
        dkv = jnp.concatenate([dk, dv], axis=1).astype(BF16)
        dh2 = _mm_nt("mm_w_q_t", dq, W("xa_w_q", l), BF16)
        emit("xa_w_q", l, _mm_tn("mm_dw_q", s["h2"], dq).reshape(N_DEV, D // N_DEV, D))
        dmn = _mm_nt("mm_w_kv_t", dkv, W("xa_w_kv", l), BF16)
        emit("xa_w_kv", l, _mm_tn("mm_dw_kv", s["mn"], dkv, nb=2 * D // N_DEV))

        def mem_bwd(mv, dv_, gv):
            _, vjp = jax.vjp(_rms, mv, gv)
            return jnp.sum(vjp(dv_.astype(F32))[1], axis=0, keepdims=True)

        small["mem_norm_g"][l] = _rowwise("rms_mem_bwd", mem_bwd, [(mem, D, 0, 0), (dmn, D, 0, 0)],
                                          [row("mem_norm_g", l)], [], [(1, D)], tile=128)[0]
        dx, dx16, small["xa_norm_g"][l] = _rms_bwd("rms_xa_bwd", s["x1"], row("xa_norm_g", l), dh2, dx)
        dmix = _mm_nt("mm_w_out_t", dx16, W("w_out", l), BF16)
        emit("w_out", l, _mm_tn("mm_dw_out", s["mix"], dx16).reshape(N_DEV, D // N_DEV, D))

        def mix_bwd(gla, glb, ya, ga, gb, dm):
            _, vjp = jax.vjp(_mixf, gla, glb, ya, ga, gb)
            return vjp(dm.astype(F32))

        nmix = D // 1024
        dgla, dglb, dya, dga, dgb = _rowwise("mix_bwd", mix_bwd, s["mix_rows"] + [(dmix, 1024, 0, 0)], [],
                                             [(D, BF16)] * 5, ncol=nmix)
        dgg = jnp.concatenate([dga, dgb], axis=1)
        dyg = _mm_nt("mm_w_glu_t", dgg, W("ssm_w_glu", l), BF16)
        emit("ssm_w_glu", l, _mm_tn("mm_dw_glu", s["yg"], dgg, nb=2 * D // N_DEV))
        du, dbr, dbi, dcr, dci, dar, dai, dd = _ssm_bwd("ssm_bwd", s["proj"], u_off, ssm_p[l], s["ck_r"], s["ck_i"],
                                                        dyg, SW)
        cots = (dar[:, 0, :].reshape(G, SSM_STATE), dai[:, 0, :].reshape(G, SSM_STATE),
                _block_diag_t(dbr, SSM_GROUP, SSM_STATE).reshape(G * SSM_GROUP, SSM_STATE),
                _block_diag_t(dbi, SSM_GROUP, SSM_STATE).reshape(G * SSM_GROUP, SSM_STATE))
        g_are, g_aim, g_ldt, g_bre, g_bim = _ssm_prep_bwd("ssm_prep_bwd", ssm_raw[l], cots)
        small["ssm_a_re"][l], small["ssm_a_im"][l], small["ssm_log_dt"][l] = g_are, g_aim, g_ldt[:, 0]
        small["ssm_b_re"][l] = jnp.transpose(g_bre.reshape(G, SSM_GROUP, SSM_STATE), (0, 2, 1))
        small["ssm_b_im"][l] = jnp.transpose(g_bim.reshape(G, SSM_GROUP, SSM_STATE), (0, 2, 1))
        small["ssm_c_re"][l] = _block_diag_t(dcr, SSM_GROUP, SSM_STATE)
        small["ssm_c_im"][l] = _block_diag_t(dci, SSM_GROUP, SSM_STATE)
        small["ssm_d"][l] = dd[:, 0, :].reshape(SW)
        dhs = _mm_nt("mm_w_pw_t", dya, W("conv_w_pw", l), BF16)
        emit("conv_w_pw", l, _mm_tn("mm_dw_pw", s["hs"], dya, nb=D // N_DEV))

        def post_bwd(hc, dh, b, lg, lb):
            _, vjp = jax.vjp(_convpost, hc, b, lg, lb)
            dhc, db, dlg, dlb = vjp(dh.astype(F32))
            return dhc, jnp.sum(db, axis=0, keepdims=True), jnp.sum(dlg, axis=0, keepdims=True), \
                jnp.sum(dlb, axis=0, keepdims=True)

        dhc, small["conv_dw_b"][l], small["conv_ln_g"][l], small["conv_ln_b"][l] = _rowwise(
            "conv_post_bwd", post_bwd, [(s["hc"], CW, 0, 0), (dhs, CW, 0, 0)],
            [row("conv_dw_b", l), row("conv_ln_g", l), row("conv_ln_b", l)], [(CW, F32)], [(1, CW)] * 3)
        da, db, d_conv_dw = _glu_conv_bwd("conv_bwd", s["proj"], dw_filter("conv_dw_w", l), dhc, CW)
        emit("conv_dw_w", l, jnp.transpose(d_conv_dw.reshape(HALO, N_DEV, CW // N_DEV), (1, 0, 2)))
        dproj = jnp.concatenate([da, db, du, dgla, dglb], axis=1)
        dh1 = _mm_nt("mm_w_in_t", dproj, W("w_in", l), BF16)
        dx, dx16, small["mix_norm_g"][l] = _rms_bwd("rms_mix_bwd", s["x_in"], row("mix_norm_g", l), dh1, dx)
        if l == 0:
            flat_g = jnp.concatenate([(jnp.stack(small[n]) if n != "final_norm_g" else d_final_g).reshape(-1)
                                      for n in names])
            g_all = _allgather("ag_small_grads", pack([flat_g], 0.0)[None], False)
        emit("w_in", l, _mm_tn("mm_dw_in", s["h1"], dproj, nb=dproj.shape[1] // N_DEV))

    while queue:
        tick()
    results = {n: [o[:, :inp[n].shape[1], :] for o in state[n]] if n in pads else state[n] for n in big}
    outs = _sum_adamw("small_adamw", g_all.reshape(N_DEV * rows_p, pack_w), N_DEV,
                      pack([inp[n] for n in names], 0.0), pack([mom_m[n] for n in names], 0.0),
                      pack([mom_v[n] for n in names], 1.0))
    offs = 0
    for n, sz in zip(names, sizes):
        results[n] = [o.reshape(-1)[offs:offs + sz].reshape(inp[n].shape) for o in outs]
        offs += sz

    _LAST_CALL.clear()
    loss = lax.psum(loss_part, ("x", "y", "c"))
    grad_x = dx[None]
    return (loss, grad_x, *[results[n][0] for n in _WEIGHTS], *[results[n][1] for n in _WEIGHTS],
            *[results[n][2] for n in _WEIGHTS], *[results[n][3] for n in _WEIGHTS])


def kernel(x, mem, mix_norm_g, w_in, conv_dw_w, conv_dw_b, conv_ln_g, conv_ln_b, conv_w_pw, ssm_a_re, ssm_a_im, ssm_log_dt, ssm_b_re, ssm_b_im, ssm_c_re, ssm_c_im, ssm_d, ssm_w_glu, w_out, xa_norm_g, mem_norm_g, xa_w_q, xa_w_kv, xa_w_o, ffn_norm_g, ffn_w_up, ffn_dw_w, ffn_w_down, final_norm_g, loss_target, m_mix_norm_g, m_w_in, m_conv_dw_w, m_conv_dw_b, m_conv_ln_g, m_conv_ln_b, m_conv_w_pw, m_ssm_a_re, m_ssm_a_im, m_ssm_log_dt, m_ssm_b_re, m_ssm_b_im, m_ssm_c_re, m_ssm_c_im, m_ssm_d, m_ssm_w_glu, m_w_out, m_xa_norm_g, m_mem_norm_g, m_xa_w_q, m_xa_w_kv, m_xa_w_o, m_ffn_norm_g, m_ffn_w_up, m_ffn_dw_w, m_ffn_w_down, m_final_norm_g, v_mix_norm_g, v_w_in, v_conv_dw_w, v_conv_dw_b, v_conv_ln_g, v_conv_ln_b, v_conv_w_pw, v_ssm_a_re, v_ssm_a_im, v_ssm_log_dt, v_ssm_b_re, v_ssm_b_im, v_ssm_c_re, v_ssm_c_im, v_ssm_d, v_ssm_w_glu, v_w_out, v_xa_norm_g, v_mem_norm_g, v_xa_w_q, v_xa_w_kv, v_xa_w_o, v_ffn_norm_g, v_ffn_w_up, v_ffn_dw_w, v_ffn_w_down, v_final_norm_g):
    args = (x, mem, mix_norm_g, w_in, conv_dw_w, conv_dw_b, conv_ln_g, conv_ln_b, conv_w_pw, ssm_a_re, ssm_a_im, ssm_log_dt, ssm_b_re, ssm_b_im, ssm_c_re, ssm_c_im, ssm_d, ssm_w_glu, w_out, xa_norm_g, mem_norm_g, xa_w_q, xa_w_kv, xa_w_o, ffn_norm_g, ffn_w_up, ffn_dw_w, ffn_w_down, final_norm_g)
    ms = (m_mix_norm_g, m_w_in, m_conv_dw_w, m_conv_dw_b, m_conv_ln_g, m_conv_ln_b, m_conv_w_pw, m_ssm_a_re, m_ssm_a_im, m_ssm_log_dt, m_ssm_b_re, m_ssm_b_im, m_ssm_c_re, m_ssm_c_im, m_ssm_d, m_ssm_w_glu, m_w_out, m_xa_norm_g, m_mem_norm_g, m_xa_w_q, m_xa_w_kv, m_xa_w_o, m_ffn_norm_g, m_ffn_w_up, m_ffn_dw_w, m_ffn_w_down, m_final_norm_g)
    vs = (v_mix_norm_g, v_w_in, v_conv_dw_w, v_conv_dw_b, v_conv_ln_g, v_conv_ln_b, v_conv_w_pw, v_ssm_a_re, v_ssm_a_im, v_ssm_log_dt, v_ssm_b_re, v_ssm_b_im, v_ssm_c_re, v_ssm_c_im, v_ssm_d, v_ssm_w_glu, v_w_out, v_xa_norm_g, v_mem_norm_g, v_xa_w_q, v_xa_w_kv, v_xa_w_o, v_ffn_norm_g, v_ffn_w_up, v_ffn_dw_w, v_ffn_w_down, v_final_norm_g)
    return _step(dict(zip(_FWD, args)), loss_target, dict(zip(_WEIGHTS, ms)), dict(zip(_WEIGHTS, vs)))
```

```python
import functools

import jax
import jax.numpy as jnp
from jax import lax
from jax.experimental import pallas as pl
from jax.experimental.pallas import tpu as pltpu
from jax.experimental.pallas import tpu_sc as plsc

F32 = jnp.float32
BF16 = jnp.bfloat16
MESH_ID = pl.DeviceIdType.MESH
N_DEV = 8
EPS = 1e-6
VMEM_LIMIT = 48 * 1024 * 1024
ANY = pl.BlockSpec(memory_space=pl.ANY)

ADAM_LR = 0.001
ADAM_B1 = 0.9
ADAM_B2 = 0.999
ADAM_EPS = 1e-08
ADAM_WD = 0.01
ADAM_STEP = 10

CONV_K = 31
FFN_K = 3
XA_HEADS = 4
SSM_GROUP = 16
SSM_STATE = 64
HALO = 32
LANE = 128
SSM_LANES = 512
AG_COLLECTIVE_ID = 1
RS_SIBLING_COLLECTIVE_ID = 2
RS_CHIPS_COLLECTIVE_ID = 3


def _pick(n, prefs):
    for p in prefs:
        if p <= n and n % p == 0:
            return p
    return n


def _params(sem, vmem=VMEM_LIMIT):
    return pltpu.CompilerParams(dimension_semantics=sem, vmem_limit_bytes=vmem)


_LAST_CALL = []


def _tc_call(*call_args, **call_kwargs):
    call = pl.pallas_call(*call_args, **call_kwargs)

    def run(*args):
        args = list(args)
        if _LAST_CALL:
            i = next(k for k, a in enumerate(args) if a.ndim >= 2)
            args[i] = lax.optimization_barrier((args[i], _LAST_CALL[0]))[0]
        out = call(*args)
        _LAST_CALL[:] = [out[0] if isinstance(out, (tuple, list)) else out]
        return out

    return run


def _sigmoid(x):
    return 1.0 / (1.0 + jnp.exp(-x))


def _silu(x):
    return x * _sigmoid(x)


def _gelu(x):
    return 0.5 * x * (1.0 + jnp.tanh(0.7978845608028654 * (x + 0.044715 * (x * x * x))))


def _rms(x, g):
    return x * lax.rsqrt(jnp.mean(x * x, axis=-1, keepdims=True) + EPS) * g


def _convpost(hc, bias, ln_g, ln_b):
    h = hc + bias
    mu = jnp.mean(h, axis=-1, keepdims=True)
    xc = h - mu
    y = xc * lax.rsqrt(jnp.mean(xc * xc, axis=-1, keepdims=True) + EPS)
    return _silu(y * ln_g + ln_b)


def _mixf(gla, glb, ya, ga, gb):
    return _sigmoid(gla) * ya + _sigmoid(glb) * (ga * _sigmoid(gb))


class _W:
    def __init__(self, arr, layer, blocked):
        self.arr, self.layer, self.blocked = arr, layer, blocked
        if blocked:
            _, _, self.K, self.nb = arr.shape
            self.N = N_DEV * self.nb
        else:
            _, self.K, self.N = arr.shape
            self.nb = self.N

    def spec(self, tk, tn, ki, ni):
        l = self.layer
        if self.blocked:
            per = self.nb // tn
            return pl.BlockSpec((None, None, tk, tn), lambda *g: (ni(*g) // per, l, ki(*g), ni(*g) % per))
        return pl.BlockSpec((None, tk, tn), lambda *g: (l, ki(*g), ni(*g)))


_M_TILES = (1024, 512, 256, 128, 64, 32, 16, 8)
_N_TILES = (1408, 1024, 896, 512, 256, 128)
_K_TILES = (512, 1408, 896, 256, 128)
MAX_FULL_K = 2048
_K_FULL_TILES = (2048, 1408, 1024, 896, 512, 256, 128)
MAX_FULL_T = 4096


def _mm_nn(name, a, w, out_dtype, add=None):
    M, K = a.shape
    assert K == w.K
    tm, tn = _pick(M, _M_TILES), _pick(w.nb, _N_TILES)
    tk = K if K <= MAX_FULL_K else _pick(K, _K_FULL_TILES)
    nk = K // tk

    def body(*refs):
        a_ref, w_ref = refs[:2]
        r_ref = refs[2] if add is not None else None
        o_ref = refs[3] if add is not None else refs[2]
        part = jnp.dot(a_ref[...].astype(BF16), w_ref[...], preferred_element_type=F32)
        if nk == 1:
            o_ref[...] = (part if add is None else part + r_ref[...]).astype(o_ref.dtype)
            return
        acc = refs[-1]
        k = pl.program_id(2)

        @pl.when(k == 0)
        def _():
            acc[...] = part

        @pl.when(k > 0)
        def _():
            acc[...] += part

        @pl.when(k == nk - 1)
        def _():
            res = acc[...]
            if add is not None:
                res = res + r_ref[...]
            o_ref[...] = res.astype(o_ref.dtype)

    in_specs = [pl.BlockSpec((tm, tk), lambda i, j, k: (i, k)),
                w.spec(tk, tn, lambda i, j, k: k, lambda i, j, k: j)]
    args = [a, w.arr]
    if add is not None:
        in_specs.append(pl.BlockSpec((tm, tn), lambda i, j, k: (i, j)))
        args.append(add)
    return _tc_call(
        body, name=name, grid=(M // tm, w.N // tn, nk), in_specs=in_specs,
        out_specs=pl.BlockSpec((tm, tn), lambda i, j, k: (i, j)),
        out_shape=jax.ShapeDtypeStruct((M, w.N), out_dtype),
        scratch_shapes=[] if nk == 1 else [pltpu.VMEM((tm, tn), F32)],
        compiler_params=_params(("parallel", "parallel", "arbitrary")),
    )(*args)


def _mm_nt(name, a, w, out_dtype):
    M, N = a.shape
    assert N == w.N
    tm, tkk = _pick(M, _M_TILES), _pick(w.K, _N_TILES)
    tnn = w.nb if w.nb <= MAX_FULL_K else _pick(w.nb, _K_FULL_TILES)
    nn = N // tnn

    def body(a_ref, w_ref, o_ref, *scratch):
        part = lax.dot_general(a_ref[...].astype(BF16), w_ref[...], (((1,), (1,)), ((), ())),
                               preferred_element_type=F32)
        if nn == 1:
            o_ref[...] = part.astype(o_ref.dtype)
            return
        acc = scratch[0]
        n = pl.program_id(2)

        @pl.when(n == 0)
        def _():
            acc[...] = part

        @pl.when(n > 0)
        def _():
            acc[...] += part

        @pl.when(n == nn - 1)
        def _():
            o_ref[...] = acc[...].astype(o_ref.dtype)

    return _tc_call(
        body, name=name, grid=(M // tm, w.K // tkk, nn),
        in_specs=[pl.BlockSpec((tm, tnn), lambda i, j, n: (i, n)),
                  w.spec(tkk, tnn, lambda i, j, n: j, lambda i, j, n: n)],
        out_specs=pl.BlockSpec((tm, tkk), lambda i, j, n: (i, j)),
        out_shape=jax.ShapeDtypeStruct((M, w.K), out_dtype),
        scratch_shapes=[] if nn == 1 else [pltpu.VMEM((tm, tkk), F32)],
        compiler_params=_params(("parallel", "parallel", "arbitrary")),
    )(a, w.arr)


def _mm_tn(name, a, b, nb=None):
    T, K = a.shape
    _, N = b.shape
    width = N if nb is None else nb
    assert T <= MAX_FULL_T
    tkk, tn = _pick(K, (512, 256, 128)), _pick(width, _N_TILES)
    per = width // tn

    def body(a_ref, b_ref, o_ref):
        o_ref[...] = lax.dot_general(a_ref[...].astype(BF16), b_ref[...].astype(BF16), (((0,), (0,)), ((), ())),
                                     preferred_element_type=F32).astype(o_ref.dtype)

    if nb is None:
        out_spec = pl.BlockSpec((tkk, tn), lambda j, i: (i, j))
        out_shape = jax.ShapeDtypeStruct((K, N), BF16)
    else:
        out_spec = pl.BlockSpec((None, tkk, tn), lambda j, i: (j // per, i, j % per))
        out_shape = jax.ShapeDtypeStruct((N_DEV, K, nb), BF16)
    return _tc_call(
        body, name=name, grid=(N // tn, K // tkk),
        in_specs=[pl.BlockSpec((T, tkk), lambda j, i: (0, i)),
                  pl.BlockSpec((T, tn), lambda j, i: (0, j))],
        out_specs=out_spec, out_shape=out_shape,
        compiler_params=_params(("parallel", "parallel")),
    )(a, b)


def _rowwise(name, fn, rows, consts, outs, accs=(), tile=256, ncol=1, nrows=None):
    n_r, n_c, n_o = len(rows), len(consts), len(outs)
    T = rows[0][0].shape[0] if nrows is None else nrows
    tile = _pick(T, tuple(t for t in (512, 256, 128, 64, 32, 16, 8) if t <= tile))
    nt = T // tile
    const_rows = [c[1] if isinstance(c, tuple) else None for c in consts]
    consts = [c[0] if isinstance(c, tuple) else c for c in consts]

    def body(*refs):
        vals = [r[...] for r in refs[:n_r]]
        vals += [r[...] if l is None else r[l:l + 1, :] for r, l in zip(refs[n_r:n_r + n_c], const_rows)]
        res = fn(*vals)
        if not isinstance(res, (tuple, list)):
            res = (res,)
        o_refs = refs[n_r + n_c:n_r + n_c + n_o]
        a_refs = refs[n_r + n_c + n_o:]
        for r, v in zip(o_refs, res[:n_o]):
            r[...] = v.astype(r.dtype)
        first = pl.program_id(1) == 0
        for r, v in zip(a_refs, res[n_o:]):
            @pl.when(first)
            def _(r=r, v=v):
                r[...] = v.astype(F32)

            @pl.when(jnp.logical_not(first))
            def _(r=r, v=v):
                r[...] += v.astype(F32)

    in_specs, args = [], []
    for arr, w, off, roff in rows:
        rb = roff // tile
        assert roff % tile == 0
        in_specs.append(pl.BlockSpec((tile, w), lambda j, i, off=off, rb=rb: (i + rb, off + j)))
        args.append(arr)
    for cst in consts:
        in_specs.append(pl.BlockSpec(cst.shape, lambda j, i: (0, 0)))
        args.append(cst)
    out_specs, out_shape = [], []
    for tw, dt in outs:
        out_specs.append(pl.BlockSpec((tile, tw // ncol), lambda j, i: (i, j)))
        out_shape.append(jax.ShapeDtypeStruct((T, tw), dt))
    for nr, tw in accs:
        out_specs.append(pl.BlockSpec((nr, tw // ncol), lambda j, i: (0, j)))
        out_shape.append(jax.ShapeDtypeStruct((nr, tw), F32))
    res = _tc_call(
        body, name=name, grid=(ncol, nt), in_specs=in_specs, out_specs=out_specs, out_shape=out_shape,
        compiler_params=_params(("parallel", "arbitrary")),
    )(*args)
    return res


def _rms_fwd(name, x, g):
    D = x.shape[1]
    return _rowwise(name, lambda xv, gv: _rms(xv, gv), [(x, D, 0, 0)], [g], [(D, BF16)])[0]


def _rms_bwd(name, x, g, dh, dx_in):
    D = x.shape[1]

    def fn(xv, dhv, dxv, gv):
        _, vjp = jax.vjp(_rms, xv, gv)
        dx, dg = vjp(dhv.astype(F32))
        tot = dx + dxv
        return tot, tot, jnp.sum(dg, axis=0, keepdims=True)

    return _rowwise(name, fn, [(x, D, 0, 0), (dh, D, 0, 0), (dx_in, D, 0, 0)], [g], [(D, F32), (D, BF16)], [(1, D)],
                    tile=256)


def _lag_views(win, K, R, forward):
    n = win.shape[0]
    for r in range(8):
        if r >= K:
            break
        if r == 0:
            rolled = win
        else:
            rolled = pltpu.roll(win, (n - r) if forward else r, axis=0)
        for q in range((K - 1 - r) // 8 + 1):
            s = 8 * q + r
            if forward:
                yield s, rolled[8 * q:8 * q + R]
            else:
                yield s, rolled[HALO - 8 * q:HALO - 8 * q + R]


def _conv_chunk(win, w_ref, K, R):
    acc = None
    for s, view in _lag_views(win, K, R, forward=False):
        term = w_ref[K - 1 - s:K - s, :] * view
        acc = term if acc is None else acc + term
    return acc


def _conv_chunk_t(win, w_ref, K, R):
    acc = None
    for s, view in _lag_views(win, K, R, forward=True):
        term = w_ref[K - 1 - s:K - s, :] * view
        acc = term if acc is None else acc + term
    return acc


def _conv_dw(xwin, dy, K, R):
    taps = [None] * K
    for s, view in _lag_views(xwin, K, R, forward=False):
        taps[K - 1 - s] = jnp.sum(dy * view, axis=0, keepdims=True)
    return taps


def _chunks(T):
    R = _pick(T, (128, 64, 32))
    return R, T // R


def _glu_conv_fwd(name, proj, w, cw_total):
    T = proj.shape[0]
    C = cw_total
    cw = LANE
    nb = C // cw
    R, nch = _chunks(T)

    def body(a_ref, b_ref, w_ref, o_ref, s_ref):
        s_ref[0:HALO, :] = jnp.zeros((HALO, cw), F32)

        def fill(i, _):
            r0 = pl.multiple_of(i * R, R)
            s_ref[pl.ds(HALO + r0, R), :] = a_ref[pl.ds(r0, R), :] * _sigmoid(b_ref[pl.ds(r0, R), :])
            return 0

        lax.fori_loop(0, nch, fill, 0)

        def conv(i, _):
            r0 = pl.multiple_of(i * R, R)
            o_ref[pl.ds(r0, R), :] = _conv_chunk(s_ref[pl.ds(r0, R + HALO), :], w_ref, CONV_K, R)
            return 0

        lax.fori_loop(0, nch, conv, 0)

    return _tc_call(
        body, name=name, grid=(nb,),
        in_specs=[pl.BlockSpec((T, cw), lambda j: (0, j)), pl.BlockSpec((T, cw), lambda j: (0, nb + j)),
                  pl.BlockSpec((HALO, cw), lambda j: (0, j))],
        out_specs=pl.BlockSpec((T, cw), lambda j: (0, j)),
        out_shape=jax.ShapeDtypeStruct((T, C), F32),
        scratch_shapes=[pltpu.VMEM((T + HALO, cw), F32)],
        compiler_params=_params(("parallel",)),
    )(proj, proj, w)


def _glu_conv_bwd(name, proj, w, dhc, cw_total):
    T = proj.shape[0]
    C = cw_total
    cw = LANE
    nb = C // cw
    R, nch = _chunks(T)

    def body(a_ref, b_ref, w_ref, dy_ref, da_ref, db_ref, dw_ref, s_ref, g_ref, acc_ref):
        s_ref[0:HALO, :] = jnp.zeros((HALO, cw), F32)
        g_ref[T:T + HALO, :] = jnp.zeros((HALO, cw), F32)
        acc_ref[...] = jnp.zeros_like(acc_ref)

        def fill(i, _):
            r0 = pl.multiple_of(i * R, R)
            s_ref[pl.ds(HALO + r0, R), :] = a_ref[pl.ds(r0, R), :] * _sigmoid(b_ref[pl.ds(r0, R), :])
            g_ref[pl.ds(r0, R), :] = dy_ref[pl.ds(r0, R), :]
            return 0

        lax.fori_loop(0, nch, fill, 0)

        def back(i, _):
            r0 = pl.multiple_of(i * R, R)
            dhg = _conv_chunk_t(g_ref[pl.ds(r0, R + HALO), :], w_ref, CONV_K, R)
            av = a_ref[pl.ds(r0, R), :]
            sg = _sigmoid(b_ref[pl.ds(r0, R), :])
            da_ref[pl.ds(r0, R), :] = (dhg * sg).astype(da_ref.dtype)
            db_ref[pl.ds(r0, R), :] = (dhg * av * sg * (1.0 - sg)).astype(db_ref.dtype)
            taps = _conv_dw(s_ref[pl.ds(r0, R + HALO), :], dy_ref[pl.ds(r0, R), :], CONV_K, R)
            for k, tap in enumerate(taps):
                acc_ref[k:k + 1, :] += tap
            return 0

        lax.fori_loop(0, nch, back, 0)
        dw_ref[...] = acc_ref[...]

    return _tc_call(
        body, name=name, grid=(nb,),
        in_specs=[pl.BlockSpec((T, cw), lambda j: (0, j)), pl.BlockSpec((T, cw), lambda j: (0, nb + j)),
                  pl.BlockSpec((HALO, cw), lambda j: (0, j)), pl.BlockSpec((T, cw), lambda j: (0, j))],
        out_specs=[pl.BlockSpec((T, cw), lambda j: (0, j)), pl.BlockSpec((T, cw), lambda j: (0, j)),
                   pl.BlockSpec((HALO, cw), lambda j: (0, j))],
        out_shape=[jax.ShapeDtypeStruct((T, C), BF16), jax.ShapeDtypeStruct((T, C), BF16),
                   jax.ShapeDtypeStruct((HALO, C), F32)],
        scratch_shapes=[pltpu.VMEM((T + HALO, cw), F32), pltpu.VMEM((T + HALO, cw), F32),
                        pltpu.VMEM((HALO, cw), F32)],
        compiler_params=_params(("parallel",)),
    )(proj, proj, w, dhc)


def _ffn_conv_fwd(name, up, w, dff):
    T = up.shape[0]
    cw = LANE
    nb = dff // cw
    R, nch = _chunks(T)

    def body(g_ref, v_ref, wg_ref, wv_ref, o_ref, sg_ref, sv_ref):
        sg_ref[0:HALO, :] = jnp.zeros((HALO, cw), F32)
        sv_ref[0:HALO, :] = jnp.zeros((HALO, cw), F32)

        def fill(i, _):
            r0 = pl.multiple_of(i * R, R)
            sg_ref[pl.ds(HALO + r0, R), :] = g_ref[pl.ds(r0, R), :]
            sv_ref[pl.ds(HALO + r0, R), :] = v_ref[pl.ds(r0, R), :]
            return 0

        lax.fori_loop(0, nch, fill, 0)

        def conv(i, _):
            r0 = pl.multiple_of(i * R, R)
            gc = _conv_chunk(sg_ref[pl.ds(r0, R + HALO), :], wg_ref, FFN_K, R)
            vc = _conv_chunk(sv_ref[pl.ds(r0, R + HALO), :], wv_ref, FFN_K, R)
            o_ref[pl.ds(r0, R), :] = (_silu(gc) * vc).astype(o_ref.dtype)
            return 0

        lax.fori_loop(0, nch, conv, 0)

    return _tc_call(
        body, name=name, grid=(nb,),
        in_specs=[pl.BlockSpec((T, cw), lambda j: (0, j)), pl.BlockSpec((T, cw), lambda j: (0, nb + j)),
                  pl.BlockSpec((8, cw), lambda j: (0, j)), pl.BlockSpec((8, cw), lambda j: (0, nb + j))],
        out_specs=pl.BlockSpec((T, cw), lambda j: (0, j)),
        out_shape=jax.ShapeDtypeStruct((T, dff), BF16),
        scratch_shapes=[pltpu.VMEM((T + HALO, cw), F32), pltpu.VMEM((T + HALO, cw), F32)],
        compiler_params=_params(("parallel",)),
    )(up, up, w, w)


def _ffn_conv_bwd(name, up, w, dact, dff):
    T = up.shape[0]
    cw = LANE
    nb = dff // cw
    R, nch = _chunks(T)

    def body(g_ref, v_ref, wg_ref, wv_ref, da_ref, dg_ref, dv_ref, dwg_ref, dwv_ref,
             sg_ref, sv_ref, tg_ref, tv_ref, ag_ref, av_ref):
        zero = jnp.zeros((HALO, cw), F32)
        sg_ref[0:HALO, :] = zero
        sv_ref[0:HALO, :] = zero
        tg_ref[T:T + HALO, :] = zero
        tv_ref[T:T + HALO, :] = zero
        ag_ref[...] = jnp.zeros_like(ag_ref)
        av_ref[...] = jnp.zeros_like(av_ref)

        def fill(i, _):
            r0 = pl.multiple_of(i * R, R)
            sg_ref[pl.ds(HALO + r0, R), :] = g_ref[pl.ds(r0, R), :]
            sv_ref[pl.ds(HALO + r0, R), :] = v_ref[pl.ds(r0, R), :]
            return 0

        lax.fori_loop(0, nch, fill, 0)

        def grads(i, _):
            r0 = pl.multiple_of(i * R, R)
            gwin = sg_ref[pl.ds(r0, R + HALO), :]
            vwin = sv_ref[pl.ds(r0, R + HALO), :]
            gc = _conv_chunk(gwin, wg_ref, FFN_K, R)
            vc = _conv_chunk(vwin, wv_ref, FFN_K, R)
            da = da_ref[pl.ds(r0, R), :].astype(F32)
            sg = _sigmoid(gc)
            dgc = da * vc * (sg * (1.0 + gc * (1.0 - sg)))
            dvc = da * (gc * sg)
            tg_ref[pl.ds(r0, R), :] = dgc
            tv_ref[pl.ds(r0, R), :] = dvc
            for k, tap in enumerate(_conv_dw(gwin, dgc, FFN_K, R)):
                ag_ref[k:k + 1, :] += tap
            for k, tap in enumerate(_conv_dw(vwin, dvc, FFN_K, R)):
                av_ref[k:k + 1, :] += tap
            return 0

        lax.fori_loop(0, nch, grads, 0)

        def back(i, _):
            r0 = pl.multiple_of(i * R, R)
            dg_ref[pl.ds(r0, R), :] = _conv_chunk_t(tg_ref[pl.ds(r0, R + HALO), :], wg_ref, FFN_K, R).astype(dg_ref.dtype)
            dv_ref[pl.ds(r0, R), :] = _conv_chunk_t(tv_ref[pl.ds(r0, R + HALO), :], wv_ref, FFN_K, R).astype(dv_ref.dtype)
            return 0

        lax.fori_loop(0, nch, back, 0)
        dwg_ref[...] = ag_ref[...]
        dwv_ref[...] = av_ref[...]

    col = lambda j: (0, j)
    dg, dv, dwg, dwv = _tc_call(
        body, name=name, grid=(nb,),
        in_specs=[pl.BlockSpec((T, cw), col), pl.BlockSpec((T, cw), lambda j: (0, nb + j)),
                  pl.BlockSpec((8, cw), col), pl.BlockSpec((8, cw), lambda j: (0, nb + j)),
                  pl.BlockSpec((T, cw), col)],
        out_specs=[pl.BlockSpec((T, cw), col), pl.BlockSpec((T, cw), col),
                   pl.BlockSpec((8, cw), col), pl.BlockSpec((8, cw), col)],
        out_shape=[jax.ShapeDtypeStruct((T, dff), BF16), jax.ShapeDtypeStruct((T, dff), BF16),
                   jax.ShapeDtypeStruct((8, dff), F32), jax.ShapeDtypeStruct((8, dff), F32)],
        scratch_shapes=[pltpu.VMEM((T + HALO, cw), F32), pltpu.VMEM((T + HALO, cw), F32),
                        pltpu.VMEM((T + HALO, cw), F32), pltpu.VMEM((T + HALO, cw), F32),
                        pltpu.VMEM((8, cw), F32), pltpu.VMEM((8, cw), F32)],
        compiler_params=_params(("parallel",)),
    )(up, up, w, w, dact)
    return jnp.concatenate([dg, dv], axis=1), jnp.concatenate([dwg, dwv], axis=1)


def _zoh(a_re, a_im, log_dt):
    ar = jnp.minimum(a_re, -1e-4)
    ai = a_im
    dt = jnp.exp(log_dt)
    mag = jnp.exp(dt * ar)
    abar_re = mag * jnp.cos(dt * ai)
    abar_im = mag * jnp.sin(dt * ai)
    den = ar * ar + ai * ai
    nr = abar_re - 1.0
    ni = abar_im
    return abar_re, abar_im, (nr * ar + ni * ai) / den, (ni * ar - nr * ai) / den


def _discretize(a_re, a_im, log_dt, a_re_h, a_im_h, log_dt_h, b_re, b_im):
    abar_re, abar_im, _, _ = _zoh(a_re, a_im, log_dt)
    _, _, z_re, z_im = _zoh(a_re_h, a_im_h, log_dt_h)
    return abar_re, abar_im, z_re * b_re - z_im * b_im, z_re * b_im + z_im * b_re


def _full_specs(arrs):
    return [pl.BlockSpec(a.shape, lambda *_, n=len(a.shape): (0,) * n) for a in arrs]


def _ssm_prep(name, raw):
    def body(*refs):
        res = _discretize(*[r[...] for r in refs[:8]])
        for r, v in zip(refs[8:], res):
            r[...] = v

    outs = [jax.ShapeDtypeStruct(raw[0].shape, F32)] * 2 + [jax.ShapeDtypeStruct(raw[6].shape, F32)] * 2
    return _tc_call(body, name=name, in_specs=_full_specs(raw), out_specs=_full_specs(outs), out_shape=outs)(*raw)


def _ssm_prep_bwd(name, raw, cots):
    G = raw[0].shape[0]
    H = raw[3].shape[0] // G

    def body(*refs):
        _, vjp = jax.vjp(_discretize, *[r[...] for r in refs[:8]])
        g = vjp(tuple(r[...] for r in refs[8:12]))
        outs = refs[12:]
        for k in range(3):
            rep = g[3 + k]
            outs[k][...] = g[k] + jnp.sum(rep.reshape(G, H, rep.shape[1]), axis=1)
        outs[3][...] = g[6]
        outs[4][...] = g[7]

    outs = [jax.ShapeDtypeStruct(a.shape, F32) for a in (raw[0], raw[1], raw[2], raw[6], raw[7])]
    return _tc_call(body, name=name, in_specs=_full_specs(list(raw) + list(cots)), out_specs=_full_specs(outs),
                          out_shape=outs)(*raw, *cots)


def _cmul(ar, ai, br, bi):
    return ar * br - ai * bi, ar * bi + ai * br


def _scan_coefs(ar, ai, reverse):
    W = ar.shape[1]
    row = lax.broadcasted_iota(jnp.int32, (8, W), 0)
    p = [None] * 9
    p[1] = (ar, ai)
    for n in range(2, 9):
        p[n] = _cmul(*p[n // 2], *p[n - n // 2])
    steps = []
    for s in (1, 2, 4):
        valid = (row <= 7 - s) if reverse else (row >= s)
        steps.append((jnp.where(valid, p[s][0], 0.0), jnp.where(valid, p[s][1], 0.0)))
    pr = jnp.zeros((8, W), F32)
    pi = jnp.zeros((8, W), F32)
    for i in range(8):
        n = (8 - i) if reverse else (i + 1)
        pr = jnp.where(row == i, p[n][0], pr)
        pi = jnp.where(row == i, p[n][1], pi)
    return steps, (pr, pi)


def _scan_tile(xr, xi, cr, ci, coefs, reverse):
    steps, (pr, pi) = coefs
    for s, (sr, si) in zip((1, 2, 4), steps):
        shift = (8 - s) if reverse else s
        rr = pltpu.roll(xr, shift, axis=0)
        ri = pltpu.roll(xi, shift, axis=0)
        xr, xi = xr + sr * rr - si * ri, xi + sr * ri + si * rr
    xr, xi = xr + pr * cr - pi * ci, xi + pr * ci + pi * cr
    return xr, xi


def _edge_rows(x, reverse):
    W = x.shape[1]
    return jnp.broadcast_to(x[0:1, :] if reverse else x[7:8, :], (8, W))


def _ssm_chunk(T):
    return _pick(T, (256, 128, 64))


def _ssm_fwd(name, proj, u_off, p, width):
    T = proj.shape[0]
    NB = width // LANE
    Q = _ssm_chunk(T)
    nch = T // Q
    W = SSM_LANES

    def body(u_ref, bre, bim, cre, cim, ar_ref, ai_ref, d_ref, y_ref, ckr_ref, cki_ref, br_s, bi_s, car_r, car_i):
        c = pl.program_id(1)

        @pl.when(c == 0)
        def _():
            car_r[...] = jnp.zeros_like(car_r)
            car_i[...] = jnp.zeros_like(car_i)

        ckr_ref[...] = car_r[...]
        cki_ref[...] = car_i[...]
        u = u_ref[...]
        u16 = u.astype(BF16)
        br_s[...] = jnp.dot(u16, bre[...], preferred_element_type=F32)
        bi_s[...] = jnp.dot(u16, bim[...], preferred_element_type=F32)
        coefs = _scan_coefs(ar_ref[...], ai_ref[...], False)

        def tile(j, carry):
            r0 = pl.multiple_of(j * 8, 8)
            xr, xi = _scan_tile(br_s[pl.ds(r0, 8), :], bi_s[pl.ds(r0, 8), :], carry[0], carry[1], coefs, False)
            br_s[pl.ds(r0, 8), :] = xr
            bi_s[pl.ds(r0, 8), :] = xi
            return _edge_rows(xr, False), _edge_rows(xi, False)

        cr, ci = lax.fori_loop(0, Q // 8, tile, (car_r[...], car_i[...]))
        car_r[...] = cr
        car_i[...] = ci
        nt = (((1,), (1,)), ((), ()))
        y = (lax.dot_general(br_s[...].astype(BF16), cre[...], nt, preferred_element_type=F32)
             - lax.dot_general(bi_s[...].astype(BF16), cim[...], nt, preferred_element_type=F32)
             + d_ref[...] * u)
        y_ref[...] = _gelu(y).astype(y_ref.dtype)

    blk = lambda b, c: (b, 0, 0)
    mat = pl.BlockSpec((None, LANE, W), blk)
    vec = pl.BlockSpec((None, 1, W), blk)
    ck = pl.BlockSpec((None, None, 8, W), lambda b, c: (b, c, 0, 0))
    return _tc_call(
        body, name=name, grid=(NB, nch),
        in_specs=[pl.BlockSpec((Q, LANE), lambda b, c: (c, u_off + b)), mat, mat, mat, mat, vec, vec,
                  pl.BlockSpec((1, LANE), lambda b, c: (0, b))],
        out_specs=[pl.BlockSpec((Q, LANE), lambda b, c: (c, b)), ck, ck],
        out_shape=[jax.ShapeDtypeStruct((T, width), BF16), jax.ShapeDtypeStruct((NB, nch, 8, W), F32),
                   jax.ShapeDtypeStruct((NB, nch, 8, W), F32)],
        scratch_shapes=[pltpu.VMEM((Q, W), F32), pltpu.VMEM((Q, W), F32), pltpu.VMEM((8, W), F32),
                        pltpu.VMEM((8, W), F32)],
        compiler_params=_params(("parallel", "arbitrary")),
    )(proj, p["bre"], p["bim"], p["cre"], p["cim"], p["ar"], p["ai"], p["d"])


def _ssm_bwd(name, proj, u_off, p, ck_r, ck_i, dyg, width):
    T = proj.shape[0]
    NB = width // LANE
    Q = _ssm_chunk(T)
    nch = T // Q
    W = SSM_LANES
    nt_dims = (((1,), (1,)), ((), ()))
    tn_dims = (((0,), (0,)), ((), ()))

    def body(u_ref, dy_ref, ckr_ref, cki_ref, bre, bim, cre, cim, ar_ref, ai_ref, d_ref,
             du_ref, dbr_ref, dbi_ref, dcr_ref, dci_ref, dar_ref, dai_ref, dd_ref,
             xr_s, xi_s, lr_s, li_s, lam_r, lam_i):
        c = pl.program_id(1)

        @pl.when(c == 0)
        def _():
            lam_r[...] = jnp.zeros_like(lam_r)
            lam_i[...] = jnp.zeros_like(lam_i)
            for r in (dbr_ref, dbi_ref, dcr_ref, dci_ref, dar_ref, dai_ref, dd_ref):
                r[...] = jnp.zeros_like(r)

        u = u_ref[...]
        u16 = u.astype(BF16)
        ar, ai = ar_ref[...], ai_ref[...]
        xr_s[0:8, :] = ckr_ref[...]
        xi_s[0:8, :] = cki_ref[...]
        xr_s[8:Q + 8, :] = jnp.dot(u16, bre[...], preferred_element_type=F32)
        xi_s[8:Q + 8, :] = jnp.dot(u16, bim[...], preferred_element_type=F32)
        fcoefs = _scan_coefs(ar, ai, False)

        def ftile(j, carry):
            r0 = pl.multiple_of(j * 8 + 8, 8)
            xr, xi = _scan_tile(xr_s[pl.ds(r0, 8), :], xi_s[pl.ds(r0, 8), :], carry[0], carry[1], fcoefs, False)
            xr_s[pl.ds(r0, 8), :] = xr
            xi_s[pl.ds(r0, 8), :] = xi
            return _edge_rows(xr, False), _edge_rows(xi, False)

        lax.fori_loop(0, Q // 8, ftile, (ckr_ref[...], cki_ref[...]))
        xr16 = xr_s[8:Q + 8, :].astype(BF16)
        xi16 = xi_s[8:Q + 8, :].astype(BF16)
        y = (lax.dot_general(xr16, cre[...], nt_dims, preferred_element_type=F32)
             - lax.dot_general(xi16, cim[...], nt_dims, preferred_element_type=F32) + d_ref[...] * u)
        _, gelu_vjp = jax.vjp(_gelu, y)
        dy = gelu_vjp(dy_ref[...].astype(F32))[0]
        dy16 = dy.astype(BF16)
        dd_ref[...] += jnp.broadcast_to(jnp.sum(dy * u, axis=0, keepdims=True), (8, LANE))
        dcr_ref[...] += lax.dot_general(dy16, xr16, tn_dims, preferred_element_type=F32)
        dci_ref[...] -= lax.dot_general(dy16, xi16, tn_dims, preferred_element_type=F32)
        lr_s[...] = jnp.dot(dy16, cre[...], preferred_element_type=F32)
        li_s[...] = -jnp.dot(dy16, cim[...], preferred_element_type=F32)
        rcoefs = _scan_coefs(ar, -ai, True)
        row = lax.broadcasted_iota(jnp.int32, (8, W), 0)

        def rtile(jj, carry):
            j = Q // 8 - 1 - jj
            r0 = pl.multiple_of(j * 8, 8)
            lr, li = _scan_tile(lr_s[pl.ds(r0, 8), :], li_s[pl.ds(r0, 8), :], carry[0], carry[1], rcoefs, True)
            lr_s[pl.ds(r0, 8), :] = lr
            li_s[pl.ds(r0, 8), :] = li
            cur_r, cur_i = xr_s[pl.ds(r0 + 8, 8), :], xi_s[pl.ds(r0 + 8, 8), :]
            prv_r, prv_i = xr_s[pl.ds(r0, 8), :], xi_s[pl.ds(r0, 8), :]
            xpr = jnp.where(row == 0, _edge_rows(prv_r, False), pltpu.roll(cur_r, 1, axis=0))
            xpi = jnp.where(row == 0, _edge_rows(prv_i, False), pltpu.roll(cur_i, 1, axis=0))
            return (_edge_rows(lr, True), _edge_rows(li, True),
                    carry[2] + lr * xpr + li * xpi, carry[3] + li * xpr - lr * xpi)

        zero = jnp.zeros((8, W), F32)
        cr, ci, sar, sai = lax.fori_loop(0, Q // 8, rtile, (lam_r[...], lam_i[...], zero, zero))
        lam_r[...] = cr
        lam_i[...] = ci
        dar_ref[...] += jnp.broadcast_to(jnp.sum(sar, axis=0, keepdims=True), (8, W))
        dai_ref[...] += jnp.broadcast_to(jnp.sum(sai, axis=0, keepdims=True), (8, W))
        lr16 = lr_s[...].astype(BF16)
        li16 = li_s[...].astype(BF16)
        dbr_ref[...] += lax.dot_general(u16, lr16, tn_dims, preferred_element_type=F32)
        dbi_ref[...] += lax.dot_general(u16, li16, tn_dims, preferred_element_type=F32)
        du = (lax.dot_general(lr16, bre[...], nt_dims, preferred_element_type=F32)
              + lax.dot_general(li16, bim[...], nt_dims, preferred_element_type=F32) + d_ref[...] * dy)
        du_ref[...] = du.astype(du_ref.dtype)

    blk = lambda b, c: (b, 0, 0)
    mat = pl.BlockSpec((None, LANE, W), blk)
    vec = pl.BlockSpec((None, 1, W), blk)
    acc8 = pl.BlockSpec((None, 8, W), blk)
    ck = pl.BlockSpec((None, None, 8, W), lambda b, c: (b, nch - 1 - c, 0, 0))
    return _tc_call(
        body, name=name, grid=(NB, nch),
        in_specs=[pl.BlockSpec((Q, LANE), lambda b, c: (nch - 1 - c, u_off + b)),
                  pl.BlockSpec((Q, LANE), lambda b, c: (nch - 1 - c, b)), ck, ck, mat, mat, mat, mat, vec, vec,
                  pl.BlockSpec((1, LANE), lambda b, c: (0, b))],
        out_specs=[pl.BlockSpec((Q, LANE), lambda b, c: (nch - 1 - c, b)), mat, mat, mat, mat, acc8, acc8,
                   pl.BlockSpec((None, 8, LANE), blk)],
        out_shape=[jax.ShapeDtypeStruct((T, width), BF16)] + [jax.ShapeDtypeStruct((NB, LANE, W), F32)] * 4
                  + [jax.ShapeDtypeStruct((NB, 8, W), F32)] * 2 + [jax.ShapeDtypeStruct((NB, 8, LANE), F32)],
        scratch_shapes=[pltpu.VMEM((Q + 8, W), F32), pltpu.VMEM((Q + 8, W), F32), pltpu.VMEM((Q, W), F32),
                        pltpu.VMEM((Q, W), F32), pltpu.VMEM((8, W), F32), pltpu.VMEM((8, W), F32)],
        compiler_params=_params(("parallel", "arbitrary")),
    )(proj, dyg, ck_r, ck_i, p["bre"], p["bim"], p["cre"], p["cim"], p["ar"], p["ai"], p["d"])


def _block_diag(w):
    G, H, P = w.shape
    eye = jnp.eye(8, dtype=w.dtype)
    return (w.reshape(G // 8, 8, H, 1, P) * eye[None, :, None, :, None]).reshape(G // 8, 8 * H, 8 * P)


def _block_diag_t(d, H, P):
    NB = d.shape[0]
    d = d.reshape(NB, 8, H, 8, P)
    eye = jnp.eye(8, dtype=d.dtype)
    return jnp.sum(d * eye[None, :, None, :, None], axis=3).reshape(NB * 8, H, P)


def _attn_fwd(name, q, kv, heads):
    T, D = q.shape
    Mm = kv.shape[0]
    hd = D // heads
    tq = _pick(T, (512, 256, 128))
    scale = hd ** -0.5

    def body(q_ref, k_ref, v_ref, o_ref):
        s = lax.dot_general(q_ref[...], k_ref[...], (((1,), (1,)), ((), ())), preferred_element_type=F32) * scale
        s = s - jnp.max(s, axis=-1, keepdims=True)
        e = jnp.exp(s)
        p = e / jnp.sum(e, axis=-1, keepdims=True)
        o_ref[...] = jnp.dot(p.astype(BF16), v_ref[...], preferred_element_type=F32).astype(o_ref.dtype)

    return _tc_call(
        body, name=name, grid=(heads, T // tq),
        in_specs=[pl.BlockSpec((tq, hd), lambda h, i: (i, h)), pl.BlockSpec((Mm, hd), lambda h, i: (0, h)),
                  pl.BlockSpec((Mm, hd), lambda h, i: (0, heads + h))],
        out_specs=pl.BlockSpec((tq, hd), lambda h, i: (i, h)),
        out_shape=jax.ShapeDtypeStruct((T, D), BF16),
        compiler_params=_params(("parallel", "parallel")),
    )(q, kv, kv)


def _attn_bwd(name, q, kv, do, heads):
    T, D = q.shape
    Mm = kv.shape[0]
    hd = D // heads
    tq = _pick(T, (512, 256, 128))
    scale = hd ** -0.5
    nt_dims = (((1,), (1,)), ((), ()))
    tn_dims = (((0,), (0,)), ((), ()))

    def body(q_ref, k_ref, v_ref, do_ref, dq_ref, dk_ref, dv_ref):
        i = pl.program_id(1)

        @pl.when(i == 0)
        def _():
            dk_ref[...] = jnp.zeros_like(dk_ref)
            dv_ref[...] = jnp.zeros_like(dv_ref)

        qv, kvl, vv, dov = q_ref[...], k_ref[...], v_ref[...], do_ref[...]
        s = lax.dot_general(qv, kvl, nt_dims, preferred_element_type=F32) * scale
        s = s - jnp.max(s, axis=-1, keepdims=True)
        e = jnp.exp(s)
        p = e / jnp.sum(e, axis=-1, keepdims=True)
        p16 = p.astype(BF16)
        dv_ref[...] += lax.dot_general(p16, dov, tn_dims, preferred_element_type=F32)
        dp = lax.dot_general(dov, vv, nt_dims, preferred_element_type=F32)
        ds = (p * (dp - jnp.sum(dp * p, axis=-1, keepdims=True)) * scale).astype(BF16)
        dq_ref[...] = jnp.dot(ds, kvl, preferred_element_type=F32).astype(dq_ref.dtype)
        dk_ref[...] += lax.dot_general(ds, qv, tn_dims, preferred_element_type=F32)

    return _tc_call(
        body, name=name, grid=(heads, T // tq),
        in_specs=[pl.BlockSpec((tq, hd), lambda h, i: (i, h)), pl.BlockSpec((Mm, hd), lambda h, i: (0, h)),
                  pl.BlockSpec((Mm, hd), lambda h, i: (0, heads + h)), pl.BlockSpec((tq, hd), lambda h, i: (i, h))],
        out_specs=[pl.BlockSpec((tq, hd), lambda h, i: (i, h)), pl.BlockSpec((Mm, hd), lambda h, i: (0, h)),
                   pl.BlockSpec((Mm, hd), lambda h, i: (0, h))],
        out_shape=[jax.ShapeDtypeStruct((T, D), BF16), jax.ShapeDtypeStruct((Mm, D), F32),
                   jax.ShapeDtypeStruct((Mm, D), F32)],
        compiler_params=_params(("parallel", "arbitrary")),
    )(q, kv, kv, do)


def _position():
    return lax.axis_index("x"), lax.axis_index("y"), lax.axis_index("c")


def _allgather(name, blk, row_mode):
    L = blk.shape[0]
    out_shape = (L, N_DEV) + blk.shape[1:] if row_mode else (N_DEV,) + blk.shape
    x_ref = jax.new_ref(blk, memory_space=pltpu.MemorySpace.HBM)
    out_ref = jax.empty_ref(jax.ShapeDtypeStruct(out_shape, blk.dtype), memory_space=pltpu.MemorySpace.HBM)

    def body(send_sems, recv_sems, local_sem):
        x, y, c = _position()
        me, sibling = (x, y, c), (x, y, 1 - c)
        chips = [(1 - x, y), (x, 1 - y), (1 - x, 1 - y)]
        barrier = pltpu.get_barrier_semaphore()
        for peer in [sibling] + [(*chip, c) for chip in chips]:
            pl.semaphore_signal(barrier, inc=1, device_id=peer, device_id_type=MESH_ID)
        pl.semaphore_wait(barrier, 4)

        def slot(px, py, pc):
            b = 4 * px + 2 * py + pc
            return out_ref.at[:, b] if row_mode else out_ref.at[b]

        def copy(k, block, to, src=None):
            return pltpu.make_async_remote_copy(
                src_ref=slot(*block) if src is None else src, dst_ref=slot(*block),
                send_sem=send_sems.at[k], recv_sem=recv_sems.at[k], device_id=to, device_id_type=MESH_ID)

        mine = pltpu.make_async_copy(x_ref, slot(*me), local_sem)
        mine.start()
        first = [copy(0, me, sibling, src=x_ref)]
        first += [copy(1 + j, me, (*chip, c), src=x_ref) for j, chip in enumerate(chips)]
        for cp in first:
            cp.start()
        passed = [copy(4 + j, (*chip, c), sibling) for j, chip in enumerate(chips)]
        for j, chip in enumerate(chips):
            copy(1 + j, (*chip, c), me).wait_recv()
            passed[j].start()
        copy(0, sibling, me).wait_recv()
        for j, chip in enumerate(chips):
            copy(4 + j, (*chip, 1 - c), me).wait_recv()
        for cp in first + passed:
            cp.wait_send()
        mine.wait()

    pl.kernel(
        body, mesh=plsc.ScalarSubcoreMesh(axis_name="sequencer", num_cores=1), name=name,
        scratch_types=(pltpu.SemaphoreType.DMA((7,)), pltpu.SemaphoreType.DMA((7,)), pltpu.SemaphoreType.DMA),
        compiler_params=pltpu.CompilerParams(collective_id=AG_COLLECTIVE_ID),
    )()
    return out_ref[...]


def _sequencer_kernel(name, body, scratch_types, collective_id):
    pl.kernel(
        body, mesh=plsc.ScalarSubcoreMesh(axis_name="sequencer", num_cores=1), name=name,
        scratch_types=scratch_types, compiler_params=pltpu.CompilerParams(collective_id=collective_id),
    )()


def _handshake(peers):
    barrier = pltpu.get_barrier_semaphore()
    for peer in peers:
        pl.semaphore_signal(barrier, inc=1, device_id=peer, device_id_type=MESH_ID)
    pl.semaphore_wait(barrier, len(peers))


def _rs_sibling(name, grads):
    hbm = pltpu.MemorySpace.HBM
    g_ref = jax.new_ref(grads, memory_space=hbm)
    out_ref = jax.empty_ref(jax.ShapeDtypeStruct((4,) + grads.shape[1:], grads.dtype), memory_space=hbm)

    def body(send_sems, recv_sems):
        x, y, c = _position()
        sibling = (x, y, 1 - c)
        _handshake([sibling])
        copies = [pltpu.make_async_remote_copy(
            src_ref=g_ref.at[2 * q + (1 - c)], dst_ref=out_ref.at[q], send_sem=send_sems.at[q],
            recv_sem=recv_sems.at[q], device_id=sibling, device_id_type=MESH_ID) for q in range(4)]
        for cp in copies:
            cp.start()
        for cp in copies:
            cp.wait_recv()
        for cp in copies:
            cp.wait_send()

    _sequencer_kernel(name, body, (pltpu.SemaphoreType.DMA((4,)), pltpu.SemaphoreType.DMA((4,))),
                      RS_SIBLING_COLLECTIVE_ID)
    return out_ref[...]


def _rs_chips(name, part):
    hbm = pltpu.MemorySpace.HBM
    p_ref = jax.new_ref(part, memory_space=hbm)
    out_ref = jax.empty_ref(jax.ShapeDtypeStruct(part.shape, part.dtype), memory_space=hbm)

    def body(send_sems, recv_sems, local_sem):
        x, y, c = _position()
        flips = [(1 - x, y), (x, 1 - y), (1 - x, 1 - y)]
        _handshake([(fx, fy, c) for fx, fy in flips])
        mine = pltpu.make_async_copy(p_ref.at[2 * x + y], out_ref.at[3], local_sem)
        copies = [pltpu.make_async_remote_copy(
            src_ref=p_ref.at[2 * fx + fy], dst_ref=out_ref.at[k], send_sem=send_sems.at[k], recv_sem=recv_sems.at[k],
            device_id=(fx, fy, c), device_id_type=MESH_ID) for k, (fx, fy) in enumerate(flips)]
        for cp in copies:
            cp.start()
        mine.start()
        for cp in copies:
            cp.wait_recv()
        for cp in copies:
            cp.wait_send()
        mine.wait()

    _sequencer_kernel(name, body, (pltpu.SemaphoreType.DMA((3,)), pltpu.SemaphoreType.DMA((3,)),
                                   pltpu.SemaphoreType.DMA), RS_CHIPS_COLLECTIVE_ID)
    return out_ref[...]


def _adamw_math(g, w, m, v):
    m = ADAM_B1 * m + (1.0 - ADAM_B1) * g
    v = ADAM_B2 * v + (1.0 - ADAM_B2) * (g * g)
    m_hat = m / (1.0 - ADAM_B1 ** ADAM_STEP)
    v_hat = v / (1.0 - ADAM_B2 ** ADAM_STEP)
    delta = -ADAM_LR * (m_hat / (jnp.sqrt(v_hat) + ADAM_EPS) + ADAM_WD * w)
    return delta, m, v


def _adamw_natural(name, gs, ws, ms, vs, lead_block=None):
    n = len(gs)

    def body(*refs):
        ins, outs = refs[:4 * n], refs[4 * n:]
        for k in range(n):
            delta, nm, nv = _adamw_math(*[ins[j * n + k][...] for j in range(4)])
            outs[k][...] = delta
            outs[n + k][...] = nm
            outs[2 * n + k][...] = nv

    out_shape = [jax.ShapeDtypeStruct(w.shape, F32) for w in ws] * 3
    if lead_block is None:
        grid = ()
        spec = lambda a: pl.BlockSpec(a.shape, lambda nd=len(a.shape): (0,) * nd)
    else:
        grid = tuple(d // b for d, b in zip(ws[0].shape[:2], lead_block))
        spec = lambda a: pl.BlockSpec(tuple(lead_block) + tuple(a.shape[2:]),
                                      lambda i, j, nd=len(a.shape): (i, j) + (0,) * (nd - 2))
    return _tc_call(
        body, name=name, grid=grid, in_specs=[spec(a) for a in list(gs) + list(ws) + list(ms) + list(vs)],
        out_specs=[spec(o) for o in out_shape], out_shape=out_shape,
        compiler_params=pltpu.CompilerParams(vmem_limit_bytes=VMEM_LIMIT),
    )(*gs, *ws, *ms, *vs)


def _ew_tile(R, C, nblocks):
    budget = (VMEM_LIMIT * 3) // 4
    return _pick(R, tuple(t for t in (1024, 512, 256, 128, 64, 32, 16, 8) if 8 * t * C * nblocks <= budget))


def _pair_sum(name, grads, landed, c_idx):
    _, r, c = grads.shape
    tile = _ew_tile(r, c, 3)

    def body(c_ref, g_ref, s_ref, o_ref):
        o_ref[...] = (g_ref[...].astype(F32) + s_ref[...].astype(F32)).astype(o_ref.dtype)

    return _tc_call(
        body, name=name,
        grid_spec=pltpu.PrefetchScalarGridSpec(
            num_scalar_prefetch=1, grid=(4, r // tile),
            in_specs=[pl.BlockSpec((None, tile, c), lambda q, i, c_ref: (2 * q + c_ref[0], i, 0)),
                      pl.BlockSpec((None, tile, c), lambda q, i, c_ref: (q, i, 0))],
            out_specs=pl.BlockSpec((None, tile, c), lambda q, i, c_ref: (q, i, 0))),
        out_shape=jax.ShapeDtypeStruct((4, r, c), BF16),
        compiler_params=_params(("parallel", "parallel")),
    )(c_idx, grads, landed)


def _adamw_layer(name, got, w, m, v, layer, prev):
    L, r, c = w.shape
    tile = _ew_tile(r, c, 11)

    def body(g0, g1, g2, g3, w_ref, m_ref, v_ref, *rest):
        outs = rest[-4:]
        g = (g0[...].astype(F32) + g1[...].astype(F32)) + (g2[...].astype(F32) + g3[...].astype(F32))
        delta, nm, nv = _adamw_math(g, w_ref[...], m_ref[...], v_ref[...])
        for ref, val in zip(outs, (g, delta, nm, nv)):
            ref[...] = val

    slab = pl.BlockSpec((None, tile, c), lambda i: (layer, i, 0))
    in_specs = [pl.BlockSpec((None, tile, c), lambda i, k=k: (k, i, 0)) for k in range(4)] + [slab] * 3
    args = [got, got, got, got, w, m, v]
    aliases = {}
    if prev is not None:
        in_specs += [ANY] * 4
        args += list(prev)
        aliases = {7 + k: k for k in range(4)}
    return _tc_call(
        body, name=name, grid=(r // tile,), in_specs=in_specs, out_specs=[slab] * 4,
        out_shape=[jax.ShapeDtypeStruct((L, r, c), F32)] * 4, input_output_aliases=aliases,
        compiler_params=_params(("parallel",)),
    )(*args)


def _reduce_pipeline(name, layer, grads, w, m, v, state):
    c_idx = lax.axis_index("c").astype(jnp.int32).reshape(1)
    landed = _rs_sibling(name + "_rs_sibling", grads)
    yield
    part = _pair_sum(name + "_pair_sum", grads, landed, c_idx)
    got = _rs_chips(name + "_rs_chips", part)
    yield
    yield
    yield
    state[name] = _adamw_layer(name + "_adamw", got, w, m, v, layer, state.get(name))


def _loss_head(name, x, g, target):
    T, D = x.shape

    def fn(xv, tv, gv):
        def f(xx, gg):
            err = _rms(xx, gg) - tv
            return 0.5 * jnp.sum(jnp.mean(err * err, axis=-1, keepdims=True))

        loss, (dx, dg) = jax.value_and_grad(f, argnums=(0, 1))(xv, gv)
        return dx, dx, jnp.full((8, LANE), loss, F32), jnp.sum(dg, axis=0, keepdims=True)

    dx, dx16, loss, dg = _rowwise(name, fn, [(x, D, 0, 0), (target, D, 0, 0)], [g], [(D, F32), (D, BF16)],
                                  [(8, LANE), (1, D)], tile=128)
    return loss[0, 0], dx, dx16, dg


_SHARDED_COL = ("w_in", "conv_dw_w", "conv_w_pw", "ssm_w_glu", "xa_w_kv", "ffn_w_up", "ffn_dw_w")
_SHARDED_ROW = ("w_out", "xa_w_q", "xa_w_o", "ffn_w_down")
_WEIGHTS = ['mix_norm_g', 'w_in', 'conv_dw_w', 'conv_dw_b', 'conv_ln_g', 'conv_ln_b', 'conv_w_pw', 'ssm_a_re',
            'ssm_a_im', 'ssm_log_dt', 'ssm_b_re', 'ssm_b_im', 'ssm_c_re', 'ssm_c_im', 'ssm_d', 'ssm_w_glu', 'w_out',
            'xa_norm_g', 'mem_norm_g', 'xa_w_q', 'xa_w_kv', 'xa_w_o', 'ffn_norm_g', 'ffn_w_up', 'ffn_dw_w',
            'ffn_w_down', 'final_norm_g']
_FWD = ['x', 'mem'] + _WEIGHTS
_AG_ORDER = ("w_in", "conv_dw_w", "conv_w_pw", "ssm_w_glu", "w_out", "xa_w_q", "xa_w_kv", "xa_w_o", "ffn_w_up",
             "ffn_dw_w", "ffn_w_down")


def _pad_rows(a, rows):
    return jnp.pad(a, ((0, 0), (0, rows - a.shape[1]), (0, 0)))


def _step(inp, target, mom_m, mom_v):
    _LAST_CALL.clear()
    x0 = inp["x"][0]
    mem = inp["mem"][0]
    T, D = x0.shape
    L = inp["w_in"].shape[0]
    CW = inp["conv_dw_b"].shape[1]
    SW = inp["ssm_d"].shape[1]
    DFF = inp["ffn_w_down"].shape[1] * N_DEV
    G = SW // SSM_GROUP
    NB = SW // LANE
    u_off = (2 * CW) // LANE
    gate_off = (2 * CW + SW) // 1024

    gathered = {n: [None] * L for n in _AG_ORDER}
    filter_rows = {"conv_dw_w": HALO, "ffn_dw_w": 8}
    for l in range(L):
        for n in _AG_ORDER:
            blk = inp[n][l:l + 1]
            blk = _pad_rows(blk, filter_rows[n]) if n in filter_rows else blk.astype(BF16)
            if n in _SHARDED_COL:
                gathered[n][l] = _allgather("ag_" + n, blk, False)
            else:
                full = _allgather("ag_" + n, blk, True)
                gathered[n][l] = full.reshape(1, N_DEV * full.shape[2], full.shape[3])

    def W(n, l):
        return _W(gathered[n][l], 0, n in _SHARDED_COL)

    def dw_filter(n, l):
        g = gathered[n][l]
        return jnp.transpose(g[:, 0], (1, 0, 2)).reshape(g.shape[2], N_DEV * g.shape[3])

    def row(n, l):
        return (inp[n], l)

    ssm_raw, ssm_p = [], []
    for l in range(L):
        a_re, a_im, ldt = inp["ssm_a_re"][l], inp["ssm_a_im"][l], inp["ssm_log_dt"][l][:, None]
        rep = lambda a: jnp.repeat(a, SSM_GROUP, axis=0)
        flat = lambda b: jnp.transpose(b, (0, 2, 1)).reshape(G * SSM_GROUP, SSM_STATE)
        raw = (a_re, a_im, ldt, rep(a_re), rep(a_im), rep(ldt), flat(inp["ssm_b_re"][l]), flat(inp["ssm_b_im"][l]))
        abar_re, abar_im, bbar_re, bbar_im = _ssm_prep("ssm_prep", raw)
        bbar_re = bbar_re.reshape(G, SSM_GROUP, SSM_STATE)
        bbar_im = bbar_im.reshape(G, SSM_GROUP, SSM_STATE)
        ssm_raw.append(raw)
        ssm_p.append(dict(
            bre=_block_diag(bbar_re).astype(BF16), bim=_block_diag(bbar_im).astype(BF16),
            cre=_block_diag(inp["ssm_c_re"][l]).astype(BF16), cim=_block_diag(inp["ssm_c_im"][l]).astype(BF16),
            ar=abar_re.reshape(NB, 1, SSM_LANES), ai=abar_im.reshape(NB, 1, SSM_LANES), d=inp["ssm_d"][l][None, :]))

    saved = []
    x = x0
    for l in range(L):
        s = {"x_in": x}
        s["h1"] = _rms_fwd("rms_mix", x, row("mix_norm_g", l))
        s["proj"] = _mm_nn("mm_w_in", s["h1"], W("w_in", l), F32)
        s["hc"] = _glu_conv_fwd("conv_fwd", s["proj"], dw_filter("conv_dw_w", l), CW)
        s["hs"] = _rowwise("conv_post", _convpost, [(s["hc"], CW, 0, 0)],
                           [row("conv_dw_b", l), row("conv_ln_g", l), row("conv_ln_b", l)], [(CW, BF16)])[0]
        s["ya"] = _mm_nn("mm_w_pw", s["hs"], W("conv_w_pw", l), F32)
        s["yg"], s["ck_r"], s["ck_i"] = _ssm_fwd("ssm_fwd", s["proj"], u_off, ssm_p[l], SW)
        s["gg"] = _mm_nn("mm_w_glu", s["yg"], W("ssm_w_glu", l), F32)
        nmix = D // 1024
        mix_rows = [(s["proj"], 1024, gate_off, 0), (s["proj"], 1024, gate_off + nmix, 0), (s["ya"], 1024, 0, 0),
                    (s["gg"], 1024, 0, 0), (s["gg"], 1024, nmix, 0)]
        s["mix_rows"] = mix_rows
        s["mix"] = _rowwise("mix_fwd", _mixf, mix_rows, [], [(D, BF16)], ncol=nmix)[0]
        x = _mm_nn("mm_w_out", s["mix"], W("w_out", l), F32, add=x)
        s["x1"] = x
        s["h2"] = _rms_fwd("rms_xa", x, row("xa_norm_g", l))
        s["q"] = _mm_nn("mm_w_q", s["h2"], W("xa_w_q", l), BF16)
        s["mn"] = _rms_fwd("rms_mem", mem, row("mem_norm_g", l))
        s["kv"] = _mm_nn("mm_w_kv", s["mn"], W("xa_w_kv", l), BF16)
        s["o"] = _attn_fwd("attn_fwd", s["q"], s["kv"], XA_HEADS)
        x = _mm_nn("mm_w_o", s["o"], W("xa_w_o", l), F32, add=x)
        s["x2"] = x
        s["h3"] = _rms_fwd("rms_ffn", x, row("ffn_norm_g", l))
        s["up"] = _mm_nn("mm_w_up", s["h3"], W("ffn_w_up", l), F32)
        s["act"] = _ffn_conv_fwd("ffn_conv_fwd", s["up"], dw_filter("ffn_dw_w", l), DFF)
        x = _mm_nn("mm_w_down", s["act"], W("ffn_w_down", l), F32, add=x)
        saved.append(s)

    loss_part, dx, dx16, d_final_g = _loss_head("loss_head", x, inp["final_norm_g"][None, :], target[0])

    big = _SHARDED_COL + _SHARDED_ROW
    small = {n: [None] * L for n in _WEIGHTS if n not in big and n != "final_norm_g"}
    pads = {"conv_dw_w": HALO, "ffn_dw_w": 8}
    shards = {n: tuple(_pad_rows(a, pads[n]) if n in pads else a for a in (inp[n], mom_m[n], mom_v[n])) for n in big}
    state, queue = {}, []
    names = [n for n in _WEIGHTS if n not in big]
    sizes = [inp[n].size for n in names]
    total = sum(sizes)
    pack_w = 8 * LANE
    rows_p = -(-total // (LANE * pack_w)) * LANE
    padn = rows_p * pack_w - total

    def pack(parts, fill):
        return jnp.concatenate([p.reshape(-1) for p in parts] + [jnp.full((padn,), fill, F32)]).reshape(rows_p, pack_w)

    def tick():
        for gen in list(queue):
            if next(gen, "done") == "done":
                queue.remove(gen)

    def emit(n, l, g):
        queue.append(_reduce_pipeline(n, l, g.astype(BF16), *shards[n], state))
        tick()

    for l in reversed(range(L)):
        s = saved[l]
        dact = _mm_nt("mm_w_down_t", dx16, W("ffn_w_down", l), BF16)
        emit("ffn_w_down", l, _mm_tn("mm_dw_down", s["act"], dx16).reshape(N_DEV, DFF // N_DEV, D))
        d_up, d_ffn_dw = _ffn_conv_bwd("ffn_conv_bwd", s["up"], dw_filter("ffn_dw_w", l), dact, DFF)
        emit("ffn_dw_w", l, jnp.transpose(d_ffn_dw.reshape(8, N_DEV, 2 * DFF // N_DEV), (1, 0, 2)))
        dh3 = _mm_nt("mm_w_up_t", d_up, W("ffn_w_up", l), BF16)
        emit("ffn_w_up", l, _mm_tn("mm_dw_up", s["h3"], d_up, nb=2 * DFF // N_DEV))
        dx, dx16, small["ffn_norm_g"][l] = _rms_bwd("rms_ffn_bwd", s["x2"], row("ffn_norm_g", l), dh3, dx)
        do = _mm_nt("mm_w_o_t", dx16, W("xa_w_o", l), BF16)
        emit("xa_w_o", l, _mm_tn("mm_dw_o", s["o"], dx16).reshape(N_DEV, D // N_DEV, D))
        dq, dk, dv = _attn_bwd("attn_bwd", s["q"], s["kv"], do, XA_HEADS)
        dkv = jnp.concatenate([dk, dv], axis=1).astype(BF16)
        dh2 = _mm_nt("mm_w_q_t", dq, W("xa_w_q", l), BF16)
        emit("xa_w_q", l, _mm_tn("mm_dw_q", s["h2"], dq).reshape(N_DEV, D // N_DEV, D))
        dmn = _mm_nt("mm_w_kv_t", dkv, W("xa_w_kv", l), BF16)
        emit("xa_w_kv", l, _mm_tn("mm_dw_kv", s["mn"], dkv, nb=2 * D // N_DEV))

        def mem_bwd(mv, dv_, gv):
            _, vjp = jax.vjp(_rms, mv, gv)
            return jnp.sum(vjp(dv_.astype(F32))[1], axis=0, keepdims=True)

        small["mem_norm_g"][l] = _rowwise("rms_mem_bwd", mem_bwd, [(mem, D, 0, 0), (dmn, D, 0, 0)],
                                          [row("mem_norm_g", l)], [], [(1, D)], tile=128)[0]
        dx, dx16, small["xa_norm_g"][l] = _rms_bwd("rms_xa_bwd", s["x1"], row("xa_norm_g", l), dh2, dx)
        dmix = _mm_nt("mm_w_out_t", dx16, W("w_out", l), BF16)
        emit("w_out", l, _mm_tn("mm_dw_out", s["mix"], dx16).reshape(N_DEV, D // N_DEV, D))

        def mix_bwd(gla, glb, ya, ga, gb, dm):
            _, vjp = jax.vjp(_mixf, gla, glb, ya, ga, gb)
            return vjp(dm.astype(F32))

        nmix = D // 1024
        dgla, dglb, dya, dga, dgb = _rowwise("mix_bwd", mix_bwd, s["mix_rows"] + [(dmix, 1024, 0, 0)], [],
                                             [(D, BF16)] * 5, ncol=nmix)
        dgg = jnp.concatenate([dga, dgb], axis=1)
        dyg = _mm_nt("mm_w_glu_t", dgg, W("ssm_w_glu", l), BF16)
        emit("ssm_w_glu", l, _mm_tn("mm_dw_glu", s["yg"], dgg, nb=2 * D // N_DEV))
        du, dbr, dbi, dcr, dci, dar, dai, dd = _ssm_bwd("ssm_bwd", s["proj"], u_off, ssm_p[l], s["ck_r"], s["ck_i"],
                                                        dyg, SW)
        cots = (dar[:, 0, :].reshape(G, SSM_STATE), dai[:, 0, :].reshape(G, SSM_STATE),
                _block_diag_t(dbr, SSM_GROUP, SSM_STATE).reshape(G * SSM_GROUP, SSM_STATE),
                _block_diag_t(dbi, SSM_GROUP, SSM_STATE).reshape(G * SSM_GROUP, SSM_STATE))
        g_are, g_aim, g_ldt, g_bre, g_bim = _ssm_prep_bwd("ssm_prep_bwd", ssm_raw[l], cots)
        small["ssm_a_re"][l], small["ssm_a_im"][l], small["ssm_log_dt"][l] = g_are, g_aim, g_ldt[:, 0]
        small["ssm_b_re"][l] = jnp.transpose(g_bre.reshape(G, SSM_GROUP, SSM_STATE), (0, 2, 1))
        small["ssm_b_im"][l] = jnp.transpose(g_bim.reshape(G, SSM_GROUP, SSM_STATE), (0, 2, 1))
        small["ssm_c_re"][l] = _block_diag_t(dcr, SSM_GROUP, SSM_STATE)
        small["ssm_c_im"][l] = _block_diag_t(dci, SSM_GROUP, SSM_STATE)
        small["ssm_d"][l] = dd[:, 0, :].reshape(SW)
        dhs = _mm_nt("mm_w_pw_t", dya, W("conv_w_pw", l), BF16)
        emit("conv_w_pw", l, _mm_tn("mm_dw_pw", s["hs"], dya, nb=D // N_DEV))

        def post_bwd(hc, dh, b, lg, lb):
            _, vjp = jax.vjp(_convpost, hc, b, lg, lb)
            dhc, db, dlg, dlb = vjp(dh.astype(F32))
            return dhc, jnp.sum(db, axis=0, keepdims=True), jnp.sum(dlg, axis=0, keepdims=True), \
                jnp.sum(dlb, axis=0, keepdims=True)

        dhc, small["conv_dw_b"][l], small["conv_ln_g"][l], small["conv_ln_b"][l] = _rowwise(
            "conv_post_bwd", post_bwd, [(s["hc"], CW, 0, 0), (dhs, CW, 0, 0)],
            [row("conv_dw_b", l), row("conv_ln_g", l), row("conv_ln_b", l)], [(CW, F32)], [(1, CW)] * 3)
        da, db, d_conv_dw = _glu_conv_bwd("conv_bwd", s["proj"], dw_filter("conv_dw_w", l), dhc, CW)
        emit("conv_dw_w", l, jnp.transpose(d_conv_dw.reshape(HALO, N_DEV, CW // N_DEV), (1, 0, 2)))
        dproj = jnp.concatenate([da, db, du, dgla, dglb], axis=1)
        dh1 = _mm_nt("mm_w_in_t", dproj, W("w_in", l), BF16)
        dx, dx16, small["mix_norm_g"][l] = _rms_bwd("rms_mix_bwd", s["x_in"], row("mix_norm_g", l), dh1, dx)
        if l == 0:
            flat_g = jnp.concatenate([(jnp.stack(small[n]) if n != "final_norm_g" else d_final_g).reshape(-1)
                                      for n in names])
            g_all = _allgather("ag_small_grads", pack([flat_g], 0.0)[None], False)
        emit("w_in", l, _mm_tn("mm_dw_in", s["h1"], dproj, nb=dproj.shape[1] // N_DEV))

    while queue:
        tick()
    results = {n: [o[:, :inp[n].shape[1], :] for o in state[n]] if n in pads else state[n] for n in big}
    def sum8(*parts):
        tot = parts[0]
        for part in parts[1:]:
            tot = tot + part
        return tot

    g_sum = _rowwise("small_grad_sum", sum8, [(g_all.reshape(N_DEV * rows_p, pack_w), pack_w, 0, k * rows_p)
                                              for k in range(N_DEV)], [], [(pack_w, F32)], tile=LANE, nrows=rows_p)[0]
    g_flat, grads, offs = g_sum.reshape(-1), {}, 0
    for n, sz in zip(names, sizes):
        grads[n] = g_flat[offs:offs + sz].reshape(inp[n].shape)
        offs += sz
    sparse = [n for n in names if inp[n].ndim == 4 and inp[n].shape[-1] < SSM_STATE]
    dense = [n for n in names if n not in sparse]
    as2d = lambda a: a[None, :] if a.ndim == 1 else a
    for group, block in ((dense, None), (sparse, (1, 16))):
        outs = _adamw_natural("small_adamw", *[[as2d(src[n]) for n in group] for src in (grads, inp, mom_m, mom_v)],
                              lead_block=block)
        for k, n in enumerate(group):
            results[n] = [grads[n]] + [outs[j * len(group) + k].reshape(inp[n].shape) for j in range(3)]

    _LAST_CALL.clear()
    loss = lax.psum(loss_part, ("x", "y", "c"))
    grad_x = dx[None]
    return (loss, grad_x, *[results[n][0] for n in _WEIGHTS], *[results[n][1] for n in _WEIGHTS],
            *[results[n][2] for n in _WEIGHTS], *[results[n][3] for n in _WEIGHTS])


def kernel(x, mem, mix_norm_g, w_in, conv_dw_w, conv_dw_b, conv_ln_g, conv_ln_b, conv_w_pw, ssm_a_re, ssm_a_im, ssm_log_dt, ssm_b_re, ssm_b_im, ssm_c_re, ssm_c_im, ssm_d, ssm_w_glu, w_out, xa_norm_g, mem_norm_g, xa_w_q, xa_w_kv, xa_w_o, ffn_norm_g, ffn_w_up, ffn_dw_w, ffn_w_down, final_norm_g, loss_target, m_mix_norm_g, m_w_in, m_conv_dw_w, m_conv_dw_b, m_conv_ln_g, m_conv_ln_b, m_conv_w_pw, m_ssm_a_re, m_ssm_a_im, m_ssm_log_dt, m_ssm_b_re, m_ssm_b_im, m_ssm_c_re, m_ssm_c_im, m_ssm_d, m_ssm_w_glu, m_w_out, m_xa_norm_g, m_mem_norm_g, m_xa_w_q, m_xa_w_kv, m_xa_w_o, m_ffn_norm_g, m_ffn_w_up, m_ffn_dw_w, m_ffn_w_down, m_final_norm_g, v_mix_norm_g, v_w_in, v_conv_dw_w, v_conv_dw_b, v_conv_ln_g, v_conv_ln_b, v_conv_w_pw, v_ssm_a_re, v_ssm_a_im, v_ssm_log_dt, v_ssm_b_re, v_ssm_b_im, v_ssm_c_re, v_ssm_c_im, v_ssm_d, v_ssm_w_glu, v_w_out, v_xa_norm_g, v_mem_norm_g, v_xa_w_q, v_xa_w_kv, v_xa_w_o, v_ffn_norm_g, v_ffn_w_up, v_ffn_dw_w, v_ffn_w_down, v_final_norm_g):
    args = (x, mem, mix_norm_g, w_in, conv_dw_w, conv_dw_b, conv_ln_g, conv_ln_b, conv_w_pw, ssm_a_re, ssm_a_im, ssm_log_dt, ssm_b_re, ssm_b_im, ssm_c_re, ssm_c_im, ssm_d, ssm_w_glu, w_out, xa_norm_g, mem_norm_g, xa_w_q, xa_w_kv, xa_w_o, ffn_norm_g, ffn_w_up, ffn_dw_w, ffn_w_down, final_norm_g)
    ms = (m_mix_norm_g, m_w_in, m_conv_dw_w, m_conv_dw_b, m_conv_ln_g, m_conv_ln_b, m_conv_w_pw, m_ssm_a_re, m_ssm_a_im, m_ssm_log_dt, m_ssm_b_re, m_ssm_b_im, m_ssm_c_re, m_ssm_c_im, m_ssm_d, m_ssm_w_glu, m_w_out, m_xa_norm_g, m_mem_norm_g, m_xa_w_q, m_xa_w_kv, m_xa_w_o, m_ffn_norm_g, m_ffn_w_up, m_ffn_dw_w, m_ffn_w_down, m_final_norm_g)
    vs = (v_mix_norm_g, v_w_in, v_conv_dw_w, v_conv_dw_b, v_conv_ln_g, v_conv_ln_b, v_conv_w_pw, v_ssm_a_re, v_ssm_a_im, v_ssm_log_dt, v_ssm_b_re, v_ssm_b_im, v_ssm_c_re, v_ssm_c_im, v_ssm_d, v_ssm_w_glu, v_w_out, v_xa_norm_g, v_mem_norm_g, v_xa_w_q, v_xa_w_kv, v_xa_w_o, v_ffn_norm_g, v_ffn_w_up, v_ffn_dw_w, v_ffn_w_down, v_final_norm_g)
    return _step(dict(zip(_FWD, args)), loss_target, dict(zip(_WEIGHTS, ms)), dict(zip(_WEIGHTS, vs)))
```

```python
import functools

import jax
import jax.numpy as jnp
from jax import lax
from jax.experimental import pallas as pl
from jax.experimental.pallas import tpu as pltpu
from jax.experimental.pallas import tpu_sc as plsc

F32 = jnp.float32
BF16 = jnp.bfloat16
MESH_ID = pl.DeviceIdType.MESH
N_DEV = 8
EPS = 1e-6
VMEM_LIMIT = 48 * 1024 * 1024
ANY = pl.BlockSpec(memory_space=pl.ANY)

ADAM_LR = 0.001
ADAM_B1 = 0.9
ADAM_B2 = 0.999
ADAM_EPS = 1e-08
ADAM_WD = 0.01
ADAM_STEP = 10

CONV_K = 31
FFN_K = 3
XA_HEADS = 4
SSM_GROUP = 16
SSM_STATE = 64
HALO = 32
LANE = 128
SSM_LANES = 512
AG_COLLECTIVE_ID = 1
RS_SIBLING_COLLECTIVE_ID = 2
RS_CHIPS_COLLECTIVE_ID = 3


def _pick(n, prefs):
    for p in prefs:
        if p <= n and n % p == 0:
            return p
    return n


def _params(sem, vmem=VMEM_LIMIT):
    return pltpu.CompilerParams(dimension_semantics=sem, vmem_limit_bytes=vmem)


_LAST_CALL = []


def _tc_call(*call_args, **call_kwargs):
    call = pl.pallas_call(*call_args, **call_kwargs)

    def run(*args):
        args = list(args)
        if _LAST_CALL:
            i = next(k for k, a in enumerate(args) if a.ndim >= 2)
            args[i] = lax.optimization_barrier((args[i], _LAST_CALL[0]))[0]
        out = call(*args)
        _LAST_CALL[:] = [out[0] if isinstance(out, (tuple, list)) else out]
        return out

    return run


def _sigmoid(x):
    return 1.0 / (1.0 + jnp.exp(-x))


def _silu(x):
    return x * _sigmoid(x)


def _gelu(x):
    return 0.5 * x * (1.0 + jnp.tanh(0.7978845608028654 * (x + 0.044715 * (x * x * x))))


def _rms(x, g):
    return x * lax.rsqrt(jnp.mean(x * x, axis=-1, keepdims=True) + EPS) * g


def _convpost(hc, bias, ln_g, ln_b):
    h = hc + bias
    mu = jnp.mean(h, axis=-1, keepdims=True)
    xc = h - mu
    y = xc * lax.rsqrt(jnp.mean(xc * xc, axis=-1, keepdims=True) + EPS)
    return _silu(y * ln_g + ln_b)


def _mixf(gla, glb, ya, ga, gb):
    return _sigmoid(gla) * ya + _sigmoid(glb) * (ga * _sigmoid(gb))


class _W:
    def __init__(self, arr, layer, blocked):
        self.arr, self.layer, self.blocked = arr, layer, blocked
        if blocked:
            _, _, self.K, self.nb = arr.shape
            self.N = N_DEV * self.nb
        else:
            _, self.K, self.N = arr.shape
            self.nb = self.N

    def spec(self, tk, tn, ki, ni):
        l = self.layer
        if self.blocked:
            per = self.nb // tn
            return pl.BlockSpec((None, None, tk, tn), lambda *g: (ni(*g) // per, l, ki(*g), ni(*g) % per))
        return pl.BlockSpec((None, tk, tn), lambda *g: (l, ki(*g), ni(*g)))


_M_TILES = (1024, 512, 256, 128, 64, 32, 16, 8)
_N_TILES = (1408, 1024, 896, 512, 256, 128)
_K_TILES = (512, 1408, 896, 256, 128)
MAX_FULL_K = 2048
_K_FULL_TILES = (2048, 1408, 1024, 896, 512, 256, 128)
MAX_FULL_T = 4096


def _mm_nn(name, a, w, out_dtype, add=None):
    M, K = a.shape
    assert K == w.K
    tm, tn = _pick(M, _M_TILES), _pick(w.nb, _N_TILES)
    tk = K if K <= MAX_FULL_K else _pick(K, _K_FULL_TILES)
    nk = K // tk

    def body(*refs):
        a_ref, w_ref = refs[:2]
        r_ref = refs[2] if add is not None else None
        o_ref = refs[3] if add is not None else refs[2]
        part = jnp.dot(a_ref[...].astype(BF16), w_ref[...], preferred_element_type=F32)
        if nk == 1:
            o_ref[...] = (part if add is None else part + r_ref[...]).astype(o_ref.dtype)
            return
        acc = refs[-1]
        k = pl.program_id(2)

        @pl.when(k == 0)
        def _():
            acc[...] = part

        @pl.when(k > 0)
        def _():
            acc[...] += part

        @pl.when(k == nk - 1)
        def _():
            res = acc[...]
            if add is not None:
                res = res + r_ref[...]
            o_ref[...] = res.astype(o_ref.dtype)

    in_specs = [pl.BlockSpec((tm, tk), lambda i, j, k: (i, k)),
                w.spec(tk, tn, lambda i, j, k: k, lambda i, j, k: j)]
    args = [a, w.arr]
    if add is not None:
        in_specs.append(pl.BlockSpec((tm, tn), lambda i, j, k: (i, j)))
        args.append(add)
    return _tc_call(
        body, name=name, grid=(M // tm, w.N // tn, nk), in_specs=in_specs,
        out_specs=pl.BlockSpec((tm, tn), lambda i, j, k: (i, j)),
        out_shape=jax.ShapeDtypeStruct((M, w.N), out_dtype),
        scratch_shapes=[] if nk == 1 else [pltpu.VMEM((tm, tn), F32)],
        compiler_params=_params(("parallel", "parallel", "arbitrary")),
    )(*args)


def _mm_nt(name, a, w, out_dtype):
    M, N = a.shape
    assert N == w.N
    tm, tkk = _pick(M, _M_TILES), _pick(w.K, _N_TILES)
    tnn = w.nb if w.nb <= MAX_FULL_K else _pick(w.nb, _K_FULL_TILES)
    nn = N // tnn

    def body(a_ref, w_ref, o_ref, *scratch):
        part = lax.dot_general(a_ref[...].astype(BF16), w_ref[...], (((1,), (1,)), ((), ())),
                               preferred_element_type=F32)
        if nn == 1:
            o_ref[...] = part.astype(o_ref.dtype)
            return
        acc = scratch[0]
        n = pl.program_id(2)

        @pl.when(n == 0)
        def _():
            acc[...] = part

        @pl.when(n > 0)
        def _():
            acc[...] += part

        @pl.when(n == nn - 1)
        def _():
            o_ref[...] = acc[...].astype(o_ref.dtype)

    return _tc_call(
        body, name=name, grid=(M // tm, w.K // tkk, nn),
        in_specs=[pl.BlockSpec((tm, tnn), lambda i, j, n: (i, n)),
                  w.spec(tkk, tnn, lambda i, j, n: j, lambda i, j, n: n)],
        out_specs=pl.BlockSpec((tm, tkk), lambda i, j, n: (i, j)),
        out_shape=jax.ShapeDtypeStruct((M, w.K), out_dtype),
        scratch_shapes=[] if nn == 1 else [pltpu.VMEM((tm, tkk), F32)],
        compiler_params=_params(("parallel", "parallel", "arbitrary")),
    )(a, w.arr)


def _mm_tn(name, a, b, nb=None):
    T, K = a.shape
    _, N = b.shape
    width = N if nb is None else nb
    assert T <= MAX_FULL_T
    tkk, tn = _pick(K, (512, 256, 128)), _pick(width, _N_TILES)
    per = width // tn

    def body(a_ref, b_ref, o_ref):
        o_ref[...] = lax.dot_general(a_ref[...].astype(BF16), b_ref[...].astype(BF16), (((0,), (0,)), ((), ())),
                                     preferred_element_type=F32).astype(o_ref.dtype)

    if nb is None:
        out_spec = pl.BlockSpec((tkk, tn), lambda j, i: (i, j))
        out_shape = jax.ShapeDtypeStruct((K, N), BF16)
    else:
        out_spec = pl.BlockSpec((None, tkk, tn), lambda j, i: (j // per, i, j % per))
        out_shape = jax.ShapeDtypeStruct((N_DEV, K, nb), BF16)
    return _tc_call(
        body, name=name, grid=(N // tn, K // tkk),
        in_specs=[pl.BlockSpec((T, tkk), lambda j, i: (0, i)),
                  pl.BlockSpec((T, tn), lambda j, i: (0, j))],
        out_specs=out_spec, out_shape=out_shape,
        compiler_params=_params(("parallel", "parallel")),
    )(a, b)


def _rowwise(name, fn, rows, consts, outs, accs=(), tile=256, ncol=1, nrows=None):
    n_r, n_c, n_o = len(rows), len(consts), len(outs)
    T = rows[0][0].shape[0] if nrows is None else nrows
    tile = _pick(T, tuple(t for t in (512, 256, 128, 64, 32, 16, 8) if t <= tile))
    nt = T // tile
    const_rows = [c[1] if isinstance(c, tuple) else None for c in consts]
    consts = [c[0] if isinstance(c, tuple) else c for c in consts]

    def body(*refs):
        vals = [r[...] for r in refs[:n_r]]
        vals += [r[...] if l is None else r[l:l + 1, :] for r, l in zip(refs[n_r:n_r + n_c], const_rows)]
        res = fn(*vals)
        if not isinstance(res, (tuple, list)):
            res = (res,)
        o_refs = refs[n_r + n_c:n_r + n_c + n_o]
        a_refs = refs[n_r + n_c + n_o:]
        for r, v in zip(o_refs, res[:n_o]):
            r[...] = v.astype(r.dtype)
        first = pl.program_id(1) == 0
        for r, v in zip(a_refs, res[n_o:]):
            @pl.when(first)
            def _(r=r, v=v):
                r[...] = v.astype(F32)

            @pl.when(jnp.logical_not(first))
            def _(r=r, v=v):
                r[...] += v.astype(F32)

    in_specs, args = [], []
    for arr, w, off, roff in rows:
        rb = roff // tile
        assert roff % tile == 0
        in_specs.append(pl.BlockSpec((tile, w), lambda j, i, off=off, rb=rb: (i + rb, off + j)))
        args.append(arr)
    for cst in consts:
        in_specs.append(pl.BlockSpec(cst.shape, lambda j, i: (0, 0)))
        args.append(cst)
    out_specs, out_shape = [], []
    for tw, dt in outs:
        out_specs.append(pl.BlockSpec((tile, tw // ncol), lambda j, i: (i, j)))
        out_shape.append(jax.ShapeDtypeStruct((T, tw), dt))
    for nr, tw in accs:
        out_specs.append(pl.BlockSpec((nr, tw // ncol), lambda j, i: (0, j)))
        out_shape.append(jax.ShapeDtypeStruct((nr, tw), F32))
    res = _tc_call(
        body, name=name, grid=(ncol, nt), in_specs=in_specs, out_specs=out_specs, out_shape=out_shape,
        compiler_params=_params(("parallel", "arbitrary")),
    )(*args)
    return res


def _rms_fwd(name, x, g):
    D = x.shape[1]
    return _rowwise(name, lambda xv, gv: _rms(xv, gv), [(x, D, 0, 0)], [g], [(D, BF16)])[0]


def _rms_bwd(name, x, g, dh, dx_in):
    D = x.shape[1]

    def fn(xv, dhv, dxv, gv):
        _, vjp = jax.vjp(_rms, xv, gv)
        dx, dg = vjp(dhv.astype(F32))
        tot = dx + dxv
        return tot, tot, jnp.sum(dg, axis=0, keepdims=True)

    return _rowwise(name, fn, [(x, D, 0, 0), (dh, D, 0, 0), (dx_in, D, 0, 0)], [g], [(D, F32), (D, BF16)], [(1, D)],
                    tile=256)


def _lag_views(win, K, R, forward):
    n = win.shape[0]
    for r in range(8):
        if r >= K:
            break
        if r == 0:
            rolled = win
        else:
            rolled = pltpu.roll(win, (n - r) if forward else r, axis=0)
        for q in range((K - 1 - r) // 8 + 1):
            s = 8 * q + r
            if forward:
                yield s, rolled[8 * q:8 * q + R]
            else:
                yield s, rolled[HALO - 8 * q:HALO - 8 * q + R]


def _conv_chunk(win, w_ref, K, R):
    acc = None
    for s, view in _lag_views(win, K, R, forward=False):
        term = w_ref[K - 1 - s:K - s, :] * view
        acc = term if acc is None else acc + term
    return acc


def _conv_chunk_t(win, w_ref, K, R):
    acc = None
    for s, view in _lag_views(win, K, R, forward=True):
        term = w_ref[K - 1 - s:K - s, :] * view
        acc = term if acc is None else acc + term
    return acc


def _conv_dw(xwin, dy, K, R):
    taps = [None] * K
    for s, view in _lag_views(xwin, K, R, forward=False):
        taps[K - 1 - s] = jnp.sum(dy * view, axis=0, keepdims=True)
    return taps


def _chunks(T):
    R = _pick(T, (128, 64, 32))
    return R, T // R


def _glu_conv_fwd(name, proj, w, cw_total):
    T = proj.shape[0]
    C = cw_total
    cw = LANE
    nb = C // cw
    R, nch = _chunks(T)

    def body(a_ref, b_ref, w_ref, o_ref, s_ref):
        s_ref[0:HALO, :] = jnp.zeros((HALO, cw), F32)

        def fill(i, _):
            r0 = pl.multiple_of(i * R, R)
            s_ref[pl.ds(HALO + r0, R), :] = a_ref[pl.ds(r0, R), :] * _sigmoid(b_ref[pl.ds(r0, R), :])
            return 0

        lax.fori_loop(0, nch, fill, 0)

        def conv(i, _):
            r0 = pl.multiple_of(i * R, R)
            o_ref[pl.ds(r0, R), :] = _conv_chunk(s_ref[pl.ds(r0, R + HALO), :], w_ref, CONV_K, R)
            return 0

        lax.fori_loop(0, nch, conv, 0)

    return _tc_call(
        body, name=name, grid=(nb,),
        in_specs=[pl.BlockSpec((T, cw), lambda j: (0, j)), pl.BlockSpec((T, cw), lambda j: (0, nb + j)),
                  pl.BlockSpec((HALO, cw), lambda j: (0, j))],
        out_specs=pl.BlockSpec((T, cw), lambda j: (0, j)),
        out_shape=jax.ShapeDtypeStruct((T, C), F32),
        scratch_shapes=[pltpu.VMEM((T + HALO, cw), F32)],
        compiler_params=_params(("parallel",)),
    )(proj, proj, w)


def _glu_conv_bwd(name, proj, w, dhc, cw_total):
    T = proj.shape[0]
    C = cw_total
    cw = LANE
    nb = C // cw
    R, nch = _chunks(T)

    def body(a_ref, b_ref, w_ref, dy_ref, da_ref, db_ref, dw_ref, s_ref, g_ref, acc_ref):
        s_ref[0:HALO, :] = jnp.zeros((HALO, cw), F32)
        g_ref[T:T + HALO, :] = jnp.zeros((HALO, cw), F32)
        acc_ref[...] = jnp.zeros_like(acc_ref)

        def fill(i, _):
            r0 = pl.multiple_of(i * R, R)
            s_ref[pl.ds(HALO + r0, R), :] = a_ref[pl.ds(r0, R), :] * _sigmoid(b_ref[pl.ds(r0, R), :])
            g_ref[pl.ds(r0, R), :] = dy_ref[pl.ds(r0, R), :]
            return 0

        lax.fori_loop(0, nch, fill, 0)

        def back(i, _):
            r0 = pl.multiple_of(i * R, R)
            dhg = _conv_chunk_t(g_ref[pl.ds(r0, R + HALO), :], w_ref, CONV_K, R)
            av = a_ref[pl.ds(r0, R), :]
            sg = _sigmoid(b_ref[pl.ds(r0, R), :])
            da_ref[pl.ds(r0, R), :] = (dhg * sg).astype(da_ref.dtype)
            db_ref[pl.ds(r0, R), :] = (dhg * av * sg * (1.0 - sg)).astype(db_ref.dtype)
            taps = _conv_dw(s_ref[pl.ds(r0, R + HALO), :], dy_ref[pl.ds(r0, R), :], CONV_K, R)
            for k, tap in enumerate(taps):
                acc_ref[k:k + 1, :] += tap
            return 0

        lax.fori_loop(0, nch, back, 0)
        dw_ref[...] = acc_ref[...]

    return _tc_call(
        body, name=name, grid=(nb,),
        in_specs=[pl.BlockSpec((T, cw), lambda j: (0, j)), pl.BlockSpec((T, cw), lambda j: (0, nb + j)),
                  pl.BlockSpec((HALO, cw), lambda j: (0, j)), pl.BlockSpec((T, cw), lambda j: (0, j))],
        out_specs=[pl.BlockSpec((T, cw), lambda j: (0, j)), pl.BlockSpec((T, cw), lambda j: (0, j)),
                   pl.BlockSpec((HALO, cw), lambda j: (0, j))],
        out_shape=[jax.ShapeDtypeStruct((T, C), BF16), jax.ShapeDtypeStruct((T, C), BF16),
                   jax.ShapeDtypeStruct((HALO, C), F32)],
        scratch_shapes=[pltpu.VMEM((T + HALO, cw), F32), pltpu.VMEM((T + HALO, cw), F32),
                        pltpu.VMEM((HALO, cw), F32)],
        compiler_params=_params(("parallel",)),
    )(proj, proj, w, dhc)


def _ffn_conv_fwd(name, up, w, dff):
    T = up.shape[0]
    cw = LANE
    nb = dff // cw
    R, nch = _chunks(T)

    def body(g_ref, v_ref, wg_ref, wv_ref, o_ref, sg_ref, sv_ref):
        sg_ref[0:HALO, :] = jnp.zeros((HALO, cw), F32)
        sv_ref[0:HALO, :] = jnp.zeros((HALO, cw), F32)

        def fill(i, _):
            r0 = pl.multiple_of(i * R, R)
            sg_ref[pl.ds(HALO + r0, R), :] = g_ref[pl.ds(r0, R), :]
            sv_ref[pl.ds(HALO + r0, R), :] = v_ref[pl.ds(r0, R), :]
            return 0

        lax.fori_loop(0, nch, fill, 0)

        def conv(i, _):
            r0 = pl.multiple_of(i * R, R)
            gc = _conv_chunk(sg_ref[pl.ds(r0, R + HALO), :], wg_ref, FFN_K, R)
            vc = _conv_chunk(sv_ref[pl.ds(r0, R + HALO), :], wv_ref, FFN_K, R)
            o_ref[pl.ds(r0, R), :] = (_silu(gc) * vc).astype(o_ref.dtype)
            return 0

        lax.fori_loop(0, nch, conv, 0)

    return _tc_call(
        body, name=name, grid=(nb,),
        in_specs=[pl.BlockSpec((T, cw), lambda j: (0, j)), pl.BlockSpec((T, cw), lambda j: (0, nb + j)),
                  pl.BlockSpec((8, cw), lambda j: (0, j)), pl.BlockSpec((8, cw), lambda j: (0, nb + j))],
        out_specs=pl.BlockSpec((T, cw), lambda j: (0, j)),
        out_shape=jax.ShapeDtypeStruct((T, dff), BF16),
        scratch_shapes=[pltpu.VMEM((T + HALO, cw), F32), pltpu.VMEM((T + HALO, cw), F32)],
        compiler_params=_params(("parallel",)),
    )(up, up, w, w)


def _ffn_conv_bwd(name, up, w, dact, dff):
    T = up.shape[0]
    cw = LANE
    nb = dff // cw
    R, nch = _chunks(T)

    def body(g_ref, v_ref, wg_ref, wv_ref, da_ref, dg_ref, dv_ref, dwg_ref, dwv_ref,
             sg_ref, sv_ref, tg_ref, tv_ref, ag_ref, av_ref):
        zero = jnp.zeros((HALO, cw), F32)
        sg_ref[0:HALO, :] = zero
        sv_ref[0:HALO, :] = zero
        tg_ref[T:T + HALO, :] = zero
        tv_ref[T:T + HALO, :] = zero
        ag_ref[...] = jnp.zeros_like(ag_ref)
        av_ref[...] = jnp.zeros_like(av_ref)

        def fill(i, _):
            r0 = pl.multiple_of(i * R, R)
            sg_ref[pl.ds(HALO + r0, R), :] = g_ref[pl.ds(r0, R), :]
            sv_ref[pl.ds(HALO + r0, R), :] = v_ref[pl.ds(r0, R), :]
            return 0

        lax.fori_loop(0, nch, fill, 0)

        def grads(i, _):
            r0 = pl.multiple_of(i * R, R)
            gwin = sg_ref[pl.ds(r0, R + HALO), :]
            vwin = sv_ref[pl.ds(r0, R + HALO), :]
            gc = _conv_chunk(gwin, wg_ref, FFN_K, R)
            vc = _conv_chunk(vwin, wv_ref, FFN_K, R)
            da = da_ref[pl.ds(r0, R), :].astype(F32)
            sg = _sigmoid(gc)
            dgc = da * vc * (sg * (1.0 + gc * (1.0 - sg)))
            dvc = da * (gc * sg)
            tg_ref[pl.ds(r0, R), :] = dgc
            tv_ref[pl.ds(r0, R), :] = dvc
            for k, tap in enumerate(_conv_dw(gwin, dgc, FFN_K, R)):
                ag_ref[k:k + 1, :] += tap
            for k, tap in enumerate(_conv_dw(vwin, dvc, FFN_K, R)):
                av_ref[k:k + 1, :] += tap
            return 0

        lax.fori_loop(0, nch, grads, 0)

        def back(i, _):
            r0 = pl.multiple_of(i * R, R)
            dg_ref[pl.ds(r0, R), :] = _conv_chunk_t(tg_ref[pl.ds(r0, R + HALO), :], wg_ref, FFN_K, R).astype(dg_ref.dtype)
            dv_ref[pl.ds(r0, R), :] = _conv_chunk_t(tv_ref[pl.ds(r0, R + HALO), :], wv_ref, FFN_K, R).astype(dv_ref.dtype)
            return 0

        lax.fori_loop(0, nch, back, 0)
        dwg_ref[...] = ag_ref[...]
        dwv_ref[...] = av_ref[...]

    col = lambda j: (0, j)
    dg, dv, dwg, dwv = _tc_call(
        body, name=name, grid=(nb,),
        in_specs=[pl.BlockSpec((T, cw), col), pl.BlockSpec((T, cw), lambda j: (0, nb + j)),
                  pl.BlockSpec((8, cw), col), pl.BlockSpec((8, cw), lambda j: (0, nb + j)),
                  pl.BlockSpec((T, cw), col)],
        out_specs=[pl.BlockSpec((T, cw), col), pl.BlockSpec((T, cw), col),
                   pl.BlockSpec((8, cw), col), pl.BlockSpec((8, cw), col)],
        out_shape=[jax.ShapeDtypeStruct((T, dff), BF16), jax.ShapeDtypeStruct((T, dff), BF16),
                   jax.ShapeDtypeStruct((8, dff), F32), jax.ShapeDtypeStruct((8, dff), F32)],
        scratch_shapes=[pltpu.VMEM((T + HALO, cw), F32), pltpu.VMEM((T + HALO, cw), F32),
                        pltpu.VMEM((T + HALO, cw), F32), pltpu.VMEM((T + HALO, cw), F32),
                        pltpu.VMEM((8, cw), F32), pltpu.VMEM((8, cw), F32)],
        compiler_params=_params(("parallel",)),
    )(up, up, w, w, dact)
    return jnp.concatenate([dg, dv], axis=1), jnp.concatenate([dwg, dwv], axis=1)


def _zoh(a_re, a_im, log_dt):
    ar = jnp.minimum(a_re, -1e-4)
    ai = a_im
    dt = jnp.exp(log_dt)
    mag = jnp.exp(dt * ar)
    abar_re = mag * jnp.cos(dt * ai)
    abar_im = mag * jnp.sin(dt * ai)
    den = ar * ar + ai * ai
    nr = abar_re - 1.0
    ni = abar_im
    return abar_re, abar_im, (nr * ar + ni * ai) / den, (ni * ar - nr * ai) / den


def _discretize(a_re, a_im, log_dt, a_re_h, a_im_h, log_dt_h, b_re, b_im):
    abar_re, abar_im, _, _ = _zoh(a_re, a_im, log_dt)
    _, _, z_re, z_im = _zoh(a_re_h, a_im_h, log_dt_h)
    return abar_re, abar_im, z_re * b_re - z_im * b_im, z_re * b_im + z_im * b_re


def _full_specs(arrs):
    return [pl.BlockSpec(a.shape, lambda *_, n=len(a.shape): (0,) * n) for a in arrs]


def _ssm_prep(name, raw):
    def body(*refs):
        res = _discretize(*[r[...] for r in refs[:8]])
        for r, v in zip(refs[8:], res):
            r[...] = v

    outs = [jax.ShapeDtypeStruct(raw[0].shape, F32)] * 2 + [jax.ShapeDtypeStruct(raw[6].shape, F32)] * 2
    return _tc_call(body, name=name, in_specs=_full_specs(raw), out_specs=_full_specs(outs), out_shape=outs)(*raw)


def _ssm_prep_bwd(name, raw, cots):
    G = raw[0].shape[0]
    H = raw[3].shape[0] // G

    def body(*refs):
        _, vjp = jax.vjp(_discretize, *[r[...] for r in refs[:8]])
        g = vjp(tuple(r[...] for r in refs[8:12]))
        outs = refs[12:]
        for k in range(3):
            rep = g[3 + k]
            outs[k][...] = g[k] + jnp.sum(rep.reshape(G, H, rep.shape[1]), axis=1)
        outs[3][...] = g[6]
        outs[4][...] = g[7]

    outs = [jax.ShapeDtypeStruct(a.shape, F32) for a in (raw[0], raw[1], raw[2], raw[6], raw[7])]
    return _tc_call(body, name=name, in_specs=_full_specs(list(raw) + list(cots)), out_specs=_full_specs(outs),
                          out_shape=outs)(*raw, *cots)


def _cmul(ar, ai, br, bi):
    return ar * br - ai * bi, ar * bi + ai * br


def _scan_coefs(ar, ai, reverse):
    W = ar.shape[1]
    row = lax.broadcasted_iota(jnp.int32, (8, W), 0)
    p = [None] * 9
    p[1] = (ar, ai)
    for n in range(2, 9):
        p[n] = _cmul(*p[n // 2], *p[n - n // 2])
    steps = []
    for s in (1, 2, 4):
        valid = (row <= 7 - s) if reverse else (row >= s)
        steps.append((jnp.where(valid, p[s][0], 0.0), jnp.where(valid, p[s][1], 0.0)))
    pr = jnp.zeros((8, W), F32)
    pi = jnp.zeros((8, W), F32)
    for i in range(8):
        n = (8 - i) if reverse else (i + 1)
        pr = jnp.where(row == i, p[n][0], pr)
        pi = jnp.where(row == i, p[n][1], pi)
    return steps, (pr, pi)


def _scan_tile(xr, xi, cr, ci, coefs, reverse):
    steps, (pr, pi) = coefs
    for s, (sr, si) in zip((1, 2, 4), steps):
        shift = (8 - s) if reverse else s
        rr = pltpu.roll(xr, shift, axis=0)
        ri = pltpu.roll(xi, shift, axis=0)
        xr, xi = xr + sr * rr - si * ri, xi + sr * ri + si * rr
    xr, xi = xr + pr * cr - pi * ci, xi + pr * ci + pi * cr
    return xr, xi


def _edge_rows(x, reverse):
    W = x.shape[1]
    return jnp.broadcast_to(x[0:1, :] if reverse else x[7:8, :], (8, W))


def _ssm_chunk(T):
    return _pick(T, (512, 256, 128, 64))


def _ssm_fwd(name, proj, u_off, p, width):
    T = proj.shape[0]
    NB = width // LANE
    Q = _ssm_chunk(T)
    nch = T // Q
    W = SSM_LANES

    def body(u_ref, bre, bim, cre, cim, ar_ref, ai_ref, d_ref, y_ref, ckr_ref, cki_ref, br_s, bi_s, car_r, car_i):
        c = pl.program_id(1)

        @pl.when(c == 0)
        def _():
            car_r[...] = jnp.zeros_like(car_r)
            car_i[...] = jnp.zeros_like(car_i)

        ckr_ref[...] = car_r[...]
        cki_ref[...] = car_i[...]
        u = u_ref[...]
        u16 = u.astype(BF16)
        br_s[...] = jnp.dot(u16, bre[...], preferred_element_type=F32)
        bi_s[...] = jnp.dot(u16, bim[...], preferred_element_type=F32)
        coefs = _scan_coefs(ar_ref[...], ai_ref[...], False)

        def tile(j, carry):
            r0 = pl.multiple_of(j * 8, 8)
            xr, xi = _scan_tile(br_s[pl.ds(r0, 8), :], bi_s[pl.ds(r0, 8), :], carry[0], carry[1], coefs, False)
            br_s[pl.ds(r0, 8), :] = xr
            bi_s[pl.ds(r0, 8), :] = xi
            return _edge_rows(xr, False), _edge_rows(xi, False)

        cr, ci = lax.fori_loop(0, Q // 8, tile, (car_r[...], car_i[...]))
        car_r[...] = cr
        car_i[...] = ci
        nt = (((1,), (1,)), ((), ()))
        y = (lax.dot_general(br_s[...].astype(BF16), cre[...], nt, preferred_element_type=F32)
             - lax.dot_general(bi_s[...].astype(BF16), cim[...], nt, preferred_element_type=F32)
             + d_ref[...] * u)
        y_ref[...] = _gelu(y).astype(y_ref.dtype)

    blk = lambda b, c: (b, 0, 0)
    mat = pl.BlockSpec((None, LANE, W), blk)
    vec = pl.BlockSpec((None, 1, W), blk)
    ck = pl.BlockSpec((None, None, 8, W), lambda b, c: (b, c, 0, 0))
    return _tc_call(
        body, name=name, grid=(NB, nch),
        in_specs=[pl.BlockSpec((Q, LANE), lambda b, c: (c, u_off + b)), mat, mat, mat, mat, vec, vec,
                  pl.BlockSpec((1, LANE), lambda b, c: (0, b))],
        out_specs=[pl.BlockSpec((Q, LANE), lambda b, c: (c, b)), ck, ck],
        out_shape=[jax.ShapeDtypeStruct((T, width), BF16), jax.ShapeDtypeStruct((NB, nch, 8, W), F32),
                   jax.ShapeDtypeStruct((NB, nch, 8, W), F32)],
        scratch_shapes=[pltpu.VMEM((Q, W), F32), pltpu.VMEM((Q, W), F32), pltpu.VMEM((8, W), F32),
                        pltpu.VMEM((8, W), F32)],
        compiler_params=_params(("parallel", "arbitrary")),
    )(proj, p["bre"], p["bim"], p["cre"], p["cim"], p["ar"], p["ai"], p["d"])


def _ssm_bwd(name, proj, u_off, p, ck_r, ck_i, dyg, width):
    T = proj.shape[0]
    NB = width // LANE
    Q = _ssm_chunk(T)
    nch = T // Q
    W = SSM_LANES
    nt_dims = (((1,), (1,)), ((), ()))
    tn_dims = (((0,), (0,)), ((), ()))

    def body(u_ref, dy_ref, ckr_ref, cki_ref, bre, bim, cre, cim, ar_ref, ai_ref, d_ref,
             du_ref, dbr_ref, dbi_ref, dcr_ref, dci_ref, dar_ref, dai_ref, dd_ref,
             xr_s, xi_s, lr_s, li_s, lam_r, lam_i):
        c = pl.program_id(1)

        @pl.when(c == 0)
        def _():
            lam_r[...] = jnp.zeros_like(lam_r)
            lam_i[...] = jnp.zeros_like(lam_i)
            for r in (dbr_ref, dbi_ref, dcr_ref, dci_ref, dar_ref, dai_ref, dd_ref):
                r[...] = jnp.zeros_like(r)

        u = u_ref[...]
        u16 = u.astype(BF16)
        ar, ai = ar_ref[...], ai_ref[...]
        xr_s[0:8, :] = ckr_ref[...]
        xi_s[0:8, :] = cki_ref[...]
        xr_s[8:Q + 8, :] = jnp.dot(u16, bre[...], preferred_element_type=F32)
        xi_s[8:Q + 8, :] = jnp.dot(u16, bim[...], preferred_element_type=F32)
        fcoefs = _scan_coefs(ar, ai, False)

        def ftile(j, carry):
            r0 = pl.multiple_of(j * 8 + 8, 8)
            xr, xi = _scan_tile(xr_s[pl.ds(r0, 8), :], xi_s[pl.ds(r0, 8), :], carry[0], carry[1], fcoefs, False)
            xr_s[pl.ds(r0, 8), :] = xr
            xi_s[pl.ds(r0, 8), :] = xi
            return _edge_rows(xr, False), _edge_rows(xi, False)

        lax.fori_loop(0, Q // 8, ftile, (ckr_ref[...], cki_ref[...]))
        xr16 = xr_s[8:Q + 8, :].astype(BF16)
        xi16 = xi_s[8:Q + 8, :].astype(BF16)
        y = (lax.dot_general(xr16, cre[...], nt_dims, preferred_element_type=F32)
             - lax.dot_general(xi16, cim[...], nt_dims, preferred_element_type=F32) + d_ref[...] * u)
        _, gelu_vjp = jax.vjp(_gelu, y)
        dy = gelu_vjp(dy_ref[...].astype(F32))[0]
        dy16 = dy.astype(BF16)
        dd_ref[...] += jnp.broadcast_to(jnp.sum(dy * u, axis=0, keepdims=True), (8, LANE))
        dcr_ref[...] += lax.dot_general(dy16, xr16, tn_dims, preferred_element_type=F32)
        dci_ref[...] -= lax.dot_general(dy16, xi16, tn_dims, preferred_element_type=F32)
        lr_s[...] = jnp.dot(dy16, cre[...], preferred_element_type=F32)
        li_s[...] = -jnp.dot(dy16, cim[...], preferred_element_type=F32)
        rcoefs = _scan_coefs(ar, -ai, True)
        row = lax.broadcasted_iota(jnp.int32, (8, W), 0)

        def rtile(jj, carry):
            j = Q // 8 - 1 - jj
            r0 = pl.multiple_of(j * 8, 8)
            lr, li = _scan_tile(lr_s[pl.ds(r0, 8), :], li_s[pl.ds(r0, 8), :], carry[0], carry[1], rcoefs, True)
            lr_s[pl.ds(r0, 8), :] = lr
            li_s[pl.ds(r0, 8), :] = li
            cur_r, cur_i = xr_s[pl.ds(r0 + 8, 8), :], xi_s[pl.ds(r0 + 8, 8), :]
            prv_r, prv_i = xr_s[pl.ds(r0, 8), :], xi_s[pl.ds(r0, 8), :]
            xpr = jnp.where(row == 0, _edge_rows(prv_r, False), pltpu.roll(cur_r, 1, axis=0))
            xpi = jnp.where(row == 0, _edge_rows(prv_i, False), pltpu.roll(cur_i, 1, axis=0))
            return (_edge_rows(lr, True), _edge_rows(li, True),
                    carry[2] + lr * xpr + li * xpi, carry[3] + li * xpr - lr * xpi)

        zero = jnp.zeros((8, W), F32)
        cr, ci, sar, sai = lax.fori_loop(0, Q // 8, rtile, (lam_r[...], lam_i[...], zero, zero))
        lam_r[...] = cr
        lam_i[...] = ci
        dar_ref[...] += jnp.broadcast_to(jnp.sum(sar, axis=0, keepdims=True), (8, W))
        dai_ref[...] += jnp.broadcast_to(jnp.sum(sai, axis=0, keepdims=True), (8, W))
        lr16 = lr_s[...].astype(BF16)
        li16 = li_s[...].astype(BF16)
        dbr_ref[...] += lax.dot_general(u16, lr16, tn_dims, preferred_element_type=F32)
        dbi_ref[...] += lax.dot_general(u16, li16, tn_dims, preferred_element_type=F32)
        du = (lax.dot_general(lr16, bre[...], nt_dims, preferred_element_type=F32)
              + lax.dot_general(li16, bim[...], nt_dims, preferred_element_type=F32) + d_ref[...] * dy)
        du_ref[...] = du.astype(du_ref.dtype)

    blk = lambda b, c: (b, 0, 0)
    mat = pl.BlockSpec((None, LANE, W), blk)
    vec = pl.BlockSpec((None, 1, W), blk)
    acc8 = pl.BlockSpec((None, 8, W), blk)
    ck = pl.BlockSpec((None, None, 8, W), lambda b, c: (b, nch - 1 - c, 0, 0))
    return _tc_call(
        body, name=name, grid=(NB, nch),
        in_specs=[pl.BlockSpec((Q, LANE), lambda b, c: (nch - 1 - c, u_off + b)),
                  pl.BlockSpec((Q, LANE), lambda b, c: (nch - 1 - c, b)), ck, ck, mat, mat, mat, mat, vec, vec,
                  pl.BlockSpec((1, LANE), lambda b, c: (0, b))],
        out_specs=[pl.BlockSpec((Q, LANE), lambda b, c: (nch - 1 - c, b)), mat, mat, mat, mat, acc8, acc8,
                   pl.BlockSpec((None, 8, LANE), blk)],
        out_shape=[jax.ShapeDtypeStruct((T, width), BF16)] + [jax.ShapeDtypeStruct((NB, LANE, W), F32)] * 4
                  + [jax.ShapeDtypeStruct((NB, 8, W), F32)] * 2 + [jax.ShapeDtypeStruct((NB, 8, LANE), F32)],
        scratch_shapes=[pltpu.VMEM((Q + 8, W), F32), pltpu.VMEM((Q + 8, W), F32), pltpu.VMEM((Q, W), F32),
                        pltpu.VMEM((Q, W), F32), pltpu.VMEM((8, W), F32), pltpu.VMEM((8, W), F32)],
        compiler_params=_params(("parallel", "arbitrary")),
    )(proj, dyg, ck_r, ck_i, p["bre"], p["bim"], p["cre"], p["cim"], p["ar"], p["ai"], p["d"])


def _block_diag(w):
    G, H, P = w.shape
    eye = jnp.eye(8, dtype=w.dtype)
    return (w.reshape(G // 8, 8, H, 1, P) * eye[None, :, None, :, None]).reshape(G // 8, 8 * H, 8 * P)


def _block_diag_t(d, H, P):
    NB = d.shape[0]
    d = d.reshape(NB, 8, H, 8, P)
    eye = jnp.eye(8, dtype=d.dtype)
    return jnp.sum(d * eye[None, :, None, :, None], axis=3).reshape(NB * 8, H, P)


def _attn_fwd(name, q, kv, heads):
    T, D = q.shape
    Mm = kv.shape[0]
    hd = D // heads
    tq = _pick(T, (512, 256, 128))
    scale = hd ** -0.5

    def body(q_ref, k_ref, v_ref, o_ref):
        s = lax.dot_general(q_ref[...], k_ref[...], (((1,), (1,)), ((), ())), preferred_element_type=F32) * scale
        s = s - jnp.max(s, axis=-1, keepdims=True)
        e = jnp.exp(s)
        p = e / jnp.sum(e, axis=-1, keepdims=True)
        o_ref[...] = jnp.dot(p.astype(BF16), v_ref[...], preferred_element_type=F32).astype(o_ref.dtype)

    return _tc_call(
        body, name=name, grid=(heads, T // tq),
        in_specs=[pl.BlockSpec((tq, hd), lambda h, i: (i, h)), pl.BlockSpec((Mm, hd), lambda h, i: (0, h)),
                  pl.BlockSpec((Mm, hd), lambda h, i: (0, heads + h))],
        out_specs=pl.BlockSpec((tq, hd), lambda h, i: (i, h)),
        out_shape=jax.ShapeDtypeStruct((T, D), BF16),
        compiler_params=_params(("parallel", "parallel")),
    )(q, kv, kv)


def _attn_bwd(name, q, kv, do, heads):
    T, D = q.shape
    Mm = kv.shape[0]
    hd = D // heads
    tq = _pick(T, (512, 256, 128))
    scale = hd ** -0.5
    nt_dims = (((1,), (1,)), ((), ()))
    tn_dims = (((0,), (0,)), ((), ()))

    def body(q_ref, k_ref, v_ref, do_ref, dq_ref, dk_ref, dv_ref):
        i = pl.program_id(1)

        @pl.when(i == 0)
        def _():
            dk_ref[...] = jnp.zeros_like(dk_ref)
            dv_ref[...] = jnp.zeros_like(dv_ref)

        qv, kvl, vv, dov = q_ref[...], k_ref[...], v_ref[...], do_ref[...]
        s = lax.dot_general(qv, kvl, nt_dims, preferred_element_type=F32) * scale
        s = s - jnp.max(s, axis=-1, keepdims=True)
        e = jnp.exp(s)
        p = e / jnp.sum(e, axis=-1, keepdims=True)
        p16 = p.astype(BF16)
        dv_ref[...] += lax.dot_general(p16, dov, tn_dims, preferred_element_type=F32)
        dp = lax.dot_general(dov, vv, nt_dims, preferred_element_type=F32)
        ds = (p * (dp - jnp.sum(dp * p, axis=-1, keepdims=True)) * scale).astype(BF16)
        dq_ref[...] = jnp.dot(ds, kvl, preferred_element_type=F32).astype(dq_ref.dtype)
        dk_ref[...] += lax.dot_general(ds, qv, tn_dims, preferred_element_type=F32)

    return _tc_call(
        body, name=name, grid=(heads, T // tq),
        in_specs=[pl.BlockSpec((tq, hd), lambda h, i: (i, h)), pl.BlockSpec((Mm, hd), lambda h, i: (0, h)),
                  pl.BlockSpec((Mm, hd), lambda h, i: (0, heads + h)), pl.BlockSpec((tq, hd), lambda h, i: (i, h))],
        out_specs=[pl.BlockSpec((tq, hd), lambda h, i: (i, h)), pl.BlockSpec((Mm, hd), lambda h, i: (0, h)),
                   pl.BlockSpec((Mm, hd), lambda h, i: (0, h))],
        out_shape=[jax.ShapeDtypeStruct((T, D), BF16), jax.ShapeDtypeStruct((Mm, D), F32),
                   jax.ShapeDtypeStruct((Mm, D), F32)],
        compiler_params=_params(("parallel", "arbitrary")),
    )(q, kv, kv, do)


def _position():
    return lax.axis_index("x"), lax.axis_index("y"), lax.axis_index("c")


def _allgather(name, blk, row_mode):
    L = blk.shape[0]
    out_shape = (L, N_DEV) + blk.shape[1:] if row_mode else (N_DEV,) + blk.shape
    x_ref = jax.new_ref(blk, memory_space=pltpu.MemorySpace.HBM)
    out_ref = jax.empty_ref(jax.ShapeDtypeStruct(out_shape, blk.dtype), memory_space=pltpu.MemorySpace.HBM)

    def body(send_sems, recv_sems, local_sem):
        x, y, c = _position()
        me, sibling = (x, y, c), (x, y, 1 - c)
        chips = [(1 - x, y), (x, 1 - y), (1 - x, 1 - y)]
        barrier = pltpu.get_barrier_semaphore()
        for peer in [sibling] + [(*chip, c) for chip in chips]:
            pl.semaphore_signal(barrier, inc=1, device_id=peer, device_id_type=MESH_ID)
        pl.semaphore_wait(barrier, 4)

        def slot(px, py, pc):
            b = 4 * px + 2 * py + pc
            return out_ref.at[:, b] if row_mode else out_ref.at[b]

        def copy(k, block, to, src=None):
            return pltpu.make_async_remote_copy(
                src_ref=slot(*block) if src is None else src, dst_ref=slot(*block),
                send_sem=send_sems.at[k], recv_sem=recv_sems.at[k], device_id=to, device_id_type=MESH_ID)

        mine = pltpu.make_async_copy(x_ref, slot(*me), local_sem)
        mine.start()
        first = [copy(0, me, sibling, src=x_ref)]
        first += [copy(1 + j, me, (*chip, c), src=x_ref) for j, chip in enumerate(chips)]
        for cp in first:
            cp.start()
        passed = [copy(4 + j, (*chip, c), sibling) for j, chip in enumerate(chips)]
        for j, chip in enumerate(chips):
            copy(1 + j, (*chip, c), me).wait_recv()
            passed[j].start()
        copy(0, sibling, me).wait_recv()
        for j, chip in enumerate(chips):
            copy(4 + j, (*chip, 1 - c), me).wait_recv()
        for cp in first + passed:
            cp.wait_send()
        mine.wait()

    pl.kernel(
        body, mesh=plsc.ScalarSubcoreMesh(axis_name="sequencer", num_cores=1), name=name,
        scratch_types=(pltpu.SemaphoreType.DMA((7,)), pltpu.SemaphoreType.DMA((7,)), pltpu.SemaphoreType.DMA),
        compiler_params=pltpu.CompilerParams(collective_id=AG_COLLECTIVE_ID),
    )()
    return out_ref[...]


def _sequencer_kernel(name, body, scratch_types, collective_id):
    pl.kernel(
        body, mesh=plsc.ScalarSubcoreMesh(axis_name="sequencer", num_cores=1), name=name,
        scratch_types=scratch_types, compiler_params=pltpu.CompilerParams(collective_id=collective_id),
    )()


def _handshake(peers):
    barrier = pltpu.get_barrier_semaphore()
    for peer in peers:
        pl.semaphore_signal(barrier, inc=1, device_id=peer, device_id_type=MESH_ID)
    pl.semaphore_wait(barrier, len(peers))


def _rs_sibling(name, grads):
    hbm = pltpu.MemorySpace.HBM
    g_ref = jax.new_ref(grads, memory_space=hbm)
    out_ref = jax.empty_ref(jax.ShapeDtypeStruct((4,) + grads.shape[1:], grads.dtype), memory_space=hbm)

    def body(send_sems, recv_sems):
        x, y, c = _position()
        sibling = (x, y, 1 - c)
        _handshake([sibling])
        copies = [pltpu.make_async_remote_copy(
            src_ref=g_ref.at[2 * q + (1 - c)], dst_ref=out_ref.at[q], send_sem=send_sems.at[q],
            recv_sem=recv_sems.at[q], device_id=sibling, device_id_type=MESH_ID) for q in range(4)]
        for cp in copies:
            cp.start()
        for cp in copies:
            cp.wait_recv()
        for cp in copies:
            cp.wait_send()

    _sequencer_kernel(name, body, (pltpu.SemaphoreType.DMA((4,)), pltpu.SemaphoreType.DMA((4,))),
                      RS_SIBLING_COLLECTIVE_ID)
    return out_ref[...]


def _rs_chips(name, part):
    hbm = pltpu.MemorySpace.HBM
    p_ref = jax.new_ref(part, memory_space=hbm)
    out_ref = jax.empty_ref(jax.ShapeDtypeStruct(part.shape, part.dtype), memory_space=hbm)

    def body(send_sems, recv_sems, local_sem):
        x, y, c = _position()
        flips = [(1 - x, y), (x, 1 - y), (1 - x, 1 - y)]
        _handshake([(fx, fy, c) for fx, fy in flips])
        mine = pltpu.make_async_copy(p_ref.at[2 * x + y], out_ref.at[3], local_sem)
        copies = [pltpu.make_async_remote_copy(
            src_ref=p_ref.at[2 * fx + fy], dst_ref=out_ref.at[k], send_sem=send_sems.at[k], recv_sem=recv_sems.at[k],
            device_id=(fx, fy, c), device_id_type=MESH_ID) for k, (fx, fy) in enumerate(flips)]
        for cp in copies:
            cp.start()
        mine.start()
        for cp in copies:
            cp.wait_recv()
        for cp in copies:
            cp.wait_send()
        mine.wait()

    _sequencer_kernel(name, body, (pltpu.SemaphoreType.DMA((3,)), pltpu.SemaphoreType.DMA((3,)),
                                   pltpu.SemaphoreType.DMA), RS_CHIPS_COLLECTIVE_ID)
    return out_ref[...]


def _adamw_math(g, w, m, v):
    m = ADAM_B1 * m + (1.0 - ADAM_B1) * g
    v = ADAM_B2 * v + (1.0 - ADAM_B2) * (g * g)
    m_hat = m / (1.0 - ADAM_B1 ** ADAM_STEP)
    v_hat = v / (1.0 - ADAM_B2 ** ADAM_STEP)
    delta = -ADAM_LR * (m_hat / (jnp.sqrt(v_hat) + ADAM_EPS) + ADAM_WD * w)
    return delta, m, v


def _adamw_natural(name, gs, ws, ms, vs, lead_block=None):
    n = len(gs)

    def body(*refs):
        ins, outs = refs[:4 * n], refs[4 * n:]
        for k in range(n):
            delta, nm, nv = _adamw_math(*[ins[j * n + k][...] for j in range(4)])
            outs[k][...] = delta
            outs[n + k][...] = nm
            outs[2 * n + k][...] = nv

    out_shape = [jax.ShapeDtypeStruct(w.shape, F32) for w in ws] * 3
    if lead_block is None:
        grid = ()
        spec = lambda a: pl.BlockSpec(a.shape, lambda nd=len(a.shape): (0,) * nd)
    else:
        grid = tuple(d // b for d, b in zip(ws[0].shape[:2], lead_block))
        spec = lambda a: pl.BlockSpec(tuple(lead_block) + tuple(a.shape[2:]),
                                      lambda i, j, nd=len(a.shape): (i, j) + (0,) * (nd - 2))
    return _tc_call(
        body, name=name, grid=grid, in_specs=[spec(a) for a in list(gs) + list(ws) + list(ms) + list(vs)],
        out_specs=[spec(o) for o in out_shape], out_shape=out_shape,
        compiler_params=pltpu.CompilerParams(vmem_limit_bytes=VMEM_LIMIT),
    )(*gs, *ws, *ms, *vs)


def _ew_tile(R, C, nblocks):
    budget = (VMEM_LIMIT * 3) // 4
    return _pick(R, tuple(t for t in (1024, 512, 256, 128, 64, 32, 16, 8) if 8 * t * C * nblocks <= budget))


def _pair_sum(name, grads, landed, c_idx):
    _, r, c = grads.shape
    tile = _ew_tile(r, c, 3)

    def body(c_ref, g_ref, s_ref, o_ref):
        o_ref[...] = (g_ref[...].astype(F32) + s_ref[...].astype(F32)).astype(o_ref.dtype)

    return _tc_call(
        body, name=name,
        grid_spec=pltpu.PrefetchScalarGridSpec(
            num_scalar_prefetch=1, grid=(4, r // tile),
            in_specs=[pl.BlockSpec((None, tile, c), lambda q, i, c_ref: (2 * q + c_ref[0], i, 0)),
                      pl.BlockSpec((None, tile, c), lambda q, i, c_ref: (q, i, 0))],
            out_specs=pl.BlockSpec((None, tile, c), lambda q, i, c_ref: (q, i, 0))),
        out_shape=jax.ShapeDtypeStruct((4, r, c), BF16),
        compiler_params=_params(("parallel", "parallel")),
    )(c_idx, grads, landed)


def _adamw_layer(name, got, w, m, v, layer, prev):
    L, r, c = w.shape
    tile = _ew_tile(r, c, 11)

    def body(g0, g1, g2, g3, w_ref, m_ref, v_ref, *rest):
        outs = rest[-4:]
        g = (g0[...].astype(F32) + g1[...].astype(F32)) + (g2[...].astype(F32) + g3[...].astype(F32))
        delta, nm, nv = _adamw_math(g, w_ref[...], m_ref[...], v_ref[...])
        for ref, val in zip(outs, (g, delta, nm, nv)):
            ref[...] = val

    slab = pl.BlockSpec((None, tile, c), lambda i: (layer, i, 0))
    in_specs = [pl.BlockSpec((None, tile, c), lambda i, k=k: (k, i, 0)) for k in range(4)] + [slab] * 3
    args = [got, got, got, got, w, m, v]
    aliases = {}
    if prev is not None:
        in_specs += [ANY] * 4
        args += list(prev)
        aliases = {7 + k: k for k in range(4)}
    return _tc_call(
        body, name=name, grid=(r // tile,), in_specs=in_specs, out_specs=[slab] * 4,
        out_shape=[jax.ShapeDtypeStruct((L, r, c), F32)] * 4, input_output_aliases=aliases,
        compiler_params=_params(("parallel",)),
    )(*args)


def _reduce_pipeline(name, layer, grads, w, m, v, state):
    c_idx = lax.axis_index("c").astype(jnp.int32).reshape(1)
    landed = _rs_sibling(name + "_rs_sibling", grads)
    yield
    part = _pair_sum(name + "_pair_sum", grads, landed, c_idx)
    got = _rs_chips(name + "_rs_chips", part)
    for _ in range(4):
        yield
    state[name] = _adamw_layer(name + "_adamw", got, w, m, v, layer, state.get(name))


def _loss_head(name, x, g, target):
    T, D = x.shape

    def fn(xv, tv, gv):
        def f(xx, gg):
            err = _rms(xx, gg) - tv
            return 0.5 * jnp.sum(jnp.mean(err * err, axis=-1, keepdims=True))

        loss, (dx, dg) = jax.value_and_grad(f, argnums=(0, 1))(xv, gv)
        return dx, dx, jnp.full((8, LANE), loss, F32), jnp.sum(dg, axis=0, keepdims=True)

    dx, dx16, loss, dg = _rowwise(name, fn, [(x, D, 0, 0), (target, D, 0, 0)], [g], [(D, F32), (D, BF16)],
                                  [(8, LANE), (1, D)], tile=128)
    return loss[0, 0], dx, dx16, dg


_SHARDED_COL = ("w_in", "conv_dw_w", "conv_w_pw", "ssm_w_glu", "xa_w_kv", "ffn_w_up", "ffn_dw_w")
_SHARDED_ROW = ("w_out", "xa_w_q", "xa_w_o", "ffn_w_down")
_WEIGHTS = ['mix_norm_g', 'w_in', 'conv_dw_w', 'conv_dw_b', 'conv_ln_g', 'conv_ln_b', 'conv_w_pw', 'ssm_a_re',
            'ssm_a_im', 'ssm_log_dt', 'ssm_b_re', 'ssm_b_im', 'ssm_c_re', 'ssm_c_im', 'ssm_d', 'ssm_w_glu', 'w_out',
            'xa_norm_g', 'mem_norm_g', 'xa_w_q', 'xa_w_kv', 'xa_w_o', 'ffn_norm_g', 'ffn_w_up', 'ffn_dw_w',
            'ffn_w_down', 'final_norm_g']
_FWD = ['x', 'mem'] + _WEIGHTS
_AG_ORDER = ("w_in", "conv_dw_w", "conv_w_pw", "ssm_w_glu", "w_out", "xa_w_q", "xa_w_kv", "xa_w_o", "ffn_w_up",
             "ffn_dw_w", "ffn_w_down")


def _pad_rows(a, rows):
    return jnp.pad(a, ((0, 0), (0, rows - a.shape[1]), (0, 0)))


def _step(inp, target, mom_m, mom_v):
    _LAST_CALL.clear()
    x0 = inp["x"][0]
    mem = inp["mem"][0]
    T, D = x0.shape
    L = inp["w_in"].shape[0]
    CW = inp["conv_dw_b"].shape[1]
    SW = inp["ssm_d"].shape[1]
    DFF = inp["ffn_w_down"].shape[1] * N_DEV
    G = SW // SSM_GROUP
    NB = SW // LANE
    u_off = (2 * CW) // LANE
    gate_off = (2 * CW + SW) // 1024

    gathered = {n: [None] * L for n in _AG_ORDER}
    filter_rows = {"conv_dw_w": HALO, "ffn_dw_w": 8}
    for l in range(L):
        for n in _AG_ORDER:
            blk = inp[n][l:l + 1]
            blk = _pad_rows(blk, filter_rows[n]) if n in filter_rows else blk.astype(BF16)
            if n in _SHARDED_COL:
                gathered[n][l] = _allgather("ag_" + n, blk, False)
            else:
                full = _allgather("ag_" + n, blk, True)
                gathered[n][l] = full.reshape(1, N_DEV * full.shape[2], full.shape[3])

    def W(n, l):
        return _W(gathered[n][l], 0, n in _SHARDED_COL)

    def dw_filter(n, l):
        g = gathered[n][l]
        return jnp.transpose(g[:, 0], (1, 0, 2)).reshape(g.shape[2], N_DEV * g.shape[3])

    def row(n, l):
        return (inp[n], l)

    ssm_raw, ssm_p = [], []
    for l in range(L):
        a_re, a_im, ldt = inp["ssm_a_re"][l], inp["ssm_a_im"][l], inp["ssm_log_dt"][l][:, None]
        rep = lambda a: jnp.repeat(a, SSM_GROUP, axis=0)
        flat = lambda b: jnp.transpose(b, (0, 2, 1)).reshape(G * SSM_GROUP, SSM_STATE)
        raw = (a_re, a_im, ldt, rep(a_re), rep(a_im), rep(ldt), flat(inp["ssm_b_re"][l]), flat(inp["ssm_b_im"][l]))
        abar_re, abar_im, bbar_re, bbar_im = _ssm_prep("ssm_prep", raw)
        bbar_re = bbar_re.reshape(G, SSM_GROUP, SSM_STATE)
        bbar_im = bbar_im.reshape(G, SSM_GROUP, SSM_STATE)
        ssm_raw.append(raw)
        ssm_p.append(dict(
            bre=_block_diag(bbar_re).astype(BF16), bim=_block_diag(bbar_im).astype(BF16),
            cre=_block_diag(inp["ssm_c_re"][l]).astype(BF16), cim=_block_diag(inp["ssm_c_im"][l]).astype(BF16),
            ar=abar_re.reshape(NB, 1, SSM_LANES), ai=abar_im.reshape(NB, 1, SSM_LANES), d=inp["ssm_d"][l][None, :]))

    saved = []
    x = x0
    for l in range(L):
        s = {"x_in": x}
        s["h1"] = _rms_fwd("rms_mix", x, row("mix_norm_g", l))
        s["proj"] = _mm_nn("mm_w_in", s["h1"], W("w_in", l), F32)
        s["hc"] = _glu_conv_fwd("conv_fwd", s["proj"], dw_filter("conv_dw_w", l), CW)
        s["hs"] = _rowwise("conv_post", _convpost, [(s["hc"], CW, 0, 0)],
                           [row("conv_dw_b", l), row("conv_ln_g", l), row("conv_ln_b", l)], [(CW, BF16)])[0]
        s["ya"] = _mm_nn("mm_w_pw", s["hs"], W("conv_w_pw", l), F32)
        s["yg"], s["ck_r"], s["ck_i"] = _ssm_fwd("ssm_fwd", s["proj"], u_off, ssm_p[l], SW)
        s["gg"] = _mm_nn("mm_w_glu", s["yg"], W("ssm_w_glu", l), F32)
        nmix = D // 1024
        mix_rows = [(s["proj"], 1024, gate_off, 0), (s["proj"], 1024, gate_off + nmix, 0), (s["ya"], 1024, 0, 0),
                    (s["gg"], 1024, 0, 0), (s["gg"], 1024, nmix, 0)]
        s["mix_rows"] = mix_rows
        s["mix"] = _rowwise("mix_fwd", _mixf, mix_rows, [], [(D, BF16)], ncol=nmix)[0]
        x = _mm_nn("mm_w_out", s["mix"], W("w_out", l), F32, add=x)
        s["x1"] = x
        s["h2"] = _rms_fwd("rms_xa", x, row("xa_norm_g", l))
        s["q"] = _mm_nn("mm_w_q", s["h2"], W("xa_w_q", l), BF16)
        s["mn"] = _rms_fwd("rms_mem", mem, row("mem_norm_g", l))
        s["kv"] = _mm_nn("mm_w_kv", s["mn"], W("xa_w_kv", l), BF16)
        s["o"] = _attn_fwd("attn_fwd", s["q"], s["kv"], XA_HEADS)
        x = _mm_nn("mm_w_o", s["o"], W("xa_w_o", l), F32, add=x)
        s["x2"] = x
        s["h3"] = _rms_fwd("rms_ffn", x, row("ffn_norm_g", l))
        s["up"] = _mm_nn("mm_w_up", s["h3"], W("ffn_w_up", l), F32)
        s["act"] = _ffn_conv_fwd("ffn_conv_fwd", s["up"], dw_filter("ffn_dw_w", l), DFF)
        x = _mm_nn("mm_w_down", s["act"], W("ffn_w_down", l), F32, add=x)
        saved.append(s)

    loss_part, dx, dx16, d_final_g = _loss_head("loss_head", x, inp["final_norm_g"][None, :], target[0])

    big = _SHARDED_COL + _SHARDED_ROW
    small = {n: [None] * L for n in _WEIGHTS if n not in big and n != "final_norm_g"}
    pads = {"conv_dw_w": HALO, "ffn_dw_w": 8}
    shards = {n: tuple(_pad_rows(a, pads[n]) if n in pads else a for a in (inp[n], mom_m[n], mom_v[n])) for n in big}
    state, queue = {}, []
    names = [n for n in _WEIGHTS if n not in big]
    sizes = [inp[n].size for n in names]
    total = sum(sizes)
    pack_w = 8 * LANE
    rows_p = -(-total // (LANE * pack_w)) * LANE
    padn = rows_p * pack_w - total

    def pack(parts, fill):
        return jnp.concatenate([p.reshape(-1) for p in parts] + [jnp.full((padn,), fill, F32)]).reshape(rows_p, pack_w)

    def tick():
        for gen in list(queue):
            if next(gen, "done") == "done":
                queue.remove(gen)

    def emit(n, l, g):
        queue.append(_reduce_pipeline(n, l, g.astype(BF16), *shards[n], state))
        tick()

    deferred = []

    def emit_small(n, l, thunk):
        if l == 0:
            deferred.append((n, thunk))
        else:
            emit(n, l, thunk())

    for l in reversed(range(L)):
        s = saved[l]
        dact = _mm_nt("mm_w_down_t", dx16, W("ffn_w_down", l), BF16)
        emit("ffn_w_down", l, _mm_tn("mm_dw_down", s["act"], dx16).reshape(N_DEV, DFF // N_DEV, D))
        d_up, d_ffn_dw = _ffn_conv_bwd("ffn_conv_bwd", s["up"], dw_filter("ffn_dw_w", l), dact, DFF)
        emit("ffn_dw_w", l, jnp.transpose(d_ffn_dw.reshape(8, N_DEV, 2 * DFF // N_DEV), (1, 0, 2)))
        dh3 = _mm_nt("mm_w_up_t", d_up, W("ffn_w_up", l), BF16)
        emit("ffn_w_up", l, _mm_tn("mm_dw_up", s["h3"], d_up, nb=2 * DFF // N_DEV))
        dx, dx16, small["ffn_norm_g"][l] = _rms_bwd("rms_ffn_bwd", s["x2"], row("ffn_norm_g", l), dh3, dx)
        do = _mm_nt("mm_w_o_t", dx16, W("xa_w_o", l), BF16)
        emit_small("xa_w_o", l, lambda a=s["o"], b=dx16: _mm_tn("mm_dw_o", a, b).reshape(N_DEV, D // N_DEV, D))
        dq, dk, dv = _attn_bwd("attn_bwd", s["q"], s["kv"], do, XA_HEADS)
        dkv = jnp.concatenate([dk, dv], axis=1).astype(BF16)
        dh2 = _mm_nt("mm_w_q_t", dq, W("xa_w_q", l), BF16)
        emit_small("xa_w_q", l, lambda a=s["h2"], b=dq: _mm_tn("mm_dw_q", a, b).reshape(N_DEV, D // N_DEV, D))
        dmn = _mm_nt("mm_w_kv_t", dkv, W("xa_w_kv", l), BF16)
        emit("xa_w_kv", l, _mm_tn("mm_dw_kv", s["mn"], dkv, nb=2 * D // N_DEV))

        def mem_bwd(mv, dv_, gv):
            _, vjp = jax.vjp(_rms, mv, gv)
            return jnp.sum(vjp(dv_.astype(F32))[1], axis=0, keepdims=True)

        small["mem_norm_g"][l] = _rowwise("rms_mem_bwd", mem_bwd, [(mem, D, 0, 0), (dmn, D, 0, 0)],
                                          [row("mem_norm_g", l)], [], [(1, D)], tile=128)[0]
        dx, dx16, small["xa_norm_g"][l] = _rms_bwd("rms_xa_bwd", s["x1"], row("xa_norm_g", l), dh2, dx)
        dmix = _mm_nt("mm_w_out_t", dx16, W("w_out", l), BF16)
        emit_small("w_out", l, lambda a=s["mix"], b=dx16: _mm_tn("mm_dw_out", a, b).reshape(N_DEV, D // N_DEV, D))

        def mix_bwd(gla, glb, ya, ga, gb, dm):
            _, vjp = jax.vjp(_mixf, gla, glb, ya, ga, gb)
            return vjp(dm.astype(F32))

        nmix = D // 1024
        dgla, dglb, dya, dga, dgb = _rowwise("mix_bwd", mix_bwd, s["mix_rows"] + [(dmix, 1024, 0, 0)], [],
                                             [(D, BF16)] * 5, ncol=nmix)
        dgg = jnp.concatenate([dga, dgb], axis=1)
        dyg = _mm_nt("mm_w_glu_t", dgg, W("ssm_w_glu", l), BF16)
        emit_small("ssm_w_glu", l, lambda a=s["yg"], b=dgg: _mm_tn("mm_dw_glu", a, b, nb=2 * D // N_DEV))
        du, dbr, dbi, dcr, dci, dar, dai, dd = _ssm_bwd("ssm_bwd", s["proj"], u_off, ssm_p[l], s["ck_r"], s["ck_i"],
                                                        dyg, SW)
        cots = (dar[:, 0, :].reshape(G, SSM_STATE), dai[:, 0, :].reshape(G, SSM_STATE),
                _block_diag_t(dbr, SSM_GROUP, SSM_STATE).reshape(G * SSM_GROUP, SSM_STATE),
                _block_diag_t(dbi, SSM_GROUP, SSM_STATE).reshape(G * SSM_GROUP, SSM_STATE))
        g_are, g_aim, g_ldt, g_bre, g_bim = _ssm_prep_bwd("ssm_prep_bwd", ssm_raw[l], cots)
        small["ssm_a_re"][l], small["ssm_a_im"][l], small["ssm_log_dt"][l] = g_are, g_aim, g_ldt[:, 0]
        small["ssm_b_re"][l] = jnp.transpose(g_bre.reshape(G, SSM_GROUP, SSM_STATE), (0, 2, 1))
        small["ssm_b_im"][l] = jnp.transpose(g_bim.reshape(G, SSM_GROUP, SSM_STATE), (0, 2, 1))
        small["ssm_c_re"][l] = _block_diag_t(dcr, SSM_GROUP, SSM_STATE)
        small["ssm_c_im"][l] = _block_diag_t(dci, SSM_GROUP, SSM_STATE)
        small["ssm_d"][l] = dd[:, 0, :].reshape(SW)
        dhs = _mm_nt("mm_w_pw_t", dya, W("conv_w_pw", l), BF16)
        emit("conv_w_pw", l, _mm_tn("mm_dw_pw", s["hs"], dya, nb=D // N_DEV))

        def post_bwd(hc, dh, b, lg, lb):
            _, vjp = jax.vjp(_convpost, hc, b, lg, lb)
            dhc, db, dlg, dlb = vjp(dh.astype(F32))
            return dhc, jnp.sum(db, axis=0, keepdims=True), jnp.sum(dlg, axis=0, keepdims=True), \
                jnp.sum(dlb, axis=0, keepdims=True)

        dhc, small["conv_dw_b"][l], small["conv_ln_g"][l], small["conv_ln_b"][l] = _rowwise(
            "conv_post_bwd", post_bwd, [(s["hc"], CW, 0, 0), (dhs, CW, 0, 0)],
            [row("conv_dw_b", l), row("conv_ln_g", l), row("conv_ln_b", l)], [(CW, F32)], [(1, CW)] * 3)
        da, db, d_conv_dw = _glu_conv_bwd("conv_bwd", s["proj"], dw_filter("conv_dw_w", l), dhc, CW)
        emit("conv_dw_w", l, jnp.transpose(d_conv_dw.reshape(HALO, N_DEV, CW // N_DEV), (1, 0, 2)))
        dproj = jnp.concatenate([da, db, du, dgla, dglb], axis=1)
        dh1 = _mm_nt("mm_w_in_t", dproj, W("w_in", l), BF16)
        dx, dx16, small["mix_norm_g"][l] = _rms_bwd("rms_mix_bwd", s["x_in"], row("mix_norm_g", l), dh1, dx)
        if l == 0:
            flat_g = jnp.concatenate([(jnp.stack(small[n]) if n != "final_norm_g" else d_final_g).reshape(-1)
                                      for n in names])
            g_all = _allgather("ag_small_grads", pack([flat_g], 0.0)[None], False)
        emit("w_in", l, _mm_tn("mm_dw_in", s["h1"], dproj, nb=dproj.shape[1] // N_DEV))
    for n, thunk in deferred:
        emit(n, 0, thunk())

    while queue:
        tick()
    results = {n: [o[:, :inp[n].shape[1], :] for o in state[n]] if n in pads else state[n] for n in big}
    def sum8(*parts):
        tot = parts[0]
        for part in parts[1:]:
            tot = tot + part
        return tot

    g_sum = _rowwise("small_grad_sum", sum8, [(g_all.reshape(N_DEV * rows_p, pack_w), pack_w, 0, k * rows_p)
                                              for k in range(N_DEV)], [], [(pack_w, F32)], tile=LANE, nrows=rows_p)[0]
    g_flat, grads, offs = g_sum.reshape(-1), {}, 0
    for n, sz in zip(names, sizes):
        grads[n] = g_flat[offs:offs + sz].reshape(inp[n].shape)
        offs += sz
    sparse = [n for n in names if inp[n].ndim == 4 and inp[n].shape[-1] < SSM_STATE]
    dense = [n for n in names if n not in sparse]
    as2d = lambda a: a[None, :] if a.ndim == 1 else a
    for group, block in ((dense, None), (sparse, (1, 16))):
        outs = _adamw_natural("small_adamw", *[[as2d(src[n]) for n in group] for src in (grads, inp, mom_m, mom_v)],
                              lead_block=block)
        for k, n in enumerate(group):
            results[n] = [grads[n]] + [outs[j * len(group) + k].reshape(inp[n].shape) for j in range(3)]

    _LAST_CALL.clear()
    loss = lax.psum(loss_part, ("x", "y", "c"))
    grad_x = dx[None]
    return (loss, grad_x, *[results[n][0] for n in _WEIGHTS], *[results[n][1] for n in _WEIGHTS],
            *[results[n][2] for n in _WEIGHTS], *[results[n][3] for n in _WEIGHTS])


def kernel(x, mem, mix_norm_g, w_in, conv_dw_w, conv_dw_b, conv_ln_g, conv_ln_b, conv_w_pw, ssm_a_re, ssm_a_im, ssm_log_dt, ssm_b_re, ssm_b_im, ssm_c_re, ssm_c_im, ssm_d, ssm_w_glu, w_out, xa_norm_g, mem_norm_g, xa_w_q, xa_w_kv, xa_w_o, ffn_norm_g, ffn_w_up, ffn_dw_w, ffn_w_down, final_norm_g, loss_target, m_mix_norm_g, m_w_in, m_conv_dw_w, m_conv_dw_b, m_conv_ln_g, m_conv_ln_b, m_conv_w_pw, m_ssm_a_re, m_ssm_a_im, m_ssm_log_dt, m_ssm_b_re, m_ssm_b_im, m_ssm_c_re, m_ssm_c_im, m_ssm_d, m_ssm_w_glu, m_w_out, m_xa_norm_g, m_mem_norm_g, m_xa_w_q, m_xa_w_kv, m_xa_w_o, m_ffn_norm_g, m_ffn_w_up, m_ffn_dw_w, m_ffn_w_down, m_final_norm_g, v_mix_norm_g, v_w_in, v_conv_dw_w, v_conv_dw_b, v_conv_ln_g, v_conv_ln_b, v_conv_w_pw, v_ssm_a_re, v_ssm_a_im, v_ssm_log_dt, v_ssm_b_re, v_ssm_b_im, v_ssm_c_re, v_ssm_c_im, v_ssm_d, v_ssm_w_glu, v_w_out, v_xa_norm_g, v_mem_norm_g, v_xa_w_q, v_xa_w_kv, v_xa_w_o, v_ffn_norm_g, v_ffn_w_up, v_ffn_dw_w, v_ffn_w_down, v_final_norm_g):
    args = (x, mem, mix_norm_g, w_in, conv_dw_w, conv_dw_b, conv_ln_g, conv_ln_b, conv_w_pw, ssm_a_re, ssm_a_im, ssm_log_dt, ssm_b_re, ssm_b_im, ssm_c_re, ssm_c_im, ssm_d, ssm_w_glu, w_out, xa_norm_g, mem_norm_g, xa_w_q, xa_w_kv, xa_w_o, ffn_norm_g, ffn_w_up, ffn_dw_w, ffn_w_down, final_norm_g)
    ms = (m_mix_norm_g, m_w_in, m_conv_dw_w, m_conv_dw_b, m_conv_ln_g, m_conv_ln_b, m_conv_w_pw, m_ssm_a_re, m_ssm_a_im, m_ssm_log_dt, m_ssm_b_re, m_ssm_b_im, m_ssm_c_re, m_ssm_c_im, m_ssm_d, m_ssm_w_glu, m_w_out, m_xa_norm_g, m_mem_norm_g, m_xa_w_q, m_xa_w_kv, m_xa_w_o, m_ffn_norm_g, m_ffn_w_up, m_ffn_dw_w, m_ffn_w_down, m_final_norm_g)
    vs = (v_mix_norm_g, v_w_in, v_conv_dw_w, v_conv_dw_b, v_conv_ln_g, v_conv_ln_b, v_conv_w_pw, v_ssm_a_re, v_ssm_a_im, v_ssm_log_dt, v_ssm_b_re, v_ssm_b_im, v_ssm_c_re, v_ssm_c_im, v_ssm_d, v_ssm_w_glu, v_w_out, v_xa_norm_g, v_mem_norm_g, v_xa_w_q, v_xa_w_kv, v_xa_w_o, v_ffn_norm_g, v_ffn_w_up, v_ffn_dw_w, v_ffn_w_down, v_final_norm_g)
    return _step(dict(zip(_FWD, args)), loss_target, dict(zip(_WEIGHTS, ms)), dict(zip(_WEIGHTS, vs)))
```

```python
import functools

import jax
import jax.numpy as jnp
from jax import lax
from jax.experimental import pallas as pl
from jax.experimental.pallas import tpu as pltpu
from jax.experimental.pallas import tpu_sc as plsc

F32 = jnp.float32
BF16 = jnp.bfloat16
MESH_ID = pl.DeviceIdType.MESH
N_DEV = 8
EPS = 1e-6
VMEM_LIMIT = 48 * 1024 * 1024
ANY = pl.BlockSpec(memory_space=pl.ANY)

ADAM_LR = 0.001
ADAM_B1 = 0.9
ADAM_B2 = 0.999
ADAM_EPS = 1e-08
ADAM_WD = 0.01
ADAM_STEP = 10

CONV_K = 31
FFN_K = 3
XA_HEADS = 4
SSM_GROUP = 16
SSM_STATE = 64
HALO = 32
LANE = 128
SSM_LANES = 512
AG_COLLECTIVE_ID = 1
RS_SIBLING_COLLECTIVE_ID = 2
RS_CHIPS_COLLECTIVE_ID = 3


def _pick(n, prefs):
    for p in prefs:
        if p <= n and n % p == 0:
            return p
    return n


def _params(sem, vmem=VMEM_LIMIT):
    return pltpu.CompilerParams(dimension_semantics=sem, vmem_limit_bytes=vmem)


_LAST_CALL = []


def _tc_call(*call_args, **call_kwargs):
    call = pl.pallas_call(*call_args, **call_kwargs)

    def run(*args):
        args = list(args)
        if _LAST_CALL:
            i = next(k for k, a in enumerate(args) if a.ndim >= 2)
            args[i] = lax.optimization_barrier((args[i], _LAST_CALL[0]))[0]
        out = call(*args)
        _LAST_CALL[:] = [out[0] if isinstance(out, (tuple, list)) else out]
        return out

    return run


def _sigmoid(x):
    return 1.0 / (1.0 + jnp.exp(-x))


def _silu(x):
    return x * _sigmoid(x)


def _gelu(x):
    return 0.5 * x * (1.0 + jnp.tanh(0.7978845608028654 * (x + 0.044715 * (x * x * x))))


def _rms(x, g):
    return x * lax.rsqrt(jnp.mean(x * x, axis=-1, keepdims=True) + EPS) * g


def _convpost(hc, bias, ln_g, ln_b):
    h = hc + bias
    mu = jnp.mean(h, axis=-1, keepdims=True)
    xc = h - mu
    y = xc * lax.rsqrt(jnp.mean(xc * xc, axis=-1, keepdims=True) + EPS)
    return _silu(y * ln_g + ln_b)


def _mixf(gla, glb, ya, ga, gb):
    return _sigmoid(gla) * ya + _sigmoid(glb) * (ga * _sigmoid(gb))


class _W:
    def __init__(self, arr, layer, blocked):
        self.arr, self.layer, self.blocked = arr, layer, blocked
        if blocked:
            _, _, self.K, self.nb = arr.shape
            self.N = N_DEV * self.nb
        else:
            _, self.K, self.N = arr.shape
            self.nb = self.N

    def spec(self, tk, tn, ki, ni):
        l = self.layer
        if self.blocked:
            per = self.nb // tn
            return pl.BlockSpec((None, None, tk, tn), lambda *g: (ni(*g) // per, l, ki(*g), ni(*g) % per))
        return pl.BlockSpec((None, tk, tn), lambda *g: (l, ki(*g), ni(*g)))


_M_TILES = (1024, 512, 256, 128, 64, 32, 16, 8)
_N_TILES = (1408, 1024, 896, 512, 256, 128)
_K_TILES = (512, 1408, 896, 256, 128)
MAX_FULL_K = 2048
_K_FULL_TILES = (2048, 1408, 1024, 896, 512, 256, 128)
MAX_FULL_T = 4096


def _mm_nn(name, a, w, out_dtype, add=None):
    M, K = a.shape
    assert K == w.K
    tm, tn = _pick(M, _M_TILES), _pick(w.nb, _N_TILES)
    tk = K if K <= MAX_FULL_K else _pick(K, _K_FULL_TILES)
    nk = K // tk

    def body(*refs):
        a_ref, w_ref = refs[:2]
        r_ref = refs[2] if add is not None else None
        o_ref = refs[3] if add is not None else refs[2]
        part = jnp.dot(a_ref[...].astype(BF16), w_ref[...], preferred_element_type=F32)
        if nk == 1:
            o_ref[...] = (part if add is None else part + r_ref[...]).astype(o_ref.dtype)
            return
        acc = refs[-1]
        k = pl.program_id(2)

        @pl.when(k == 0)
        def _():
            acc[...] = part

        @pl.when(k > 0)
        def _():
            acc[...] += part

        @pl.when(k == nk - 1)
        def _():
            res = acc[...]
            if add is not None:
                res = res + r_ref[...]
            o_ref[...] = res.astype(o_ref.dtype)

    in_specs = [pl.BlockSpec((tm, tk), lambda i, j, k: (i, k)),
                w.spec(tk, tn, lambda i, j, k: k, lambda i, j, k: j)]
    args = [a, w.arr]
    if add is not None:
        in_specs.append(pl.BlockSpec((tm, tn), lambda i, j, k: (i, j)))
        args.append(add)
    return _tc_call(
        body, name=name, grid=(M // tm, w.N // tn, nk), in_specs=in_specs,
        out_specs=pl.BlockSpec((tm, tn), lambda i, j, k: (i, j)),
        out_shape=jax.ShapeDtypeStruct((M, w.N), out_dtype),
        scratch_shapes=[] if nk == 1 else [pltpu.VMEM((tm, tn), F32)],
        compiler_params=_params(("parallel", "parallel", "arbitrary")),
    )(*args)


def _mm_nt(name, a, w, out_dtype):
    M, N = a.shape
    assert N == w.N
    tm, tkk = _pick(M, _M_TILES), _pick(w.K, _N_TILES)
    tnn = w.nb if w.nb <= MAX_FULL_K else _pick(w.nb, _K_FULL_TILES)
    nn = N // tnn

    def body(a_ref, w_ref, o_ref, *scratch):
        part = lax.dot_general(a_ref[...].astype(BF16), w_ref[...], (((1,), (1,)), ((), ())),
                               preferred_element_type=F32)
        if nn == 1:
            o_ref[...] = part.astype(o_ref.dtype)
            return
        acc = scratch[0]
        n = pl.program_id(2)

        @pl.when(n == 0)
        def _():
            acc[...] = part

        @pl.when(n > 0)
        def _():
            acc[...] += part

        @pl.when(n == nn - 1)
        def _():
            o_ref[...] = acc[...].astype(o_ref.dtype)

    return _tc_call(
        body, name=name, grid=(M // tm, w.K // tkk, nn),
        in_specs=[pl.BlockSpec((tm, tnn), lambda i, j, n: (i, n)),
                  w.spec(tkk, tnn, lambda i, j, n: j, lambda i, j, n: n)],
        out_specs=pl.BlockSpec((tm, tkk), lambda i, j, n: (i, j)),
        out_shape=jax.ShapeDtypeStruct((M, w.K), out_dtype),
        scratch_shapes=[] if nn == 1 else [pltpu.VMEM((tm, tkk), F32)],
        compiler_params=_params(("parallel", "parallel", "arbitrary")),
    )(a, w.arr)


def _mm_tn(name, a, b, nb=None):
    T, K = a.shape
    _, N = b.shape
    width = N if nb is None else nb
    assert T <= MAX_FULL_T
    tkk, tn = _pick(K, (512, 256, 128)), _pick(width, _N_TILES)
    per = width // tn

    def body(a_ref, b_ref, o_ref):
        o_ref[...] = lax.dot_general(a_ref[...].astype(BF16), b_ref[...].astype(BF16), (((0,), (0,)), ((), ())),
                                     preferred_element_type=F32).astype(o_ref.dtype)

    if nb is None:
        out_spec = pl.BlockSpec((tkk, tn), lambda j, i: (i, j))
        out_shape = jax.ShapeDtypeStruct((K, N), BF16)
    else:
        out_spec = pl.BlockSpec((None, tkk, tn), lambda j, i: (j // per, i, j % per))
        out_shape = jax.ShapeDtypeStruct((N_DEV, K, nb), BF16)
    return _tc_call(
        body, name=name, grid=(N // tn, K // tkk),
        in_specs=[pl.BlockSpec((T, tkk), lambda j, i: (0, i)),
                  pl.BlockSpec((T, tn), lambda j, i: (0, j))],
        out_specs=out_spec, out_shape=out_shape,
        compiler_params=_params(("parallel", "parallel")),
    )(a, b)


def _rowwise(name, fn, rows, consts, outs, accs=(), tile=256, ncol=1, nrows=None):
    n_r, n_c, n_o = len(rows), len(consts), len(outs)
    T = rows[0][0].shape[0] if nrows is None else nrows
    tile = _pick(T, tuple(t for t in (512, 256, 128, 64, 32, 16, 8) if t <= tile))
    nt = T // tile
    const_rows = [c[1] if isinstance(c, tuple) else None for c in consts]
    consts = [c[0] if isinstance(c, tuple) else c for c in consts]

    def body(*refs):
        vals = [r[...] for r in refs[:n_r]]
        vals += [r[...] if l is None else r[l:l + 1, :] for r, l in zip(refs[n_r:n_r + n_c], const_rows)]
        res = fn(*vals)
        if not isinstance(res, (tuple, list)):
            res = (res,)
        o_refs = refs[n_r + n_c:n_r + n_c + n_o]
        a_refs = refs[n_r + n_c + n_o:]
        for r, v in zip(o_refs, res[:n_o]):
            r[...] = v.astype(r.dtype)
        first = pl.program_id(1) == 0
        for r, v in zip(a_refs, res[n_o:]):
            @pl.when(first)
            def _(r=r, v=v):
                r[...] = v.astype(F32)

            @pl.when(jnp.logical_not(first))
            def _(r=r, v=v):
                r[...] += v.astype(F32)

    in_specs, args = [], []
    for arr, w, off, roff in rows:
        rb = roff // tile
        assert roff % tile == 0
        in_specs.append(pl.BlockSpec((tile, w), lambda j, i, off=off, rb=rb: (i + rb, off + j)))
        args.append(arr)
    for cst in consts:
        in_specs.append(pl.BlockSpec(cst.shape, lambda j, i: (0, 0)))
        args.append(cst)
    out_specs, out_shape = [], []
    for tw, dt in outs:
        out_specs.append(pl.BlockSpec((tile, tw // ncol), lambda j, i: (i, j)))
        out_shape.append(jax.ShapeDtypeStruct((T, tw), dt))
    for nr, tw in accs:
        out_specs.append(pl.BlockSpec((nr, tw // ncol), lambda j, i: (0, j)))
        out_shape.append(jax.ShapeDtypeStruct((nr, tw), F32))
    res = _tc_call(
        body, name=name, grid=(ncol, nt), in_specs=in_specs, out_specs=out_specs, out_shape=out_shape,
        compiler_params=_params(("parallel", "arbitrary")),
    )(*args)
    return res


def _rms_fwd(name, x, g):
    D = x.shape[1]
    return _rowwise(name, lambda xv, gv: _rms(xv, gv), [(x, D, 0, 0)], [g], [(D, BF16)])[0]


def _rms_bwd(name, x, g, dh, dx_in):
    D = x.shape[1]

    def fn(xv, dhv, dxv, gv):
        _, vjp = jax.vjp(_rms, xv, gv)
        dx, dg = vjp(dhv.astype(F32))
        tot = dx + dxv
        return tot, tot, jnp.sum(dg, axis=0, keepdims=True)

    return _rowwise(name, fn, [(x, D, 0, 0), (dh, D, 0, 0), (dx_in, D, 0, 0)], [g], [(D, F32), (D, BF16)], [(1, D)],
                    tile=256)


def _lag_views(win, K, R, forward):
    n = win.shape[0]
    for r in range(8):
        if r >= K:
            break
        if r == 0:
            rolled = win
        else:
            rolled = pltpu.roll(win, (n - r) if forward else r, axis=0)
        for q in range((K - 1 - r) // 8 + 1):
            s = 8 * q + r
            if forward:
                yield s, rolled[8 * q:8 * q + R]
            else:
                yield s, rolled[HALO - 8 * q:HALO - 8 * q + R]


def _conv_chunk(win, w_ref, K, R):
    acc = None
    for s, view in _lag_views(win, K, R, forward=False):
        term = w_ref[K - 1 - s:K - s, :] * view
        acc = term if acc is None else acc + term
    return acc


def _conv_chunk_t(win, w_ref, K, R):
    acc = None
    for s, view in _lag_views(win, K, R, forward=True):
        term = w_ref[K - 1 - s:K - s, :] * view
        acc = term if acc is None else acc + term
    return acc


def _conv_dw(xwin, dy, K, R):
    taps = [None] * K
    for s, view in _lag_views(xwin, K, R, forward=False):
        taps[K - 1 - s] = jnp.sum(dy * view, axis=0, keepdims=True)
    return taps


def _chunks(T):
    R = _pick(T, (128, 64, 32))
    return R, T // R


def _glu_conv_fwd(name, proj, w, cw_total):
    T = proj.shape[0]
    C = cw_total
    cw = LANE
    nb = C // cw
    R, nch = _chunks(T)

    def body(a_ref, b_ref, w_ref, o_ref, s_ref):
        s_ref[0:HALO, :] = jnp.zeros((HALO, cw), F32)

        def fill(i, _):
            r0 = pl.multiple_of(i * R, R)
            s_ref[pl.ds(HALO + r0, R), :] = a_ref[pl.ds(r0, R), :] * _sigmoid(b_ref[pl.ds(r0, R), :])
            return 0

        lax.fori_loop(0, nch, fill, 0)

        def conv(i, _):
            r0 = pl.multiple_of(i * R, R)
            o_ref[pl.ds(r0, R), :] = _conv_chunk(s_ref[pl.ds(r0, R + HALO), :], w_ref, CONV_K, R)
            return 0

        lax.fori_loop(0, nch, conv, 0)

    return _tc_call(
        body, name=name, grid=(nb,),
        in_specs=[pl.BlockSpec((T, cw), lambda j: (0, j)), pl.BlockSpec((T, cw), lambda j: (0, nb + j)),
                  pl.BlockSpec((HALO, cw), lambda j: (0, j))],
        out_specs=pl.BlockSpec((T, cw), lambda j: (0, j)),
        out_shape=jax.ShapeDtypeStruct((T, C), F32),
        scratch_shapes=[pltpu.VMEM((T + HALO, cw), F32)],
        compiler_params=_params(("parallel",)),
    )(proj, proj, w)


def _glu_conv_bwd(name, proj, w, dhc, cw_total):
    T = proj.shape[0]
    C = cw_total
    cw = LANE
    nb = C // cw
    R, nch = _chunks(T)

    def body(a_ref, b_ref, w_ref, dy_ref, da_ref, db_ref, dw_ref, s_ref, g_ref, acc_ref):
        s_ref[0:HALO, :] = jnp.zeros((HALO, cw), F32)
        g_ref[T:T + HALO, :] = jnp.zeros((HALO, cw), F32)
        acc_ref[...] = jnp.zeros_like(acc_ref)

        def fill(i, _):
            r0 = pl.multiple_of(i * R, R)
            s_ref[pl.ds(HALO + r0, R), :] = a_ref[pl.ds(r0, R), :] * _sigmoid(b_ref[pl.ds(r0, R), :])
            g_ref[pl.ds(r0, R), :] = dy_ref[pl.ds(r0, R), :]
            return 0

        lax.fori_loop(0, nch, fill, 0)

        def back(i, _):
            r0 = pl.multiple_of(i * R, R)
            dhg = _conv_chunk_t(g_ref[pl.ds(r0, R + HALO), :], w_ref, CONV_K, R)
            av = a_ref[pl.ds(r0, R), :]
            sg = _sigmoid(b_ref[pl.ds(r0, R), :])
            da_ref[pl.ds(r0, R), :] = (dhg * sg).astype(da_ref.dtype)
            db_ref[pl.ds(r0, R), :] = (dhg * av * sg * (1.0 - sg)).astype(db_ref.dtype)
            taps = _conv_dw(s_ref[pl.ds(r0, R + HALO), :], dy_ref[pl.ds(r0, R), :], CONV_K, R)
            for k, tap in enumerate(taps):
                acc_ref[k:k + 1, :] += tap
            return 0

        lax.fori_loop(0, nch, back, 0)
        dw_ref[...] = acc_ref[...]

    return _tc_call(
        body, name=name, grid=(nb,),
        in_specs=[pl.BlockSpec((T, cw), lambda j: (0, j)), pl.BlockSpec((T, cw), lambda j: (0, nb + j)),
                  pl.BlockSpec((HALO, cw), lambda j: (0, j)), pl.BlockSpec((T, cw), lambda j: (0, j))],
        out_specs=[pl.BlockSpec((T, cw), lambda j: (0, j)), pl.BlockSpec((T, cw), lambda j: (0, j)),
                   pl.BlockSpec((HALO, cw), lambda j: (0, j))],
        out_shape=[jax.ShapeDtypeStruct((T, C), BF16), jax.ShapeDtypeStruct((T, C), BF16),
                   jax.ShapeDtypeStruct((HALO, C), F32)],
        scratch_shapes=[pltpu.VMEM((T + HALO, cw), F32), pltpu.VMEM((T + HALO, cw), F32),
                        pltpu.VMEM((HALO, cw), F32)],
        compiler_params=_params(("parallel",)),
    )(proj, proj, w, dhc)


def _ffn_conv_fwd(name, up, w, dff):
    T = up.shape[0]
    cw = LANE
    nb = dff // cw
    R, nch = _chunks(T)

    def body(g_ref, v_ref, wg_ref, wv_ref, o_ref, sg_ref, sv_ref):
        sg_ref[0:HALO, :] = jnp.zeros((HALO, cw), F32)
        sv_ref[0:HALO, :] = jnp.zeros((HALO, cw), F32)

        def fill(i, _):
            r0 = pl.multiple_of(i * R, R)
            sg_ref[pl.ds(HALO + r0, R), :] = g_ref[pl.ds(r0, R), :]
            sv_ref[pl.ds(HALO + r0, R), :] = v_ref[pl.ds(r0, R), :]
            return 0

        lax.fori_loop(0, nch, fill, 0)

        def conv(i, _):
            r0 = pl.multiple_of(i * R, R)
            gc = _conv_chunk(sg_ref[pl.ds(r0, R + HALO), :], wg_ref, FFN_K, R)
            vc = _conv_chunk(sv_ref[pl.ds(r0, R + HALO), :], wv_ref, FFN_K, R)
            o_ref[pl.ds(r0, R), :] = (_silu(gc) * vc).astype(o_ref.dtype)
            return 0

        lax.fori_loop(0, nch, conv, 0)

    return _tc_call(
        body, name=name, grid=(nb,),
        in_specs=[pl.BlockSpec((T, cw), lambda j: (0, j)), pl.BlockSpec((T, cw), lambda j: (0, nb + j)),
                  pl.BlockSpec((8, cw), lambda j: (0, j)), pl.BlockSpec((8, cw), lambda j: (0, nb + j))],
        out_specs=pl.BlockSpec((T, cw), lambda j: (0, j)),
        out_shape=jax.ShapeDtypeStruct((T, dff), BF16),
        scratch_shapes=[pltpu.VMEM((T + HALO, cw), F32), pltpu.VMEM((T + HALO, cw), F32)],
        compiler_params=_params(("parallel",)),
    )(up, up, w, w)


def _ffn_conv_bwd(name, up, w, dact, dff):
    T = up.shape[0]
    cw = LANE
    nb = dff // cw
    R, nch = _chunks(T)

    def body(g_ref, v_ref, wg_ref, wv_ref, da_ref, dg_ref, dv_ref, dwg_ref, dwv_ref,
             sg_ref, sv_ref, tg_ref, tv_ref, ag_ref, av_ref):
        zero = jnp.zeros((HALO, cw), F32)
        sg_ref[0:HALO, :] = zero
        sv_ref[0:HALO, :] = zero
        tg_ref[T:T + HALO, :] = zero
        tv_ref[T:T + HALO, :] = zero
        ag_ref[...] = jnp.zeros_like(ag_ref)
        av_ref[...] = jnp.zeros_like(av_ref)

        def fill(i, _):
            r0 = pl.multiple_of(i * R, R)
            sg_ref[pl.ds(HALO + r0, R), :] = g_ref[pl.ds(r0, R), :]
            sv_ref[pl.ds(HALO + r0, R), :] = v_ref[pl.ds(r0, R), :]
            return 0

        lax.fori_loop(0, nch, fill, 0)

        def grads(i, _):
            r0 = pl.multiple_of(i * R, R)
            gwin = sg_ref[pl.ds(r0, R + HALO), :]
            vwin = sv_ref[pl.ds(r0, R + HALO), :]
            gc = _conv_chunk(gwin, wg_ref, FFN_K, R)
            vc = _conv_chunk(vwin, wv_ref, FFN_K, R)
            da = da_ref[pl.ds(r0, R), :].astype(F32)
            sg = _sigmoid(gc)
            dgc = da * vc * (sg * (1.0 + gc * (1.0 - sg)))
            dvc = da * (gc * sg)
            tg_ref[pl.ds(r0, R), :] = dgc
            tv_ref[pl.ds(r0, R), :] = dvc
            for k, tap in enumerate(_conv_dw(gwin, dgc, FFN_K, R)):
                ag_ref[k:k + 1, :] += tap
            for k, tap in enumerate(_conv_dw(vwin, dvc, FFN_K, R)):
                av_ref[k:k + 1, :] += tap
            return 0

        lax.fori_loop(0, nch, grads, 0)

        def back(i, _):
            r0 = pl.multiple_of(i * R, R)
            dg_ref[pl.ds(r0, R), :] = _conv_chunk_t(tg_ref[pl.ds(r0, R + HALO), :], wg_ref, FFN_K, R).astype(dg_ref.dtype)
            dv_ref[pl.ds(r0, R), :] = _conv_chunk_t(tv_ref[pl.ds(r0, R + HALO), :], wv_ref, FFN_K, R).astype(dv_ref.dtype)
            return 0

        lax.fori_loop(0, nch, back, 0)
        dwg_ref[...] = ag_ref[...]
        dwv_ref[...] = av_ref[...]

    col = lambda j: (0, j)
    dg, dv, dwg, dwv = _tc_call(
        body, name=name, grid=(nb,),
        in_specs=[pl.BlockSpec((T, cw), col), pl.BlockSpec((T, cw), lambda j: (0, nb + j)),
                  pl.BlockSpec((8, cw), col), pl.BlockSpec((8, cw), lambda j: (0, nb + j)),
                  pl.BlockSpec((T, cw), col)],
        out_specs=[pl.BlockSpec((T, cw), col), pl.BlockSpec((T, cw), col),
                   pl.BlockSpec((8, cw), col), pl.BlockSpec((8, cw), col)],
        out_shape=[jax.ShapeDtypeStruct((T, dff), BF16), jax.ShapeDtypeStruct((T, dff), BF16),
                   jax.ShapeDtypeStruct((8, dff), F32), jax.ShapeDtypeStruct((8, dff), F32)],
        scratch_shapes=[pltpu.VMEM((T + HALO, cw), F32), pltpu.VMEM((T + HALO, cw), F32),
                        pltpu.VMEM((T + HALO, cw), F32), pltpu.VMEM((T + HALO, cw), F32),
                        pltpu.VMEM((8, cw), F32), pltpu.VMEM((8, cw), F32)],
        compiler_params=_params(("parallel",)),
    )(up, up, w, w, dact)
    return jnp.concatenate([dg, dv], axis=1), jnp.concatenate([dwg, dwv], axis=1)


def _zoh(a_re, a_im, log_dt):
    ar = jnp.minimum(a_re, -1e-4)
    ai = a_im
    dt = jnp.exp(log_dt)
    mag = jnp.exp(dt * ar)
    abar_re = mag * jnp.cos(dt * ai)
    abar_im = mag * jnp.sin(dt * ai)
    den = ar * ar + ai * ai
    nr = abar_re - 1.0
    ni = abar_im
    return abar_re, abar_im, (nr * ar + ni * ai) / den, (ni * ar - nr * ai) / den


def _discretize(a_re, a_im, log_dt, a_re_h, a_im_h, log_dt_h, b_re, b_im):
    abar_re, abar_im, _, _ = _zoh(a_re, a_im, log_dt)
    _, _, z_re, z_im = _zoh(a_re_h, a_im_h, log_dt_h)
    return abar_re, abar_im, z_re * b_re - z_im * b_im, z_re * b_im + z_im * b_re


def _full_specs(arrs):
    return [pl.BlockSpec(a.shape, lambda *_, n=len(a.shape): (0,) * n) for a in arrs]


def _ssm_prep(name, raw):
    def body(*refs):
        res = _discretize(*[r[...] for r in refs[:8]])
        for r, v in zip(refs[8:], res):
            r[...] = v

    outs = [jax.ShapeDtypeStruct(raw[0].shape, F32)] * 2 + [jax.ShapeDtypeStruct(raw[6].shape, F32)] * 2
    return _tc_call(body, name=name, in_specs=_full_specs(raw), out_specs=_full_specs(outs), out_shape=outs)(*raw)


def _ssm_prep_bwd(name, raw, cots):
    G = raw[0].shape[0]
    H = raw[3].shape[0] // G

    def body(*refs):
        _, vjp = jax.vjp(_discretize, *[r[...] for r in refs[:8]])
        g = vjp(tuple(r[...] for r in refs[8:12]))
        outs = refs[12:]
        for k in range(3):
            rep = g[3 + k]
            outs[k][...] = g[k] + jnp.sum(rep.reshape(G, H, rep.shape[1]), axis=1)
        outs[3][...] = g[6]
        outs[4][...] = g[7]

    outs = [jax.ShapeDtypeStruct(a.shape, F32) for a in (raw[0], raw[1], raw[2], raw[6], raw[7])]
    return _tc_call(body, name=name, in_specs=_full_specs(list(raw) + list(cots)), out_specs=_full_specs(outs),
                          out_shape=outs)(*raw, *cots)


def _cmul(ar, ai, br, bi):
    return ar * br - ai * bi, ar * bi + ai * br


def _scan_coefs(ar, ai, reverse):
    W = ar.shape[1]
    row = lax.broadcasted_iota(jnp.int32, (8, W), 0)
    p = [None] * 9
    p[1] = (ar, ai)
    for n in range(2, 9):
        p[n] = _cmul(*p[n // 2], *p[n - n // 2])
    steps = []
    for s in (1, 2, 4):
        valid = (row <= 7 - s) if reverse else (row >= s)
        steps.append((jnp.where(valid, p[s][0], 0.0), jnp.where(valid, p[s][1], 0.0)))
    pr = jnp.zeros((8, W), F32)
    pi = jnp.zeros((8, W), F32)
    for i in range(8):
        n = (8 - i) if reverse else (i + 1)
        pr = jnp.where(row == i, p[n][0], pr)
        pi = jnp.where(row == i, p[n][1], pi)
    return steps, (pr, pi)


def _scan_tile(xr, xi, cr, ci, coefs, reverse):
    steps, (pr, pi) = coefs
    for s, (sr, si) in zip((1, 2, 4), steps):
        shift = (8 - s) if reverse else s
        rr = pltpu.roll(xr, shift, axis=0)
        ri = pltpu.roll(xi, shift, axis=0)
        xr, xi = xr + sr * rr - si * ri, xi + sr * ri + si * rr
    xr, xi = xr + pr * cr - pi * ci, xi + pr * ci + pi * cr
    return xr, xi


def _edge_rows(x, reverse):
    W = x.shape[1]
    return jnp.broadcast_to(x[0:1, :] if reverse else x[7:8, :], (8, W))


def _ssm_chunk(T):
    return _pick(T, (512, 256, 128, 64))


def _ssm_fwd(name, proj, u_off, p, width):
    T = proj.shape[0]
    NB = width // LANE
    Q = _ssm_chunk(T)
    nch = T // Q
    W = SSM_LANES

    def body(u_ref, bre, bim, cre, cim, ar_ref, ai_ref, d_ref, y_ref, ckr_ref, cki_ref, br_s, bi_s, car_r, car_i):
        c = pl.program_id(1)

        @pl.when(c == 0)
        def _():
            car_r[...] = jnp.zeros_like(car_r)
            car_i[...] = jnp.zeros_like(car_i)

        ckr_ref[...] = car_r[...]
        cki_ref[...] = car_i[...]
        u = u_ref[...]
        u16 = u.astype(BF16)
        br_s[...] = jnp.dot(u16, bre[...], preferred_element_type=F32)
        bi_s[...] = jnp.dot(u16, bim[...], preferred_element_type=F32)
        coefs = _scan_coefs(ar_ref[...], ai_ref[...], False)

        def tile(j, carry):
            r0 = pl.multiple_of(j * 8, 8)
            xr, xi = _scan_tile(br_s[pl.ds(r0, 8), :], bi_s[pl.ds(r0, 8), :], carry[0], carry[1], coefs, False)
            br_s[pl.ds(r0, 8), :] = xr
            bi_s[pl.ds(r0, 8), :] = xi
            return _edge_rows(xr, False), _edge_rows(xi, False)

        cr, ci = lax.fori_loop(0, Q // 8, tile, (car_r[...], car_i[...]))
        car_r[...] = cr
        car_i[...] = ci
        nt = (((1,), (1,)), ((), ()))
        y = (lax.dot_general(br_s[...].astype(BF16), cre[...], nt, preferred_element_type=F32)
             - lax.dot_general(bi_s[...].astype(BF16), cim[...], nt, preferred_element_type=F32)
             + d_ref[...] * u)
        y_ref[...] = _gelu(y).astype(y_ref.dtype)

    blk = lambda b, c: (b, 0, 0)
    mat = pl.BlockSpec((None, LANE, W), blk)
    vec = pl.BlockSpec((None, 1, W), blk)
    ck = pl.BlockSpec((None, None, 8, W), lambda b, c: (b, c, 0, 0))
    return _tc_call(
        body, name=name, grid=(NB, nch),
        in_specs=[pl.BlockSpec((Q, LANE), lambda b, c: (c, u_off + b)), mat, mat, mat, mat, vec, vec,
                  pl.BlockSpec((1, LANE), lambda b, c: (0, b))],
        out_specs=[pl.BlockSpec((Q, LANE), lambda b, c: (c, b)), ck, ck],
        out_shape=[jax.ShapeDtypeStruct((T, width), BF16), jax.ShapeDtypeStruct((NB, nch, 8, W), F32),
                   jax.ShapeDtypeStruct((NB, nch, 8, W), F32)],
        scratch_shapes=[pltpu.VMEM((Q, W), F32), pltpu.VMEM((Q, W), F32), pltpu.VMEM((8, W), F32),
                        pltpu.VMEM((8, W), F32)],
        compiler_params=_params(("parallel", "arbitrary")),
    )(proj, p["bre"], p["bim"], p["cre"], p["cim"], p["ar"], p["ai"], p["d"])


def _ssm_bwd(name, proj, u_off, p, ck_r, ck_i, dyg, width):
    T = proj.shape[0]
    NB = width // LANE
    Q = _ssm_chunk(T)
    nch = T // Q
    W = SSM_LANES
    nt_dims = (((1,), (1,)), ((), ()))
    tn_dims = (((0,), (0,)), ((), ()))

    def body(u_ref, dy_ref, ckr_ref, cki_ref, bre, bim, cre, cim, ar_ref, ai_ref, d_ref,
             du_ref, dbr_ref, dbi_ref, dcr_ref, dci_ref, dar_ref, dai_ref, dd_ref,
             xr_s, xi_s, lr_s, li_s, lam_r, lam_i):
        c = pl.program_id(1)

        @pl.when(c == 0)
        def _():
            lam_r[...] = jnp.zeros_like(lam_r)
            lam_i[...] = jnp.zeros_like(lam_i)
            for r in (dbr_ref, dbi_ref, dcr_ref, dci_ref, dar_ref, dai_ref, dd_ref):
                r[...] = jnp.zeros_like(r)

        u = u_ref[...]
        u16 = u.astype(BF16)
        ar, ai = ar_ref[...], ai_ref[...]
        xr_s[0:8, :] = ckr_ref[...]
        xi_s[0:8, :] = cki_ref[...]
        xr_s[8:Q + 8, :] = jnp.dot(u16, bre[...], preferred_element_type=F32)
        xi_s[8:Q + 8, :] = jnp.dot(u16, bim[...], preferred_element_type=F32)
        fcoefs = _scan_coefs(ar, ai, False)

        def ftile(j, carry):
            r0 = pl.multiple_of(j * 8 + 8, 8)
            xr, xi = _scan_tile(xr_s[pl.ds(r0, 8), :], xi_s[pl.ds(r0, 8), :], carry[0], carry[1], fcoefs, False)
            xr_s[pl.ds(r0, 8), :] = xr
            xi_s[pl.ds(r0, 8), :] = xi
            return _edge_rows(xr, False), _edge_rows(xi, False)

        lax.fori_loop(0, Q // 8, ftile, (ckr_ref[...], cki_ref[...]))
        xr16 = xr_s[8:Q + 8, :].astype(BF16)
        xi16 = xi_s[8:Q + 8, :].astype(BF16)
        y = (lax.dot_general(xr16, cre[...], nt_dims, preferred_element_type=F32)
             - lax.dot_general(xi16, cim[...], nt_dims, preferred_element_type=F32) + d_ref[...] * u)
        _, gelu_vjp = jax.vjp(_gelu, y)
        dy = gelu_vjp(dy_ref[...].astype(F32))[0]
        dy16 = dy.astype(BF16)
        dd_ref[...] += jnp.broadcast_to(jnp.sum(dy * u, axis=0, keepdims=True), (8, LANE))
        dcr_ref[...] += lax.dot_general(dy16, xr16, tn_dims, preferred_element_type=F32)
        dci_ref[...] -= lax.dot_general(dy16, xi16, tn_dims, preferred_element_type=F32)
        lr_s[...] = jnp.dot(dy16, cre[...], preferred_element_type=F32)
        li_s[...] = -jnp.dot(dy16, cim[...], preferred_element_type=F32)
        rcoefs = _scan_coefs(ar, -ai, True)
        row = lax.broadcasted_iota(jnp.int32, (8, W), 0)

        def rtile(jj, carry):
            j = Q // 8 - 1 - jj
            r0 = pl.multiple_of(j * 8, 8)
            lr, li = _scan_tile(lr_s[pl.ds(r0, 8), :], li_s[pl.ds(r0, 8), :], carry[0], carry[1], rcoefs, True)
            lr_s[pl.ds(r0, 8), :] = lr
            li_s[pl.ds(r0, 8), :] = li
            cur_r, cur_i = xr_s[pl.ds(r0 + 8, 8), :], xi_s[pl.ds(r0 + 8, 8), :]
            prv_r, prv_i = xr_s[pl.ds(r0, 8), :], xi_s[pl.ds(r0, 8), :]
            xpr = jnp.where(row == 0, _edge_rows(prv_r, False), pltpu.roll(cur_r, 1, axis=0))
            xpi = jnp.where(row == 0, _edge_rows(prv_i, False), pltpu.roll(cur_i, 1, axis=0))
            return (_edge_rows(lr, True), _edge_rows(li, True),
                    carry[2] + lr * xpr + li * xpi, carry[3] + li * xpr - lr * xpi)

        zero = jnp.zeros((8, W), F32)
        cr, ci, sar, sai = lax.fori_loop(0, Q // 8, rtile, (lam_r[...], lam_i[...], zero, zero))
        lam_r[...] = cr
        lam_i[...] = ci
        dar_ref[...] += jnp.broadcast_to(jnp.sum(sar, axis=0, keepdims=True), (8, W))
        dai_ref[...] += jnp.broadcast_to(jnp.sum(sai, axis=0, keepdims=True), (8, W))
        lr16 = lr_s[...].astype(BF16)
        li16 = li_s[...].astype(BF16)
        dbr_ref[...] += lax.dot_general(u16, lr16, tn_dims, preferred_element_type=F32)
        dbi_ref[...] += lax.dot_general(u16, li16, tn_dims, preferred_element_type=F32)
        du = (lax.dot_general(lr16, bre[...], nt_dims, preferred_element_type=F32)
              + lax.dot_general(li16, bim[...], nt_dims, preferred_element_type=F32) + d_ref[...] * dy)
        du_ref[...] = du.astype(du_ref.dtype)

    blk = lambda b, c: (b, 0, 0)
    mat = pl.BlockSpec((None, LANE, W), blk)
    vec = pl.BlockSpec((None, 1, W), blk)
    acc8 = pl.BlockSpec((None, 8, W), blk)
    ck = pl.BlockSpec((None, None, 8, W), lambda b, c: (b, nch - 1 - c, 0, 0))
    return _tc_call(
        body, name=name, grid=(NB, nch),
        in_specs=[pl.BlockSpec((Q, LANE), lambda b, c: (nch - 1 - c, u_off + b)),
                  pl.BlockSpec((Q, LANE), lambda b, c: (nch - 1 - c, b)), ck, ck, mat, mat, mat, mat, vec, vec,
                  pl.BlockSpec((1, LANE), lambda b, c: (0, b))],
        out_specs=[pl.BlockSpec((Q, LANE), lambda b, c: (nch - 1 - c, b)), mat, mat, mat, mat, acc8, acc8,
                   pl.BlockSpec((None, 8, LANE), blk)],
        out_shape=[jax.ShapeDtypeStruct((T, width), BF16)] + [jax.ShapeDtypeStruct((NB, LANE, W), F32)] * 4
                  + [jax.ShapeDtypeStruct((NB, 8, W), F32)] * 2 + [jax.ShapeDtypeStruct((NB, 8, LANE), F32)],
        scratch_shapes=[pltpu.VMEM((Q + 8, W), F32), pltpu.VMEM((Q + 8, W), F32), pltpu.VMEM((Q, W), F32),
                        pltpu.VMEM((Q, W), F32), pltpu.VMEM((8, W), F32), pltpu.VMEM((8, W), F32)],
        compiler_params=_params(("parallel", "arbitrary")),
    )(proj, dyg, ck_r, ck_i, p["bre"], p["bim"], p["cre"], p["cim"], p["ar"], p["ai"], p["d"])


def _block_diag(w):
    G, H, P = w.shape
    eye = jnp.eye(8, dtype=w.dtype)
    return (w.reshape(G // 8, 8, H, 1, P) * eye[None, :, None, :, None]).reshape(G // 8, 8 * H, 8 * P)


def _block_diag_t(d, H, P):
    NB = d.shape[0]
    d = d.reshape(NB, 8, H, 8, P)
    eye = jnp.eye(8, dtype=d.dtype)
    return jnp.sum(d * eye[None, :, None, :, None], axis=3).reshape(NB * 8, H, P)


def _attn_fwd(name, q, kv, heads):
    T, D = q.shape
    Mm = kv.shape[0]
    hd = D // heads
    tq = _pick(T, (512, 256, 128))
    scale = hd ** -0.5

    def body(q_ref, k_ref, v_ref, o_ref):
        s = lax.dot_general(q_ref[...], k_ref[...], (((1,), (1,)), ((), ())), preferred_element_type=F32) * scale
        s = s - jnp.max(s, axis=-1, keepdims=True)
        e = jnp.exp(s)
        p = e / jnp.sum(e, axis=-1, keepdims=True)
        o_ref[...] = jnp.dot(p.astype(BF16), v_ref[...], preferred_element_type=F32).astype(o_ref.dtype)

    return _tc_call(
        body, name=name, grid=(heads, T // tq),
        in_specs=[pl.BlockSpec((tq, hd), lambda h, i: (i, h)), pl.BlockSpec((Mm, hd), lambda h, i: (0, h)),
                  pl.BlockSpec((Mm, hd), lambda h, i: (0, heads + h))],
        out_specs=pl.BlockSpec((tq, hd), lambda h, i: (i, h)),
        out_shape=jax.ShapeDtypeStruct((T, D), BF16),
        compiler_params=_params(("parallel", "parallel")),
    )(q, kv, kv)


def _attn_bwd(name, q, kv, do, heads):
    T, D = q.shape
    Mm = kv.shape[0]
    hd = D // heads
    tq = _pick(T, (512, 256, 128))
    scale = hd ** -0.5
    nt_dims = (((1,), (1,)), ((), ()))
    tn_dims = (((0,), (0,)), ((), ()))

    def body(q_ref, k_ref, v_ref, do_ref, dq_ref, dk_ref, dv_ref):
        i = pl.program_id(1)

        @pl.when(i == 0)
        def _():
            dk_ref[...] = jnp.zeros_like(dk_ref)
            dv_ref[...] = jnp.zeros_like(dv_ref)

        qv, kvl, vv, dov = q_ref[...], k_ref[...], v_ref[...], do_ref[...]
        s = lax.dot_general(qv, kvl, nt_dims, preferred_element_type=F32) * scale
        s = s - jnp.max(s, axis=-1, keepdims=True)
        e = jnp.exp(s)
        p = e / jnp.sum(e, axis=-1, keepdims=True)
        p16 = p.astype(BF16)
        dv_ref[...] += lax.dot_general(p16, dov, tn_dims, preferred_element_type=F32)
        dp = lax.dot_general(dov, vv, nt_dims, preferred_element_type=F32)
        ds = (p * (dp - jnp.sum(dp * p, axis=-1, keepdims=True)) * scale).astype(BF16)
        dq_ref[...] = jnp.dot(ds, kvl, preferred_element_type=F32).astype(dq_ref.dtype)
        dk_ref[...] += lax.dot_general(ds, qv, tn_dims, preferred_element_type=F32)

    return _tc_call(
        body, name=name, grid=(heads, T // tq),
        in_specs=[pl.BlockSpec((tq, hd), lambda h, i: (i, h)), pl.BlockSpec((Mm, hd), lambda h, i: (0, h)),
                  pl.BlockSpec((Mm, hd), lambda h, i: (0, heads + h)), pl.BlockSpec((tq, hd), lambda h, i: (i, h))],
        out_specs=[pl.BlockSpec((tq, hd), lambda h, i: (i, h)), pl.BlockSpec((Mm, hd), lambda h, i: (0, h)),
                   pl.BlockSpec((Mm, hd), lambda h, i: (0, h))],
        out_shape=[jax.ShapeDtypeStruct((T, D), BF16), jax.ShapeDtypeStruct((Mm, D), F32),
                   jax.ShapeDtypeStruct((Mm, D), F32)],
        compiler_params=_params(("parallel", "arbitrary")),
    )(q, kv, kv, do)


def _position():
    return lax.axis_index("x"), lax.axis_index("y"), lax.axis_index("c")


def _allgather(name, blk, row_mode):
    L = blk.shape[0]
    out_shape = (L, N_DEV) + blk.shape[1:] if row_mode else (N_DEV,) + blk.shape
    x_ref = jax.new_ref(blk, memory_space=pltpu.MemorySpace.HBM)
    out_ref = jax.empty_ref(jax.ShapeDtypeStruct(out_shape, blk.dtype), memory_space=pltpu.MemorySpace.HBM)

    def body(send_sems, recv_sems, local_sem):
        x, y, c = _position()
        me, sibling = (x, y, c), (x, y, 1 - c)
        chips = [(1 - x, y), (x, 1 - y), (1 - x, 1 - y)]
        barrier = pltpu.get_barrier_semaphore()
        for peer in [sibling] + [(*chip, c) for chip in chips]:
            pl.semaphore_signal(barrier, inc=1, device_id=peer, device_id_type=MESH_ID)
        pl.semaphore_wait(barrier, 4)

        def slot(px, py, pc):
            b = 4 * px + 2 * py + pc
            return out_ref.at[:, b] if row_mode else out_ref.at[b]

        def copy(k, block, to, src=None):
            return pltpu.make_async_remote_copy(
                src_ref=slot(*block) if src is None else src, dst_ref=slot(*block),
                send_sem=send_sems.at[k], recv_sem=recv_sems.at[k], device_id=to, device_id_type=MESH_ID)

        mine = pltpu.make_async_copy(x_ref, slot(*me), local_sem)
        mine.start()
        first = [copy(0, me, sibling, src=x_ref)]
        first += [copy(1 + j, me, (*chip, c), src=x_ref) for j, chip in enumerate(chips)]
        for cp in first:
            cp.start()
        passed = [copy(4 + j, (*chip, c), sibling) for j, chip in enumerate(chips)]
        for j, chip in enumerate(chips):
            copy(1 + j, (*chip, c), me).wait_recv()
            passed[j].start()
        copy(0, sibling, me).wait_recv()
        for j, chip in enumerate(chips):
            copy(4 + j, (*chip, 1 - c), me).wait_recv()
        for cp in first + passed:
            cp.wait_send()
        mine.wait()

    pl.kernel(
        body, mesh=plsc.ScalarSubcoreMesh(axis_name="sequencer", num_cores=1), name=name,
        scratch_types=(pltpu.SemaphoreType.DMA((7,)), pltpu.SemaphoreType.DMA((7,)), pltpu.SemaphoreType.DMA),
        compiler_params=pltpu.CompilerParams(collective_id=AG_COLLECTIVE_ID),
    )()
    return out_ref[...]


def _sequencer_kernel(name, body, scratch_types, collective_id):
    pl.kernel(
        body, mesh=plsc.ScalarSubcoreMesh(axis_name="sequencer", num_cores=1), name=name,
        scratch_types=scratch_types, compiler_params=pltpu.CompilerParams(collective_id=collective_id),
    )()


def _handshake(peers):
    barrier = pltpu.get_barrier_semaphore()
    for peer in peers:
        pl.semaphore_signal(barrier, inc=1, device_id=peer, device_id_type=MESH_ID)
    pl.semaphore_wait(barrier, len(peers))


def _rs_sibling(name, grads):
    hbm = pltpu.MemorySpace.HBM
    g_ref = jax.new_ref(grads, memory_space=hbm)
    out_ref = jax.empty_ref(jax.ShapeDtypeStruct((4,) + grads.shape[1:], grads.dtype), memory_space=hbm)

    def body(send_sems, recv_sems):
        x, y, c = _position()
        sibling = (x, y, 1 - c)
        _handshake([sibling])
        copies = [pltpu.make_async_remote_copy(
            src_ref=g_ref.at[2 * q + (1 - c)], dst_ref=out_ref.at[q], send_sem=send_sems.at[q],
            recv_sem=recv_sems.at[q], device_id=sibling, device_id_type=MESH_ID) for q in range(4)]
        for cp in copies:
            cp.start()
        for cp in copies:
            cp.wait_recv()
        for cp in copies:
            cp.wait_send()

    _sequencer_kernel(name, body, (pltpu.SemaphoreType.DMA((4,)), pltpu.SemaphoreType.DMA((4,))),
                      RS_SIBLING_COLLECTIVE_ID)
    return out_ref[...]


def _rs_chips(name, part):
    hbm = pltpu.MemorySpace.HBM
    p_ref = jax.new_ref(part, memory_space=hbm)
    out_ref = jax.empty_ref(jax.ShapeDtypeStruct(part.shape, part.dtype), memory_space=hbm)

    def body(send_sems, recv_sems, local_sem):
        x, y, c = _position()
        flips = [(1 - x, y), (x, 1 - y), (1 - x, 1 - y)]
        _handshake([(fx, fy, c) for fx, fy in flips])
        mine = pltpu.make_async_copy(p_ref.at[2 * x + y], out_ref.at[3], local_sem)
        copies = [pltpu.make_async_remote_copy(
            src_ref=p_ref.at[2 * fx + fy], dst_ref=out_ref.at[k], send_sem=send_sems.at[k], recv_sem=recv_sems.at[k],
            device_id=(fx, fy, c), device_id_type=MESH_ID) for k, (fx, fy) in enumerate(flips)]
        for cp in copies:
            cp.start()
        mine.start()
        for cp in copies:
            cp.wait_recv()
        for cp in copies:
            cp.wait_send()
        mine.wait()

    _sequencer_kernel(name, body, (pltpu.SemaphoreType.DMA((3,)), pltpu.SemaphoreType.DMA((3,)),
                                   pltpu.SemaphoreType.DMA), RS_CHIPS_COLLECTIVE_ID)
    return out_ref[...]


def _adamw_math(g, w, m, v):
    m = ADAM_B1 * m + (1.0 - ADAM_B1) * g
    v = ADAM_B2 * v + (1.0 - ADAM_B2) * (g * g)
    m_hat = m / (1.0 - ADAM_B1 ** ADAM_STEP)
    v_hat = v / (1.0 - ADAM_B2 ** ADAM_STEP)
    delta = -ADAM_LR * (m_hat / (jnp.sqrt(v_hat) + ADAM_EPS) + ADAM_WD * w)
    return delta, m, v


def _adamw_natural(name, gs, ws, ms, vs, lead_block=None):
    n = len(gs)

    def body(*refs):
        ins, outs = refs[:4 * n], refs[4 * n:]
        for k in range(n):
            delta, nm, nv = _adamw_math(*[ins[j * n + k][...] for j in range(4)])
            outs[k][...] = delta
            outs[n + k][...] = nm
            outs[2 * n + k][...] = nv

    out_shape = [jax.ShapeDtypeStruct(w.shape, F32) for w in ws] * 3
    if lead_block is None:
        grid = ()
        spec = lambda a: pl.BlockSpec(a.shape, lambda nd=len(a.shape): (0,) * nd)
    else:
        grid = tuple(d // b for d, b in zip(ws[0].shape[:2], lead_block))
        spec = lambda a: pl.BlockSpec(tuple(lead_block) + tuple(a.shape[2:]),
                                      lambda i, j, nd=len(a.shape): (i, j) + (0,) * (nd - 2))
    return _tc_call(
        body, name=name, grid=grid, in_specs=[spec(a) for a in list(gs) + list(ws) + list(ms) + list(vs)],
        out_specs=[spec(o) for o in out_shape], out_shape=out_shape,
        compiler_params=pltpu.CompilerParams(vmem_limit_bytes=VMEM_LIMIT),
    )(*gs, *ws, *ms, *vs)


def _ew_tile(R, C, nblocks):
    budget = (VMEM_LIMIT * 3) // 4
    return _pick(R, tuple(t for t in (1024, 512, 256, 128, 64, 32, 16, 8) if 8 * t * C * nblocks <= budget))


def _pair_sum(name, grads, landed, c_idx):
    _, r, c = grads.shape
    tile = _ew_tile(r, c, 3)

    def body(c_ref, g_ref, s_ref, o_ref):
        o_ref[...] = (g_ref[...].astype(F32) + s_ref[...].astype(F32)).astype(o_ref.dtype)

    return _tc_call(
        body, name=name,
        grid_spec=pltpu.PrefetchScalarGridSpec(
            num_scalar_prefetch=1, grid=(4, r // tile),
            in_specs=[pl.BlockSpec((None, tile, c), lambda q, i, c_ref: (2 * q + c_ref[0], i, 0)),
                      pl.BlockSpec((None, tile, c), lambda q, i, c_ref: (q, i, 0))],
            out_specs=pl.BlockSpec((None, tile, c), lambda q, i, c_ref: (q, i, 0))),
        out_shape=jax.ShapeDtypeStruct((4, r, c), BF16),
        compiler_params=_params(("parallel", "parallel")),
    )(c_idx, grads, landed)


def _adamw_layer(name, got, w, m, v, layer, prev):
    L, r, c = w.shape
    tile = _ew_tile(r, c, 11)

    def body(g0, g1, g2, g3, w_ref, m_ref, v_ref, *rest):
        outs = rest[-4:]
        g = (g0[...].astype(F32) + g1[...].astype(F32)) + (g2[...].astype(F32) + g3[...].astype(F32))
        delta, nm, nv = _adamw_math(g, w_ref[...], m_ref[...], v_ref[...])
        for ref, val in zip(outs, (g, delta, nm, nv)):
            ref[...] = val

    slab = pl.BlockSpec((None, tile, c), lambda i: (layer, i, 0))
    in_specs = [pl.BlockSpec((None, tile, c), lambda i, k=k: (k, i, 0)) for k in range(4)] + [slab] * 3
    args = [got, got, got, got, w, m, v]
    aliases = {}
    if prev is not None:
        in_specs += [ANY] * 4
        args += list(prev)
        aliases = {7 + k: k for k in range(4)}
    return _tc_call(
        body, name=name, grid=(r // tile,), in_specs=in_specs, out_specs=[slab] * 4,
        out_shape=[jax.ShapeDtypeStruct((L, r, c), F32)] * 4, input_output_aliases=aliases,
        compiler_params=_params(("parallel",)),
    )(*args)


def _reduce_pipeline(name, layer, grads, w, m, v, state, flush):
    c_idx = lax.axis_index("c").astype(jnp.int32).reshape(1)
    landed = _rs_sibling(name + "_rs_sibling", grads)
    yield
    part = _pair_sum(name + "_pair_sum", grads, landed, c_idx)
    got = _rs_chips(name + "_rs_chips", part)
    yield
    while not flush:
        yield
    state[name] = _adamw_layer(name + "_adamw", got, w, m, v, layer, state.get(name))


def _loss_head(name, x, g, target):
    T, D = x.shape

    def fn(xv, tv, gv):
        def f(xx, gg):
            err = _rms(xx, gg) - tv
            return 0.5 * jnp.sum(jnp.mean(err * err, axis=-1, keepdims=True))

        loss, (dx, dg) = jax.value_and_grad(f, argnums=(0, 1))(xv, gv)
        return dx, dx, jnp.full((8, LANE), loss, F32), jnp.sum(dg, axis=0, keepdims=True)

    dx, dx16, loss, dg = _rowwise(name, fn, [(x, D, 0, 0), (target, D, 0, 0)], [g], [(D, F32), (D, BF16)],
                                  [(8, LANE), (1, D)], tile=128)
    return loss[0, 0], dx, dx16, dg


_SHARDED_COL = ("w_in", "conv_dw_w", "conv_w_pw", "ssm_w_glu", "xa_w_kv", "ffn_w_up", "ffn_dw_w")
_SHARDED_ROW = ("w_out", "xa_w_q", "xa_w_o", "ffn_w_down")
_WEIGHTS = ['mix_norm_g', 'w_in', 'conv_dw_w', 'conv_dw_b', 'conv_ln_g', 'conv_ln_b', 'conv_w_pw', 'ssm_a_re',
            'ssm_a_im', 'ssm_log_dt', 'ssm_b_re', 'ssm_b_im', 'ssm_c_re', 'ssm_c_im', 'ssm_d', 'ssm_w_glu', 'w_out',
            'xa_norm_g', 'mem_norm_g', 'xa_w_q', 'xa_w_kv', 'xa_w_o', 'ffn_norm_g', 'ffn_w_up', 'ffn_dw_w',
            'ffn_w_down', 'final_norm_g']
_FWD = ['x', 'mem'] + _WEIGHTS
_AG_ORDER = ("w_in", "conv_dw_w", "conv_w_pw", "ssm_w_glu", "w_out", "xa_w_q", "xa_w_kv", "xa_w_o", "ffn_w_up",
             "ffn_dw_w", "ffn_w_down")


def _pad_rows(a, rows):
    return jnp.pad(a, ((0, 0), (0, rows - a.shape[1]), (0, 0)))


def _step(inp, target, mom_m, mom_v):
    _LAST_CALL.clear()
    x0 = inp["x"][0]
    mem = inp["mem"][0]
    T, D = x0.shape
    L = inp["w_in"].shape[0]
    CW = inp["conv_dw_b"].shape[1]
    SW = inp["ssm_d"].shape[1]
    DFF = inp["ffn_w_down"].shape[1] * N_DEV
    G = SW // SSM_GROUP
    NB = SW // LANE
    u_off = (2 * CW) // LANE
    gate_off = (2 * CW + SW) // 1024

    gathered = {n: [None] * L for n in _AG_ORDER}
    filter_rows = {"conv_dw_w": HALO, "ffn_dw_w": 8}
    for l in range(L):
        for n in _AG_ORDER:
            blk = inp[n][l:l + 1]
            blk = _pad_rows(blk, filter_rows[n]) if n in filter_rows else blk.astype(BF16)
            if n in _SHARDED_COL:
                gathered[n][l] = _allgather("ag_" + n, blk, False)
            else:
                full = _allgather("ag_" + n, blk, True)
                gathered[n][l] = full.reshape(1, N_DEV * full.shape[2], full.shape[3])

    def W(n, l):
        return _W(gathered[n][l], 0, n in _SHARDED_COL)

    def dw_filter(n, l):
        g = gathered[n][l]
        return jnp.transpose(g[:, 0], (1, 0, 2)).reshape(g.shape[2], N_DEV * g.shape[3])

    def row(n, l):
        return (inp[n], l)

    ssm_raw, ssm_p = [], []
    for l in range(L):
        a_re, a_im, ldt = inp["ssm_a_re"][l], inp["ssm_a_im"][l], inp["ssm_log_dt"][l][:, None]
        rep = lambda a: jnp.repeat(a, SSM_GROUP, axis=0)
        flat = lambda b: jnp.transpose(b, (0, 2, 1)).reshape(G * SSM_GROUP, SSM_STATE)
        raw = (a_re, a_im, ldt, rep(a_re), rep(a_im), rep(ldt), flat(inp["ssm_b_re"][l]), flat(inp["ssm_b_im"][l]))
        abar_re, abar_im, bbar_re, bbar_im = _ssm_prep("ssm_prep", raw)
        bbar_re = bbar_re.reshape(G, SSM_GROUP, SSM_STATE)
        bbar_im = bbar_im.reshape(G, SSM_GROUP, SSM_STATE)
        ssm_raw.append(raw)
        ssm_p.append(dict(
            bre=_block_diag(bbar_re).astype(BF16), bim=_block_diag(bbar_im).astype(BF16),
            cre=_block_diag(inp["ssm_c_re"][l]).astype(BF16), cim=_block_diag(inp["ssm_c_im"][l]).astype(BF16),
            ar=abar_re.reshape(NB, 1, SSM_LANES), ai=abar_im.reshape(NB, 1, SSM_LANES), d=inp["ssm_d"][l][None, :]))

    saved = []
    x = x0
    for l in range(L):
        s = {"x_in": x}
        s["h1"] = _rms_fwd("rms_mix", x, row("mix_norm_g", l))
        s["proj"] = _mm_nn("mm_w_in", s["h1"], W("w_in", l), F32)
        s["hc"] = _glu_conv_fwd("conv_fwd", s["proj"], dw_filter("conv_dw_w", l), CW)
        s["hs"] = _rowwise("conv_post", _convpost, [(s["hc"], CW, 0, 0)],
                           [row("conv_dw_b", l), row("conv_ln_g", l), row("conv_ln_b", l)], [(CW, BF16)])[0]
        s["ya"] = _mm_nn("mm_w_pw", s["hs"], W("conv_w_pw", l), F32)
        s["yg"], s["ck_r"], s["ck_i"] = _ssm_fwd("ssm_fwd", s["proj"], u_off, ssm_p[l], SW)
        s["gg"] = _mm_nn("mm_w_glu", s["yg"], W("ssm_w_glu", l), F32)
        nmix = D // 1024
        mix_rows = [(s["proj"], 1024, gate_off, 0), (s["proj"], 1024, gate_off + nmix, 0), (s["ya"], 1024, 0, 0),
                    (s["gg"], 1024, 0, 0), (s["gg"], 1024, nmix, 0)]
        s["mix_rows"] = mix_rows
        s["mix"] = _rowwise("mix_fwd", _mixf, mix_rows, [], [(D, BF16)], ncol=nmix)[0]
        x = _mm_nn("mm_w_out", s["mix"], W("w_out", l), F32, add=x)
        s["x1"] = x
        s["h2"] = _rms_fwd("rms_xa", x, row("xa_norm_g", l))
        s["q"] = _mm_nn("mm_w_q", s["h2"], W("xa_w_q", l), BF16)
        s["mn"] = _rms_fwd("rms_mem", mem, row("mem_norm_g", l))
        s["kv"] = _mm_nn("mm_w_kv", s["mn"], W("xa_w_kv", l), BF16)
        s["o"] = _attn_fwd("attn_fwd", s["q"], s["kv"], XA_HEADS)
        x = _mm_nn("mm_w_o", s["o"], W("xa_w_o", l), F32, add=x)
        s["x2"] = x
        s["h3"] = _rms_fwd("rms_ffn", x, row("ffn_norm_g", l))
        s["up"] = _mm_nn("mm_w_up", s["h3"], W("ffn_w_up", l), F32)
        s["act"] = _ffn_conv_fwd("ffn_conv_fwd", s["up"], dw_filter("ffn_dw_w", l), DFF)
        x = _mm_nn("mm_w_down", s["act"], W("ffn_w_down", l), F32, add=x)
        saved.append(s)

    loss_part, dx, dx16, d_final_g = _loss_head("loss_head", x, inp["final_norm_g"][None, :], target[0])

    big = _SHARDED_COL + _SHARDED_ROW
    small = {n: [None] * L for n in _WEIGHTS if n not in big and n != "final_norm_g"}
    pads = {"conv_dw_w": HALO, "ffn_dw_w": 8}
    shards = {n: tuple(_pad_rows(a, pads[n]) if n in pads else a for a in (inp[n], mom_m[n], mom_v[n])) for n in big}
    state, queue, flush = {}, [], []
    names =[n for n in _WEIGHTS if n not in big]
    sizes = [inp[n].size for n in names]
    total = sum(sizes)
    pack_w = 8 * LANE
    rows_p = -(-total // (LANE * pack_w)) * LANE
    padn = rows_p * pack_w - total

    def pack(parts, fill):
        return jnp.concatenate([p.reshape(-1) for p in parts] + [jnp.full((padn,), fill, F32)]).reshape(rows_p, pack_w)

    def tick():
        for gen in list(queue):
            if next(gen, "done") == "done":
                queue.remove(gen)

    def emit(n, l, g):
        queue.append(_reduce_pipeline(n, l, g.astype(BF16), *shards[n], state, flush))
        tick()

    deferred = []

    def emit_small(n, l, thunk):
        if l == 0:
            deferred.append((n, thunk))
        else:
            emit(n, l, thunk())

    for l in reversed(range(L)):
        s = saved[l]
        dact = _mm_nt("mm_w_down_t", dx16, W("ffn_w_down", l), BF16)
        emit("ffn_w_down", l, _mm_tn("mm_dw_down", s["act"], dx16).reshape(N_DEV, DFF // N_DEV, D))
        d_up, d_ffn_dw = _ffn_conv_bwd("ffn_conv_bwd", s["up"], dw_filter("ffn_dw_w", l), dact, DFF)
        emit("ffn_dw_w", l, jnp.transpose(d_ffn_dw.reshape(8, N_DEV, 2 * DFF // N_DEV), (1, 0, 2)))
        dh3 = _mm_nt("mm_w_up_t", d_up, W("ffn_w_up", l), BF16)
        emit("ffn_w_up", l, _mm_tn("mm_dw_up", s["h3"], d_up, nb=2 * DFF // N_DEV))
        dx, dx16, small["ffn_norm_g"][l] = _rms_bwd("rms_ffn_bwd", s["x2"], row("ffn_norm_g", l), dh3, dx)
        do = _mm_nt("mm_w_o_t", dx16, W("xa_w_o", l), BF16)
        emit_small("xa_w_o", l, lambda a=s["o"], b=dx16: _mm_tn("mm_dw_o", a, b).reshape(N_DEV, D // N_DEV, D))
        dq, dk, dv = _attn_bwd("attn_bwd", s["q"], s["kv"], do, XA_HEADS)
        dkv = jnp.concatenate([dk, dv], axis=1).astype(BF16)
        dh2 = _mm_nt("mm_w_q_t", dq, W("xa_w_q", l), BF16)
        emit_small("xa_w_q", l, lambda a=s["h2"], b=dq: _mm_tn("mm_dw_q", a, b).reshape(N_DEV, D // N_DEV, D))
        dmn = _mm_nt("mm_w_kv_t", dkv, W("xa_w_kv", l), BF16)
        emit("xa_w_kv", l, _mm_tn("mm_dw_kv", s["mn"], dkv, nb=2 * D // N_DEV))

        def mem_bwd(mv, dv_, gv):
            _, vjp = jax.vjp(_rms, mv, gv)
            return jnp.sum(vjp(dv_.astype(F32))[1], axis=0, keepdims=True)

        small["mem_norm_g"][l] = _rowwise("rms_mem_bwd", mem_bwd, [(mem, D, 0, 0), (dmn, D, 0, 0)],
                                          [row("mem_norm_g", l)], [], [(1, D)], tile=128)[0]
        dx, dx16, small["xa_norm_g"][l] = _rms_bwd("rms_xa_bwd", s["x1"], row("xa_norm_g", l), dh2, dx)
        dmix = _mm_nt("mm_w_out_t", dx16, W("w_out", l), BF16)
        emit_small("w_out", l, lambda a=s["mix"], b=dx16: _mm_tn("mm_dw_out", a, b).reshape(N_DEV, D // N_DEV, D))

        def mix_bwd(gla, glb, ya, ga, gb, dm):
            _, vjp = jax.vjp(_mixf, gla, glb, ya, ga, gb)
            return vjp(dm.astype(F32))

        nmix = D // 1024
        dgla, dglb, dya, dga, dgb = _rowwise("mix_bwd", mix_bwd, s["mix_rows"] + [(dmix, 1024, 0, 0)], [],
                                             [(D, BF16)] * 5, ncol=nmix)
        dgg = jnp.concatenate([dga, dgb], axis=1)
        dyg = _mm_nt("mm_w_glu_t", dgg, W("ssm_w_glu", l), BF16)
        emit_small("ssm_w_glu", l, lambda a=s["yg"], b=dgg: _mm_tn("mm_dw_glu", a, b, nb=2 * D // N_DEV))
        du, dbr, dbi, dcr, dci, dar, dai, dd = _ssm_bwd("ssm_bwd", s["proj"], u_off, ssm_p[l], s["ck_r"], s["ck_i"],
                                                        dyg, SW)
        cots = (dar[:, 0, :].reshape(G, SSM_STATE), dai[:, 0, :].reshape(G, SSM_STATE),
                _block_diag_t(dbr, SSM_GROUP, SSM_STATE).reshape(G * SSM_GROUP, SSM_STATE),
                _block_diag_t(dbi, SSM_GROUP, SSM_STATE).reshape(G * SSM_GROUP, SSM_STATE))
        g_are, g_aim, g_ldt, g_bre, g_bim = _ssm_prep_bwd("ssm_prep_bwd", ssm_raw[l], cots)
        small["ssm_a_re"][l], small["ssm_a_im"][l], small["ssm_log_dt"][l] = g_are, g_aim, g_ldt[:, 0]
        small["ssm_b_re"][l] = jnp.transpose(g_bre.reshape(G, SSM_GROUP, SSM_STATE), (0, 2, 1))
        small["ssm_b_im"][l] = jnp.transpose(g_bim.reshape(G, SSM_GROUP, SSM_STATE), (0, 2, 1))
        small["ssm_c_re"][l] = _block_diag_t(dcr, SSM_GROUP, SSM_STATE)
        small["ssm_c_im"][l] = _block_diag_t(dci, SSM_GROUP, SSM_STATE)
        small["ssm_d"][l] = dd[:, 0, :].reshape(SW)
        dhs = _mm_nt("mm_w_pw_t", dya, W("conv_w_pw", l), BF16)
        emit("conv_w_pw", l, _mm_tn("mm_dw_pw", s["hs"], dya, nb=D // N_DEV))

        def post_bwd(hc, dh, b, lg, lb):
            _, vjp = jax.vjp(_convpost, hc, b, lg, lb)
            dhc, db, dlg, dlb = vjp(dh.astype(F32))
            return dhc, jnp.sum(db, axis=0, keepdims=True), jnp.sum(dlg, axis=0, keepdims=True), \
                jnp.sum(dlb, axis=0, keepdims=True)

        dhc, small["conv_dw_b"][l], small["conv_ln_g"][l], small["conv_ln_b"][l] = _rowwise(
            "conv_post_bwd", post_bwd, [(s["hc"], CW, 0, 0), (dhs, CW, 0, 0)],
            [row("conv_dw_b", l), row("conv_ln_g", l), row("conv_ln_b", l)], [(CW, F32)], [(1, CW)] * 3)
        da, db, d_conv_dw = _glu_conv_bwd("conv_bwd", s["proj"], dw_filter("conv_dw_w", l), dhc, CW)
        emit("conv_dw_w", l, jnp.transpose(d_conv_dw.reshape(HALO, N_DEV, CW // N_DEV), (1, 0, 2)))
        dproj = jnp.concatenate([da, db, du, dgla, dglb], axis=1)
        dh1 = _mm_nt("mm_w_in_t", dproj, W("w_in", l), BF16)
        dx, dx16, small["mix_norm_g"][l] = _rms_bwd("rms_mix_bwd", s["x_in"], row("mix_norm_g", l), dh1, dx)
        if l == 0:
            flat_g = jnp.concatenate([(jnp.stack(small[n]) if n != "final_norm_g" else d_final_g).reshape(-1)
                                      for n in names])
            g_all = _allgather("ag_small_grads", pack([flat_g], 0.0)[None], False)
        emit("w_in", l, _mm_tn("mm_dw_in", s["h1"], dproj, nb=dproj.shape[1] // N_DEV))
    for n, thunk in deferred:
        emit(n, 0, thunk())

    flush.append(True)
    while queue:
        tick()
    results ={n: [o[:, :inp[n].shape[1], :] for o in state[n]] if n in pads else state[n] for n in big}
    def sum8(*parts):
        tot = parts[0]
        for part in parts[1:]:
            tot = tot + part
        return tot

    g_sum = _rowwise("small_grad_sum", sum8, [(g_all.reshape(N_DEV * rows_p, pack_w), pack_w, 0, k * rows_p)
                                              for k in range(N_DEV)], [], [(pack_w, F32)], tile=LANE, nrows=rows_p)[0]
    g_flat, grads, offs = g_sum.reshape(-1), {}, 0
    for n, sz in zip(names, sizes):
        grads[n] = g_flat[offs:offs + sz].reshape(inp[n].shape)
        offs += sz
    sparse = [n for n in names if inp[n].ndim == 4 and inp[n].shape[-1] < SSM_STATE]
    dense = [n for n in names if n not in sparse]
    as2d = lambda a: a[None, :] if a.ndim == 1 else a
    for group, block in ((dense, None), (sparse, (1, 16))):
        outs = _adamw_natural("small_adamw", *[[as2d(src[n]) for n in group] for src in (grads, inp, mom_m, mom_v)],
                              lead_block=block)
        for k, n in enumerate(group):
            results[n] = [grads[n]] + [outs[j * len(group) + k].reshape(inp[n].shape) for j in range(3)]

    _LAST_CALL.clear()
    loss = lax.psum(loss_part, ("x", "y", "c"))
    grad_x = dx[None]
    return (loss, grad_x, *[results[n][0] for n in _WEIGHTS], *[results[n][1] for n in _WEIGHTS],
            *[results[n][2] for n in _WEIGHTS], *[results[n][3] for n in _WEIGHTS])


def kernel(x, mem, mix_norm_g, w_in, conv_dw_w, conv_dw_b, conv_ln_g, conv_ln_b, conv_w_pw, ssm_a_re, ssm_a_im, ssm_log_dt, ssm_b_re, ssm_b_im, ssm_c_re, ssm_c_im, ssm_d, ssm_w_glu, w_out, xa_norm_g, mem_norm_g, xa_w_q, xa_w_kv, xa_w_o, ffn_norm_g, ffn_w_up, ffn_dw_w, ffn_w_down, final_norm_g, loss_target, m_mix_norm_g, m_w_in, m_conv_dw_w, m_conv_dw_b, m_conv_ln_g, m_conv_ln_b, m_conv_w_pw, m_ssm_a_re, m_ssm_a_im, m_ssm_log_dt, m_ssm_b_re, m_ssm_b_im, m_ssm_c_re, m_ssm_c_im, m_ssm_d, m_ssm_w_glu, m_w_out, m_xa_norm_g, m_mem_norm_g, m_xa_w_q, m_xa_w_kv, m_xa_w_o, m_ffn_norm_g, m_ffn_w_up, m_ffn_dw_w, m_ffn_w_down, m_final_norm_g, v_mix_norm_g, v_w_in, v_conv_dw_w, v_conv_dw_b, v_conv_ln_g, v_conv_ln_b, v_conv_w_pw, v_ssm_a_re, v_ssm_a_im, v_ssm_log_dt, v_ssm_b_re, v_ssm_b_im, v_ssm_c_re, v_ssm_c_im, v_ssm_d, v_ssm_w_glu, v_w_out, v_xa_norm_g, v_mem_norm_g, v_xa_w_q, v_xa_w_kv, v_xa_w_o, v_ffn_norm_g, v_ffn_w_up, v_ffn_dw_w, v_ffn_w_down, v_final_norm_g):
    args = (x, mem, mix_norm_g, w_in, conv_dw_w, conv_dw_b, conv_ln_g, conv_ln_b, conv_w_pw, ssm_a_re, ssm_a_im, ssm_log_dt, ssm_b_re, ssm_b_im, ssm_c_re, ssm_c_im, ssm_d, ssm_w_glu, w_out, xa_norm_g, mem_norm_g, xa_w_q, xa_w_kv, xa_w_o, ffn_norm_g, ffn_w_up, ffn_dw_w, ffn_w_down, final_norm_g)
    ms = (m_mix_norm_g, m_w_in, m_conv_dw_w, m_conv_dw_b, m_conv_ln_g, m_conv_ln_b, m_conv_w_pw, m_ssm_a_re, m_ssm_a_im, m_ssm_log_dt, m_ssm_b_re, m_ssm_b_im, m_ssm_c_re, m_ssm_c_im, m_ssm_d, m_ssm_w_glu, m_w_out, m_xa_norm_g, m_mem_norm_g, m_xa_w_q, m_xa_w_kv, m_xa_w_o, m_ffn_norm_g, m_ffn_w_up, m_ffn_dw_w, m_ffn_w_down, m_final_norm_g)
    vs = (v_mix_norm_g, v_w_in, v_conv_dw_w, v_conv_dw_b, v_conv_ln_g, v_conv_ln_b, v_conv_w_pw, v_ssm_a_re, v_ssm_a_im, v_ssm_log_dt, v_ssm_b_re, v_ssm_b_im, v_ssm_c_re, v_ssm_c_im, v_ssm_d, v_ssm_w_glu, v_w_out, v_xa_norm_g, v_mem_norm_g, v_xa_w_q, v_xa_w_kv, v_xa_w_o, v_ffn_norm_g, v_ffn_w_up, v_ffn_dw_w, v_ffn_w_down, v_final_norm_g)
    return _step(dict(zip(_FWD, args)), loss_target, dict(zip(_WEIGHTS, ms)), dict(zip(_WEIGHTS, vs)))
```

```python
import functools

import jax
import jax.numpy as jnp
from jax import lax
from jax.experimental import pallas as pl
from jax.experimental.pallas import tpu as pltpu
from jax.experimental.pallas import tpu_sc as plsc

F32 = jnp.float32
BF16 = jnp.bfloat16
MESH_ID = pl.DeviceIdType.MESH
N_DEV = 8
EPS = 1e-6
VMEM_LIMIT = 48 * 1024 * 1024
ANY = pl.BlockSpec(memory_space=pl.ANY)

ADAM_LR = 0.001
ADAM_B1 = 0.9
ADAM_B2 = 0.999
ADAM_EPS = 1e-08
ADAM_WD = 0.01
ADAM_STEP = 10

CONV_K = 31
FFN_K = 3
XA_HEADS = 4
SSM_GROUP = 16
SSM_STATE = 64
HALO = 32
LANE = 128
SSM_LANES = 512
AG_COLLECTIVE_ID = 1
RS_SIBLING_COLLECTIVE_ID = 2
RS_CHIPS_COLLECTIVE_ID = 3


def _pick(n, prefs):
    for p in prefs:
        if p <= n and n % p == 0:
            return p
    return n


def _params(sem, vmem=VMEM_LIMIT):
    return pltpu.CompilerParams(dimension_semantics=sem, vmem_limit_bytes=vmem)


_LAST_CALL = []


def _tc_call(*call_args, **call_kwargs):
    call = pl.pallas_call(*call_args, **call_kwargs)

    def run(*args):
        args = list(args)
        if _LAST_CALL:
            i = next(k for k, a in enumerate(args) if a.ndim >= 2)
            args[i] = lax.optimization_barrier((args[i], _LAST_CALL[0]))[0]
        out = call(*args)
        _LAST_CALL[:] = [out[0] if isinstance(out, (tuple, list)) else out]
        return out

    return run


def _sigmoid(x):
    return 1.0 / (1.0 + jnp.exp(-x))


def _silu(x):
    return x * _sigmoid(x)


def _gelu(x):
    return 0.5 * x * (1.0 + jnp.tanh(0.7978845608028654 * (x + 0.044715 * (x * x * x))))


def _rms(x, g):
    return x * lax.rsqrt(jnp.mean(x * x, axis=-1, keepdims=True) + EPS) * g


def _convpost(hc, bias, ln_g, ln_b):
    h = hc + bias
    mu = jnp.mean(h, axis=-1, keepdims=True)
    xc = h - mu
    y = xc * lax.rsqrt(jnp.mean(xc * xc, axis=-1, keepdims=True) + EPS)
    return _silu(y * ln_g + ln_b)


def _mixf(gla, glb, ya, ga, gb):
    return _sigmoid(gla) * ya + _sigmoid(glb) * (ga * _sigmoid(gb))


class _W:
    def __init__(self, arr, layer, blocked):
        self.arr, self.layer, self.blocked = arr, layer, blocked
        if blocked:
            _, _, self.K, self.nb = arr.shape
            self.N = N_DEV * self.nb
        else:
            _, self.K, self.N = arr.shape
            self.nb = self.N

    def spec(self, tk, tn, ki, ni):
        l = self.layer
        if self.blocked:
            per = self.nb // tn
            return pl.BlockSpec((None, None, tk, tn), lambda *g: (ni(*g) // per, l, ki(*g), ni(*g) % per))
        return pl.BlockSpec((None, tk, tn), lambda *g: (l, ki(*g), ni(*g)))


_M_TILES = (1024, 512, 256, 128, 64, 32, 16, 8)
_N_TILES = (1408, 1024, 896, 512, 256, 128)
_K_TILES = (512, 1408, 896, 256, 128)
MAX_FULL_K = 2048
_K_FULL_TILES = (2048, 1408, 1024, 896, 512, 256, 128)
MAX_FULL_T = 4096


def _mm_nn(name, a, w, out_dtype, add=None):
    M, K = a.shape
    assert K == w.K
    tm, tn = _pick(M, _M_TILES), _pick(w.nb, _N_TILES)
    tk = K if K <= MAX_FULL_K else _pick(K, _K_FULL_TILES)
    nk = K // tk

    def body(*refs):
        a_ref, w_ref = refs[:2]
        r_ref = refs[2] if add is not None else None
        o_ref = refs[3] if add is not None else refs[2]
        part = jnp.dot(a_ref[...].astype(BF16), w_ref[...], preferred_element_type=F32)
        if nk == 1:
            o_ref[...] = (part if add is None else part + r_ref[...]).astype(o_ref.dtype)
            return
        acc = refs[-1]
        k = pl.program_id(2)

        @pl.when(k == 0)
        def _():
            acc[...] = part

        @pl.when(k > 0)
        def _():
            acc[...] += part

        @pl.when(k == nk - 1)
        def _():
            res = acc[...]
            if add is not None:
                res = res + r_ref[...]
            o_ref[...] = res.astype(o_ref.dtype)

    in_specs = [pl.BlockSpec((tm, tk), lambda i, j, k: (i, k)),
                w.spec(tk, tn, lambda i, j, k: k, lambda i, j, k: j)]
    args = [a, w.arr]
    if add is not None:
        in_specs.append(pl.BlockSpec((tm, tn), lambda i, j, k: (i, j)))
        args.append(add)
    return _tc_call(
        body, name=name, grid=(M // tm, w.N // tn, nk), in_specs=in_specs,
        out_specs=pl.BlockSpec((tm, tn), lambda i, j, k: (i, j)),
        out_shape=jax.ShapeDtypeStruct((M, w.N), out_dtype),
        scratch_shapes=[] if nk == 1 else [pltpu.VMEM((tm, tn), F32)],
        compiler_params=_params(("parallel", "parallel", "arbitrary")),
    )(*args)


def _mm_nt(name, a, w, out_dtype):
    M, N = a.shape
    assert N == w.N
    tm, tkk = _pick(M, _M_TILES), _pick(w.K, _N_TILES)
    tnn = w.nb if w.nb <= MAX_FULL_K else _pick(w.nb, _K_FULL_TILES)
    nn = N // tnn

    def body(a_ref, w_ref, o_ref, *scratch):
        part = lax.dot_general(a_ref[...].astype(BF16), w_ref[...], (((1,), (1,)), ((), ())),
                               preferred_element_type=F32)
        if nn == 1:
            o_ref[...] = part.astype(o_ref.dtype)
            return
        acc = scratch[0]
        n = pl.program_id(2)

        @pl.when(n == 0)
        def _():
            acc[...] = part

        @pl.when(n > 0)
        def _():
            acc[...] += part

        @pl.when(n == nn - 1)
        def _():
            o_ref[...] = acc[...].astype(o_ref.dtype)

    return _tc_call(
        body, name=name, grid=(M // tm, w.K // tkk, nn),
        in_specs=[pl.BlockSpec((tm, tnn), lambda i, j, n: (i, n)),
                  w.spec(tkk, tnn, lambda i, j, n: j, lambda i, j, n: n)],
        out_specs=pl.BlockSpec((tm, tkk), lambda i, j, n: (i, j)),
        out_shape=jax.ShapeDtypeStruct((M, w.K), out_dtype),
        scratch_shapes=[] if nn == 1 else [pltpu.VMEM((tm, tkk), F32)],
        compiler_params=_params(("parallel", "parallel", "arbitrary")),
    )(a, w.arr)


def _mm_tn(name, a, b, nb=None):
    T, K = a.shape
    _, N = b.shape
    width = N if nb is None else nb
    assert T <= MAX_FULL_T
    tkk, tn = _pick(K, (512, 256, 128)), _pick(width, _N_TILES)
    per = width // tn

    def body(a_ref, b_ref, o_ref):
        o_ref[...] = lax.dot_general(a_ref[...].astype(BF16), b_ref[...].astype(BF16), (((0,), (0,)), ((), ())),
                                     preferred_element_type=F32).astype(o_ref.dtype)

    if nb is None:
        out_spec = pl.BlockSpec((tkk, tn), lambda j, i: (i, j))
        out_shape = jax.ShapeDtypeStruct((K, N), BF16)
    else:
        out_spec = pl.BlockSpec((None, tkk, tn), lambda j, i: (j // per, i, j % per))
        out_shape = jax.ShapeDtypeStruct((N_DEV, K, nb), BF16)
    return _tc_call(
        body, name=name, grid=(N // tn, K // tkk),
        in_specs=[pl.BlockSpec((T, tkk), lambda j, i: (0, i)),
                  pl.BlockSpec((T, tn), lambda j, i: (0, j))],
        out_specs=out_spec, out_shape=out_shape,
        compiler_params=_params(("parallel", "parallel")),
    )(a, b)


def _rowwise(name, fn, rows, consts, outs, accs=(), tile=256, ncol=1, nrows=None):
    n_r, n_c, n_o = len(rows), len(consts), len(outs)
    T = rows[0][0].shape[0] if nrows is None else nrows
    tile = _pick(T, tuple(t for t in (512, 256, 128, 64, 32, 16, 8) if t <= tile))
    nt = T // tile
    const_rows = [c[1] if isinstance(c, tuple) else None for c in consts]
    consts = [c[0] if isinstance(c, tuple) else c for c in consts]

    def body(*refs):
        vals = [r[...] for r in refs[:n_r]]
        vals += [r[...] if l is None else r[l:l + 1, :] for r, l in zip(refs[n_r:n_r + n_c], const_rows)]
        res = fn(*vals)
        if not isinstance(res, (tuple, list)):
            res = (res,)
        o_refs = refs[n_r + n_c:n_r + n_c + n_o]
        a_refs = refs[n_r + n_c + n_o:]
        for r, v in zip(o_refs, res[:n_o]):
            r[...] = v.astype(r.dtype)
        first = pl.program_id(1) == 0
        for r, v in zip(a_refs, res[n_o:]):
            @pl.when(first)
            def _(r=r, v=v):
                r[...] = v.astype(F32)

            @pl.when(jnp.logical_not(first))
            def _(r=r, v=v):
                r[...] += v.astype(F32)

    in_specs, args = [], []
    for arr, w, off, roff in rows:
        rb = roff // tile
        assert roff % tile == 0
        in_specs.append(pl.BlockSpec((tile, w), lambda j, i, off=off, rb=rb: (i + rb, off + j)))
        args.append(arr)
    for cst in consts:
        in_specs.append(pl.BlockSpec(cst.shape, lambda j, i: (0, 0)))
        args.append(cst)
    out_specs, out_shape = [], []
    for tw, dt in outs:
        out_specs.append(pl.BlockSpec((tile, tw // ncol), lambda j, i: (i, j)))
        out_shape.append(jax.ShapeDtypeStruct((T, tw), dt))
    for nr, tw in accs:
        out_specs.append(pl.BlockSpec((nr, tw // ncol), lambda j, i: (0, j)))
        out_shape.append(jax.ShapeDtypeStruct((nr, tw), F32))
    res = _tc_call(
        body, name=name, grid=(ncol, nt), in_specs=in_specs, out_specs=out_specs, out_shape=out_shape,
        compiler_params=_params(("parallel", "arbitrary")),
    )(*args)
    return res


def _rms_fwd(name, x, g):
    D = x.shape[1]
    return _rowwise(name, lambda xv, gv: _rms(xv, gv), [(x, D, 0, 0)], [g], [(D, BF16)])[0]


def _rms_bwd(name, x, g, dh, dx_in):
    D = x.shape[1]

    def fn(xv, dhv, dxv, gv):
        _, vjp = jax.vjp(_rms, xv, gv)
        dx, dg = vjp(dhv.astype(F32))
        tot = dx + dxv
        return tot, tot, jnp.sum(dg, axis=0, keepdims=True)

    return _rowwise(name, fn, [(x, D, 0, 0), (dh, D, 0, 0), (dx_in, D, 0, 0)], [g], [(D, F32), (D, BF16)], [(1, D)],
                    tile=256)


def _lag_views(win, K, R, forward):
    n = win.shape[0]
    for r in range(8):
        if r >= K:
            break
        if r == 0:
            rolled = win
        else:
            rolled = pltpu.roll(win, (n - r) if forward else r, axis=0)
        for q in range((K - 1 - r) // 8 + 1):
            s = 8 * q + r
            if forward:
                yield s, rolled[8 * q:8 * q + R]
            else:
                yield s, rolled[HALO - 8 * q:HALO - 8 * q + R]


def _conv_chunk(win, w_ref, K, R):
    acc = None
    for s, view in _lag_views(win, K, R, forward=False):
        term = w_ref[K - 1 - s:K - s, :] * view
        acc = term if acc is None else acc + term
    return acc


def _conv_chunk_t(win, w_ref, K, R):
    acc = None
    for s, view in _lag_views(win, K, R, forward=True):
        term = w_ref[K - 1 - s:K - s, :] * view
        acc = term if acc is None else acc + term
    return acc


def _conv_dw(xwin, dy, K, R):
    taps = [None] * K
    for s, view in _lag_views(xwin, K, R, forward=False):
        taps[K - 1 - s] = jnp.sum(dy * view, axis=0, keepdims=True)
    return taps


def _chunks(T):
    R = _pick(T, (128, 64, 32))
    return R, T // R


def _glu_conv_fwd(name, proj, w, cw_total):
    T = proj.shape[0]
    C = cw_total
    cw = LANE
    nb = C // cw
    R, nch = _chunks(T)

    def body(a_ref, b_ref, w_ref, o_ref, s_ref):
        s_ref[0:HALO, :] = jnp.zeros((HALO, cw), F32)

        def fill(i, _):
            r0 = pl.multiple_of(i * R, R)
            s_ref[pl.ds(HALO + r0, R), :] = a_ref[pl.ds(r0, R), :] * _sigmoid(b_ref[pl.ds(r0, R), :])
            return 0

        lax.fori_loop(0, nch, fill, 0)

        def conv(i, _):
            r0 = pl.multiple_of(i * R, R)
            o_ref[pl.ds(r0, R), :] = _conv_chunk(s_ref[pl.ds(r0, R + HALO), :], w_ref, CONV_K, R)
            return 0

        lax.fori_loop(0, nch, conv, 0)

    return _tc_call(
        body, name=name, grid=(nb,),
        in_specs=[pl.BlockSpec((T, cw), lambda j: (0, j)), pl.BlockSpec((T, cw), lambda j: (0, nb + j)),
                  pl.BlockSpec((HALO, cw), lambda j: (0, j))],
        out_specs=pl.BlockSpec((T, cw), lambda j: (0, j)),
        out_shape=jax.ShapeDtypeStruct((T, C), F32),
        scratch_shapes=[pltpu.VMEM((T + HALO, cw), F32)],
        compiler_params=_params(("parallel",)),
    )(proj, proj, w)


def _glu_conv_bwd(name, proj, w, dhc, cw_total):
    T = proj.shape[0]
    C = cw_total
    cw = LANE
    nb = C // cw
    R, nch = _chunks(T)

    def body(a_ref, b_ref, w_ref, dy_ref, da_ref, db_ref, dw_ref, s_ref, g_ref, acc_ref):
        s_ref[0:HALO, :] = jnp.zeros((HALO, cw), F32)
        g_ref[T:T + HALO, :] = jnp.zeros((HALO, cw), F32)
        acc_ref[...] = jnp.zeros_like(acc_ref)

        def fill(i, _):
            r0 = pl.multiple_of(i * R, R)
            s_ref[pl.ds(HALO + r0, R), :] = a_ref[pl.ds(r0, R), :] * _sigmoid(b_ref[pl.ds(r0, R), :])
            g_ref[pl.ds(r0, R), :] = dy_ref[pl.ds(r0, R), :]
            return 0

        lax.fori_loop(0, nch, fill, 0)

        def back(i, _):
            r0 = pl.multiple_of(i * R, R)
            dhg = _conv_chunk_t(g_ref[pl.ds(r0, R + HALO), :], w_ref, CONV_K, R)
            av = a_ref[pl.ds(r0, R), :]
            sg = _sigmoid(b_ref[pl.ds(r0, R), :])
            da_ref[pl.ds(r0, R), :] = (dhg * sg).astype(da_ref.dtype)
            db_ref[pl.ds(r0, R), :] = (dhg * av * sg * (1.0 - sg)).astype(db_ref.dtype)
            taps = _conv_dw(s_ref[pl.ds(r0, R + HALO), :], dy_ref[pl.ds(r0, R), :], CONV_K, R)
            for k, tap in enumerate(taps):
                acc_ref[k:k + 1, :] += tap
            return 0

        lax.fori_loop(0, nch, back, 0)
        dw_ref[...] = acc_ref[...]

    return _tc_call(
        body, name=name, grid=(nb,),
        in_specs=[pl.BlockSpec((T, cw), lambda j: (0, j)), pl.BlockSpec((T, cw), lambda j: (0, nb + j)),
                  pl.BlockSpec((HALO, cw), lambda j: (0, j)), pl.BlockSpec((T, cw), lambda j: (0, j))],
        out_specs=[pl.BlockSpec((T, cw), lambda j: (0, j)), pl.BlockSpec((T, cw), lambda j: (0, j)),
                   pl.BlockSpec((HALO, cw), lambda j: (0, j))],
        out_shape=[jax.ShapeDtypeStruct((T, C), BF16), jax.ShapeDtypeStruct((T, C), BF16),
                   jax.ShapeDtypeStruct((HALO, C), F32)],
        scratch_shapes=[pltpu.VMEM((T + HALO, cw), F32), pltpu.VMEM((T + HALO, cw), F32),
                        pltpu.VMEM((HALO, cw), F32)],
        compiler_params=_params(("parallel",)),
    )(proj, proj, w, dhc)


def _ffn_conv_fwd(name, up, w, dff):
    T = up.shape[0]
    cw = LANE
    nb = dff // cw
    R, nch = _chunks(T)

    def body(g_ref, v_ref, wg_ref, wv_ref, o_ref, sg_ref, sv_ref):
        sg_ref[0:HALO, :] = jnp.zeros((HALO, cw), F32)
        sv_ref[0:HALO, :] = jnp.zeros((HALO, cw), F32)

        def fill(i, _):
            r0 = pl.multiple_of(i * R, R)
            sg_ref[pl.ds(HALO + r0, R), :] = g_ref[pl.ds(r0, R), :]
            sv_ref[pl.ds(HALO + r0, R), :] = v_ref[pl.ds(r0, R), :]
            return 0

        lax.fori_loop(0, nch, fill, 0)

        def conv(i, _):
            r0 = pl.multiple_of(i * R, R)
            gc = _conv_chunk(sg_ref[pl.ds(r0, R + HALO), :], wg_ref, FFN_K, R)
            vc = _conv_chunk(sv_ref[pl.ds(r0, R + HALO), :], wv_ref, FFN_K, R)
            o_ref[pl.ds(r0, R), :] = (_silu(gc) * vc).astype(o_ref.dtype)
            return 0

        lax.fori_loop(0, nch, conv, 0)

    return _tc_call(
        body, name=name, grid=(nb,),
        in_specs=[pl.BlockSpec((T, cw), lambda j: (0, j)), pl.BlockSpec((T, cw), lambda j: (0, nb + j)),
                  pl.BlockSpec((8, cw), lambda j: (0, j)), pl.BlockSpec((8, cw), lambda j: (0, nb + j))],
        out_specs=pl.BlockSpec((T, cw), lambda j: (0, j)),
        out_shape=jax.ShapeDtypeStruct((T, dff), BF16),
        scratch_shapes=[pltpu.VMEM((T + HALO, cw), F32), pltpu.VMEM((T + HALO, cw), F32)],
        compiler_params=_params(("parallel",)),
    )(up, up, w, w)


def _ffn_conv_bwd(name, up, w, dact, dff):
    T = up.shape[0]
    cw = LANE
    nb = dff // cw
    R, nch = _chunks(T)

    def body(g_ref, v_ref, wg_ref, wv_ref, da_ref, dg_ref, dv_ref, dwg_ref, dwv_ref,
             sg_ref, sv_ref, tg_ref, tv_ref, ag_ref, av_ref):
        zero = jnp.zeros((HALO, cw), F32)
        sg_ref[0:HALO, :] = zero
        sv_ref[0:HALO, :] = zero
        tg_ref[T:T + HALO, :] = zero
        tv_ref[T:T + HALO, :] = zero
        ag_ref[...] = jnp.zeros_like(ag_ref)
        av_ref[...] = jnp.zeros_like(av_ref)

        def fill(i, _):
            r0 = pl.multiple_of(i * R, R)
            sg_ref[pl.ds(HALO + r0, R), :] = g_ref[pl.ds(r0, R), :]
            sv_ref[pl.ds(HALO + r0, R), :] = v_ref[pl.ds(r0, R), :]
            return 0

        lax.fori_loop(0, nch, fill, 0)

        def grads(i, _):
            r0 = pl.multiple_of(i * R, R)
            gwin = sg_ref[pl.ds(r0, R + HALO), :]
            vwin = sv_ref[pl.ds(r0, R + HALO), :]
            gc = _conv_chunk(gwin, wg_ref, FFN_K, R)
            vc = _conv_chunk(vwin, wv_ref, FFN_K, R)
            da = da_ref[pl.ds(r0, R), :].astype(F32)
            sg = _sigmoid(gc)
            dgc = da * vc * (sg * (1.0 + gc * (1.0 - sg)))
            dvc = da * (gc * sg)
            tg_ref[pl.ds(r0, R), :] = dgc
            tv_ref[pl.ds(r0, R), :] = dvc
            for k, tap in enumerate(_conv_dw(gwin, dgc, FFN_K, R)):
                ag_ref[k:k + 1, :] += tap
            for k, tap in enumerate(_conv_dw(vwin, dvc, FFN_K, R)):
                av_ref[k:k + 1, :] += tap
            return 0

        lax.fori_loop(0, nch, grads, 0)

        def back(i, _):
            r0 = pl.multiple_of(i * R, R)
            dg_ref[pl.ds(r0, R), :] = _conv_chunk_t(tg_ref[pl.ds(r0, R + HALO), :], wg_ref, FFN_K, R).astype(dg_ref.dtype)
            dv_ref[pl.ds(r0, R), :] = _conv_chunk_t(tv_ref[pl.ds(r0, R + HALO), :], wv_ref, FFN_K, R).astype(dv_ref.dtype)
            return 0

        lax.fori_loop(0, nch, back, 0)
        dwg_ref[...] = ag_ref[...]
        dwv_ref[...] = av_ref[...]

    col = lambda j: (0, j)
    dg, dv, dwg, dwv = _tc_call(
        body, name=name, grid=(nb,),
        in_specs=[pl.BlockSpec((T, cw), col), pl.BlockSpec((T, cw), lambda j: (0, nb + j)),
                  pl.BlockSpec((8, cw), col), pl.BlockSpec((8, cw), lambda j: (0, nb + j)),
                  pl.BlockSpec((T, cw), col)],
        out_specs=[pl.BlockSpec((T, cw), col), pl.BlockSpec((T, cw), col),
                   pl.BlockSpec((8, cw), col), pl.BlockSpec((8, cw), col)],
        out_shape=[jax.ShapeDtypeStruct((T, dff), BF16), jax.ShapeDtypeStruct((T, dff), BF16),
                   jax.ShapeDtypeStruct((8, dff), F32), jax.ShapeDtypeStruct((8, dff), F32)],
        scratch_shapes=[pltpu.VMEM((T + HALO, cw), F32), pltpu.VMEM((T + HALO, cw), F32),
                        pltpu.VMEM((T + HALO, cw), F32), pltpu.VMEM((T + HALO, cw), F32),
                        pltpu.VMEM((8, cw), F32), pltpu.VMEM((8, cw), F32)],
        compiler_params=_params(("parallel",)),
    )(up, up, w, w, dact)
    return jnp.concatenate([dg, dv], axis=1), jnp.concatenate([dwg, dwv], axis=1)


def _zoh(a_re, a_im, log_dt):
    ar = jnp.minimum(a_re, -1e-4)
    ai = a_im
    dt = jnp.exp(log_dt)
    mag = jnp.exp(dt * ar)
    abar_re = mag * jnp.cos(dt * ai)
    abar_im = mag * jnp.sin(dt * ai)
    den = ar * ar + ai * ai
    nr = abar_re - 1.0
    ni = abar_im
    return abar_re, abar_im, (nr * ar + ni * ai) / den, (ni * ar - nr * ai) / den


def _discretize(a_re, a_im, log_dt, a_re_h, a_im_h, log_dt_h, b_re, b_im):
    abar_re, abar_im, _, _ = _zoh(a_re, a_im, log_dt)
    _, _, z_re, z_im = _zoh(a_re_h, a_im_h, log_dt_h)
    return abar_re, abar_im, z_re * b_re - z_im * b_im, z_re * b_im + z_im * b_re


def _full_specs(arrs):
    return [pl.BlockSpec(a.shape, lambda *_, n=len(a.shape): (0,) * n) for a in arrs]


def _ssm_prep(name, raw):
    def body(*refs):
        res = _discretize(*[r[...] for r in refs[:8]])
        for r, v in zip(refs[8:], res):
            r[...] = v

    outs = [jax.ShapeDtypeStruct(raw[0].shape, F32)] * 2 + [jax.ShapeDtypeStruct(raw[6].shape, F32)] * 2
    return _tc_call(body, name=name, in_specs=_full_specs(raw), out_specs=_full_specs(outs), out_shape=outs)(*raw)


def _ssm_prep_bwd(name, raw, cots):
    G = raw[0].shape[0]
    H = raw[3].shape[0] // G

    def body(*refs):
        _, vjp = jax.vjp(_discretize, *[r[...] for r in refs[:8]])
        g = vjp(tuple(r[...] for r in refs[8:12]))
        outs = refs[12:]
        for k in range(3):
            rep = g[3 + k]
            outs[k][...] = g[k] + jnp.sum(rep.reshape(G, H, rep.shape[1]), axis=1)
        outs[3][...] = g[6]
        outs[4][...] = g[7]

    outs = [jax.ShapeDtypeStruct(a.shape, F32) for a in (raw[0], raw[1], raw[2], raw[6], raw[7])]
    return _tc_call(body, name=name, in_specs=_full_specs(list(raw) + list(cots)), out_specs=_full_specs(outs),
                          out_shape=outs)(*raw, *cots)


def _cmul(ar, ai, br, bi):
    return ar * br - ai * bi, ar * bi + ai * br


def _scan_coefs(ar, ai, reverse):
    W = ar.shape[1]
    row = lax.broadcasted_iota(jnp.int32, (8, W), 0)
    p = [None] * 9
    p[1] = (ar, ai)
    for n in range(2, 9):
        p[n] = _cmul(*p[n // 2], *p[n - n // 2])
    steps = []
    for s in (1, 2, 4):
        valid = (row <= 7 - s) if reverse else (row >= s)
        steps.append((jnp.where(valid, p[s][0], 0.0), jnp.where(valid, p[s][1], 0.0)))
    pr = jnp.zeros((8, W), F32)
    pi = jnp.zeros((8, W), F32)
    for i in range(8):
        n = (8 - i) if reverse else (i + 1)
        pr = jnp.where(row == i, p[n][0], pr)
        pi = jnp.where(row == i, p[n][1], pi)
    return steps, (pr, pi)


def _scan_tile(xr, xi, cr, ci, coefs, reverse):
    steps, (pr, pi) = coefs
    for s, (sr, si) in zip((1, 2, 4), steps):
        shift = (8 - s) if reverse else s
        rr = pltpu.roll(xr, shift, axis=0)
        ri = pltpu.roll(xi, shift, axis=0)
        xr, xi = xr + sr * rr - si * ri, xi + sr * ri + si * rr
    xr, xi = xr + pr * cr - pi * ci, xi + pr * ci + pi * cr
    return xr, xi


def _edge_rows(x, reverse):
    W = x.shape[1]
    return jnp.broadcast_to(x[0:1, :] if reverse else x[7:8, :], (8, W))


def _power_table(ar, ai, pw_r, pw_i, n):
    W = ar.shape[1]
    a8r, a8i = jnp.broadcast_to(ar, (8, W)), jnp.broadcast_to(ai, (8, W))
    pr, pi = a8r, a8i
    for k in range(n):
        pw_r[8 * k:8 * k + 8, :] = pr
        pw_i[8 * k:8 * k + 8, :] = pi
        pr, pi = _cmul(pr, pi, a8r, a8i)


def _segment_order(Q):
    r = jnp.arange(Q)
    return (jnp.arange(Q)[None, :] == ((r % 8) * (Q // 8) + r // 8)[:, None]).astype(BF16)


def _to_segments(u, g, pm):
    C = u.shape[1]
    hi = u.astype(BF16)
    lo = (u - hi.astype(F32)).astype(BF16)
    moved = jnp.dot(pm, jnp.concatenate([hi, lo, g], axis=1), preferred_element_type=F32)
    return moved[:, :C].astype(BF16), moved[:, :C] + moved[:, C:2 * C], moved[:, 2 * C:]


def _segment_scan(xr_ref, xi_ref, row0, seg, ar, ai, pw_r, pw_i, carry_r, carry_i, reverse, visit=None, extra=()):
    W = ar.shape[1]
    sign = -1.0 if reverse else 1.0
    a8r, a8i = jnp.broadcast_to(ar, (8, W)), sign * jnp.broadcast_to(ai, (8, W))
    tile = lambda j: pl.ds(pl.multiple_of(row0 + 8 * j, 8), 8)

    def local(jj, x):
        j = seg - 1 - jj if reverse else jj
        xr = a8r * x[0] - a8i * x[1] + xr_ref[tile(j), :]
        xi = a8r * x[1] + a8i * x[0] + xi_ref[tile(j), :]
        xr_ref[tile(j), :] = xr
        xi_ref[tile(j), :] = xi
        return xr, xi

    zero = jnp.zeros((8, W), F32)
    xr, xi = lax.fori_loop(0, seg, local, (zero, zero))
    row = lax.broadcasted_iota(jnp.int32, (8, W), 0)
    edge = 7 if reverse else 0
    shift = 7 if reverse else 1
    gr = jnp.where(row == edge, carry_r, pltpu.roll(xr, shift, axis=0))
    gi = jnp.where(row == edge, carry_i, pltpu.roll(xi, shift, axis=0))
    top = 8 * (seg - 1)
    alr, ali = pw_r[top:top + 1, :], sign * pw_i[top:top + 1, :]
    er, ei = _scan_tile(gr, gi, zero, zero, _scan_coefs(alr, ali, reverse), reverse)
    far_r = _edge_rows(alr * er - ali * ei + xr, reverse)
    far_i = _edge_rows(alr * ei + ali * er + xi, reverse)

    def fix(j, ex):
        k = pl.ds(pl.multiple_of(8 * (seg - 1 - j if reverse else j), 8), 8)
        pr, pi = pw_r[k, :], sign * pw_i[k, :]
        fr = xr_ref[tile(j), :] + pr * er - pi * ei
        fi = xi_ref[tile(j), :] + pr * ei + pi * er
        xr_ref[tile(j), :] = fr
        xi_ref[tile(j), :] = fi
        return ex if visit is None else visit(j, fr, fi, ex)

    extra = lax.fori_loop(0, seg, fix, extra)
    return er, ei, far_r, far_i, extra


def _ssm_chunk(T):
    return _pick(T, (512, 256, 128, 64))


def _ssm_fwd(name, proj, u_off, p, width):
    T = proj.shape[0]
    NB = width // LANE
    Q = _ssm_chunk(T)
    nch = T // Q
    W = SSM_LANES

    def body(u_ref, bre, bim, cre, cim, ar_ref, ai_ref, d_ref, y_ref, ckr_ref, cki_ref, br_s, bi_s, car_r, car_i):
        c = pl.program_id(1)

        @pl.when(c == 0)
        def _():
            car_r[...] = jnp.zeros_like(car_r)
            car_i[...] = jnp.zeros_like(car_i)

        ckr_ref[...] = car_r[...]
        cki_ref[...] = car_i[...]
        u = u_ref[...]
        u16 = u.astype(BF16)
        br_s[...] = jnp.dot(u16, bre[...], preferred_element_type=F32)
        bi_s[...] = jnp.dot(u16, bim[...], preferred_element_type=F32)
        coefs = _scan_coefs(ar_ref[...], ai_ref[...], False)

        def tile(j, carry):
            r0 = pl.multiple_of(j * 8, 8)
            xr, xi = _scan_tile(br_s[pl.ds(r0, 8), :], bi_s[pl.ds(r0, 8), :], carry[0], carry[1], coefs, False)
            br_s[pl.ds(r0, 8), :] = xr
            bi_s[pl.ds(r0, 8), :] = xi
            return _edge_rows(xr, False), _edge_rows(xi, False)

        cr, ci = lax.fori_loop(0, Q // 8, tile, (car_r[...], car_i[...]))
        car_r[...] = cr
        car_i[...] = ci
        nt = (((1,), (1,)), ((), ()))
        y = (lax.dot_general(br_s[...].astype(BF16), cre[...], nt, preferred_element_type=F32)
             - lax.dot_general(bi_s[...].astype(BF16), cim[...], nt, preferred_element_type=F32)
             + d_ref[...] * u)
        y_ref[...] = _gelu(y).astype(y_ref.dtype)

    blk = lambda b, c: (b, 0, 0)
    mat = pl.BlockSpec((None, LANE, W), blk)
    vec = pl.BlockSpec((None, 1, W), blk)
    ck = pl.BlockSpec((None, None, 8, W), lambda b, c: (b, c, 0, 0))
    return _tc_call(
        body, name=name, grid=(NB, nch),
        in_specs=[pl.BlockSpec((Q, LANE), lambda b, c: (c, u_off + b)), mat, mat, mat, mat, vec, vec,
                  pl.BlockSpec((1, LANE), lambda b, c: (0, b))],
        out_specs=[pl.BlockSpec((Q, LANE), lambda b, c: (c, b)), ck, ck],
        out_shape=[jax.ShapeDtypeStruct((T, width), BF16), jax.ShapeDtypeStruct((NB, nch, 8, W), F32),
                   jax.ShapeDtypeStruct((NB, nch, 8, W), F32)],
        scratch_shapes=[pltpu.VMEM((Q, W), F32), pltpu.VMEM((Q, W), F32), pltpu.VMEM((8, W), F32),
                        pltpu.VMEM((8, W), F32)],
        compiler_params=_params(("parallel", "arbitrary")),
    )(proj, p["bre"], p["bim"], p["cre"], p["cim"], p["ar"], p["ai"], p["d"])


def _ssm_bwd(name, proj, u_off, p, ck_r, ck_i, dyg, width):
    T = proj.shape[0]
    NB = width // LANE
    Q = _ssm_chunk(T)
    nch = T // Q
    W = SSM_LANES
    nt_dims = (((1,), (1,)), ((), ()))
    tn_dims = (((0,), (0,)), ((), ()))

    def body(u_ref, dy_ref, pm_ref, pmt_ref, ckr_ref, cki_ref, bre, bim, cre, cim, ar_ref, ai_ref, d_ref,
             du_ref, dbr_ref, dbi_ref, dcr_ref, dci_ref, dar_ref, dai_ref, dd_ref,
             xr_s, xi_s, lr_s, li_s, lam_r, lam_i, pw_r, pw_i):
        c = pl.program_id(1)
        seg = Q // 8
        ar, ai = ar_ref[...], ai_ref[...]

        @pl.when(c == 0)
        def _():
            lam_r[...] = jnp.zeros_like(lam_r)
            lam_i[...] = jnp.zeros_like(lam_i)
            for r in (dbr_ref, dbi_ref, dcr_ref, dci_ref, dar_ref, dai_ref, dd_ref):
                r[...] = jnp.zeros_like(r)
            _power_table(ar, ai, pw_r, pw_i, seg)

        u16, u, dyg = _to_segments(u_ref[...], dy_ref[...], pm_ref[...])
        xr_s[8:Q + 8, :] = jnp.dot(u16, bre[...], preferred_element_type=F32)
        xi_s[8:Q + 8, :] = jnp.dot(u16, bim[...], preferred_element_type=F32)
        enter_r, enter_i, _, _, _ = _segment_scan(xr_s, xi_s, 8, seg, ar, ai, pw_r, pw_i, ckr_ref[...], cki_ref[...],
                                                  False)
        xr_s[0:8, :] = enter_r
        xi_s[0:8, :] = enter_i
        xr16 = xr_s[8:Q + 8, :].astype(BF16)
        xi16 = xi_s[8:Q + 8, :].astype(BF16)
        y = (lax.dot_general(xr16, cre[...], nt_dims, preferred_element_type=F32)
             - lax.dot_general(xi16, cim[...], nt_dims, preferred_element_type=F32) + d_ref[...] * u)
        _, gelu_vjp = jax.vjp(_gelu, y)
        dy = gelu_vjp(dyg)[0]
        dy16 = dy.astype(BF16)
        dd_ref[...] += jnp.broadcast_to(jnp.sum(dy * u, axis=0, keepdims=True), (8, LANE))
        dcr_ref[...] += lax.dot_general(dy16, xr16, tn_dims, preferred_element_type=F32)
        dci_ref[...] -= lax.dot_general(dy16, xi16, tn_dims, preferred_element_type=F32)
        lr_s[...] = jnp.dot(dy16, cre[...], preferred_element_type=F32)
        li_s[...] = -jnp.dot(dy16, cim[...], preferred_element_type=F32)

        def visit(j, lr, li, sums):
            before = pl.ds(pl.multiple_of(8 * j, 8), 8)
            xpr, xpi = xr_s[before, :], xi_s[before, :]
            return sums[0] + lr * xpr + li * xpi, sums[1] + li * xpr - lr * xpi

        zero = jnp.zeros((8, W), F32)
        _, _, cr, ci, sums = _segment_scan(lr_s, li_s, 0, seg, ar, ai, pw_r, pw_i, lam_r[...], lam_i[...], True, visit,
                                           (zero, zero))
        lam_r[...] = cr
        lam_i[...] = ci
        dar_ref[...] += jnp.broadcast_to(jnp.sum(sums[0], axis=0, keepdims=True), (8, W))
        dai_ref[...] += jnp.broadcast_to(jnp.sum(sums[1], axis=0, keepdims=True), (8, W))
        lr16 = lr_s[...].astype(BF16)
        li16 = li_s[...].astype(BF16)
        dbr_ref[...] += lax.dot_general(u16, lr16, tn_dims, preferred_element_type=F32)
        dbi_ref[...] += lax.dot_general(u16, li16, tn_dims, preferred_element_type=F32)
        du = (lax.dot_general(lr16, bre[...], nt_dims, preferred_element_type=F32)
              + lax.dot_general(li16, bim[...], nt_dims, preferred_element_type=F32) + d_ref[...] * dy)
        du_ref[...] = jnp.dot(pmt_ref[...], du.astype(BF16), preferred_element_type=F32).astype(du_ref.dtype)

    blk = lambda b, c: (b, 0, 0)
    mat = pl.BlockSpec((None, LANE, W), blk)
    vec = pl.BlockSpec((None, 1, W), blk)
    acc8 = pl.BlockSpec((None, 8, W), blk)
    ck = pl.BlockSpec((None, None, 8, W), lambda b, c: (b, nch - 1 - c, 0, 0))
    order = pl.BlockSpec((Q, Q), lambda b, c: (0, 0))
    pm = _segment_order(Q)
    return _tc_call(
        body, name=name, grid=(NB, nch),
        in_specs=[pl.BlockSpec((Q, LANE), lambda b, c: (nch - 1 - c, u_off + b)),
                  pl.BlockSpec((Q, LANE), lambda b, c: (nch - 1 - c, b)), order, order, ck, ck, mat, mat, mat, mat,
                  vec, vec, pl.BlockSpec((1, LANE), lambda b, c: (0, b))],
        out_specs=[pl.BlockSpec((Q, LANE), lambda b, c: (nch - 1 - c, b)), mat, mat, mat, mat, acc8, acc8,
                   pl.BlockSpec((None, 8, LANE), blk)],
        out_shape=[jax.ShapeDtypeStruct((T, width), BF16)] + [jax.ShapeDtypeStruct((NB, LANE, W), F32)] * 4
                  + [jax.ShapeDtypeStruct((NB, 8, W), F32)] * 2 + [jax.ShapeDtypeStruct((NB, 8, LANE), F32)],
        scratch_shapes=[pltpu.VMEM((Q + 8, W), F32), pltpu.VMEM((Q + 8, W), F32), pltpu.VMEM((Q, W), F32),
                        pltpu.VMEM((Q, W), F32), pltpu.VMEM((8, W), F32), pltpu.VMEM((8, W), F32),
                        pltpu.VMEM((Q, W), F32), pltpu.VMEM((Q, W), F32)],
        compiler_params=_params(("parallel", "arbitrary")),
    )(proj, dyg, pm, pm.T, ck_r, ck_i, p["bre"], p["bim"], p["cre"], p["cim"], p["ar"], p["ai"], p["d"])


def _block_diag(w):
    G, H, P = w.shape
    eye = jnp.eye(8, dtype=w.dtype)
    return (w.reshape(G // 8, 8, H, 1, P) * eye[None, :, None, :, None]).reshape(G // 8, 8 * H, 8 * P)


def _block_diag_t(d, H, P):
    NB = d.shape[0]
    d = d.reshape(NB, 8, H, 8, P)
    eye = jnp.eye(8, dtype=d.dtype)
    return jnp.sum(d * eye[None, :, None, :, None], axis=3).reshape(NB * 8, H, P)


def _attn_fwd(name, q, kv, heads):
    T, D = q.shape
    Mm = kv.shape[0]
    hd = D // heads
    tq = _pick(T, (512, 256, 128))
    scale = hd ** -0.5

    def body(q_ref, k_ref, v_ref, o_ref):
        s = lax.dot_general(q_ref[...], k_ref[...], (((1,), (1,)), ((), ())), preferred_element_type=F32) * scale
        s = s - jnp.max(s, axis=-1, keepdims=True)
        e = jnp.exp(s)
        p = e / jnp.sum(e, axis=-1, keepdims=True)
        o_ref[...] = jnp.dot(p.astype(BF16), v_ref[...], preferred_element_type=F32).astype(o_ref.dtype)

    return _tc_call(
        body, name=name, grid=(heads, T // tq),
        in_specs=[pl.BlockSpec((tq, hd), lambda h, i: (i, h)), pl.BlockSpec((Mm, hd), lambda h, i: (0, h)),
                  pl.BlockSpec((Mm, hd), lambda h, i: (0, heads + h))],
        out_specs=pl.BlockSpec((tq, hd), lambda h, i: (i, h)),
        out_shape=jax.ShapeDtypeStruct((T, D), BF16),
        compiler_params=_params(("parallel", "parallel")),
    )(q, kv, kv)


def _attn_bwd(name, q, kv, do, heads):
    T, D = q.shape
    Mm = kv.shape[0]
    hd = D // heads
    tq = _pick(T, (512, 256, 128))
    scale = hd ** -0.5
    nt_dims = (((1,), (1,)), ((), ()))
    tn_dims = (((0,), (0,)), ((), ()))

    def body(q_ref, k_ref, v_ref, do_ref, dq_ref, dk_ref, dv_ref):
        i = pl.program_id(1)

        @pl.when(i == 0)
        def _():
            dk_ref[...] = jnp.zeros_like(dk_ref)
            dv_ref[...] = jnp.zeros_like(dv_ref)

        qv, kvl, vv, dov = q_ref[...], k_ref[...], v_ref[...], do_ref[...]
        s = lax.dot_general(qv, kvl, nt_dims, preferred_element_type=F32) * scale
        s = s - jnp.max(s, axis=-1, keepdims=True)
        e = jnp.exp(s)
        p = e / jnp.sum(e, axis=-1, keepdims=True)
        p16 = p.astype(BF16)
        dv_ref[...] += lax.dot_general(p16, dov, tn_dims, preferred_element_type=F32)
        dp = lax.dot_general(dov, vv, nt_dims, preferred_element_type=F32)
        ds = (p * (dp - jnp.sum(dp * p, axis=-1, keepdims=True)) * scale).astype(BF16)
        dq_ref[...] = jnp.dot(ds, kvl, preferred_element_type=F32).astype(dq_ref.dtype)
        dk_ref[...] += lax.dot_general(ds, qv, tn_dims, preferred_element_type=F32)

    return _tc_call(
        body, name=name, grid=(heads, T // tq),
        in_specs=[pl.BlockSpec((tq, hd), lambda h, i: (i, h)), pl.BlockSpec((Mm, hd), lambda h, i: (0, h)),
                  pl.BlockSpec((Mm, hd), lambda h, i: (0, heads + h)), pl.BlockSpec((tq, hd), lambda h, i: (i, h))],
        out_specs=[pl.BlockSpec((tq, hd), lambda h, i: (i, h)), pl.BlockSpec((Mm, hd), lambda h, i: (0, h)),
                   pl.BlockSpec((Mm, hd), lambda h, i: (0, h))],
        out_shape=[jax.ShapeDtypeStruct((T, D), BF16), jax.ShapeDtypeStruct((Mm, D), F32),
                   jax.ShapeDtypeStruct((Mm, D), F32)],
        compiler_params=_params(("parallel", "arbitrary")),
    )(q, kv, kv, do)


def _position():
    return lax.axis_index("x"), lax.axis_index("y"), lax.axis_index("c")


def _allgather(name, blk, row_mode):
    L = blk.shape[0]
    out_shape = (L, N_DEV) + blk.shape[1:] if row_mode else (N_DEV,) + blk.shape
    x_ref = jax.new_ref(blk, memory_space=pltpu.MemorySpace.HBM)
    out_ref = jax.empty_ref(jax.ShapeDtypeStruct(out_shape, blk.dtype), memory_space=pltpu.MemorySpace.HBM)

    def body(send_sems, recv_sems, local_sem):
        x, y, c = _position()
        me, sibling = (x, y, c), (x, y, 1 - c)
        chips = [(1 - x, y), (x, 1 - y), (1 - x, 1 - y)]
        barrier = pltpu.get_barrier_semaphore()
        for peer in [sibling] + [(*chip, c) for chip in chips]:
            pl.semaphore_signal(barrier, inc=1, device_id=peer, device_id_type=MESH_ID)
        pl.semaphore_wait(barrier, 4)

        def slot(px, py, pc):
            b = 4 * px + 2 * py + pc
            return out_ref.at[:, b] if row_mode else out_ref.at[b]

        def copy(k, block, to, src=None):
            return pltpu.make_async_remote_copy(
                src_ref=slot(*block) if src is None else src, dst_ref=slot(*block),
                send_sem=send_sems.at[k], recv_sem=recv_sems.at[k], device_id=to, device_id_type=MESH_ID)

        mine = pltpu.make_async_copy(x_ref, slot(*me), local_sem)
        mine.start()
        first = [copy(0, me, sibling, src=x_ref)]
        first += [copy(1 + j, me, (*chip, c), src=x_ref) for j, chip in enumerate(chips)]
        for cp in first:
            cp.start()
        passed = [copy(4 + j, (*chip, c), sibling) for j, chip in enumerate(chips)]
        for j, chip in enumerate(chips):
            copy(1 + j, (*chip, c), me).wait_recv()
            passed[j].start()
        copy(0, sibling, me).wait_recv()
        for j, chip in enumerate(chips):
            copy(4 + j, (*chip, 1 - c), me).wait_recv()
        for cp in first + passed:
            cp.wait_send()
        mine.wait()

    pl.kernel(
        body, mesh=plsc.ScalarSubcoreMesh(axis_name="sequencer", num_cores=1), name=name,
        scratch_types=(pltpu.SemaphoreType.DMA((7,)), pltpu.SemaphoreType.DMA((7,)), pltpu.SemaphoreType.DMA),
        compiler_params=pltpu.CompilerParams(collective_id=AG_COLLECTIVE_ID),
    )()
    return out_ref[...]


def _sequencer_kernel(name, body, scratch_types, collective_id):
    pl.kernel(
        body, mesh=plsc.ScalarSubcoreMesh(axis_name="sequencer", num_cores=1), name=name,
        scratch_types=scratch_types, compiler_params=pltpu.CompilerParams(collective_id=collective_id),
    )()


def _handshake(peers):
    barrier = pltpu.get_barrier_semaphore()
    for peer in peers:
        pl.semaphore_signal(barrier, inc=1, device_id=peer, device_id_type=MESH_ID)
    pl.semaphore_wait(barrier, len(peers))


def _rs_sibling(name, grads):
    hbm = pltpu.MemorySpace.HBM
    g_ref = jax.new_ref(grads, memory_space=hbm)
    out_ref = jax.empty_ref(jax.ShapeDtypeStruct((4,) + grads.shape[1:], grads.dtype), memory_space=hbm)

    def body(send_sems, recv_sems):
        x, y, c = _position()
        sibling = (x, y, 1 - c)
        _handshake([sibling])
        copies = [pltpu.make_async_remote_copy(
            src_ref=g_ref.at[2 * q + (1 - c)], dst_ref=out_ref.at[q], send_sem=send_sems.at[q],
            recv_sem=recv_sems.at[q], device_id=sibling, device_id_type=MESH_ID) for q in range(4)]
        for cp in copies:
            cp.start()
        for cp in copies:
            cp.wait_recv()
        for cp in copies:
            cp.wait_send()

    _sequencer_kernel(name, body, (pltpu.SemaphoreType.DMA((4,)), pltpu.SemaphoreType.DMA((4,))),
                      RS_SIBLING_COLLECTIVE_ID)
    return out_ref[...]


def _rs_chips(name, part):
    hbm = pltpu.MemorySpace.HBM
    p_ref = jax.new_ref(part, memory_space=hbm)
    out_ref = jax.empty_ref(jax.ShapeDtypeStruct(part.shape, part.dtype), memory_space=hbm)

    def body(send_sems, recv_sems, local_sem):
        x, y, c = _position()
        flips = [(1 - x, y), (x, 1 - y), (1 - x, 1 - y)]
        _handshake([(fx, fy, c) for fx, fy in flips])
        mine = pltpu.make_async_copy(p_ref.at[2 * x + y], out_ref.at[3], local_sem)
        copies = [pltpu.make_async_remote_copy(
            src_ref=p_ref.at[2 * fx + fy], dst_ref=out_ref.at[k], send_sem=send_sems.at[k], recv_sem=recv_sems.at[k],
            device_id=(fx, fy, c), device_id_type=MESH_ID) for k, (fx, fy) in enumerate(flips)]
        for cp in copies:
            cp.start()
        mine.start()
        for cp in copies:
            cp.wait_recv()
        for cp in copies:
            cp.wait_send()
        mine.wait()

    _sequencer_kernel(name, body, (pltpu.SemaphoreType.DMA((3,)), pltpu.SemaphoreType.DMA((3,)),
                                   pltpu.SemaphoreType.DMA), RS_CHIPS_COLLECTIVE_ID)
    return out_ref[...]


def _adamw_math(g, w, m, v):
    m = ADAM_B1 * m + (1.0 - ADAM_B1) * g
    v = ADAM_B2 * v + (1.0 - ADAM_B2) * (g * g)
    m_hat = m / (1.0 - ADAM_B1 ** ADAM_STEP)
    v_hat = v / (1.0 - ADAM_B2 ** ADAM_STEP)
    delta = -ADAM_LR * (m_hat / (jnp.sqrt(v_hat) + ADAM_EPS) + ADAM_WD * w)
    return delta, m, v


def _adamw_natural(name, gs, ws, ms, vs, lead_block=None):
    n = len(gs)

    def body(*refs):
        ins, outs = refs[:4 * n], refs[4 * n:]
        for k in range(n):
            delta, nm, nv = _adamw_math(*[ins[j * n + k][...] for j in range(4)])
            outs[k][...] = delta
            outs[n + k][...] = nm
            outs[2 * n + k][...] = nv

    out_shape = [jax.ShapeDtypeStruct(w.shape, F32) for w in ws] * 3
    if lead_block is None:
        grid = ()
        spec = lambda a: pl.BlockSpec(a.shape, lambda nd=len(a.shape): (0,) * nd)
    else:
        grid = tuple(d // b for d, b in zip(ws[0].shape[:2], lead_block))
        spec = lambda a: pl.BlockSpec(tuple(lead_block) + tuple(a.shape[2:]),
                                      lambda i, j, nd=len(a.shape): (i, j) + (0,) * (nd - 2))
    return _tc_call(
        body, name=name, grid=grid, in_specs=[spec(a) for a in list(gs) + list(ws) + list(ms) + list(vs)],
        out_specs=[spec(o) for o in out_shape], out_shape=out_shape,
        compiler_params=pltpu.CompilerParams(vmem_limit_bytes=VMEM_LIMIT),
    )(*gs, *ws, *ms, *vs)


def _ew_tile(R, C, nblocks):
    budget = (VMEM_LIMIT * 3) // 4
    return _pick(R, tuple(t for t in (1024, 512, 256, 128, 64, 32, 16, 8) if 8 * t * C * nblocks <= budget))


def _pair_sum(name, grads, landed, c_idx):
    _, r, c = grads.shape
    tile = _ew_tile(r, c, 3)

    def body(c_ref, g_ref, s_ref, o_ref):
        o_ref[...] = (g_ref[...].astype(F32) + s_ref[...].astype(F32)).astype(o_ref.dtype)

    return _tc_call(
        body, name=name,
        grid_spec=pltpu.PrefetchScalarGridSpec(
            num_scalar_prefetch=1, grid=(4, r // tile),
            in_specs=[pl.BlockSpec((None, tile, c), lambda q, i, c_ref: (2 * q + c_ref[0], i, 0)),
                      pl.BlockSpec((None, tile, c), lambda q, i, c_ref: (q, i, 0))],
            out_specs=pl.BlockSpec((None, tile, c), lambda q, i, c_ref: (q, i, 0))),
        out_shape=jax.ShapeDtypeStruct((4, r, c), BF16),
        compiler_params=_params(("parallel", "parallel")),
    )(c_idx, grads, landed)


def _adamw_layer(name, got, w, m, v, layer, prev):
    L, r, c = w.shape
    tile = _ew_tile(r, c, 11)

    def body(g0, g1, g2, g3, w_ref, m_ref, v_ref, *rest):
        outs = rest[-4:]
        g = (g0[...].astype(F32) + g1[...].astype(F32)) + (g2[...].astype(F32) + g3[...].astype(F32))
        delta, nm, nv = _adamw_math(g, w_ref[...], m_ref[...], v_ref[...])
        for ref, val in zip(outs, (g, delta, nm, nv)):
            ref[...] = val

    slab = pl.BlockSpec((None, tile, c), lambda i: (layer, i, 0))
    in_specs = [pl.BlockSpec((None, tile, c), lambda i, k=k: (k, i, 0)) for k in range(4)] + [slab] * 3
    args = [got, got, got, got, w, m, v]
    aliases = {}
    if prev is not None:
        in_specs += [ANY] * 4
        args += list(prev)
        aliases = {7 + k: k for k in range(4)}
    return _tc_call(
        body, name=name, grid=(r // tile,), in_specs=in_specs, out_specs=[slab] * 4,
        out_shape=[jax.ShapeDtypeStruct((L, r, c), F32)] * 4, input_output_aliases=aliases,
        compiler_params=_params(("parallel",)),
    )(*args)


def _reduce_pipeline(name, layer, grads, w, m, v, state):
    c_idx = lax.axis_index("c").astype(jnp.int32).reshape(1)
    landed = _rs_sibling(name + "_rs_sibling", grads)
    yield
    part = _pair_sum(name + "_pair_sum", grads, landed, c_idx)
    got = _rs_chips(name + "_rs_chips", part)
    for _ in range(4):
        yield
    state[name] = _adamw_layer(name + "_adamw", got, w, m, v, layer, state.get(name))


def _loss_head(name, x, g, target):
    T, D = x.shape

    def fn(xv, tv, gv):
        def f(xx, gg):
            err = _rms(xx, gg) - tv
            return 0.5 * jnp.sum(jnp.mean(err * err, axis=-1, keepdims=True))

        loss, (dx, dg) = jax.value_and_grad(f, argnums=(0, 1))(xv, gv)
        return dx, dx, jnp.full((8, LANE), loss, F32), jnp.sum(dg, axis=0, keepdims=True)

    dx, dx16, loss, dg = _rowwise(name, fn, [(x, D, 0, 0), (target, D, 0, 0)], [g], [(D, F32), (D, BF16)],
                                  [(8, LANE), (1, D)], tile=128)
    return loss[0, 0], dx, dx16, dg


_SHARDED_COL = ("w_in", "conv_dw_w", "conv_w_pw", "ssm_w_glu", "xa_w_kv", "ffn_w_up", "ffn_dw_w")
_SHARDED_ROW = ("w_out", "xa_w_q", "xa_w_o", "ffn_w_down")
_WEIGHTS = ['mix_norm_g', 'w_in', 'conv_dw_w', 'conv_dw_b', 'conv_ln_g', 'conv_ln_b', 'conv_w_pw', 'ssm_a_re',
            'ssm_a_im', 'ssm_log_dt', 'ssm_b_re', 'ssm_b_im', 'ssm_c_re', 'ssm_c_im', 'ssm_d', 'ssm_w_glu', 'w_out',
            'xa_norm_g', 'mem_norm_g', 'xa_w_q', 'xa_w_kv', 'xa_w_o', 'ffn_norm_g', 'ffn_w_up', 'ffn_dw_w',
            'ffn_w_down', 'final_norm_g']
_FWD = ['x', 'mem'] + _WEIGHTS
_AG_ORDER = ("w_in", "conv_dw_w", "conv_w_pw", "ssm_w_glu", "w_out", "xa_w_q", "xa_w_kv", "xa_w_o", "ffn_w_up",
             "ffn_dw_w", "ffn_w_down")


def _pad_rows(a, rows):
    return jnp.pad(a, ((0, 0), (0, rows - a.shape[1]), (0, 0)))


def _step(inp, target, mom_m, mom_v):
    _LAST_CALL.clear()
    x0 = inp["x"][0]
    mem = inp["mem"][0]
    T, D = x0.shape
    L = inp["w_in"].shape[0]
    CW = inp["conv_dw_b"].shape[1]
    SW = inp["ssm_d"].shape[1]
    DFF = inp["ffn_w_down"].shape[1] * N_DEV
    G = SW // SSM_GROUP
    NB = SW // LANE
    u_off = (2 * CW) // LANE
    gate_off = (2 * CW + SW) // 1024

    gathered = {n: [None] * L for n in _AG_ORDER}
    filter_rows = {"conv_dw_w": HALO, "ffn_dw_w": 8}
    for l in range(L):
        for n in _AG_ORDER:
            blk = inp[n][l:l + 1]
            blk = _pad_rows(blk, filter_rows[n]) if n in filter_rows else blk.astype(BF16)
            if n in _SHARDED_COL:
                gathered[n][l] = _allgather("ag_" + n, blk, False)
            else:
                full = _allgather("ag_" + n, blk, True)
                gathered[n][l] = full.reshape(1, N_DEV * full.shape[2], full.shape[3])

    def W(n, l):
        return _W(gathered[n][l], 0, n in _SHARDED_COL)

    def dw_filter(n, l):
        g = gathered[n][l]
        return jnp.transpose(g[:, 0], (1, 0, 2)).reshape(g.shape[2], N_DEV * g.shape[3])

    def row(n, l):
        return (inp[n], l)

    ssm_raw, ssm_p = [], []
    for l in range(L):
        a_re, a_im, ldt = inp["ssm_a_re"][l], inp["ssm_a_im"][l], inp["ssm_log_dt"][l][:, None]
        rep = lambda a: jnp.repeat(a, SSM_GROUP, axis=0)
        flat = lambda b: jnp.transpose(b, (0, 2, 1)).reshape(G * SSM_GROUP, SSM_STATE)
        raw = (a_re, a_im, ldt, rep(a_re), rep(a_im), rep(ldt), flat(inp["ssm_b_re"][l]), flat(inp["ssm_b_im"][l]))
        abar_re, abar_im, bbar_re, bbar_im = _ssm_prep("ssm_prep", raw)
        bbar_re = bbar_re.reshape(G, SSM_GROUP, SSM_STATE)
        bbar_im = bbar_im.reshape(G, SSM_GROUP, SSM_STATE)
        ssm_raw.append(raw)
        ssm_p.append(dict(
            bre=_block_diag(bbar_re).astype(BF16), bim=_block_diag(bbar_im).astype(BF16),
            cre=_block_diag(inp["ssm_c_re"][l]).astype(BF16), cim=_block_diag(inp["ssm_c_im"][l]).astype(BF16),
            ar=abar_re.reshape(NB, 1, SSM_LANES), ai=abar_im.reshape(NB, 1, SSM_LANES), d=inp["ssm_d"][l][None, :]))

    saved = []
    x = x0
    for l in range(L):
        s = {"x_in": x}
        s["h1"] = _rms_fwd("rms_mix", x, row("mix_norm_g", l))
        s["proj"] = _mm_nn("mm_w_in", s["h1"], W("w_in", l), F32)
        s["hc"] = _glu_conv_fwd("conv_fwd", s["proj"], dw_filter("conv_dw_w", l), CW)
        s["hs"] = _rowwise("conv_post", _convpost, [(s["hc"], CW, 0, 0)],
                           [row("conv_dw_b", l), row("conv_ln_g", l), row("conv_ln_b", l)], [(CW, BF16)])[0]
        s["ya"] = _mm_nn("mm_w_pw", s["hs"], W("conv_w_pw", l), F32)
        s["yg"], s["ck_r"], s["ck_i"] = _ssm_fwd("ssm_fwd", s["proj"], u_off, ssm_p[l], SW)
        s["gg"] = _mm_nn("mm_w_glu", s["yg"], W("ssm_w_glu", l), F32)
        nmix = D // 1024
        mix_rows = [(s["proj"], 1024, gate_off, 0), (s["proj"], 1024, gate_off + nmix, 0), (s["ya"], 1024, 0, 0),
                    (s["gg"], 1024, 0, 0), (s["gg"], 1024, nmix, 0)]
        s["mix_rows"] = mix_rows
        s["mix"] = _rowwise("mix_fwd", _mixf, mix_rows, [], [(D, BF16)], ncol=nmix)[0]
        x = _mm_nn("mm_w_out", s["mix"], W("w_out", l), F32, add=x)
        s["x1"] = x
        s["h2"] = _rms_fwd("rms_xa", x, row("xa_norm_g", l))
        s["q"] = _mm_nn("mm_w_q", s["h2"], W("xa_w_q", l), BF16)
        s["mn"] = _rms_fwd("rms_mem", mem, row("mem_norm_g", l))
        s["kv"] = _mm_nn("mm_w_kv", s["mn"], W("xa_w_kv", l), BF16)
        s["o"] = _attn_fwd("attn_fwd", s["q"], s["kv"], XA_HEADS)
        x = _mm_nn("mm_w_o", s["o"], W("xa_w_o", l), F32, add=x)
        s["x2"] = x
        s["h3"] = _rms_fwd("rms_ffn", x, row("ffn_norm_g", l))
        s["up"] = _mm_nn("mm_w_up", s["h3"], W("ffn_w_up", l), F32)
        s["act"] = _ffn_conv_fwd("ffn_conv_fwd", s["up"], dw_filter("ffn_dw_w", l), DFF)
        x = _mm_nn("mm_w_down", s["act"], W("ffn_w_down", l), F32, add=x)
        saved.append(s)

    loss_part, dx, dx16, d_final_g = _loss_head("loss_head", x, inp["final_norm_g"][None, :], target[0])

    big = _SHARDED_COL + _SHARDED_ROW
    small = {n: [None] * L for n in _WEIGHTS if n not in big and n != "final_norm_g"}
    pads = {"conv_dw_w": HALO, "ffn_dw_w": 8}
    shards = {n: tuple(_pad_rows(a, pads[n]) if n in pads else a for a in (inp[n], mom_m[n], mom_v[n])) for n in big}
    state, queue = {}, []
    names = [n for n in _WEIGHTS if n not in big]
    sizes = [inp[n].size for n in names]
    total = sum(sizes)
    pack_w = 8 * LANE
    rows_p = -(-total // (LANE * pack_w)) * LANE
    padn = rows_p * pack_w - total

    def pack(parts, fill):
        return jnp.concatenate([p.reshape(-1) for p in parts] + [jnp.full((padn,), fill, F32)]).reshape(rows_p, pack_w)

    def tick():
        for gen in list(queue):
            if next(gen, "done") == "done":
                queue.remove(gen)

    def emit(n, l, g):
        queue.append(_reduce_pipeline(n, l, g.astype(BF16), *shards[n], state))
        tick()

    deferred = []

    def emit_small(n, l, thunk):
        if l == 0:
            deferred.append((n, thunk))
        else:
            emit(n, l, thunk())

    for l in reversed(range(L)):
        s = saved[l]
        dact = _mm_nt("mm_w_down_t", dx16, W("ffn_w_down", l), BF16)
        emit("ffn_w_down", l, _mm_tn("mm_dw_down", s["act"], dx16).reshape(N_DEV, DFF // N_DEV, D))
        d_up, d_ffn_dw = _ffn_conv_bwd("ffn_conv_bwd", s["up"], dw_filter("ffn_dw_w", l), dact, DFF)
        emit("ffn_dw_w", l, jnp.transpose(d_ffn_dw.reshape(8, N_DEV, 2 * DFF // N_DEV), (1, 0, 2)))
        dh3 = _mm_nt("mm_w_up_t", d_up, W("ffn_w_up", l), BF16)
        emit("ffn_w_up", l, _mm_tn("mm_dw_up", s["h3"], d_up, nb=2 * DFF // N_DEV))
        dx, dx16, small["ffn_norm_g"][l] = _rms_bwd("rms_ffn_bwd", s["x2"], row("ffn_norm_g", l), dh3, dx)
        do = _mm_nt("mm_w_o_t", dx16, W("xa_w_o", l), BF16)
        emit_small("xa_w_o", l, lambda a=s["o"], b=dx16: _mm_tn("mm_dw_o", a, b).reshape(N_DEV, D // N_DEV, D))
        dq, dk, dv = _attn_bwd("attn_bwd", s["q"], s["kv"], do, XA_HEADS)
        dkv = jnp.concatenate([dk, dv], axis=1).astype(BF16)
        dh2 = _mm_nt("mm_w_q_t", dq, W("xa_w_q", l), BF16)
        emit_small("xa_w_q", l, lambda a=s["h2"], b=dq: _mm_tn("mm_dw_q", a, b).reshape(N_DEV, D // N_DEV, D))
        dmn = _mm_nt("mm_w_kv_t", dkv, W("xa_w_kv", l), BF16)
        emit("xa_w_kv", l, _mm_tn("mm_dw_kv", s["mn"], dkv, nb=2 * D // N_DEV))

        def mem_bwd(mv, dv_, gv):
            _, vjp = jax.vjp(_rms, mv, gv)
            return jnp.sum(vjp(dv_.astype(F32))[1], axis=0, keepdims=True)

        small["mem_norm_g"][l] = _rowwise("rms_mem_bwd", mem_bwd, [(mem, D, 0, 0), (dmn, D, 0, 0)],
                                          [row("mem_norm_g", l)], [], [(1, D)], tile=128)[0]
        dx, dx16, small["xa_norm_g"][l] = _rms_bwd("rms_xa_bwd", s["x1"], row("xa_norm_g", l), dh2, dx)
        dmix = _mm_nt("mm_w_out_t", dx16, W("w_out", l), BF16)
        emit_small("w_out", l, lambda a=s["mix"], b=dx16: _mm_tn("mm_dw_out", a, b).reshape(N_DEV, D // N_DEV, D))

        def mix_bwd(gla, glb, ya, ga, gb, dm):
            _, vjp = jax.vjp(_mixf, gla, glb, ya, ga, gb)
            return vjp(dm.astype(F32))

        nmix = D // 1024
        dgla, dglb, dya, dga, dgb = _rowwise("mix_bwd", mix_bwd, s["mix_rows"] + [(dmix, 1024, 0, 0)], [],
                                             [(D, BF16)] * 5, ncol=nmix)
        dgg = jnp.concatenate([dga, dgb], axis=1)
        dyg = _mm_nt("mm_w_glu_t", dgg, W("ssm_w_glu", l), BF16)
        emit_small("ssm_w_glu", l, lambda a=s["yg"], b=dgg: _mm_tn("mm_dw_glu", a, b, nb=2 * D // N_DEV))
        du, dbr, dbi, dcr, dci, dar, dai, dd = _ssm_bwd("ssm_bwd", s["proj"], u_off, ssm_p[l], s["ck_r"], s["ck_i"],
                                                        dyg, SW)
        cots = (dar[:, 0, :].reshape(G, SSM_STATE), dai[:, 0, :].reshape(G, SSM_STATE),
                _block_diag_t(dbr, SSM_GROUP, SSM_STATE).reshape(G * SSM_GROUP, SSM_STATE),
                _block_diag_t(dbi, SSM_GROUP, SSM_STATE).reshape(G * SSM_GROUP, SSM_STATE))
        g_are, g_aim, g_ldt, g_bre, g_bim = _ssm_prep_bwd("ssm_prep_bwd", ssm_raw[l], cots)
        small["ssm_a_re"][l], small["ssm_a_im"][l], small["ssm_log_dt"][l] = g_are, g_aim, g_ldt[:, 0]
        small["ssm_b_re"][l] = jnp.transpose(g_bre.reshape(G, SSM_GROUP, SSM_STATE), (0, 2, 1))
        small["ssm_b_im"][l] = jnp.transpose(g_bim.reshape(G, SSM_GROUP, SSM_STATE), (0, 2, 1))
        small["ssm_c_re"][l] = _block_diag_t(dcr, SSM_GROUP, SSM_STATE)
        small["ssm_c_im"][l] = _block_diag_t(dci, SSM_GROUP, SSM_STATE)
        small["ssm_d"][l] = dd[:, 0, :].reshape(SW)
        dhs = _mm_nt("mm_w_pw_t", dya, W("conv_w_pw", l), BF16)
        emit("conv_w_pw", l, _mm_tn("mm_dw_pw", s["hs"], dya, nb=D // N_DEV))

        def post_bwd(hc, dh, b, lg, lb):
            _, vjp = jax.vjp(_convpost, hc, b, lg, lb)
            dhc, db, dlg, dlb = vjp(dh.astype(F32))
            return dhc, jnp.sum(db, axis=0, keepdims=True), jnp.sum(dlg, axis=0, keepdims=True), \
                jnp.sum(dlb, axis=0, keepdims=True)

        dhc, small["conv_dw_b"][l], small["conv_ln_g"][l], small["conv_ln_b"][l] = _rowwise(
            "conv_post_bwd", post_bwd, [(s["hc"], CW, 0, 0), (dhs, CW, 0, 0)],
            [row("conv_dw_b", l), row("conv_ln_g", l), row("conv_ln_b", l)], [(CW, F32)], [(1, CW)] * 3)
        da, db, d_conv_dw = _glu_conv_bwd("conv_bwd", s["proj"], dw_filter("conv_dw_w", l), dhc, CW)
        emit("conv_dw_w", l, jnp.transpose(d_conv_dw.reshape(HALO, N_DEV, CW // N_DEV), (1, 0, 2)))
        dproj = jnp.concatenate([da, db, du, dgla, dglb], axis=1)
        dh1 = _mm_nt("mm_w_in_t", dproj, W("w_in", l), BF16)
        dx, dx16, small["mix_norm_g"][l] = _rms_bwd("rms_mix_bwd", s["x_in"], row("mix_norm_g", l), dh1, dx)
        if l == 0:
            flat_g = jnp.concatenate([(jnp.stack(small[n]) if n != "final_norm_g" else d_final_g).reshape(-1)
                                      for n in names])
            g_all = _allgather("ag_small_grads", pack([flat_g], 0.0)[None], False)
        emit("w_in", l, _mm_tn("mm_dw_in", s["h1"], dproj, nb=dproj.shape[1] // N_DEV))
    for n, thunk in deferred:
        emit(n, 0, thunk())

    while queue:
        tick()
    results = {n:[o[:, :inp[n].shape[1], :] for o in state[n]] if n in pads else state[n] for n in big}
    def sum8(*parts):
        tot = parts[0]
        for part in parts[1:]:
            tot = tot + part
        return tot

    g_sum = _rowwise("small_grad_sum", sum8, [(g_all.reshape(N_DEV * rows_p, pack_w), pack_w, 0, k * rows_p)
                                              for k in range(N_DEV)], [], [(pack_w, F32)], tile=LANE, nrows=rows_p)[0]
    g_flat, grads, offs = g_sum.reshape(-1), {}, 0
    for n, sz in zip(names, sizes):
        grads[n] = g_flat[offs:offs + sz].reshape(inp[n].shape)
        offs += sz
    sparse = [n for n in names if inp[n].ndim == 4 and inp[n].shape[-1] < SSM_STATE]
    dense = [n for n in names if n not in sparse]
    as2d = lambda a: a[None, :] if a.ndim == 1 else a
    for group, block in ((dense, None), (sparse, (1, 16))):
        outs = _adamw_natural("small_adamw", *[[as2d(src[n]) for n in group] for src in (grads, inp, mom_m, mom_v)],
                              lead_block=block)
        for k, n in enumerate(group):
            results[n] = [grads[n]] + [outs[j * len(group) + k].reshape(inp[n].shape) for j in range(3)]

    _LAST_CALL.clear()
    loss = lax.psum(loss_part, ("x", "y", "c"))
    grad_x = dx[None]
    return (loss, grad_x, *[results[n][0] for n in _WEIGHTS], *[results[n][1] for n in _WEIGHTS],
            *[results[n][2] for n in _WEIGHTS], *[results[n][3] for n in _WEIGHTS])


def kernel(x, mem, mix_norm_g, w_in, conv_dw_w, conv_dw_b, conv_ln_g, conv_ln_b, conv_w_pw, ssm_a_re, ssm_a_im, ssm_log_dt, ssm_b_re, ssm_b_im, ssm_c_re, ssm_c_im, ssm_d, ssm_w_glu, w_out, xa_norm_g, mem_norm_g, xa_w_q, xa_w_kv, xa_w_o, ffn_norm_g, ffn_w_up, ffn_dw_w, ffn_w_down, final_norm_g, loss_target, m_mix_norm_g, m_w_in, m_conv_dw_w, m_conv_dw_b, m_conv_ln_g, m_conv_ln_b, m_conv_w_pw, m_ssm_a_re, m_ssm_a_im, m_ssm_log_dt, m_ssm_b_re, m_ssm_b_im, m_ssm_c_re, m_ssm_c_im, m_ssm_d, m_ssm_w_glu, m_w_out, m_xa_norm_g, m_mem_norm_g, m_xa_w_q, m_xa_w_kv, m_xa_w_o, m_ffn_norm_g, m_ffn_w_up, m_ffn_dw_w, m_ffn_w_down, m_final_norm_g, v_mix_norm_g, v_w_in, v_conv_dw_w, v_conv_dw_b, v_conv_ln_g, v_conv_ln_b, v_conv_w_pw, v_ssm_a_re, v_ssm_a_im, v_ssm_log_dt, v_ssm_b_re, v_ssm_b_im, v_ssm_c_re, v_ssm_c_im, v_ssm_d, v_ssm_w_glu, v_w_out, v_xa_norm_g, v_mem_norm_g, v_xa_w_q, v_xa_w_kv, v_xa_w_o, v_ffn_norm_g, v_ffn_w_up, v_ffn_dw_w, v_ffn_w_down, v_final_norm_g):
    args = (x, mem, mix_norm_g, w_in, conv_dw_w, conv_dw_b, conv_ln_g, conv_ln_b, conv_w_pw, ssm_a_re, ssm_a_im, ssm_log_dt, ssm_b_re, ssm_b_im, ssm_c_re, ssm_c_im, ssm_d, ssm_w_glu, w_out, xa_norm_g, mem_norm_g, xa_w_q, xa_w_kv, xa_w_o, ffn_norm_g, ffn_w_up, ffn_dw_w, ffn_w_down, final_norm_g)
    ms = (m_mix_norm_g, m_w_in, m_conv_dw_w, m_conv_dw_b, m_conv_ln_g, m_conv_ln_b, m_conv_w_pw, m_ssm_a_re, m_ssm_a_im, m_ssm_log_dt, m_ssm_b_re, m_ssm_b_im, m_ssm_c_re, m_ssm_c_im, m_ssm_d, m_ssm_w_glu, m_w_out, m_xa_norm_g, m_mem_norm_g, m_xa_w_q, m_xa_w_kv, m_xa_w_o, m_ffn_norm_g, m_ffn_w_up, m_ffn_dw_w, m_ffn_w_down, m_final_norm_g)
    vs = (v_mix_norm_g, v_w_in, v_conv_dw_w, v_conv_dw_b, v_conv_ln_g, v_conv_ln_b, v_conv_w_pw, v_ssm_a_re, v_ssm_a_im, v_ssm_log_dt, v_ssm_b_re, v_ssm_b_im, v_ssm_c_re, v_ssm_c_im, v_ssm_d, v_ssm_w_glu, v_w_out, v_xa_norm_g, v_mem_norm_g, v_xa_w_q, v_xa_w_kv, v_xa_w_o, v_ffn_norm_g, v_ffn_w_up, v_ffn_dw_w, v_ffn_w_down, v_final_norm_g)
    return _step(dict(zip(_FWD, args)), loss_target, dict(zip(_WEIGHTS, ms)), dict(zip(_WEIGHTS, vs)))
```

```python
import functools

import jax
import jax.numpy as jnp
from jax import lax
from jax.experimental import pallas as pl
from jax.experimental.pallas import tpu as pltpu
from jax.experimental.pallas import tpu_sc as plsc

F32 = jnp.float32
BF16 = jnp.bfloat16
MESH_ID = pl.DeviceIdType.MESH
N_DEV = 8
EPS = 1e-6
VMEM_LIMIT = 48 * 1024 * 1024
ANY = pl.BlockSpec(memory_space=pl.ANY)

ADAM_LR = 0.001
ADAM_B1 = 0.9
ADAM_B2 = 0.999
ADAM_EPS = 1e-08
ADAM_WD = 0.01
ADAM_STEP = 10

CONV_K = 31
FFN_K = 3
XA_HEADS = 4
SSM_GROUP = 16
SSM_STATE = 64
HALO = 32
LANE = 128
SSM_LANES = 512
AG_COLLECTIVE_ID = 1
RS_SIBLING_COLLECTIVE_ID = 2
RS_CHIPS_COLLECTIVE_ID = 3


def _pick(n, prefs):
    for p in prefs:
        if p <= n and n % p == 0:
            return p
    return n


def _params(sem, vmem=VMEM_LIMIT):
    return pltpu.CompilerParams(dimension_semantics=sem, vmem_limit_bytes=vmem)


_LAST_CALL = []


def _tc_call(*call_args, **call_kwargs):
    call = pl.pallas_call(*call_args, **call_kwargs)

    def run(*args):
        args = list(args)
        if _LAST_CALL:
            i = next(k for k, a in enumerate(args) if a.ndim >= 2)
            args[i] = lax.optimization_barrier((args[i], _LAST_CALL[0]))[0]
        out = call(*args)
        _LAST_CALL[:] = [out[0] if isinstance(out, (tuple, list)) else out]
        return out

    return run


def _sigmoid(x):
    return 1.0 / (1.0 + jnp.exp(-x))


def _silu(x):
    return x * _sigmoid(x)


def _gelu(x):
    return 0.5 * x * (1.0 + jnp.tanh(0.7978845608028654 * (x + 0.044715 * (x * x * x))))


def _rms(x, g):
    return x * lax.rsqrt(jnp.mean(x * x, axis=-1, keepdims=True) + EPS) * g


def _convpost(hc, bias, ln_g, ln_b):
    h = hc + bias
    mu = jnp.mean(h, axis=-1, keepdims=True)
    xc = h - mu
    y = xc * lax.rsqrt(jnp.mean(xc * xc, axis=-1, keepdims=True) + EPS)
    return _silu(y * ln_g + ln_b)


def _mixf(gla, glb, ya, ga, gb):
    return _sigmoid(gla) * ya + _sigmoid(glb) * (ga * _sigmoid(gb))


class _W:
    def __init__(self, arr, layer, blocked):
        self.arr, self.layer, self.blocked = arr, layer, blocked
        if blocked:
            _, _, self.K, self.nb = arr.shape
            self.N = N_DEV * self.nb
        else:
            _, self.K, self.N = arr.shape
            self.nb = self.N

    def spec(self, tk, tn, ki, ni):
        l = self.layer
        if self.blocked:
            per = self.nb // tn
            return pl.BlockSpec((None, None, tk, tn), lambda *g: (ni(*g) // per, l, ki(*g), ni(*g) % per))
        return pl.BlockSpec((None, tk, tn), lambda *g: (l, ki(*g), ni(*g)))


_M_TILES = (1024, 512, 256, 128, 64, 32, 16, 8)
_N_TILES = (1408, 1024, 896, 512, 256, 128)
_K_TILES = (512, 1408, 896, 256, 128)
MAX_FULL_K = 2048
_K_FULL_TILES = (2048, 1408, 1024, 896, 512, 256, 128)
MAX_FULL_T = 4096


def _mm_nn(name, a, w, out_dtype, add=None):
    M, K = a.shape
    assert K == w.K
    tm, tn = _pick(M, _M_TILES), _pick(w.nb, _N_TILES)
    tk = K if K <= MAX_FULL_K else _pick(K, _K_FULL_TILES)
    nk = K // tk

    def body(*refs):
        a_ref, w_ref = refs[:2]
        r_ref = refs[2] if add is not None else None
        o_ref = refs[3] if add is not None else refs[2]
        part = jnp.dot(a_ref[...].astype(BF16), w_ref[...], preferred_element_type=F32)
        if nk == 1:
            o_ref[...] = (part if add is None else part + r_ref[...]).astype(o_ref.dtype)
            return
        acc = refs[-1]
        k = pl.program_id(2)

        @pl.when(k == 0)
        def _():
            acc[...] = part

        @pl.when(k > 0)
        def _():
            acc[...] += part

        @pl.when(k == nk - 1)
        def _():
            res = acc[...]
            if add is not None:
                res = res + r_ref[...]
            o_ref[...] = res.astype(o_ref.dtype)

    in_specs = [pl.BlockSpec((tm, tk), lambda i, j, k: (i, k)),
                w.spec(tk, tn, lambda i, j, k: k, lambda i, j, k: j)]
    args = [a, w.arr]
    if add is not None:
        in_specs.append(pl.BlockSpec((tm, tn), lambda i, j, k: (i, j)))
        args.append(add)
    return _tc_call(
        body, name=name, grid=(M // tm, w.N // tn, nk), in_specs=in_specs,
        out_specs=pl.BlockSpec((tm, tn), lambda i, j, k: (i, j)),
        out_shape=jax.ShapeDtypeStruct((M, w.N), out_dtype),
        scratch_shapes=[] if nk == 1 else [pltpu.VMEM((tm, tn), F32)],
        compiler_params=_params(("parallel", "parallel", "arbitrary")),
    )(*args)


def _mm_nt(name, a, w, out_dtype):
    M, N = a.shape
    assert N == w.N
    tm, tkk = _pick(M, _M_TILES), _pick(w.K, _N_TILES)
    tnn = w.nb if w.nb <= MAX_FULL_K else _pick(w.nb, _K_FULL_TILES)
    nn = N // tnn

    def body(a_ref, w_ref, o_ref, *scratch):
        part = lax.dot_general(a_ref[...].astype(BF16), w_ref[...], (((1,), (1,)), ((), ())),
                               preferred_element_type=F32)
        if nn == 1:
            o_ref[...] = part.astype(o_ref.dtype)
            return
        acc = scratch[0]
        n = pl.program_id(2)

        @pl.when(n == 0)
        def _():
            acc[...] = part

        @pl.when(n > 0)
        def _():
            acc[...] += part

        @pl.when(n == nn - 1)
        def _():
            o_ref[...] = acc[...].astype(o_ref.dtype)

    return _tc_call(
        body, name=name, grid=(M // tm, w.K // tkk, nn),
        in_specs=[pl.BlockSpec((tm, tnn), lambda i, j, n: (i, n)),
                  w.spec(tkk, tnn, lambda i, j, n: j, lambda i, j, n: n)],
        out_specs=pl.BlockSpec((tm, tkk), lambda i, j, n: (i, j)),
        out_shape=jax.ShapeDtypeStruct((M, w.K), out_dtype),
        scratch_shapes=[] if nn == 1 else [pltpu.VMEM((tm, tkk), F32)],
        compiler_params=_params(("parallel", "parallel", "arbitrary")),
    )(a, w.arr)


def _mm_tn(name, a, b, nb=None):
    T, K = a.shape
    _, N = b.shape
    width = N if nb is None else nb
    assert T <= MAX_FULL_T
    tkk, tn = _pick(K, (512, 256, 128)), _pick(width, _N_TILES)
    per = width // tn

    def body(a_ref, b_ref, o_ref):
        o_ref[...] = lax.dot_general(a_ref[...].astype(BF16), b_ref[...].astype(BF16), (((0,), (0,)), ((), ())),
                                     preferred_element_type=F32).astype(o_ref.dtype)

    if nb is None:
        out_spec = pl.BlockSpec((tkk, tn), lambda j, i: (i, j))
        out_shape = jax.ShapeDtypeStruct((K, N), BF16)
    else:
        out_spec = pl.BlockSpec((None, tkk, tn), lambda j, i: (j // per, i, j % per))
        out_shape = jax.ShapeDtypeStruct((N_DEV, K, nb), BF16)
    return _tc_call(
        body, name=name, grid=(N // tn, K // tkk),
        in_specs=[pl.BlockSpec((T, tkk), lambda j, i: (0, i)),
                  pl.BlockSpec((T, tn), lambda j, i: (0, j))],
        out_specs=out_spec, out_shape=out_shape,
        compiler_params=_params(("parallel", "parallel")),
    )(a, b)


def _rowwise(name, fn, rows, consts, outs, accs=(), tile=256, ncol=1, nrows=None):
    n_r, n_c, n_o = len(rows), len(consts), len(outs)
    T = rows[0][0].shape[0] if nrows is None else nrows
    tile = _pick(T, tuple(t for t in (512, 256, 128, 64, 32, 16, 8) if t <= tile))
    nt = T // tile
    const_rows = [c[1] if isinstance(c, tuple) else None for c in consts]
    consts = [c[0] if isinstance(c, tuple) else c for c in consts]

    def body(*refs):
        vals = [r[...] for r in refs[:n_r]]
        vals += [r[...] if l is None else r[l:l + 1, :] for r, l in zip(refs[n_r:n_r + n_c], const_rows)]
        res = fn(*vals)
        if not isinstance(res, (tuple, list)):
            res = (res,)
        o_refs = refs[n_r + n_c:n_r + n_c + n_o]
        a_refs = refs[n_r + n_c + n_o:]
        for r, v in zip(o_refs, res[:n_o]):
            r[...] = v.astype(r.dtype)
        first = pl.program_id(1) == 0
        for r, v in zip(a_refs, res[n_o:]):
            @pl.when(first)
            def _(r=r, v=v):
                r[...] = v.astype(F32)

            @pl.when(jnp.logical_not(first))
            def _(r=r, v=v):
                r[...] += v.astype(F32)

    in_specs, args = [], []
    for arr, w, off, roff in rows:
        rb = roff // tile
        assert roff % tile == 0
        in_specs.append(pl.BlockSpec((tile, w), lambda j, i, off=off, rb=rb: (i + rb, off + j)))
        args.append(arr)
    for cst in consts:
        in_specs.append(pl.BlockSpec(cst.shape, lambda j, i: (0, 0)))
        args.append(cst)
    out_specs, out_shape = [], []
    for tw, dt in outs:
        out_specs.append(pl.BlockSpec((tile, tw // ncol), lambda j, i: (i, j)))
        out_shape.append(jax.ShapeDtypeStruct((T, tw), dt))
    for nr, tw in accs:
        out_specs.append(pl.BlockSpec((nr, tw // ncol), lambda j, i: (0, j)))
        out_shape.append(jax.ShapeDtypeStruct((nr, tw), F32))
    res = _tc_call(
        body, name=name, grid=(ncol, nt), in_specs=in_specs, out_specs=out_specs, out_shape=out_shape,
        compiler_params=_params(("parallel", "arbitrary")),
    )(*args)
    return res


def _rms_fwd(name, x, g):
    D = x.shape[1]
    return _rowwise(name, lambda xv, gv: _rms(xv, gv), [(x, D, 0, 0)], [g], [(D, BF16)])[0]


def _rms_bwd(name, x, g, dh, dx_in):
    D = x.shape[1]

    def fn(xv, dhv, dxv, gv):
        _, vjp = jax.vjp(_rms, xv, gv)
        dx, dg = vjp(dhv.astype(F32))
        tot = dx + dxv
        return tot, tot, jnp.sum(dg, axis=0, keepdims=True)

    return _rowwise(name, fn, [(x, D, 0, 0), (dh, D, 0, 0), (dx_in, D, 0, 0)], [g], [(D, F32), (D, BF16)], [(1, D)],
                    tile=256)


def _lag_views(win, K, R, forward):
    n = win.shape[0]
    for r in range(8):
        if r >= K:
            break
        if r == 0:
            rolled = win
        else:
            rolled = pltpu.roll(win, (n - r) if forward else r, axis=0)
        for q in range((K - 1 - r) // 8 + 1):
            s = 8 * q + r
            if forward:
                yield s, rolled[8 * q:8 * q + R]
            else:
                yield s, rolled[HALO - 8 * q:HALO - 8 * q + R]


def _conv_chunk(win, w_ref, K, R):
    acc = None
    for s, view in _lag_views(win, K, R, forward=False):
        term = w_ref[K - 1 - s:K - s, :] * view
        acc = term if acc is None else acc + term
    return acc


def _conv_chunk_t(win, w_ref, K, R):
    acc = None
    for s, view in _lag_views(win, K, R, forward=True):
        term = w_ref[K - 1 - s:K - s, :] * view
        acc = term if acc is None else acc + term
    return acc


def _conv_dw(xwin, dy, K, R):
    taps = [None] * K
    for s, view in _lag_views(xwin, K, R, forward=False):
        taps[K - 1 - s] = jnp.sum(dy * view, axis=0, keepdims=True)
    return taps


def _chunks(T):
    R = _pick(T, (128, 64, 32))
    return R, T // R


def _glu_conv_fwd(name, proj, w, cw_total):
    T = proj.shape[0]
    C = cw_total
    cw = LANE
    nb = C // cw
    R, nch = _chunks(T)

    def body(a_ref, b_ref, w_ref, o_ref, s_ref):
        s_ref[0:HALO, :] = jnp.zeros((HALO, cw), F32)

        def fill(i, _):
            r0 = pl.multiple_of(i * R, R)
            s_ref[pl.ds(HALO + r0, R), :] = a_ref[pl.ds(r0, R), :] * _sigmoid(b_ref[pl.ds(r0, R), :])
            return 0

        lax.fori_loop(0, nch, fill, 0)

        def conv(i, _):
            r0 = pl.multiple_of(i * R, R)
            o_ref[pl.ds(r0, R), :] = _conv_chunk(s_ref[pl.ds(r0, R + HALO), :], w_ref, CONV_K, R)
            return 0

        lax.fori_loop(0, nch, conv, 0)

    return _tc_call(
        body, name=name, grid=(nb,),
        in_specs=[pl.BlockSpec((T, cw), lambda j: (0, j)), pl.BlockSpec((T, cw), lambda j: (0, nb + j)),
                  pl.BlockSpec((HALO, cw), lambda j: (0, j))],
        out_specs=pl.BlockSpec((T, cw), lambda j: (0, j)),
        out_shape=jax.ShapeDtypeStruct((T, C), F32),
        scratch_shapes=[pltpu.VMEM((T + HALO, cw), F32)],
        compiler_params=_params(("parallel",)),
    )(proj, proj, w)


def _glu_conv_bwd(name, proj, w, dhc, cw_total):
    T = proj.shape[0]
    C = cw_total
    cw = LANE
    nb = C // cw
    R, nch = _chunks(T)

    def body(a_ref, b_ref, w_ref, dy_ref, da_ref, db_ref, dw_ref, s_ref, g_ref, acc_ref):
        s_ref[0:HALO, :] = jnp.zeros((HALO, cw), F32)
        g_ref[T:T + HALO, :] = jnp.zeros((HALO, cw), F32)
        acc_ref[...] = jnp.zeros_like(acc_ref)

        def fill(i, _):
            r0 = pl.multiple_of(i * R, R)
            s_ref[pl.ds(HALO + r0, R), :] = a_ref[pl.ds(r0, R), :] * _sigmoid(b_ref[pl.ds(r0, R), :])
            g_ref[pl.ds(r0, R), :] = dy_ref[pl.ds(r0, R), :]
            return 0

        lax.fori_loop(0, nch, fill, 0)

        def back(i, _):
            r0 = pl.multiple_of(i * R, R)
            dhg = _conv_chunk_t(g_ref[pl.ds(r0, R + HALO), :], w_ref, CONV_K, R)
            av = a_ref[pl.ds(r0, R), :]
            sg = _sigmoid(b_ref[pl.ds(r0, R), :])
            da_ref[pl.ds(r0, R), :] = (dhg * sg).astype(da_ref.dtype)
            db_ref[pl.ds(r0, R), :] = (dhg * av * sg * (1.0 - sg)).astype(db_ref.dtype)
            taps = _conv_dw(s_ref[pl.ds(r0, R + HALO), :], dy_ref[pl.ds(r0, R), :], CONV_K, R)
            for k, tap in enumerate(taps):
                acc_ref[k:k + 1, :] += tap
            return 0

        lax.fori_loop(0, nch, back, 0)
        dw_ref[...] = acc_ref[...]

    return _tc_call(
        body, name=name, grid=(nb,),
        in_specs=[pl.BlockSpec((T, cw), lambda j: (0, j)), pl.BlockSpec((T, cw), lambda j: (0, nb + j)),
                  pl.BlockSpec((HALO, cw), lambda j: (0, j)), pl.BlockSpec((T, cw), lambda j: (0, j))],
        out_specs=[pl.BlockSpec((T, cw), lambda j: (0, j)), pl.BlockSpec((T, cw), lambda j: (0, j)),
                   pl.BlockSpec((HALO, cw), lambda j: (0, j))],
        out_shape=[jax.ShapeDtypeStruct((T, C), BF16), jax.ShapeDtypeStruct((T, C), BF16),
                   jax.ShapeDtypeStruct((HALO, C), F32)],
        scratch_shapes=[pltpu.VMEM((T + HALO, cw), F32), pltpu.VMEM((T + HALO, cw), F32),
                        pltpu.VMEM((HALO, cw), F32)],
        compiler_params=_params(("parallel",)),
    )(proj, proj, w, dhc)


def _ffn_conv_fwd(name, up, w, dff):
    T = up.shape[0]
    cw = LANE
    nb = dff // cw
    R, nch = _chunks(T)

    def body(g_ref, v_ref, wg_ref, wv_ref, o_ref, sg_ref, sv_ref):
        sg_ref[0:HALO, :] = jnp.zeros((HALO, cw), F32)
        sv_ref[0:HALO, :] = jnp.zeros((HALO, cw), F32)

        def fill(i, _):
            r0 = pl.multiple_of(i * R, R)
            sg_ref[pl.ds(HALO + r0, R), :] = g_ref[pl.ds(r0, R), :]
            sv_ref[pl.ds(HALO + r0, R), :] = v_ref[pl.ds(r0, R), :]
            return 0

        lax.fori_loop(0, nch, fill, 0)

        def conv(i, _):
            r0 = pl.multiple_of(i * R, R)
            gc = _conv_chunk(sg_ref[pl.ds(r0, R + HALO), :], wg_ref, FFN_K, R)
            vc = _conv_chunk(sv_ref[pl.ds(r0, R + HALO), :], wv_ref, FFN_K, R)
            o_ref[pl.ds(r0, R), :] = (_silu(gc) * vc).astype(o_ref.dtype)
            return 0

        lax.fori_loop(0, nch, conv, 0)

    return _tc_call(
        body, name=name, grid=(nb,),
        in_specs=[pl.BlockSpec((T, cw), lambda j: (0, j)), pl.BlockSpec((T, cw), lambda j: (0, nb + j)),
                  pl.BlockSpec((8, cw), lambda j: (0, j)), pl.BlockSpec((8, cw), lambda j: (0, nb + j))],
        out_specs=pl.BlockSpec((T, cw), lambda j: (0, j)),
        out_shape=jax.ShapeDtypeStruct((T, dff), BF16),
        scratch_shapes=[pltpu.VMEM((T + HALO, cw), F32), pltpu.VMEM((T + HALO, cw), F32)],
        compiler_params=_params(("parallel",)),
    )(up, up, w, w)


def _ffn_conv_bwd(name, up, w, dact, dff):
    T = up.shape[0]
    cw = LANE
    nb = dff // cw
    R, nch = _chunks(T)

    def body(g_ref, v_ref, wg_ref, wv_ref, da_ref, dg_ref, dv_ref, dwg_ref, dwv_ref,
             sg_ref, sv_ref, tg_ref, tv_ref, ag_ref, av_ref):
        zero = jnp.zeros((HALO, cw), F32)
        sg_ref[0:HALO, :] = zero
        sv_ref[0:HALO, :] = zero
        tg_ref[T:T + HALO, :] = zero
        tv_ref[T:T + HALO, :] = zero
        ag_ref[...] = jnp.zeros_like(ag_ref)
        av_ref[...] = jnp.zeros_like(av_ref)

        def fill(i, _):
            r0 = pl.multiple_of(i * R, R)
            sg_ref[pl.ds(HALO + r0, R), :] = g_ref[pl.ds(r0, R), :]
            sv_ref[pl.ds(HALO + r0, R), :] = v_ref[pl.ds(r0, R), :]
            return 0

        lax.fori_loop(0, nch, fill, 0)

        def grads(i, _):
            r0 = pl.multiple_of(i * R, R)
            gwin = sg_ref[pl.ds(r0, R + HALO), :]
            vwin = sv_ref[pl.ds(r0, R + HALO), :]
            gc = _conv_chunk(gwin, wg_ref, FFN_K, R)
            vc = _conv_chunk(vwin, wv_ref, FFN_K, R)
            da = da_ref[pl.ds(r0, R), :].astype(F32)
            sg = _sigmoid(gc)
            dgc = da * vc * (sg * (1.0 + gc * (1.0 - sg)))
            dvc = da * (gc * sg)
            tg_ref[pl.ds(r0, R), :] = dgc
            tv_ref[pl.ds(r0, R), :] = dvc
            for k, tap in enumerate(_conv_dw(gwin, dgc, FFN_K, R)):
                ag_ref[k:k + 1, :] += tap
            for k, tap in enumerate(_conv_dw(vwin, dvc, FFN_K, R)):
                av_ref[k:k + 1, :] += tap
            return 0

        lax.fori_loop(0, nch, grads, 0)

        def back(i, _):
            r0 = pl.multiple_of(i * R, R)
            dg_ref[pl.ds(r0, R), :] = _conv_chunk_t(tg_ref[pl.ds(r0, R + HALO), :], wg_ref, FFN_K, R).astype(dg_ref.dtype)
            dv_ref[pl.ds(r0, R), :] = _conv_chunk_t(tv_ref[pl.ds(r0, R + HALO), :], wv_ref, FFN_K, R).astype(dv_ref.dtype)
            return 0

        lax.fori_loop(0, nch, back, 0)
        dwg_ref[...] = ag_ref[...]
        dwv_ref[...] = av_ref[...]

    col = lambda j: (0, j)
    dg, dv, dwg, dwv = _tc_call(
        body, name=name, grid=(nb,),
        in_specs=[pl.BlockSpec((T, cw), col), pl.BlockSpec((T, cw), lambda j: (0, nb + j)),
                  pl.BlockSpec((8, cw), col), pl.BlockSpec((8, cw), lambda j: (0, nb + j)),
                  pl.BlockSpec((T, cw), col)],
        out_specs=[pl.BlockSpec((T, cw), col), pl.BlockSpec((T, cw), col),
                   pl.BlockSpec((8, cw), col), pl.BlockSpec((8, cw), col)],
        out_shape=[jax.ShapeDtypeStruct((T, dff), BF16), jax.ShapeDtypeStruct((T, dff), BF16),
                   jax.ShapeDtypeStruct((8, dff), F32), jax.ShapeDtypeStruct((8, dff), F32)],
        scratch_shapes=[pltpu.VMEM((T + HALO, cw), F32), pltpu.VMEM((T + HALO, cw), F32),
                        pltpu.VMEM((T + HALO, cw), F32), pltpu.VMEM((T + HALO, cw), F32),
                        pltpu.VMEM((8, cw), F32), pltpu.VMEM((8, cw), F32)],
        compiler_params=_params(("parallel",)),
    )(up, up, w, w, dact)
    return jnp.concatenate([dg, dv], axis=1), jnp.concatenate([dwg, dwv], axis=1)


def _zoh(a_re, a_im, log_dt):
    ar = jnp.minimum(a_re, -1e-4)
    ai = a_im
    dt = jnp.exp(log_dt)
    mag = jnp.exp(dt * ar)
    abar_re = mag * jnp.cos(dt * ai)
    abar_im = mag * jnp.sin(dt * ai)
    den = ar * ar + ai * ai
    nr = abar_re - 1.0
    ni = abar_im
    return abar_re, abar_im, (nr * ar + ni * ai) / den, (ni * ar - nr * ai) / den


def _discretize(a_re, a_im, log_dt, a_re_h, a_im_h, log_dt_h, b_re, b_im):
    abar_re, abar_im, _, _ = _zoh(a_re, a_im, log_dt)
    _, _, z_re, z_im = _zoh(a_re_h, a_im_h, log_dt_h)
    return abar_re, abar_im, z_re * b_re - z_im * b_im, z_re * b_im + z_im * b_re


def _full_specs(arrs):
    return [pl.BlockSpec(a.shape, lambda *_, n=len(a.shape): (0,) * n) for a in arrs]


def _ssm_prep(name, raw):
    def body(*refs):
        res = _discretize(*[r[...] for r in refs[:8]])
        for r, v in zip(refs[8:], res):
            r[...] = v

    outs = [jax.ShapeDtypeStruct(raw[0].shape, F32)] * 2 + [jax.ShapeDtypeStruct(raw[6].shape, F32)] * 2
    return _tc_call(body, name=name, in_specs=_full_specs(raw), out_specs=_full_specs(outs), out_shape=outs)(*raw)


def _ssm_prep_bwd(name, raw, cots):
    G = raw[0].shape[0]
    H = raw[3].shape[0] // G

    def body(*refs):
        _, vjp = jax.vjp(_discretize, *[r[...] for r in refs[:8]])
        g = vjp(tuple(r[...] for r in refs[8:12]))
        outs = refs[12:]
        for k in range(3):
            rep = g[3 + k]
            outs[k][...] = g[k] + jnp.sum(rep.reshape(G, H, rep.shape[1]), axis=1)
        outs[3][...] = g[6]
        outs[4][...] = g[7]

    outs = [jax.ShapeDtypeStruct(a.shape, F32) for a in (raw[0], raw[1], raw[2], raw[6], raw[7])]
    return _tc_call(body, name=name, in_specs=_full_specs(list(raw) + list(cots)), out_specs=_full_specs(outs),
                          out_shape=outs)(*raw, *cots)


def _cmul(ar, ai, br, bi):
    return ar * br - ai * bi, ar * bi + ai * br


def _scan_coefs(ar, ai, reverse):
    W = ar.shape[1]
    row = lax.broadcasted_iota(jnp.int32, (8, W), 0)
    p = [None] * 9
    p[1] = (ar, ai)
    for n in range(2, 9):
        p[n] = _cmul(*p[n // 2], *p[n - n // 2])
    steps = []
    for s in (1, 2, 4):
        valid = (row <= 7 - s) if reverse else (row >= s)
        steps.append((jnp.where(valid, p[s][0], 0.0), jnp.where(valid, p[s][1], 0.0)))
    pr = jnp.zeros((8, W), F32)
    pi = jnp.zeros((8, W), F32)
    for i in range(8):
        n = (8 - i) if reverse else (i + 1)
        pr = jnp.where(row == i, p[n][0], pr)
        pi = jnp.where(row == i, p[n][1], pi)
    return steps, (pr, pi)


def _scan_tile(xr, xi, cr, ci, coefs, reverse):
    steps, (pr, pi) = coefs
    for s, (sr, si) in zip((1, 2, 4), steps):
        shift = (8 - s) if reverse else s
        rr = pltpu.roll(xr, shift, axis=0)
        ri = pltpu.roll(xi, shift, axis=0)
        xr, xi = xr + sr * rr - si * ri, xi + sr * ri + si * rr
    xr, xi = xr + pr * cr - pi * ci, xi + pr * ci + pi * cr
    return xr, xi


def _edge_rows(x, reverse):
    W = x.shape[1]
    return jnp.broadcast_to(x[0:1, :] if reverse else x[7:8, :], (8, W))


def _power_table(ar, ai, pw_r, pw_i, n):
    W = ar.shape[1]
    a8r, a8i = jnp.broadcast_to(ar, (8, W)), jnp.broadcast_to(ai, (8, W))
    pr, pi = a8r, a8i
    for k in range(n):
        pw_r[8 * k:8 * k + 8, :] = pr
        pw_i[8 * k:8 * k + 8, :] = pi
        pr, pi = _cmul(pr, pi, a8r, a8i)


def _segment_order(Q):
    r = jnp.arange(Q)
    return (jnp.arange(Q)[None, :] == ((r % 8) * (Q // 8) + r // 8)[:, None]).astype(BF16)


def _to_segments(u, g, pm):
    C = u.shape[1]
    hi = u.astype(BF16)
    lo = (u - hi.astype(F32)).astype(BF16)
    moved = jnp.dot(pm, jnp.concatenate([hi, lo, g], axis=1), preferred_element_type=F32)
    return moved[:, :C].astype(BF16), moved[:, :C] + moved[:, C:2 * C], moved[:, 2 * C:]


def _segment_scan(xr_ref, xi_ref, row0, seg, ar, ai, pw_r, pw_i, carry_r, carry_i, reverse, visit=None, extra=()):
    W = ar.shape[1]
    sign = -1.0 if reverse else 1.0
    a8r, a8i = jnp.broadcast_to(ar, (8, W)), sign * jnp.broadcast_to(ai, (8, W))
    tile = lambda j: pl.ds(pl.multiple_of(row0 + 8 * j, 8), 8)

    def local(jj, x):
        j = seg - 1 - jj if reverse else jj
        xr = a8r * x[0] - a8i * x[1] + xr_ref[tile(j), :]
        xi = a8r * x[1] + a8i * x[0] + xi_ref[tile(j), :]
        xr_ref[tile(j), :] = xr
        xi_ref[tile(j), :] = xi
        return xr, xi

    zero = jnp.zeros((8, W), F32)
    xr, xi = lax.fori_loop(0, seg, local, (zero, zero))
    row = lax.broadcasted_iota(jnp.int32, (8, W), 0)
    edge = 7 if reverse else 0
    shift = 7 if reverse else 1
    gr = jnp.where(row == edge, carry_r, pltpu.roll(xr, shift, axis=0))
    gi = jnp.where(row == edge, carry_i, pltpu.roll(xi, shift, axis=0))
    top = 8 * (seg - 1)
    alr, ali = pw_r[top:top + 1, :], sign * pw_i[top:top + 1, :]
    er, ei = _scan_tile(gr, gi, zero, zero, _scan_coefs(alr, ali, reverse), reverse)
    far_r = _edge_rows(alr * er - ali * ei + xr, reverse)
    far_i = _edge_rows(alr * ei + ali * er + xi, reverse)

    def fix(j, ex):
        k = pl.ds(pl.multiple_of(8 * (seg - 1 - j if reverse else j), 8), 8)
        pr, pi = pw_r[k, :], sign * pw_i[k, :]
        fr = xr_ref[tile(j), :] + pr * er - pi * ei
        fi = xi_ref[tile(j), :] + pr * ei + pi * er
        xr_ref[tile(j), :] = fr
        xi_ref[tile(j), :] = fi
        return ex if visit is None else visit(j, fr, fi, ex)

    extra = lax.fori_loop(0, seg, fix, extra)
    return er, ei, far_r, far_i, extra


def _ssm_chunk(T):
    return _pick(T, (512, 256, 128, 64))


def _ssm_fwd(name, proj, u_off, p, width):
    T = proj.shape[0]
    NB = width // LANE
    Q = _ssm_chunk(T)
    nch = T // Q
    W = SSM_LANES

    def body(u_ref, bre, bim, cre, cim, ar_ref, ai_ref, d_ref, y_ref, ckr_ref, cki_ref, br_s, bi_s, car_r, car_i):
        c = pl.program_id(1)

        @pl.when(c == 0)
        def _():
            car_r[...] = jnp.zeros_like(car_r)
            car_i[...] = jnp.zeros_like(car_i)

        ckr_ref[...] = car_r[...]
        cki_ref[...] = car_i[...]
        u = u_ref[...]
        u16 = u.astype(BF16)
        br_s[...] = jnp.dot(u16, bre[...], preferred_element_type=F32)
        bi_s[...] = jnp.dot(u16, bim[...], preferred_element_type=F32)
        coefs = _scan_coefs(ar_ref[...], ai_ref[...], False)

        def tile(j, carry):
            r0 = pl.multiple_of(j * 8, 8)
            xr, xi = _scan_tile(br_s[pl.ds(r0, 8), :], bi_s[pl.ds(r0, 8), :], carry[0], carry[1], coefs, False)
            br_s[pl.ds(r0, 8), :] = xr
            bi_s[pl.ds(r0, 8), :] = xi
            return _edge_rows(xr, False), _edge_rows(xi, False)

        cr, ci = lax.fori_loop(0, Q // 8, tile, (car_r[...], car_i[...]))
        car_r[...] = cr
        car_i[...] = ci
        nt = (((1,), (1,)), ((), ()))
        y = (lax.dot_general(br_s[...].astype(BF16), cre[...], nt, preferred_element_type=F32)
             - lax.dot_general(bi_s[...].astype(BF16), cim[...], nt, preferred_element_type=F32)
             + d_ref[...] * u)
        y_ref[...] = _gelu(y).astype(y_ref.dtype)

    blk = lambda b, c: (b, 0, 0)
    mat = pl.BlockSpec((None, LANE, W), blk)
    vec = pl.BlockSpec((None, 1, W), blk)
    ck = pl.BlockSpec((None, None, 8, W), lambda b, c: (b, c, 0, 0))
    return _tc_call(
        body, name=name, grid=(NB, nch),
        in_specs=[pl.BlockSpec((Q, LANE), lambda b, c: (c, u_off + b)), mat, mat, mat, mat, vec, vec,
                  pl.BlockSpec((1, LANE), lambda b, c: (0, b))],
        out_specs=[pl.BlockSpec((Q, LANE), lambda b, c: (c, b)), ck, ck],
        out_shape=[jax.ShapeDtypeStruct((T, width), BF16), jax.ShapeDtypeStruct((NB, nch, 8, W), F32),
                   jax.ShapeDtypeStruct((NB, nch, 8, W), F32)],
        scratch_shapes=[pltpu.VMEM((Q, W), F32), pltpu.VMEM((Q, W), F32), pltpu.VMEM((8, W), F32),
                        pltpu.VMEM((8, W), F32)],
        compiler_params=_params(("parallel", "arbitrary")),
    )(proj, p["bre"], p["bim"], p["cre"], p["cim"], p["ar"], p["ai"], p["d"])


def _ssm_bwd(name, proj, u_off, p, ck_r, ck_i, dyg, width):
    T = proj.shape[0]
    NB = width // LANE
    Q = _ssm_chunk(T)
    nch = T // Q
    W = SSM_LANES
    nt_dims = (((1,), (1,)), ((), ()))
    tn_dims = (((0,), (0,)), ((), ()))

    def body(u_ref, dy_ref, pm_ref, pmt_ref, ckr_ref, cki_ref, bre, bim, cre, cim, ar_ref, ai_ref, d_ref,
             du_ref, dbr_ref, dbi_ref, dcr_ref, dci_ref, dar_ref, dai_ref, dd_ref,
             xr_s, xi_s, lr_s, li_s, lam_r, lam_i, pw_r, pw_i):
        c = pl.program_id(1)
        seg = Q // 8
        ar, ai = ar_ref[...], ai_ref[...]

        @pl.when(c == 0)
        def _():
            lam_r[...] = jnp.zeros_like(lam_r)
            lam_i[...] = jnp.zeros_like(lam_i)
            for r in (dbr_ref, dbi_ref, dcr_ref, dci_ref, dar_ref, dai_ref, dd_ref):
                r[...] = jnp.zeros_like(r)
            _power_table(ar, ai, pw_r, pw_i, seg)

        u16, u, dyg = _to_segments(u_ref[...], dy_ref[...], pm_ref[...])
        xr_s[8:Q + 8, :] = jnp.dot(u16, bre[...], preferred_element_type=F32)
        xi_s[8:Q + 8, :] = jnp.dot(u16, bim[...], preferred_element_type=F32)
        enter_r, enter_i, _, _, _ = _segment_scan(xr_s, xi_s, 8, seg, ar, ai, pw_r, pw_i, ckr_ref[...], cki_ref[...],
                                                  False)
        xr_s[0:8, :] = enter_r
        xi_s[0:8, :] = enter_i
        xr16 = xr_s[8:Q + 8, :].astype(BF16)
        xi16 = xi_s[8:Q + 8, :].astype(BF16)
        y = (lax.dot_general(xr16, cre[...], nt_dims, preferred_element_type=F32)
             - lax.dot_general(xi16, cim[...], nt_dims, preferred_element_type=F32) + d_ref[...] * u)
        _, gelu_vjp = jax.vjp(_gelu, y)
        dy = gelu_vjp(dyg)[0]
        dy16 = dy.astype(BF16)
        dd_ref[...] += jnp.broadcast_to(jnp.sum(dy * u, axis=0, keepdims=True), (8, LANE))
        dcr_ref[...] += lax.dot_general(dy16, xr16, tn_dims, preferred_element_type=F32)
        dci_ref[...] -= lax.dot_general(dy16, xi16, tn_dims, preferred_element_type=F32)
        lr_s[...] = jnp.dot(dy16, cre[...], preferred_element_type=F32)
        li_s[...] = -jnp.dot(dy16, cim[...], preferred_element_type=F32)

        def visit(j, lr, li, sums):
            before = pl.ds(pl.multiple_of(8 * j, 8), 8)
            xpr, xpi = xr_s[before, :], xi_s[before, :]
            return sums[0] + lr * xpr + li * xpi, sums[1] + li * xpr - lr * xpi

        zero = jnp.zeros((8, W), F32)
        _, _, cr, ci, sums = _segment_scan(lr_s, li_s, 0, seg, ar, ai, pw_r, pw_i, lam_r[...], lam_i[...], True, visit,
                                           (zero, zero))
        lam_r[...] = cr
        lam_i[...] = ci
        dar_ref[...] += jnp.broadcast_to(jnp.sum(sums[0], axis=0, keepdims=True), (8, W))
        dai_ref[...] += jnp.broadcast_to(jnp.sum(sums[1], axis=0, keepdims=True), (8, W))
        lr16 = lr_s[...].astype(BF16)
        li16 = li_s[...].astype(BF16)
        dbr_ref[...] += lax.dot_general(u16, lr16, tn_dims, preferred_element_type=F32)
        dbi_ref[...] += lax.dot_general(u16, li16, tn_dims, preferred_element_type=F32)
        du = (lax.dot_general(lr16, bre[...], nt_dims, preferred_element_type=F32)
              + lax.dot_general(li16, bim[...], nt_dims, preferred_element_type=F32) + d_ref[...] * dy)
        du_ref[...] = jnp.dot(pmt_ref[...], du.astype(BF16), preferred_element_type=F32).astype(du_ref.dtype)

    blk = lambda b, c: (b, 0, 0)
    mat = pl.BlockSpec((None, LANE, W), blk)
    vec = pl.BlockSpec((None, 1, W), blk)
    acc8 = pl.BlockSpec((None, 8, W), blk)
    ck = pl.BlockSpec((None, None, 8, W), lambda b, c: (b, nch - 1 - c, 0, 0))
    order = pl.BlockSpec((Q, Q), lambda b, c: (0, 0))
    pm = _segment_order(Q)
    return _tc_call(
        body, name=name, grid=(NB, nch),
        in_specs=[pl.BlockSpec((Q, LANE), lambda b, c: (nch - 1 - c, u_off + b)),
                  pl.BlockSpec((Q, LANE), lambda b, c: (nch - 1 - c, b)), order, order, ck, ck, mat, mat, mat, mat,
                  vec, vec, pl.BlockSpec((1, LANE), lambda b, c: (0, b))],
        out_specs=[pl.BlockSpec((Q, LANE), lambda b, c: (nch - 1 - c, b)), mat, mat, mat, mat, acc8, acc8,
                   pl.BlockSpec((None, 8, LANE), blk)],
        out_shape=[jax.ShapeDtypeStruct((T, width), BF16)] + [jax.ShapeDtypeStruct((NB, LANE, W), F32)] * 4
                  + [jax.ShapeDtypeStruct((NB, 8, W), F32)] * 2 + [jax.ShapeDtypeStruct((NB, 8, LANE), F32)],
        scratch_shapes=[pltpu.VMEM((Q + 8, W), F32), pltpu.VMEM((Q + 8, W), F32), pltpu.VMEM((Q, W), F32),
                        pltpu.VMEM((Q, W), F32), pltpu.VMEM((8, W), F32), pltpu.VMEM((8, W), F32),
                        pltpu.VMEM((Q, W), F32), pltpu.VMEM((Q, W), F32)],
        compiler_params=_params(("parallel", "arbitrary")),
    )(proj, dyg, pm, pm.T, ck_r, ck_i, p["bre"], p["bim"], p["cre"], p["cim"], p["ar"], p["ai"], p["d"])


def _block_diag(w):
    G, H, P = w.shape
    eye = jnp.eye(8, dtype=w.dtype)
    return (w.reshape(G // 8, 8, H, 1, P) * eye[None, :, None, :, None]).reshape(G // 8, 8 * H, 8 * P)


def _block_diag_t(d, H, P):
    NB = d.shape[0]
    d = d.reshape(NB, 8, H, 8, P)
    eye = jnp.eye(8, dtype=d.dtype)
    return jnp.sum(d * eye[None, :, None, :, None], axis=3).reshape(NB * 8, H, P)


def _attn_fwd(name, q, kv, heads):
    T, D = q.shape
    Mm = kv.shape[0]
    hd = D // heads
    tq = _pick(T, (512, 256, 128))
    scale = hd ** -0.5

    def body(q_ref, k_ref, v_ref, o_ref):
        s = lax.dot_general(q_ref[...], k_ref[...], (((1,), (1,)), ((), ())), preferred_element_type=F32) * scale
        s = s - jnp.max(s, axis=-1, keepdims=True)
        e = jnp.exp(s)
        p = e / jnp.sum(e, axis=-1, keepdims=True)
        o_ref[...] = jnp.dot(p.astype(BF16), v_ref[...], preferred_element_type=F32).astype(o_ref.dtype)

    return _tc_call(
        body, name=name, grid=(heads, T // tq),
        in_specs=[pl.BlockSpec((tq, hd), lambda h, i: (i, h)), pl.BlockSpec((Mm, hd), lambda h, i: (0, h)),
                  pl.BlockSpec((Mm, hd), lambda h, i: (0, heads + h))],
        out_specs=pl.BlockSpec((tq, hd), lambda h, i: (i, h)),
        out_shape=jax.ShapeDtypeStruct((T, D), BF16),
        compiler_params=_params(("parallel", "parallel")),
    )(q, kv, kv)


def _attn_bwd(name, q, kv, do, heads):
    T, D = q.shape
    Mm = kv.shape[0]
    hd = D // heads
    tq = _pick(T, (512, 256, 128))
    scale = hd ** -0.5
    nt_dims = (((1,), (1,)), ((), ()))
    tn_dims = (((0,), (0,)), ((), ()))

    def body(q_ref, k_ref, v_ref, do_ref, dq_ref, dk_ref, dv_ref):
        i = pl.program_id(1)

        @pl.when(i == 0)
        def _():
            dk_ref[...] = jnp.zeros_like(dk_ref)
            dv_ref[...] = jnp.zeros_like(dv_ref)

        qv, kvl, vv, dov = q_ref[...], k_ref[...], v_ref[...], do_ref[...]
        s = lax.dot_general(qv, kvl, nt_dims, preferred_element_type=F32) * scale
        s = s - jnp.max(s, axis=-1, keepdims=True)
        e = jnp.exp(s)
        p = e / jnp.sum(e, axis=-1, keepdims=True)
        p16 = p.astype(BF16)
        dv_ref[...] += lax.dot_general(p16, dov, tn_dims, preferred_element_type=F32)
        dp = lax.dot_general(dov, vv, nt_dims, preferred_element_type=F32)
        ds = (p * (dp - jnp.sum(dp * p, axis=-1, keepdims=True)) * scale).astype(BF16)
        dq_ref[...] = jnp.dot(ds, kvl, preferred_element_type=F32).astype(dq_ref.dtype)
        dk_ref[...] += lax.dot_general(ds, qv, tn_dims, preferred_element_type=F32)

    return _tc_call(
        body, name=name, grid=(heads, T // tq),
        in_specs=[pl.BlockSpec((tq, hd), lambda h, i: (i, h)), pl.BlockSpec((Mm, hd), lambda h, i: (0, h)),
                  pl.BlockSpec((Mm, hd), lambda h, i: (0, heads + h)), pl.BlockSpec((tq, hd), lambda h, i: (i, h))],
        out_specs=[pl.BlockSpec((tq, hd), lambda h, i: (i, h)), pl.BlockSpec((Mm, hd), lambda h, i: (0, h)),
                   pl.BlockSpec((Mm, hd), lambda h, i: (0, h))],
        out_shape=[jax.ShapeDtypeStruct((T, D), BF16), jax.ShapeDtypeStruct((Mm, D), F32),
                   jax.ShapeDtypeStruct((Mm, D), F32)],
        compiler_params=_params(("parallel", "arbitrary")),
    )(q, kv, kv, do)


def _position():
    return lax.axis_index("x"), lax.axis_index("y"), lax.axis_index("c")


def _allgather(name, blk, row_mode):
    L = blk.shape[0]
    out_shape = (L, N_DEV) + blk.shape[1:] if row_mode else (N_DEV,) + blk.shape
    x_ref = jax.new_ref(blk, memory_space=pltpu.MemorySpace.HBM)
    out_ref = jax.empty_ref(jax.ShapeDtypeStruct(out_shape, blk.dtype), memory_space=pltpu.MemorySpace.HBM)

    def body(send_sems, recv_sems, local_sem):
        x, y, c = _position()
        me, sibling = (x, y, c), (x, y, 1 - c)
        chips = [(1 - x, y), (x, 1 - y), (1 - x, 1 - y)]
        barrier = pltpu.get_barrier_semaphore()
        for peer in [sibling] + [(*chip, c) for chip in chips]:
            pl.semaphore_signal(barrier, inc=1, device_id=peer, device_id_type=MESH_ID)
        pl.semaphore_wait(barrier, 4)

        def slot(px, py, pc):
            b = 4 * px + 2 * py + pc
            return out_ref.at[:, b] if row_mode else out_ref.at[b]

        def copy(k, block, to, src=None):
            return pltpu.make_async_remote_copy(
                src_ref=slot(*block) if src is None else src, dst_ref=slot(*block),
                send_sem=send_sems.at[k], recv_sem=recv_sems.at[k], device_id=to, device_id_type=MESH_ID)

        mine = pltpu.make_async_copy(x_ref, slot(*me), local_sem)
        mine.start()
        first = [copy(0, me, sibling, src=x_ref)]
        first += [copy(1 + j, me, (*chip, c), src=x_ref) for j, chip in enumerate(chips)]
        for cp in first:
            cp.start()
        passed = [copy(4 + j, (*chip, c), sibling) for j, chip in enumerate(chips)]
        for j, chip in enumerate(chips):
            copy(1 + j, (*chip, c), me).wait_recv()
            passed[j].start()
        copy(0, sibling, me).wait_recv()
        for j, chip in enumerate(chips):
            copy(4 + j, (*chip, 1 - c), me).wait_recv()
        for cp in first + passed:
            cp.wait_send()
        mine.wait()

    pl.kernel(
        body, mesh=plsc.ScalarSubcoreMesh(axis_name="sequencer", num_cores=1), name=name,
        scratch_types=(pltpu.SemaphoreType.DMA((7,)), pltpu.SemaphoreType.DMA((7,)), pltpu.SemaphoreType.DMA),
        compiler_params=pltpu.CompilerParams(collective_id=AG_COLLECTIVE_ID),
    )()
    return out_ref[...]


def _sequencer_kernel(name, body, scratch_types, collective_id):
    pl.kernel(
        body, mesh=plsc.ScalarSubcoreMesh(axis_name="sequencer", num_cores=1), name=name,
        scratch_types=scratch_types, compiler_params=pltpu.CompilerParams(collective_id=collective_id),
    )()


def _handshake(peers):
    barrier = pltpu.get_barrier_semaphore()
    for peer in peers:
        pl.semaphore_signal(barrier, inc=1, device_id=peer, device_id_type=MESH_ID)
    pl.semaphore_wait(barrier, len(peers))


def _rs_sibling(name, grads):
    hbm = pltpu.MemorySpace.HBM
    g_ref = jax.new_ref(grads, memory_space=hbm)
    out_ref = jax.empty_ref(jax.ShapeDtypeStruct((4,) + grads.shape[1:], grads.dtype), memory_space=hbm)

    def body(send_sems, recv_sems):
        x, y, c = _position()
        sibling = (x, y, 1 - c)
        _handshake([sibling])
        copies = [pltpu.make_async_remote_copy(
            src_ref=g_ref.at[2 * q + (1 - c)], dst_ref=out_ref.at[q], send_sem=send_sems.at[q],
            recv_sem=recv_sems.at[q], device_id=sibling, device_id_type=MESH_ID) for q in range(4)]
        for cp in copies:
            cp.start()
        for cp in copies:
            cp.wait_recv()
        for cp in copies:
            cp.wait_send()

    _sequencer_kernel(name, body, (pltpu.SemaphoreType.DMA((4,)), pltpu.SemaphoreType.DMA((4,))),
                      RS_SIBLING_COLLECTIVE_ID)
    return out_ref[...]


def _rs_chips(name, part):
    hbm = pltpu.MemorySpace.HBM
    p_ref = jax.new_ref(part, memory_space=hbm)
    out_ref = jax.empty_ref(jax.ShapeDtypeStruct(part.shape, part.dtype), memory_space=hbm)

    def body(send_sems, recv_sems, local_sem):
        x, y, c = _position()
        flips = [(1 - x, y), (x, 1 - y), (1 - x, 1 - y)]
        _handshake([(fx, fy, c) for fx, fy in flips])
        mine = pltpu.make_async_copy(p_ref.at[2 * x + y], out_ref.at[3], local_sem)
        copies = [pltpu.make_async_remote_copy(
            src_ref=p_ref.at[2 * fx + fy], dst_ref=out_ref.at[k], send_sem=send_sems.at[k], recv_sem=recv_sems.at[k],
            device_id=(fx, fy, c), device_id_type=MESH_ID) for k, (fx, fy) in enumerate(flips)]
        for cp in copies:
            cp.start()
        mine.start()
        for cp in copies:
            cp.wait_recv()
        for cp in copies:
            cp.wait_send()
        mine.wait()

    _sequencer_kernel(name, body, (pltpu.SemaphoreType.DMA((3,)), pltpu.SemaphoreType.DMA((3,)),
                                   pltpu.SemaphoreType.DMA), RS_CHIPS_COLLECTIVE_ID)
    return out_ref[...]


def _adamw_math(g, w, m, v):
    m = ADAM_B1 * m + (1.0 - ADAM_B1) * g
    v = ADAM_B2 * v + (1.0 - ADAM_B2) * (g * g)
    m_hat = m / (1.0 - ADAM_B1 ** ADAM_STEP)
    v_hat = v / (1.0 - ADAM_B2 ** ADAM_STEP)
    delta = -ADAM_LR * (m_hat / (jnp.sqrt(v_hat) + ADAM_EPS) + ADAM_WD * w)
    return delta, m, v


def _adamw_natural(name, gs, ws, ms, vs, lead_block=None):
    n = len(gs)

    def body(*refs):
        ins, outs = refs[:4 * n], refs[4 * n:]
        for k in range(n):
            delta, nm, nv = _adamw_math(*[ins[j * n + k][...] for j in range(4)])
            outs[k][...] = delta
            outs[n + k][...] = nm
            outs[2 * n + k][...] = nv

    out_shape = [jax.ShapeDtypeStruct(w.shape, F32) for w in ws] * 3
    if lead_block is None:
        grid = ()
        spec = lambda a: pl.BlockSpec(a.shape, lambda nd=len(a.shape): (0,) * nd)
    else:
        grid = tuple(d // b for d, b in zip(ws[0].shape[:2], lead_block))
        spec = lambda a: pl.BlockSpec(tuple(lead_block) + tuple(a.shape[2:]),
                                      lambda i, j, nd=len(a.shape): (i, j) + (0,) * (nd - 2))
    return _tc_call(
        body, name=name, grid=grid, in_specs=[spec(a) for a in list(gs) + list(ws) + list(ms) + list(vs)],
        out_specs=[spec(o) for o in out_shape], out_shape=out_shape,
        compiler_params=pltpu.CompilerParams(vmem_limit_bytes=VMEM_LIMIT),
    )(*gs, *ws, *ms, *vs)


def _ew_tile(R, C, nblocks):
    budget = (VMEM_LIMIT * 3) // 4
    return _pick(R, tuple(t for t in (1024, 512, 256, 128, 64, 32, 16, 8) if 8 * t * C * nblocks <= budget))


def _pair_sum(name, grads, landed, c_idx):
    _, r, c = grads.shape
    tile = _ew_tile(r, c, 3)

    def body(c_ref, g_ref, s_ref, o_ref):
        o_ref[...] = (g_ref[...].astype(F32) + s_ref[...].astype(F32)).astype(o_ref.dtype)

    return _tc_call(
        body, name=name,
        grid_spec=pltpu.PrefetchScalarGridSpec(
            num_scalar_prefetch=1, grid=(4, r // tile),
            in_specs=[pl.BlockSpec((None, tile, c), lambda q, i, c_ref: (2 * q + c_ref[0], i, 0)),
                      pl.BlockSpec((None, tile, c), lambda q, i, c_ref: (q, i, 0))],
            out_specs=pl.BlockSpec((None, tile, c), lambda q, i, c_ref: (q, i, 0))),
        out_shape=jax.ShapeDtypeStruct((4, r, c), BF16),
        compiler_params=_params(("parallel", "parallel")),
    )(c_idx, grads, landed)


def _adamw_layer(name, got, w, m, v, layer, prev):
    L, r, c = w.shape
    tile = _ew_tile(r, c, 11)

    def body(g0, g1, g2, g3, w_ref, m_ref, v_ref, *rest):
        outs = rest[-4:]
        g = (g0[...].astype(F32) + g1[...].astype(F32)) + (g2[...].astype(F32) + g3[...].astype(F32))
        delta, nm, nv = _adamw_math(g, w_ref[...], m_ref[...], v_ref[...])
        for ref, val in zip(outs, (g, delta, nm, nv)):
            ref[...] = val

    slab = pl.BlockSpec((None, tile, c), lambda i: (layer, i, 0))
    in_specs = [pl.BlockSpec((None, tile, c), lambda i, k=k: (k, i, 0)) for k in range(4)] + [slab] * 3
    args = [got, got, got, got, w, m, v]
    aliases = {}
    if prev is not None:
        in_specs += [ANY] * 4
        args += list(prev)
        aliases = {7 + k: k for k in range(4)}
    return _tc_call(
        body, name=name, grid=(r // tile,), in_specs=in_specs, out_specs=[slab] * 4,
        out_shape=[jax.ShapeDtypeStruct((L, r, c), F32)] * 4, input_output_aliases=aliases,
        compiler_params=_params(("parallel",)),
    )(*args)


def _reduce_pipeline(name, layer, grads, w, m, v, state):
    c_idx = lax.axis_index("c").astype(jnp.int32).reshape(1)
    landed = _rs_sibling(name + "_rs_sibling", grads)
    yield
    part = _pair_sum(name + "_pair_sum", grads, landed, c_idx)
    got = _rs_chips(name + "_rs_chips", part)
    for _ in range(4):
        yield
    state[name] = _adamw_layer(name + "_adamw", got, w, m, v, layer, state.get(name))


def _loss_head(name, x, g, target):
    T, D = x.shape

    def fn(xv, tv, gv):
        def f(xx, gg):
            err = _rms(xx, gg) - tv
            return 0.5 * jnp.sum(jnp.mean(err * err, axis=-1, keepdims=True))

        loss, (dx, dg) = jax.value_and_grad(f, argnums=(0, 1))(xv, gv)
        return dx, dx, jnp.full((8, LANE), loss, F32), jnp.sum(dg, axis=0, keepdims=True)

    dx, dx16, loss, dg = _rowwise(name, fn, [(x, D, 0, 0), (target, D, 0, 0)], [g], [(D, F32), (D, BF16)],
                                  [(8, LANE), (1, D)], tile=128)
    return loss[0, 0], dx, dx16, dg


_SHARDED_COL = ("w_in", "conv_dw_w", "conv_w_pw", "ssm_w_glu", "xa_w_kv", "ffn_w_up", "ffn_dw_w")
_SHARDED_ROW = ("w_out", "xa_w_q", "xa_w_o", "ffn_w_down")
_WEIGHTS = ['mix_norm_g', 'w_in', 'conv_dw_w', 'conv_dw_b', 'conv_ln_g', 'conv_ln_b', 'conv_w_pw', 'ssm_a_re',
            'ssm_a_im', 'ssm_log_dt', 'ssm_b_re', 'ssm_b_im', 'ssm_c_re', 'ssm_c_im', 'ssm_d', 'ssm_w_glu', 'w_out',
            'xa_norm_g', 'mem_norm_g', 'xa_w_q', 'xa_w_kv', 'xa_w_o', 'ffn_norm_g', 'ffn_w_up', 'ffn_dw_w',
            'ffn_w_down', 'final_norm_g']
_FWD = ['x', 'mem'] + _WEIGHTS
_AG_ORDER = ("w_in", "conv_dw_w", "conv_w_pw", "ssm_w_glu", "w_out", "xa_w_q", "xa_w_kv", "xa_w_o", "ffn_w_up",
             "ffn_dw_w", "ffn_w_down")


def _pad_rows(a, rows):
    return jnp.pad(a, ((0, 0), (0, rows - a.shape[1]), (0, 0)))


def _step(inp, target, mom_m, mom_v):
    _LAST_CALL.clear()
    x0 = inp["x"][0]
    mem = inp["mem"][0]
    T, D = x0.shape
    L = inp["w_in"].shape[0]
    CW = inp["conv_dw_b"].shape[1]
    SW = inp["ssm_d"].shape[1]
    DFF = inp["ffn_w_down"].shape[1] * N_DEV
    G = SW // SSM_GROUP
    NB = SW // LANE
    u_off = (2 * CW) // LANE
    gate_off = (2 * CW + SW) // 1024

    gathered = {n: [None] * L for n in _AG_ORDER}
    filter_rows = {"conv_dw_w": HALO, "ffn_dw_w": 8}
    for l in range(L):
        for n in _AG_ORDER:
            blk = inp[n][l:l + 1]
            blk = _pad_rows(blk, filter_rows[n]) if n in filter_rows else blk.astype(BF16)
            if n in _SHARDED_COL:
                gathered[n][l] = _allgather("ag_" + n, blk, False)
            else:
                full = _allgather("ag_" + n, blk, True)
                gathered[n][l] = full.reshape(1, N_DEV * full.shape[2], full.shape[3])

    def W(n, l):
        return _W(gathered[n][l], 0, n in _SHARDED_COL)

    def dw_filter(n, l):
        g = gathered[n][l]
        return jnp.transpose(g[:, 0], (1, 0, 2)).reshape(g.shape[2], N_DEV * g.shape[3])

    def row(n, l):
        return (inp[n], l)

    ssm_raw, ssm_p = [], []
    for l in range(L):
        a_re, a_im, ldt = inp["ssm_a_re"][l], inp["ssm_a_im"][l], inp["ssm_log_dt"][l][:, None]
        rep = lambda a: jnp.repeat(a, SSM_GROUP, axis=0)
        flat = lambda b: jnp.transpose(b, (0, 2, 1)).reshape(G * SSM_GROUP, SSM_STATE)
        raw = (a_re, a_im, ldt, rep(a_re), rep(a_im), rep(ldt), flat(inp["ssm_b_re"][l]), flat(inp["ssm_b_im"][l]))
        abar_re, abar_im, bbar_re, bbar_im = _ssm_prep("ssm_prep", raw)
        bbar_re = bbar_re.reshape(G, SSM_GROUP, SSM_STATE)
        bbar_im = bbar_im.reshape(G, SSM_GROUP, SSM_STATE)
        ssm_raw.append(raw)
        ssm_p.append(dict(
            bre=_block_diag(bbar_re).astype(BF16), bim=_block_diag(bbar_im).astype(BF16),
            cre=_block_diag(inp["ssm_c_re"][l]).astype(BF16), cim=_block_diag(inp["ssm_c_im"][l]).astype(BF16),
            ar=abar_re.reshape(NB, 1, SSM_LANES), ai=abar_im.reshape(NB, 1, SSM_LANES), d=inp["ssm_d"][l][None, :]))

    saved = []
    x = x0
    for l in range(L):
        s = {"x_in": x}
        s["h1"] = _rms_fwd("rms_mix", x, row("mix_norm_g", l))
        s["proj"] = _mm_nn("mm_w_in", s["h1"], W("w_in", l), F32)
        s["hc"] = _glu_conv_fwd("conv_fwd", s["proj"], dw_filter("conv_dw_w", l), CW)
        s["hs"] = _rowwise("conv_post", _convpost, [(s["hc"], CW, 0, 0)],
                           [row("conv_dw_b", l), row("conv_ln_g", l), row("conv_ln_b", l)], [(CW, BF16)])[0]
        s["ya"] = _mm_nn("mm_w_pw", s["hs"], W("conv_w_pw", l), F32)
        s["yg"], s["ck_r"], s["ck_i"] = _ssm_fwd("ssm_fwd", s["proj"], u_off, ssm_p[l], SW)
        s["gg"] = _mm_nn("mm_w_glu", s["yg"], W("ssm_w_glu", l), F32)
        nmix = D // 1024
        mix_rows = [(s["proj"], 1024, gate_off, 0), (s["proj"], 1024, gate_off + nmix, 0), (s["ya"], 1024, 0, 0),
                    (s["gg"], 1024, 0, 0), (s["gg"], 1024, nmix, 0)]
        s["mix_rows"] = mix_rows
        s["mix"] = _rowwise("mix_fwd", _mixf, mix_rows, [], [(D, BF16)], ncol=nmix)[0]
        x = _mm_nn("mm_w_out", s["mix"], W("w_out", l), F32, add=x)
        s["x1"] = x
        s["h2"] = _rms_fwd("rms_xa", x, row("xa_norm_g", l))
        s["q"] = _mm_nn("mm_w_q", s["h2"], W("xa_w_q", l), BF16)
        s["mn"] = _rms_fwd("rms_mem", mem, row("mem_norm_g", l))
        s["kv"] = _mm_nn("mm_w_kv", s["mn"], W("xa_w_kv", l), BF16)
        s["o"] = _attn_fwd("attn_fwd", s["q"], s["kv"], XA_HEADS)
        x = _mm_nn("mm_w_o", s["o"], W("xa_w_o", l), F32, add=x)
        s["x2"] = x
        s["h3"] = _rms_fwd("rms_ffn", x, row("ffn_norm_g", l))
        s["up"] = _mm_nn("mm_w_up", s["h3"], W("ffn_w_up", l), F32)
        s["act"] = _ffn_conv_fwd("ffn_conv_fwd", s["up"], dw_filter("ffn_dw_w", l), DFF)
        x = _mm_nn("mm_w_down", s["act"], W("ffn_w_down", l), F32, add=x)
        saved.append(s)

    loss_part, dx, dx16, d_final_g = _loss_head("loss_head", x, inp["final_norm_g"][None, :], target[0])

    big = _SHARDED_COL + _SHARDED_ROW
    small = {n: [None] * L for n in _WEIGHTS if n not in big and n != "final_norm_g"}
    pads = {"conv_dw_w": HALO, "ffn_dw_w": 8}
    shards = {n: tuple(_pad_rows(a, pads[n]) if n in pads else a for a in (inp[n], mom_m[n], mom_v[n])) for n in big}
    state, queue = {}, []
    names = [n for n in _WEIGHTS if n not in big]
    sizes = [inp[n].size for n in names]
    total = sum(sizes)
    pack_w = 8 * LANE
    rows_p = -(-total // (LANE * pack_w)) * LANE
    padn = rows_p * pack_w - total

    def pack(parts, fill):
        return jnp.concatenate([p.reshape(-1) for p in parts] + [jnp.full((padn,), fill, F32)]).reshape(rows_p, pack_w)

    def tick():
        for gen in list(queue):
            if next(gen, "done") == "done":
                queue.remove(gen)

    def emit(n, l, g):
        queue.append(_reduce_pipeline(n, l, g.astype(BF16), *shards[n], state))
        tick()

    deferred = []

    def emit_small(n, l, thunk):
        if l == 0:
            deferred.append((n, thunk))
        else:
            emit(n, l, thunk())

    for l in reversed(range(L)):
        s = saved[l]
        dact = _mm_nt("mm_w_down_t", dx16, W("ffn_w_down", l), BF16)
        emit("ffn_w_down", l, _mm_tn("mm_dw_down", s["act"], dx16).reshape(N_DEV, DFF // N_DEV, D))
        d_up, d_ffn_dw = _ffn_conv_bwd("ffn_conv_bwd", s["up"], dw_filter("ffn_dw_w", l), dact, DFF)
        emit("ffn_dw_w", l, jnp.transpose(d_ffn_dw.reshape(8, N_DEV, 2 * DFF // N_DEV), (1, 0, 2)))
        dh3 = _mm_nt("mm_w_up_t", d_up, W("ffn_w_up", l), BF16)
        emit("ffn_w_up", l, _mm_tn("mm_dw_up", s["h3"], d_up, nb=2 * DFF // N_DEV))
        dx, dx16, small["ffn_norm_g"][l] = _rms_bwd("rms_ffn_bwd", s["x2"], row("ffn_norm_g", l), dh3, dx)
        do = _mm_nt("mm_w_o_t", dx16, W("xa_w_o", l), BF16)
        emit_small("xa_w_o", l, lambda a=s["o"], b=dx16: _mm_tn("mm_dw_o", a, b).reshape(N_DEV, D // N_DEV, D))
        dq, dk, dv = _attn_bwd("attn_bwd", s["q"], s["kv"], do, XA_HEADS)
        dkv = jnp.concatenate([dk, dv], axis=1).astype(BF16)
        dh2 = _mm_nt("mm_w_q_t", dq, W("xa_w_q", l), BF16)
        emit_small("xa_w_q", l, lambda a=s["h2"], b=dq: _mm_tn("mm_dw_q", a, b).reshape(N_DEV, D // N_DEV, D))
        dmn = _mm_nt("mm_w_kv_t", dkv, W("xa_w_kv", l), BF16)
        emit("xa_w_kv", l, _mm_tn("mm_dw_kv", s["mn"], dkv, nb=2 * D // N_DEV))

        def mem_bwd(mv, dv_, gv):
            _, vjp = jax.vjp(_rms, mv, gv)
            return jnp.sum(vjp(dv_.astype(F32))[1], axis=0, keepdims=True)

        small["mem_norm_g"][l] = _rowwise("rms_mem_bwd", mem_bwd, [(mem, D, 0, 0), (dmn, D, 0, 0)],
                                          [row("mem_norm_g", l)], [], [(1, D)], tile=128)[0]
        dx, dx16, small["xa_norm_g"][l] = _rms_bwd("rms_xa_bwd", s["x1"], row("xa_norm_g", l), dh2, dx)
        dmix = _mm_nt("mm_w_out_t", dx16, W("w_out", l), BF16)
        emit_small("w_out", l, lambda a=s["mix"], b=dx16: _mm_tn("mm_dw_out", a, b).reshape(N_DEV, D // N_DEV, D))

        def mix_bwd(gla, glb, ya, ga, gb, dm):
            _, vjp = jax.vjp(_mixf, gla, glb, ya, ga, gb)
            return vjp(dm.astype(F32))

        nmix = D // 1024
        dgla, dglb, dya, dga, dgb = _rowwise("mix_bwd", mix_bwd, s["mix_rows"] + [(dmix, 1024, 0, 0)], [],
                                             [(D, BF16)] * 5, ncol=nmix)
        dgg = jnp.concatenate([dga, dgb], axis=1)
        dyg = _mm_nt("mm_w_glu_t", dgg, W("ssm_w_glu", l), BF16)
        emit_small("ssm_w_glu", l, lambda a=s["yg"], b=dgg: _mm_tn("mm_dw_glu", a, b, nb=2 * D // N_DEV))
        du, dbr, dbi, dcr, dci, dar, dai, dd = _ssm_bwd("ssm_bwd", s["proj"], u_off, ssm_p[l], s["ck_r"], s["ck_i"],
                                                        dyg, SW)
        cots = (dar[:, 0, :].reshape(G, SSM_STATE), dai[:, 0, :].reshape(G, SSM_STATE),
                _block_diag_t(dbr, SSM_GROUP, SSM_STATE).reshape(G * SSM_GROUP, SSM_STATE),
                _block_diag_t(dbi, SSM_GROUP, SSM_STATE).reshape(G * SSM_GROUP, SSM_STATE))
        g_are, g_aim, g_ldt, g_bre, g_bim = _ssm_prep_bwd("ssm_prep_bwd", ssm_raw[l], cots)
        small["ssm_a_re"][l], small["ssm_a_im"][l], small["ssm_log_dt"][l] = g_are, g_aim, g_ldt[:, 0]
        small["ssm_b_re"][l] = jnp.transpose(g_bre.reshape(G, SSM_GROUP, SSM_STATE), (0, 2, 1))
        small["ssm_b_im"][l] = jnp.transpose(g_bim.reshape(G, SSM_GROUP, SSM_STATE), (0, 2, 1))
        small["ssm_c_re"][l] = _block_diag_t(dcr, SSM_GROUP, SSM_STATE)
        small["ssm_c_im"][l] = _block_diag_t(dci, SSM_GROUP, SSM_STATE)
        small["ssm_d"][l] = dd[:, 0, :].reshape(SW)
        dhs = _mm_nt("mm_w_pw_t", dya, W("conv_w_pw", l), BF16)
        emit("conv_w_pw", l, _mm_tn("mm_dw_pw", s["hs"], dya, nb=D // N_DEV))

        def post_bwd(hc, dh, b, lg, lb):
            _, vjp = jax.vjp(_convpost, hc, b, lg, lb)
            dhc, db, dlg, dlb = vjp(dh.astype(F32))
            return dhc, jnp.sum(db, axis=0, keepdims=True), jnp.sum(dlg, axis=0, keepdims=True), \
                jnp.sum(dlb, axis=0, keepdims=True)

        dhc, small["conv_dw_b"][l], small["conv_ln_g"][l], small["conv_ln_b"][l] = _rowwise(
            "conv_post_bwd", post_bwd, [(s["hc"], CW, 0, 0), (dhs, CW, 0, 0)],
            [row("conv_dw_b", l), row("conv_ln_g", l), row("conv_ln_b", l)], [(CW, F32)], [(1, CW)] * 3)
        if l == 0:
            early = {**small, "mix_norm_g": [jnp.zeros((1, D), F32)] + small["mix_norm_g"][1:]}
            flat_g = jnp.concatenate([(jnp.stack(early[n]) if n != "final_norm_g" else d_final_g).reshape(-1)
                                      for n in names])
            g_all = _allgather("ag_small_grads", pack([flat_g], 0.0)[None], False)
        da, db, d_conv_dw = _glu_conv_bwd("conv_bwd", s["proj"], dw_filter("conv_dw_w", l), dhc, CW)
        emit("conv_dw_w", l, jnp.transpose(d_conv_dw.reshape(HALO, N_DEV, CW // N_DEV), (1, 0, 2)))
        dproj = jnp.concatenate([da, db, du, dgla, dglb], axis=1)
        dh1 = _mm_nt("mm_w_in_t", dproj, W("w_in", l), BF16)
        dx, dx16, small["mix_norm_g"][l] = _rms_bwd("rms_mix_bwd", s["x_in"], row("mix_norm_g", l), dh1, dx)
        if l == 0:
            assert names[0] == "mix_norm_g" and D % pack_w == 0 and D // pack_w <= 8
            late = jnp.pad(small["mix_norm_g"][0].reshape(D // pack_w, pack_w), ((0, 8 - D // pack_w), (0, 0)))
            g_late = _allgather("ag_mix_grad", late[None], False)
        emit("w_in", l, _mm_tn("mm_dw_in", s["h1"], dproj, nb=dproj.shape[1] // N_DEV))
    for n, thunk in deferred:
        emit(n, 0, thunk())

    while queue:
        tick()
    results = {n: [o[:, :inp[n].shape[1], :] for o in state[n]] if n in pads else state[n] for n in big}

    def total(*parts):
        tot = parts[0]
        for part in parts[1:]:
            tot = tot + part
        return tot

    g_sum = _rowwise("small_grad_sum", total, [(g_all.reshape(N_DEV * rows_p, pack_w), pack_w, 0, k * rows_p)
                                               for k in range(N_DEV)], [], [(pack_w, F32)], tile=LANE, nrows=rows_p)[0]
    head = _rowwise("mix_grad_sum", total, [(g_sum, pack_w, 0, 0)] + [(g_late.reshape(N_DEV * 8, pack_w), pack_w, 0, 8 * k)
                                                                      for k in range(N_DEV)],
                    [], [(pack_w, F32)], tile=8, nrows=8)[0]
    g_sum = jnp.concatenate([head, g_sum[8:]], axis=0)
    g_flat, grads, offs = g_sum.reshape(-1), {}, 0
    for n, sz in zip(names, sizes):
        grads[n] = g_flat[offs:offs + sz].reshape(inp[n].shape)
        offs += sz
    sparse = [n for n in names if inp[n].ndim == 4 and inp[n].shape[-1] < SSM_STATE]
    dense = [n for n in names if n not in sparse]
    as2d = lambda a: a[None, :] if a.ndim == 1 else a
    for group, block in ((dense, None), (sparse, (1, 16))):
        outs = _adamw_natural("small_adamw", *[[as2d(src[n]) for n in group] for src in (grads, inp, mom_m, mom_v)],
                              lead_block=block)
        for k, n in enumerate(group):
            results[n] = [grads[n]] + [outs[j * len(group) + k].reshape(inp[n].shape) for j in range(3)]

    _LAST_CALL.clear()
    loss = lax.psum(loss_part, ("x", "y", "c"))
    grad_x = dx[None]
    return (loss, grad_x, *[results[n][0] for n in _WEIGHTS], *[results[n][1] for n in _WEIGHTS],
            *[results[n][2] for n in _WEIGHTS], *[results[n][3] for n in _WEIGHTS])


def kernel(x, mem, mix_norm_g, w_in, conv_dw_w, conv_dw_b, conv_ln_g, conv_ln_b, conv_w_pw, ssm_a_re, ssm_a_im, ssm_log_dt, ssm_b_re, ssm_b_im, ssm_c_re, ssm_c_im, ssm_d, ssm_w_glu, w_out, xa_norm_g, mem_norm_g, xa_w_q, xa_w_kv, xa_w_o, ffn_norm_g, ffn_w_up, ffn_dw_w, ffn_w_down, final_norm_g, loss_target, m_mix_norm_g, m_w_in, m_conv_dw_w, m_conv_dw_b, m_conv_ln_g, m_conv_ln_b, m_conv_w_pw, m_ssm_a_re, m_ssm_a_im, m_ssm_log_dt, m_ssm_b_re, m_ssm_b_im, m_ssm_c_re, m_ssm_c_im, m_ssm_d, m_ssm_w_glu, m_w_out, m_xa_norm_g, m_mem_norm_g, m_xa_w_q, m_xa_w_kv, m_xa_w_o, m_ffn_norm_g, m_ffn_w_up, m_ffn_dw_w, m_ffn_w_down, m_final_norm_g, v_mix_norm_g, v_w_in, v_conv_dw_w, v_conv_dw_b, v_conv_ln_g, v_conv_ln_b, v_conv_w_pw, v_ssm_a_re, v_ssm_a_im, v_ssm_log_dt, v_ssm_b_re, v_ssm_b_im, v_ssm_c_re, v_ssm_c_im, v_ssm_d, v_ssm_w_glu, v_w_out, v_xa_norm_g, v_mem_norm_g, v_xa_w_q, v_xa_w_kv, v_xa_w_o, v_ffn_norm_g, v_ffn_w_up, v_ffn_dw_w, v_ffn_w_down, v_final_norm_g):
    args = (x, mem, mix_norm_g, w_in, conv_dw_w, conv_dw_b, conv_ln_g, conv_ln_b, conv_w_pw, ssm_a_re, ssm_a_im, ssm_log_dt, ssm_b_re, ssm_b_im, ssm_c_re, ssm_c_im, ssm_d, ssm_w_glu, w_out, xa_norm_g, mem_norm_g, xa_w_q, xa_w_kv, xa_w_o, ffn_norm_g, ffn_w_up, ffn_dw_w, ffn_w_down, final_norm_g)
    ms = (m_mix_norm_g, m_w_in, m_conv_dw_w, m_conv_dw_b, m_conv_ln_g, m_conv_ln_b, m_conv_w_pw, m_ssm_a_re, m_ssm_a_im, m_ssm_log_dt, m_ssm_b_re, m_ssm_b_im, m_ssm_c_re, m_ssm_c_im, m_ssm_d, m_ssm_w_glu, m_w_out, m_xa_norm_g, m_mem_norm_g, m_xa_w_q, m_xa_w_kv, m_xa_w_o, m_ffn_norm_g, m_ffn_w_up, m_ffn_dw_w, m_ffn_w_down, m_final_norm_g)
    vs = (v_mix_norm_g, v_w_in, v_conv_dw_w, v_conv_dw_b, v_conv_ln_g, v_conv_ln_b, v_conv_w_pw, v_ssm_a_re, v_ssm_a_im, v_ssm_log_dt, v_ssm_b_re, v_ssm_b_im, v_ssm_c_re, v_ssm_c_im, v_ssm_d, v_ssm_w_glu, v_w_out, v_xa_norm_g, v_mem_norm_g, v_xa_w_q, v_xa_w_kv, v_xa_w_o, v_ffn_norm_g, v_ffn_w_up, v_ffn_dw_w, v_ffn_w_down, v_final_norm_g)
    return _step(dict(zip(_FWD, args)), loss_target, dict(zip(_WEIGHTS, ms)), dict(zip(_WEIGHTS, vs)))
```

```python
import functools

import jax
import jax.numpy as jnp
from jax import lax
from jax.experimental import pallas as pl
from jax.experimental.pallas import tpu as pltpu
from jax.experimental.pallas import tpu_sc as plsc

F32 = jnp.float32
BF16 = jnp.bfloat16
MESH_ID = pl.DeviceIdType.MESH
N_DEV = 8
EPS = 1e-6
VMEM_LIMIT = 48 * 1024 * 1024
ANY = pl.BlockSpec(memory_space=pl.ANY)

ADAM_LR = 0.001
ADAM_B1 = 0.9
ADAM_B2 = 0.999
ADAM_EPS = 1e-08
ADAM_WD = 0.01
ADAM_STEP = 10

CONV_K = 31
FFN_K = 3
XA_HEADS = 4
SSM_GROUP = 16
SSM_STATE = 64
HALO = 32
LANE = 128
SSM_LANES = 512
AG_COLLECTIVE_ID = 1
RS_SIBLING_COLLECTIVE_ID = 2
RS_CHIPS_COLLECTIVE_ID = 3


def _pick(n, prefs):
    for p in prefs:
        if p <= n and n % p == 0:
            return p
    return n


def _params(sem, vmem=VMEM_LIMIT):
    return pltpu.CompilerParams(dimension_semantics=sem, vmem_limit_bytes=vmem)


_LAST_CALL = []


def _tc_call(*call_args, **call_kwargs):
    call = pl.pallas_call(*call_args, **call_kwargs)

    def run(*args):
        args = list(args)
        if _LAST_CALL:
            i = next(k for k, a in enumerate(args) if a.ndim >= 2)
            args[i] = lax.optimization_barrier((args[i], _LAST_CALL[0]))[0]
        out = call(*args)
        _LAST_CALL[:] = [out[0] if isinstance(out, (tuple, list)) else out]
        return out

    return run


def _sigmoid(x):
    return 1.0 / (1.0 + jnp.exp(-x))


def _silu(x):
    return x * _sigmoid(x)


def _gelu(x):
    return 0.5 * x * (1.0 + jnp.tanh(0.7978845608028654 * (x + 0.044715 * (x * x * x))))


def _rms(x, g):
    return x * lax.rsqrt(jnp.mean(x * x, axis=-1, keepdims=True) + EPS) * g


def _convpost(hc, bias, ln_g, ln_b):
    h = hc + bias
    mu = jnp.mean(h, axis=-1, keepdims=True)
    xc = h - mu
    y = xc * lax.rsqrt(jnp.mean(xc * xc, axis=-1, keepdims=True) + EPS)
    return _silu(y * ln_g + ln_b)


def _mixf(gla, glb, ya, ga, gb):
    return _sigmoid(gla) * ya + _sigmoid(glb) * (ga * _sigmoid(gb))


class _W:
    def __init__(self, arr, layer, blocked):
        self.arr, self.layer, self.blocked = arr, layer, blocked
        if blocked:
            _, _, self.K, self.nb = arr.shape
            self.N = N_DEV * self.nb
        else:
            _, self.K, self.N = arr.shape
            self.nb = self.N

    def spec(self, tk, tn, ki, ni):
        l = self.layer
        if self.blocked:
            per = self.nb // tn
            return pl.BlockSpec((None, None, tk, tn), lambda *g: (ni(*g) // per, l, ki(*g), ni(*g) % per))
        return pl.BlockSpec((None, tk, tn), lambda *g: (l, ki(*g), ni(*g)))


_M_TILES = (1024, 512, 256, 128, 64, 32, 16, 8)
_N_TILES = (1408, 1024, 896, 512, 256, 128)
_K_TILES = (512, 1408, 896, 256, 128)
MAX_FULL_K = 2048
_K_FULL_TILES = (2048, 1408, 1024, 896, 512, 256, 128)
MAX_FULL_T = 4096


def _mm_nn(name, a, w, out_dtype, add=None):
    M, K = a.shape
    assert K == w.K
    tm, tn = _pick(M, _M_TILES), _pick(w.nb, _N_TILES)
    tk = K if K <= MAX_FULL_K else _pick(K, _K_FULL_TILES)
    nk = K // tk

    def body(*refs):
        a_ref, w_ref = refs[:2]
        r_ref = refs[2] if add is not None else None
        o_ref = refs[3] if add is not None else refs[2]
        part = jnp.dot(a_ref[...].astype(BF16), w_ref[...], preferred_element_type=F32)
        if nk == 1:
            o_ref[...] = (part if add is None else part + r_ref[...]).astype(o_ref.dtype)
            return
        acc = refs[-1]
        k = pl.program_id(2)

        @pl.when(k == 0)
        def _():
            acc[...] = part

        @pl.when(k > 0)
        def _():
            acc[...] += part

        @pl.when(k == nk - 1)
        def _():
            res = acc[...]
            if add is not None:
                res = res + r_ref[...]
            o_ref[...] = res.astype(o_ref.dtype)

    in_specs = [pl.BlockSpec((tm, tk), lambda i, j, k: (i, k)),
                w.spec(tk, tn, lambda i, j, k: k, lambda i, j, k: j)]
    args = [a, w.arr]
    if add is not None:
        in_specs.append(pl.BlockSpec((tm, tn), lambda i, j, k: (i, j)))
        args.append(add)
    return _tc_call(
        body, name=name, grid=(M // tm, w.N // tn, nk), in_specs=in_specs,
        out_specs=pl.BlockSpec((tm, tn), lambda i, j, k: (i, j)),
        out_shape=jax.ShapeDtypeStruct((M, w.N), out_dtype),
        scratch_shapes=[] if nk == 1 else [pltpu.VMEM((tm, tn), F32)],
        compiler_params=_params(("parallel", "parallel", "arbitrary")),
    )(*args)


def _mm_nt(name, a, w, out_dtype):
    M, N = a.shape
    assert N == w.N
    tm, tkk = _pick(M, _M_TILES), _pick(w.K, _N_TILES)
    tnn = w.nb if w.nb <= MAX_FULL_K else _pick(w.nb, _K_FULL_TILES)
    nn = N // tnn

    def body(a_ref, w_ref, o_ref, *scratch):
        part = lax.dot_general(a_ref[...].astype(BF16), w_ref[...], (((1,), (1,)), ((), ())),
                               preferred_element_type=F32)
        if nn == 1:
            o_ref[...] = part.astype(o_ref.dtype)
            return
        acc = scratch[0]
        n = pl.program_id(2)

        @pl.when(n == 0)
        def _():
            acc[...] = part

        @pl.when(n > 0)
        def _():
            acc[...] += part

        @pl.when(n == nn - 1)
        def _():
            o_ref[...] = acc[...].astype(o_ref.dtype)

    return _tc_call(
        body, name=name, grid=(M // tm, w.K // tkk, nn),
        in_specs=[pl.BlockSpec((tm, tnn), lambda i, j, n: (i, n)),
                  w.spec(tkk, tnn, lambda i, j, n: j, lambda i, j, n: n)],
        out_specs=pl.BlockSpec((tm, tkk), lambda i, j, n: (i, j)),
        out_shape=jax.ShapeDtypeStruct((M, w.K), out_dtype),
        scratch_shapes=[] if nn == 1 else [pltpu.VMEM((tm, tkk), F32)],
        compiler_params=_params(("parallel", "parallel", "arbitrary")),
    )(a, w.arr)


def _mm_tn(name, a, b, nb=None):
    T, K = a.shape
    _, N = b.shape
    width = N if nb is None else nb
    assert T <= MAX_FULL_T
    tkk, tn = _pick(K, (512, 256, 128)), _pick(width, _N_TILES)
    per = width // tn

    def body(a_ref, b_ref, o_ref):
        o_ref[...] = lax.dot_general(a_ref[...].astype(BF16), b_ref[...].astype(BF16), (((0,), (0,)), ((), ())),
                                     preferred_element_type=F32).astype(o_ref.dtype)

    if nb is None:
        out_spec = pl.BlockSpec((tkk, tn), lambda j, i: (i, j))
        out_shape = jax.ShapeDtypeStruct((K, N), BF16)
    else:
        out_spec = pl.BlockSpec((None, tkk, tn), lambda j, i: (j // per, i, j % per))
        out_shape = jax.ShapeDtypeStruct((N_DEV, K, nb), BF16)
    return _tc_call(
        body, name=name, grid=(N // tn, K // tkk),
        in_specs=[pl.BlockSpec((T, tkk), lambda j, i: (0, i)),
                  pl.BlockSpec((T, tn), lambda j, i: (0, j))],
        out_specs=out_spec, out_shape=out_shape,
        compiler_params=_params(("parallel", "parallel")),
    )(a, b)


def _rowwise(name, fn, rows, consts, outs, accs=(), tile=256, ncol=1, nrows=None):
    n_r, n_c, n_o = len(rows), len(consts), len(outs)
    T = rows[0][0].shape[0] if nrows is None else nrows
    tile = _pick(T, tuple(t for t in (512, 256, 128, 64, 32, 16, 8) if t <= tile))
    nt = T // tile
    const_rows = [c[1] if isinstance(c, tuple) else None for c in consts]
    consts = [c[0] if isinstance(c, tuple) else c for c in consts]

    def body(*refs):
        vals = [r[...] for r in refs[:n_r]]
        vals += [r[...] if l is None else r[l:l + 1, :] for r, l in zip(refs[n_r:n_r + n_c], const_rows)]
        res = fn(*vals)
        if not isinstance(res, (tuple, list)):
            res = (res,)
        o_refs = refs[n_r + n_c:n_r + n_c + n_o]
        a_refs = refs[n_r + n_c + n_o:]
        for r, v in zip(o_refs, res[:n_o]):
            r[...] = v.astype(r.dtype)
        first = pl.program_id(1) == 0
        for r, v in zip(a_refs, res[n_o:]):
            @pl.when(first)
            def _(r=r, v=v):
                r[...] = v.astype(F32)

            @pl.when(jnp.logical_not(first))
            def _(r=r, v=v):
                r[...] += v.astype(F32)

    in_specs, args = [], []
    for arr, w, off, roff in rows:
        rb = roff // tile
        assert roff % tile == 0
        in_specs.append(pl.BlockSpec((tile, w), lambda j, i, off=off, rb=rb: (i + rb, off + j)))
        args.append(arr)
    for cst in consts:
        in_specs.append(pl.BlockSpec(cst.shape, lambda j, i: (0, 0)))
        args.append(cst)
    out_specs, out_shape = [], []
    for tw, dt in outs:
        out_specs.append(pl.BlockSpec((tile, tw // ncol), lambda j, i: (i, j)))
        out_shape.append(jax.ShapeDtypeStruct((T, tw), dt))
    for nr, tw in accs:
        out_specs.append(pl.BlockSpec((nr, tw // ncol), lambda j, i: (0, j)))
        out_shape.append(jax.ShapeDtypeStruct((nr, tw), F32))
    res = _tc_call(
        body, name=name, grid=(ncol, nt), in_specs=in_specs, out_specs=out_specs, out_shape=out_shape,
        compiler_params=_params(("parallel", "arbitrary")),
    )(*args)
    return res


def _rms_fwd(name, x, g):
    D = x.shape[1]
    return _rowwise(name, lambda xv, gv: _rms(xv, gv), [(x, D, 0, 0)], [g], [(D, BF16)])[0]


def _rms_bwd(name, x, g, dh, dx_in):
    D = x.shape[1]

    def fn(xv, dhv, dxv, gv):
        _, vjp = jax.vjp(_rms, xv, gv)
        dx, dg = vjp(dhv.astype(F32))
        tot = dx + dxv
        return tot, tot, jnp.sum(dg, axis=0, keepdims=True)

    return _rowwise(name, fn, [(x, D, 0, 0), (dh, D, 0, 0), (dx_in, D, 0, 0)], [g], [(D, F32), (D, BF16)], [(1, D)],
                    tile=256)


def _lag_views(win, K, R, forward):
    n = win.shape[0]
    for r in range(8):
        if r >= K:
            break
        if r == 0:
            rolled = win
        else:
            rolled = pltpu.roll(win, (n - r) if forward else r, axis=0)
        for q in range((K - 1 - r) // 8 + 1):
            s = 8 * q + r
            if forward:
                yield s, rolled[8 * q:8 * q + R]
            else:
                yield s, rolled[HALO - 8 * q:HALO - 8 * q + R]


def _conv_chunk(win, w_ref, K, R):
    acc = None
    for s, view in _lag_views(win, K, R, forward=False):
        term = w_ref[K - 1 - s:K - s, :] * view
        acc = term if acc is None else acc + term
    return acc


def _conv_chunk_t(win, w_ref, K, R):
    acc = None
    for s, view in _lag_views(win, K, R, forward=True):
        term = w_ref[K - 1 - s:K - s, :] * view
        acc = term if acc is None else acc + term
    return acc


def _conv_dw(xwin, dy, K, R):
    taps = [None] * K
    for s, view in _lag_views(xwin, K, R, forward=False):
        taps[K - 1 - s] = jnp.sum(dy * view, axis=0, keepdims=True)
    return taps


def _chunks(T):
    R = _pick(T, (128, 64, 32))
    return R, T // R


def _glu_conv_fwd(name, proj, w, cw_total):
    T = proj.shape[0]
    C = cw_total
    cw = LANE
    nb = C // cw
    R, nch = _chunks(T)

    def body(a_ref, b_ref, w_ref, o_ref, s_ref):
        s_ref[0:HALO, :] = jnp.zeros((HALO, cw), F32)

        def fill(i, _):
            r0 = pl.multiple_of(i * R, R)
            s_ref[pl.ds(HALO + r0, R), :] = a_ref[pl.ds(r0, R), :] * _sigmoid(b_ref[pl.ds(r0, R), :])
            return 0

        lax.fori_loop(0, nch, fill, 0)

        def conv(i, _):
            r0 = pl.multiple_of(i * R, R)
            o_ref[pl.ds(r0, R), :] = _conv_chunk(s_ref[pl.ds(r0, R + HALO), :], w_ref, CONV_K, R)
            return 0

        lax.fori_loop(0, nch, conv, 0)

    return _tc_call(
        body, name=name, grid=(nb,),
        in_specs=[pl.BlockSpec((T, cw), lambda j: (0, j)), pl.BlockSpec((T, cw), lambda j: (0, nb + j)),
                  pl.BlockSpec((HALO, cw), lambda j: (0, j))],
        out_specs=pl.BlockSpec((T, cw), lambda j: (0, j)),
        out_shape=jax.ShapeDtypeStruct((T, C), F32),
        scratch_shapes=[pltpu.VMEM((T + HALO, cw), F32)],
        compiler_params=_params(("parallel",)),
    )(proj, proj, w)


def _glu_conv_bwd(name, proj, w, dhc, cw_total):
    T = proj.shape[0]
    C = cw_total
    cw = LANE
    nb = C // cw
    R, nch = _chunks(T)

    def body(a_ref, b_ref, w_ref, dy_ref, da_ref, db_ref, dw_ref, s_ref, g_ref, acc_ref):
        s_ref[0:HALO, :] = jnp.zeros((HALO, cw), F32)
        g_ref[T:T + HALO, :] = jnp.zeros((HALO, cw), F32)
        acc_ref[...] = jnp.zeros_like(acc_ref)

        def fill(i, _):
            r0 = pl.multiple_of(i * R, R)
            s_ref[pl.ds(HALO + r0, R), :] = a_ref[pl.ds(r0, R), :] * _sigmoid(b_ref[pl.ds(r0, R), :])
            g_ref[pl.ds(r0, R), :] = dy_ref[pl.ds(r0, R), :]
            return 0

        lax.fori_loop(0, nch, fill, 0)

        def back(i, _):
            r0 = pl.multiple_of(i * R, R)
            dhg = _conv_chunk_t(g_ref[pl.ds(r0, R + HALO), :], w_ref, CONV_K, R)
            av = a_ref[pl.ds(r0, R), :]
            sg = _sigmoid(b_ref[pl.ds(r0, R), :])
            da_ref[pl.ds(r0, R), :] = (dhg * sg).astype(da_ref.dtype)
            db_ref[pl.ds(r0, R), :] = (dhg * av * sg * (1.0 - sg)).astype(db_ref.dtype)
            taps = _conv_dw(s_ref[pl.ds(r0, R + HALO), :], dy_ref[pl.ds(r0, R), :], CONV_K, R)
            for k, tap in enumerate(taps):
                acc_ref[k:k + 1, :] += tap
            return 0

        lax.fori_loop(0, nch, back, 0)
        dw_ref[...] = acc_ref[...]

    return _tc_call(
        body, name=name, grid=(nb,),
        in_specs=[pl.BlockSpec((T, cw), lambda j: (0, j)), pl.BlockSpec((T, cw), lambda j: (0, nb + j)),
                  pl.BlockSpec((HALO, cw), lambda j: (0, j)), pl.BlockSpec((T, cw), lambda j: (0, j))],
        out_specs=[pl.BlockSpec((T, cw), lambda j: (0, j)), pl.BlockSpec((T, cw), lambda j: (0, j)),
                   pl.BlockSpec((HALO, cw), lambda j: (0, j))],
        out_shape=[jax.ShapeDtypeStruct((T, C), BF16), jax.ShapeDtypeStruct((T, C), BF16),
                   jax.ShapeDtypeStruct((HALO, C), F32)],
        scratch_shapes=[pltpu.VMEM((T + HALO, cw), F32), pltpu.VMEM((T + HALO, cw), F32),
                        pltpu.VMEM((HALO, cw), F32)],
        compiler_params=_params(("parallel",)),
    )(proj, proj, w, dhc)


def _ffn_conv_fwd(name, up, w, dff):
    T = up.shape[0]
    cw = LANE
    nb = dff // cw
    R, nch = _chunks(T)

    def body(g_ref, v_ref, wg_ref, wv_ref, o_ref, sg_ref, sv_ref):
        sg_ref[0:HALO, :] = jnp.zeros((HALO, cw), F32)
        sv_ref[0:HALO, :] = jnp.zeros((HALO, cw), F32)

        def fill(i, _):
            r0 = pl.multiple_of(i * R, R)
            sg_ref[pl.ds(HALO + r0, R), :] = g_ref[pl.ds(r0, R), :]
            sv_ref[pl.ds(HALO + r0, R), :] = v_ref[pl.ds(r0, R), :]
            return 0

        lax.fori_loop(0, nch, fill, 0)

        def conv(i, _):
            r0 = pl.multiple_of(i * R, R)
            gc = _conv_chunk(sg_ref[pl.ds(r0, R + HALO), :], wg_ref, FFN_K, R)
            vc = _conv_chunk(sv_ref[pl.ds(r0, R + HALO), :], wv_ref, FFN_K, R)
            o_ref[pl.ds(r0, R), :] = (_silu(gc) * vc).astype(o_ref.dtype)
            return 0

        lax.fori_loop(0, nch, conv, 0)

    return _tc_call(
        body, name=name, grid=(nb,),
        in_specs=[pl.BlockSpec((T, cw), lambda j: (0, j)), pl.BlockSpec((T, cw), lambda j: (0, nb + j)),
                  pl.BlockSpec((8, cw), lambda j: (0, j)), pl.BlockSpec((8, cw), lambda j: (0, nb + j))],
        out_specs=pl.BlockSpec((T, cw), lambda j: (0, j)),
        out_shape=jax.ShapeDtypeStruct((T, dff), BF16),
        scratch_shapes=[pltpu.VMEM((T + HALO, cw), F32), pltpu.VMEM((T + HALO, cw), F32)],
        compiler_params=_params(("parallel",)),
    )(up, up, w, w)


def _ffn_conv_bwd(name, up, w, dact, dff):
    T = up.shape[0]
    cw = LANE
    nb = dff // cw
    R, nch = _chunks(T)

    def body(g_ref, v_ref, wg_ref, wv_ref, da_ref, dg_ref, dv_ref, dwg_ref, dwv_ref,
             sg_ref, sv_ref, tg_ref, tv_ref, ag_ref, av_ref):
        zero = jnp.zeros((HALO, cw), F32)
        sg_ref[0:HALO, :] = zero
        sv_ref[0:HALO, :] = zero
        tg_ref[T:T + HALO, :] = zero
        tv_ref[T:T + HALO, :] = zero
        ag_ref[...] = jnp.zeros_like(ag_ref)
        av_ref[...] = jnp.zeros_like(av_ref)

        def fill(i, _):
            r0 = pl.multiple_of(i * R, R)
            sg_ref[pl.ds(HALO + r0, R), :] = g_ref[pl.ds(r0, R), :]
            sv_ref[pl.ds(HALO + r0, R), :] = v_ref[pl.ds(r0, R), :]
            return 0

        lax.fori_loop(0, nch, fill, 0)

        def grads(i, _):
            r0 = pl.multiple_of(i * R, R)
            gwin = sg_ref[pl.ds(r0, R + HALO), :]
            vwin = sv_ref[pl.ds(r0, R + HALO), :]
            gc = _conv_chunk(gwin, wg_ref, FFN_K, R)
            vc = _conv_chunk(vwin, wv_ref, FFN_K, R)
            da = da_ref[pl.ds(r0, R), :].astype(F32)
            sg = _sigmoid(gc)
            dgc = da * vc * (sg * (1.0 + gc * (1.0 - sg)))
            dvc = da * (gc * sg)
            tg_ref[pl.ds(r0, R), :] = dgc
            tv_ref[pl.ds(r0, R), :] = dvc
            for k, tap in enumerate(_conv_dw(gwin, dgc, FFN_K, R)):
                ag_ref[k:k + 1, :] += tap
            for k, tap in enumerate(_conv_dw(vwin, dvc, FFN_K, R)):
                av_ref[k:k + 1, :] += tap
            return 0

        lax.fori_loop(0, nch, grads, 0)

        def back(i, _):
            r0 = pl.multiple_of(i * R, R)
            dg_ref[pl.ds(r0, R), :] = _conv_chunk_t(tg_ref[pl.ds(r0, R + HALO), :], wg_ref, FFN_K, R).astype(dg_ref.dtype)
            dv_ref[pl.ds(r0, R), :] = _conv_chunk_t(tv_ref[pl.ds(r0, R + HALO), :], wv_ref, FFN_K, R).astype(dv_ref.dtype)
            return 0

        lax.fori_loop(0, nch, back, 0)
        dwg_ref[...] = ag_ref[...]
        dwv_ref[...] = av_ref[...]

    col = lambda j: (0, j)
    dg, dv, dwg, dwv = _tc_call(
        body, name=name, grid=(nb,),
        in_specs=[pl.BlockSpec((T, cw), col), pl.BlockSpec((T, cw), lambda j: (0, nb + j)),
                  pl.BlockSpec((8, cw), col), pl.BlockSpec((8, cw), lambda j: (0, nb + j)),
                  pl.BlockSpec((T, cw), col)],
        out_specs=[pl.BlockSpec((T, cw), col), pl.BlockSpec((T, cw), col),
                   pl.BlockSpec((8, cw), col), pl.BlockSpec((8, cw), col)],
        out_shape=[jax.ShapeDtypeStruct((T, dff), BF16), jax.ShapeDtypeStruct((T, dff), BF16),
                   jax.ShapeDtypeStruct((8, dff), F32), jax.ShapeDtypeStruct((8, dff), F32)],
        scratch_shapes=[pltpu.VMEM((T + HALO, cw), F32), pltpu.VMEM((T + HALO, cw), F32),
                        pltpu.VMEM((T + HALO, cw), F32), pltpu.VMEM((T + HALO, cw), F32),
                        pltpu.VMEM((8, cw), F32), pltpu.VMEM((8, cw), F32)],
        compiler_params=_params(("parallel",)),
    )(up, up, w, w, dact)
    return jnp.concatenate([dg, dv], axis=1), jnp.concatenate([dwg, dwv], axis=1)


def _zoh(a_re, a_im, log_dt):
    ar = jnp.minimum(a_re, -1e-4)
    ai = a_im
    dt = jnp.exp(log_dt)
    mag = jnp.exp(dt * ar)
    abar_re = mag * jnp.cos(dt * ai)
    abar_im = mag * jnp.sin(dt * ai)
    den = ar * ar + ai * ai
    nr = abar_re - 1.0
    ni = abar_im
    return abar_re, abar_im, (nr * ar + ni * ai) / den, (ni * ar - nr * ai) / den


def _discretize(a_re, a_im, log_dt, a_re_h, a_im_h, log_dt_h, b_re, b_im):
    abar_re, abar_im, _, _ = _zoh(a_re, a_im, log_dt)
    _, _, z_re, z_im = _zoh(a_re_h, a_im_h, log_dt_h)
    return abar_re, abar_im, z_re * b_re - z_im * b_im, z_re * b_im + z_im * b_re


def _full_specs(arrs):
    return [pl.BlockSpec(a.shape, lambda *_, n=len(a.shape): (0,) * n) for a in arrs]


def _ssm_prep(name, raw):
    def body(*refs):
        res = _discretize(*[r[...] for r in refs[:8]])
        for r, v in zip(refs[8:], res):
            r[...] = v

    outs = [jax.ShapeDtypeStruct(raw[0].shape, F32)] * 2 + [jax.ShapeDtypeStruct(raw[6].shape, F32)] * 2
    return _tc_call(body, name=name, in_specs=_full_specs(raw), out_specs=_full_specs(outs), out_shape=outs)(*raw)


def _ssm_prep_bwd(name, raw, cots):
    G = raw[0].shape[0]
    H = raw[3].shape[0] // G

    def body(*refs):
        _, vjp = jax.vjp(_discretize, *[r[...] for r in refs[:8]])
        g = vjp(tuple(r[...] for r in refs[8:12]))
        outs = refs[12:]
        for k in range(3):
            rep = g[3 + k]
            outs[k][...] = g[k] + jnp.sum(rep.reshape(G, H, rep.shape[1]), axis=1)
        outs[3][...] = g[6]
        outs[4][...] = g[7]

    outs = [jax.ShapeDtypeStruct(a.shape, F32) for a in (raw[0], raw[1], raw[2], raw[6], raw[7])]
    return _tc_call(body, name=name, in_specs=_full_specs(list(raw) + list(cots)), out_specs=_full_specs(outs),
                          out_shape=outs)(*raw, *cots)


def _cmul(ar, ai, br, bi):
    return ar * br - ai * bi, ar * bi + ai * br


def _scan_coefs(ar, ai, reverse):
    W = ar.shape[1]
    row = lax.broadcasted_iota(jnp.int32, (8, W), 0)
    p = [None] * 9
    p[1] = (ar, ai)
    for n in range(2, 9):
        p[n] = _cmul(*p[n // 2], *p[n - n // 2])
    steps = []
    for s in (1, 2, 4):
        valid = (row <= 7 - s) if reverse else (row >= s)
        steps.append((jnp.where(valid, p[s][0], 0.0), jnp.where(valid, p[s][1], 0.0)))
    pr = jnp.zeros((8, W), F32)
    pi = jnp.zeros((8, W), F32)
    for i in range(8):
        n = (8 - i) if reverse else (i + 1)
        pr = jnp.where(row == i, p[n][0], pr)
        pi = jnp.where(row == i, p[n][1], pi)
    return steps, (pr, pi)


def _scan_tile(xr, xi, cr, ci, coefs, reverse):
    steps, (pr, pi) = coefs
    for s, (sr, si) in zip((1, 2, 4), steps):
        shift = (8 - s) if reverse else s
        rr = pltpu.roll(xr, shift, axis=0)
        ri = pltpu.roll(xi, shift, axis=0)
        xr, xi = xr + sr * rr - si * ri, xi + sr * ri + si * rr
    xr, xi = xr + pr * cr - pi * ci, xi + pr * ci + pi * cr
    return xr, xi


def _edge_rows(x, reverse):
    W = x.shape[1]
    return jnp.broadcast_to(x[0:1, :] if reverse else x[7:8, :], (8, W))


def _power_table(ar, ai, pw_r, pw_i, n):
    W = ar.shape[1]
    a8r, a8i = jnp.broadcast_to(ar, (8, W)), jnp.broadcast_to(ai, (8, W))
    pr, pi = a8r, a8i
    for k in range(n):
        pw_r[8 * k:8 * k + 8, :] = pr
        pw_i[8 * k:8 * k + 8, :] = pi
        pr, pi = _cmul(pr, pi, a8r, a8i)


def _segment_order(Q):
    r = jnp.arange(Q)
    return (jnp.arange(Q)[None, :] == ((r % 8) * (Q // 8) + r // 8)[:, None]).astype(BF16)


def _to_segments(u, g, pm):
    C = u.shape[1]
    hi = u.astype(BF16)
    lo = (u - hi.astype(F32)).astype(BF16)
    moved = jnp.dot(pm, jnp.concatenate([hi, lo, g], axis=1), preferred_element_type=F32)
    return moved[:, :C].astype(BF16), moved[:, :C] + moved[:, C:2 * C], moved[:, 2 * C:]


def _segment_scan(xr_ref, xi_ref, row0, seg, ar, ai, pw_r, pw_i, carry_r, carry_i, reverse, visit=None, extra=()):
    W = ar.shape[1]
    sign = -1.0 if reverse else 1.0
    a8r, a8i = jnp.broadcast_to(ar, (8, W)), sign * jnp.broadcast_to(ai, (8, W))
    tile = lambda j: pl.ds(pl.multiple_of(row0 + 8 * j, 8), 8)

    def local(jj, x):
        j = seg - 1 - jj if reverse else jj
        xr = a8r * x[0] - a8i * x[1] + xr_ref[tile(j), :]
        xi = a8r * x[1] + a8i * x[0] + xi_ref[tile(j), :]
        xr_ref[tile(j), :] = xr
        xi_ref[tile(j), :] = xi
        return xr, xi

    zero = jnp.zeros((8, W), F32)
    xr, xi = lax.fori_loop(0, seg, local, (zero, zero))
    row = lax.broadcasted_iota(jnp.int32, (8, W), 0)
    edge = 7 if reverse else 0
    shift = 7 if reverse else 1
    gr = jnp.where(row == edge, carry_r, pltpu.roll(xr, shift, axis=0))
    gi = jnp.where(row == edge, carry_i, pltpu.roll(xi, shift, axis=0))
    top = 8 * (seg - 1)
    alr, ali = pw_r[top:top + 1, :], sign * pw_i[top:top + 1, :]
    er, ei = _scan_tile(gr, gi, zero, zero, _scan_coefs(alr, ali, reverse), reverse)
    far_r = _edge_rows(alr * er - ali * ei + xr, reverse)
    far_i = _edge_rows(alr * ei + ali * er + xi, reverse)

    def fix(j, ex):
        k = pl.ds(pl.multiple_of(8 * (seg - 1 - j if reverse else j), 8), 8)
        pr, pi = pw_r[k, :], sign * pw_i[k, :]
        fr = xr_ref[tile(j), :] + pr * er - pi * ei
        fi = xi_ref[tile(j), :] + pr * ei + pi * er
        xr_ref[tile(j), :] = fr
        xi_ref[tile(j), :] = fi
        return ex if visit is None else visit(j, fr, fi, ex)

    extra = lax.fori_loop(0, seg, fix, extra)
    return er, ei, far_r, far_i, extra


def _ssm_chunk(T):
    return _pick(T, (512, 256, 128, 64))


def _ssm_fwd(name, proj, u_off, p, width):
    T = proj.shape[0]
    NB = width // LANE
    Q = _ssm_chunk(T)
    nch = T // Q
    W = SSM_LANES

    def body(u_ref, bre, bim, cre, cim, ar_ref, ai_ref, d_ref, y_ref, ckr_ref, cki_ref, br_s, bi_s, car_r, car_i):
        c = pl.program_id(1)

        @pl.when(c == 0)
        def _():
            car_r[...] = jnp.zeros_like(car_r)
            car_i[...] = jnp.zeros_like(car_i)

        ckr_ref[...] = car_r[...]
        cki_ref[...] = car_i[...]
        u = u_ref[...]
        u16 = u.astype(BF16)
        br_s[...] = jnp.dot(u16, bre[...], preferred_element_type=F32)
        bi_s[...] = jnp.dot(u16, bim[...], preferred_element_type=F32)
        coefs = _scan_coefs(ar_ref[...], ai_ref[...], False)

        def tile(j, carry):
            r0 = pl.multiple_of(j * 8, 8)
            xr, xi = _scan_tile(br_s[pl.ds(r0, 8), :], bi_s[pl.ds(r0, 8), :], carry[0], carry[1], coefs, False)
            br_s[pl.ds(r0, 8), :] = xr
            bi_s[pl.ds(r0, 8), :] = xi
            return _edge_rows(xr, False), _edge_rows(xi, False)

        cr, ci = lax.fori_loop(0, Q // 8, tile, (car_r[...], car_i[...]))
        car_r[...] = cr
        car_i[...] = ci
        nt = (((1,), (1,)), ((), ()))
        y = (lax.dot_general(br_s[...].astype(BF16), cre[...], nt, preferred_element_type=F32)
             - lax.dot_general(bi_s[...].astype(BF16), cim[...], nt, preferred_element_type=F32)
             + d_ref[...] * u)
        y_ref[...] = _gelu(y).astype(y_ref.dtype)

    blk = lambda b, c: (b, 0, 0)
    mat = pl.BlockSpec((None, LANE, W), blk)
    vec = pl.BlockSpec((None, 1, W), blk)
    ck = pl.BlockSpec((None, None, 8, W), lambda b, c: (b, c, 0, 0))
    return _tc_call(
        body, name=name, grid=(NB, nch),
        in_specs=[pl.BlockSpec((Q, LANE), lambda b, c: (c, u_off + b)), mat, mat, mat, mat, vec, vec,
                  pl.BlockSpec((1, LANE), lambda b, c: (0, b))],
        out_specs=[pl.BlockSpec((Q, LANE), lambda b, c: (c, b)), ck, ck],
        out_shape=[jax.ShapeDtypeStruct((T, width), BF16), jax.ShapeDtypeStruct((NB, nch, 8, W), F32),
                   jax.ShapeDtypeStruct((NB, nch, 8, W), F32)],
        scratch_shapes=[pltpu.VMEM((Q, W), F32), pltpu.VMEM((Q, W), F32), pltpu.VMEM((8, W), F32),
                        pltpu.VMEM((8, W), F32)],
        compiler_params=_params(("parallel", "arbitrary")),
    )(proj, p["bre"], p["bim"], p["cre"], p["cim"], p["ar"], p["ai"], p["d"])


def _ssm_bwd(name, proj, u_off, p, ck_r, ck_i, dyg, width):
    T = proj.shape[0]
    NB = width // LANE
    Q = _ssm_chunk(T)
    nch = T // Q
    W = SSM_LANES
    nt_dims = (((1,), (1,)), ((), ()))
    tn_dims = (((0,), (0,)), ((), ()))

    def body(u_ref, dy_ref, pm_ref, pmt_ref, ckr_ref, cki_ref, bre, bim, cre, cim, ar_ref, ai_ref, d_ref,
             du_ref, dbr_ref, dbi_ref, dcr_ref, dci_ref, dar_ref, dai_ref, dd_ref,
             xr_s, xi_s, lr_s, li_s, lam_r, lam_i, pw_r, pw_i):
        c = pl.program_id(1)
        seg = Q // 8
        ar, ai = ar_ref[...], ai_ref[...]

        @pl.when(c == 0)
        def _():
            lam_r[...] = jnp.zeros_like(lam_r)
            lam_i[...] = jnp.zeros_like(lam_i)
            for r in (dbr_ref, dbi_ref, dcr_ref, dci_ref, dar_ref, dai_ref, dd_ref):
                r[...] = jnp.zeros_like(r)
            _power_table(ar, ai, pw_r, pw_i, seg)

        u16, u, dyg = _to_segments(u_ref[...], dy_ref[...], pm_ref[...])
        xr_s[8:Q + 8, :] = jnp.dot(u16, bre[...], preferred_element_type=F32)
        xi_s[8:Q + 8, :] = jnp.dot(u16, bim[...], preferred_element_type=F32)
        enter_r, enter_i, _, _, _ = _segment_scan(xr_s, xi_s, 8, seg, ar, ai, pw_r, pw_i, ckr_ref[...], cki_ref[...],
                                                  False)
        xr_s[0:8, :] = enter_r
        xi_s[0:8, :] = enter_i
        xr16 = xr_s[8:Q + 8, :].astype(BF16)
        xi16 = xi_s[8:Q + 8, :].astype(BF16)
        y = (lax.dot_general(xr16, cre[...], nt_dims, preferred_element_type=F32)
             - lax.dot_general(xi16, cim[...], nt_dims, preferred_element_type=F32) + d_ref[...] * u)
        _, gelu_vjp = jax.vjp(_gelu, y)
        dy = gelu_vjp(dyg)[0]
        dy16 = dy.astype(BF16)
        dd_ref[...] += jnp.broadcast_to(jnp.sum(dy * u, axis=0, keepdims=True), (8, LANE))
        dcr_ref[...] += lax.dot_general(dy16, xr16, tn_dims, preferred_element_type=F32)
        dci_ref[...] -= lax.dot_general(dy16, xi16, tn_dims, preferred_element_type=F32)
        lr_s[...] = jnp.dot(dy16, cre[...], preferred_element_type=F32)
        li_s[...] = -jnp.dot(dy16, cim[...], preferred_element_type=F32)

        def visit(j, lr, li, sums):
            before = pl.ds(pl.multiple_of(8 * j, 8), 8)
            xpr, xpi = xr_s[before, :], xi_s[before, :]
            return sums[0] + lr * xpr + li * xpi, sums[1] + li * xpr - lr * xpi

        zero = jnp.zeros((8, W), F32)
        _, _, cr, ci, sums = _segment_scan(lr_s, li_s, 0, seg, ar, ai, pw_r, pw_i, lam_r[...], lam_i[...], True, visit,
                                           (zero, zero))
        lam_r[...] = cr
        lam_i[...] = ci
        dar_ref[...] += jnp.broadcast_to(jnp.sum(sums[0], axis=0, keepdims=True), (8, W))
        dai_ref[...] += jnp.broadcast_to(jnp.sum(sums[1], axis=0, keepdims=True), (8, W))
        lr16 = lr_s[...].astype(BF16)
        li16 = li_s[...].astype(BF16)
        dbr_ref[...] += lax.dot_general(u16, lr16, tn_dims, preferred_element_type=F32)
        dbi_ref[...] += lax.dot_general(u16, li16, tn_dims, preferred_element_type=F32)
        du = (lax.dot_general(lr16, bre[...], nt_dims, preferred_element_type=F32)
              + lax.dot_general(li16, bim[...], nt_dims, preferred_element_type=F32) + d_ref[...] * dy)
        du_ref[...] = jnp.dot(pmt_ref[...], du.astype(BF16), preferred_element_type=F32).astype(du_ref.dtype)

    blk = lambda b, c: (b, 0, 0)
    mat = pl.BlockSpec((None, LANE, W), blk)
    vec = pl.BlockSpec((None, 1, W), blk)
    acc8 = pl.BlockSpec((None, 8, W), blk)
    ck = pl.BlockSpec((None, None, 8, W), lambda b, c: (b, nch - 1 - c, 0, 0))
    order = pl.BlockSpec((Q, Q), lambda b, c: (0, 0))
    pm = _segment_order(Q)
    return _tc_call(
        body, name=name, grid=(NB, nch),
        in_specs=[pl.BlockSpec((Q, LANE), lambda b, c: (nch - 1 - c, u_off + b)),
                  pl.BlockSpec((Q, LANE), lambda b, c: (nch - 1 - c, b)), order, order, ck, ck, mat, mat, mat, mat,
                  vec, vec, pl.BlockSpec((1, LANE), lambda b, c: (0, b))],
        out_specs=[pl.BlockSpec((Q, LANE), lambda b, c: (nch - 1 - c, b)), mat, mat, mat, mat, acc8, acc8,
                   pl.BlockSpec((None, 8, LANE), blk)],
        out_shape=[jax.ShapeDtypeStruct((T, width), BF16)] + [jax.ShapeDtypeStruct((NB, LANE, W), F32)] * 4
                  + [jax.ShapeDtypeStruct((NB, 8, W), F32)] * 2 + [jax.ShapeDtypeStruct((NB, 8, LANE), F32)],
        scratch_shapes=[pltpu.VMEM((Q + 8, W), F32), pltpu.VMEM((Q + 8, W), F32), pltpu.VMEM((Q, W), F32),
                        pltpu.VMEM((Q, W), F32), pltpu.VMEM((8, W), F32), pltpu.VMEM((8, W), F32),
                        pltpu.VMEM((Q, W), F32), pltpu.VMEM((Q, W), F32)],
        compiler_params=_params(("parallel", "arbitrary")),
    )(proj, dyg, pm, pm.T, ck_r, ck_i, p["bre"], p["bim"], p["cre"], p["cim"], p["ar"], p["ai"], p["d"])


def _block_diag(w):
    G, H, P = w.shape
    eye = jnp.eye(8, dtype=w.dtype)
    return (w.reshape(G // 8, 8, H, 1, P) * eye[None, :, None, :, None]).reshape(G // 8, 8 * H, 8 * P)


def _block_diag_t(d, H, P):
    NB = d.shape[0]
    d = d.reshape(NB, 8, H, 8, P)
    eye = jnp.eye(8, dtype=d.dtype)
    return jnp.sum(d * eye[None, :, None, :, None], axis=3).reshape(NB * 8, H, P)


def _attn_fwd(name, q, kv, heads):
    T, D = q.shape
    Mm = kv.shape[0]
    hd = D // heads
    tq = _pick(T, (512, 256, 128))
    scale = hd ** -0.5

    def body(q_ref, k_ref, v_ref, o_ref):
        s = lax.dot_general(q_ref[...], k_ref[...], (((1,), (1,)), ((), ())), preferred_element_type=F32) * scale
        s = s - jnp.max(s, axis=-1, keepdims=True)
        e = jnp.exp(s)
        p = e / jnp.sum(e, axis=-1, keepdims=True)
        o_ref[...] = jnp.dot(p.astype(BF16), v_ref[...], preferred_element_type=F32).astype(o_ref.dtype)

    return _tc_call(
        body, name=name, grid=(heads, T // tq),
        in_specs=[pl.BlockSpec((tq, hd), lambda h, i: (i, h)), pl.BlockSpec((Mm, hd), lambda h, i: (0, h)),
                  pl.BlockSpec((Mm, hd), lambda h, i: (0, heads + h))],
        out_specs=pl.BlockSpec((tq, hd), lambda h, i: (i, h)),
        out_shape=jax.ShapeDtypeStruct((T, D), BF16),
        compiler_params=_params(("parallel", "parallel")),
    )(q, kv, kv)


def _attn_bwd(name, q, kv, do, heads):
    T, D = q.shape
    Mm = kv.shape[0]
    hd = D // heads
    tq = _pick(T, (512, 256, 128))
    scale = hd ** -0.5
    nt_dims = (((1,), (1,)), ((), ()))
    tn_dims = (((0,), (0,)), ((), ()))

    def body(q_ref, k_ref, v_ref, do_ref, dq_ref, dk_ref, dv_ref):
        i = pl.program_id(1)

        @pl.when(i == 0)
        def _():
            dk_ref[...] = jnp.zeros_like(dk_ref)
            dv_ref[...] = jnp.zeros_like(dv_ref)

        qv, kvl, vv, dov = q_ref[...], k_ref[...], v_ref[...], do_ref[...]
        s = lax.dot_general(qv, kvl, nt_dims, preferred_element_type=F32) * scale
        s = s - jnp.max(s, axis=-1, keepdims=True)
        e = jnp.exp(s)
        p = e / jnp.sum(e, axis=-1, keepdims=True)
        p16 = p.astype(BF16)
        dv_ref[...] += lax.dot_general(p16, dov, tn_dims, preferred_element_type=F32)
        dp = lax.dot_general(dov, vv, nt_dims, preferred_element_type=F32)
        ds = (p * (dp - jnp.sum(dp * p, axis=-1, keepdims=True)) * scale).astype(BF16)
        dq_ref[...] = jnp.dot(ds, kvl, preferred_element_type=F32).astype(dq_ref.dtype)
        dk_ref[...] += lax.dot_general(ds, qv, tn_dims, preferred_element_type=F32)

    return _tc_call(
        body, name=name, grid=(heads, T // tq),
        in_specs=[pl.BlockSpec((tq, hd), lambda h, i: (i, h)), pl.BlockSpec((Mm, hd), lambda h, i: (0, h)),
                  pl.BlockSpec((Mm, hd), lambda h, i: (0, heads + h)), pl.BlockSpec((tq, hd), lambda h, i: (i, h))],
        out_specs=[pl.BlockSpec((tq, hd), lambda h, i: (i, h)), pl.BlockSpec((Mm, hd), lambda h, i: (0, h)),
                   pl.BlockSpec((Mm, hd), lambda h, i: (0, h))],
        out_shape=[jax.ShapeDtypeStruct((T, D), BF16), jax.ShapeDtypeStruct((Mm, D), F32),
                   jax.ShapeDtypeStruct((Mm, D), F32)],
        compiler_params=_params(("parallel", "arbitrary")),
    )(q, kv, kv, do)


def _position():
    return lax.axis_index("x"), lax.axis_index("y"), lax.axis_index("c")


def _allgather(name, blk, row_mode):
    L = blk.shape[0]
    out_shape = (L, N_DEV) + blk.shape[1:] if row_mode else (N_DEV,) + blk.shape
    x_ref = jax.new_ref(blk, memory_space=pltpu.MemorySpace.HBM)
    out_ref = jax.empty_ref(jax.ShapeDtypeStruct(out_shape, blk.dtype), memory_space=pltpu.MemorySpace.HBM)

    def body(send_sems, recv_sems, local_sem):
        x, y, c = _position()
        me, sibling = (x, y, c), (x, y, 1 - c)
        chips = [(1 - x, y), (x, 1 - y), (1 - x, 1 - y)]
        barrier = pltpu.get_barrier_semaphore()
        for peer in [sibling] + [(*chip, c) for chip in chips]:
            pl.semaphore_signal(barrier, inc=1, device_id=peer, device_id_type=MESH_ID)
        pl.semaphore_wait(barrier, 4)

        def slot(px, py, pc):
            b = 4 * px + 2 * py + pc
            return out_ref.at[:, b] if row_mode else out_ref.at[b]

        def copy(k, block, to, src=None):
            return pltpu.make_async_remote_copy(
                src_ref=slot(*block) if src is None else src, dst_ref=slot(*block),
                send_sem=send_sems.at[k], recv_sem=recv_sems.at[k], device_id=to, device_id_type=MESH_ID)

        mine = pltpu.make_async_copy(x_ref, slot(*me), local_sem)
        mine.start()
        first = [copy(0, me, sibling, src=x_ref)]
        first += [copy(1 + j, me, (*chip, c), src=x_ref) for j, chip in enumerate(chips)]
        for cp in first:
            cp.start()
        passed = [copy(4 + j, (*chip, c), sibling) for j, chip in enumerate(chips)]
        for j, chip in enumerate(chips):
            copy(1 + j, (*chip, c), me).wait_recv()
            passed[j].start()
        copy(0, sibling, me).wait_recv()
        for j, chip in enumerate(chips):
            copy(4 + j, (*chip, 1 - c), me).wait_recv()
        for cp in first + passed:
            cp.wait_send()
        mine.wait()

    pl.kernel(
        body, mesh=plsc.ScalarSubcoreMesh(axis_name="sequencer", num_cores=1), name=name,
        scratch_types=(pltpu.SemaphoreType.DMA((7,)), pltpu.SemaphoreType.DMA((7,)), pltpu.SemaphoreType.DMA),
        compiler_params=pltpu.CompilerParams(collective_id=AG_COLLECTIVE_ID),
    )()
    return out_ref[...]


def _sequencer_kernel(name, body, scratch_types, collective_id):
    pl.kernel(
        body, mesh=plsc.ScalarSubcoreMesh(axis_name="sequencer", num_cores=1), name=name,
        scratch_types=scratch_types, compiler_params=pltpu.CompilerParams(collective_id=collective_id),
    )()


def _handshake(peers):
    barrier = pltpu.get_barrier_semaphore()
    for peer in peers:
        pl.semaphore_signal(barrier, inc=1, device_id=peer, device_id_type=MESH_ID)
    pl.semaphore_wait(barrier, len(peers))


def _rs_sibling(name, grads):
    hbm = pltpu.MemorySpace.HBM
    g_ref = jax.new_ref(grads, memory_space=hbm)
    out_ref = jax.empty_ref(jax.ShapeDtypeStruct((4,) + grads.shape[1:], grads.dtype), memory_space=hbm)

    def body(send_sems, recv_sems):
        x, y, c = _position()
        sibling = (x, y, 1 - c)
        _handshake([sibling])
        copies = [pltpu.make_async_remote_copy(
            src_ref=g_ref.at[2 * q + (1 - c)], dst_ref=out_ref.at[q], send_sem=send_sems.at[q],
            recv_sem=recv_sems.at[q], device_id=sibling, device_id_type=MESH_ID) for q in range(4)]
        for cp in copies:
            cp.start()
        for cp in copies:
            cp.wait_recv()
        for cp in copies:
            cp.wait_send()

    _sequencer_kernel(name, body, (pltpu.SemaphoreType.DMA((4,)), pltpu.SemaphoreType.DMA((4,))),
                      RS_SIBLING_COLLECTIVE_ID)
    return out_ref[...]


def _rs_chips(name, part):
    hbm = pltpu.MemorySpace.HBM
    p_ref = jax.new_ref(part, memory_space=hbm)
    out_ref = jax.empty_ref(jax.ShapeDtypeStruct(part.shape, part.dtype), memory_space=hbm)

    def body(send_sems, recv_sems, local_sem):
        x, y, c = _position()
        flips = [(1 - x, y), (x, 1 - y), (1 - x, 1 - y)]
        _handshake([(fx, fy, c) for fx, fy in flips])
        mine = pltpu.make_async_copy(p_ref.at[2 * x + y], out_ref.at[3], local_sem)
        copies = [pltpu.make_async_remote_copy(
            src_ref=p_ref.at[2 * fx + fy], dst_ref=out_ref.at[k], send_sem=send_sems.at[k], recv_sem=recv_sems.at[k],
            device_id=(fx, fy, c), device_id_type=MESH_ID) for k, (fx, fy) in enumerate(flips)]
        for cp in copies:
            cp.start()
        mine.start()
        for cp in copies:
            cp.wait_recv()
        for cp in copies:
            cp.wait_send()
        mine.wait()

    _sequencer_kernel(name, body, (pltpu.SemaphoreType.DMA((3,)), pltpu.SemaphoreType.DMA((3,)),
                                   pltpu.SemaphoreType.DMA), RS_CHIPS_COLLECTIVE_ID)
    return out_ref[...]


def _adamw_math(g, w, m, v):
    m = ADAM_B1 * m + (1.0 - ADAM_B1) * g
    v = ADAM_B2 * v + (1.0 - ADAM_B2) * (g * g)
    m_hat = m / (1.0 - ADAM_B1 ** ADAM_STEP)
    v_hat = v / (1.0 - ADAM_B2 ** ADAM_STEP)
    delta = -ADAM_LR * (m_hat / (jnp.sqrt(v_hat) + ADAM_EPS) + ADAM_WD * w)
    return delta, m, v


def _adamw_natural(name, gs, ws, ms, vs, lead_block=None):
    n = len(gs)

    def body(*refs):
        ins, outs = refs[:4 * n], refs[4 * n:]
        for k in range(n):
            delta, nm, nv = _adamw_math(*[ins[j * n + k][...] for j in range(4)])
            outs[k][...] = delta
            outs[n + k][...] = nm
            outs[2 * n + k][...] = nv

    out_shape = [jax.ShapeDtypeStruct(w.shape, F32) for w in ws] * 3
    if lead_block is None:
        grid = ()
        spec = lambda a: pl.BlockSpec(a.shape, lambda nd=len(a.shape): (0,) * nd)
    else:
        grid = tuple(d // b for d, b in zip(ws[0].shape[:2], lead_block))
        spec = lambda a: pl.BlockSpec(tuple(lead_block) + tuple(a.shape[2:]),
                                      lambda i, j, nd=len(a.shape): (i, j) + (0,) * (nd - 2))
    return _tc_call(
        body, name=name, grid=grid, in_specs=[spec(a) for a in list(gs) + list(ws) + list(ms) + list(vs)],
        out_specs=[spec(o) for o in out_shape], out_shape=out_shape,
        compiler_params=pltpu.CompilerParams(vmem_limit_bytes=VMEM_LIMIT),
    )(*gs, *ws, *ms, *vs)


def _ew_tile(R, C, nblocks):
    budget = (VMEM_LIMIT * 3) // 4
    return _pick(R, tuple(t for t in (1024, 512, 256, 128, 64, 32, 16, 8) if 8 * t * C * nblocks <= budget))


def _pair_sum(name, grads, landed, c_idx):
    _, r, c = grads.shape
    tile = _ew_tile(r, c, 3)

    def body(c_ref, g_ref, s_ref, o_ref):
        o_ref[...] = (g_ref[...].astype(F32) + s_ref[...].astype(F32)).astype(o_ref.dtype)

    return _tc_call(
        body, name=name,
        grid_spec=pltpu.PrefetchScalarGridSpec(
            num_scalar_prefetch=1, grid=(4, r // tile),
            in_specs=[pl.BlockSpec((None, tile, c), lambda q, i, c_ref: (2 * q + c_ref[0], i, 0)),
                      pl.BlockSpec((None, tile, c), lambda q, i, c_ref: (q, i, 0))],
            out_specs=pl.BlockSpec((None, tile, c), lambda q, i, c_ref: (q, i, 0))),
        out_shape=jax.ShapeDtypeStruct((4, r, c), BF16),
        compiler_params=_params(("parallel", "parallel")),
    )(c_idx, grads, landed)


def _adamw_layer(name, got, w, m, v, layer, prev):
    L, r, c = w.shape
    tile = _ew_tile(r, c, 11)

    def body(g0, g1, g2, g3, w_ref, m_ref, v_ref, *rest):
        outs = rest[-4:]
        g = (g0[...].astype(F32) + g1[...].astype(F32)) + (g2[...].astype(F32) + g3[...].astype(F32))
        delta, nm, nv = _adamw_math(g, w_ref[...], m_ref[...], v_ref[...])
        for ref, val in zip(outs, (g, delta, nm, nv)):
            ref[...] = val

    slab = pl.BlockSpec((None, tile, c), lambda i: (layer, i, 0))
    in_specs = [pl.BlockSpec((None, tile, c), lambda i, k=k: (k, i, 0)) for k in range(4)] + [slab] * 3
    args = [got, got, got, got, w, m, v]
    aliases = {}
    if prev is not None:
        in_specs += [ANY] * 4
        args += list(prev)
        aliases = {7 + k: k for k in range(4)}
    return _tc_call(
        body, name=name, grid=(r // tile,), in_specs=in_specs, out_specs=[slab] * 4,
        out_shape=[jax.ShapeDtypeStruct((L, r, c), F32)] * 4, input_output_aliases=aliases,
        compiler_params=_params(("parallel",)),
    )(*args)


def _reduce_pipeline(name, layer, grads, w, m, v, state):
    c_idx = lax.axis_index("c").astype(jnp.int32).reshape(1)
    landed = _rs_sibling(name + "_rs_sibling", grads)
    yield
    part = _pair_sum(name + "_pair_sum", grads, landed, c_idx)
    got = _rs_chips(name + "_rs_chips", part)
    for _ in range(4):
        yield
    state[name] = _adamw_layer(name + "_adamw", got, w, m, v, layer, state.get(name))


def _loss_head(name, x, g, target):
    T, D = x.shape

    def fn(xv, tv, gv):
        def f(xx, gg):
            err = _rms(xx, gg) - tv
            return 0.5 * jnp.sum(jnp.mean(err * err, axis=-1, keepdims=True))

        loss, (dx, dg) = jax.value_and_grad(f, argnums=(0, 1))(xv, gv)
        return dx, dx, jnp.full((8, LANE), loss, F32), jnp.sum(dg, axis=0, keepdims=True)

    dx, dx16, loss, dg = _rowwise(name, fn, [(x, D, 0, 0), (target, D, 0, 0)], [g], [(D, F32), (D, BF16)],
                                  [(8, LANE), (1, D)], tile=128)
    return loss[0, 0], dx, dx16, dg


_SHARDED_COL = ("w_in", "conv_dw_w", "conv_w_pw", "ssm_w_glu", "xa_w_kv", "ffn_w_up", "ffn_dw_w")
_SHARDED_ROW = ("w_out", "xa_w_q", "xa_w_o", "ffn_w_down")
_WEIGHTS = ['mix_norm_g', 'w_in', 'conv_dw_w', 'conv_dw_b', 'conv_ln_g', 'conv_ln_b', 'conv_w_pw', 'ssm_a_re',
            'ssm_a_im', 'ssm_log_dt', 'ssm_b_re', 'ssm_b_im', 'ssm_c_re', 'ssm_c_im', 'ssm_d', 'ssm_w_glu', 'w_out',
            'xa_norm_g', 'mem_norm_g', 'xa_w_q', 'xa_w_kv', 'xa_w_o', 'ffn_norm_g', 'ffn_w_up', 'ffn_dw_w',
            'ffn_w_down', 'final_norm_g']
_FWD = ['x', 'mem'] + _WEIGHTS
_AG_ORDER = ("w_in", "conv_dw_w", "conv_w_pw", "ssm_w_glu", "w_out", "xa_w_q", "xa_w_kv", "xa_w_o", "ffn_w_up",
             "ffn_dw_w", "ffn_w_down")


def _pad_rows(a, rows):
    return jnp.pad(a, ((0, 0), (0, rows - a.shape[1]), (0, 0)))


def _step(inp, target, mom_m, mom_v):
    _LAST_CALL.clear()
    x0 = inp["x"][0]
    mem = inp["mem"][0]
    T, D = x0.shape
    L = inp["w_in"].shape[0]
    CW = inp["conv_dw_b"].shape[1]
    SW = inp["ssm_d"].shape[1]
    DFF = inp["ffn_w_down"].shape[1] * N_DEV
    G = SW // SSM_GROUP
    NB = SW // LANE
    u_off = (2 * CW) // LANE
    gate_off = (2 * CW + SW) // 1024

    gathered = {n: [None] * L for n in _AG_ORDER}
    filter_rows = {"conv_dw_w": HALO, "ffn_dw_w": 8}
    for l in range(L):
        for n in _AG_ORDER:
            blk = inp[n][l:l + 1]
            blk = _pad_rows(blk, filter_rows[n]) if n in filter_rows else blk.astype(BF16)
            if n in _SHARDED_COL:
                gathered[n][l] = _allgather("ag_" + n, blk, False)
            else:
                full = _allgather("ag_" + n, blk, True)
                gathered[n][l] = full.reshape(1, N_DEV * full.shape[2], full.shape[3])

    def W(n, l):
        return _W(gathered[n][l], 0, n in _SHARDED_COL)

    def dw_filter(n, l):
        g = gathered[n][l]
        return jnp.transpose(g[:, 0], (1, 0, 2)).reshape(g.shape[2], N_DEV * g.shape[3])

    def row(n, l):
        return (inp[n], l)

    ssm_raw, ssm_p = [], []
    for l in range(L):
        a_re, a_im, ldt = inp["ssm_a_re"][l], inp["ssm_a_im"][l], inp["ssm_log_dt"][l][:, None]
        rep = lambda a: jnp.repeat(a, SSM_GROUP, axis=0)
        flat = lambda b: jnp.transpose(b, (0, 2, 1)).reshape(G * SSM_GROUP, SSM_STATE)
        raw = (a_re, a_im, ldt, rep(a_re), rep(a_im), rep(ldt), flat(inp["ssm_b_re"][l]), flat(inp["ssm_b_im"][l]))
        abar_re, abar_im, bbar_re, bbar_im = _ssm_prep("ssm_prep", raw)
        bbar_re = bbar_re.reshape(G, SSM_GROUP, SSM_STATE)
        bbar_im = bbar_im.reshape(G, SSM_GROUP, SSM_STATE)
        ssm_raw.append(raw)
        ssm_p.append(dict(
            bre=_block_diag(bbar_re).astype(BF16), bim=_block_diag(bbar_im).astype(BF16),
            cre=_block_diag(inp["ssm_c_re"][l]).astype(BF16), cim=_block_diag(inp["ssm_c_im"][l]).astype(BF16),
            ar=abar_re.reshape(NB, 1, SSM_LANES), ai=abar_im.reshape(NB, 1, SSM_LANES), d=inp["ssm_d"][l][None, :]))

    saved = []
    x = x0
    for l in range(L):
        s = {"x_in": x}
        s["h1"] = _rms_fwd("rms_mix", x, row("mix_norm_g", l))
        s["proj"] = _mm_nn("mm_w_in", s["h1"], W("w_in", l), F32)
        s["hc"] = _glu_conv_fwd("conv_fwd", s["proj"], dw_filter("conv_dw_w", l), CW)
        s["hs"] = _rowwise("conv_post", _convpost, [(s["hc"], CW, 0, 0)],
                           [row("conv_dw_b", l), row("conv_ln_g", l), row("conv_ln_b", l)], [(CW, BF16)])[0]
        s["ya"] = _mm_nn("mm_w_pw", s["hs"], W("conv_w_pw", l), F32)
        s["yg"], s["ck_r"], s["ck_i"] = _ssm_fwd("ssm_fwd", s["proj"], u_off, ssm_p[l], SW)
        s["gg"] = _mm_nn("mm_w_glu", s["yg"], W("ssm_w_glu", l), F32)
        nmix = D // 1024
        mix_rows = [(s["proj"], 1024, gate_off, 0), (s["proj"], 1024, gate_off + nmix, 0), (s["ya"], 1024, 0, 0),
                    (s["gg"], 1024, 0, 0), (s["gg"], 1024, nmix, 0)]
        s["mix_rows"] = mix_rows
        s["mix"] = _rowwise("mix_fwd", _mixf, mix_rows, [], [(D, BF16)], ncol=nmix)[0]
        x = _mm_nn("mm_w_out", s["mix"], W("w_out", l), F32, add=x)
        s["x1"] = x
        s["h2"] = _rms_fwd("rms_xa", x, row("xa_norm_g", l))
        s["q"] = _mm_nn("mm_w_q", s["h2"], W("xa_w_q", l), BF16)
        s["mn"] = _rms_fwd("rms_mem", mem, row("mem_norm_g", l))
        s["kv"] = _mm_nn("mm_w_kv", s["mn"], W("xa_w_kv", l), BF16)
        s["o"] = _attn_fwd("attn_fwd", s["q"], s["kv"], XA_HEADS)
        x = _mm_nn("mm_w_o", s["o"], W("xa_w_o", l), F32, add=x)
        s["x2"] = x
        s["h3"] = _rms_fwd("rms_ffn", x, row("ffn_norm_g", l))
        s["up"] = _mm_nn("mm_w_up", s["h3"], W("ffn_w_up", l), F32)
        s["act"] = _ffn_conv_fwd("ffn_conv_fwd", s["up"], dw_filter("ffn_dw_w", l), DFF)
        x = _mm_nn("mm_w_down", s["act"], W("ffn_w_down", l), F32, add=x)
        saved.append(s)

    loss_part, dx, dx16, d_final_g = _loss_head("loss_head", x, inp["final_norm_g"][None, :], target[0])

    big = _SHARDED_COL + _SHARDED_ROW
    small = {n: [None] * L for n in _WEIGHTS if n not in big and n != "final_norm_g"}
    pads = {"conv_dw_w": HALO, "ffn_dw_w": 8}
    shards = {n: tuple(_pad_rows(a, pads[n]) if n in pads else a for a in (inp[n], mom_m[n], mom_v[n])) for n in big}
    state, queue = {}, []
    names = [n for n in _WEIGHTS if n not in big]
    sizes = [inp[n].size for n in names]
    total = sum(sizes)
    pack_w = 8 * LANE
    rows_p = -(-total // (LANE * pack_w)) * LANE
    padn = rows_p * pack_w - total

    def pack(parts, fill):
        return jnp.concatenate([p.reshape(-1) for p in parts] + [jnp.full((padn,), fill, F32)]).reshape(rows_p, pack_w)

    def total(*parts):
        tot = parts[0]
        for part in parts[1:]:
            tot = tot + part
        return tot

    def tick():
        for gen in list(queue):
            if next(gen, "done") == "done":
                queue.remove(gen)

    def emit(n, l, g):
        queue.append(_reduce_pipeline(n, l, g.astype(BF16), *shards[n], state))
        tick()

    deferred = []

    def emit_small(n, l, thunk):
        if l == 0:
            deferred.append((n, thunk))
        else:
            emit(n, l, thunk())

    for l in reversed(range(L)):
        s = saved[l]
        dact = _mm_nt("mm_w_down_t", dx16, W("ffn_w_down", l), BF16)
        emit("ffn_w_down", l, _mm_tn("mm_dw_down", s["act"], dx16).reshape(N_DEV, DFF // N_DEV, D))
        d_up, d_ffn_dw = _ffn_conv_bwd("ffn_conv_bwd", s["up"], dw_filter("ffn_dw_w", l), dact, DFF)
        emit("ffn_dw_w", l, jnp.transpose(d_ffn_dw.reshape(8, N_DEV, 2 * DFF // N_DEV), (1, 0, 2)))
        dh3 = _mm_nt("mm_w_up_t", d_up, W("ffn_w_up", l), BF16)
        emit("ffn_w_up", l, _mm_tn("mm_dw_up", s["h3"], d_up, nb=2 * DFF // N_DEV))
        dx, dx16, small["ffn_norm_g"][l] = _rms_bwd("rms_ffn_bwd", s["x2"], row("ffn_norm_g", l), dh3, dx)
        do = _mm_nt("mm_w_o_t", dx16, W("xa_w_o", l), BF16)
        emit_small("xa_w_o", l, lambda a=s["o"], b=dx16: _mm_tn("mm_dw_o", a, b).reshape(N_DEV, D // N_DEV, D))
        dq, dk, dv = _attn_bwd("attn_bwd", s["q"], s["kv"], do, XA_HEADS)
        dkv = jnp.concatenate([dk, dv], axis=1).astype(BF16)
        dh2 = _mm_nt("mm_w_q_t", dq, W("xa_w_q", l), BF16)
        emit_small("xa_w_q", l, lambda a=s["h2"], b=dq: _mm_tn("mm_dw_q", a, b).reshape(N_DEV, D // N_DEV, D))
        dmn = _mm_nt("mm_w_kv_t", dkv, W("xa_w_kv", l), BF16)
        emit("xa_w_kv", l, _mm_tn("mm_dw_kv", s["mn"], dkv, nb=2 * D // N_DEV))

        def mem_bwd(mv, dv_, gv):
            _, vjp = jax.vjp(_rms, mv, gv)
            return jnp.sum(vjp(dv_.astype(F32))[1], axis=0, keepdims=True)

        small["mem_norm_g"][l] = _rowwise("rms_mem_bwd", mem_bwd, [(mem, D, 0, 0), (dmn, D, 0, 0)],
                                          [row("mem_norm_g", l)], [], [(1, D)], tile=128)[0]
        dx, dx16, small["xa_norm_g"][l] = _rms_bwd("rms_xa_bwd", s["x1"], row("xa_norm_g", l), dh2, dx)
        dmix = _mm_nt("mm_w_out_t", dx16, W("w_out", l), BF16)
        emit_small("w_out", l, lambda a=s["mix"], b=dx16: _mm_tn("mm_dw_out", a, b).reshape(N_DEV, D // N_DEV, D))

        def mix_bwd(gla, glb, ya, ga, gb, dm):
            _, vjp = jax.vjp(_mixf, gla, glb, ya, ga, gb)
            return vjp(dm.astype(F32))

        nmix = D // 1024
        dgla, dglb, dya, dga, dgb = _rowwise("mix_bwd", mix_bwd, s["mix_rows"] + [(dmix, 1024, 0, 0)], [],
                                             [(D, BF16)] * 5, ncol=nmix)
        dgg = jnp.concatenate([dga, dgb], axis=1)
        dyg = _mm_nt("mm_w_glu_t", dgg, W("ssm_w_glu", l), BF16)
        emit_small("ssm_w_glu", l, lambda a=s["yg"], b=dgg: _mm_tn("mm_dw_glu", a, b, nb=2 * D // N_DEV))
        du, dbr, dbi, dcr, dci, dar, dai, dd = _ssm_bwd("ssm_bwd", s["proj"], u_off, ssm_p[l], s["ck_r"], s["ck_i"],
                                                        dyg, SW)
        cots = (dar[:, 0, :].reshape(G, SSM_STATE), dai[:, 0, :].reshape(G, SSM_STATE),
                _block_diag_t(dbr, SSM_GROUP, SSM_STATE).reshape(G * SSM_GROUP, SSM_STATE),
                _block_diag_t(dbi, SSM_GROUP, SSM_STATE).reshape(G * SSM_GROUP, SSM_STATE))
        g_are, g_aim, g_ldt, g_bre, g_bim = _ssm_prep_bwd("ssm_prep_bwd", ssm_raw[l], cots)
        small["ssm_a_re"][l], small["ssm_a_im"][l], small["ssm_log_dt"][l] = g_are, g_aim, g_ldt[:, 0]
        small["ssm_b_re"][l] = jnp.transpose(g_bre.reshape(G, SSM_GROUP, SSM_STATE), (0, 2, 1))
        small["ssm_b_im"][l] = jnp.transpose(g_bim.reshape(G, SSM_GROUP, SSM_STATE), (0, 2, 1))
        small["ssm_c_re"][l] = _block_diag_t(dcr, SSM_GROUP, SSM_STATE)
        small["ssm_c_im"][l] = _block_diag_t(dci, SSM_GROUP, SSM_STATE)
        small["ssm_d"][l] = dd[:, 0, :].reshape(SW)
        dhs = _mm_nt("mm_w_pw_t", dya, W("conv_w_pw", l), BF16)
        emit("conv_w_pw", l, _mm_tn("mm_dw_pw", s["hs"], dya, nb=D // N_DEV))

        def post_bwd(hc, dh, b, lg, lb):
            _, vjp = jax.vjp(_convpost, hc, b, lg, lb)
            dhc, db, dlg, dlb = vjp(dh.astype(F32))
            return dhc, jnp.sum(db, axis=0, keepdims=True), jnp.sum(dlg, axis=0, keepdims=True), \
                jnp.sum(dlb, axis=0, keepdims=True)

        dhc, small["conv_dw_b"][l], small["conv_ln_g"][l], small["conv_ln_b"][l] = _rowwise(
            "conv_post_bwd", post_bwd, [(s["hc"], CW, 0, 0), (dhs, CW, 0, 0)],
            [row("conv_dw_b", l), row("conv_ln_g", l), row("conv_ln_b", l)], [(CW, F32)], [(1, CW)] * 3)
        if l == 0:
            early = {**small, "mix_norm_g": [jnp.zeros((1, D), F32)] + small["mix_norm_g"][1:]}
            flat_g = jnp.concatenate([(jnp.stack(early[n]) if n != "final_norm_g" else d_final_g).reshape(-1)
                                      for n in names])
            g_all = _allgather("ag_small_grads", pack([flat_g], 0.0)[None], False)
        da, db, d_conv_dw = _glu_conv_bwd("conv_bwd", s["proj"], dw_filter("conv_dw_w", l), dhc, CW)
        emit("conv_dw_w", l, jnp.transpose(d_conv_dw.reshape(HALO, N_DEV, CW // N_DEV), (1, 0, 2)))
        dproj = jnp.concatenate([da, db, du, dgla, dglb], axis=1)
        dh1 = _mm_nt("mm_w_in_t", dproj, W("w_in", l), BF16)
        dx, dx16, small["mix_norm_g"][l] = _rms_bwd("rms_mix_bwd", s["x_in"], row("mix_norm_g", l), dh1, dx)
        if l == 0:
            assert names[0] == "mix_norm_g" and D % pack_w == 0 and D // pack_w <= 8
            late = jnp.pad(small["mix_norm_g"][0].reshape(D // pack_w, pack_w), ((0, 8 - D // pack_w), (0, 0)))
            g_late = _allgather("ag_mix_grad", late[None], False)
            g_sum = _rowwise("small_grad_sum", total, [(g_all.reshape(N_DEV * rows_p, pack_w), pack_w, 0, k * rows_p)
                                                       for k in range(N_DEV)], [], [(pack_w, F32)], tile=LANE,
                             nrows=rows_p)[0]
        emit("w_in", l, _mm_tn("mm_dw_in", s["h1"], dproj, nb=dproj.shape[1] // N_DEV))
    for k, (n, thunk) in enumerate(deferred):
        emit(n, 0, thunk())
        if k == 0:
            head = _rowwise("mix_grad_sum", total,
                            [(g_sum, pack_w, 0, 0)] + [(g_late.reshape(N_DEV * 8, pack_w), pack_w, 0, 8 * d)
                                                       for d in range(N_DEV)], [], [(pack_w, F32)], tile=8, nrows=8)[0]

    while queue:
        tick()
    results = {n: [o[:, :inp[n].shape[1], :] for o in state[n]] if n in pads else state[n] for n in big}
    g_sum = jnp.concatenate([head, g_sum[8:]], axis=0)
    g_flat, grads, offs = g_sum.reshape(-1), {}, 0
    for n, sz in zip(names, sizes):
        grads[n] = g_flat[offs:offs + sz].reshape(inp[n].shape)
        offs += sz
    sparse = [n for n in names if inp[n].ndim == 4 and inp[n].shape[-1] < SSM_STATE]
    dense = [n for n in names if n not in sparse]
    as2d = lambda a: a[None, :] if a.ndim == 1 else a
    for group, block in ((dense, None), (sparse, (1, 16))):
        outs = _adamw_natural("small_adamw", *[[as2d(src[n]) for n in group] for src in (grads, inp, mom_m, mom_v)],
                              lead_block=block)
        for k, n in enumerate(group):
            results[n] = [grads[n]] + [outs[j * len(group) + k].reshape(inp[n].shape) for j in range(3)]

    _LAST_CALL.clear()
    loss = lax.psum(loss_part, ("x", "y", "c"))
    grad_x = dx[None]
    return (loss, grad_x, *[results[n][0] for n in _WEIGHTS], *[results[n][1] for n in _WEIGHTS],
            *[results[n][2] for n in _WEIGHTS], *[results[n][3] for n in _WEIGHTS])


def kernel(x, mem, mix_norm_g, w_in, conv_dw_w, conv_dw_b, conv_ln_g, conv_ln_b, conv_w_pw, ssm_a_re, ssm_a_im, ssm_log_dt, ssm_b_re, ssm_b_im, ssm_c_re, ssm_c_im, ssm_d, ssm_w_glu, w_out, xa_norm_g, mem_norm_g, xa_w_q, xa_w_kv, xa_w_o, ffn_norm_g, ffn_w_up, ffn_dw_w, ffn_w_down, final_norm_g, loss_target, m_mix_norm_g, m_w_in, m_conv_dw_w, m_conv_dw_b, m_conv_ln_g, m_conv_ln_b, m_conv_w_pw, m_ssm_a_re, m_ssm_a_im, m_ssm_log_dt, m_ssm_b_re, m_ssm_b_im, m_ssm_c_re, m_ssm_c_im, m_ssm_d, m_ssm_w_glu, m_w_out, m_xa_norm_g, m_mem_norm_g, m_xa_w_q, m_xa_w_kv, m_xa_w_o, m_ffn_norm_g, m_ffn_w_up, m_ffn_dw_w, m_ffn_w_down, m_final_norm_g, v_mix_norm_g, v_w_in, v_conv_dw_w, v_conv_dw_b, v_conv_ln_g, v_conv_ln_b, v_conv_w_pw, v_ssm_a_re, v_ssm_a_im, v_ssm_log_dt, v_ssm_b_re, v_ssm_b_im, v_ssm_c_re, v_ssm_c_im, v_ssm_d, v_ssm_w_glu, v_w_out, v_xa_norm_g, v_mem_norm_g, v_xa_w_q, v_xa_w_kv, v_xa_w_o, v_ffn_norm_g, v_ffn_w_up, v_ffn_dw_w, v_ffn_w_down, v_final_norm_g):
    args = (x, mem, mix_norm_g, w_in, conv_dw_w, conv_dw_b, conv_ln_g, conv_ln_b, conv_w_pw, ssm_a_re, ssm_a_im, ssm_log_dt, ssm_b_re, ssm_b_im, ssm_c_re, ssm_c_im, ssm_d, ssm_w_glu, w_out, xa_norm_g, mem_norm_g, xa_w_q, xa_w_kv, xa_w_o, ffn_norm_g, ffn_w_up, ffn_dw_w, ffn_w_down, final_norm_g)
    ms = (m_mix_norm_g, m_w_in, m_conv_dw_w, m_conv_dw_b, m_conv_ln_g, m_conv_ln_b, m_conv_w_pw, m_ssm_a_re, m_ssm_a_im, m_ssm_log_dt, m_ssm_b_re, m_ssm_b_im, m_ssm_c_re, m_ssm_c_im, m_ssm_d, m_ssm_w_glu, m_w_out, m_xa_norm_g, m_mem_norm_g, m_xa_w_q, m_xa_w_kv, m_xa_w_o, m_ffn_norm_g, m_ffn_w_up, m_ffn_dw_w, m_ffn_w_down, m_final_norm_g)
    vs = (v_mix_norm_g, v_w_in, v_conv_dw_w, v_conv_dw_b, v_conv_ln_g, v_conv_ln_b, v_conv_w_pw, v_ssm_a_re, v_ssm_a_im, v_ssm_log_dt, v_ssm_b_re, v_ssm_b_im, v_ssm_c_re, v_ssm_c_im, v_ssm_d, v_ssm_w_glu, v_w_out, v_xa_norm_g, v_mem_norm_g, v_xa_w_q, v_xa_w_kv, v_xa_w_o, v_ffn_norm_g, v_ffn_w_up, v_ffn_dw_w, v_ffn_w_down, v_final_norm_g)
    return _step(dict(zip(_FWD, args)), loss_target, dict(zip(_WEIGHTS, ms)), dict(zip(_WEIGHTS, vs)))
```

```python
import functools

import jax
import jax.numpy as jnp
from jax import lax
from jax.experimental import pallas as pl
from jax.experimental.pallas import tpu as pltpu
from jax.experimental.pallas import tpu_sc as plsc

F32 = jnp.float32
BF16 = jnp.bfloat16
MESH_ID = pl.DeviceIdType.MESH
N_DEV = 8
EPS = 1e-6
VMEM_LIMIT = 48 * 1024 * 1024
ANY = pl.BlockSpec(memory_space=pl.ANY)

ADAM_LR = 0.001
ADAM_B1 = 0.9
ADAM_B2 = 0.999
ADAM_EPS = 1e-08
ADAM_WD = 0.01
ADAM_STEP = 10

CONV_K = 31
FFN_K = 3
XA_HEADS = 4
SSM_GROUP = 16
SSM_STATE = 64
HALO = 32
LANE = 128
SSM_LANES = 512
AG_COLLECTIVE_ID = 1
RS_SIBLING_COLLECTIVE_ID = 2
RS_CHIPS_COLLECTIVE_ID = 3


def _pick(n, prefs):
    for p in prefs:
        if p <= n and n % p == 0:
            return p
    return n


def _params(sem, vmem=VMEM_LIMIT):
    return pltpu.CompilerParams(dimension_semantics=sem, vmem_limit_bytes=vmem)


_LAST_CALL = []


def _tc_call(*call_args, **call_kwargs):
    call = pl.pallas_call(*call_args, **call_kwargs)

    def run(*args):
        args = list(args)
        if _LAST_CALL:
            i = next(k for k, a in enumerate(args) if a.ndim >= 2)
            args[i] = lax.optimization_barrier((args[i], _LAST_CALL[0]))[0]
        out = call(*args)
        _LAST_CALL[:] = [out[0] if isinstance(out, (tuple, list)) else out]
        return out

    return run


def _sigmoid(x):
    return 1.0 / (1.0 + jnp.exp(-x))


def _silu(x):
    return x * _sigmoid(x)


def _gelu(x):
    return 0.5 * x * (1.0 + jnp.tanh(0.7978845608028654 * (x + 0.044715 * (x * x * x))))


def _rms(x, g):
    return x * lax.rsqrt(jnp.mean(x * x, axis=-1, keepdims=True) + EPS) * g


def _convpost(hc, bias, ln_g, ln_b):
    h = hc + bias
    mu = jnp.mean(h, axis=-1, keepdims=True)
    xc = h - mu
    y = xc * lax.rsqrt(jnp.mean(xc * xc, axis=-1, keepdims=True) + EPS)
    return _silu(y * ln_g + ln_b)


def _mixf(gla, glb, ya, ga, gb):
    return _sigmoid(gla) * ya + _sigmoid(glb) * (ga * _sigmoid(gb))


class _W:
    def __init__(self, arr, layer, blocked):
        self.arr, self.layer, self.blocked = arr, layer, blocked
        if blocked:
            _, _, self.K, self.nb = arr.shape
            self.N = N_DEV * self.nb
        else:
            _, self.K, self.N = arr.shape
            self.nb = self.N

    def spec(self, tk, tn, ki, ni):
        l = self.layer
        if self.blocked:
            per = self.nb // tn
            return pl.BlockSpec((None, None, tk, tn), lambda *g: (ni(*g) // per, l, ki(*g), ni(*g) % per))
        return pl.BlockSpec((None, tk, tn), lambda *g: (l, ki(*g), ni(*g)))


_M_TILES = (1024, 512, 256, 128, 64, 32, 16, 8)
_N_TILES = (1408, 1024, 896, 512, 256, 128)
_K_TILES = (512, 1408, 896, 256, 128)
MAX_FULL_K = 2048
_K_FULL_TILES = (2048, 1408, 1024, 896, 512, 256, 128)
MAX_FULL_T = 4096


def _mm_nn(name, a, w, out_dtype, add=None):
    M, K = a.shape
    assert K == w.K
    tm, tn = _pick(M, _M_TILES), _pick(w.nb, _N_TILES)
    tk = K if K <= MAX_FULL_K else _pick(K, _K_FULL_TILES)
    nk = K // tk

    def body(*refs):
        a_ref, w_ref = refs[:2]
        r_ref = refs[2] if add is not None else None
        o_ref = refs[3] if add is not None else refs[2]
        part = jnp.dot(a_ref[...].astype(BF16), w_ref[...], preferred_element_type=F32)
        if nk == 1:
            o_ref[...] = (part if add is None else part + r_ref[...]).astype(o_ref.dtype)
            return
        acc = refs[-1]
        k = pl.program_id(2)

        @pl.when(k == 0)
        def _():
            acc[...] = part

        @pl.when(k > 0)
        def _():
            acc[...] += part

        @pl.when(k == nk - 1)
        def _():
            res = acc[...]
            if add is not None:
                res = res + r_ref[...]
            o_ref[...] = res.astype(o_ref.dtype)

    in_specs = [pl.BlockSpec((tm, tk), lambda i, j, k: (i, k)),
                w.spec(tk, tn, lambda i, j, k: k, lambda i, j, k: j)]
    args = [a, w.arr]
    if add is not None:
        in_specs.append(pl.BlockSpec((tm, tn), lambda i, j, k: (i, j)))
        args.append(add)
    return _tc_call(
        body, name=name, grid=(M // tm, w.N // tn, nk), in_specs=in_specs,
        out_specs=pl.BlockSpec((tm, tn), lambda i, j, k: (i, j)),
        out_shape=jax.ShapeDtypeStruct((M, w.N), out_dtype),
        scratch_shapes=[] if nk == 1 else [pltpu.VMEM((tm, tn), F32)],
        compiler_params=_params(("parallel", "parallel", "arbitrary")),
    )(*args)


def _mm_nt(name, a, w, out_dtype):
    M, N = a.shape
    assert N == w.N
    tm, tkk = _pick(M, _M_TILES), _pick(w.K, _N_TILES)
    tnn = w.nb if w.nb <= MAX_FULL_K else _pick(w.nb, _K_FULL_TILES)
    nn = N // tnn

    def body(a_ref, w_ref, o_ref, *scratch):
        part = lax.dot_general(a_ref[...].astype(BF16), w_ref[...], (((1,), (1,)), ((), ())),
                               preferred_element_type=F32)
        if nn == 1:
            o_ref[...] = part.astype(o_ref.dtype)
            return
        acc = scratch[0]
        n = pl.program_id(2)

        @pl.when(n == 0)
        def _():
            acc[...] = part

        @pl.when(n > 0)
        def _():
            acc[...] += part

        @pl.when(n == nn - 1)
        def _():
            o_ref[...] = acc[...].astype(o_ref.dtype)

    return _tc_call(
        body, name=name, grid=(M // tm, w.K // tkk, nn),
        in_specs=[pl.BlockSpec((tm, tnn), lambda i, j, n: (i, n)),
                  w.spec(tkk, tnn, lambda i, j, n: j, lambda i, j, n: n)],
        out_specs=pl.BlockSpec((tm, tkk), lambda i, j, n: (i, j)),
        out_shape=jax.ShapeDtypeStruct((M, w.K), out_dtype),
        scratch_shapes=[] if nn == 1 else [pltpu.VMEM((tm, tkk), F32)],
        compiler_params=_params(("parallel", "parallel", "arbitrary")),
    )(a, w.arr)


def _mm_tn(name, a, b, nb=None):
    T, K = a.shape
    _, N = b.shape
    width = N if nb is None else nb
    assert T <= MAX_FULL_T
    tkk, tn = _pick(K, (512, 256, 128)), _pick(width, _N_TILES)
    per = width // tn

    def body(a_ref, b_ref, o_ref):
        o_ref[...] = lax.dot_general(a_ref[...].astype(BF16), b_ref[...].astype(BF16), (((0,), (0,)), ((), ())),
                                     preferred_element_type=F32).astype(o_ref.dtype)

    if nb is None:
        out_spec = pl.BlockSpec((tkk, tn), lambda j, i: (i, j))
        out_shape = jax.ShapeDtypeStruct((K, N), BF16)
    else:
        out_spec = pl.BlockSpec((None, tkk, tn), lambda j, i: (j // per, i, j % per))
        out_shape = jax.ShapeDtypeStruct((N_DEV, K, nb), BF16)
    return _tc_call(
        body, name=name, grid=(N // tn, K // tkk),
        in_specs=[pl.BlockSpec((T, tkk), lambda j, i: (0, i)),
                  pl.BlockSpec((T, tn), lambda j, i: (0, j))],
        out_specs=out_spec, out_shape=out_shape,
        compiler_params=_params(("parallel", "parallel")),
    )(a, b)


def _rowwise(name, fn, rows, consts, outs, accs=(), tile=256, ncol=1, nrows=None):
    n_r, n_c, n_o = len(rows), len(consts), len(outs)
    T = rows[0][0].shape[0] if nrows is None else nrows
    tile = _pick(T, tuple(t for t in (512, 256, 128, 64, 32, 16, 8) if t <= tile))
    nt = T // tile
    const_rows = [c[1] if isinstance(c, tuple) else None for c in consts]
    consts = [c[0] if isinstance(c, tuple) else c for c in consts]

    def body(*refs):
        vals = [r[...] for r in refs[:n_r]]
        vals += [r[...] if l is None else r[l:l + 1, :] for r, l in zip(refs[n_r:n_r + n_c], const_rows)]
        res = fn(*vals)
        if not isinstance(res, (tuple, list)):
            res = (res,)
        o_refs = refs[n_r + n_c:n_r + n_c + n_o]
        a_refs = refs[n_r + n_c + n_o:]
        for r, v in zip(o_refs, res[:n_o]):
            r[...] = v.astype(r.dtype)
        first = pl.program_id(1) == 0
        for r, v in zip(a_refs, res[n_o:]):
            @pl.when(first)
            def _(r=r, v=v):
                r[...] = v.astype(F32)

            @pl.when(jnp.logical_not(first))
            def _(r=r, v=v):
                r[...] += v.astype(F32)

    in_specs, args = [], []
    for arr, w, off, roff in rows:
        rb = roff // tile
        assert roff % tile == 0
        in_specs.append(pl.BlockSpec((tile, w), lambda j, i, off=off, rb=rb: (i + rb, off + j)))
        args.append(arr)
    for cst in consts:
        in_specs.append(pl.BlockSpec(cst.shape, lambda j, i: (0, 0)))
        args.append(cst)
    out_specs, out_shape = [], []
    for tw, dt in outs:
        out_specs.append(pl.BlockSpec((tile, tw // ncol), lambda j, i: (i, j)))
        out_shape.append(jax.ShapeDtypeStruct((T, tw), dt))
    for nr, tw in accs:
        out_specs.append(pl.BlockSpec((nr, tw // ncol), lambda j, i: (0, j)))
        out_shape.append(jax.ShapeDtypeStruct((nr, tw), F32))
    res = _tc_call(
        body, name=name, grid=(ncol, nt), in_specs=in_specs, out_specs=out_specs, out_shape=out_shape,
        compiler_params=_params(("parallel", "arbitrary")),
    )(*args)
    return res


def _rms_fwd(name, x, g):
    D = x.shape[1]
    return _rowwise(name, lambda xv, gv: _rms(xv, gv), [(x, D, 0, 0)], [g], [(D, BF16)])[0]


def _rms_bwd(name, x, g, dh, dx_in):
    D = x.shape[1]

    def fn(xv, dhv, dxv, gv):
        _, vjp = jax.vjp(_rms, xv, gv)
        dx, dg = vjp(dhv.astype(F32))
        tot = dx + dxv
        return tot, tot, jnp.sum(dg, axis=0, keepdims=True)

    return _rowwise(name, fn, [(x, D, 0, 0), (dh, D, 0, 0), (dx_in, D, 0, 0)], [g], [(D, F32), (D, BF16)], [(1, D)],
                    tile=256)


def _lag_views(win, K, R, forward):
    n = win.shape[0]
    for r in range(8):
        if r >= K:
            break
        if r == 0:
            rolled = win
        else:
            rolled = pltpu.roll(win, (n - r) if forward else r, axis=0)
        for q in range((K - 1 - r) // 8 + 1):
            s = 8 * q + r
            if forward:
                yield s, rolled[8 * q:8 * q + R]
            else:
                yield s, rolled[HALO - 8 * q:HALO - 8 * q + R]


def _conv_chunk(win, w_ref, K, R):
    acc = None
    for s, view in _lag_views(win, K, R, forward=False):
        term = w_ref[K - 1 - s:K - s, :] * view
        acc = term if acc is None else acc + term
    return acc


def _conv_chunk_t(win, w_ref, K, R):
    acc = None
    for s, view in _lag_views(win, K, R, forward=True):
        term = w_ref[K - 1 - s:K - s, :] * view
        acc = term if acc is None else acc + term
    return acc


def _conv_dw(xwin, dy, K, R):
    taps = [None] * K
    for s, view in _lag_views(xwin, K, R, forward=False):
        taps[K - 1 - s] = jnp.sum(dy * view, axis=0, keepdims=True)
    return taps


def _chunks(T):
    R = _pick(T, (128, 64, 32))
    return R, T // R


def _glu_conv_fwd(name, proj, w, cw_total):
    T = proj.shape[0]
    C = cw_total
    cw = LANE
    nb = C // cw
    R, nch = _chunks(T)

    def body(a_ref, b_ref, w_ref, o_ref, s_ref):
        s_ref[0:HALO, :] = jnp.zeros((HALO, cw), F32)

        def fill(i, _):
            r0 = pl.multiple_of(i * R, R)
            s_ref[pl.ds(HALO + r0, R), :] = a_ref[pl.ds(r0, R), :] * _sigmoid(b_ref[pl.ds(r0, R), :])
            return 0

        lax.fori_loop(0, nch, fill, 0)

        def conv(i, _):
            r0 = pl.multiple_of(i * R, R)
            o_ref[pl.ds(r0, R), :] = _conv_chunk(s_ref[pl.ds(r0, R + HALO), :], w_ref, CONV_K, R)
            return 0

        lax.fori_loop(0, nch, conv, 0)

    return _tc_call(
        body, name=name, grid=(nb,),
        in_specs=[pl.BlockSpec((T, cw), lambda j: (0, j)), pl.BlockSpec((T, cw), lambda j: (0, nb + j)),
                  pl.BlockSpec((HALO, cw), lambda j: (0, j))],
        out_specs=pl.BlockSpec((T, cw), lambda j: (0, j)),
        out_shape=jax.ShapeDtypeStruct((T, C), F32),
        scratch_shapes=[pltpu.VMEM((T + HALO, cw), F32)],
        compiler_params=_params(("parallel",)),
    )(proj, proj, w)


def _glu_conv_bwd(name, proj, w, dhc, cw_total):
    T = proj.shape[0]
    C = cw_total
    cw = LANE
    nb = C // cw
    R, nch = _chunks(T)

    def body(a_ref, b_ref, w_ref, dy_ref, da_ref, db_ref, dw_ref, s_ref, g_ref, acc_ref):
        s_ref[0:HALO, :] = jnp.zeros((HALO, cw), F32)
        g_ref[T:T + HALO, :] = jnp.zeros((HALO, cw), F32)
        acc_ref[...] = jnp.zeros_like(acc_ref)

        def fill(i, _):
            r0 = pl.multiple_of(i * R, R)
            s_ref[pl.ds(HALO + r0, R), :] = a_ref[pl.ds(r0, R), :] * _sigmoid(b_ref[pl.ds(r0, R), :])
            g_ref[pl.ds(r0, R), :] = dy_ref[pl.ds(r0, R), :]
            return 0

        lax.fori_loop(0, nch, fill, 0)

        def back(i, _):
            r0 = pl.multiple_of(i * R, R)
            dhg = _conv_chunk_t(g_ref[pl.ds(r0, R + HALO), :], w_ref, CONV_K, R)
            av = a_ref[pl.ds(r0, R), :]
            sg = _sigmoid(b_ref[pl.ds(r0, R), :])
            da_ref[pl.ds(r0, R), :] = (dhg * sg).astype(da_ref.dtype)
            db_ref[pl.ds(r0, R), :] = (dhg * av * sg * (1.0 - sg)).astype(db_ref.dtype)
            taps = _conv_dw(s_ref[pl.ds(r0, R + HALO), :], dy_ref[pl.ds(r0, R), :], CONV_K, R)
            for k, tap in enumerate(taps):
                acc_ref[k:k + 1, :] += tap
            return 0

        lax.fori_loop(0, nch, back, 0)
        dw_ref[...] = acc_ref[...]

    return _tc_call(
        body, name=name, grid=(nb,),
        in_specs=[pl.BlockSpec((T, cw), lambda j: (0, j)), pl.BlockSpec((T, cw), lambda j: (0, nb + j)),
                  pl.BlockSpec((HALO, cw), lambda j: (0, j)), pl.BlockSpec((T, cw), lambda j: (0, j))],
        out_specs=[pl.BlockSpec((T, cw), lambda j: (0, j)), pl.BlockSpec((T, cw), lambda j: (0, j)),
                   pl.BlockSpec((HALO, cw), lambda j: (0, j))],
        out_shape=[jax.ShapeDtypeStruct((T, C), BF16), jax.ShapeDtypeStruct((T, C), BF16),
                   jax.ShapeDtypeStruct((HALO, C), F32)],
        scratch_shapes=[pltpu.VMEM((T + HALO, cw), F32), pltpu.VMEM((T + HALO, cw), F32),
                        pltpu.VMEM((HALO, cw), F32)],
        compiler_params=_params(("parallel",)),
    )(proj, proj, w, dhc)


def _ffn_conv_fwd(name, up, w, dff):
    T = up.shape[0]
    cw = LANE
    nb = dff // cw
    R, nch = _chunks(T)

    def body(g_ref, v_ref, wg_ref, wv_ref, o_ref, sg_ref, sv_ref):
        sg_ref[0:HALO, :] = jnp.zeros((HALO, cw), F32)
        sv_ref[0:HALO, :] = jnp.zeros((HALO, cw), F32)

        def fill(i, _):
            r0 = pl.multiple_of(i * R, R)
            sg_ref[pl.ds(HALO + r0, R), :] = g_ref[pl.ds(r0, R), :]
            sv_ref[pl.ds(HALO + r0, R), :] = v_ref[pl.ds(r0, R), :]
            return 0

        lax.fori_loop(0, nch, fill, 0)

        def conv(i, _):
            r0 = pl.multiple_of(i * R, R)
            gc = _conv_chunk(sg_ref[pl.ds(r0, R + HALO), :], wg_ref, FFN_K, R)
            vc = _conv_chunk(sv_ref[pl.ds(r0, R + HALO), :], wv_ref, FFN_K, R)
            o_ref[pl.ds(r0, R), :] = (_silu(gc) * vc).astype(o_ref.dtype)
            return 0

        lax.fori_loop(0, nch, conv, 0)

    return _tc_call(
        body, name=name, grid=(nb,),
        in_specs=[pl.BlockSpec((T, cw), lambda j: (0, j)), pl.BlockSpec((T, cw), lambda j: (0, nb + j)),
                  pl.BlockSpec((8, cw), lambda j: (0, j)), pl.BlockSpec((8, cw), lambda j: (0, nb + j))],
        out_specs=pl.BlockSpec((T, cw), lambda j: (0, j)),
        out_shape=jax.ShapeDtypeStruct((T, dff), BF16),
        scratch_shapes=[pltpu.VMEM((T + HALO, cw), F32), pltpu.VMEM((T + HALO, cw), F32)],
        compiler_params=_params(("parallel",)),
    )(up, up, w, w)


def _ffn_conv_bwd(name, up, w, dact, dff):
    T = up.shape[0]
    cw = LANE
    nb = dff // cw
    R, nch = _chunks(T)

    def body(g_ref, v_ref, wg_ref, wv_ref, da_ref, dg_ref, dv_ref, dwg_ref, dwv_ref,
             sg_ref, sv_ref, tg_ref, tv_ref, ag_ref, av_ref):
        zero = jnp.zeros((HALO, cw), F32)
        sg_ref[0:HALO, :] = zero
        sv_ref[0:HALO, :] = zero
        tg_ref[T:T + HALO, :] = zero
        tv_ref[T:T + HALO, :] = zero
        ag_ref[...] = jnp.zeros_like(ag_ref)
        av_ref[...] = jnp.zeros_like(av_ref)

        def fill(i, _):
            r0 = pl.multiple_of(i * R, R)
            sg_ref[pl.ds(HALO + r0, R), :] = g_ref[pl.ds(r0, R), :]
            sv_ref[pl.ds(HALO + r0, R), :] = v_ref[pl.ds(r0, R), :]
            return 0

        lax.fori_loop(0, nch, fill, 0)

        def grads(i, _):
            r0 = pl.multiple_of(i * R, R)
            gwin = sg_ref[pl.ds(r0, R + HALO), :]
            vwin = sv_ref[pl.ds(r0, R + HALO), :]
            gc = _conv_chunk(gwin, wg_ref, FFN_K, R)
            vc = _conv_chunk(vwin, wv_ref, FFN_K, R)
            da = da_ref[pl.ds(r0, R), :].astype(F32)
            sg = _sigmoid(gc)
            dgc = da * vc * (sg * (1.0 + gc * (1.0 - sg)))
            dvc = da * (gc * sg)
            tg_ref[pl.ds(r0, R), :] = dgc
            tv_ref[pl.ds(r0, R), :] = dvc
            for k, tap in enumerate(_conv_dw(gwin, dgc, FFN_K, R)):
                ag_ref[k:k + 1, :] += tap
            for k, tap in enumerate(_conv_dw(vwin, dvc, FFN_K, R)):
                av_ref[k:k + 1, :] += tap
            return 0

        lax.fori_loop(0, nch, grads, 0)

        def back(i, _):
            r0 = pl.multiple_of(i * R, R)
            dg_ref[pl.ds(r0, R), :] = _conv_chunk_t(tg_ref[pl.ds(r0, R + HALO), :], wg_ref, FFN_K, R).astype(dg_ref.dtype)
            dv_ref[pl.ds(r0, R), :] = _conv_chunk_t(tv_ref[pl.ds(r0, R + HALO), :], wv_ref, FFN_K, R).astype(dv_ref.dtype)
            return 0

        lax.fori_loop(0, nch, back, 0)
        dwg_ref[...] = ag_ref[...]
        dwv_ref[...] = av_ref[...]

    col = lambda j: (0, j)
    dg, dv, dwg, dwv = _tc_call(
        body, name=name, grid=(nb,),
        in_specs=[pl.BlockSpec((T, cw), col), pl.BlockSpec((T, cw), lambda j: (0, nb + j)),
                  pl.BlockSpec((8, cw), col), pl.BlockSpec((8, cw), lambda j: (0, nb + j)),
                  pl.BlockSpec((T, cw), col)],
        out_specs=[pl.BlockSpec((T, cw), col), pl.BlockSpec((T, cw), col),
                   pl.BlockSpec((8, cw), col), pl.BlockSpec((8, cw), col)],
        out_shape=[jax.ShapeDtypeStruct((T, dff), BF16), jax.ShapeDtypeStruct((T, dff), BF16),
                   jax.ShapeDtypeStruct((8, dff), F32), jax.ShapeDtypeStruct((8, dff), F32)],
        scratch_shapes=[pltpu.VMEM((T + HALO, cw), F32), pltpu.VMEM((T + HALO, cw), F32),
                        pltpu.VMEM((T + HALO, cw), F32), pltpu.VMEM((T + HALO, cw), F32),
                        pltpu.VMEM((8, cw), F32), pltpu.VMEM((8, cw), F32)],
        compiler_params=_params(("parallel",)),
    )(up, up, w, w, dact)
    return jnp.concatenate([dg, dv], axis=1), jnp.concatenate([dwg, dwv], axis=1)


def _zoh(a_re, a_im, log_dt):
    ar = jnp.minimum(a_re, -1e-4)
    ai = a_im
    dt = jnp.exp(log_dt)
    mag = jnp.exp(dt * ar)
    abar_re = mag * jnp.cos(dt * ai)
    abar_im = mag * jnp.sin(dt * ai)
    den = ar * ar + ai * ai
    nr = abar_re - 1.0
    ni = abar_im
    return abar_re, abar_im, (nr * ar + ni * ai) / den, (ni * ar - nr * ai) / den


def _discretize(a_re, a_im, log_dt, a_re_h, a_im_h, log_dt_h, b_re, b_im):
    abar_re, abar_im, _, _ = _zoh(a_re, a_im, log_dt)
    _, _, z_re, z_im = _zoh(a_re_h, a_im_h, log_dt_h)
    return abar_re, abar_im, z_re * b_re - z_im * b_im, z_re * b_im + z_im * b_re


def _full_specs(arrs):
    return [pl.BlockSpec(a.shape, lambda *_, n=len(a.shape): (0,) * n) for a in arrs]


def _ssm_prep(name, raw):
    def body(*refs):
        res = _discretize(*[r[...] for r in refs[:8]])
        for r, v in zip(refs[8:], res):
            r[...] = v

    outs = [jax.ShapeDtypeStruct(raw[0].shape, F32)] * 2 + [jax.ShapeDtypeStruct(raw[6].shape, F32)] * 2
    return _tc_call(body, name=name, in_specs=_full_specs(raw), out_specs=_full_specs(outs), out_shape=outs)(*raw)


def _ssm_prep_bwd(name, raw, cots):
    G = raw[0].shape[0]
    H = raw[3].shape[0] // G

    def body(*refs):
        _, vjp = jax.vjp(_discretize, *[r[...] for r in refs[:8]])
        g = vjp(tuple(r[...] for r in refs[8:12]))
        outs = refs[12:]
        for k in range(3):
            rep = g[3 + k]
            outs[k][...] = g[k] + jnp.sum(rep.reshape(G, H, rep.shape[1]), axis=1)
        outs[3][...] = g[6]
        outs[4][...] = g[7]

    outs = [jax.ShapeDtypeStruct(a.shape, F32) for a in (raw[0], raw[1], raw[2], raw[6], raw[7])]
    return _tc_call(body, name=name, in_specs=_full_specs(list(raw) + list(cots)), out_specs=_full_specs(outs),
                          out_shape=outs)(*raw, *cots)


def _cmul(ar, ai, br, bi):
    return ar * br - ai * bi, ar * bi + ai * br


def _scan_coefs(ar, ai, reverse):
    W = ar.shape[1]
    row = lax.broadcasted_iota(jnp.int32, (8, W), 0)
    p = [None] * 9
    p[1] = (ar, ai)
    for n in range(2, 9):
        p[n] = _cmul(*p[n // 2], *p[n - n // 2])
    steps = []
    for s in (1, 2, 4):
        valid = (row <= 7 - s) if reverse else (row >= s)
        steps.append((jnp.where(valid, p[s][0], 0.0), jnp.where(valid, p[s][1], 0.0)))
    pr = jnp.zeros((8, W), F32)
    pi = jnp.zeros((8, W), F32)
    for i in range(8):
        n = (8 - i) if reverse else (i + 1)
        pr = jnp.where(row == i, p[n][0], pr)
        pi = jnp.where(row == i, p[n][1], pi)
    return steps, (pr, pi)


def _scan_tile(xr, xi, cr, ci, coefs, reverse):
    steps, (pr, pi) = coefs
    for s, (sr, si) in zip((1, 2, 4), steps):
        shift = (8 - s) if reverse else s
        rr = pltpu.roll(xr, shift, axis=0)
        ri = pltpu.roll(xi, shift, axis=0)
        xr, xi = xr + sr * rr - si * ri, xi + sr * ri + si * rr
    xr, xi = xr + pr * cr - pi * ci, xi + pr * ci + pi * cr
    return xr, xi


def _edge_rows(x, reverse):
    W = x.shape[1]
    return jnp.broadcast_to(x[0:1, :] if reverse else x[7:8, :], (8, W))


def _power_table(ar, ai, pw_r, pw_i, n):
    W = ar.shape[1]
    a8r, a8i = jnp.broadcast_to(ar, (8, W)), jnp.broadcast_to(ai, (8, W))
    pr, pi = a8r, a8i
    for k in range(n):
        pw_r[8 * k:8 * k + 8, :] = pr
        pw_i[8 * k:8 * k + 8, :] = pi
        pr, pi = _cmul(pr, pi, a8r, a8i)


def _segment_order(Q):
    r = jnp.arange(Q)
    return (jnp.arange(Q)[None, :] == ((r % 8) * (Q // 8) + r // 8)[:, None]).astype(BF16)


def _to_segments(u, g, pm):
    C = u.shape[1]
    hi = u.astype(BF16)
    lo = (u - hi.astype(F32)).astype(BF16)
    moved = jnp.dot(pm, jnp.concatenate([hi, lo, g], axis=1), preferred_element_type=F32)
    return moved[:, :C].astype(BF16), moved[:, :C] + moved[:, C:2 * C], moved[:, 2 * C:]


def _segment_scan(xr_ref, xi_ref, row0, seg, ar, ai, pw_r, pw_i, carry_r, carry_i, reverse, visit=None, extra=()):
    W = ar.shape[1]
    sign = -1.0 if reverse else 1.0
    a8r, a8i = jnp.broadcast_to(ar, (8, W)), sign * jnp.broadcast_to(ai, (8, W))
    tile = lambda j: pl.ds(pl.multiple_of(row0 + 8 * j, 8), 8)

    def local(jj, x):
        j = seg - 1 - jj if reverse else jj
        xr = a8r * x[0] - a8i * x[1] + xr_ref[tile(j), :]
        xi = a8r * x[1] + a8i * x[0] + xi_ref[tile(j), :]
        xr_ref[tile(j), :] = xr
        xi_ref[tile(j), :] = xi
        return xr, xi

    zero = jnp.zeros((8, W), F32)
    xr, xi = lax.fori_loop(0, seg, local, (zero, zero))
    row = lax.broadcasted_iota(jnp.int32, (8, W), 0)
    edge = 7 if reverse else 0
    shift = 7 if reverse else 1
    gr = jnp.where(row == edge, carry_r, pltpu.roll(xr, shift, axis=0))
    gi = jnp.where(row == edge, carry_i, pltpu.roll(xi, shift, axis=0))
    top = 8 * (seg - 1)
    alr, ali = pw_r[top:top + 1, :], sign * pw_i[top:top + 1, :]
    er, ei = _scan_tile(gr, gi, zero, zero, _scan_coefs(alr, ali, reverse), reverse)
    far_r = _edge_rows(alr * er - ali * ei + xr, reverse)
    far_i = _edge_rows(alr * ei + ali * er + xi, reverse)

    def fix(j, ex):
        k = pl.ds(pl.multiple_of(8 * (seg - 1 - j if reverse else j), 8), 8)
        pr, pi = pw_r[k, :], sign * pw_i[k, :]
        fr = xr_ref[tile(j), :] + pr * er - pi * ei
        fi = xi_ref[tile(j), :] + pr * ei + pi * er
        xr_ref[tile(j), :] = fr
        xi_ref[tile(j), :] = fi
        return ex if visit is None else visit(j, fr, fi, ex)

    extra = lax.fori_loop(0, seg, fix, extra)
    return er, ei, far_r, far_i, extra


def _ssm_chunk(T):
    return _pick(T, (512, 256, 128, 64))


def _ssm_fwd(name, proj, u_off, p, width):
    T = proj.shape[0]
    NB = width // LANE
    Q = _ssm_chunk(T)
    nch = T // Q
    W = SSM_LANES

    def body(u_ref, bre, bim, cre, cim, ar_ref, ai_ref, d_ref, y_ref, ckr_ref, cki_ref, br_s, bi_s, car_r, car_i):
        c = pl.program_id(1)

        @pl.when(c == 0)
        def _():
            car_r[...] = jnp.zeros_like(car_r)
            car_i[...] = jnp.zeros_like(car_i)

        ckr_ref[...] = car_r[...]
        cki_ref[...] = car_i[...]
        u = u_ref[...]
        u16 = u.astype(BF16)
        br_s[...] = jnp.dot(u16, bre[...], preferred_element_type=F32)
        bi_s[...] = jnp.dot(u16, bim[...], preferred_element_type=F32)
        coefs = _scan_coefs(ar_ref[...], ai_ref[...], False)

        def tile(j, carry):
            r0 = pl.multiple_of(j * 8, 8)
            xr, xi = _scan_tile(br_s[pl.ds(r0, 8), :], bi_s[pl.ds(r0, 8), :], carry[0], carry[1], coefs, False)
            br_s[pl.ds(r0, 8), :] = xr
            bi_s[pl.ds(r0, 8), :] = xi
            return _edge_rows(xr, False), _edge_rows(xi, False)

        cr, ci = lax.fori_loop(0, Q // 8, tile, (car_r[...], car_i[...]))
        car_r[...] = cr
        car_i[...] = ci
        nt = (((1,), (1,)), ((), ()))
        y = (lax.dot_general(br_s[...].astype(BF16), cre[...], nt, preferred_element_type=F32)
             - lax.dot_general(bi_s[...].astype(BF16), cim[...], nt, preferred_element_type=F32)
             + d_ref[...] * u)
        y_ref[...] = _gelu(y).astype(y_ref.dtype)

    blk = lambda b, c: (b, 0, 0)
    mat = pl.BlockSpec((None, LANE, W), blk)
    vec = pl.BlockSpec((None, 1, W), blk)
    ck = pl.BlockSpec((None, None, 8, W), lambda b, c: (b, c, 0, 0))
    return _tc_call(
        body, name=name, grid=(NB, nch),
        in_specs=[pl.BlockSpec((Q, LANE), lambda b, c: (c, u_off + b)), mat, mat, mat, mat, vec, vec,
                  pl.BlockSpec((1, LANE), lambda b, c: (0, b))],
        out_specs=[pl.BlockSpec((Q, LANE), lambda b, c: (c, b)), ck, ck],
        out_shape=[jax.ShapeDtypeStruct((T, width), BF16), jax.ShapeDtypeStruct((NB, nch, 8, W), F32),
                   jax.ShapeDtypeStruct((NB, nch, 8, W), F32)],
        scratch_shapes=[pltpu.VMEM((Q, W), F32), pltpu.VMEM((Q, W), F32), pltpu.VMEM((8, W), F32),
                        pltpu.VMEM((8, W), F32)],
        compiler_params=_params(("parallel", "arbitrary")),
    )(proj, p["bre"], p["bim"], p["cre"], p["cim"], p["ar"], p["ai"], p["d"])


def _ssm_bwd(name, proj, u_off, p, ck_r, ck_i, dyg, width):
    T = proj.shape[0]
    NB = width // LANE
    Q = _ssm_chunk(T)
    nch = T // Q
    W = SSM_LANES
    nt_dims = (((1,), (1,)), ((), ()))
    tn_dims = (((0,), (0,)), ((), ()))

    def body(u_ref, dy_ref, pm_ref, pmt_ref, ckr_ref, cki_ref, bre, bim, cre, cim, ar_ref, ai_ref, d_ref,
             du_ref, dbr_ref, dbi_ref, dcr_ref, dci_ref, dar_ref, dai_ref, dd_ref,
             xr_s, xi_s, lr_s, li_s, lam_r, lam_i, pw_r, pw_i):
        c = pl.program_id(1)
        seg = Q // 8
        ar, ai = ar_ref[...], ai_ref[...]

        @pl.when(c == 0)
        def _():
            lam_r[...] = jnp.zeros_like(lam_r)
            lam_i[...] = jnp.zeros_like(lam_i)
            for r in (dbr_ref, dbi_ref, dcr_ref, dci_ref, dar_ref, dai_ref, dd_ref):
                r[...] = jnp.zeros_like(r)
            _power_table(ar, ai, pw_r, pw_i, seg)

        u16, u, dyg = _to_segments(u_ref[...], dy_ref[...], pm_ref[...])
        xr_s[8:Q + 8, :] = jnp.dot(u16, bre[...], preferred_element_type=F32)
        xi_s[8:Q + 8, :] = jnp.dot(u16, bim[...], preferred_element_type=F32)
        enter_r, enter_i, _, _, _ = _segment_scan(xr_s, xi_s, 8, seg, ar, ai, pw_r, pw_i, ckr_ref[...], cki_ref[...],
                                                  False)
        xr_s[0:8, :] = enter_r
        xi_s[0:8, :] = enter_i
        xr16 = xr_s[8:Q + 8, :].astype(BF16)
        xi16 = xi_s[8:Q + 8, :].astype(BF16)
        y = (lax.dot_general(xr16, cre[...], nt_dims, preferred_element_type=F32)
             - lax.dot_general(xi16, cim[...], nt_dims, preferred_element_type=F32) + d_ref[...] * u)
        _, gelu_vjp = jax.vjp(_gelu, y)
        dy = gelu_vjp(dyg)[0]
        dy16 = dy.astype(BF16)
        dd_ref[...] += jnp.broadcast_to(jnp.sum(dy * u, axis=0, keepdims=True), (8, LANE))
        dcr_ref[...] += lax.dot_general(dy16, xr16, tn_dims, preferred_element_type=F32)
        dci_ref[...] -= lax.dot_general(dy16, xi16, tn_dims, preferred_element_type=F32)
        lr_s[...] = jnp.dot(dy16, cre[...], preferred_element_type=F32)
        li_s[...] = -jnp.dot(dy16, cim[...], preferred_element_type=F32)

        def visit(j, lr, li, sums):
            before = pl.ds(pl.multiple_of(8 * j, 8), 8)
            xpr, xpi = xr_s[before, :], xi_s[before, :]
            return sums[0] + lr * xpr + li * xpi, sums[1] + li * xpr - lr * xpi

        zero = jnp.zeros((8, W), F32)
        _, _, cr, ci, sums = _segment_scan(lr_s, li_s, 0, seg, ar, ai, pw_r, pw_i, lam_r[...], lam_i[...], True, visit,
                                           (zero, zero))
        lam_r[...] = cr
        lam_i[...] = ci
        dar_ref[...] += jnp.broadcast_to(jnp.sum(sums[0], axis=0, keepdims=True), (8, W))
        dai_ref[...] += jnp.broadcast_to(jnp.sum(sums[1], axis=0, keepdims=True), (8, W))
        lr16 = lr_s[...].astype(BF16)
        li16 = li_s[...].astype(BF16)
        dbr_ref[...] += lax.dot_general(u16, lr16, tn_dims, preferred_element_type=F32)
        dbi_ref[...] += lax.dot_general(u16, li16, tn_dims, preferred_element_type=F32)
        du = (lax.dot_general(lr16, bre[...], nt_dims, preferred_element_type=F32)
              + lax.dot_general(li16, bim[...], nt_dims, preferred_element_type=F32) + d_ref[...] * dy)
        du_ref[...] = jnp.dot(pmt_ref[...], du.astype(BF16), preferred_element_type=F32).astype(du_ref.dtype)

    blk = lambda b, c: (b, 0, 0)
    mat = pl.BlockSpec((None, LANE, W), blk)
    vec = pl.BlockSpec((None, 1, W), blk)
    acc8 = pl.BlockSpec((None, 8, W), blk)
    ck = pl.BlockSpec((None, None, 8, W), lambda b, c: (b, nch - 1 - c, 0, 0))
    order = pl.BlockSpec((Q, Q), lambda b, c: (0, 0))
    pm = _segment_order(Q)
    return _tc_call(
        body, name=name, grid=(NB, nch),
        in_specs=[pl.BlockSpec((Q, LANE), lambda b, c: (nch - 1 - c, u_off + b)),
                  pl.BlockSpec((Q, LANE), lambda b, c: (nch - 1 - c, b)), order, order, ck, ck, mat, mat, mat, mat,
                  vec, vec, pl.BlockSpec((1, LANE), lambda b, c: (0, b))],
        out_specs=[pl.BlockSpec((Q, LANE), lambda b, c: (nch - 1 - c, b)), mat, mat, mat, mat, acc8, acc8,
                   pl.BlockSpec((None, 8, LANE), blk)],
        out_shape=[jax.ShapeDtypeStruct((T, width), BF16)] + [jax.ShapeDtypeStruct((NB, LANE, W), F32)] * 4
                  + [jax.ShapeDtypeStruct((NB, 8, W), F32)] * 2 + [jax.ShapeDtypeStruct((NB, 8, LANE), F32)],
        scratch_shapes=[pltpu.VMEM((Q + 8, W), F32), pltpu.VMEM((Q + 8, W), F32), pltpu.VMEM((Q, W), F32),
                        pltpu.VMEM((Q, W), F32), pltpu.VMEM((8, W), F32), pltpu.VMEM((8, W), F32),
                        pltpu.VMEM((Q, W), F32), pltpu.VMEM((Q, W), F32)],
        compiler_params=_params(("parallel", "arbitrary")),
    )(proj, dyg, pm, pm.T, ck_r, ck_i, p["bre"], p["bim"], p["cre"], p["cim"], p["ar"], p["ai"], p["d"])


def _block_diag(w):
    G, H, P = w.shape
    eye = jnp.eye(8, dtype=w.dtype)
    return (w.reshape(G // 8, 8, H, 1, P) * eye[None, :, None, :, None]).reshape(G // 8, 8 * H, 8 * P)


def _block_diag_t(d, H, P):
    NB = d.shape[0]
    d = d.reshape(NB, 8, H, 8, P)
    eye = jnp.eye(8, dtype=d.dtype)
    return jnp.sum(d * eye[None, :, None, :, None], axis=3).reshape(NB * 8, H, P)


def _attn_fwd(name, q, kv, heads):
    T, D = q.shape
    Mm = kv.shape[0]
    hd = D // heads
    tq = _pick(T, (512, 256, 128))
    scale = hd ** -0.5

    def body(q_ref, k_ref, v_ref, o_ref):
        s = lax.dot_general(q_ref[...], k_ref[...], (((1,), (1,)), ((), ())), preferred_element_type=F32) * scale
        s = s - jnp.max(s, axis=-1, keepdims=True)
        e = jnp.exp(s)
        p = e / jnp.sum(e, axis=-1, keepdims=True)
        o_ref[...] = jnp.dot(p.astype(BF16), v_ref[...], preferred_element_type=F32).astype(o_ref.dtype)

    return _tc_call(
        body, name=name, grid=(heads, T // tq),
        in_specs=[pl.BlockSpec((tq, hd), lambda h, i: (i, h)), pl.BlockSpec((Mm, hd), lambda h, i: (0, h)),
                  pl.BlockSpec((Mm, hd), lambda h, i: (0, heads + h))],
        out_specs=pl.BlockSpec((tq, hd), lambda h, i: (i, h)),
        out_shape=jax.ShapeDtypeStruct((T, D), BF16),
        compiler_params=_params(("parallel", "parallel")),
    )(q, kv, kv)


def _attn_bwd(name, q, kv, do, heads):
    T, D = q.shape
    Mm = kv.shape[0]
    hd = D // heads
    tq = _pick(T, (512, 256, 128))
    scale = hd ** -0.5
    nt_dims = (((1,), (1,)), ((), ()))
    tn_dims = (((0,), (0,)), ((), ()))

    def body(q_ref, k_ref, v_ref, do_ref, dq_ref, dk_ref, dv_ref):
        i = pl.program_id(1)

        @pl.when(i == 0)
        def _():
            dk_ref[...] = jnp.zeros_like(dk_ref)
            dv_ref[...] = jnp.zeros_like(dv_ref)

        qv, kvl, vv, dov = q_ref[...], k_ref[...], v_ref[...], do_ref[...]
        s = lax.dot_general(qv, kvl, nt_dims, preferred_element_type=F32) * scale
        s = s - jnp.max(s, axis=-1, keepdims=True)
        e = jnp.exp(s)
        p = e / jnp.sum(e, axis=-1, keepdims=True)
        p16 = p.astype(BF16)
        dv_ref[...] += lax.dot_general(p16, dov, tn_dims, preferred_element_type=F32)
        dp = lax.dot_general(dov, vv, nt_dims, preferred_element_type=F32)
        ds = (p * (dp - jnp.sum(dp * p, axis=-1, keepdims=True)) * scale).astype(BF16)
        dq_ref[...] = jnp.dot(ds, kvl, preferred_element_type=F32).astype(dq_ref.dtype)
        dk_ref[...] += lax.dot_general(ds, qv, tn_dims, preferred_element_type=F32)

    return _tc_call(
        body, name=name, grid=(heads, T // tq),
        in_specs=[pl.BlockSpec((tq, hd), lambda h, i: (i, h)), pl.BlockSpec((Mm, hd), lambda h, i: (0, h)),
                  pl.BlockSpec((Mm, hd), lambda h, i: (0, heads + h)), pl.BlockSpec((tq, hd), lambda h, i: (i, h))],
        out_specs=[pl.BlockSpec((tq, hd), lambda h, i: (i, h)), pl.BlockSpec((Mm, hd), lambda h, i: (0, h)),
                   pl.BlockSpec((Mm, hd), lambda h, i: (0, h))],
        out_shape=[jax.ShapeDtypeStruct((T, D), BF16), jax.ShapeDtypeStruct((Mm, D), F32),
                   jax.ShapeDtypeStruct((Mm, D), F32)],
        compiler_params=_params(("parallel", "arbitrary")),
    )(q, kv, kv, do)


def _position():
    return lax.axis_index("x"), lax.axis_index("y"), lax.axis_index("c")


def _allgather(name, blk, row_mode):
    L = blk.shape[0]
    out_shape = (L, N_DEV) + blk.shape[1:] if row_mode else (N_DEV,) + blk.shape
    x_ref = jax.new_ref(blk, memory_space=pltpu.MemorySpace.HBM)
    out_ref = jax.empty_ref(jax.ShapeDtypeStruct(out_shape, blk.dtype), memory_space=pltpu.MemorySpace.HBM)

    def body(send_sems, recv_sems, local_sem):
        x, y, c = _position()
        me, sibling = (x, y, c), (x, y, 1 - c)
        chips = [(1 - x, y), (x, 1 - y), (1 - x, 1 - y)]
        barrier = pltpu.get_barrier_semaphore()
        for peer in [sibling] + [(*chip, c) for chip in chips]:
            pl.semaphore_signal(barrier, inc=1, device_id=peer, device_id_type=MESH_ID)
        pl.semaphore_wait(barrier, 4)

        def slot(px, py, pc):
            b = 4 * px + 2 * py + pc
            return out_ref.at[:, b] if row_mode else out_ref.at[b]

        def copy(k, block, to, src=None):
            return pltpu.make_async_remote_copy(
                src_ref=slot(*block) if src is None else src, dst_ref=slot(*block),
                send_sem=send_sems.at[k], recv_sem=recv_sems.at[k], device_id=to, device_id_type=MESH_ID)

        mine = pltpu.make_async_copy(x_ref, slot(*me), local_sem)
        mine.start()
        first = [copy(0, me, sibling, src=x_ref)]
        first += [copy(1 + j, me, (*chip, c), src=x_ref) for j, chip in enumerate(chips)]
        for cp in first:
            cp.start()
        passed = [copy(4 + j, (*chip, c), sibling) for j, chip in enumerate(chips)]
        for j, chip in enumerate(chips):
            copy(1 + j, (*chip, c), me).wait_recv()
            passed[j].start()
        copy(0, sibling, me).wait_recv()
        for j, chip in enumerate(chips):
            copy(4 + j, (*chip, 1 - c), me).wait_recv()
        for cp in first + passed:
            cp.wait_send()
        mine.wait()

    pl.kernel(
        body, mesh=plsc.ScalarSubcoreMesh(axis_name="sequencer", num_cores=1), name=name,
        scratch_types=(pltpu.SemaphoreType.DMA((7,)), pltpu.SemaphoreType.DMA((7,)), pltpu.SemaphoreType.DMA),
        compiler_params=pltpu.CompilerParams(collective_id=AG_COLLECTIVE_ID),
    )()
    return out_ref[...]


def _sequencer_kernel(name, body, scratch_types, collective_id):
    pl.kernel(
        body, mesh=plsc.ScalarSubcoreMesh(axis_name="sequencer", num_cores=1), name=name,
        scratch_types=scratch_types, compiler_params=pltpu.CompilerParams(collective_id=collective_id),
    )()


def _handshake(peers):
    barrier = pltpu.get_barrier_semaphore()
    for peer in peers:
        pl.semaphore_signal(barrier, inc=1, device_id=peer, device_id_type=MESH_ID)
    pl.semaphore_wait(barrier, len(peers))


def _rs_sibling(name, grads):
    hbm = pltpu.MemorySpace.HBM
    g_ref = jax.new_ref(grads, memory_space=hbm)
    out_ref = jax.empty_ref(jax.ShapeDtypeStruct((4,) + grads.shape[1:], grads.dtype), memory_space=hbm)

    def body(send_sems, recv_sems):
        x, y, c = _position()
        sibling = (x, y, 1 - c)
        _handshake([sibling])
        copies = [pltpu.make_async_remote_copy(
            src_ref=g_ref.at[2 * q + (1 - c)], dst_ref=out_ref.at[q], send_sem=send_sems.at[q],
            recv_sem=recv_sems.at[q], device_id=sibling, device_id_type=MESH_ID) for q in range(4)]
        for cp in copies:
            cp.start()
        for cp in copies:
            cp.wait_recv()
        for cp in copies:
            cp.wait_send()

    _sequencer_kernel(name, body, (pltpu.SemaphoreType.DMA((4,)), pltpu.SemaphoreType.DMA((4,))),
                      RS_SIBLING_COLLECTIVE_ID)
    return out_ref[...]


def _rs_chips(name, part):
    hbm = pltpu.MemorySpace.HBM
    p_ref = jax.new_ref(part, memory_space=hbm)
    out_ref = jax.empty_ref(jax.ShapeDtypeStruct(part.shape, part.dtype), memory_space=hbm)

    def body(send_sems, recv_sems, local_sem):
        x, y, c = _position()
        flips = [(1 - x, y), (x, 1 - y), (1 - x, 1 - y)]
        _handshake([(fx, fy, c) for fx, fy in flips])
        mine = pltpu.make_async_copy(p_ref.at[2 * x + y], out_ref.at[3], local_sem)
        copies = [pltpu.make_async_remote_copy(
            src_ref=p_ref.at[2 * fx + fy], dst_ref=out_ref.at[k], send_sem=send_sems.at[k], recv_sem=recv_sems.at[k],
            device_id=(fx, fy, c), device_id_type=MESH_ID) for k, (fx, fy) in enumerate(flips)]
        for cp in copies:
            cp.start()
        mine.start()
        for cp in copies:
            cp.wait_recv()
        for cp in copies:
            cp.wait_send()
        mine.wait()

    _sequencer_kernel(name, body, (pltpu.SemaphoreType.DMA((3,)), pltpu.SemaphoreType.DMA((3,)),
                                   pltpu.SemaphoreType.DMA), RS_CHIPS_COLLECTIVE_ID)
    return out_ref[...]


def _adamw_math(g, w, m, v):
    m = ADAM_B1 * m + (1.0 - ADAM_B1) * g
    v = ADAM_B2 * v + (1.0 - ADAM_B2) * (g * g)
    m_hat = m / (1.0 - ADAM_B1 ** ADAM_STEP)
    v_hat = v / (1.0 - ADAM_B2 ** ADAM_STEP)
    delta = -ADAM_LR * (m_hat / (jnp.sqrt(v_hat) + ADAM_EPS) + ADAM_WD * w)
    return delta, m, v


def _adamw_natural(name, gs, ws, ms, vs, lead_block=None):
    n = len(gs)

    def body(*refs):
        ins, outs = refs[:4 * n], refs[4 * n:]
        for k in range(n):
            delta, nm, nv = _adamw_math(*[ins[j * n + k][...] for j in range(4)])
            outs[k][...] = delta
            outs[n + k][...] = nm
            outs[2 * n + k][...] = nv

    out_shape = [jax.ShapeDtypeStruct(w.shape, F32) for w in ws] * 3
    if lead_block is None:
        grid = ()
        spec = lambda a: pl.BlockSpec(a.shape, lambda nd=len(a.shape): (0,) * nd)
    else:
        grid = tuple(d // b for d, b in zip(ws[0].shape[:2], lead_block))
        spec = lambda a: pl.BlockSpec(tuple(lead_block) + tuple(a.shape[2:]),
                                      lambda i, j, nd=len(a.shape): (i, j) + (0,) * (nd - 2))
    return _tc_call(
        body, name=name, grid=grid, in_specs=[spec(a) for a in list(gs) + list(ws) + list(ms) + list(vs)],
        out_specs=[spec(o) for o in out_shape], out_shape=out_shape,
        compiler_params=pltpu.CompilerParams(vmem_limit_bytes=VMEM_LIMIT),
    )(*gs, *ws, *ms, *vs)


def _ew_tile(R, C, nblocks):
    budget = (VMEM_LIMIT * 3) // 4
    return _pick(R, tuple(t for t in (1024, 512, 256, 128, 64, 32, 16, 8) if 8 * t * C * nblocks <= budget))


def _pair_sum(name, grads, landed, c_idx):
    _, r, c = grads.shape
    tile = _ew_tile(r, c, 3)

    def body(c_ref, g_ref, s_ref, o_ref):
        o_ref[...] = (g_ref[...].astype(F32) + s_ref[...].astype(F32)).astype(o_ref.dtype)

    return _tc_call(
        body, name=name,
        grid_spec=pltpu.PrefetchScalarGridSpec(
            num_scalar_prefetch=1, grid=(4, r // tile),
            in_specs=[pl.BlockSpec((None, tile, c), lambda q, i, c_ref: (2 * q + c_ref[0], i, 0)),
                      pl.BlockSpec((None, tile, c), lambda q, i, c_ref: (q, i, 0))],
            out_specs=pl.BlockSpec((None, tile, c), lambda q, i, c_ref: (q, i, 0))),
        out_shape=jax.ShapeDtypeStruct((4, r, c), BF16),
        compiler_params=_params(("parallel", "parallel")),
    )(c_idx, grads, landed)


def _adamw_layer(name, got, w, m, v, layer, prev):
    L, r, c = w.shape
    tile = _ew_tile(r, c, 11)

    def body(g0, g1, g2, g3, w_ref, m_ref, v_ref, *rest):
        outs = rest[-4:]
        g = (g0[...].astype(F32) + g1[...].astype(F32)) + (g2[...].astype(F32) + g3[...].astype(F32))
        delta, nm, nv = _adamw_math(g, w_ref[...], m_ref[...], v_ref[...])
        for ref, val in zip(outs, (g, delta, nm, nv)):
            ref[...] = val

    slab = pl.BlockSpec((None, tile, c), lambda i: (layer, i, 0))
    in_specs = [pl.BlockSpec((None, tile, c), lambda i, k=k: (k, i, 0)) for k in range(4)] + [slab] * 3
    args = [got, got, got, got, w, m, v]
    aliases = {}
    if prev is not None:
        in_specs += [ANY] * 4
        args += list(prev)
        aliases = {7 + k: k for k in range(4)}
    return _tc_call(
        body, name=name, grid=(r // tile,), in_specs=in_specs, out_specs=[slab] * 4,
        out_shape=[jax.ShapeDtypeStruct((L, r, c), F32)] * 4, input_output_aliases=aliases,
        compiler_params=_params(("parallel",)),
    )(*args)


def _touch(name, x):
    rows = min(x.shape[1], 16)

    def body(x_ref, o_ref):
        o_ref[...] = jnp.broadcast_to(x_ref[0:1, :].astype(F32), o_ref.shape)

    return _tc_call(body, name=name, grid=(1,), in_specs=[pl.BlockSpec((None, rows, LANE), lambda i: (0, 0, 0))],
                    out_specs=pl.BlockSpec((8, LANE), lambda i: (0, 0)),
                    out_shape=jax.ShapeDtypeStruct((8, LANE), F32))(x)


def _reduce_pipeline(name, layer, grads, w, m, v, state, hold, gate):
    c_idx = lax.axis_index("c").astype(jnp.int32).reshape(1)
    landed = _rs_sibling(name + "_rs_sibling", grads)
    yield
    part = _pair_sum(name + "_pair_sum", grads, landed, c_idx)
    got = _rs_chips(name + "_rs_chips", part)
    for _ in range(4):
        yield
    if hold:
        _touch(name + "_touch", got)
        while not gate:
            yield
    state[name] = _adamw_layer(name + "_adamw", got, w, m, v, layer, state.get(name))


def _loss_head(name, x, g, target):
    T, D = x.shape

    def fn(xv, tv, gv):
        def f(xx, gg):
            err = _rms(xx, gg) - tv
            return 0.5 * jnp.sum(jnp.mean(err * err, axis=-1, keepdims=True))

        loss, (dx, dg) = jax.value_and_grad(f, argnums=(0, 1))(xv, gv)
        return dx, dx, jnp.full((8, LANE), loss, F32), jnp.sum(dg, axis=0, keepdims=True)

    dx, dx16, loss, dg = _rowwise(name, fn, [(x, D, 0, 0), (target, D, 0, 0)], [g], [(D, F32), (D, BF16)],
                                  [(8, LANE), (1, D)], tile=128)
    return loss[0, 0], dx, dx16, dg


_SHARDED_COL = ("w_in", "conv_dw_w", "conv_w_pw", "ssm_w_glu", "xa_w_kv", "ffn_w_up", "ffn_dw_w")
_SHARDED_ROW = ("w_out", "xa_w_q", "xa_w_o", "ffn_w_down")
_WEIGHTS = ['mix_norm_g', 'w_in', 'conv_dw_w', 'conv_dw_b', 'conv_ln_g', 'conv_ln_b', 'conv_w_pw', 'ssm_a_re',
            'ssm_a_im', 'ssm_log_dt', 'ssm_b_re', 'ssm_b_im', 'ssm_c_re', 'ssm_c_im', 'ssm_d', 'ssm_w_glu', 'w_out',
            'xa_norm_g', 'mem_norm_g', 'xa_w_q', 'xa_w_kv', 'xa_w_o', 'ffn_norm_g', 'ffn_w_up', 'ffn_dw_w',
            'ffn_w_down', 'final_norm_g']
_FWD = ['x', 'mem'] + _WEIGHTS
_AG_ORDER = ("w_in", "conv_dw_w", "conv_w_pw", "ssm_w_glu", "w_out", "xa_w_q", "xa_w_kv", "xa_w_o", "ffn_w_up",
             "ffn_dw_w", "ffn_w_down")


def _pad_rows(a, rows):
    return jnp.pad(a, ((0, 0), (0, rows - a.shape[1]), (0, 0)))


def _step(inp, target, mom_m, mom_v):
    _LAST_CALL.clear()
    x0 = inp["x"][0]
    mem = inp["mem"][0]
    T, D = x0.shape
    L = inp["w_in"].shape[0]
    CW = inp["conv_dw_b"].shape[1]
    SW = inp["ssm_d"].shape[1]
    DFF = inp["ffn_w_down"].shape[1] * N_DEV
    G = SW // SSM_GROUP
    NB = SW // LANE
    u_off = (2 * CW) // LANE
    gate_off = (2 * CW + SW) // 1024

    gathered = {n: [None] * L for n in _AG_ORDER}
    filter_rows = {"conv_dw_w": HALO, "ffn_dw_w": 8}
    for l in range(L):
        for n in _AG_ORDER:
            blk = inp[n][l:l + 1]
            blk = _pad_rows(blk, filter_rows[n]) if n in filter_rows else blk.astype(BF16)
            if n in _SHARDED_COL:
                gathered[n][l] = _allgather("ag_" + n, blk, False)
            else:
                full = _allgather("ag_" + n, blk, True)
                gathered[n][l] = full.reshape(1, N_DEV * full.shape[2], full.shape[3])

    def W(n, l):
        return _W(gathered[n][l], 0, n in _SHARDED_COL)

    def dw_filter(n, l):
        g = gathered[n][l]
        return jnp.transpose(g[:, 0], (1, 0, 2)).reshape(g.shape[2], N_DEV * g.shape[3])

    def row(n, l):
        return (inp[n], l)

    ssm_raw, ssm_p = [], []
    for l in range(L):
        a_re, a_im, ldt = inp["ssm_a_re"][l], inp["ssm_a_im"][l], inp["ssm_log_dt"][l][:, None]
        rep = lambda a: jnp.repeat(a, SSM_GROUP, axis=0)
        flat = lambda b: jnp.transpose(b, (0, 2, 1)).reshape(G * SSM_GROUP, SSM_STATE)
        raw = (a_re, a_im, ldt, rep(a_re), rep(a_im), rep(ldt), flat(inp["ssm_b_re"][l]), flat(inp["ssm_b_im"][l]))
        abar_re, abar_im, bbar_re, bbar_im = _ssm_prep("ssm_prep", raw)
        bbar_re = bbar_re.reshape(G, SSM_GROUP, SSM_STATE)
        bbar_im = bbar_im.reshape(G, SSM_GROUP, SSM_STATE)
        ssm_raw.append(raw)
        ssm_p.append(dict(
            bre=_block_diag(bbar_re).astype(BF16), bim=_block_diag(bbar_im).astype(BF16),
            cre=_block_diag(inp["ssm_c_re"][l]).astype(BF16), cim=_block_diag(inp["ssm_c_im"][l]).astype(BF16),
            ar=abar_re.reshape(NB, 1, SSM_LANES), ai=abar_im.reshape(NB, 1, SSM_LANES), d=inp["ssm_d"][l][None, :]))

    saved = []
    x = x0
    for l in range(L):
        s = {"x_in": x}
        s["h1"] = _rms_fwd("rms_mix", x, row("mix_norm_g", l))
        s["proj"] = _mm_nn("mm_w_in", s["h1"], W("w_in", l), F32)
        s["hc"] = _glu_conv_fwd("conv_fwd", s["proj"], dw_filter("conv_dw_w", l), CW)
        s["hs"] = _rowwise("conv_post", _convpost, [(s["hc"], CW, 0, 0)],
                           [row("conv_dw_b", l), row("conv_ln_g", l), row("conv_ln_b", l)], [(CW, BF16)])[0]
        s["ya"] = _mm_nn("mm_w_pw", s["hs"], W("conv_w_pw", l), F32)
        s["yg"], s["ck_r"], s["ck_i"] = _ssm_fwd("ssm_fwd", s["proj"], u_off, ssm_p[l], SW)
        s["gg"] = _mm_nn("mm_w_glu", s["yg"], W("ssm_w_glu", l), F32)
        nmix = D // 1024
        mix_rows = [(s["proj"], 1024, gate_off, 0), (s["proj"], 1024, gate_off + nmix, 0), (s["ya"], 1024, 0, 0),
                    (s["gg"], 1024, 0, 0), (s["gg"], 1024, nmix, 0)]
        s["mix_rows"] = mix_rows
        s["mix"] = _rowwise("mix_fwd", _mixf, mix_rows, [], [(D, BF16)], ncol=nmix)[0]
        x = _mm_nn("mm_w_out", s["mix"], W("w_out", l), F32, add=x)
        s["x1"] = x
        s["h2"] = _rms_fwd("rms_xa", x, row("xa_norm_g", l))
        s["q"] = _mm_nn("mm_w_q", s["h2"], W("xa_w_q", l), BF16)
        s["mn"] = _rms_fwd("rms_mem", mem, row("mem_norm_g", l))
        s["kv"] = _mm_nn("mm_w_kv", s["mn"], W("xa_w_kv", l), BF16)
        s["o"] = _attn_fwd("attn_fwd", s["q"], s["kv"], XA_HEADS)
        x = _mm_nn("mm_w_o", s["o"], W("xa_w_o", l), F32, add=x)
        s["x2"] = x
        s["h3"] = _rms_fwd("rms_ffn", x, row("ffn_norm_g", l))
        s["up"] = _mm_nn("mm_w_up", s["h3"], W("ffn_w_up", l), F32)
        s["act"] = _ffn_conv_fwd("ffn_conv_fwd", s["up"], dw_filter("ffn_dw_w", l), DFF)
        x = _mm_nn("mm_w_down", s["act"], W("ffn_w_down", l), F32, add=x)
        saved.append(s)

    loss_part, dx, dx16, d_final_g = _loss_head("loss_head", x, inp["final_norm_g"][None, :], target[0])

    big = _SHARDED_COL + _SHARDED_ROW
    small = {n: [None] * L for n in _WEIGHTS if n not in big and n != "final_norm_g"}
    pads = {"conv_dw_w": HALO, "ffn_dw_w": 8}
    shards = {n: tuple(_pad_rows(a, pads[n]) if n in pads else a for a in (inp[n], mom_m[n], mom_v[n])) for n in big}
    state, queue, gates, hold = {}, [], [], [True]
    names = [n for n in _WEIGHTS if n not in big]
    sizes = [inp[n].size for n in names]
    total = sum(sizes)
    pack_w = 8 * LANE
    rows_p = -(-total // (LANE * pack_w)) * LANE
    padn = rows_p * pack_w - total

    def pack(parts, fill):
        return jnp.concatenate([p.reshape(-1) for p in parts] + [jnp.full((padn,), fill, F32)]).reshape(rows_p, pack_w)

    def total(*parts):
        tot = parts[0]
        for part in parts[1:]:
            tot = tot + part
        return tot

    def tick(newest_first=False):
        for gen in (list(reversed(queue)) if newest_first else list(queue)):
            if next(gen, "done") == "done":
                queue.remove(gen)

    def emit(n, l, g):
        gates.append([])
        queue.append(_reduce_pipeline(n, l, g.astype(BF16), *shards[n], state, hold, gates[-1]))
        tick()

    deferred = []

    def emit_small(n, l, thunk):
        if l == 0:
            deferred.append((n, thunk))
        else:
            emit(n, l, thunk())

    for l in reversed(range(L)):
        s = saved[l]
        dact = _mm_nt("mm_w_down_t", dx16, W("ffn_w_down", l), BF16)
        emit("ffn_w_down", l, _mm_tn("mm_dw_down", s["act"], dx16).reshape(N_DEV, DFF // N_DEV, D))
        d_up, d_ffn_dw = _ffn_conv_bwd("ffn_conv_bwd", s["up"], dw_filter("ffn_dw_w", l), dact, DFF)
        emit("ffn_dw_w", l, jnp.transpose(d_ffn_dw.reshape(8, N_DEV, 2 * DFF // N_DEV), (1, 0, 2)))
        dh3 = _mm_nt("mm_w_up_t", d_up, W("ffn_w_up", l), BF16)
        emit("ffn_w_up", l, _mm_tn("mm_dw_up", s["h3"], d_up, nb=2 * DFF // N_DEV))
        dx, dx16, small["ffn_norm_g"][l] = _rms_bwd("rms_ffn_bwd", s["x2"], row("ffn_norm_g", l), dh3, dx)
        do = _mm_nt("mm_w_o_t", dx16, W("xa_w_o", l), BF16)
        emit_small("xa_w_o", l, lambda a=s["o"], b=dx16: _mm_tn("mm_dw_o", a, b).reshape(N_DEV, D // N_DEV, D))
        dq, dk, dv = _attn_bwd("attn_bwd", s["q"], s["kv"], do, XA_HEADS)
        dkv = jnp.concatenate([dk, dv], axis=1).astype(BF16)
        dh2 = _mm_nt("mm_w_q_t", dq, W("xa_w_q", l), BF16)
        emit_small("xa_w_q", l, lambda a=s["h2"], b=dq: _mm_tn("mm_dw_q", a, b).reshape(N_DEV, D // N_DEV, D))
        dmn = _mm_nt("mm_w_kv_t", dkv, W("xa_w_kv", l), BF16)
        emit("xa_w_kv", l, _mm_tn("mm_dw_kv", s["mn"], dkv, nb=2 * D // N_DEV))

        def mem_bwd(mv, dv_, gv):
            _, vjp = jax.vjp(_rms, mv, gv)
            return jnp.sum(vjp(dv_.astype(F32))[1], axis=0, keepdims=True)

        small["mem_norm_g"][l] = _rowwise("rms_mem_bwd", mem_bwd, [(mem, D, 0, 0), (dmn, D, 0, 0)],
                                          [row("mem_norm_g", l)], [], [(1, D)], tile=128)[0]
        dx, dx16, small["xa_norm_g"][l] = _rms_bwd("rms_xa_bwd", s["x1"], row("xa_norm_g", l), dh2, dx)
        dmix = _mm_nt("mm_w_out_t", dx16, W("w_out", l), BF16)
        emit_small("w_out", l, lambda a=s["mix"], b=dx16: _mm_tn("mm_dw_out", a, b).reshape(N_DEV, D // N_DEV, D))

        def mix_bwd(gla, glb, ya, ga, gb, dm):
            _, vjp = jax.vjp(_mixf, gla, glb, ya, ga, gb)
            return vjp(dm.astype(F32))

        nmix = D // 1024
        dgla, dglb, dya, dga, dgb = _rowwise("mix_bwd", mix_bwd, s["mix_rows"] + [(dmix, 1024, 0, 0)], [],
                                             [(D, BF16)] * 5, ncol=nmix)
        dgg = jnp.concatenate([dga, dgb], axis=1)
        dyg = _mm_nt("mm_w_glu_t", dgg, W("ssm_w_glu", l), BF16)
        emit_small("ssm_w_glu", l, lambda a=s["yg"], b=dgg: _mm_tn("mm_dw_glu", a, b, nb=2 * D // N_DEV))
        du, dbr, dbi, dcr, dci, dar, dai, dd = _ssm_bwd("ssm_bwd", s["proj"], u_off, ssm_p[l], s["ck_r"], s["ck_i"],
                                                        dyg, SW)
        cots = (dar[:, 0, :].reshape(G, SSM_STATE), dai[:, 0, :].reshape(G, SSM_STATE),
                _block_diag_t(dbr, SSM_GROUP, SSM_STATE).reshape(G * SSM_GROUP, SSM_STATE),
                _block_diag_t(dbi, SSM_GROUP, SSM_STATE).reshape(G * SSM_GROUP, SSM_STATE))
        g_are, g_aim, g_ldt, g_bre, g_bim = _ssm_prep_bwd("ssm_prep_bwd", ssm_raw[l], cots)
        small["ssm_a_re"][l], small["ssm_a_im"][l], small["ssm_log_dt"][l] = g_are, g_aim, g_ldt[:, 0]
        small["ssm_b_re"][l] = jnp.transpose(g_bre.reshape(G, SSM_GROUP, SSM_STATE), (0, 2, 1))
        small["ssm_b_im"][l] = jnp.transpose(g_bim.reshape(G, SSM_GROUP, SSM_STATE), (0, 2, 1))
        small["ssm_c_re"][l] = _block_diag_t(dcr, SSM_GROUP, SSM_STATE)
        small["ssm_c_im"][l] = _block_diag_t(dci, SSM_GROUP, SSM_STATE)
        small["ssm_d"][l] = dd[:, 0, :].reshape(SW)
        dhs = _mm_nt("mm_w_pw_t", dya, W("conv_w_pw", l), BF16)
        emit("conv_w_pw", l, _mm_tn("mm_dw_pw", s["hs"], dya, nb=D // N_DEV))

        def post_bwd(hc, dh, b, lg, lb):
            _, vjp = jax.vjp(_convpost, hc, b, lg, lb)
            dhc, db, dlg, dlb = vjp(dh.astype(F32))
            return dhc, jnp.sum(db, axis=0, keepdims=True), jnp.sum(dlg, axis=0, keepdims=True), \
                jnp.sum(dlb, axis=0, keepdims=True)

        dhc, small["conv_dw_b"][l], small["conv_ln_g"][l], small["conv_ln_b"][l] = _rowwise(
            "conv_post_bwd", post_bwd, [(s["hc"], CW, 0, 0), (dhs, CW, 0, 0)],
            [row("conv_dw_b", l), row("conv_ln_g", l), row("conv_ln_b", l)], [(CW, F32)], [(1, CW)] * 3)
        if l == 0:
            early = {**small, "mix_norm_g": [jnp.zeros((1, D), F32)] + small["mix_norm_g"][1:]}
            flat_g = jnp.concatenate([(jnp.stack(early[n]) if n != "final_norm_g" else d_final_g).reshape(-1)
                                      for n in names])
            g_all = _allgather("ag_small_grads", pack([flat_g], 0.0)[None], False)
        da, db, d_conv_dw = _glu_conv_bwd("conv_bwd", s["proj"], dw_filter("conv_dw_w", l), dhc, CW)
        emit("conv_dw_w", l, jnp.transpose(d_conv_dw.reshape(HALO, N_DEV, CW // N_DEV), (1, 0, 2)))
        dproj = jnp.concatenate([da, db, du, dgla, dglb], axis=1)
        dh1 = _mm_nt("mm_w_in_t", dproj, W("w_in", l), BF16)
        dx, dx16, small["mix_norm_g"][l] = _rms_bwd("rms_mix_bwd", s["x_in"], row("mix_norm_g", l), dh1, dx)
        if l == 0:
            assert names[0] == "mix_norm_g" and D % pack_w == 0 and D // pack_w <= 8
            late = jnp.pad(small["mix_norm_g"][0].reshape(D // pack_w, pack_w), ((0, 8 - D // pack_w), (0, 0)))
            g_late = _allgather("ag_mix_grad", late[None], False)
            g_sum = _rowwise("small_grad_sum", total, [(g_all.reshape(N_DEV * rows_p, pack_w), pack_w, 0, k * rows_p)
                                                       for k in range(N_DEV)], [], [(pack_w, F32)], tile=LANE,
                             nrows=rows_p)[0]
        emit("w_in", l, _mm_tn("mm_dw_in", s["h1"], dproj, nb=dproj.shape[1] // N_DEV))
    for k, (n, thunk) in enumerate(deferred):
        emit(n, 0, thunk())
        if k == 0:
            head = _rowwise("mix_grad_sum", total,
                            [(g_sum, pack_w, 0, 0)] + [(g_late.reshape(N_DEV * 8, pack_w), pack_w, 0, 8 * d)
                                                       for d in range(N_DEV)], [], [(pack_w, F32)], tile=8, nrows=8)[0]

    hold.clear()
    per_round = -(-len(gates) // 6)
    while queue:
        for gate in gates[:per_round]:
            gate.append(True)
        del gates[:per_round]
        tick(newest_first=True)
    results = {n: [o[:, :inp[n].shape[1], :] for o in state[n]] if n in pads else state[n] for n in big}
    g_sum = jnp.concatenate([head, g_sum[8:]], axis=0)
    g_flat, grads, offs = g_sum.reshape(-1), {}, 0
    for n, sz in zip(names, sizes):
        grads[n] = g_flat[offs:offs + sz].reshape(inp[n].shape)
        offs += sz
    sparse = [n for n in names if inp[n].ndim == 4 and inp[n].shape[-1] < SSM_STATE]
    dense = [n for n in names if n not in sparse]
    as2d = lambda a: a[None, :] if a.ndim == 1 else a
    for group, block in ((dense, None), (sparse, (1, 16))):
        outs = _adamw_natural("small_adamw", *[[as2d(src[n]) for n in group] for src in (grads, inp, mom_m, mom_v)],
                              lead_block=block)
        for k, n in enumerate(group):
            results[n] = [grads[n]] + [outs[j * len(group) + k].reshape(inp[n].shape) for j in range(3)]

    _LAST_CALL.clear()
    loss = lax.psum(loss_part, ("x", "y", "c"))
    grad_x = dx[None]
    return (loss, grad_x, *[results[n][0] for n in _WEIGHTS], *[results[n][1] for n in _WEIGHTS],
            *[results[n][2] for n in _WEIGHTS], *[results[n][3] for n in _WEIGHTS])


def kernel(x, mem, mix_norm_g, w_in, conv_dw_w, conv_dw_b, conv_ln_g, conv_ln_b, conv_w_pw, ssm_a_re, ssm_a_im, ssm_log_dt, ssm_b_re, ssm_b_im, ssm_c_re, ssm_c_im, ssm_d, ssm_w_glu, w_out, xa_norm_g, mem_norm_g, xa_w_q, xa_w_kv, xa_w_o, ffn_norm_g, ffn_w_up, ffn_dw_w, ffn_w_down, final_norm_g, loss_target, m_mix_norm_g, m_w_in, m_conv_dw_w, m_conv_dw_b, m_conv_ln_g, m_conv_ln_b, m_conv_w_pw, m_ssm_a_re, m_ssm_a_im, m_ssm_log_dt, m_ssm_b_re, m_ssm_b_im, m_ssm_c_re, m_ssm_c_im, m_ssm_d, m_ssm_w_glu, m_w_out, m_xa_norm_g, m_mem_norm_g, m_xa_w_q, m_xa_w_kv, m_xa_w_o, m_ffn_norm_g, m_ffn_w_up, m_ffn_dw_w, m_ffn_w_down, m_final_norm_g, v_mix_norm_g, v_w_in, v_conv_dw_w, v_conv_dw_b, v_conv_ln_g, v_conv_ln_b, v_conv_w_pw, v_ssm_a_re, v_ssm_a_im, v_ssm_log_dt, v_ssm_b_re, v_ssm_b_im, v_ssm_c_re, v_ssm_c_im, v_ssm_d, v_ssm_w_glu, v_w_out, v_xa_norm_g, v_mem_norm_g, v_xa_w_q, v_xa_w_kv, v_xa_w_o, v_ffn_norm_g, v_ffn_w_up, v_ffn_dw_w, v_ffn_w_down, v_final_norm_g):
    args = (x, mem, mix_norm_g, w_in, conv_dw_w, conv_dw_b, conv_ln_g, conv_ln_b, conv_w_pw, ssm_a_re, ssm_a_im, ssm_log_dt, ssm_b_re, ssm_b_im, ssm_c_re, ssm_c_im, ssm_d, ssm_w_glu, w_out, xa_norm_g, mem_norm_g, xa_w_q, xa_w_kv, xa_w_o, ffn_norm_g, ffn_w_up, ffn_dw_w, ffn_w_down, final_norm_g)
    ms = (m_mix_norm_g, m_w_in, m_conv_dw_w, m_conv_dw_b, m_conv_ln_g, m_conv_ln_b, m_conv_w_pw, m_ssm_a_re, m_ssm_a_im, m_ssm_log_dt, m_ssm_b_re, m_ssm_b_im, m_ssm_c_re, m_ssm_c_im, m_ssm_d, m_ssm_w_glu, m_w_out, m_xa_norm_g, m_mem_norm_g, m_xa_w_q, m_xa_w_kv, m_xa_w_o, m_ffn_norm_g, m_ffn_w_up, m_ffn_dw_w, m_ffn_w_down, m_final_norm_g)
    vs = (v_mix_norm_g, v_w_in, v_conv_dw_w, v_conv_dw_b, v_conv_ln_g, v_conv_ln_b, v_conv_w_pw, v_ssm_a_re, v_ssm_a_im, v_ssm_log_dt, v_ssm_b_re, v_ssm_b_im, v_ssm_c_re, v_ssm_c_im, v_ssm_d, v_ssm_w_glu, v_w_out, v_xa_norm_g, v_mem_norm_g, v_xa_w_q, v_xa_w_kv, v_xa_w_o, v_ffn_norm_g, v_ffn_w_up, v_ffn_dw_w, v_ffn_w_down, v_final_norm_g)
    return _step(dict(zip(_FWD, args)), loss_target, dict(zip(_WEIGHTS, ms)), dict(zip(_WEIGHTS, vs)))
```

```python
import functools

import jax
import jax.numpy as jnp
from jax import lax
from jax.experimental import pallas as pl
from jax.experimental.pallas import tpu as pltpu
from jax.experimental.pallas import tpu_sc as plsc

F32 = jnp.float32
BF16 = jnp.bfloat16
MESH_ID = pl.DeviceIdType.MESH
N_DEV = 8
EPS = 1e-6
VMEM_LIMIT = 48 * 1024 * 1024
ANY = pl.BlockSpec(memory_space=pl.ANY)

ADAM_LR = 0.001
ADAM_B1 = 0.9
ADAM_B2 = 0.999
ADAM_EPS = 1e-08
ADAM_WD = 0.01
ADAM_STEP = 10

CONV_K = 31
FFN_K = 3
XA_HEADS = 4
SSM_GROUP = 16
SSM_STATE = 64
HALO = 32
LANE = 128
SSM_LANES = 512
AG_COLLECTIVE_ID = 1
RS_SIBLING_COLLECTIVE_ID = 2
RS_CHIPS_COLLECTIVE_ID = 3


def _pick(n, prefs):
    for p in prefs:
        if p <= n and n % p == 0:
            return p
    return n


def _params(sem, vmem=VMEM_LIMIT):
    return pltpu.CompilerParams(dimension_semantics=sem, vmem_limit_bytes=vmem)


_LAST_CALL = []


def _tc_call(*call_args, **call_kwargs):
    call = pl.pallas_call(*call_args, **call_kwargs)

    def run(*args):
        args = list(args)
        if _LAST_CALL:
            i = next(k for k, a in enumerate(args) if a.ndim >= 2)
            args[i] = lax.optimization_barrier((args[i], _LAST_CALL[0]))[0]
        out = call(*args)
        _LAST_CALL[:] = [out[0] if isinstance(out, (tuple, list)) else out]
        return out

    return run


def _sigmoid(x):
    return 1.0 / (1.0 + jnp.exp(-x))


def _silu(x):
    return x * _sigmoid(x)


def _gelu(x):
    return 0.5 * x * (1.0 + jnp.tanh(0.7978845608028654 * (x + 0.044715 * (x * x * x))))


def _rms(x, g):
    return x * lax.rsqrt(jnp.mean(x * x, axis=-1, keepdims=True) + EPS) * g


def _convpost(hc, bias, ln_g, ln_b):
    h = hc + bias
    mu = jnp.mean(h, axis=-1, keepdims=True)
    xc = h - mu
    y = xc * lax.rsqrt(jnp.mean(xc * xc, axis=-1, keepdims=True) + EPS)
    return _silu(y * ln_g + ln_b)


def _mixf(gla, glb, ya, ga, gb):
    return _sigmoid(gla) * ya + _sigmoid(glb) * (ga * _sigmoid(gb))


class _W:
    def __init__(self, arr, layer, blocked):
        self.arr, self.layer, self.blocked = arr, layer, blocked
        if blocked:
            _, _, self.K, self.nb = arr.shape
            self.N = N_DEV * self.nb
        else:
            _, self.K, self.N = arr.shape
            self.nb = self.N

    def spec(self, tk, tn, ki, ni):
        l = self.layer
        if self.blocked:
            per = self.nb // tn
            return pl.BlockSpec((None, None, tk, tn), lambda *g: (ni(*g) // per, l, ki(*g), ni(*g) % per))
        return pl.BlockSpec((None, tk, tn), lambda *g: (l, ki(*g), ni(*g)))


_M_TILES = (1024, 512, 256, 128, 64, 32, 16, 8)
_N_TILES = (1408, 1024, 896, 512, 256, 128)
_K_TILES = (512, 1408, 896, 256, 128)
MAX_FULL_K = 2048
_K_FULL_TILES = (2048, 1408, 1024, 896, 512, 256, 128)
MAX_FULL_T = 4096


def _mm_nn(name, a, w, out_dtype, add=None):
    M, K = a.shape
    assert K == w.K
    tm, tn = _pick(M, _M_TILES), _pick(w.nb, _N_TILES)
    tk = K if K <= MAX_FULL_K else _pick(K, _K_FULL_TILES)
    nk = K // tk

    def body(*refs):
        a_ref, w_ref = refs[:2]
        r_ref = refs[2] if add is not None else None
        o_ref = refs[3] if add is not None else refs[2]
        part = jnp.dot(a_ref[...].astype(BF16), w_ref[...], preferred_element_type=F32)
        if nk == 1:
            o_ref[...] = (part if add is None else part + r_ref[...]).astype(o_ref.dtype)
            return
        acc = refs[-1]
        k = pl.program_id(2)

        @pl.when(k == 0)
        def _():
            acc[...] = part

        @pl.when(k > 0)
        def _():
            acc[...] += part

        @pl.when(k == nk - 1)
        def _():
            res = acc[...]
            if add is not None:
                res = res + r_ref[...]
            o_ref[...] = res.astype(o_ref.dtype)

    in_specs = [pl.BlockSpec((tm, tk), lambda i, j, k: (i, k)),
                w.spec(tk, tn, lambda i, j, k: k, lambda i, j, k: j)]
    args = [a, w.arr]
    if add is not None:
        in_specs.append(pl.BlockSpec((tm, tn), lambda i, j, k: (i, j)))
        args.append(add)
    return _tc_call(
        body, name=name, grid=(M // tm, w.N // tn, nk), in_specs=in_specs,
        out_specs=pl.BlockSpec((tm, tn), lambda i, j, k: (i, j)),
        out_shape=jax.ShapeDtypeStruct((M, w.N), out_dtype),
        scratch_shapes=[] if nk == 1 else [pltpu.VMEM((tm, tn), F32)],
        compiler_params=_params(("parallel", "parallel", "arbitrary")),
    )(*args)


def _mm_nt(name, a, w, out_dtype):
    M, N = a.shape
    assert N == w.N
    tm, tkk = _pick(M, _M_TILES), _pick(w.K, _N_TILES)
    tnn = w.nb if w.nb <= MAX_FULL_K else _pick(w.nb, _K_FULL_TILES)
    nn = N // tnn

    def body(a_ref, w_ref, o_ref, *scratch):
        part = lax.dot_general(a_ref[...].astype(BF16), w_ref[...], (((1,), (1,)), ((), ())),
                               preferred_element_type=F32)
        if nn == 1:
            o_ref[...] = part.astype(o_ref.dtype)
            return
        acc = scratch[0]
        n = pl.program_id(2)

        @pl.when(n == 0)
        def _():
            acc[...] = part

        @pl.when(n > 0)
        def _():
            acc[...] += part

        @pl.when(n == nn - 1)
        def _():
            o_ref[...] = acc[...].astype(o_ref.dtype)

    return _tc_call(
        body, name=name, grid=(M // tm, w.K // tkk, nn),
        in_specs=[pl.BlockSpec((tm, tnn), lambda i, j, n: (i, n)),
                  w.spec(tkk, tnn, lambda i, j, n: j, lambda i, j, n: n)],
        out_specs=pl.BlockSpec((tm, tkk), lambda i, j, n: (i, j)),
        out_shape=jax.ShapeDtypeStruct((M, w.K), out_dtype),
        scratch_shapes=[] if nn == 1 else [pltpu.VMEM((tm, tkk), F32)],
        compiler_params=_params(("parallel", "parallel", "arbitrary")),
    )(a, w.arr)


def _mm_tn(name, a, b, nb=None):
    T, K = a.shape
    _, N = b.shape
    width = N if nb is None else nb
    assert T <= MAX_FULL_T
    tkk, tn = _pick(K, (512, 256, 128)), _pick(width, _N_TILES)
    per = width // tn

    def body(a_ref, b_ref, o_ref):
        o_ref[...] = lax.dot_general(a_ref[...].astype(BF16), b_ref[...].astype(BF16), (((0,), (0,)), ((), ())),
                                     preferred_element_type=F32).astype(o_ref.dtype)

    if nb is None:
        out_spec = pl.BlockSpec((tkk, tn), lambda j, i: (i, j))
        out_shape = jax.ShapeDtypeStruct((K, N), BF16)
    else:
        out_spec = pl.BlockSpec((None, tkk, tn), lambda j, i: (j // per, i, j % per))
        out_shape = jax.ShapeDtypeStruct((N_DEV, K, nb), BF16)
    return _tc_call(
        body, name=name, grid=(N // tn, K // tkk),
        in_specs=[pl.BlockSpec((T, tkk), lambda j, i: (0, i)),
                  pl.BlockSpec((T, tn), lambda j, i: (0, j))],
        out_specs=out_spec, out_shape=out_shape,
        compiler_params=_params(("parallel", "parallel")),
    )(a, b)


def _rowwise(name, fn, rows, consts, outs, accs=(), tile=256, ncol=1, nrows=None):
    n_r, n_c, n_o = len(rows), len(consts), len(outs)
    T = rows[0][0].shape[0] if nrows is None else nrows
    tile = _pick(T, tuple(t for t in (512, 256, 128, 64, 32, 16, 8) if t <= tile))
    nt = T // tile
    const_rows = [c[1] if isinstance(c, tuple) else None for c in consts]
    consts = [c[0] if isinstance(c, tuple) else c for c in consts]

    def body(*refs):
        vals = [r[...] for r in refs[:n_r]]
        vals += [r[...] if l is None else r[l:l + 1, :] for r, l in zip(refs[n_r:n_r + n_c], const_rows)]
        res = fn(*vals)
        if not isinstance(res, (tuple, list)):
            res = (res,)
        o_refs = refs[n_r + n_c:n_r + n_c + n_o]
        a_refs = refs[n_r + n_c + n_o:]
        for r, v in zip(o_refs, res[:n_o]):
            r[...] = v.astype(r.dtype)
        first = pl.program_id(1) == 0
        for r, v in zip(a_refs, res[n_o:]):
            @pl.when(first)
            def _(r=r, v=v):
                r[...] = v.astype(F32)

            @pl.when(jnp.logical_not(first))
            def _(r=r, v=v):
                r[...] += v.astype(F32)

    in_specs, args = [], []
    for arr, w, off, roff in rows:
        rb = roff // tile
        assert roff % tile == 0
        in_specs.append(pl.BlockSpec((tile, w), lambda j, i, off=off, rb=rb: (i + rb, off + j)))
        args.append(arr)
    for cst in consts:
        in_specs.append(pl.BlockSpec(cst.shape, lambda j, i: (0, 0)))
        args.append(cst)
    out_specs, out_shape = [], []
    for tw, dt in outs:
        out_specs.append(pl.BlockSpec((tile, tw // ncol), lambda j, i: (i, j)))
        out_shape.append(jax.ShapeDtypeStruct((T, tw), dt))
    for nr, tw in accs:
        out_specs.append(pl.BlockSpec((nr, tw // ncol), lambda j, i: (0, j)))
        out_shape.append(jax.ShapeDtypeStruct((nr, tw), F32))
    res = _tc_call(
        body, name=name, grid=(ncol, nt), in_specs=in_specs, out_specs=out_specs, out_shape=out_shape,
        compiler_params=_params(("parallel", "arbitrary")),
    )(*args)
    return res


def _rms_fwd(name, x, g):
    D = x.shape[1]
    return _rowwise(name, lambda xv, gv: _rms(xv, gv), [(x, D, 0, 0)], [g], [(D, BF16)])[0]


def _rms_bwd(name, x, g, dh, dx_in):
    D = x.shape[1]

    def fn(xv, dhv, dxv, gv):
        _, vjp = jax.vjp(_rms, xv, gv)
        dx, dg = vjp(dhv.astype(F32))
        tot = dx + dxv
        return tot, tot, jnp.sum(dg, axis=0, keepdims=True)

    return _rowwise(name, fn, [(x, D, 0, 0), (dh, D, 0, 0), (dx_in, D, 0, 0)], [g], [(D, F32), (D, BF16)], [(1, D)],
                    tile=256)


def _lag_views(win, K, R, forward):
    n = win.shape[0]
    for r in range(8):
        if r >= K:
            break
        if r == 0:
            rolled = win
        else:
            rolled = pltpu.roll(win, (n - r) if forward else r, axis=0)
        for q in range((K - 1 - r) // 8 + 1):
            s = 8 * q + r
            if forward:
                yield s, rolled[8 * q:8 * q + R]
            else:
                yield s, rolled[HALO - 8 * q:HALO - 8 * q + R]


def _conv_chunk(win, w_ref, K, R):
    acc = None
    for s, view in _lag_views(win, K, R, forward=False):
        term = w_ref[K - 1 - s:K - s, :] * view
        acc = term if acc is None else acc + term
    return acc


def _conv_chunk_t(win, w_ref, K, R):
    acc = None
    for s, view in _lag_views(win, K, R, forward=True):
        term = w_ref[K - 1 - s:K - s, :] * view
        acc = term if acc is None else acc + term
    return acc


def _conv_dw(xwin, dy, K, R):
    taps = [None] * K
    for s, view in _lag_views(xwin, K, R, forward=False):
        taps[K - 1 - s] = jnp.sum(dy * view, axis=0, keepdims=True)
    return taps


def _chunks(T):
    R = _pick(T, (128, 64, 32))
    return R, T // R


def _glu_conv_fwd(name, proj, w, cw_total):
    T = proj.shape[0]
    C = cw_total
    cw = LANE
    nb = C // cw
    R, nch = _chunks(T)

    def body(a_ref, b_ref, w_ref, o_ref, s_ref):
        s_ref[0:HALO, :] = jnp.zeros((HALO, cw), F32)

        def fill(i, _):
            r0 = pl.multiple_of(i * R, R)
            s_ref[pl.ds(HALO + r0, R), :] = a_ref[pl.ds(r0, R), :] * _sigmoid(b_ref[pl.ds(r0, R), :])
            return 0

        lax.fori_loop(0, nch, fill, 0)

        def conv(i, _):
            r0 = pl.multiple_of(i * R, R)
            o_ref[pl.ds(r0, R), :] = _conv_chunk(s_ref[pl.ds(r0, R + HALO), :], w_ref, CONV_K, R)
            return 0

        lax.fori_loop(0, nch, conv, 0)

    return _tc_call(
        body, name=name, grid=(nb,),
        in_specs=[pl.BlockSpec((T, cw), lambda j: (0, j)), pl.BlockSpec((T, cw), lambda j: (0, nb + j)),
                  pl.BlockSpec((HALO, cw), lambda j: (0, j))],
        out_specs=pl.BlockSpec((T, cw), lambda j: (0, j)),
        out_shape=jax.ShapeDtypeStruct((T, C), F32),
        scratch_shapes=[pltpu.VMEM((T + HALO, cw), F32)],
        compiler_params=_params(("parallel",)),
    )(proj, proj, w)


def _glu_conv_bwd(name, proj, w, dhc, cw_total):
    T = proj.shape[0]
    C = cw_total
    cw = LANE
    nb = C // cw
    R, nch = _chunks(T)

    def body(a_ref, b_ref, w_ref, dy_ref, da_ref, db_ref, dw_ref, s_ref, g_ref, acc_ref):
        s_ref[0:HALO, :] = jnp.zeros((HALO, cw), F32)
        g_ref[T:T + HALO, :] = jnp.zeros((HALO, cw), F32)
        acc_ref[...] = jnp.zeros_like(acc_ref)

        def fill(i, _):
            r0 = pl.multiple_of(i * R, R)
            s_ref[pl.ds(HALO + r0, R), :] = a_ref[pl.ds(r0, R), :] * _sigmoid(b_ref[pl.ds(r0, R), :])
            g_ref[pl.ds(r0, R), :] = dy_ref[pl.ds(r0, R), :]
            return 0

        lax.fori_loop(0, nch, fill, 0)

        def back(i, _):
            r0 = pl.multiple_of(i * R, R)
            dhg = _conv_chunk_t(g_ref[pl.ds(r0, R + HALO), :], w_ref, CONV_K, R)
            av = a_ref[pl.ds(r0, R), :]
            sg = _sigmoid(b_ref[pl.ds(r0, R), :])
            da_ref[pl.ds(r0, R), :] = (dhg * sg).astype(da_ref.dtype)
            db_ref[pl.ds(r0, R), :] = (dhg * av * sg * (1.0 - sg)).astype(db_ref.dtype)
            taps = _conv_dw(s_ref[pl.ds(r0, R + HALO), :], dy_ref[pl.ds(r0, R), :], CONV_K, R)
            for k, tap in enumerate(taps):
                acc_ref[k:k + 1, :] += tap
            return 0

        lax.fori_loop(0, nch, back, 0)
        dw_ref[...] = acc_ref[...]

    return _tc_call(
        body, name=name, grid=(nb,),
        in_specs=[pl.BlockSpec((T, cw), lambda j: (0, j)), pl.BlockSpec((T, cw), lambda j: (0, nb + j)),
                  pl.BlockSpec((HALO, cw), lambda j: (0, j)), pl.BlockSpec((T, cw), lambda j: (0, j))],
        out_specs=[pl.BlockSpec((T, cw), lambda j: (0, j)), pl.BlockSpec((T, cw), lambda j: (0, j)),
                   pl.BlockSpec((HALO, cw), lambda j: (0, j))],
        out_shape=[jax.ShapeDtypeStruct((T, C), BF16), jax.ShapeDtypeStruct((T, C), BF16),
                   jax.ShapeDtypeStruct((HALO, C), F32)],
        scratch_shapes=[pltpu.VMEM((T + HALO, cw), F32), pltpu.VMEM((T + HALO, cw), F32),
                        pltpu.VMEM((HALO, cw), F32)],
        compiler_params=_params(("parallel",)),
    )(proj, proj, w, dhc)


def _ffn_conv_fwd(name, up, w, dff):
    T = up.shape[0]
    cw = LANE
    nb = dff // cw
    R, nch = _chunks(T)

    def body(g_ref, v_ref, wg_ref, wv_ref, o_ref, sg_ref, sv_ref):
        sg_ref[0:HALO, :] = jnp.zeros((HALO, cw), F32)
        sv_ref[0:HALO, :] = jnp.zeros((HALO, cw), F32)

        def fill(i, _):
            r0 = pl.multiple_of(i * R, R)
            sg_ref[pl.ds(HALO + r0, R), :] = g_ref[pl.ds(r0, R), :]
            sv_ref[pl.ds(HALO + r0, R), :] = v_ref[pl.ds(r0, R), :]
            return 0

        lax.fori_loop(0, nch, fill, 0)

        def conv(i, _):
            r0 = pl.multiple_of(i * R, R)
            gc = _conv_chunk(sg_ref[pl.ds(r0, R + HALO), :], wg_ref, FFN_K, R)
            vc = _conv_chunk(sv_ref[pl.ds(r0, R + HALO), :], wv_ref, FFN_K, R)
            o_ref[pl.ds(r0, R), :] = (_silu(gc) * vc).astype(o_ref.dtype)
            return 0

        lax.fori_loop(0, nch, conv, 0)

    return _tc_call(
        body, name=name, grid=(nb,),
        in_specs=[pl.BlockSpec((T, cw), lambda j: (0, j)), pl.BlockSpec((T, cw), lambda j: (0, nb + j)),
                  pl.BlockSpec((8, cw), lambda j: (0, j)), pl.BlockSpec((8, cw), lambda j: (0, nb + j))],
        out_specs=pl.BlockSpec((T, cw), lambda j: (0, j)),
        out_shape=jax.ShapeDtypeStruct((T, dff), BF16),
        scratch_shapes=[pltpu.VMEM((T + HALO, cw), F32), pltpu.VMEM((T + HALO, cw), F32)],
        compiler_params=_params(("parallel",)),
    )(up, up, w, w)


def _ffn_conv_bwd(name, up, w, dact, dff):
    T = up.shape[0]
    cw = LANE
    nb = dff // cw
    R, nch = _chunks(T)

    def body(g_ref, v_ref, wg_ref, wv_ref, da_ref, dg_ref, dv_ref, dwg_ref, dwv_ref,
             sg_ref, sv_ref, tg_ref, tv_ref, ag_ref, av_ref):
        zero = jnp.zeros((HALO, cw), F32)
        sg_ref[0:HALO, :] = zero
        sv_ref[0:HALO, :] = zero
        tg_ref[T:T + HALO, :] = zero
        tv_ref[T:T + HALO, :] = zero
        ag_ref[...] = jnp.zeros_like(ag_ref)
        av_ref[...] = jnp.zeros_like(av_ref)

        def fill(i, _):
            r0 = pl.multiple_of(i * R, R)
            sg_ref[pl.ds(HALO + r0, R), :] = g_ref[pl.ds(r0, R), :]
            sv_ref[pl.ds(HALO + r0, R), :] = v_ref[pl.ds(r0, R), :]
            return 0

        lax.fori_loop(0, nch, fill, 0)

        def grads(i, _):
            r0 = pl.multiple_of(i * R, R)
            gwin = sg_ref[pl.ds(r0, R + HALO), :]
            vwin = sv_ref[pl.ds(r0, R + HALO), :]
            gc = _conv_chunk(gwin, wg_ref, FFN_K, R)
            vc = _conv_chunk(vwin, wv_ref, FFN_K, R)
            da = da_ref[pl.ds(r0, R), :].astype(F32)
            sg = _sigmoid(gc)
            dgc = da * vc * (sg * (1.0 + gc * (1.0 - sg)))
            dvc = da * (gc * sg)
            tg_ref[pl.ds(r0, R), :] = dgc
            tv_ref[pl.ds(r0, R), :] = dvc
            for k, tap in enumerate(_conv_dw(gwin, dgc, FFN_K, R)):
                ag_ref[k:k + 1, :] += tap
            for k, tap in enumerate(_conv_dw(vwin, dvc, FFN_K, R)):
                av_ref[k:k + 1, :] += tap
            return 0

        lax.fori_loop(0, nch, grads, 0)

        def back(i, _):
            r0 = pl.multiple_of(i * R, R)
            dg_ref[pl.ds(r0, R), :] = _conv_chunk_t(tg_ref[pl.ds(r0, R + HALO), :], wg_ref, FFN_K, R).astype(dg_ref.dtype)
            dv_ref[pl.ds(r0, R), :] = _conv_chunk_t(tv_ref[pl.ds(r0, R + HALO), :], wv_ref, FFN_K, R).astype(dv_ref.dtype)
            return 0

        lax.fori_loop(0, nch, back, 0)
        dwg_ref[...] = ag_ref[...]
        dwv_ref[...] = av_ref[...]

    col = lambda j: (0, j)
    dg, dv, dwg, dwv = _tc_call(
        body, name=name, grid=(nb,),
        in_specs=[pl.BlockSpec((T, cw), col), pl.BlockSpec((T, cw), lambda j: (0, nb + j)),
                  pl.BlockSpec((8, cw), col), pl.BlockSpec((8, cw), lambda j: (0, nb + j)),
                  pl.BlockSpec((T, cw), col)],
        out_specs=[pl.BlockSpec((T, cw), col), pl.BlockSpec((T, cw), col),
                   pl.BlockSpec((8, cw), col), pl.BlockSpec((8, cw), col)],
        out_shape=[jax.ShapeDtypeStruct((T, dff), BF16), jax.ShapeDtypeStruct((T, dff), BF16),
                   jax.ShapeDtypeStruct((8, dff), F32), jax.ShapeDtypeStruct((8, dff), F32)],
        scratch_shapes=[pltpu.VMEM((T + HALO, cw), F32), pltpu.VMEM((T + HALO, cw), F32),
                        pltpu.VMEM((T + HALO, cw), F32), pltpu.VMEM((T + HALO, cw), F32),
                        pltpu.VMEM((8, cw), F32), pltpu.VMEM((8, cw), F32)],
        compiler_params=_params(("parallel",)),
    )(up, up, w, w, dact)
    return jnp.concatenate([dg, dv], axis=1), jnp.concatenate([dwg, dwv], axis=1)


def _zoh(a_re, a_im, log_dt):
    ar = jnp.minimum(a_re, -1e-4)
    ai = a_im
    dt = jnp.exp(log_dt)
    mag = jnp.exp(dt * ar)
    abar_re = mag * jnp.cos(dt * ai)
    abar_im = mag * jnp.sin(dt * ai)
    den = ar * ar + ai * ai
    nr = abar_re - 1.0
    ni = abar_im
    return abar_re, abar_im, (nr * ar + ni * ai) / den, (ni * ar - nr * ai) / den


def _discretize(a_re, a_im, log_dt, a_re_h, a_im_h, log_dt_h, b_re, b_im):
    abar_re, abar_im, _, _ = _zoh(a_re, a_im, log_dt)
    _, _, z_re, z_im = _zoh(a_re_h, a_im_h, log_dt_h)
    return abar_re, abar_im, z_re * b_re - z_im * b_im, z_re * b_im + z_im * b_re


def _full_specs(arrs):
    return [pl.BlockSpec(a.shape, lambda *_, n=len(a.shape): (0,) * n) for a in arrs]


def _ssm_prep(name, raw):
    def body(*refs):
        res = _discretize(*[r[...] for r in refs[:8]])
        for r, v in zip(refs[8:], res):
            r[...] = v

    outs = [jax.ShapeDtypeStruct(raw[0].shape, F32)] * 2 + [jax.ShapeDtypeStruct(raw[6].shape, F32)] * 2
    return _tc_call(body, name=name, in_specs=_full_specs(raw), out_specs=_full_specs(outs), out_shape=outs)(*raw)


def _ssm_prep_bwd(name, raw, cots):
    G = raw[0].shape[0]
    H = raw[3].shape[0] // G

    def body(*refs):
        _, vjp = jax.vjp(_discretize, *[r[...] for r in refs[:8]])
        g = vjp(tuple(r[...] for r in refs[8:12]))
        outs = refs[12:]
        for k in range(3):
            rep = g[3 + k]
            outs[k][...] = g[k] + jnp.sum(rep.reshape(G, H, rep.shape[1]), axis=1)
        outs[3][...] = g[6]
        outs[4][...] = g[7]

    outs = [jax.ShapeDtypeStruct(a.shape, F32) for a in (raw[0], raw[1], raw[2], raw[6], raw[7])]
    return _tc_call(body, name=name, in_specs=_full_specs(list(raw) + list(cots)), out_specs=_full_specs(outs),
                          out_shape=outs)(*raw, *cots)


def _cmul(ar, ai, br, bi):
    return ar * br - ai * bi, ar * bi + ai * br


def _scan_coefs(ar, ai, reverse):
    W = ar.shape[1]
    row = lax.broadcasted_iota(jnp.int32, (8, W), 0)
    p = [None] * 9
    p[1] = (ar, ai)
    for n in range(2, 9):
        p[n] = _cmul(*p[n // 2], *p[n - n // 2])
    steps = []
    for s in (1, 2, 4):
        valid = (row <= 7 - s) if reverse else (row >= s)
        steps.append((jnp.where(valid, p[s][0], 0.0), jnp.where(valid, p[s][1], 0.0)))
    pr = jnp.zeros((8, W), F32)
    pi = jnp.zeros((8, W), F32)
    for i in range(8):
        n = (8 - i) if reverse else (i + 1)
        pr = jnp.where(row == i, p[n][0], pr)
        pi = jnp.where(row == i, p[n][1], pi)
    return steps, (pr, pi)


def _scan_tile(xr, xi, cr, ci, coefs, reverse):
    steps, (pr, pi) = coefs
    for s, (sr, si) in zip((1, 2, 4), steps):
        shift = (8 - s) if reverse else s
        rr = pltpu.roll(xr, shift, axis=0)
        ri = pltpu.roll(xi, shift, axis=0)
        xr, xi = xr + sr * rr - si * ri, xi + sr * ri + si * rr
    xr, xi = xr + pr * cr - pi * ci, xi + pr * ci + pi * cr
    return xr, xi


def _edge_rows(x, reverse):
    W = x.shape[1]
    return jnp.broadcast_to(x[0:1, :] if reverse else x[7:8, :], (8, W))


def _power_table(ar, ai, pw_r, pw_i, n):
    W = ar.shape[1]
    a8r, a8i = jnp.broadcast_to(ar, (8, W)), jnp.broadcast_to(ai, (8, W))
    pr, pi = a8r, a8i
    for k in range(n):
        pw_r[8 * k:8 * k + 8, :] = pr
        pw_i[8 * k:8 * k + 8, :] = pi
        pr, pi = _cmul(pr, pi, a8r, a8i)


def _segment_order(Q):
    r = jnp.arange(Q)
    return (jnp.arange(Q)[None, :] == ((r % 8) * (Q // 8) + r // 8)[:, None]).astype(BF16)


def _to_segments(u, g, pm):
    C = u.shape[1]
    hi = u.astype(BF16)
    lo = (u - hi.astype(F32)).astype(BF16)
    moved = jnp.dot(pm, jnp.concatenate([hi, lo, g], axis=1), preferred_element_type=F32)
    return moved[:, :C].astype(BF16), moved[:, :C] + moved[:, C:2 * C], moved[:, 2 * C:]


def _segment_scan(xr_ref, xi_ref, row0, seg, ar, ai, pw_r, pw_i, carry_r, carry_i, reverse, visit=None, extra=()):
    W = ar.shape[1]
    sign = -1.0 if reverse else 1.0
    a8r, a8i = jnp.broadcast_to(ar, (8, W)), sign * jnp.broadcast_to(ai, (8, W))
    tile = lambda j: pl.ds(pl.multiple_of(row0 + 8 * j, 8), 8)

    def local(jj, x):
        j = seg - 1 - jj if reverse else jj
        xr = a8r * x[0] - a8i * x[1] + xr_ref[tile(j), :]
        xi = a8r * x[1] + a8i * x[0] + xi_ref[tile(j), :]
        xr_ref[tile(j), :] = xr
        xi_ref[tile(j), :] = xi
        return xr, xi

    zero = jnp.zeros((8, W), F32)
    xr, xi = lax.fori_loop(0, seg, local, (zero, zero))
    row = lax.broadcasted_iota(jnp.int32, (8, W), 0)
    edge = 7 if reverse else 0
    shift = 7 if reverse else 1
    gr = jnp.where(row == edge, carry_r, pltpu.roll(xr, shift, axis=0))
    gi = jnp.where(row == edge, carry_i, pltpu.roll(xi, shift, axis=0))
    top = 8 * (seg - 1)
    alr, ali = pw_r[top:top + 1, :], sign * pw_i[top:top + 1, :]
    er, ei = _scan_tile(gr, gi, zero, zero, _scan_coefs(alr, ali, reverse), reverse)
    far_r = _edge_rows(alr * er - ali * ei + xr, reverse)
    far_i = _edge_rows(alr * ei + ali * er + xi, reverse)

    def fix(j, ex):
        k = pl.ds(pl.multiple_of(8 * (seg - 1 - j if reverse else j), 8), 8)
        pr, pi = pw_r[k, :], sign * pw_i[k, :]
        fr = xr_ref[tile(j), :] + pr * er - pi * ei
        fi = xi_ref[tile(j), :] + pr * ei + pi * er
        xr_ref[tile(j), :] = fr
        xi_ref[tile(j), :] = fi
        return ex if visit is None else visit(j, fr, fi, ex)

    extra = lax.fori_loop(0, seg, fix, extra)
    return er, ei, far_r, far_i, extra


def _ssm_chunk(T):
    return _pick(T, (512, 256, 128, 64))


def _ssm_fwd(name, proj, u_off, p, width):
    T = proj.shape[0]
    NB = width // LANE
    Q = _ssm_chunk(T)
    nch = T // Q
    W = SSM_LANES

    def body(u_ref, bre, bim, cre, cim, ar_ref, ai_ref, d_ref, y_ref, ckr_ref, cki_ref, br_s, bi_s, car_r, car_i):
        c = pl.program_id(1)

        @pl.when(c == 0)
        def _():
            car_r[...] = jnp.zeros_like(car_r)
            car_i[...] = jnp.zeros_like(car_i)

        ckr_ref[...] = car_r[...]
        cki_ref[...] = car_i[...]
        u = u_ref[...]
        u16 = u.astype(BF16)
        br_s[...] = jnp.dot(u16, bre[...], preferred_element_type=F32)
        bi_s[...] = jnp.dot(u16, bim[...], preferred_element_type=F32)
        coefs = _scan_coefs(ar_ref[...], ai_ref[...], False)

        def tile(j, carry):
            r0 = pl.multiple_of(j * 8, 8)
            xr, xi = _scan_tile(br_s[pl.ds(r0, 8), :], bi_s[pl.ds(r0, 8), :], carry[0], carry[1], coefs, False)
            br_s[pl.ds(r0, 8), :] = xr
            bi_s[pl.ds(r0, 8), :] = xi
            return _edge_rows(xr, False), _edge_rows(xi, False)

        cr, ci = lax.fori_loop(0, Q // 8, tile, (car_r[...], car_i[...]))
        car_r[...] = cr
        car_i[...] = ci
        nt = (((1,), (1,)), ((), ()))
        y = (lax.dot_general(br_s[...].astype(BF16), cre[...], nt, preferred_element_type=F32)
             - lax.dot_general(bi_s[...].astype(BF16), cim[...], nt, preferred_element_type=F32)
             + d_ref[...] * u)
        y_ref[...] = _gelu(y).astype(y_ref.dtype)

    blk = lambda b, c: (b, 0, 0)
    mat = pl.BlockSpec((None, LANE, W), blk)
    vec = pl.BlockSpec((None, 1, W), blk)
    ck = pl.BlockSpec((None, None, 8, W), lambda b, c: (b, c, 0, 0))
    return _tc_call(
        body, name=name, grid=(NB, nch),
        in_specs=[pl.BlockSpec((Q, LANE), lambda b, c: (c, u_off + b)), mat, mat, mat, mat, vec, vec,
                  pl.BlockSpec((1, LANE), lambda b, c: (0, b))],
        out_specs=[pl.BlockSpec((Q, LANE), lambda b, c: (c, b)), ck, ck],
        out_shape=[jax.ShapeDtypeStruct((T, width), BF16), jax.ShapeDtypeStruct((NB, nch, 8, W), F32),
                   jax.ShapeDtypeStruct((NB, nch, 8, W), F32)],
        scratch_shapes=[pltpu.VMEM((Q, W), F32), pltpu.VMEM((Q, W), F32), pltpu.VMEM((8, W), F32),
                        pltpu.VMEM((8, W), F32)],
        compiler_params=_params(("parallel", "arbitrary")),
    )(proj, p["bre"], p["bim"], p["cre"], p["cim"], p["ar"], p["ai"], p["d"])


def _ssm_bwd(name, proj, u_off, p, ck_r, ck_i, dyg, width):
    T = proj.shape[0]
    NB = width // LANE
    Q = _ssm_chunk(T)
    nch = T // Q
    W = SSM_LANES
    nt_dims = (((1,), (1,)), ((), ()))
    tn_dims = (((0,), (0,)), ((), ()))

    def body(u_ref, dy_ref, pm_ref, pmt_ref, ckr_ref, cki_ref, bre, bim, cre, cim, ar_ref, ai_ref, d_ref,
             du_ref, dbr_ref, dbi_ref, dcr_ref, dci_ref, dar_ref, dai_ref, dd_ref,
             xr_s, xi_s, lr_s, li_s, lam_r, lam_i, pw_r, pw_i):
        c = pl.program_id(1)
        seg = Q // 8
        ar, ai = ar_ref[...], ai_ref[...]

        @pl.when(c == 0)
        def _():
            lam_r[...] = jnp.zeros_like(lam_r)
            lam_i[...] = jnp.zeros_like(lam_i)
            for r in (dbr_ref, dbi_ref, dcr_ref, dci_ref, dar_ref, dai_ref, dd_ref):
                r[...] = jnp.zeros_like(r)
            _power_table(ar, ai, pw_r, pw_i, seg)

        u16, u, dyg = _to_segments(u_ref[...], dy_ref[...], pm_ref[...])
        xr_s[8:Q + 8, :] = jnp.dot(u16, bre[...], preferred_element_type=F32)
        xi_s[8:Q + 8, :] = jnp.dot(u16, bim[...], preferred_element_type=F32)
        enter_r, enter_i, _, _, _ = _segment_scan(xr_s, xi_s, 8, seg, ar, ai, pw_r, pw_i, ckr_ref[...], cki_ref[...],
                                                  False)
        xr_s[0:8, :] = enter_r
        xi_s[0:8, :] = enter_i
        xr16 = xr_s[8:Q + 8, :].astype(BF16)
        xi16 = xi_s[8:Q + 8, :].astype(BF16)
        y = (lax.dot_general(xr16, cre[...], nt_dims, preferred_element_type=F32)
             - lax.dot_general(xi16, cim[...], nt_dims, preferred_element_type=F32) + d_ref[...] * u)
        _, gelu_vjp = jax.vjp(_gelu, y)
        dy = gelu_vjp(dyg)[0]
        dy16 = dy.astype(BF16)
        dd_ref[...] += jnp.broadcast_to(jnp.sum(dy * u, axis=0, keepdims=True), (8, LANE))
        dcr_ref[...] += lax.dot_general(dy16, xr16, tn_dims, preferred_element_type=F32)
        dci_ref[...] -= lax.dot_general(dy16, xi16, tn_dims, preferred_element_type=F32)
        lr_s[...] = jnp.dot(dy16, cre[...], preferred_element_type=F32)
        li_s[...] = -jnp.dot(dy16, cim[...], preferred_element_type=F32)

        def visit(j, lr, li, sums):
            before = pl.ds(pl.multiple_of(8 * j, 8), 8)
            xpr, xpi = xr_s[before, :], xi_s[before, :]
            return sums[0] + lr * xpr + li * xpi, sums[1] + li * xpr - lr * xpi

        zero = jnp.zeros((8, W), F32)
        _, _, cr, ci, sums = _segment_scan(lr_s, li_s, 0, seg, ar, ai, pw_r, pw_i, lam_r[...], lam_i[...], True, visit,
                                           (zero, zero))
        lam_r[...] = cr
        lam_i[...] = ci
        dar_ref[...] += jnp.broadcast_to(jnp.sum(sums[0], axis=0, keepdims=True), (8, W))
        dai_ref[...] += jnp.broadcast_to(jnp.sum(sums[1], axis=0, keepdims=True), (8, W))
        lr16 = lr_s[...].astype(BF16)
        li16 = li_s[...].astype(BF16)
        dbr_ref[...] += lax.dot_general(u16, lr16, tn_dims, preferred_element_type=F32)
        dbi_ref[...] += lax.dot_general(u16, li16, tn_dims, preferred_element_type=F32)
        du = (lax.dot_general(lr16, bre[...], nt_dims, preferred_element_type=F32)
              + lax.dot_general(li16, bim[...], nt_dims, preferred_element_type=F32) + d_ref[...] * dy)
        du_ref[...] = jnp.dot(pmt_ref[...], du.astype(BF16), preferred_element_type=F32).astype(du_ref.dtype)

    blk = lambda b, c: (b, 0, 0)
    mat = pl.BlockSpec((None, LANE, W), blk)
    vec = pl.BlockSpec((None, 1, W), blk)
    acc8 = pl.BlockSpec((None, 8, W), blk)
    ck = pl.BlockSpec((None, None, 8, W), lambda b, c: (b, nch - 1 - c, 0, 0))
    order = pl.BlockSpec((Q, Q), lambda b, c: (0, 0))
    pm = _segment_order(Q)
    return _tc_call(
        body, name=name, grid=(NB, nch),
        in_specs=[pl.BlockSpec((Q, LANE), lambda b, c: (nch - 1 - c, u_off + b)),
                  pl.BlockSpec((Q, LANE), lambda b, c: (nch - 1 - c, b)), order, order, ck, ck, mat, mat, mat, mat,
                  vec, vec, pl.BlockSpec((1, LANE), lambda b, c: (0, b))],
        out_specs=[pl.BlockSpec((Q, LANE), lambda b, c: (nch - 1 - c, b)), mat, mat, mat, mat, acc8, acc8,
                   pl.BlockSpec((None, 8, LANE), blk)],
        out_shape=[jax.ShapeDtypeStruct((T, width), BF16)] + [jax.ShapeDtypeStruct((NB, LANE, W), F32)] * 4
                  + [jax.ShapeDtypeStruct((NB, 8, W), F32)] * 2 + [jax.ShapeDtypeStruct((NB, 8, LANE), F32)],
        scratch_shapes=[pltpu.VMEM((Q + 8, W), F32), pltpu.VMEM((Q + 8, W), F32), pltpu.VMEM((Q, W), F32),
                        pltpu.VMEM((Q, W), F32), pltpu.VMEM((8, W), F32), pltpu.VMEM((8, W), F32),
                        pltpu.VMEM((Q, W), F32), pltpu.VMEM((Q, W), F32)],
        compiler_params=_params(("parallel", "arbitrary")),
    )(proj, dyg, pm, pm.T, ck_r, ck_i, p["bre"], p["bim"], p["cre"], p["cim"], p["ar"], p["ai"], p["d"])


def _block_diag(w):
    G, H, P = w.shape
    eye = jnp.eye(8, dtype=w.dtype)
    return (w.reshape(G // 8, 8, H, 1, P) * eye[None, :, None, :, None]).reshape(G // 8, 8 * H, 8 * P)


def _block_diag_t(d, H, P):
    NB = d.shape[0]
    d = d.reshape(NB, 8, H, 8, P)
    eye = jnp.eye(8, dtype=d.dtype)
    return jnp.sum(d * eye[None, :, None, :, None], axis=3).reshape(NB * 8, H, P)


def _attn_fwd(name, q, kv, heads):
    T, D = q.shape
    Mm = kv.shape[0]
    hd = D // heads
    tq = _pick(T, (512, 256, 128))
    scale = hd ** -0.5

    def body(q_ref, k_ref, v_ref, o_ref):
        s = lax.dot_general(q_ref[...], k_ref[...], (((1,), (1,)), ((), ())), preferred_element_type=F32) * scale
        s = s - jnp.max(s, axis=-1, keepdims=True)
        e = jnp.exp(s)
        p = e / jnp.sum(e, axis=-1, keepdims=True)
        o_ref[...] = jnp.dot(p.astype(BF16), v_ref[...], preferred_element_type=F32).astype(o_ref.dtype)

    return _tc_call(
        body, name=name, grid=(heads, T // tq),
        in_specs=[pl.BlockSpec((tq, hd), lambda h, i: (i, h)), pl.BlockSpec((Mm, hd), lambda h, i: (0, h)),
                  pl.BlockSpec((Mm, hd), lambda h, i: (0, heads + h))],
        out_specs=pl.BlockSpec((tq, hd), lambda h, i: (i, h)),
        out_shape=jax.ShapeDtypeStruct((T, D), BF16),
        compiler_params=_params(("parallel", "parallel")),
    )(q, kv, kv)


def _attn_bwd(name, q, kv, do, heads):
    T, D = q.shape
    Mm = kv.shape[0]
    hd = D // heads
    tq = _pick(T, (512, 256, 128))
    scale = hd ** -0.5
    nt_dims = (((1,), (1,)), ((), ()))
    tn_dims = (((0,), (0,)), ((), ()))

    def body(q_ref, k_ref, v_ref, do_ref, dq_ref, dk_ref, dv_ref):
        i = pl.program_id(1)

        @pl.when(i == 0)
        def _():
            dk_ref[...] = jnp.zeros_like(dk_ref)
            dv_ref[...] = jnp.zeros_like(dv_ref)

        qv, kvl, vv, dov = q_ref[...], k_ref[...], v_ref[...], do_ref[...]
        s = lax.dot_general(qv, kvl, nt_dims, preferred_element_type=F32) * scale
        s = s - jnp.max(s, axis=-1, keepdims=True)
        e = jnp.exp(s)
        p = e / jnp.sum(e, axis=-1, keepdims=True)
        p16 = p.astype(BF16)
        dv_ref[...] += lax.dot_general(p16, dov, tn_dims, preferred_element_type=F32)
        dp = lax.dot_general(dov, vv, nt_dims, preferred_element_type=F32)
        ds = (p * (dp - jnp.sum(dp * p, axis=-1, keepdims=True)) * scale).astype(BF16)
        dq_ref[...] = jnp.dot(ds, kvl, preferred_element_type=F32).astype(dq_ref.dtype)
        dk_ref[...] += lax.dot_general(ds, qv, tn_dims, preferred_element_type=F32)

    return _tc_call(
        body, name=name, grid=(heads, T // tq),
        in_specs=[pl.BlockSpec((tq, hd), lambda h, i: (i, h)), pl.BlockSpec((Mm, hd), lambda h, i: (0, h)),
                  pl.BlockSpec((Mm, hd), lambda h, i: (0, heads + h)), pl.BlockSpec((tq, hd), lambda h, i: (i, h))],
        out_specs=[pl.BlockSpec((tq, hd), lambda h, i: (i, h)), pl.BlockSpec((Mm, hd), lambda h, i: (0, h)),
                   pl.BlockSpec((Mm, hd), lambda h, i: (0, h))],
        out_shape=[jax.ShapeDtypeStruct((T, D), BF16), jax.ShapeDtypeStruct((Mm, D), F32),
                   jax.ShapeDtypeStruct((Mm, D), F32)],
        compiler_params=_params(("parallel", "arbitrary")),
    )(q, kv, kv, do)


def _position():
    return lax.axis_index("x"), lax.axis_index("y"), lax.axis_index("c")


def _allgather(name, blk, row_mode):
    L = blk.shape[0]
    out_shape = (L, N_DEV) + blk.shape[1:] if row_mode else (N_DEV,) + blk.shape
    x_ref = jax.new_ref(blk, memory_space=pltpu.MemorySpace.HBM)
    out_ref = jax.empty_ref(jax.ShapeDtypeStruct(out_shape, blk.dtype), memory_space=pltpu.MemorySpace.HBM)

    def body(send_sems, recv_sems, local_sem):
        x, y, c = _position()
        me, sibling = (x, y, c), (x, y, 1 - c)
        chips = [(1 - x, y), (x, 1 - y), (1 - x, 1 - y)]
        barrier = pltpu.get_barrier_semaphore()
        for peer in [sibling] + [(*chip, c) for chip in chips]:
            pl.semaphore_signal(barrier, inc=1, device_id=peer, device_id_type=MESH_ID)
        pl.semaphore_wait(barrier, 4)

        def slot(px, py, pc):
            b = 4 * px + 2 * py + pc
            return out_ref.at[:, b] if row_mode else out_ref.at[b]

        def copy(k, block, to, src=None):
            return pltpu.make_async_remote_copy(
                src_ref=slot(*block) if src is None else src, dst_ref=slot(*block),
                send_sem=send_sems.at[k], recv_sem=recv_sems.at[k], device_id=to, device_id_type=MESH_ID)

        mine = pltpu.make_async_copy(x_ref, slot(*me), local_sem)
        mine.start()
        first = [copy(0, me, sibling, src=x_ref)]
        first += [copy(1 + j, me, (*chip, c), src=x_ref) for j, chip in enumerate(chips)]
        for cp in first:
            cp.start()
        passed = [copy(4 + j, (*chip, c), sibling) for j, chip in enumerate(chips)]
        for j, chip in enumerate(chips):
            copy(1 + j, (*chip, c), me).wait_recv()
            passed[j].start()
        copy(0, sibling, me).wait_recv()
        for j, chip in enumerate(chips):
            copy(4 + j, (*chip, 1 - c), me).wait_recv()
        for cp in first + passed:
            cp.wait_send()
        mine.wait()

    pl.kernel(
        body, mesh=plsc.ScalarSubcoreMesh(axis_name="sequencer", num_cores=1), name=name,
        scratch_types=(pltpu.SemaphoreType.DMA((7,)), pltpu.SemaphoreType.DMA((7,)), pltpu.SemaphoreType.DMA),
        compiler_params=pltpu.CompilerParams(collective_id=AG_COLLECTIVE_ID),
    )()
    return out_ref[...]


def _sequencer_kernel(name, body, scratch_types, collective_id):
    pl.kernel(
        body, mesh=plsc.ScalarSubcoreMesh(axis_name="sequencer", num_cores=1), name=name,
        scratch_types=scratch_types, compiler_params=pltpu.CompilerParams(collective_id=collective_id),
    )()


def _handshake(peers):
    barrier = pltpu.get_barrier_semaphore()
    for peer in peers:
        pl.semaphore_signal(barrier, inc=1, device_id=peer, device_id_type=MESH_ID)
    pl.semaphore_wait(barrier, len(peers))


def _rs_sibling(name, grads):
    hbm = pltpu.MemorySpace.HBM
    g_ref = jax.new_ref(grads, memory_space=hbm)
    out_ref = jax.empty_ref(jax.ShapeDtypeStruct((4,) + grads.shape[1:], grads.dtype), memory_space=hbm)

    def body(send_sems, recv_sems):
        x, y, c = _position()
        sibling = (x, y, 1 - c)
        _handshake([sibling])
        copies = [pltpu.make_async_remote_copy(
            src_ref=g_ref.at[2 * q + (1 - c)], dst_ref=out_ref.at[q], send_sem=send_sems.at[q],
            recv_sem=recv_sems.at[q], device_id=sibling, device_id_type=MESH_ID) for q in range(4)]
        for cp in copies:
            cp.start()
        for cp in copies:
            cp.wait_recv()
        for cp in copies:
            cp.wait_send()

    _sequencer_kernel(name, body, (pltpu.SemaphoreType.DMA((4,)), pltpu.SemaphoreType.DMA((4,))),
                      RS_SIBLING_COLLECTIVE_ID)
    return out_ref[...]


def _rs_chips(name, part):
    hbm = pltpu.MemorySpace.HBM
    p_ref = jax.new_ref(part, memory_space=hbm)
    out_ref = jax.empty_ref(jax.ShapeDtypeStruct(part.shape, part.dtype), memory_space=hbm)

    def body(send_sems, recv_sems, local_sem):
        x, y, c = _position()
        flips = [(1 - x, y), (x, 1 - y), (1 - x, 1 - y)]
        _handshake([(fx, fy, c) for fx, fy in flips])
        mine = pltpu.make_async_copy(p_ref.at[2 * x + y], out_ref.at[3], local_sem)
        copies = [pltpu.make_async_remote_copy(
            src_ref=p_ref.at[2 * fx + fy], dst_ref=out_ref.at[k], send_sem=send_sems.at[k], recv_sem=recv_sems.at[k],
            device_id=(fx, fy, c), device_id_type=MESH_ID) for k, (fx, fy) in enumerate(flips)]
        for cp in copies:
            cp.start()
        mine.start()
        for cp in copies:
            cp.wait_recv()
        for cp in copies:
            cp.wait_send()
        mine.wait()

    _sequencer_kernel(name, body, (pltpu.SemaphoreType.DMA((3,)), pltpu.SemaphoreType.DMA((3,)),
                                   pltpu.SemaphoreType.DMA), RS_CHIPS_COLLECTIVE_ID)
    return out_ref[...]


def _adamw_math(g, w, m, v):
    m = ADAM_B1 * m + (1.0 - ADAM_B1) * g
    v = ADAM_B2 * v + (1.0 - ADAM_B2) * (g * g)
    m_hat = m / (1.0 - ADAM_B1 ** ADAM_STEP)
    v_hat = v / (1.0 - ADAM_B2 ** ADAM_STEP)
    delta = -ADAM_LR * (m_hat / (jnp.sqrt(v_hat) + ADAM_EPS) + ADAM_WD * w)
    return delta, m, v


def _adamw_natural(name, gs, ws, ms, vs, lead_block=None):
    n = len(gs)

    def body(*refs):
        ins, outs = refs[:4 * n], refs[4 * n:]
        for k in range(n):
            delta, nm, nv = _adamw_math(*[ins[j * n + k][...] for j in range(4)])
            outs[k][...] = delta
            outs[n + k][...] = nm
            outs[2 * n + k][...] = nv

    out_shape = [jax.ShapeDtypeStruct(w.shape, F32) for w in ws] * 3
    if lead_block is None:
        grid = ()
        spec = lambda a: pl.BlockSpec(a.shape, lambda nd=len(a.shape): (0,) * nd)
    else:
        grid = tuple(d // b for d, b in zip(ws[0].shape[:2], lead_block))
        spec = lambda a: pl.BlockSpec(tuple(lead_block) + tuple(a.shape[2:]),
                                      lambda i, j, nd=len(a.shape): (i, j) + (0,) * (nd - 2))
    return _tc_call(
        body, name=name, grid=grid, in_specs=[spec(a) for a in list(gs) + list(ws) + list(ms) + list(vs)],
        out_specs=[spec(o) for o in out_shape], out_shape=out_shape,
        compiler_params=pltpu.CompilerParams(vmem_limit_bytes=VMEM_LIMIT),
    )(*gs, *ws, *ms, *vs)


def _ew_tile(R, C, nblocks):
    budget = (VMEM_LIMIT * 3) // 4
    return _pick(R, tuple(t for t in (1024, 512, 256, 128, 64, 32, 16, 8) if 8 * t * C * nblocks <= budget))


def _pair_sum(name, grads, landed, c_idx):
    _, r, c = grads.shape
    tile = _ew_tile(r, c, 3)

    def body(c_ref, g_ref, s_ref, o_ref):
        o_ref[...] = (g_ref[...].astype(F32) + s_ref[...].astype(F32)).astype(o_ref.dtype)

    return _tc_call(
        body, name=name,
        grid_spec=pltpu.PrefetchScalarGridSpec(
            num_scalar_prefetch=1, grid=(4, r // tile),
            in_specs=[pl.BlockSpec((None, tile, c), lambda q, i, c_ref: (2 * q + c_ref[0], i, 0)),
                      pl.BlockSpec((None, tile, c), lambda q, i, c_ref: (q, i, 0))],
            out_specs=pl.BlockSpec((None, tile, c), lambda q, i, c_ref: (q, i, 0))),
        out_shape=jax.ShapeDtypeStruct((4, r, c), BF16),
        compiler_params=_params(("parallel", "parallel")),
    )(c_idx, grads, landed)


def _adamw_layer(name, got, w, m, v, layer, prev):
    L, r, c = w.shape
    tile = _ew_tile(r, c, 11)

    def body(g0, g1, g2, g3, w_ref, m_ref, v_ref, *rest):
        outs = rest[-4:]
        g = (g0[...].astype(F32) + g1[...].astype(F32)) + (g2[...].astype(F32) + g3[...].astype(F32))
        delta, nm, nv = _adamw_math(g, w_ref[...], m_ref[...], v_ref[...])
        for ref, val in zip(outs, (g, delta, nm, nv)):
            ref[...] = val

    slab = pl.BlockSpec((None, tile, c), lambda i: (layer, i, 0))
    in_specs = [pl.BlockSpec((None, tile, c), lambda i, k=k: (k, i, 0)) for k in range(4)] + [slab] * 3
    args = [got, got, got, got, w, m, v]
    aliases = {}
    if prev is not None:
        in_specs += [ANY] * 4
        args += list(prev)
        aliases = {7 + k: k for k in range(4)}
    return _tc_call(
        body, name=name, grid=(r // tile,), in_specs=in_specs, out_specs=[slab] * 4,
        out_shape=[jax.ShapeDtypeStruct((L, r, c), F32)] * 4, input_output_aliases=aliases,
        compiler_params=_params(("parallel",)),
    )(*args)


def _reduce_pipeline(name, layer, grads, w, m, v, state):
    c_idx = lax.axis_index("c").astype(jnp.int32).reshape(1)
    landed = _rs_sibling(name + "_rs_sibling", grads)
    yield
    part = _pair_sum(name + "_pair_sum", grads, landed, c_idx)
    got = _rs_chips(name + "_rs_chips", part)
    for _ in range(4):
        yield
    state[name] = _adamw_layer(name + "_adamw", got, w, m, v, layer, state.get(name))


def _loss_head(name, x, g, target):
    T, D = x.shape

    def fn(xv, tv, gv):
        def f(xx, gg):
            err = _rms(xx, gg) - tv
            return 0.5 * jnp.sum(jnp.mean(err * err, axis=-1, keepdims=True))

        loss, (dx, dg) = jax.value_and_grad(f, argnums=(0, 1))(xv, gv)
        return dx, dx, jnp.full((8, LANE), loss, F32), jnp.sum(dg, axis=0, keepdims=True)

    dx, dx16, loss, dg = _rowwise(name, fn, [(x, D, 0, 0), (target, D, 0, 0)], [g], [(D, F32), (D, BF16)],
                                  [(8, LANE), (1, D)], tile=128)
    return loss[0, 0], dx, dx16, dg


_SHARDED_COL = ("w_in", "conv_dw_w", "conv_w_pw", "ssm_w_glu", "xa_w_kv", "ffn_w_up", "ffn_dw_w")
_SHARDED_ROW = ("w_out", "xa_w_q", "xa_w_o", "ffn_w_down")
_WEIGHTS = ['mix_norm_g', 'w_in', 'conv_dw_w', 'conv_dw_b', 'conv_ln_g', 'conv_ln_b', 'conv_w_pw', 'ssm_a_re',
            'ssm_a_im', 'ssm_log_dt', 'ssm_b_re', 'ssm_b_im', 'ssm_c_re', 'ssm_c_im', 'ssm_d', 'ssm_w_glu', 'w_out',
            'xa_norm_g', 'mem_norm_g', 'xa_w_q', 'xa_w_kv', 'xa_w_o', 'ffn_norm_g', 'ffn_w_up', 'ffn_dw_w',
            'ffn_w_down', 'final_norm_g']
_FWD = ['x', 'mem'] + _WEIGHTS
_AG_ORDER = ("w_in", "conv_dw_w", "conv_w_pw", "ssm_w_glu", "w_out", "xa_w_q", "xa_w_kv", "xa_w_o", "ffn_w_up",
             "ffn_dw_w", "ffn_w_down")


def _pad_rows(a, rows):
    return jnp.pad(a, ((0, 0), (0, rows - a.shape[1]), (0, 0)))


def _step(inp, target, mom_m, mom_v):
    _LAST_CALL.clear()
    x0 = inp["x"][0]
    mem = inp["mem"][0]
    T, D = x0.shape
    L = inp["w_in"].shape[0]
    CW = inp["conv_dw_b"].shape[1]
    SW = inp["ssm_d"].shape[1]
    DFF = inp["ffn_w_down"].shape[1] * N_DEV
    G = SW // SSM_GROUP
    NB = SW // LANE
    u_off = (2 * CW) // LANE
    gate_off = (2 * CW + SW) // 1024

    gathered = {n: [None] * L for n in _AG_ORDER}
    filter_rows = {"conv_dw_w": HALO, "ffn_dw_w": 8}
    for l in range(L):
        for n in _AG_ORDER:
            blk = inp[n][l:l + 1]
            blk = _pad_rows(blk, filter_rows[n]) if n in filter_rows else blk.astype(BF16)
            if n in _SHARDED_COL:
                gathered[n][l] = _allgather("ag_" + n, blk, False)
            else:
                full = _allgather("ag_" + n, blk, True)
                gathered[n][l] = full.reshape(1, N_DEV * full.shape[2], full.shape[3])

    def W(n, l):
        return _W(gathered[n][l], 0, n in _SHARDED_COL)

    def dw_filter(n, l):
        g = gathered[n][l]
        return jnp.transpose(g[:, 0], (1, 0, 2)).reshape(g.shape[2], N_DEV * g.shape[3])

    def row(n, l):
        return (inp[n], l)

    ssm_raw, ssm_p = [], []
    for l in range(L):
        a_re, a_im, ldt = inp["ssm_a_re"][l], inp["ssm_a_im"][l], inp["ssm_log_dt"][l][:, None]
        rep = lambda a: jnp.repeat(a, SSM_GROUP, axis=0)
        flat = lambda b: jnp.transpose(b, (0, 2, 1)).reshape(G * SSM_GROUP, SSM_STATE)
        raw = (a_re, a_im, ldt, rep(a_re), rep(a_im), rep(ldt), flat(inp["ssm_b_re"][l]), flat(inp["ssm_b_im"][l]))
        abar_re, abar_im, bbar_re, bbar_im = _ssm_prep("ssm_prep", raw)
        bbar_re = bbar_re.reshape(G, SSM_GROUP, SSM_STATE)
        bbar_im = bbar_im.reshape(G, SSM_GROUP, SSM_STATE)
        ssm_raw.append(raw)
        ssm_p.append(dict(
            bre=_block_diag(bbar_re).astype(BF16), bim=_block_diag(bbar_im).astype(BF16),
            cre=_block_diag(inp["ssm_c_re"][l]).astype(BF16), cim=_block_diag(inp["ssm_c_im"][l]).astype(BF16),
            ar=abar_re.reshape(NB, 1, SSM_LANES), ai=abar_im.reshape(NB, 1, SSM_LANES), d=inp["ssm_d"][l][None, :]))

    saved = []
    x = x0
    for l in range(L):
        s = {"x_in": x}
        s["h1"] = _rms_fwd("rms_mix", x, row("mix_norm_g", l))
        s["proj"] = _mm_nn("mm_w_in", s["h1"], W("w_in", l), F32)
        s["hc"] = _glu_conv_fwd("conv_fwd", s["proj"], dw_filter("conv_dw_w", l), CW)
        s["hs"] = _rowwise("conv_post", _convpost, [(s["hc"], CW, 0, 0)],
                           [row("conv_dw_b", l), row("conv_ln_g", l), row("conv_ln_b", l)], [(CW, BF16)])[0]
        s["ya"] = _mm_nn("mm_w_pw", s["hs"], W("conv_w_pw", l), F32)
        s["yg"], s["ck_r"], s["ck_i"] = _ssm_fwd("ssm_fwd", s["proj"], u_off, ssm_p[l], SW)
        s["gg"] = _mm_nn("mm_w_glu", s["yg"], W("ssm_w_glu", l), F32)
        nmix = D // 1024
        mix_rows = [(s["proj"], 1024, gate_off, 0), (s["proj"], 1024, gate_off + nmix, 0), (s["ya"], 1024, 0, 0),
                    (s["gg"], 1024, 0, 0), (s["gg"], 1024, nmix, 0)]
        s["mix_rows"] = mix_rows
        s["mix"] = _rowwise("mix_fwd", _mixf, mix_rows, [], [(D, BF16)], ncol=nmix)[0]
        x = _mm_nn("mm_w_out", s["mix"], W("w_out", l), F32, add=x)
        s["x1"] = x
        s["h2"] = _rms_fwd("rms_xa", x, row("xa_norm_g", l))
        s["q"] = _mm_nn("mm_w_q", s["h2"], W("xa_w_q", l), BF16)
        s["mn"] = _rms_fwd("rms_mem", mem, row("mem_norm_g", l))
        s["kv"] = _mm_nn("mm_w_kv", s["mn"], W("xa_w_kv", l), BF16)
        s["o"] = _attn_fwd("attn_fwd", s["q"], s["kv"], XA_HEADS)
        x = _mm_nn("mm_w_o", s["o"], W("xa_w_o", l), F32, add=x)
        s["x2"] = x
        s["h3"] = _rms_fwd("rms_ffn", x, row("ffn_norm_g", l))
        s["up"] = _mm_nn("mm_w_up", s["h3"], W("ffn_w_up", l), F32)
        s["act"] = _ffn_conv_fwd("ffn_conv_fwd", s["up"], dw_filter("ffn_dw_w", l), DFF)
        x = _mm_nn("mm_w_down", s["act"], W("ffn_w_down", l), F32, add=x)
        saved.append(s)

    loss_part, dx, dx16, d_final_g = _loss_head("loss_head", x, inp["final_norm_g"][None, :], target[0])

    big = _SHARDED_COL + _SHARDED_ROW
    small = {n: [None] * L for n in _WEIGHTS if n not in big and n != "final_norm_g"}
    pads = {"conv_dw_w": HALO, "ffn_dw_w": 8}
    shards = {n: tuple(_pad_rows(a, pads[n]) if n in pads else a for a in (inp[n], mom_m[n], mom_v[n])) for n in big}
    state, queue = {}, []
    names = [n for n in _WEIGHTS if n not in big]
    sizes = [inp[n].size for n in names]
    total = sum(sizes)
    pack_w = 8 * LANE
    rows_p = -(-total // (LANE * pack_w)) * LANE
    padn = rows_p * pack_w - total

    def pack(parts, fill):
        return jnp.concatenate([p.reshape(-1) for p in parts] + [jnp.full((padn,), fill, F32)]).reshape(rows_p, pack_w)

    def total(*parts):
        tot = parts[0]
        for part in parts[1:]:
            tot = tot + part
        return tot

    def tick():
        for gen in list(queue):
            if next(gen, "done") == "done":
                queue.remove(gen)

    def emit(n, l, g):
        queue.append(_reduce_pipeline(n, l, g.astype(BF16), *shards[n], state))
        tick()

    deferred = []

    def emit_small(n, l, thunk):
        if l == 0:
            deferred.append((n, thunk))
        else:
            emit(n, l, thunk())

    for l in reversed(range(L)):
        s = saved[l]
        dact = _mm_nt("mm_w_down_t", dx16, W("ffn_w_down", l), BF16)
        emit("ffn_w_down", l, _mm_tn("mm_dw_down", s["act"], dx16).reshape(N_DEV, DFF // N_DEV, D))
        d_up, d_ffn_dw = _ffn_conv_bwd("ffn_conv_bwd", s["up"], dw_filter("ffn_dw_w", l), dact, DFF)
        emit("ffn_dw_w", l, jnp.transpose(d_ffn_dw.reshape(8, N_DEV, 2 * DFF // N_DEV), (1, 0, 2)))
        dh3 = _mm_nt("mm_w_up_t", d_up, W("ffn_w_up", l), BF16)
        emit("ffn_w_up", l, _mm_tn("mm_dw_up", s["h3"], d_up, nb=2 * DFF // N_DEV))
        dx, dx16, small["ffn_norm_g"][l] = _rms_bwd("rms_ffn_bwd", s["x2"], row("ffn_norm_g", l), dh3, dx)
        do = _mm_nt("mm_w_o_t", dx16, W("xa_w_o", l), BF16)
        emit_small("xa_w_o", l, lambda a=s["o"], b=dx16: _mm_tn("mm_dw_o", a, b).reshape(N_DEV, D // N_DEV, D))
        dq, dk, dv = _attn_bwd("attn_bwd", s["q"], s["kv"], do, XA_HEADS)
        dkv = jnp.concatenate([dk, dv], axis=1).astype(BF16)
        dh2 = _mm_nt("mm_w_q_t", dq, W("xa_w_q", l), BF16)
        emit_small("xa_w_q", l, lambda a=s["h2"], b=dq: _mm_tn("mm_dw_q", a, b).reshape(N_DEV, D // N_DEV, D))
        dmn = _mm_nt("mm_w_kv_t", dkv, W("xa_w_kv", l), BF16)
        emit("xa_w_kv", l, _mm_tn("mm_dw_kv", s["mn"], dkv, nb=2 * D // N_DEV))

        def mem_bwd(mv, dv_, gv):
            _, vjp = jax.vjp(_rms, mv, gv)
            return jnp.sum(vjp(dv_.astype(F32))[1], axis=0, keepdims=True)

        small["mem_norm_g"][l] = _rowwise("rms_mem_bwd", mem_bwd, [(mem, D, 0, 0), (dmn, D, 0, 0)],
                                          [row("mem_norm_g", l)], [], [(1, D)], tile=128)[0]
        dx, dx16, small["xa_norm_g"][l] = _rms_bwd("rms_xa_bwd", s["x1"], row("xa_norm_g", l), dh2, dx)
        dmix = _mm_nt("mm_w_out_t", dx16, W("w_out", l), BF16)
        emit_small("w_out", l, lambda a=s["mix"], b=dx16: _mm_tn("mm_dw_out", a, b).reshape(N_DEV, D // N_DEV, D))

        def mix_bwd(gla, glb, ya, ga, gb, dm):
            _, vjp = jax.vjp(_mixf, gla, glb, ya, ga, gb)
            return vjp(dm.astype(F32))

        nmix = D // 1024
        dgla, dglb, dya, dga, dgb = _rowwise("mix_bwd", mix_bwd, s["mix_rows"] + [(dmix, 1024, 0, 0)], [],
                                             [(D, BF16)] * 5, ncol=nmix)
        dgg = jnp.concatenate([dga, dgb], axis=1)
        dyg = _mm_nt("mm_w_glu_t", dgg, W("ssm_w_glu", l), BF16)
        emit_small("ssm_w_glu", l, lambda a=s["yg"], b=dgg: _mm_tn("mm_dw_glu", a, b, nb=2 * D // N_DEV))
        du, dbr, dbi, dcr, dci, dar, dai, dd = _ssm_bwd("ssm_bwd", s["proj"], u_off, ssm_p[l], s["ck_r"], s["ck_i"],
                                                        dyg, SW)
        cots = (dar[:, 0, :].reshape(G, SSM_STATE), dai[:, 0, :].reshape(G, SSM_STATE),
                _block_diag_t(dbr, SSM_GROUP, SSM_STATE).reshape(G * SSM_GROUP, SSM_STATE),
                _block_diag_t(dbi, SSM_GROUP, SSM_STATE).reshape(G * SSM_GROUP, SSM_STATE))
        g_are, g_aim, g_ldt, g_bre, g_bim = _ssm_prep_bwd("ssm_prep_bwd", ssm_raw[l], cots)
        small["ssm_a_re"][l], small["ssm_a_im"][l], small["ssm_log_dt"][l] = g_are, g_aim, g_ldt[:, 0]
        small["ssm_b_re"][l] = jnp.transpose(g_bre.reshape(G, SSM_GROUP, SSM_STATE), (0, 2, 1))
        small["ssm_b_im"][l] = jnp.transpose(g_bim.reshape(G, SSM_GROUP, SSM_STATE), (0, 2, 1))
        small["ssm_c_re"][l] = _block_diag_t(dcr, SSM_GROUP, SSM_STATE)
        small["ssm_c_im"][l] = _block_diag_t(dci, SSM_GROUP, SSM_STATE)
        small["ssm_d"][l] = dd[:, 0, :].reshape(SW)
        dhs = _mm_nt("mm_w_pw_t", dya, W("conv_w_pw", l), BF16)
        emit_small("conv_w_pw", l, lambda a=s["hs"], b=dya: _mm_tn("mm_dw_pw", a, b, nb=D // N_DEV))

        def post_bwd(hc, dh, b, lg, lb):
            _, vjp = jax.vjp(_convpost, hc, b, lg, lb)
            dhc, db, dlg, dlb = vjp(dh.astype(F32))
            return dhc, jnp.sum(db, axis=0, keepdims=True), jnp.sum(dlg, axis=0, keepdims=True), \
                jnp.sum(dlb, axis=0, keepdims=True)

        dhc, small["conv_dw_b"][l], small["conv_ln_g"][l], small["conv_ln_b"][l] = _rowwise(
            "conv_post_bwd", post_bwd, [(s["hc"], CW, 0, 0), (dhs, CW, 0, 0)],
            [row("conv_dw_b", l), row("conv_ln_g", l), row("conv_ln_b", l)], [(CW, F32)], [(1, CW)] * 3)
        if l == 0:
            early = {**small, "mix_norm_g": [jnp.zeros((1, D), F32)] + small["mix_norm_g"][1:]}
            flat_g = jnp.concatenate([(jnp.stack(early[n]) if n != "final_norm_g" else d_final_g).reshape(-1)
                                      for n in names])
            g_all = _allgather("ag_small_grads", pack([flat_g], 0.0)[None], False)
        da, db, d_conv_dw = _glu_conv_bwd("conv_bwd", s["proj"], dw_filter("conv_dw_w", l), dhc, CW)
        emit("conv_dw_w", l, jnp.transpose(d_conv_dw.reshape(HALO, N_DEV, CW // N_DEV), (1, 0, 2)))
        dproj = jnp.concatenate([da, db, du, dgla, dglb], axis=1)
        dh1 = _mm_nt("mm_w_in_t", dproj, W("w_in", l), BF16)
        dx, dx16, small["mix_norm_g"][l] = _rms_bwd("rms_mix_bwd", s["x_in"], row("mix_norm_g", l), dh1, dx)
        if l == 0:
            assert names[0] == "mix_norm_g" and D % pack_w == 0 and D // pack_w <= 8
            late = jnp.pad(small["mix_norm_g"][0].reshape(D // pack_w, pack_w), ((0, 8 - D // pack_w), (0, 0)))
            g_late = _allgather("ag_mix_grad", late[None], False)
            g_sum = _rowwise("small_grad_sum", total, [(g_all.reshape(N_DEV * rows_p, pack_w), pack_w, 0, k * rows_p)
                                                       for k in range(N_DEV)], [], [(pack_w, F32)], tile=LANE,
                             nrows=rows_p)[0]
        emit("w_in", l, _mm_tn("mm_dw_in", s["h1"], dproj, nb=dproj.shape[1] // N_DEV))
    for k, (n, thunk) in enumerate(deferred):
        emit(n, 0, thunk())
        if k == 0:
            head = _rowwise("mix_grad_sum", total,
                            [(g_sum, pack_w, 0, 0)] + [(g_late.reshape(N_DEV * 8, pack_w), pack_w, 0, 8 * d)
                                                       for d in range(N_DEV)], [], [(pack_w, F32)], tile=8, nrows=8)[0]

    while queue:
        tick()
    results = {n: [o[:, :inp[n].shape[1], :] for o in state[n]] if n in pads else state[n] for n in big}
    g_sum = jnp.concatenate([head, g_sum[8:]], axis=0)
    g_flat, grads, offs = g_sum.reshape(-1), {}, 0
    for n, sz in zip(names, sizes):
        grads[n] = g_flat[offs:offs + sz].reshape(inp[n].shape)
        offs += sz
    sparse = [n for n in names if inp[n].ndim == 4 and inp[n].shape[-1] < SSM_STATE]
    dense = [n for n in names if n not in sparse]
    as2d = lambda a: a[None, :] if a.ndim == 1 else a
    for group, block in ((dense, None), (sparse, (1, 16))):
        outs = _adamw_natural("small_adamw", *[[as2d(src[n]) for n in group] for src in (grads, inp, mom_m, mom_v)],
                              lead_block=block)
        for k, n in enumerate(group):
            results[n] = [grads[n]] + [outs[j * len(group) + k].reshape(inp[n].shape) for j in range(3)]

    _LAST_CALL.clear()
    loss = lax.psum(loss_part, ("x", "y", "c"))
    grad_x = dx[None]
    return (loss, grad_x, *[results[n][0] for n in _WEIGHTS], *[results[n][1] for n in _WEIGHTS],
            *[results[n][2] for n in _WEIGHTS], *[results[n][3] for n in _WEIGHTS])


def kernel(x, mem, mix_norm_g, w_in, conv_dw_w, conv_dw_b, conv_ln_g, conv_ln_b, conv_w_pw, ssm_a_re, ssm_a_im, ssm_log_dt, ssm_b_re, ssm_b_im, ssm_c_re, ssm_c_im, ssm_d, ssm_w_glu, w_out, xa_norm_g, mem_norm_g, xa_w_q, xa_w_kv, xa_w_o, ffn_norm_g, ffn_w_up, ffn_dw_w, ffn_w_down, final_norm_g, loss_target, m_mix_norm_g, m_w_in, m_conv_dw_w, m_conv_dw_b, m_conv_ln_g, m_conv_ln_b, m_conv_w_pw, m_ssm_a_re, m_ssm_a_im, m_ssm_log_dt, m_ssm_b_re, m_ssm_b_im, m_ssm_c_re, m_ssm_c_im, m_ssm_d, m_ssm_w_glu, m_w_out, m_xa_norm_g, m_mem_norm_g, m_xa_w_q, m_xa_w_kv, m_xa_w_o, m_ffn_norm_g, m_ffn_w_up, m_ffn_dw_w, m_ffn_w_down, m_final_norm_g, v_mix_norm_g, v_w_in, v_conv_dw_w, v_conv_dw_b, v_conv_ln_g, v_conv_ln_b, v_conv_w_pw, v_ssm_a_re, v_ssm_a_im, v_ssm_log_dt, v_ssm_b_re, v_ssm_b_im, v_ssm_c_re, v_ssm_c_im, v_ssm_d, v_ssm_w_glu, v_w_out, v_xa_norm_g, v_mem_norm_g, v_xa_w_q, v_xa_w_kv, v_xa_w_o, v_ffn_norm_g, v_ffn_w_up, v_ffn_dw_w, v_ffn_w_down, v_final_norm_g):
    args = (x, mem, mix_norm_g, w_in, conv_dw_w, conv_dw_b, conv_ln_g, conv_ln_b, conv_w_pw, ssm_a_re, ssm_a_im, ssm_log_dt, ssm_b_re, ssm_b_im, ssm_c_re, ssm_c_im, ssm_d, ssm_w_glu, w_out, xa_norm_g, mem_norm_g, xa_w_q, xa_w_kv, xa_w_o, ffn_norm_g, ffn_w_up, ffn_dw_w, ffn_w_down, final_norm_g)
    ms = (m_mix_norm_g, m_w_in, m_conv_dw_w, m_conv_dw_b, m_conv_ln_g, m_conv_ln_b, m_conv_w_pw, m_ssm_a_re, m_ssm_a_im, m_ssm_log_dt, m_ssm_b_re, m_ssm_b_im, m_ssm_c_re, m_ssm_c_im, m_ssm_d, m_ssm_w_glu, m_w_out, m_xa_norm_g, m_mem_norm_g, m_xa_w_q, m_xa_w_kv, m_xa_w_o, m_ffn_norm_g, m_ffn_w_up, m_ffn_dw_w, m_ffn_w_down, m_final_norm_g)
    vs = (v_mix_norm_g, v_w_in, v_conv_dw_w, v_conv_dw_b, v_conv_ln_g, v_conv_ln_b, v_conv_w_pw, v_ssm_a_re, v_ssm_a_im, v_ssm_log_dt, v_ssm_b_re, v_ssm_b_im, v_ssm_c_re, v_ssm_c_im, v_ssm_d, v_ssm_w_glu, v_w_out, v_xa_norm_g, v_mem_norm_g, v_xa_w_q, v_xa_w_kv, v_xa_w_o, v_ffn_norm_g, v_ffn_w_up, v_ffn_dw_w, v_ffn_w_down, v_final_norm_g)
    return _step(dict(zip(_FWD, args)), loss_target, dict(zip(_WEIGHTS, ms)), dict(zip(_WEIGHTS, vs)))
```
